```python
import math
import jax, jax.numpy as jnp
from jax import lax
import numpy as np

D_MODEL = 1024
BATCH = 16
SEQ = 2048
DEPTH = 2

N_META = 16
N_EVEN = (DEPTH + 1) // 2
N_ODD = DEPTH // 2

S5_WIDTH = D_MODEL // 2
S5_GROUP = 16
S5_GROUPS = S5_WIDTH // S5_GROUP
S5_STATE = 64
S5_DT_MIN = 0.001
S5_DT_MAX = 0.1

SB_HEAD_DIM = 64
SB_WIDTH = D_MODEL // 2
SB_HEADS = SB_WIDTH // SB_HEAD_DIM
SB_BLOCK = 128

HG_WIDTH = D_MODEL
HG_DK = 128
HG_HEADS = HG_WIDTH // HG_DK
HG_DV = HG_WIDTH // HG_HEADS
HG_CHUNK = 64

D_FF = 4 * D_MODEL
IN_AB = S5_WIDTH + 3 * SB_WIDTH
IN_C = 4 * HG_WIDTH

ALPHA = (2.0 * DEPTH) ** 0.25
BETA = (8.0 * DEPTH) ** -0.25
LN_EPS = 1e-5
RMS_EPS = 1e-6

kernel_name = 'hybrid_s5_stickbreak_hgrn2_deepnorm_meta'


def layer_norm(x, g, b):
    xf = x.astype(jnp.float32)
    mu = jnp.mean(xf, axis=-1, keepdims=True)
    var = jnp.mean(jnp.square(xf - mu), axis=-1, keepdims=True)
    return (xf - mu) * lax.rsqrt(var + LN_EPS) * g + b


def s5_mixer(u, lam_re, lam_im, log_dt, b_re, b_im, c_re, c_im, d, w_glu, b_glu):
    bn, seq_len, _ = u.shape
    f32 = jnp.float32
    uf = u.astype(f32).reshape(bn, seq_len, S5_GROUPS, S5_GROUP)
    lam = lax.complex(lam_re.astype(f32), lam_im.astype(f32))
    dt = jnp.exp(log_dt.astype(f32))[:, None]
    lam_bar = jnp.exp(lam * dt)
    b_mat = lax.complex(b_re.astype(f32), b_im.astype(f32))
    b_bar = ((lam_bar - 1.0) / lam)[:, :, None] * b_mat
    bu = jnp.einsum('blgh,gph->blgp', uf.astype(jnp.complex64), b_bar)
    a = jnp.broadcast_to(lam_bar, (1, seq_len) + lam_bar.shape)

    def combine(left, right):
        a_l, s_l = left
        a_r, s_r = right
        return a_r * a_l, a_r * s_l + s_r

    _, states = lax.associative_scan(combine, (a, bu), axis=1)
    c_mat = lax.complex(c_re.astype(f32), c_im.astype(f32))
    y = jnp.real(jnp.einsum('blgp,ghp->blgh', states, c_mat)) + d.astype(f32) * uf
    y = jax.nn.gelu(y.reshape(bn, seq_len, S5_WIDTH))
    return y * jax.nn.sigmoid(y @ w_glu + b_glu)


def _stick_breaking_block(qb, kb, vb, start):
    z = jnp.einsum('bhqd,bhkd->bhqk', qb, kb)
    nq = qb.shape[2]
    nk = kb.shape[2]
    visible = jnp.arange(nk)[None, :] < (start + jnp.arange(nq))[:, None]
    log_keep = jnp.where(visible, jax.nn.log_sigmoid(-z), 0.0)
    later = lax.cumsum(log_keep, axis=3, reverse=True) - log_keep
    w = jnp.where(visible, jnp.exp(jax.nn.log_sigmoid(z) + later), 0.0)
    return jnp.einsum('bhqk,bhkd->bhqd', w, vb)


def stick_breaking_attention(q, k, v):
    bn, seq_len = q.shape[0], q.shape[1]
    f32 = jnp.float32
    q = q.astype(f32).transpose(0, 2, 1, 3) * (SB_HEAD_DIM ** -0.5)
    k = k.astype(f32).transpose(0, 2, 1, 3)
    v = v.astype(f32).transpose(0, 2, 1, 3)
    blocks = [(0, N_META)] + [(N_META + i * SB_BLOCK, SB_BLOCK)
                              for i in range((seq_len - N_META) // SB_BLOCK)]
    outs = [_stick_breaking_block(q[:, :, s:s + n], k[:, :, :s + n], v[:, :, :s + n], s)
            for s, n in blocks]
    o = jnp.concatenate(outs, axis=2)
    return o.transpose(0, 2, 1, 3).reshape(bn, seq_len, SB_HEADS * SB_HEAD_DIM)


def hgrn_lower_bound(gamma, layer):
    p = jax.nn.softmax(gamma.astype(jnp.float32), axis=0)
    return (jnp.cumsum(p, axis=0) - p[0])[layer]


def _to_chunks(t, n_chunks):
    bn, _, nh, dd = t.shape
    return t.reshape(bn, n_chunks, HG_CHUNK, nh, dd).transpose(1, 0, 3, 2, 4)


def hgrn2_mixer(q, f_pre, i_in, g, lb, norm_g):
    bn, seq_len, _ = q.shape
    f32 = jnp.float32
    f = lb + (1.0 - lb) * jax.nn.sigmoid(f_pre.astype(f32))
    log_f = jnp.log(f)
    k = 1.0 - f
    pad = HG_CHUNK - N_META
    n_chunks = (seq_len + pad) // HG_CHUNK

    def heads(t, dd):
        t = t.astype(f32).reshape(bn, seq_len, HG_HEADS, dd)
        return _to_chunks(jnp.pad(t, ((0, 0), (pad, 0), (0, 0), (0, 0))), n_chunks)

    qh, kh, vh, lh = heads(q, HG_DK), heads(k, HG_DK), heads(i_in, HG_DV), heads(log_f, HG_DK)
    causal = jnp.tril(jnp.ones((HG_CHUNK, HG_CHUNK), dtype=bool))

    def step(state, xs):
        qc, kc, vc, lc = xs
        b = jnp.cumsum(lc, axis=2)
        b_last = b[:, :, -1:, :]
        q_dec = qc * jnp.exp(b)
        scores = jnp.einsum('bhtd,bhsd->bhts', q_dec, kc * jnp.exp(-b))
        scores = jnp.where(causal, scores, 0.0)
        o = (jnp.einsum('bhts,bhse->bhte', scores, vc)
             + jnp.einsum('bhtd,bhde->bhte', q_dec, state))
        state = (jnp.exp(b_last[:, :, 0, :])[..., None] * state
                 + jnp.einsum('bhsd,bhse->bhde', kc * jnp.exp(b_last - b), vc))
        return state, o

    s0 = jnp.zeros((bn, HG_HEADS, HG_DK, HG_DV), f32)
    _, o = lax.scan(step, s0, (qh, kh, vh, lh))
    o = o.transpose(1, 0, 3, 2, 4).reshape(bn, seq_len + pad, HG_HEADS, HG_DV)[:, pad:]
    o = o * lax.rsqrt(jnp.mean(jnp.square(o), axis=-1, keepdims=True) + RMS_EPS)
    o = o * norm_g.astype(f32).reshape(HG_HEADS, HG_DV)
    return o.reshape(bn, seq_len, HG_WIDTH) * jax.nn.silu(g.astype(f32))


def _fwd_setup_inputs(seed: int = 0) -> dict:
    key = jax.random.key(seed)
    ks = jax.random.split(key, 32)
    f32 = jnp.float32
    nrm = lambda k, shape, scale: jax.random.normal(k, shape, f32) * scale
    n_arange = jnp.pi * jnp.arange(S5_STATE, dtype=f32)
    return {
        'x': nrm(ks[0], (BATCH, SEQ, D_MODEL), 1.0),
        'meta': nrm(ks[1], (N_META, D_MODEL), 1.0),
        'w_in_ab': nrm(ks[2], (N_EVEN, D_MODEL, IN_AB), D_MODEL ** -0.5),
        's5_lam_re': -0.5 * jnp.exp(nrm(ks[3], (N_EVEN, S5_GROUPS, S5_STATE), 0.05)),
        's5_lam_im': n_arange + nrm(ks[4], (N_EVEN, S5_GROUPS, S5_STATE), 0.01),
        's5_log_dt': jax.random.uniform(ks[5], (N_EVEN, S5_GROUPS), f32,
                                        minval=math.log(S5_DT_MIN), maxval=math.log(S5_DT_MAX)),
        's5_b_re': nrm(ks[6], (N_EVEN, S5_GROUPS, S5_STATE, S5_GROUP), (2.0 * S5_GROUP) ** -0.5),
        's5_b_im': nrm(ks[7], (N_EVEN, S5_GROUPS, S5_STATE, S5_GROUP), (2.0 * S5_GROUP) ** -0.5),
        's5_c_re': nrm(ks[8], (N_EVEN, S5_GROUPS, S5_GROUP, S5_STATE), S5_STATE ** -0.5),
        's5_c_im': nrm(ks[9], (N_EVEN, S5_GROUPS, S5_GROUP, S5_STATE), S5_STATE ** -0.5),
        's5_d': nrm(ks[10], (N_EVEN, S5_GROUPS, S5_GROUP), 1.0),
        's5_w_glu': nrm(ks[11], (N_EVEN, S5_WIDTH, S5_WIDTH), S5_WIDTH ** -0.5),
        's5_b_glu': nrm(ks[12], (N_EVEN, S5_WIDTH), 0.01),
        'w_out_ab': nrm(ks[13], (N_EVEN, S5_WIDTH + SB_WIDTH, D_MODEL), BETA * (S5_WIDTH + SB_WIDTH) ** -0.5),
        'w_in_c': nrm(ks[14], (N_ODD, D_MODEL, IN_C), D_MODEL ** -0.5),
        'hgrn_gamma': nrm(ks[15], (DEPTH, HG_WIDTH), 0.1),
        'hgrn_norm_g': 1.0 + nrm(ks[16], (N_ODD, HG_WIDTH), 0.01),
        'w_out_c': nrm(ks[17], (N_ODD, HG_WIDTH, D_MODEL), BETA * HG_WIDTH ** -0.5),
        'ln_mix_g': 1.0 + nrm(ks[18], (DEPTH, D_MODEL), 0.01),
        'ln_mix_b': nrm(ks[19], (DEPTH, D_MODEL), 0.01),
        'mlp_w_up': nrm(ks[20], (DEPTH, D_MODEL, D_FF), D_MODEL ** -0.5),
        'mlp_b_up': nrm(ks[21], (DEPTH, D_FF), 0.01),
        'mlp_w_down': nrm(ks[22], (DEPTH, D_FF, D_MODEL), BETA * D_FF ** -0.5),
        'mlp_b_down': nrm(ks[23], (DEPTH, D_MODEL), 0.01),
        'ln_mlp_g': 1.0 + nrm(ks[24], (DEPTH, D_MODEL), 0.01),
        'ln_mlp_b': nrm(ks[25], (DEPTH, D_MODEL), 0.01),
    }


def _fwd_reference(x, meta, w_in_ab, s5_lam_re, s5_lam_im, s5_log_dt, s5_b_re, s5_b_im, s5_c_re, s5_c_im,
              s5_d, s5_w_glu, s5_b_glu, w_out_ab, w_in_c, hgrn_gamma, hgrn_norm_g, w_out_c,
              ln_mix_g, ln_mix_b, mlp_w_up, mlp_b_up, mlp_w_down, mlp_b_down, ln_mlp_g, ln_mlp_b):
    bn = x.shape[0]
    h = jnp.concatenate([jnp.broadcast_to(meta.astype(x.dtype)[None], (bn, N_META, D_MODEL)), x], axis=1)
    seq_len = h.shape[1]
    for layer in range(DEPTH):
        if layer % 2 == 0:
            e = layer // 2
            proj = h @ w_in_ab[e]
            u = proj[..., :S5_WIDTH]
            qkv = proj[..., S5_WIDTH:].reshape(bn, seq_len, 3, SB_HEADS, SB_HEAD_DIM)
            a_out = s5_mixer(u, s5_lam_re[e], s5_lam_im[e], s5_log_dt[e], s5_b_re[e], s5_b_im[e],
                             s5_c_re[e], s5_c_im[e], s5_d[e], s5_w_glu[e], s5_b_glu[e])
            b_out = stick_breaking_attention(qkv[:, :, 0], qkv[:, :, 1], qkv[:, :, 2])
            mix = jnp.concatenate([a_out, b_out], axis=-1) @ w_out_ab[e]
        else:
            o_idx = layer // 2
            proj = h @ w_in_c[o_idx]
            q, f_pre, i_in, g = jnp.split(proj, 4, axis=-1)
            lb = hgrn_lower_bound(hgrn_gamma, layer)
            mix = hgrn2_mixer(q, f_pre, i_in, g, lb, hgrn_norm_g[o_idx]) @ w_out_c[o_idx]
        h = layer_norm(ALPHA * h + mix, ln_mix_g[layer], ln_mix_b[layer])
        hid = jnp.square(jax.nn.relu(h @ mlp_w_up[layer] + mlp_b_up[layer]))
        h = layer_norm(ALPHA * h + hid @ mlp_w_down[layer] + mlp_b_down[layer],
                       ln_mlp_g[layer], ln_mlp_b[layer])
    return h[:, N_META:, :]


import jax as _jax
import jax.numpy as _jnp

TWIN_FORMAT = 'train_step'
FWD_PARAMS = ['x', 'meta', 'w_in_ab', 's5_lam_re', 's5_lam_im', 's5_log_dt', 's5_b_re', 's5_b_im', 's5_c_re', 's5_c_im', 's5_d', 's5_w_glu', 's5_b_glu', 'w_out_ab', 'w_in_c', 'hgrn_gamma', 'hgrn_norm_g', 'w_out_c', 'ln_mix_g', 'ln_mix_b', 'mlp_w_up', 'mlp_b_up', 'mlp_w_down', 'mlp_b_down', 'ln_mlp_g', 'ln_mlp_b']
TWIN_WEIGHTS = ['meta', 'w_in_ab', 's5_lam_re', 's5_lam_im', 's5_log_dt', 's5_b_re', 's5_b_im', 's5_c_re', 's5_c_im', 's5_d', 's5_w_glu', 's5_b_glu', 'w_out_ab', 'w_in_c', 'hgrn_gamma', 'hgrn_norm_g', 'w_out_c', 'ln_mix_g', 'ln_mix_b', 'mlp_w_up', 'mlp_b_up', 'mlp_w_down', 'mlp_b_down', 'ln_mlp_g', 'ln_mlp_b']
TWIN_DIFF_INPUT = 'x'
TWIN_INPUTS = ['x', 'meta', 'w_in_ab', 's5_lam_re', 's5_lam_im', 's5_log_dt', 's5_b_re', 's5_b_im', 's5_c_re', 's5_c_im', 's5_d', 's5_w_glu', 's5_b_glu', 'w_out_ab', 'w_in_c', 'hgrn_gamma', 'hgrn_norm_g', 'w_out_c', 'ln_mix_g', 'ln_mix_b', 'mlp_w_up', 'mlp_b_up', 'mlp_w_down', 'mlp_b_down', 'ln_mlp_g', 'ln_mlp_b', 'loss_target', 'm_meta', 'm_w_in_ab', 'm_s5_lam_re', 'm_s5_lam_im', 'm_s5_log_dt', 'm_s5_b_re', 'm_s5_b_im', 'm_s5_c_re', 'm_s5_c_im', 'm_s5_d', 'm_s5_w_glu', 'm_s5_b_glu', 'm_w_out_ab', 'm_w_in_c', 'm_hgrn_gamma', 'm_hgrn_norm_g', 'm_w_out_c', 'm_ln_mix_g', 'm_ln_mix_b', 'm_mlp_w_up', 'm_mlp_b_up', 'm_mlp_w_down', 'm_mlp_b_down', 'm_ln_mlp_g', 'm_ln_mlp_b', 'v_meta', 'v_w_in_ab', 'v_s5_lam_re', 'v_s5_lam_im', 'v_s5_log_dt', 'v_s5_b_re', 'v_s5_b_im', 'v_s5_c_re', 'v_s5_c_im', 'v_s5_d', 'v_s5_w_glu', 'v_s5_b_glu', 'v_w_out_ab', 'v_w_in_c', 'v_hgrn_gamma', 'v_hgrn_norm_g', 'v_w_out_c', 'v_ln_mix_g', 'v_ln_mix_b', 'v_mlp_w_up', 'v_mlp_b_up', 'v_mlp_w_down', 'v_mlp_b_down', 'v_ln_mlp_g', 'v_ln_mlp_b']
TWIN_OUTPUTS = ['loss', 'grad_x', 'grad_meta', 'grad_w_in_ab', 'grad_s5_lam_re', 'grad_s5_lam_im', 'grad_s5_log_dt', 'grad_s5_b_re', 'grad_s5_b_im', 'grad_s5_c_re', 'grad_s5_c_im', 'grad_s5_d', 'grad_s5_w_glu', 'grad_s5_b_glu', 'grad_w_out_ab', 'grad_w_in_c', 'grad_hgrn_gamma', 'grad_hgrn_norm_g', 'grad_w_out_c', 'grad_ln_mix_g', 'grad_ln_mix_b', 'grad_mlp_w_up', 'grad_mlp_b_up', 'grad_mlp_w_down', 'grad_mlp_b_down', 'grad_ln_mlp_g', 'grad_ln_mlp_b', 'delta_meta', 'delta_w_in_ab', 'delta_s5_lam_re', 'delta_s5_lam_im', 'delta_s5_log_dt', 'delta_s5_b_re', 'delta_s5_b_im', 'delta_s5_c_re', 'delta_s5_c_im', 'delta_s5_d', 'delta_s5_w_glu', 'delta_s5_b_glu', 'delta_w_out_ab', 'delta_w_in_c', 'delta_hgrn_gamma', 'delta_hgrn_norm_g', 'delta_w_out_c', 'delta_ln_mix_g', 'delta_ln_mix_b', 'delta_mlp_w_up', 'delta_mlp_b_up', 'delta_mlp_w_down', 'delta_mlp_b_down', 'delta_ln_mlp_g', 'delta_ln_mlp_b', 'new_m_meta', 'new_m_w_in_ab', 'new_m_s5_lam_re', 'new_m_s5_lam_im', 'new_m_s5_log_dt', 'new_m_s5_b_re', 'new_m_s5_b_im', 'new_m_s5_c_re', 'new_m_s5_c_im', 'new_m_s5_d', 'new_m_s5_w_glu', 'new_m_s5_b_glu', 'new_m_w_out_ab', 'new_m_w_in_c', 'new_m_hgrn_gamma', 'new_m_hgrn_norm_g', 'new_m_w_out_c', 'new_m_ln_mix_g', 'new_m_ln_mix_b', 'new_m_mlp_w_up', 'new_m_mlp_b_up', 'new_m_mlp_w_down', 'new_m_mlp_b_down', 'new_m_ln_mlp_g', 'new_m_ln_mlp_b', 'new_v_meta', 'new_v_w_in_ab', 'new_v_s5_lam_re', 'new_v_s5_lam_im', 'new_v_s5_log_dt', 'new_v_s5_b_re', 'new_v_s5_b_im', 'new_v_s5_c_re', 'new_v_s5_c_im', 'new_v_s5_d', 'new_v_s5_w_glu', 'new_v_s5_b_glu', 'new_v_w_out_ab', 'new_v_w_in_c', 'new_v_hgrn_gamma', 'new_v_hgrn_norm_g', 'new_v_w_out_c', 'new_v_ln_mix_g', 'new_v_ln_mix_b', 'new_v_mlp_w_up', 'new_v_mlp_b_up', 'new_v_mlp_w_down', 'new_v_mlp_b_down', 'new_v_ln_mlp_g', 'new_v_ln_mlp_b']
TWIN_LEAF_KINDS = {'loss': 'loss', 'grad_x': 'grad_x', 'grad_meta': 'grad_w', 'grad_w_in_ab': 'grad_w', 'grad_s5_lam_re': 'grad_w', 'grad_s5_lam_im': 'grad_w', 'grad_s5_log_dt': 'grad_w', 'grad_s5_b_re': 'grad_w', 'grad_s5_b_im': 'grad_w', 'grad_s5_c_re': 'grad_w', 'grad_s5_c_im': 'grad_w', 'grad_s5_d': 'grad_w', 'grad_s5_w_glu': 'grad_w', 'grad_s5_b_glu': 'grad_w', 'grad_w_out_ab': 'grad_w', 'grad_w_in_c': 'grad_w', 'grad_hgrn_gamma': 'grad_w', 'grad_hgrn_norm_g': 'grad_w', 'grad_w_out_c': 'grad_w', 'grad_ln_mix_g': 'grad_w', 'grad_ln_mix_b': 'grad_w', 'grad_mlp_w_up': 'grad_w', 'grad_mlp_b_up': 'grad_w', 'grad_mlp_w_down': 'grad_w', 'grad_mlp_b_down': 'grad_w', 'grad_ln_mlp_g': 'grad_w', 'grad_ln_mlp_b': 'grad_w', 'delta_meta': 'delta_w', 'delta_w_in_ab': 'delta_w', 'delta_s5_lam_re': 'delta_w', 'delta_s5_lam_im': 'delta_w', 'delta_s5_log_dt': 'delta_w', 'delta_s5_b_re': 'delta_w', 'delta_s5_b_im': 'delta_w', 'delta_s5_c_re': 'delta_w', 'delta_s5_c_im': 'delta_w', 'delta_s5_d': 'delta_w', 'delta_s5_w_glu': 'delta_w', 'delta_s5_b_glu': 'delta_w', 'delta_w_out_ab': 'delta_w', 'delta_w_in_c': 'delta_w', 'delta_hgrn_gamma': 'delta_w', 'delta_hgrn_norm_g': 'delta_w', 'delta_w_out_c': 'delta_w', 'delta_ln_mix_g': 'delta_w', 'delta_ln_mix_b': 'delta_w', 'delta_mlp_w_up': 'delta_w', 'delta_mlp_b_up': 'delta_w', 'delta_mlp_w_down': 'delta_w', 'delta_mlp_b_down': 'delta_w', 'delta_ln_mlp_g': 'delta_w', 'delta_ln_mlp_b': 'delta_w', 'new_m_meta': 'new_m', 'new_m_w_in_ab': 'new_m', 'new_m_s5_lam_re': 'new_m', 'new_m_s5_lam_im': 'new_m', 'new_m_s5_log_dt': 'new_m', 'new_m_s5_b_re': 'new_m', 'new_m_s5_b_im': 'new_m', 'new_m_s5_c_re': 'new_m', 'new_m_s5_c_im': 'new_m', 'new_m_s5_d': 'new_m', 'new_m_s5_w_glu': 'new_m', 'new_m_s5_b_glu': 'new_m', 'new_m_w_out_ab': 'new_m', 'new_m_w_in_c': 'new_m', 'new_m_hgrn_gamma': 'new_m', 'new_m_hgrn_norm_g': 'new_m', 'new_m_w_out_c': 'new_m', 'new_m_ln_mix_g': 'new_m', 'new_m_ln_mix_b': 'new_m', 'new_m_mlp_w_up': 'new_m', 'new_m_mlp_b_up': 'new_m', 'new_m_mlp_w_down': 'new_m', 'new_m_mlp_b_down': 'new_m', 'new_m_ln_mlp_g': 'new_m', 'new_m_ln_mlp_b': 'new_m', 'new_v_meta': 'new_v', 'new_v_w_in_ab': 'new_v', 'new_v_s5_lam_re': 'new_v', 'new_v_s5_lam_im': 'new_v', 'new_v_s5_log_dt': 'new_v', 'new_v_s5_b_re': 'new_v', 'new_v_s5_b_im': 'new_v', 'new_v_s5_c_re': 'new_v', 'new_v_s5_c_im': 'new_v', 'new_v_s5_d': 'new_v', 'new_v_s5_w_glu': 'new_v', 'new_v_s5_b_glu': 'new_v', 'new_v_w_out_ab': 'new_v', 'new_v_w_in_c': 'new_v', 'new_v_hgrn_gamma': 'new_v', 'new_v_hgrn_norm_g': 'new_v', 'new_v_w_out_c': 'new_v', 'new_v_ln_mix_g': 'new_v', 'new_v_ln_mix_b': 'new_v', 'new_v_mlp_w_up': 'new_v', 'new_v_mlp_b_up': 'new_v', 'new_v_mlp_w_down': 'new_v', 'new_v_mlp_b_down': 'new_v', 'new_v_ln_mlp_g': 'new_v', 'new_v_ln_mlp_b': 'new_v'}


def _forward(args):
    return _fwd_reference(*[args[k] for k in FWD_PARAMS])


def _output_shape():
    out = _jax.eval_shape(lambda: _forward(_fwd_setup_inputs(0)))
    return out.shape, out.dtype

N_MICROBATCH = 1
ADAM_LR = 0.001
ADAM_B1 = 0.9
ADAM_B2 = 0.999
ADAM_EPS = 1e-08
ADAM_WD = 0.01
ADAM_STEP = 10
PER_EXAMPLE_BATCH_AXIS = {'x': 0, 'loss_target': 0}
SHARED_INPUTS = []
_WEIGHT_DTYPES = {'meta': _jnp.float32, 'w_in_ab': _jnp.float32, 's5_lam_re': _jnp.float32, 's5_lam_im': _jnp.float32, 's5_log_dt': _jnp.float32, 's5_b_re': _jnp.float32, 's5_b_im': _jnp.float32, 's5_c_re': _jnp.float32, 's5_c_im': _jnp.float32, 's5_d': _jnp.float32, 's5_w_glu': _jnp.float32, 's5_b_glu': _jnp.float32, 'w_out_ab': _jnp.float32, 'w_in_c': _jnp.float32, 'hgrn_gamma': _jnp.float32, 'hgrn_norm_g': _jnp.float32, 'w_out_c': _jnp.float32, 'ln_mix_g': _jnp.float32, 'ln_mix_b': _jnp.float32, 'mlp_w_up': _jnp.float32, 'mlp_b_up': _jnp.float32, 'mlp_w_down': _jnp.float32, 'mlp_b_down': _jnp.float32, 'ln_mlp_g': _jnp.float32, 'ln_mlp_b': _jnp.float32}
MOMENT_SCALE = {'meta': 1.094488e-03, 'w_in_ab': 3.020733e-02, 's5_lam_re': 1.956207e-03, 's5_lam_im': 1.869794e-03, 's5_log_dt': 9.505003e-01, 's5_b_re': 1.291552e-03, 's5_b_im': 1.231164e-03, 's5_c_re': 1.803319e-03, 's5_c_im': 1.817705e-03, 's5_d': 2.751324e-02, 's5_w_glu': 7.972353e-03, 's5_b_glu': 1.227672e-02, 'w_out_ab': 7.476080e-02, 'w_in_c': 4.158466e-02, 'hgrn_gamma': 2.389228e-02, 'hgrn_norm_g': 3.863964e-02, 'w_out_c': 7.504502e-02, 'ln_mix_g': 3.522404e-01, 'ln_mix_b': 2.666476e-01, 'mlp_w_up': 4.281029e-02, 'mlp_b_up': 4.856362e-02, 'mlp_w_down': 1.615481e-01, 'mlp_b_down': 1.818198e-01, 'ln_mlp_g': 2.267936e+01, 'ln_mlp_b': 5.042114e+00}


def _to_microbatches(a, axis):
    t = _jnp.moveaxis(a, axis, 0)
    t = t.reshape((N_MICROBATCH, t.shape[0] // N_MICROBATCH) + t.shape[1:])
    return _jnp.moveaxis(t, 1, axis + 1)


def setup_inputs(seed: int = 0) -> dict:
    inp = _fwd_setup_inputs(seed)
    key = _jax.random.fold_in(_jax.random.key(seed), 7919)
    shape, _ = _output_shape()
    out = dict(inp)
    out["loss_target"] = _jax.random.normal(_jax.random.fold_in(key, 0), shape, _jnp.float32)
    for i, name in enumerate(TWIN_WEIGHTS):
        w = inp[name].astype(_jnp.float32)
        if MOMENT_SCALE is None:
            s = _jnp.sqrt(_jnp.mean(_jnp.square(w)) + 1e-30)
        else:
            s = MOMENT_SCALE[name]
        km, kv = _jax.random.split(_jax.random.fold_in(key, i + 1))
        out[name] = w
        out["m_" + name] = s * _jax.random.normal(km, w.shape, _jnp.float32)
        out["v_" + name] = (s * s) * _jax.random.uniform(kv, w.shape, _jnp.float32, 0.5, 1.5)
    if N_MICROBATCH > 1:
        for name, axis in PER_EXAMPLE_BATCH_AXIS.items():
            out[name] = _to_microbatches(out[name], axis)
    return {'x': out['x'], 'meta': out['meta'], 'w_in_ab': out['w_in_ab'], 's5_lam_re': out['s5_lam_re'], 's5_lam_im': out['s5_lam_im'], 's5_log_dt': out['s5_log_dt'], 's5_b_re': out['s5_b_re'], 's5_b_im': out['s5_b_im'], 's5_c_re': out['s5_c_re'], 's5_c_im': out['s5_c_im'], 's5_d': out['s5_d'], 's5_w_glu': out['s5_w_glu'], 's5_b_glu': out['s5_b_glu'], 'w_out_ab': out['w_out_ab'], 'w_in_c': out['w_in_c'], 'hgrn_gamma': out['hgrn_gamma'], 'hgrn_norm_g': out['hgrn_norm_g'], 'w_out_c': out['w_out_c'], 'ln_mix_g': out['ln_mix_g'], 'ln_mix_b': out['ln_mix_b'], 'mlp_w_up': out['mlp_w_up'], 'mlp_b_up': out['mlp_b_up'], 'mlp_w_down': out['mlp_w_down'], 'mlp_b_down': out['mlp_b_down'], 'ln_mlp_g': out['ln_mlp_g'], 'ln_mlp_b': out['ln_mlp_b'], 'loss_target': out['loss_target'], 'm_meta': out['m_meta'], 'm_w_in_ab': out['m_w_in_ab'], 'm_s5_lam_re': out['m_s5_lam_re'], 'm_s5_lam_im': out['m_s5_lam_im'], 'm_s5_log_dt': out['m_s5_log_dt'], 'm_s5_b_re': out['m_s5_b_re'], 'm_s5_b_im': out['m_s5_b_im'], 'm_s5_c_re': out['m_s5_c_re'], 'm_s5_c_im': out['m_s5_c_im'], 'm_s5_d': out['m_s5_d'], 'm_s5_w_glu': out['m_s5_w_glu'], 'm_s5_b_glu': out['m_s5_b_glu'], 'm_w_out_ab': out['m_w_out_ab'], 'm_w_in_c': out['m_w_in_c'], 'm_hgrn_gamma': out['m_hgrn_gamma'], 'm_hgrn_norm_g': out['m_hgrn_norm_g'], 'm_w_out_c': out['m_w_out_c'], 'm_ln_mix_g': out['m_ln_mix_g'], 'm_ln_mix_b': out['m_ln_mix_b'], 'm_mlp_w_up': out['m_mlp_w_up'], 'm_mlp_b_up': out['m_mlp_b_up'], 'm_mlp_w_down': out['m_mlp_w_down'], 'm_mlp_b_down': out['m_mlp_b_down'], 'm_ln_mlp_g': out['m_ln_mlp_g'], 'm_ln_mlp_b': out['m_ln_mlp_b'], 'v_meta': out['v_meta'], 'v_w_in_ab': out['v_w_in_ab'], 'v_s5_lam_re': out['v_s5_lam_re'], 'v_s5_lam_im': out['v_s5_lam_im'], 'v_s5_log_dt': out['v_s5_log_dt'], 'v_s5_b_re': out['v_s5_b_re'], 'v_s5_b_im': out['v_s5_b_im'], 'v_s5_c_re': out['v_s5_c_re'], 'v_s5_c_im': out['v_s5_c_im'], 'v_s5_d': out['v_s5_d'], 'v_s5_w_glu': out['v_s5_w_glu'], 'v_s5_b_glu': out['v_s5_b_glu'], 'v_w_out_ab': out['v_w_out_ab'], 'v_w_in_c': out['v_w_in_c'], 'v_hgrn_gamma': out['v_hgrn_gamma'], 'v_hgrn_norm_g': out['v_hgrn_norm_g'], 'v_w_out_c': out['v_w_out_c'], 'v_ln_mix_g': out['v_ln_mix_g'], 'v_ln_mix_b': out['v_ln_mix_b'], 'v_mlp_w_up': out['v_mlp_w_up'], 'v_mlp_b_up': out['v_mlp_b_up'], 'v_mlp_w_down': out['v_mlp_w_down'], 'v_mlp_b_down': out['v_mlp_b_down'], 'v_ln_mlp_g': out['v_ln_mlp_g'], 'v_ln_mlp_b': out['v_ln_mlp_b']}


def _loss(weights, diff, rest, loss_target):
    with _jax.named_scope("forward"):
        args = {**rest, TWIN_DIFF_INPUT: diff, **{k: w.astype(_WEIGHT_DTYPES[k]) for k, w in weights.items()}}
        y = _forward(args)
    with _jax.named_scope("loss_head"):
        err = _jnp.square(y.astype(_jnp.float32) - loss_target)
        return 0.5 * _jnp.sum(_jnp.mean(err, axis=-1)) if err.ndim else 0.5 * err


def _adamw(w, g, m, v):
    m = ADAM_B1 * m + (1.0 - ADAM_B1) * g
    v = ADAM_B2 * v + (1.0 - ADAM_B2) * _jnp.square(g)
    m_hat = m / (1.0 - ADAM_B1 ** ADAM_STEP)
    v_hat = v / (1.0 - ADAM_B2 ** ADAM_STEP)
    delta = -ADAM_LR * (m_hat / (_jnp.sqrt(v_hat) + ADAM_EPS) + ADAM_WD * w)
    return delta, m, v


def reference(x, meta, w_in_ab, s5_lam_re, s5_lam_im, s5_log_dt, s5_b_re, s5_b_im, s5_c_re, s5_c_im, s5_d, s5_w_glu, s5_b_glu, w_out_ab, w_in_c, hgrn_gamma, hgrn_norm_g, w_out_c, ln_mix_g, ln_mix_b, mlp_w_up, mlp_b_up, mlp_w_down, mlp_b_down, ln_mlp_g, ln_mlp_b, loss_target, m_meta, m_w_in_ab, m_s5_lam_re, m_s5_lam_im, m_s5_log_dt, m_s5_b_re, m_s5_b_im, m_s5_c_re, m_s5_c_im, m_s5_d, m_s5_w_glu, m_s5_b_glu, m_w_out_ab, m_w_in_c, m_hgrn_gamma, m_hgrn_norm_g, m_w_out_c, m_ln_mix_g, m_ln_mix_b, m_mlp_w_up, m_mlp_b_up, m_mlp_w_down, m_mlp_b_down, m_ln_mlp_g, m_ln_mlp_b, v_meta, v_w_in_ab, v_s5_lam_re, v_s5_lam_im, v_s5_log_dt, v_s5_b_re, v_s5_b_im, v_s5_c_re, v_s5_c_im, v_s5_d, v_s5_w_glu, v_s5_b_glu, v_w_out_ab, v_w_in_c, v_hgrn_gamma, v_hgrn_norm_g, v_w_out_c, v_ln_mix_g, v_ln_mix_b, v_mlp_w_up, v_mlp_b_up, v_mlp_w_down, v_mlp_b_down, v_ln_mlp_g, v_ln_mlp_b):
    given = dict(x=x, meta=meta, w_in_ab=w_in_ab, s5_lam_re=s5_lam_re, s5_lam_im=s5_lam_im, s5_log_dt=s5_log_dt, s5_b_re=s5_b_re, s5_b_im=s5_b_im, s5_c_re=s5_c_re, s5_c_im=s5_c_im, s5_d=s5_d, s5_w_glu=s5_w_glu, s5_b_glu=s5_b_glu, w_out_ab=w_out_ab, w_in_c=w_in_c, hgrn_gamma=hgrn_gamma, hgrn_norm_g=hgrn_norm_g, w_out_c=w_out_c, ln_mix_g=ln_mix_g, ln_mix_b=ln_mix_b, mlp_w_up=mlp_w_up, mlp_b_up=mlp_b_up, mlp_w_down=mlp_w_down, mlp_b_down=mlp_b_down, ln_mlp_g=ln_mlp_g, ln_mlp_b=ln_mlp_b, loss_target=loss_target, m_meta=m_meta, m_w_in_ab=m_w_in_ab, m_s5_lam_re=m_s5_lam_re, m_s5_lam_im=m_s5_lam_im, m_s5_log_dt=m_s5_log_dt, m_s5_b_re=m_s5_b_re, m_s5_b_im=m_s5_b_im, m_s5_c_re=m_s5_c_re, m_s5_c_im=m_s5_c_im, m_s5_d=m_s5_d, m_s5_w_glu=m_s5_w_glu, m_s5_b_glu=m_s5_b_glu, m_w_out_ab=m_w_out_ab, m_w_in_c=m_w_in_c, m_hgrn_gamma=m_hgrn_gamma, m_hgrn_norm_g=m_hgrn_norm_g, m_w_out_c=m_w_out_c, m_ln_mix_g=m_ln_mix_g, m_ln_mix_b=m_ln_mix_b, m_mlp_w_up=m_mlp_w_up, m_mlp_b_up=m_mlp_b_up, m_mlp_w_down=m_mlp_w_down, m_mlp_b_down=m_mlp_b_down, m_ln_mlp_g=m_ln_mlp_g, m_ln_mlp_b=m_ln_mlp_b, v_meta=v_meta, v_w_in_ab=v_w_in_ab, v_s5_lam_re=v_s5_lam_re, v_s5_lam_im=v_s5_lam_im, v_s5_log_dt=v_s5_log_dt, v_s5_b_re=v_s5_b_re, v_s5_b_im=v_s5_b_im, v_s5_c_re=v_s5_c_re, v_s5_c_im=v_s5_c_im, v_s5_d=v_s5_d, v_s5_w_glu=v_s5_w_glu, v_s5_b_glu=v_s5_b_glu, v_w_out_ab=v_w_out_ab, v_w_in_c=v_w_in_c, v_hgrn_gamma=v_hgrn_gamma, v_hgrn_norm_g=v_hgrn_norm_g, v_w_out_c=v_w_out_c, v_ln_mix_g=v_ln_mix_g, v_ln_mix_b=v_ln_mix_b, v_mlp_w_up=v_mlp_w_up, v_mlp_b_up=v_mlp_b_up, v_mlp_w_down=v_mlp_w_down, v_mlp_b_down=v_mlp_b_down, v_ln_mlp_g=v_ln_mlp_g, v_ln_mlp_b=v_ln_mlp_b)
    weights = {n: given[n] for n in TWIN_WEIGHTS}
    shared = {n: given[n] for n in SHARED_INPUTS}
    per_example = {n: given[n] for n in ['x']}
    grad_fn = _jax.value_and_grad(_loss, argnums=(0, 1))

    def one_microbatch(ex, loss_target):
        ex = dict(ex)
        diff = ex.pop(TWIN_DIFF_INPUT)
        return grad_fn(weights, diff, {**shared, **ex}, loss_target)

    if N_MICROBATCH == 1:
        loss, (grad_w, grad_x) = one_microbatch(per_example, given["loss_target"])
    else:
        def body(carry, xs):
            loss_sum, grad_sum = carry
            l_k, (gw_k, gx_k) = one_microbatch(xs[0], xs[1])
            with _jax.named_scope("update"):
                return (loss_sum + l_k, _jax.tree.map(_jnp.add, grad_sum, gw_k)), gx_k

        init = (_jnp.zeros((), _jnp.float32), _jax.tree.map(_jnp.zeros_like, weights))
        (loss, grad_w), grad_x = _jax.lax.scan(body, init, (per_example, given["loss_target"]))
    with _jax.named_scope("update"):
        delta_w, new_m, new_v = {}, {}, {}
        for n in TWIN_WEIGHTS:
            delta_w[n], new_m[n], new_v[n] = _adamw(weights[n], grad_w[n], given["m_" + n], given["v_" + n])
    return (loss, grad_x, *[grad_w[n] for n in TWIN_WEIGHTS], *[delta_w[n] for n in TWIN_WEIGHTS],
            *[new_m[n] for n in TWIN_WEIGHTS], *[new_v[n] for n in TWIN_WEIGHTS])
```

```python
import functools
import math

import jax
import jax.numpy as jnp
from jax import lax
from jax.experimental import pallas as pl
from jax.experimental.pallas import tpu as pltpu

F32 = jnp.float32
BF16 = jnp.bfloat16

N_DEV = 8
DEPTH = 2
ALPHA = (2.0 * DEPTH) ** 0.25
LN_EPS = 1e-5
RMS_EPS = 1e-6
SB_HEAD_DIM = 64
HG_DK = 128
HG_CHUNK = 64
LANES = 128
SUBLANES = 8
VMEM_LIMIT_BYTES = 56 * 1024 * 1024
ROW_TILE = 544
SCAN_LANES = 256
PACK_COLS = 1024

ADAM_LR = 0.001
ADAM_B1 = 0.9
ADAM_B2 = 0.999
ADAM_EPS = 1e-08
ADAM_WD = 0.01
ADAM_STEP = 10

NN = (((1,), (0,)), ((), ()))
NT = (((1,), (1,)), ((), ()))
TN = (((0,), (0,)), ((), ()))


def _tile(n, pref, align=SUBLANES):
    t = min(n, pref)
    t -= t % align
    while t >= align:
        if n % t == 0:
            return t
        t -= align
    return n


def _params(sem):
    return pltpu.CompilerParams(dimension_semantics=sem, vmem_limit_bytes=VMEM_LIMIT_BYTES)


def _dot_raw(a, b, dims):
    return lax.dot_general(a.astype(BF16), b.astype(BF16), dims, preferred_element_type=F32)


def _make_dot(dims, da_rule, db_rule):
    @jax.custom_vjp
    def f(a, b):
        return _dot_raw(a, b, dims)

    def fwd(a, b):
        return _dot_raw(a, b, dims), (a, b)

    def bwd(res, g):
        a, b = res
        return da_rule(g, a, b), db_rule(g, a, b)

    f.defvjp(fwd, bwd)
    return f


_DOTS = {
    NN: _make_dot(NN, lambda g, a, b: _dot_raw(g, b, NT), lambda g, a, b: _dot_raw(a, g, TN)),
    NT: _make_dot(NT, lambda g, a, b: _dot_raw(g, b, NN), lambda g, a, b: _dot_raw(g, a, TN)),
    TN: _make_dot(TN, lambda g, a, b: _dot_raw(b, g, NT), lambda g, a, b: _dot_raw(a, g, NN)),
}


def _dot(a, b, dims):
    return _DOTS[dims](a, b)


def _dot_split(a, b, dims):
    hi = a.astype(BF16)
    lo = (a - hi.astype(F32)).astype(BF16)
    return (lax.dot_general(hi, b, dims, preferred_element_type=F32)
            + lax.dot_general(lo, b, dims, preferred_element_type=F32))


def _piece_specs(pieces, block_rows, block_cols, row_of, col_of, cb0):
    per = pieces[0].shape[1] // block_cols if len(pieces) > 1 else None
    specs = []
    for p in range(len(pieces)):
        if per is None:
            specs.append(pl.BlockSpec((block_rows, block_cols), lambda *g: (row_of(*g), cb0 + col_of(*g))))
        else:
            specs.append(pl.BlockSpec(
                (block_rows, block_cols),
                lambda *g, p=p: (row_of(*g), jnp.clip(col_of(*g) - p * per, 0, per - 1))))
    return specs, per


def _mm_call(name, grid, dims, a_pieces, a_specs, a_sel, b_pieces, b_specs, b_sel, extras, extra_specs,
             out_shape, out_specs, acc_shape, a_fn, store, colsum_width=0):
    na, nb, ne, no = len(a_pieces), len(b_pieces), len(extras), len(out_shape)
    nk = grid[2]

    def body(*refs):
        a_refs, b_refs = refs[:na], refs[na:na + nb]
        extra = refs[na + nb:na + nb + ne]
        outs = refs[na + nb + ne:na + nb + ne + no]
        acc = refs[na + nb + ne + no]
        ids = (pl.program_id(0), pl.program_id(1), pl.program_id(2))
        k = ids[2]

        @pl.when(k == 0)
        def _():
            acc[...] = jnp.zeros_like(acc)

        def run(a_ref, b_ref):
            a = a_ref[...]
            if a_fn is not None:
                a = a_fn(a)
            b = b_ref[...]
            acc[...] += _dot_raw(a, b, dims)
            if colsum_width:
                cs = refs[-1]
                first = ids[1] == 0

                @pl.when(first & (k == 0))
                def _():
                    cs[...] = jnp.zeros_like(cs)

                @pl.when(first)
                def _():
                    cs[...] += jnp.sum(b.astype(F32), axis=0, keepdims=True)

        if na == 1 and nb == 1:
            run(a_refs[0], b_refs[0])
        elif nb == 1:
            per, fn = a_sel
            which = fn(*ids) // per
            for p in range(na):
                pl.when(which == p)(functools.partial(run, a_refs[p], b_refs[0]))
        else:
            assert na == 1
            per, fn = b_sel
            which = fn(*ids) // per
            for p in range(nb):
                pl.when(which == p)(functools.partial(run, a_refs[0], b_refs[p]))

        @pl.when(k == nk - 1)
        def _():
            store(outs, acc[...], *[e[...] for e in extra])
            if colsum_width:
                @pl.when(ids[1] == 0)
                def _():
                    outs[-1][...] = refs[-1][...]

    scratch = [pltpu.VMEM(acc_shape, F32)]
    if colsum_width:
        scratch.append(pltpu.VMEM((1, colsum_width), F32))
    sem = ("parallel", "arbitrary", "arbitrary") if colsum_width else ("parallel", "parallel", "arbitrary")
    return pl.pallas_call(
        body, name=name, grid=grid, in_specs=[*a_specs, *b_specs, *extra_specs], out_specs=out_specs,
        out_shape=out_shape, scratch_shapes=scratch, compiler_params=_params(sem),
    )(*a_pieces, *b_pieces, *extras)


def _store_plain(outs, acc):
    outs[0][...] = acc.astype(outs[0].dtype)


def _row_spec(tm, tn):
    return pl.BlockSpec((tm, tn), lambda i, j, k: (i, j))


def _vec_spec(tn):
    return pl.BlockSpec((1, tn), lambda i, j, k: (0, j))


def _mm_act(name, a, w, wkind, *, n_out_cols, k_total, tn, tk, a_cb0=0, a_fn=None, extras=(), extra_specs=(),
            store=_store_plain, out_shape=None, out_specs=None):
    a_pieces = list(a) if isinstance(a, (list, tuple)) else [a]
    rows = a_pieces[0].shape[0]
    tm = _tile(rows, ROW_TILE)
    grid = (rows // tm, n_out_cols // tn, k_total // tk)
    a_specs, per = _piece_specs(a_pieces, tm, tk, lambda i, j, k: i, lambda i, j, k: k, a_cb0)
    if wkind == "nat":
        b_spec, dims = pl.BlockSpec((tk, tn), lambda i, j, k: (k, j)), NN
    elif wkind == "stk":
        assert tn == w.shape[2]
        b_spec, dims = pl.BlockSpec((None, tk, tn), lambda i, j, k: (j, k, 0)), NN
    elif wkind == "natT":
        b_spec, dims = pl.BlockSpec((tn, tk), lambda i, j, k: (j, k)), NT
    else:
        assert wkind == "stkT" and tk == w.shape[2]
        b_spec, dims = pl.BlockSpec((None, tn, tk), lambda i, j, k: (k, j, 0)), NT
    if out_shape is None:
        out_shape = [jax.ShapeDtypeStruct((rows, n_out_cols), F32)]
        out_specs = [_row_spec(tm, tn)]
    return _mm_call(name, grid, dims, a_pieces, a_specs, (per, lambda i, j, k: k), [w], [b_spec], None,
                    list(extras), list(extra_specs), out_shape, out_specs, (tm, tn), a_fn, store)


def _mm_wgrad(name, a, g, *, kw, n, tmw, tn, a_cb0=0, a_fn=None, shard_cols=0, out_dtype=F32, colsum=False):
    a_pieces = list(a) if isinstance(a, (list, tuple)) else [a]
    g_pieces = list(g) if isinstance(g, (list, tuple)) else [g]
    rows = a_pieces[0].shape[0]
    tr = _tile(rows, ROW_TILE)
    grid = (n // tn, kw // tmw, rows // tr)
    a_specs, a_per = _piece_specs(a_pieces, tr, tmw, lambda j, i, k: k, lambda j, i, k: i, a_cb0)
    g_specs, g_per = _piece_specs(g_pieces, tr, tn, lambda j, i, k: k, lambda j, i, k: j, 0)
    if shard_cols:
        per = tn // shard_cols
        out_shape = [jax.ShapeDtypeStruct((n // shard_cols, kw, shard_cols), out_dtype)]
        out_specs = [pl.BlockSpec((per, tmw, shard_cols), lambda j, i, k: (j, i, 0))]

        def store(outs, acc):
            for q in range(per):
                outs[0][q] = acc[:, q * shard_cols:(q + 1) * shard_cols].astype(out_dtype)
    else:
        out_shape = [jax.ShapeDtypeStruct((kw, n), out_dtype)]
        out_specs = [pl.BlockSpec((tmw, tn), lambda j, i, k: (i, j))]

        def store(outs, acc):
            outs[0][...] = acc.astype(out_dtype)
    if colsum:
        out_shape.append(jax.ShapeDtypeStruct((1, n), F32))
        out_specs.append(pl.BlockSpec((1, tn), lambda j, i, k: (0, j)))
    res = _mm_call(name, grid, TN, a_pieces, a_specs, (a_per, lambda j, i, k: i), g_pieces, g_specs,
                   (g_per, lambda j, i, k: j), [], [], out_shape, out_specs, (tmw, tn), a_fn, store,
                   colsum_width=tn if colsum else 0)
    return res if colsum else res[0]


def _ln(x, g, b):
    mu = jnp.mean(x, axis=-1, keepdims=True)
    xc = x - mu
    var = jnp.mean(xc * xc, axis=-1, keepdims=True)
    return xc * lax.rsqrt(var + LN_EPS) * g + b


def _relu2(x):
    r = jnp.maximum(x, 0.0)
    return r * r


def _glu(y, gate):
    return y * jax.nn.sigmoid(gate)


def _ln_bwd(name, r, g, b, gy):
    rows, d = r.shape
    tm = _tile(rows, ROW_TILE)

    def body(r_ref, g_ref, b_ref, gy_ref, gr_ref, gg_ref, gb_ref):
        _, vjp = jax.vjp(_ln, r_ref[...], g_ref[...], b_ref[...])
        gr, gg, gb = vjp(gy_ref[...])
        gr_ref[...] = gr

        @pl.when(pl.program_id(0) == 0)
        def _():
            gg_ref[...] = jnp.zeros_like(gg_ref)
            gb_ref[...] = jnp.zeros_like(gb_ref)

        gg_ref[...] += gg
        gb_ref[...] += gb

    row = pl.BlockSpec((tm, d), lambda i: (i, 0))
    vec = pl.BlockSpec((1, d), lambda i: (0, 0))
    return pl.pallas_call(
        body, name=name, grid=(rows // tm,), in_specs=[row, vec, vec, row], out_specs=[row, vec, vec],
        out_shape=[jax.ShapeDtypeStruct((rows, d), F32), jax.ShapeDtypeStruct((1, d), F32),
                   jax.ShapeDtypeStruct((1, d), F32)],
        compiler_params=_params(("arbitrary",)),
    )(r, g, b, gy)


def _rowwise(name, fn, ins, n_out, width):
    rows = ins[0][0].shape[0]
    tm = _tile(rows, ROW_TILE)

    def body(*refs):
        res = fn(*[r[...] for r in refs[:len(ins)]])
        for o, v in zip(refs[len(ins):], res):
            o[...] = v

    return pl.pallas_call(
        body, name=name, grid=(rows // tm,),
        in_specs=[pl.BlockSpec((tm, wd), lambda i, cb=cb: (i, cb)) for _, cb, wd in ins],
        out_specs=[pl.BlockSpec((tm, width), lambda i: (i, 0))] * n_out,
        out_shape=[jax.ShapeDtypeStruct((rows, width), F32)] * n_out, compiler_params=_params(("parallel",)),
    )(*[a for a, _, _ in ins])


def _loss_grad(name, h, target, n_batch, lp, lead):
    rows, d = h.shape
    nq = lp // LANES
    lead_blocks = lead // LANES

    def body(h_ref, t_ref, g_ref, loss_ref):
        i = pl.program_id(1)

        @pl.when((pl.program_id(0) == 0) & (i == 0))
        def _():
            loss_ref[...] = jnp.zeros_like(loss_ref)

        diff = jnp.where(i >= lead_blocks, h_ref[...] - t_ref[...], 0.0)
        g_ref[...] = diff * (1.0 / d)
        loss_ref[...] += 0.5 * jnp.sum(diff * diff) * (1.0 / d)

    return pl.pallas_call(
        body, name=name, grid=(n_batch, nq),
        in_specs=[pl.BlockSpec((LANES, d), lambda b, i: (b * nq + i, 0)),
                  pl.BlockSpec((None, LANES, d), lambda b, i: (b, jnp.maximum(i - lead_blocks, 0), 0))],
        out_specs=[pl.BlockSpec((LANES, d), lambda b, i: (b * nq + i, 0)),
                   pl.BlockSpec((SUBLANES, LANES), lambda b, i: (0, 0))],
        out_shape=[jax.ShapeDtypeStruct((rows, d), F32), jax.ShapeDtypeStruct((SUBLANES, LANES), F32)],
        compiler_params=_params(("arbitrary", "arbitrary")),
    )(h, target)


def _meta_grad(name, g_h0, n_batch, lp, pad, n_meta):
    d = g_h0.shape[1]
    per = lp // n_meta
    at = pad // n_meta

    def body(g_ref, o_ref):
        @pl.when(pl.program_id(0) == 0)
        def _():
            o_ref[...] = jnp.zeros_like(o_ref)

        o_ref[...] += g_ref[...]

    return pl.pallas_call(
        body, name=name, grid=(n_batch,),
        in_specs=[pl.BlockSpec((n_meta, d), lambda b: (b * per + at, 0))],
        out_specs=pl.BlockSpec((n_meta, d), lambda b: (0, 0)),
        out_shape=jax.ShapeDtypeStruct((n_meta, d), F32),
        compiler_params=_params(("arbitrary",)),
    )(g_h0)


def _s5_param_fn(lr, li, ldt, br, bi):
    dt = jnp.exp(ldt)
    e = jnp.exp(lr * dt)
    w = li * dt
    lbr = e * jnp.cos(w)
    lbi = e * jnp.sin(w)
    nr = lbr - 1.0
    den = lr * lr + li * li
    cr = (nr * lr + lbi * li) / den
    ci = (lbi * lr - nr * li) / den
    bbr = cr[:, None, :] * br - ci[:, None, :] * bi
    bbi = cr[:, None, :] * bi + ci[:, None, :] * br
    return lbr, lbi, bbr, bbi


def _s5_params(name, lr, li, ldt, br, bi):
    def body(lr_ref, li_ref, ldt_ref, br_ref, bi_ref, o1, o2, o3, o4):
        res = _s5_param_fn(lr_ref[...], li_ref[...], ldt_ref[...], br_ref[...], bi_ref[...])
        for o, v in zip((o1, o2, o3, o4), res):
            o[...] = v

    shp = [jax.ShapeDtypeStruct(lr.shape, F32)] * 2 + [jax.ShapeDtypeStruct(br.shape, F32)] * 2
    return pl.pallas_call(body, name=name, out_shape=shp)(lr, li, ldt, br, bi)


def _s5_params_bwd(name, lr, li, ldt, br, bi, g_lbr, g_lbi, g_bbr, g_bbi, gd_parts):
    def body(lr_ref, li_ref, ldt_ref, br_ref, bi_ref, g1, g2, g3, g4, gd_ref, o1, o2, o3, o4, o5, o6):
        _, vjp = jax.vjp(_s5_param_fn, lr_ref[...], li_ref[...], ldt_ref[...], br_ref[...], bi_ref[...])
        res = vjp((jnp.sum(g1[...], axis=0), jnp.sum(g2[...], axis=0), g3[...], g4[...]))
        for o, v in zip((o1, o2, o3, o4, o5), res):
            o[...] = v
        o6[...] = jnp.sum(gd_ref[...], axis=0)

    shp = ([jax.ShapeDtypeStruct(lr.shape, F32)] * 2 + [jax.ShapeDtypeStruct(ldt.shape, F32)]
           + [jax.ShapeDtypeStruct(br.shape, F32)] * 2 + [jax.ShapeDtypeStruct(gd_parts.shape[1:], F32)])
    return pl.pallas_call(body, name=name, out_shape=shp)(lr, li, ldt, br, bi, g_lbr, g_lbi, g_bbr, g_bbi, gd_parts)


def _interleave(re, im, w):
    lead = re.shape[:-1]
    nj = re.shape[-1] // w
    return jnp.stack([re.reshape(*lead, nj, w), im.reshape(*lead, nj, w)], axis=-2).reshape(*lead, 2 * nj * w)


def _deinterleave(x, w):
    lead = x.shape[:-1]
    nj = x.shape[-1] // (2 * w)
    y = x.reshape(*lead, nj, 2, w)
    return y[..., 0, :].reshape(*lead, nj * w), y[..., 1, :].reshape(*lead, nj * w)


def _cmul(ar, ai, br, bi):
    return ar * br - ai * bi, ar * bi + ai * br


def _powers(lr, li):
    p = [(lr, li)]
    p.append(_cmul(*p[0], *p[0]))
    p.append(_cmul(*p[1], *p[0]))
    p.append(_cmul(*p[1], *p[1]))
    p.append(_cmul(*p[3], *p[0]))
    p.append(_cmul(*p[3], *p[1]))
    p.append(_cmul(*p[3], *p[2]))
    p.append(_cmul(*p[3], *p[3]))
    return p


def _scan_tile(xr, xi, steps):
    for sh, br, bi, m in steps:
        rr = jnp.where(m, pltpu.roll(xr, sh, 0), 0.0)
        ri = jnp.where(m, pltpu.roll(xi, sh, 0), 0.0)
        xr, xi = xr + (br * rr - bi * ri), xi + (br * ri + bi * rr)
    return xr, xi


def _s5_scan(name, bu, lam, n_batch, lp, w):
    rows, two_ns = bu.shape
    nj = two_ns // (2 * w)
    nt = lp // SUBLANES

    def body(x_ref, lam_ref, s_ref):
        pw = _powers(lam_ref[:, :w], lam_ref[:, w:])
        tab_r = jnp.concatenate([p[0] for p in pw], axis=0)
        tab_i = jnp.concatenate([p[1] for p in pw], axis=0)
        row = lax.broadcasted_iota(jnp.int32, (SUBLANES, w), 0)
        steps = [(s, jnp.broadcast_to(pw[s - 1][0], (SUBLANES, w)), jnp.broadcast_to(pw[s - 1][1], (SUBLANES, w)),
                  row >= s) for s in (1, 2, 4)]

        def tile(t, carry):
            cr, ci = carry
            r0 = pl.multiple_of(t * SUBLANES, SUBLANES)
            x = x_ref[pl.ds(r0, SUBLANES), :]
            xr, xi = _scan_tile(x[:, :w], x[:, w:], steps)
            sr = xr + (tab_r * cr - tab_i * ci)
            si = xi + (tab_r * ci + tab_i * cr)
            s_ref[pl.ds(r0, SUBLANES), :] = jnp.concatenate([sr, si], axis=1)
            return sr[SUBLANES - 1:, :], si[SUBLANES - 1:, :]

        zero = jnp.zeros((1, w), F32)
        lax.fori_loop(0, nt, tile, (zero, zero))

    spec = pl.BlockSpec((lp, 2 * w), lambda b, j: (b, j))
    return pl.pallas_call(
        body, name=name, grid=(n_batch, nj), in_specs=[spec, pl.BlockSpec((1, 2 * w), lambda b, j: (0, j))],
        out_specs=spec, out_shape=jax.ShapeDtypeStruct((rows, two_ns), F32),
        compiler_params=_params(("parallel", "parallel")),
    )(bu, lam)


def _s5_scan_bwd(name, gd, states, lam, n_batch, lp, w):
    rows, two_ns = gd.shape
    nj = two_ns // (2 * w)
    nt = lp // SUBLANES

    def body(x_ref, s_ref, lam_ref, g_ref, gl_ref):
        pw = _powers(lam_ref[:, :w], -lam_ref[:, w:])
        tab_r = jnp.concatenate([p[0] for p in reversed(pw)], axis=0)
        tab_i = jnp.concatenate([p[1] for p in reversed(pw)], axis=0)
        row = lax.broadcasted_iota(jnp.int32, (SUBLANES, w), 0)
        steps = [(SUBLANES - s, jnp.broadcast_to(pw[s - 1][0], (SUBLANES, w)),
                  jnp.broadcast_to(pw[s - 1][1], (SUBLANES, w)), row < SUBLANES - s) for s in (1, 2, 4)]

        def tile(u, carry):
            cr, ci, ar, ai = carry
            t = nt - 1 - u
            r0 = pl.multiple_of(t * SUBLANES, SUBLANES)
            x = x_ref[pl.ds(r0, SUBLANES), :]
            xr, xi = _scan_tile(x[:, :w], x[:, w:], steps)
            gr = xr + (tab_r * cr - tab_i * ci)
            gi = xi + (tab_r * ci + tab_i * cr)
            g_ref[pl.ds(r0, SUBLANES), :] = jnp.concatenate([gr, gi], axis=1)
            p0 = pl.multiple_of(jnp.maximum(t - 1, 0) * SUBLANES, SUBLANES)
            prev = s_ref[pl.ds(p0, SUBLANES), :][SUBLANES - 1:, :] * jnp.where(t > 0, 1.0, 0.0)
            cur = s_ref[pl.ds(r0, SUBLANES), :]
            spr = jnp.where(row >= 1, pltpu.roll(cur[:, :w], 1, 0), prev[:, :w])
            spi = jnp.where(row >= 1, pltpu.roll(cur[:, w:], 1, 0), prev[:, w:])
            return gr[:1, :], gi[:1, :], ar + gr * spr + gi * spi, ai + gi * spr - gr * spi

        z1 = jnp.zeros((1, w), F32)
        z8 = jnp.zeros((SUBLANES, w), F32)
        _, _, ar, ai = lax.fori_loop(0, nt, tile, (z1, z1, z8, z8))
        gl_ref[...] = jnp.concatenate([jnp.sum(ar, axis=0, keepdims=True), jnp.sum(ai, axis=0, keepdims=True)], axis=1)

    spec = pl.BlockSpec((lp, 2 * w), lambda b, j: (b, j))
    return pl.pallas_call(
        body, name=name, grid=(n_batch, nj),
        in_specs=[spec, spec, pl.BlockSpec((1, 2 * w), lambda b, j: (0, j))],
        out_specs=[spec, pl.BlockSpec((None, 1, 2 * w), lambda b, j: (b, 0, j))],
        out_shape=[jax.ShapeDtypeStruct((rows, two_ns), F32), jax.ShapeDtypeStruct((n_batch, 1, two_ns), F32)],
        compiler_params=_params(("parallel", "parallel")),
    )(gd, states, lam)


def _log_sigmoid(z):
    return jnp.minimum(z, 0.0) - jnp.log(1.0 + jnp.exp(-jnp.abs(z)))


def _attn_masks(i, j, pad):
    rowpos = i * LANES + lax.broadcasted_iota(jnp.int32, (LANES, LANES), 0)
    colpos = j * LANES + lax.broadcasted_iota(jnp.int32, (LANES, LANES), 1)
    return (colpos < rowpos) & (colpos >= pad)


def _tri(strict_upper):
    r = lax.broadcasted_iota(jnp.int32, (LANES, LANES), 0)
    c = lax.broadcasted_iota(jnp.int32, (LANES, LANES), 1)
    return jnp.where(r > c if strict_upper else r < c, 1.0, 0.0).astype(BF16)


def _attn_block(q, k_ref, v_ref, cols, i, j, pad, upper, acc):
    r0 = pl.multiple_of(j * LANES, LANES)
    kj = k_ref[pl.ds(r0, LANES), cols].astype(BF16)
    vj = v_ref[pl.ds(r0, LANES), cols].astype(BF16)
    z = lax.dot_general(q, kj, NT, preferred_element_type=F32)
    vis = _attn_masks(i, j, pad)
    lsz = _log_sigmoid(z)
    lk = jnp.where(vis, lsz - z, 0.0)
    later = _dot_split(lk, upper, NN) + acc
    wgt = jnp.where(vis, jnp.exp(lsz + later), 0.0)
    return r0, kj, vj, lsz, lk, wgt


def _attn_fwd(name, proj, n_batch, lp, pad, q_cb, k_cb, v_cb, n_pairs):
    rows = proj.shape[0]
    nq = lp // LANES
    scale = SB_HEAD_DIM ** -0.5

    def body(q_ref, k_ref, v_ref, o_ref):
        i = pl.program_id(2)
        upper = _tri(True)
        for hh in range(LANES // SB_HEAD_DIM):
            cols = slice(hh * SB_HEAD_DIM, (hh + 1) * SB_HEAD_DIM)
            q = (q_ref[:, cols] * scale).astype(BF16)

            def step(t, carry):
                acc, o = carry
                _, _, vj, _, lk, wgt = _attn_block(q, k_ref, v_ref, cols, i, i - t, pad, upper, acc)
                o = o + lax.dot_general(wgt.astype(BF16), vj, NN, preferred_element_type=F32)
                return acc + jnp.sum(lk, axis=1, keepdims=True), o

            _, o = lax.fori_loop(0, i + 1, step, (jnp.zeros((LANES, 1), F32), jnp.zeros((LANES, SB_HEAD_DIM), F32)))
            o_ref[:, cols] = o

    return pl.pallas_call(
        body, name=name, grid=(n_batch, n_pairs, nq),
        in_specs=[pl.BlockSpec((LANES, LANES), lambda b, h, i: (b * nq + i, q_cb + h)),
                  pl.BlockSpec((lp, LANES), lambda b, h, i: (b, k_cb + h)),
                  pl.BlockSpec((lp, LANES), lambda b, h, i: (b, v_cb + h))],
        out_specs=pl.BlockSpec((LANES, LANES), lambda b, h, i: (b * nq + i, h)),
        out_shape=jax.ShapeDtypeStruct((rows, n_pairs * LANES), F32),
        compiler_params=_params(("parallel", "parallel", "arbitrary")),
    )(proj, proj, proj)


def _attn_bwd(name, proj, g_out, n_batch, lp, pad, q_cb, k_cb, v_cb, go_cb, n_pairs):
    rows = proj.shape[0]
    nq = lp // LANES
    scale = SB_HEAD_DIM ** -0.5

    def body(q_ref, k_ref, v_ref, go_ref, gq_ref, gk_ref, gv_ref, ga_s, sz_s):
        i = pl.program_id(2)

        @pl.when(i == 0)
        def _():
            gk_ref[...] = jnp.zeros_like(gk_ref)
            gv_ref[...] = jnp.zeros_like(gv_ref)

        upper = _tri(True)
        lower = _tri(False)
        for hh in range(LANES // SB_HEAD_DIM):
            cols = slice(hh * SB_HEAD_DIM, (hh + 1) * SB_HEAD_DIM)
            q = (q_ref[:, cols] * scale).astype(BF16)
            go = go_ref[:, cols].astype(BF16)

            def sweep_down(t, acc):
                j = i - t
                r0, _, vj, lsz, lk, wgt = _attn_block(q, k_ref, v_ref, cols, i, j, pad, upper, acc)
                gw = lax.dot_general(go, vj, NT, preferred_element_type=F32)
                ga_s[j] = gw * wgt
                sz_s[j] = jnp.exp(lsz)
                gv_ref[pl.ds(r0, LANES), cols] += lax.dot_general(wgt.astype(BF16), go, TN, preferred_element_type=F32)
                return acc + jnp.sum(lk, axis=1, keepdims=True)

            lax.fori_loop(0, i + 1, sweep_down, jnp.zeros((LANES, 1), F32))

            def sweep_up(j, carry):
                pre, gq = carry
                r0 = pl.multiple_of(j * LANES, LANES)
                kj = k_ref[pl.ds(r0, LANES), cols].astype(BF16)
                vis = _attn_masks(i, j, pad)
                ga = ga_s[j]
                sz = sz_s[j]
                glk = _dot_split(ga, lower, NN) + pre
                gz = jnp.where(vis, ga * (1.0 - sz) - glk * sz, 0.0).astype(BF16)
                gq = gq + lax.dot_general(gz, kj, NN, preferred_element_type=F32)
                gk_ref[pl.ds(r0, LANES), cols] += lax.dot_general(gz, q, TN, preferred_element_type=F32)
                return pre + jnp.sum(ga, axis=1, keepdims=True), gq

            _, gq = lax.fori_loop(0, i + 1, sweep_up, (jnp.zeros((LANES, 1), F32), jnp.zeros((LANES, SB_HEAD_DIM), F32)))
            gq_ref[:, cols] = gq * scale

    blk = lambda cb: pl.BlockSpec((LANES, LANES), lambda b, h, i: (b * nq + i, cb + h))
    full = lambda cb: pl.BlockSpec((lp, LANES), lambda b, h, i: (b, cb + h))
    shp = jax.ShapeDtypeStruct((rows, n_pairs * LANES), F32)
    return pl.pallas_call(
        body, name=name, grid=(n_batch, n_pairs, nq),
        in_specs=[blk(q_cb), full(k_cb), full(v_cb), blk(go_cb)],
        out_specs=[blk(0), full(0), full(0)], out_shape=[shp, shp, shp],
        scratch_shapes=[pltpu.VMEM((nq, LANES, LANES), F32), pltpu.VMEM((nq, LANES, LANES), F32)],
        compiler_params=_params(("parallel", "parallel", "arbitrary")),
    )(proj, proj, proj, g_out)


def _lb_fn(gamma):
    g0, g1 = gamma[0:1, :], gamma[1:2, :]
    mx = jnp.maximum(g0, g1)
    e0, e1 = jnp.exp(g0 - mx), jnp.exp(g1 - mx)
    p0, p1 = e0 / (e0 + e1), e1 / (e0 + e1)
    return (p0 + p1) - p0


def _lower_bound(name, gamma):
    def body(g_ref, o_ref):
        o_ref[...] = _lb_fn(g_ref[...])

    return pl.pallas_call(body, name=name, out_shape=jax.ShapeDtypeStruct((1, gamma.shape[1]), F32))(gamma)


def _lower_bound_bwd(name, gamma, g_lb_parts, g_ng_parts):
    def body(g_ref, glb_ref, gng_ref, o_ref, o2_ref):
        _, vjp = jax.vjp(_lb_fn, g_ref[...])
        o_ref[...] = vjp(jnp.sum(glb_ref[...], axis=0))[0]
        o2_ref[...] = jnp.sum(gng_ref[...], axis=0)

    return pl.pallas_call(
        body, name=name,
        out_shape=[jax.ShapeDtypeStruct(gamma.shape, F32), jax.ShapeDtypeStruct((1, gamma.shape[1]), F32)],
    )(gamma, g_lb_parts, g_ng_parts)


def _hg_gates(fc, lb, rowmask, tril):
    f = lb + (1.0 - lb) * jax.nn.sigmoid(fc)
    bcum = jnp.dot(tril, jnp.log(f) * rowmask, preferred_element_type=F32, precision=lax.Precision.HIGHEST)
    return 1.0 - f, bcum


def _hg_state(fc, ic, lb, st, rowmask, tril):
    k, bcum = _hg_gates(fc, lb, rowmask, tril)
    blast = bcum[HG_CHUNK - 1:, :]
    return jnp.exp(blast) * st + _dot(ic * rowmask, k * jnp.exp(blast - bcum), TN)


def _hg_chunk(qc, fc, ic, gc, lb, ng, st, rowmask, tril):
    k, bcum = _hg_gates(fc, lb, rowmask, tril)
    blast = bcum[HG_CHUNK - 1:, :]
    v = ic * rowmask
    qd = qc * jnp.exp(bcum)
    scores = jnp.where(tril > 0.5, _dot(qd, k * jnp.exp(-bcum), NT), 0.0)
    o = _dot(scores, v, NN) + _dot(qd, st, NT)
    st_new = jnp.exp(blast) * st + _dot(v, k * jnp.exp(blast - bcum), TN)
    o = o * lax.rsqrt(jnp.mean(o * o, axis=-1, keepdims=True) + RMS_EPS) * ng
    return o * (gc * jax.nn.sigmoid(gc)), st_new


def _hg_consts(c, pad):
    r = lax.broadcasted_iota(jnp.int32, (HG_CHUNK, HG_CHUNK), 0)
    cc = lax.broadcasted_iota(jnp.int32, (HG_CHUNK, HG_CHUNK), 1)
    tril = jnp.where(r >= cc, 1.0, 0.0).astype(F32)
    pos = c * HG_CHUNK + lax.broadcasted_iota(jnp.int32, (HG_CHUNK, 1), 0)
    return tril, jnp.where(pos >= pad, 1.0, 0.0).astype(F32)


def _hgrn_fwd(name, proj, lb, ng, n_batch, lp, pad, n_heads):
    rows = proj.shape[0]
    nc = lp // HG_CHUNK

    def body(q_ref, f_ref, i_ref, g_ref, lb_ref, ng_ref, o_ref):
        lbv, ngv = lb_ref[...], ng_ref[...]

        def chunk(c, st):
            sl = pl.ds(pl.multiple_of(c * HG_CHUNK, HG_CHUNK), HG_CHUNK)
            tril, rowmask = _hg_consts(c, pad)
            o, st = _hg_chunk(q_ref[sl, :], f_ref[sl, :], i_ref[sl, :], g_ref[sl, :], lbv, ngv, st, rowmask, tril)
            o_ref[sl, :] = o
            return st

        lax.fori_loop(0, nc, chunk, jnp.zeros((HG_DK, HG_DK), F32))

    col = lambda off: pl.BlockSpec((lp, HG_DK), lambda b, h: (b, off * n_heads + h))
    vec = pl.BlockSpec((1, HG_DK), lambda b, h: (0, h))
    return pl.pallas_call(
        body, name=name, grid=(n_batch, n_heads), in_specs=[col(0), col(1), col(2), col(3), vec, vec],
        out_specs=col(0), out_shape=jax.ShapeDtypeStruct((rows, n_heads * HG_DK), F32),
        compiler_params=_params(("parallel", "parallel")),
    )(proj, proj, proj, proj, lb, ng)


def _hgrn_bwd(name, proj, lb, ng, g_out, n_batch, lp, pad, n_heads):
    rows = proj.shape[0]
    width = n_heads * HG_DK
    nc = lp // HG_CHUNK

    def body(q_ref, f_ref, i_ref, g_ref, lb_ref, ng_ref, go_ref, gq_ref, gf_ref, gi_ref, gg_ref, glb_ref, gng_ref, st_s):
        lbv, ngv = lb_ref[...], ng_ref[...]

        def fwd(c, st):
            st_s[c] = st
            sl = pl.ds(pl.multiple_of(c * HG_CHUNK, HG_CHUNK), HG_CHUNK)
            tril, rowmask = _hg_consts(c, pad)
            return _hg_state(f_ref[sl, :], i_ref[sl, :], lbv, st, rowmask, tril)

        lax.fori_loop(0, nc, fwd, jnp.zeros((HG_DK, HG_DK), F32))

        def bwd(u, carry):
            gst, glb, gng = carry
            c = nc - 1 - u
            sl = pl.ds(pl.multiple_of(c * HG_CHUNK, HG_CHUNK), HG_CHUNK)
            tril, rowmask = _hg_consts(c, pad)
            fn = functools.partial(_hg_chunk, rowmask=rowmask, tril=tril)
            _, vjp = jax.vjp(fn, q_ref[sl, :], f_ref[sl, :], i_ref[sl, :], g_ref[sl, :], lbv, ngv, st_s[c])
            gq, gf, gi, gg, dlb, dng, gst = vjp((go_ref[sl, :], gst))
            gq_ref[sl, :] = gq
            gf_ref[sl, :] = gf
            gi_ref[sl, :] = gi
            gg_ref[sl, :] = gg
            return gst, glb + dlb, gng + dng

        zv = jnp.zeros((1, HG_DK), F32)
        _, glb, gng = lax.fori_loop(0, nc, bwd, (jnp.zeros((HG_DK, HG_DK), F32), zv, zv))
        glb_ref[...] = glb
        gng_ref[...] = gng

    col = lambda off: pl.BlockSpec((lp, HG_DK), lambda b, h: (b, off * n_heads + h))
    vec = pl.BlockSpec((1, HG_DK), lambda b, h: (0, h))
    part = pl.BlockSpec((None, 1, HG_DK), lambda b, h: (b, 0, h))
    big = jax.ShapeDtypeStruct((rows, width), F32)
    small = jax.ShapeDtypeStruct((n_batch, 1, width), F32)
    return pl.pallas_call(
        body, name=name, grid=(n_batch, n_heads),
        in_specs=[col(0), col(1), col(2), col(3), vec, vec, col(0)],
        out_specs=[col(0), col(0), col(0), col(0), part, part],
        out_shape=[big, big, big, big, small, small],
        scratch_shapes=[pltpu.VMEM((nc, HG_DK, HG_DK), F32)],
        compiler_params=_params(("parallel", "parallel")),
    )(proj, proj, proj, proj, lb, ng, g_out)


def _exchange(name, srcs, scatter):
    nw = len(srcs)

    def body(*refs):
        src, dst = refs[:nw], refs[nw:2 * nw]
        send, recv, loc = refs[2 * nw:]
        x, y, c = lax.axis_index("x"), lax.axis_index("y"), lax.axis_index("c")
        me = 4 * x + 2 * y + c
        started = []
        for w in range(nw):
            own = pltpu.make_async_copy(src[w].at[me] if scatter else src[w], dst[w].at[me], loc.at[w])
            own.start()
            started.append(own)
        for k in range(1, N_DEV):
            px = 1 - x if k & 4 else x
            py = 1 - y if k & 2 else y
            pc = 1 - c if k & 1 else c
            peer = 4 * px + 2 * py + pc
            for w in range(nw):
                cp = pltpu.make_async_remote_copy(
                    src_ref=src[w].at[peer] if scatter else src[w], dst_ref=dst[w].at[me],
                    send_sem=send.at[w, k - 1], recv_sem=recv.at[w, k - 1],
                    device_id=(px, py, pc), device_id_type=pl.DeviceIdType.MESH)
                cp.start()
                started.append(cp)
        for cp in started:
            cp.wait()

    any_spec = pl.BlockSpec(memory_space=pl.ANY)
    out_shape = [jax.ShapeDtypeStruct(s.shape if scatter else (N_DEV,) + s.shape, s.dtype) for s in srcs]
    return pl.pallas_call(
        body, name=name, in_specs=[any_spec] * nw, out_specs=[any_spec] * nw, out_shape=out_shape,
        scratch_shapes=[pltpu.SemaphoreType.DMA((nw, N_DEV - 1)), pltpu.SemaphoreType.DMA((nw, N_DEV - 1)),
                        pltpu.SemaphoreType.DMA((nw,))],
    )(*srcs)


def _adamw(w, g, m, v):
    m = ADAM_B1 * m + (1.0 - ADAM_B1) * g
    v = ADAM_B2 * v + (1.0 - ADAM_B2) * (g * g)
    m_hat = m / (1.0 - ADAM_B1 ** ADAM_STEP)
    v_hat = v / (1.0 - ADAM_B2 ** ADAM_STEP)
    delta = -ADAM_LR * (m_hat / (jnp.sqrt(v_hat) + ADAM_EPS) + ADAM_WD * w)
    return delta, m, v


def _adamw_summed(name, parts, w, m, v):
    rows, cols = w.shape
    n_parts = parts.shape[0]
    tr = _tile(rows, max(SUBLANES, (1 << 18) // cols))

    def body(p_ref, w_ref, m_ref, v_ref, g_ref, d_ref, nm_ref, nv_ref):
        g = p_ref[0].astype(F32)
        for s in range(1, n_parts):
            g = g + p_ref[s].astype(F32)
        d, nm, nv = _adamw(w_ref[...], g, m_ref[...], v_ref[...])
        g_ref[...] = g
        d_ref[...] = d
        nm_ref[...] = nm
        nv_ref[...] = nv

    spec = pl.BlockSpec((tr, cols), lambda i: (i, 0))
    shp = jax.ShapeDtypeStruct((rows, cols), F32)
    return pl.pallas_call(
        body, name=name, grid=(rows // tr,),
        in_specs=[pl.BlockSpec((n_parts, tr, cols), lambda i: (0, i, 0)), spec, spec, spec],
        out_specs=[spec] * 4, out_shape=[shp] * 4, compiler_params=_params(("parallel",)),
    )(parts, w, m, v)


def _pack_rows(arrays, cols):
    out = []
    for a in arrays:
        flat = a.reshape(-1)
        n = -(-flat.shape[0] // cols) * cols
        out.append(jnp.pad(flat, (0, n - flat.shape[0])).reshape(-1, cols))
    packed = jnp.concatenate(out, axis=0)
    return jnp.pad(packed, ((0, -packed.shape[0] % SUBLANES), (0, 0)))


def _unpack_rows(packed, shapes, cols):
    out, r = [], 0
    for s in shapes:
        n = math.prod(s)
        nr = -(-n // cols)
        out.append(packed[r:r + nr].reshape(-1)[:n].reshape(s))
        r += nr
    return out


def _block_diag(blocks):
    g, a, b = blocks.shape
    eye = jnp.eye(g, dtype=blocks.dtype)
    return (eye[:, None, :, None] * blocks[:, :, None, :]).reshape(g * a, g * b)


def _diag_blocks(dense, g):
    a, b = dense.shape[0] // g, dense.shape[1] // g
    return jnp.einsum("gagb->gab", dense.reshape(g, a, g, b))


def _local_step(x, target, meta, wts, small):
    n_batch, seq, d = x.shape
    n_meta = meta.shape[0]
    pad = -(seq + n_meta) % LANES
    lead = pad + n_meta
    lp = lead + seq
    rows = n_batch * lp
    s5w = wts["glu"].shape[0]
    n_ab = wts["in_ab"].shape[2]
    n_c = wts["in_c"].shape[2]
    n_up = wts["up"][0].shape[2]
    ab_cols = wts["in_ab"].shape[0] * n_ab
    sbw = (ab_cols - s5w) // 3
    dff = wts["up"][0].shape[0] * n_up
    n_pairs = sbw // LANES
    n_hg = d // HG_DK
    s5_cb = s5w // LANES
    sb_cb = sbw // LANES
    tm = _tile(rows, ROW_TILE)
    groups, n_state, grp = small["s5_b_re"].shape[1:]
    ns = groups * n_state
    sw = min(SCAN_LANES, ns)

    h0 = jnp.concatenate(
        [jnp.zeros((n_batch, pad, d), F32), jnp.broadcast_to(meta[None], (n_batch, n_meta, d)), x], axis=1
    ).reshape(rows, d)

    lam_re, lam_im = small["s5_lam_re"][0], small["s5_lam_im"][0]
    log_dt = small["s5_log_dt"][0][:, None]
    b_re_t = small["s5_b_re"][0].transpose(0, 2, 1)
    b_im_t = small["s5_b_im"][0].transpose(0, 2, 1)
    c_re, c_im = small["s5_c_re"][0], small["s5_c_im"][0]
    lbr, lbi, bbr, bbi = _s5_params("s5_params", lam_re, lam_im, log_dt, b_re_t, b_im_t)
    b_blk = _interleave(_block_diag(bbr), _block_diag(bbi), sw).astype(BF16)
    c_blk = _interleave(_block_diag(c_re), _block_diag(-c_im), sw).T.astype(BF16)
    lam_row = _interleave(lbr.reshape(1, ns), lbi.reshape(1, ns), sw)
    d_row = small["s5_d"].reshape(1, s5w)

    def ln_store(outs, acc, res, bias, g, b):
        r = ALPHA * res + acc + bias
        outs[0][...] = r
        outs[1][...] = _ln(r, g, b)

    zero_bias = jnp.zeros((1, d), F32)

    def mix_ln(name, a, w, k_total, tk, res, bias, g, b, a_fn=None):
        return _mm_act(name, a, w, "nat", n_out_cols=d, k_total=k_total, tn=d, tk=tk, a_fn=a_fn,
                       extras=(res, bias, g, b), extra_specs=(_row_spec(tm, d), _vec_spec(d), _vec_spec(d), _vec_spec(d)),
                       store=ln_store, out_shape=[jax.ShapeDtypeStruct((rows, d), F32)] * 2,
                       out_specs=[_row_spec(tm, d)] * 2)

    def two(width):
        return [jax.ShapeDtypeStruct((rows, width), F32)] * 2, [_row_spec(tm, width)] * 2

    proj_ab = _mm_act("in_ab", h0, wts["in_ab"], "stk", n_out_cols=ab_cols, k_total=d, tn=n_ab, tk=d)[0]
    bu = _mm_act("s5_bu", proj_ab, b_blk, "nat", n_out_cols=2 * ns, k_total=s5w, tn=min(2 * ns, 2048), tk=s5w)[0]
    states = _s5_scan("s5_scan", bu, lam_row, n_batch, lp, sw)

    def gelu_store(outs, acc, u, dv):
        ypre = acc + dv * u
        outs[0][...] = ypre
        outs[1][...] = jax.nn.gelu(ypre)

    shp2, spec2 = two(s5w)
    ypre, y = _mm_act(
        "s5_y", states, c_blk, "nat", n_out_cols=s5w, k_total=2 * ns, tn=s5w, tk=min(2 * ns, 1024),
        extras=(proj_ab, d_row), extra_specs=(_row_spec(tm, s5w), _vec_spec(s5w)), store=gelu_store,
        out_shape=shp2, out_specs=spec2)

    def glu_store(outs, acc, yv, bias):
        gate = acc + bias
        outs[0][...] = gate
        outs[1][...] = _glu(yv, gate)

    gate, a_out = _mm_act(
        "s5_glu", y, wts["glu"], "nat", n_out_cols=s5w, k_total=s5w, tn=s5w, tk=s5w,
        extras=(y, small["s5_b_glu"]), extra_specs=(_row_spec(tm, s5w), _vec_spec(s5w)), store=glu_store,
        out_shape=shp2, out_specs=spec2)
    b_out = _attn_fwd("sb_attn", proj_ab, n_batch, lp, pad, s5_cb, s5_cb + sb_cb, s5_cb + 2 * sb_cb, n_pairs)

    def bias_store(outs, acc, bias):
        outs[0][...] = acc + bias

    def mlp_fwd(layer, h_in):
        up = _mm_act(f"up{layer}", h_in, wts["up"][layer], "stk", n_out_cols=dff, k_total=d, tn=n_up, tk=d,
                     extras=(small["mlp_b_up"][layer:layer + 1],), extra_specs=(_vec_spec(n_up),), store=bias_store)[0]
        r, h = mix_ln(f"down{layer}", up, wts["down"][layer], dff, min(dff, 1024), h_in,
                      small["mlp_b_down"][layer:layer + 1], small["ln_mlp_g"][layer:layer + 1],
                      small["ln_mlp_b"][layer:layer + 1], a_fn=_relu2)
        return up, r, h

    r1, h1 = mix_ln("out_ab", [a_out, b_out], wts["out_ab"], s5w + sbw, min(s5w, sbw), h0, zero_bias,
                    small["ln_mix_g"][0:1], small["ln_mix_b"][0:1])
    up0, r2, h2 = mlp_fwd(0, h1)

    lb = _lower_bound("hg_lb", small["hgrn_gamma"])
    proj_c = _mm_act("in_c", h2, wts["in_c"], "stk", n_out_cols=4 * d, k_total=d, tn=n_c, tk=d)[0]
    c_out = _hgrn_fwd("hgrn", proj_c, lb, wts["ng"], n_batch, lp, pad, n_hg)
    r3, h3 = mix_ln("out_c", c_out, wts["out_c"], d, d, h2, zero_bias, small["ln_mix_g"][1:2], small["ln_mix_b"][1:2])
    up1, r4, h4 = mlp_fwd(1, h3)

    g_h4, loss_tile = _loss_grad("loss", h4, target, n_batch, lp, lead)

    gr = {}

    def res_store(outs, acc, g_res):
        outs[0][...] = acc + ALPHA * g_res

    def mlp_bwd(layer, g_h_out, r_out, up, h_in):
        g_r, gr[f"ln_mlp_g{layer}"], gr[f"ln_mlp_b{layer}"] = _ln_bwd(
            f"ln_mlp_bwd{layer}", r_out, small["ln_mlp_g"][layer:layer + 1], small["ln_mlp_b"][layer:layer + 1], g_h_out)

        def gup_store(outs, acc, upv):
            outs[0][...] = acc * (2.0 * jnp.maximum(upv, 0.0))

        tf = min(dff, 1024)
        g_up = _mm_act(f"g_up{layer}", g_r, wts["down"][layer], "natT", n_out_cols=dff, k_total=d, tn=tf, tk=d,
                       extras=(up,), extra_specs=(_row_spec(tm, tf),), store=gup_store)[0]
        gr[f"down{layer}"], gr[f"mlp_b_down{layer}"] = _mm_wgrad(
            f"dw_down{layer}", up, g_r, kw=dff, n=d, tmw=tf, tn=d, a_fn=_relu2, out_dtype=BF16, colsum=True)
        gr[f"up{layer}"], gr[f"mlp_b_up{layer}"] = _mm_wgrad(
            f"dw_up{layer}", h_in, g_up, kw=d, n=dff, tmw=d, tn=min(dff, 2048), shard_cols=n_up, out_dtype=BF16, colsum=True)
        return _mm_act(f"g_hmid{layer}", g_up, wts["up"][layer], "stkT", n_out_cols=d, k_total=dff, tn=d, tk=n_up,
                       extras=(g_r,), extra_specs=(_row_spec(tm, d),), store=res_store)[0]

    g_h3 = mlp_bwd(1, g_h4, r4, up1, h3)
    g_r3, gr["ln_mix_g1"], gr["ln_mix_b1"] = _ln_bwd("ln_mix_bwd1", r3, small["ln_mix_g"][1:2], small["ln_mix_b"][1:2], g_h3)
    g_cout = _mm_act("g_cout", g_r3, wts["out_c"], "natT", n_out_cols=d, k_total=d, tn=d, tk=d)[0]
    gr["out_c"] = _mm_wgrad("dw_out_c", c_out, g_r3, kw=d, n=d, tmw=d, tn=d, out_dtype=BF16)
    gq, gf, gi, gg_, g_lb_parts, g_ng_parts = _hgrn_bwd("hgrn_bwd", proj_c, lb, wts["ng"], g_cout, n_batch, lp, pad, n_hg)
    g_pc = [gq, gf, gi, gg_]
    gr["hgrn_gamma"], gr["ng"] = _lower_bound_bwd("hg_lb_bwd", small["hgrn_gamma"], g_lb_parts, g_ng_parts)
    gr["in_c"] = _mm_wgrad("dw_in_c", h2, g_pc, kw=d, n=4 * d, tmw=d, tn=d, shard_cols=n_c, out_dtype=BF16)
    g_h2 = _mm_act("g_h2", g_pc, wts["in_c"], "stkT", n_out_cols=d, k_total=4 * d, tn=d, tk=n_c,
                   extras=(g_r3,), extra_specs=(_row_spec(tm, d),), store=res_store)[0]

    g_h1 = mlp_bwd(0, g_h2, r2, up0, h1)
    g_r1, gr["ln_mix_g0"], gr["ln_mix_b0"] = _ln_bwd("ln_mix_bwd0", r1, small["ln_mix_g"][0:1], small["ln_mix_b"][0:1], g_h1)
    g_cat = _mm_act("g_cat", g_r1, wts["out_ab"], "natT", n_out_cols=d, k_total=d, tn=d, tk=d)[0]
    gr["out_ab"] = _mm_wgrad("dw_out_ab", [a_out, b_out], g_r1, kw=s5w + sbw, n=d, tmw=min(s5w, sbw), tn=d, out_dtype=BF16)
    g_q, g_k, g_v = _attn_bwd("sb_attn_bwd", proj_ab, g_cat, n_batch, lp, pad, s5_cb, s5_cb + sb_cb, s5_cb + 2 * sb_cb,
                              s5_cb, n_pairs)

    g_y_direct, g_gate = _rowwise("s5_glu_bwd", lambda ga, yv, gt: jax.vjp(_glu, yv, gt)[1](ga),
                                  [(g_cat, 0, s5w), (y, 0, s5w), (gate, 0, s5w)], 2, s5w)

    def gelu_bwd_store(outs, acc, gyd, yp, u, dv):
        gyp = jax.vjp(jax.nn.gelu, yp)[1](acc + gyd)[0]
        outs[0][...] = gyp
        outs[1][...] = dv * gyp
        outs[2][...] = jnp.sum(gyp * u, axis=0, keepdims=True)

    rs = _row_spec(tm, s5w)
    g_ypre, g_u_direct, gd_parts = _mm_act(
        "s5_g_y", g_gate, wts["glu"], "natT", n_out_cols=s5w, k_total=s5w, tn=s5w, tk=s5w,
        extras=(g_y_direct, ypre, proj_ab, d_row), extra_specs=(rs, rs, rs, _vec_spec(s5w)), store=gelu_bwd_store,
        out_shape=[jax.ShapeDtypeStruct((rows, s5w), F32)] * 2 + [jax.ShapeDtypeStruct((rows // tm, 1, s5w), F32)],
        out_specs=[rs, rs, pl.BlockSpec((None, 1, s5w), lambda i, j, k: (i, 0, j))])
    gr["glu"], gr["s5_b_glu"] = _mm_wgrad("dw_glu", y, g_gate, kw=s5w, n=s5w, tmw=s5w, tn=s5w, out_dtype=BF16, colsum=True)
    g_sd = _mm_act("s5_g_states", g_ypre, c_blk, "natT", n_out_cols=2 * ns, k_total=s5w, tn=min(2 * ns, 2048), tk=s5w)[0]
    d_cblk = _mm_wgrad("dw_cblk", states, g_ypre, kw=2 * ns, n=s5w, tmw=min(2 * ns, 1024), tn=s5w)
    gs, gl_parts = _s5_scan_bwd("s5_scan_bwd", g_sd, states, lam_row, n_batch, lp, sw)

    def add_store(outs, acc, other):
        outs[0][...] = acc + other

    g_u = _mm_act("s5_g_u", gs, b_blk, "natT", n_out_cols=s5w, k_total=2 * ns, tn=s5w, tk=min(2 * ns, 1024),
                  extras=(g_u_direct,), extra_specs=(rs,), store=add_store)[0]
    d_bblk = _mm_wgrad("dw_bblk", proj_ab, gs, kw=s5w, n=2 * ns, tmw=s5w, tn=min(2 * ns, 2048))
    db_re, db_im = _deinterleave(d_bblk, sw)
    dc_re, dc_im = _deinterleave(d_cblk.T, sw)
    glr, gli = _deinterleave(gl_parts, sw)
    g_lam_re, g_lam_im, g_log_dt, g_b_re_t, g_b_im_t, g_d = _s5_params_bwd(
        "s5_params_bwd", lam_re, lam_im, log_dt, b_re_t, b_im_t,
        glr.reshape(n_batch, groups, n_state), gli.reshape(n_batch, groups, n_state),
        _diag_blocks(db_re, groups), _diag_blocks(db_im, groups), gd_parts)

    g_pab = [g_u, g_q, g_k, g_v]
    assert s5w == sbw
    gr["in_ab"] = _mm_wgrad("dw_in_ab", h0, g_pab, kw=d, n=ab_cols, tmw=d, tn=s5w, shard_cols=n_ab, out_dtype=BF16)
    g_h0 = _mm_act("g_h0", g_pab, wts["in_ab"], "stkT", n_out_cols=d, k_total=ab_cols, tn=d, tk=n_ab,
                   extras=(g_r1,), extra_specs=(_row_spec(tm, d),), store=res_store)[0]
    grad_x = g_h0.reshape(n_batch, lp, d)[:, lead:, :]
    g_meta = _meta_grad("g_meta", g_h0, n_batch, lp, pad, n_meta)

    cat2 = lambda key: jnp.concatenate([gr[key + "0"], gr[key + "1"]], axis=0)
    small_grads = {
        "s5_lam_re": g_lam_re[None], "s5_lam_im": g_lam_im[None], "s5_log_dt": g_log_dt.reshape(1, groups),
        "s5_b_re": g_b_re_t.transpose(0, 2, 1)[None], "s5_b_im": g_b_im_t.transpose(0, 2, 1)[None],
        "s5_c_re": _diag_blocks(dc_re, groups)[None], "s5_c_im": -_diag_blocks(dc_im, groups)[None],
        "s5_d": g_d.reshape(1, groups, grp), "s5_b_glu": gr["s5_b_glu"], "hgrn_gamma": gr["hgrn_gamma"],
        "ln_mix_g": cat2("ln_mix_g"), "ln_mix_b": cat2("ln_mix_b"), "mlp_b_up": cat2("mlp_b_up"),
        "mlp_b_down": cat2("mlp_b_down"), "ln_mlp_g": cat2("ln_mlp_g"), "ln_mlp_b": cat2("ln_mlp_b"),
    }
    big_grads = {
        "meta": g_meta, "in_ab": gr["in_ab"], "glu": gr["glu"], "out_ab": gr["out_ab"], "in_c": gr["in_c"],
        "ng": gr["ng"], "out_c": gr["out_c"], "up": [gr["up0"], gr["up1"]], "down": [gr["down0"], gr["down1"]],
    }
    return loss_tile, grad_x, big_grads, small_grads


SMALL_NAMES = ("s5_lam_re", "s5_lam_im", "s5_log_dt", "s5_b_re", "s5_b_im", "s5_c_re", "s5_c_im", "s5_d", "s5_b_glu",
               "hgrn_gamma", "ln_mix_g", "ln_mix_b", "mlp_b_up", "mlp_b_down", "ln_mlp_g", "ln_mlp_b")
WEIGHT_ORDER = ("meta", "w_in_ab", "s5_lam_re", "s5_lam_im", "s5_log_dt", "s5_b_re", "s5_b_im", "s5_c_re", "s5_c_im",
                "s5_d", "s5_w_glu", "s5_b_glu", "w_out_ab", "w_in_c", "hgrn_gamma", "hgrn_norm_g", "w_out_c", "ln_mix_g",
                "ln_mix_b", "mlp_w_up", "mlp_b_up", "mlp_w_down", "mlp_b_down", "ln_mlp_g", "ln_mlp_b")


def kernel(x, meta, w_in_ab, s5_lam_re, s5_lam_im, s5_log_dt, s5_b_re, s5_b_im, s5_c_re, s5_c_im, s5_d, s5_w_glu, s5_b_glu, w_out_ab, w_in_c, hgrn_gamma, hgrn_norm_g, w_out_c, ln_mix_g, ln_mix_b, mlp_w_up, mlp_b_up, mlp_w_down, mlp_b_down, ln_mlp_g, ln_mlp_b, loss_target, m_meta, m_w_in_ab, m_s5_lam_re, m_s5_lam_im, m_s5_log_dt, m_s5_b_re, m_s5_b_im, m_s5_c_re, m_s5_c_im, m_s5_d, m_s5_w_glu, m_s5_b_glu, m_w_out_ab, m_w_in_c, m_hgrn_gamma, m_hgrn_norm_g, m_w_out_c, m_ln_mix_g, m_ln_mix_b, m_mlp_w_up, m_mlp_b_up, m_mlp_w_down, m_mlp_b_down, m_ln_mlp_g, m_ln_mlp_b, v_meta, v_w_in_ab, v_s5_lam_re, v_s5_lam_im, v_s5_log_dt, v_s5_b_re, v_s5_b_im, v_s5_c_re, v_s5_c_im, v_s5_d, v_s5_w_glu, v_s5_b_glu, v_w_out_ab, v_w_in_c, v_hgrn_gamma, v_hgrn_norm_g, v_w_out_c, v_ln_mix_g, v_ln_mix_b, v_mlp_w_up, v_mlp_b_up, v_mlp_w_down, v_mlp_b_down, v_ln_mlp_g, v_ln_mlp_b):
    args = dict(locals())
    w = {n: args[n] for n in WEIGHT_ORDER}
    mom = {n: args["m_" + n] for n in WEIGHT_ORDER}
    var = {n: args["v_" + n] for n in WEIGHT_ORDER}
    d = x.shape[2]
    n_meta = meta.shape[0]

    shards = [
        w["meta"], w["hgrn_norm_g"], w["w_in_ab"][0].astype(BF16), w["s5_w_glu"][0].astype(BF16),
        w["w_out_ab"][0].astype(BF16), w["w_in_c"][0].astype(BF16), w["w_out_c"][0].astype(BF16),
        w["mlp_w_up"][0].astype(BF16), w["mlp_w_up"][1].astype(BF16), w["mlp_w_down"][0].astype(BF16),
        w["mlp_w_down"][1].astype(BF16),
    ]
    a_meta, a_ng, a_in_ab, a_glu, a_out_ab, a_in_c, a_out_c, a_up0, a_up1, a_dn0, a_dn1 = _exchange(
        "gather_weights", shards, False)
    wts = {
        "in_ab": a_in_ab, "glu": a_glu.reshape(-1, a_glu.shape[2]), "out_ab": a_out_ab.reshape(-1, d), "in_c": a_in_c,
        "ng": a_ng.transpose(1, 0, 2).reshape(1, d), "out_c": a_out_c.reshape(-1, d), "up": [a_up0, a_up1],
        "down": [a_dn0.reshape(-1, d), a_dn1.reshape(-1, d)],
    }
    meta_full = a_meta.transpose(1, 0, 2).reshape(n_meta, d)
    small = {n: w[n] for n in SMALL_NAMES}

    loss_tile, grad_x, big, sg = _local_step(x, loss_target, meta_full, wts, small)

    n_loc = d // N_DEV
    parts = [
        big["meta"].reshape(n_meta, N_DEV, n_loc).transpose(1, 0, 2),
        big["ng"].reshape(1, N_DEV, n_loc).transpose(1, 0, 2),
        big["in_ab"], big["glu"].reshape(N_DEV, -1, big["glu"].shape[1]), big["out_ab"].reshape(N_DEV, -1, d),
        big["in_c"], big["out_c"].reshape(N_DEV, -1, d), big["up"][0], big["up"][1],
        big["down"][0].reshape(N_DEV, -1, d), big["down"][1].reshape(N_DEV, -1, d),
    ]
    recv = _exchange("scatter_grads", parts, True)
    names = ("meta", "hgrn_norm_g", "w_in_ab", "s5_w_glu", "w_out_ab", "w_in_c", "w_out_c", "mlp_w_up", "mlp_w_up",
             "mlp_w_down", "mlp_w_down")
    layer_of = (None, None, 0, 0, 0, 0, 0, 0, 1, 0, 1)
    res = {}
    for idx, (nm, ly, rc) in enumerate(zip(names, layer_of, recv)):
        sel = (lambda t: t) if ly is None else (lambda t, ly=ly: t[ly])
        res.setdefault(nm, []).append(_adamw_summed(f"adamw_{idx}", rc, sel(w[nm]), sel(mom[nm]), sel(var[nm])))

    shapes = [w[n].shape for n in SMALL_NAMES] + [loss_tile.shape]
    zeros = jnp.zeros_like(loss_tile)
    g_pack = _pack_rows([sg[n] for n in SMALL_NAMES] + [loss_tile], PACK_COLS)
    w_pack = _pack_rows([w[n] for n in SMALL_NAMES] + [zeros], PACK_COLS)
    m_pack = _pack_rows([mom[n] for n in SMALL_NAMES] + [zeros], PACK_COLS)
    v_pack = _pack_rows([var[n] for n in SMALL_NAMES] + [zeros], PACK_COLS)
    g_all = _exchange("gather_small", [g_pack], False)[0]
    packed = _adamw_summed("adamw_small", g_all, w_pack, m_pack, v_pack)
    unpacked = [_unpack_rows(p, shapes, PACK_COLS) for p in packed]
    loss = unpacked[0][-1][0, 0]

    def pick(nm, which):
        if nm in SMALL_NAMES:
            return unpacked[which][SMALL_NAMES.index(nm)]
        outs = [o[which] for o in res[nm]]
        return outs[0] if nm in ("meta", "hgrn_norm_g") else jnp.stack(outs, axis=0)

    return (loss, grad_x, *[pick(n, 0) for n in WEIGHT_ORDER], *[pick(n, 1) for n in WEIGHT_ORDER],
            *[pick(n, 2) for n in WEIGHT_ORDER], *[pick(n, 3) for n in WEIGHT_ORDER])
```

```python
import functools
import math

import jax
import jax.numpy as jnp
from jax import lax
from jax.experimental import pallas as pl
from jax.experimental.pallas import tpu as pltpu

F32 = jnp.float32
BF16 = jnp.bfloat16

N_DEV = 8
DEPTH = 2
ALPHA = (2.0 * DEPTH) ** 0.25
LN_EPS = 1e-5
RMS_EPS = 1e-6
SB_HEAD_DIM = 64
HG_DK = 128
HG_CHUNK = 64
LANES = 128
SUBLANES = 8
VMEM_LIMIT_BYTES = 56 * 1024 * 1024
ROW_TILE = 544
SCAN_LANES = 256
PACK_COLS = 1024

ADAM_LR = 0.001
ADAM_B1 = 0.9
ADAM_B2 = 0.999
ADAM_EPS = 1e-08
ADAM_WD = 0.01
ADAM_STEP = 10

NN = (((1,), (0,)), ((), ()))
NT = (((1,), (1,)), ((), ()))
TN = (((0,), (0,)), ((), ()))


def _tile(n, pref, align=SUBLANES):
    t = min(n, pref)
    t -= t % align
    while t >= align:
        if n % t == 0:
            return t
        t -= align
    return n


def _params(sem):
    return pltpu.CompilerParams(dimension_semantics=sem, vmem_limit_bytes=VMEM_LIMIT_BYTES)


def _dot_raw(a, b, dims):
    return lax.dot_general(a.astype(BF16), b.astype(BF16), dims, preferred_element_type=F32)


def _make_dot(dims, da_rule, db_rule):
    @jax.custom_vjp
    def f(a, b):
        return _dot_raw(a, b, dims)

    def fwd(a, b):
        return _dot_raw(a, b, dims), (a, b)

    def bwd(res, g):
        a, b = res
        return da_rule(g, a, b), db_rule(g, a, b)

    f.defvjp(fwd, bwd)
    return f


_DOTS = {
    NN: _make_dot(NN, lambda g, a, b: _dot_raw(g, b, NT), lambda g, a, b: _dot_raw(a, g, TN)),
    NT: _make_dot(NT, lambda g, a, b: _dot_raw(g, b, NN), lambda g, a, b: _dot_raw(g, a, TN)),
    TN: _make_dot(TN, lambda g, a, b: _dot_raw(b, g, NT), lambda g, a, b: _dot_raw(a, g, NN)),
}


def _dot(a, b, dims):
    return _DOTS[dims](a, b)


def _dot_split(a, b, dims):
    hi = a.astype(BF16)
    lo = (a - hi.astype(F32)).astype(BF16)
    return (lax.dot_general(hi, b, dims, preferred_element_type=F32)
            + lax.dot_general(lo, b, dims, preferred_element_type=F32))


def _piece_specs(pieces, block_rows, block_cols, row_of, col_of, cb0):
    per = pieces[0].shape[1] // block_cols if len(pieces) > 1 else None
    specs = []
    for p in range(len(pieces)):
        if per is None:
            specs.append(pl.BlockSpec((block_rows, block_cols), lambda *g: (row_of(*g), cb0 + col_of(*g))))
        else:
            specs.append(pl.BlockSpec(
                (block_rows, block_cols),
                lambda *g, p=p: (row_of(*g), jnp.clip(col_of(*g) - p * per, 0, per - 1))))
    return specs, per


def _mm_call(name, grid, dims, a_pieces, a_specs, a_sel, b_pieces, b_specs, b_sel, extras, extra_specs,
             out_shape, out_specs, acc_shape, a_fn, store, colsum_width=0):
    na, nb, ne, no = len(a_pieces), len(b_pieces), len(extras), len(out_shape)
    nk = grid[2]

    def body(*refs):
        a_refs, b_refs = refs[:na], refs[na:na + nb]
        extra = refs[na + nb:na + nb + ne]
        outs = refs[na + nb + ne:na + nb + ne + no]
        acc = refs[na + nb + ne + no]
        ids = (pl.program_id(0), pl.program_id(1), pl.program_id(2))
        k = ids[2]

        @pl.when(k == 0)
        def _():
            acc[...] = jnp.zeros_like(acc)

        def run(a_ref, b_ref):
            a = a_ref[...]
            if a_fn is not None:
                a = a_fn(a)
            b = b_ref[...]
            acc[...] += _dot_raw(a, b, dims)
            if colsum_width:
                cs = refs[-1]
                first = ids[1] == 0

                @pl.when(first & (k == 0))
                def _():
                    cs[...] = jnp.zeros_like(cs)

                @pl.when(first)
                def _():
                    cs[...] += jnp.sum(b.astype(F32), axis=0, keepdims=True)

        if na == 1 and nb == 1:
            run(a_refs[0], b_refs[0])
        elif nb == 1:
            per, fn = a_sel
            which = fn(*ids) // per
            for p in range(na):
                pl.when(which == p)(functools.partial(run, a_refs[p], b_refs[0]))
        else:
            assert na == 1
            per, fn = b_sel
            which = fn(*ids) // per
            for p in range(nb):
                pl.when(which == p)(functools.partial(run, a_refs[0], b_refs[p]))

        @pl.when(k == nk - 1)
        def _():
            store(outs, acc[...], *[e[...] for e in extra])
            if colsum_width:
                @pl.when(ids[1] == 0)
                def _():
                    outs[-1][...] = refs[-1][...]

    scratch = [pltpu.VMEM(acc_shape, F32)]
    if colsum_width:
        scratch.append(pltpu.VMEM((1, colsum_width), F32))
    sem = ("parallel", "arbitrary", "arbitrary") if colsum_width else ("parallel", "parallel", "arbitrary")
    return pl.pallas_call(
        body, name=name, grid=grid, in_specs=[*a_specs, *b_specs, *extra_specs], out_specs=out_specs,
        out_shape=out_shape, scratch_shapes=scratch, compiler_params=_params(sem),
    )(*a_pieces, *b_pieces, *extras)


def _store_plain(outs, acc):
    outs[0][...] = acc.astype(outs[0].dtype)


def _row_spec(tm, tn):
    return pl.BlockSpec((tm, tn), lambda i, j, k: (i, j))


def _vec_spec(tn):
    return pl.BlockSpec((1, tn), lambda i, j, k: (0, j))


def _mm_act(name, a, w, wkind, *, n_out_cols, k_total, tn, tk, a_cb0=0, a_fn=None, extras=(), extra_specs=(),
            store=_store_plain, out_shape=None, out_specs=None):
    a_pieces = list(a) if isinstance(a, (list, tuple)) else [a]
    rows = a_pieces[0].shape[0]
    tm = _tile(rows, ROW_TILE)
    grid = (rows // tm, n_out_cols // tn, k_total // tk)
    a_specs, per = _piece_specs(a_pieces, tm, tk, lambda i, j, k: i, lambda i, j, k: k, a_cb0)
    if wkind == "nat":
        b_spec, dims = pl.BlockSpec((tk, tn), lambda i, j, k: (k, j)), NN
    elif wkind == "stk":
        assert tn == w.shape[2]
        b_spec, dims = pl.BlockSpec((None, tk, tn), lambda i, j, k: (j, k, 0)), NN
    elif wkind == "natT":
        b_spec, dims = pl.BlockSpec((tn, tk), lambda i, j, k: (j, k)), NT
    else:
        assert wkind == "stkT" and tk == w.shape[2]
        b_spec, dims = pl.BlockSpec((None, tn, tk), lambda i, j, k: (k, j, 0)), NT
    if out_shape is None:
        out_shape = [jax.ShapeDtypeStruct((rows, n_out_cols), F32)]
        out_specs = [_row_spec(tm, tn)]
    return _mm_call(name, grid, dims, a_pieces, a_specs, (per, lambda i, j, k: k), [w], [b_spec], None,
                    list(extras), list(extra_specs), out_shape, out_specs, (tm, tn), a_fn, store)


def _mm_wgrad(name, a, g, *, kw, n, tmw, tn, a_cb0=0, a_fn=None, shard_cols=0, out_dtype=F32, colsum=False):
    a_pieces = list(a) if isinstance(a, (list, tuple)) else [a]
    g_pieces = list(g) if isinstance(g, (list, tuple)) else [g]
    rows = a_pieces[0].shape[0]
    tr = _tile(rows, ROW_TILE)
    grid = (n // tn, kw // tmw, rows // tr)
    a_specs, a_per = _piece_specs(a_pieces, tr, tmw, lambda j, i, k: k, lambda j, i, k: i, a_cb0)
    g_specs, g_per = _piece_specs(g_pieces, tr, tn, lambda j, i, k: k, lambda j, i, k: j, 0)
    if shard_cols:
        per = tn // shard_cols
        out_shape = [jax.ShapeDtypeStruct((n // shard_cols, kw, shard_cols), out_dtype)]
        out_specs = [pl.BlockSpec((per, tmw, shard_cols), lambda j, i, k: (j, i, 0))]

        def store(outs, acc):
            for q in range(per):
                outs[0][q] = acc[:, q * shard_cols:(q + 1) * shard_cols].astype(out_dtype)
    else:
        out_shape = [jax.ShapeDtypeStruct((kw, n), out_dtype)]
        out_specs = [pl.BlockSpec((tmw, tn), lambda j, i, k: (i, j))]

        def store(outs, acc):
            outs[0][...] = acc.astype(out_dtype)
    if colsum:
        out_shape.append(jax.ShapeDtypeStruct((1, n), F32))
        out_specs.append(pl.BlockSpec((1, tn), lambda j, i, k: (0, j)))
    res = _mm_call(name, grid, TN, a_pieces, a_specs, (a_per, lambda j, i, k: i), g_pieces, g_specs,
                   (g_per, lambda j, i, k: j), [], [], out_shape, out_specs, (tmw, tn), a_fn, store,
                   colsum_width=tn if colsum else 0)
    return res if colsum else res[0]


def _ln(x, g, b):
    mu = jnp.mean(x, axis=-1, keepdims=True)
    xc = x - mu
    var = jnp.mean(xc * xc, axis=-1, keepdims=True)
    return xc * lax.rsqrt(var + LN_EPS) * g + b


def _relu2(x):
    r = jnp.maximum(x, 0.0)
    return r * r


def _glu(y, gate):
    return y * jax.nn.sigmoid(gate)


def _ln_bwd(name, r, g, b, gy):
    rows, d = r.shape
    tm = _tile(rows, ROW_TILE)

    def body(r_ref, g_ref, b_ref, gy_ref, gr_ref, gg_ref, gb_ref):
        _, vjp = jax.vjp(_ln, r_ref[...], g_ref[...], b_ref[...])
        gr, gg, gb = vjp(gy_ref[...])
        gr_ref[...] = gr

        @pl.when(pl.program_id(0) == 0)
        def _():
            gg_ref[...] = jnp.zeros_like(gg_ref)
            gb_ref[...] = jnp.zeros_like(gb_ref)

        gg_ref[...] += gg
        gb_ref[...] += gb

    row = pl.BlockSpec((tm, d), lambda i: (i, 0))
    vec = pl.BlockSpec((1, d), lambda i: (0, 0))
    return pl.pallas_call(
        body, name=name, grid=(rows // tm,), in_specs=[row, vec, vec, row], out_specs=[row, vec, vec],
        out_shape=[jax.ShapeDtypeStruct((rows, d), F32), jax.ShapeDtypeStruct((1, d), F32),
                   jax.ShapeDtypeStruct((1, d), F32)],
        compiler_params=_params(("arbitrary",)),
    )(r, g, b, gy)


def _rowwise(name, fn, ins, n_out, width):
    rows = ins[0][0].shape[0]
    tm = _tile(rows, ROW_TILE)

    def body(*refs):
        res = fn(*[r[...] for r in refs[:len(ins)]])
        for o, v in zip(refs[len(ins):], res):
            o[...] = v

    return pl.pallas_call(
        body, name=name, grid=(rows // tm,),
        in_specs=[pl.BlockSpec((tm, wd), lambda i, cb=cb: (i, cb)) for _, cb, wd in ins],
        out_specs=[pl.BlockSpec((tm, width), lambda i: (i, 0))] * n_out,
        out_shape=[jax.ShapeDtypeStruct((rows, width), F32)] * n_out, compiler_params=_params(("parallel",)),
    )(*[a for a, _, _ in ins])


def _loss_grad(name, h, target, n_batch, lp, lead):
    rows, d = h.shape
    nq = lp // LANES
    lead_blocks = lead // LANES

    def body(h_ref, t_ref, g_ref, loss_ref):
        i = pl.program_id(1)

        @pl.when((pl.program_id(0) == 0) & (i == 0))
        def _():
            loss_ref[...] = jnp.zeros_like(loss_ref)

        diff = jnp.where(i >= lead_blocks, h_ref[...] - t_ref[...], 0.0)
        g_ref[...] = diff * (1.0 / d)
        loss_ref[...] += 0.5 * jnp.sum(diff * diff) * (1.0 / d)

    return pl.pallas_call(
        body, name=name, grid=(n_batch, nq),
        in_specs=[pl.BlockSpec((LANES, d), lambda b, i: (b * nq + i, 0)),
                  pl.BlockSpec((None, LANES, d), lambda b, i: (b, jnp.maximum(i - lead_blocks, 0), 0))],
        out_specs=[pl.BlockSpec((LANES, d), lambda b, i: (b * nq + i, 0)),
                   pl.BlockSpec((SUBLANES, LANES), lambda b, i: (0, 0))],
        out_shape=[jax.ShapeDtypeStruct((rows, d), F32), jax.ShapeDtypeStruct((SUBLANES, LANES), F32)],
        compiler_params=_params(("arbitrary", "arbitrary")),
    )(h, target)


def _meta_grad(name, g_h0, n_batch, lp, pad, n_meta):
    d = g_h0.shape[1]
    per = lp // n_meta
    at = pad // n_meta

    def body(g_ref, o_ref):
        @pl.when(pl.program_id(0) == 0)
        def _():
            o_ref[...] = jnp.zeros_like(o_ref)

        o_ref[...] += g_ref[...]

    return pl.pallas_call(
        body, name=name, grid=(n_batch,),
        in_specs=[pl.BlockSpec((n_meta, d), lambda b: (b * per + at, 0))],
        out_specs=pl.BlockSpec((n_meta, d), lambda b: (0, 0)),
        out_shape=jax.ShapeDtypeStruct((n_meta, d), F32),
        compiler_params=_params(("arbitrary",)),
    )(g_h0)


def _s5_param_fn(lr, li, ldt, br, bi):
    dt = jnp.exp(ldt)
    e = jnp.exp(lr * dt)
    w = li * dt
    lbr = e * jnp.cos(w)
    lbi = e * jnp.sin(w)
    nr = lbr - 1.0
    den = lr * lr + li * li
    cr = (nr * lr + lbi * li) / den
    ci = (lbi * lr - nr * li) / den
    bbr = cr[:, None, :] * br - ci[:, None, :] * bi
    bbi = cr[:, None, :] * bi + ci[:, None, :] * br
    return lbr, lbi, bbr, bbi


def _s5_params(name, lr, li, ldt, br, bi):
    def body(lr_ref, li_ref, ldt_ref, br_ref, bi_ref, o1, o2, o3, o4):
        res = _s5_param_fn(lr_ref[...], li_ref[...], ldt_ref[...], br_ref[...], bi_ref[...])
        for o, v in zip((o1, o2, o3, o4), res):
            o[...] = v

    shp = [jax.ShapeDtypeStruct(lr.shape, F32)] * 2 + [jax.ShapeDtypeStruct(br.shape, F32)] * 2
    return pl.pallas_call(body, name=name, out_shape=shp)(lr, li, ldt, br, bi)


def _s5_params_bwd(name, lr, li, ldt, br, bi, g_lbr, g_lbi, g_bbr, g_bbi, gd_parts):
    def body(lr_ref, li_ref, ldt_ref, br_ref, bi_ref, g1, g2, g3, g4, gd_ref, o1, o2, o3, o4, o5, o6):
        _, vjp = jax.vjp(_s5_param_fn, lr_ref[...], li_ref[...], ldt_ref[...], br_ref[...], bi_ref[...])
        res = vjp((jnp.sum(g1[...], axis=0), jnp.sum(g2[...], axis=0), g3[...], g4[...]))
        for o, v in zip((o1, o2, o3, o4, o5), res):
            o[...] = v
        o6[...] = jnp.sum(gd_ref[...], axis=0)

    shp = ([jax.ShapeDtypeStruct(lr.shape, F32)] * 2 + [jax.ShapeDtypeStruct(ldt.shape, F32)]
           + [jax.ShapeDtypeStruct(br.shape, F32)] * 2 + [jax.ShapeDtypeStruct(gd_parts.shape[1:], F32)])
    return pl.pallas_call(body, name=name, out_shape=shp)(lr, li, ldt, br, bi, g_lbr, g_lbi, g_bbr, g_bbi, gd_parts)


def _interleave(re, im, w):
    lead = re.shape[:-1]
    nj = re.shape[-1] // w
    return jnp.stack([re.reshape(*lead, nj, w), im.reshape(*lead, nj, w)], axis=-2).reshape(*lead, 2 * nj * w)


def _deinterleave(x, w):
    lead = x.shape[:-1]
    nj = x.shape[-1] // (2 * w)
    y = x.reshape(*lead, nj, 2, w)
    return y[..., 0, :].reshape(*lead, nj * w), y[..., 1, :].reshape(*lead, nj * w)


def _cmul(ar, ai, br, bi):
    return ar * br - ai * bi, ar * bi + ai * br


def _powers(lr, li):
    p = [(lr, li)]
    p.append(_cmul(*p[0], *p[0]))
    p.append(_cmul(*p[1], *p[0]))
    p.append(_cmul(*p[1], *p[1]))
    p.append(_cmul(*p[3], *p[0]))
    p.append(_cmul(*p[3], *p[1]))
    p.append(_cmul(*p[3], *p[2]))
    p.append(_cmul(*p[3], *p[3]))
    return p


def _scan_tile(xr, xi, steps):
    for sh, br, bi, m in steps:
        rr = jnp.where(m, pltpu.roll(xr, sh, 0), 0.0)
        ri = jnp.where(m, pltpu.roll(xi, sh, 0), 0.0)
        xr, xi = xr + (br * rr - bi * ri), xi + (br * ri + bi * rr)
    return xr, xi


def _s5_scan(name, bu, lam, n_batch, lp, w):
    rows, two_ns = bu.shape
    nj = two_ns // (2 * w)
    nt = lp // SUBLANES

    def body(x_ref, lam_ref, s_ref):
        pw = _powers(lam_ref[:, :w], lam_ref[:, w:])
        tab_r = jnp.concatenate([p[0] for p in pw], axis=0)
        tab_i = jnp.concatenate([p[1] for p in pw], axis=0)
        row = lax.broadcasted_iota(jnp.int32, (SUBLANES, w), 0)
        steps = [(s, jnp.broadcast_to(pw[s - 1][0], (SUBLANES, w)), jnp.broadcast_to(pw[s - 1][1], (SUBLANES, w)),
                  row >= s) for s in (1, 2, 4)]

        def tile(t, carry):
            cr, ci = carry
            r0 = pl.multiple_of(t * SUBLANES, SUBLANES)
            x = x_ref[pl.ds(r0, SUBLANES), :]
            xr, xi = _scan_tile(x[:, :w], x[:, w:], steps)
            sr = xr + (tab_r * cr - tab_i * ci)
            si = xi + (tab_r * ci + tab_i * cr)
            s_ref[pl.ds(r0, SUBLANES), :] = jnp.concatenate([sr, si], axis=1)
            return sr[SUBLANES - 1:, :], si[SUBLANES - 1:, :]

        zero = jnp.zeros((1, w), F32)
        lax.fori_loop(0, nt, tile, (zero, zero))

    spec = pl.BlockSpec((lp, 2 * w), lambda b, j: (b, j))
    return pl.pallas_call(
        body, name=name, grid=(n_batch, nj), in_specs=[spec, pl.BlockSpec((1, 2 * w), lambda b, j: (0, j))],
        out_specs=spec, out_shape=jax.ShapeDtypeStruct((rows, two_ns), F32),
        compiler_params=_params(("parallel", "parallel")),
    )(bu, lam)


def _s5_scan_bwd(name, gd, states, lam, n_batch, lp, w):
    rows, two_ns = gd.shape
    nj = two_ns // (2 * w)
    nt = lp // SUBLANES

    def body(x_ref, s_ref, lam_ref, g_ref, gl_ref):
        pw = _powers(lam_ref[:, :w], -lam_ref[:, w:])
        tab_r = jnp.concatenate([p[0] for p in reversed(pw)], axis=0)
        tab_i = jnp.concatenate([p[1] for p in reversed(pw)], axis=0)
        row = lax.broadcasted_iota(jnp.int32, (SUBLANES, w), 0)
        steps = [(SUBLANES - s, jnp.broadcast_to(pw[s - 1][0], (SUBLANES, w)),
                  jnp.broadcast_to(pw[s - 1][1], (SUBLANES, w)), row < SUBLANES - s) for s in (1, 2, 4)]

        def tile(u, carry):
            cr, ci, ar, ai = carry
            t = nt - 1 - u
            r0 = pl.multiple_of(t * SUBLANES, SUBLANES)
            x = x_ref[pl.ds(r0, SUBLANES), :]
            xr, xi = _scan_tile(x[:, :w], x[:, w:], steps)
            gr = xr + (tab_r * cr - tab_i * ci)
            gi = xi + (tab_r * ci + tab_i * cr)
            g_ref[pl.ds(r0, SUBLANES), :] = jnp.concatenate([gr, gi], axis=1)
            p0 = pl.multiple_of(jnp.maximum(t - 1, 0) * SUBLANES, SUBLANES)
            prev = s_ref[pl.ds(p0, SUBLANES), :][SUBLANES - 1:, :] * jnp.where(t > 0, 1.0, 0.0)
            cur = s_ref[pl.ds(r0, SUBLANES), :]
            spr = jnp.where(row >= 1, pltpu.roll(cur[:, :w], 1, 0), prev[:, :w])
            spi = jnp.where(row >= 1, pltpu.roll(cur[:, w:], 1, 0), prev[:, w:])
            return gr[:1, :], gi[:1, :], ar + gr * spr + gi * spi, ai + gi * spr - gr * spi

        z1 = jnp.zeros((1, w), F32)
        z8 = jnp.zeros((SUBLANES, w), F32)
        _, _, ar, ai = lax.fori_loop(0, nt, tile, (z1, z1, z8, z8))
        gl_ref[...] = jnp.concatenate([jnp.sum(ar, axis=0, keepdims=True), jnp.sum(ai, axis=0, keepdims=True)], axis=1)

    spec = pl.BlockSpec((lp, 2 * w), lambda b, j: (b, j))
    return pl.pallas_call(
        body, name=name, grid=(n_batch, nj),
        in_specs=[spec, spec, pl.BlockSpec((1, 2 * w), lambda b, j: (0, j))],
        out_specs=[spec, pl.BlockSpec((None, 1, 2 * w), lambda b, j: (b, 0, j))],
        out_shape=[jax.ShapeDtypeStruct((rows, two_ns), F32), jax.ShapeDtypeStruct((n_batch, 1, two_ns), F32)],
        compiler_params=_params(("parallel", "parallel")),
    )(gd, states, lam)


def _log_sigmoid(z):
    return jnp.minimum(z, 0.0) - jnp.log(1.0 + jnp.exp(-jnp.abs(z)))


def _attn_masks(i, j, pad):
    rowpos = i * LANES + lax.broadcasted_iota(jnp.int32, (LANES, LANES), 0)
    colpos = j * LANES + lax.broadcasted_iota(jnp.int32, (LANES, LANES), 1)
    return (colpos < rowpos) & (colpos >= pad)


ATTN_GROUP = 4


def _tri_ones(strict_upper):
    r = lax.broadcasted_iota(jnp.int32, (LANES, 2 * LANES), 0)
    c = lax.broadcasted_iota(jnp.int32, (LANES, 2 * LANES), 1)
    tri = (r > c) if strict_upper else (r < c)
    return jnp.where((c >= LANES) | tri, 1.0, 0.0).astype(BF16)


def _head_masks():
    lane = lax.broadcasted_iota(jnp.int32, (1, LANES), 1)
    return [lane < SB_HEAD_DIM, lane >= SB_HEAD_DIM]


def _run_groups(n, first, sign, make):
    j, left, g = first, n, ATTN_GROUP
    while g >= 1:
        shift = g.bit_length() - 1
        count = lax.shift_right_logical(left, shift)
        fn = make(g)

        def loop(_, jcur, fn=fn, g=g):
            fn(jcur)
            return jcur + sign * g

        j = lax.fori_loop(0, count, loop, j)
        left = left - lax.shift_left(count, shift)
        g //= 2


def _attn_fwd(name, proj, n_batch, lp, pad, q_cb, k_cb, v_cb, n_pairs):
    rows = proj.shape[0]
    nq = lp // LANES
    scale = SB_HEAD_DIM ** -0.5

    def body(q_ref, k_ref, v_ref, o_ref, acc_s):
        i = pl.program_id(2)
        hm = _head_masks()
        comb = _tri_ones(True)
        qs = q_ref[...] * scale
        qh = [jnp.where(m, qs, 0.0).astype(BF16) for m in hm]
        acc_s[...] = jnp.zeros_like(acc_s)
        o_ref[...] = jnp.zeros_like(o_ref)

        def make(group):
            def fn(jtop):
                pend = []
                for g in range(group):
                    j = jtop - g
                    r0 = pl.multiple_of(j * LANES, LANES)
                    kj = k_ref[pl.ds(r0, LANES), :].astype(BF16)
                    vj = v_ref[pl.ds(r0, LANES), :]
                    vis = _attn_masks(i, j, pad)
                    for h in range(2):
                        z = lax.dot_general(qh[h], kj, NT, preferred_element_type=F32)
                        lsz = _log_sigmoid(z)
                        cr = _dot_split(jnp.where(vis, lsz - z, 0.0), comb, NN)
                        pend.append((h, vis, lsz, cr, jnp.where(hm[h], vj, 0.0).astype(BF16)))
                for h, vis, lsz, cr, vh in pend:
                    acc = acc_s[h]
                    wgt = jnp.where(vis, jnp.exp(lsz + cr[:, :LANES] + acc), 0.0)
                    acc_s[h] = acc + cr[:, LANES:]
                    o_ref[...] += lax.dot_general(wgt.astype(BF16), vh, NN, preferred_element_type=F32)
            return fn

        _run_groups(i + 1, i, -1, make)

    return pl.pallas_call(
        body, name=name, grid=(n_batch, n_pairs, nq),
        in_specs=[pl.BlockSpec((LANES, LANES), lambda b, h, i: (b * nq + i, q_cb + h)),
                  pl.BlockSpec((lp, LANES), lambda b, h, i: (b, k_cb + h)),
                  pl.BlockSpec((lp, LANES), lambda b, h, i: (b, v_cb + h))],
        out_specs=pl.BlockSpec((LANES, LANES), lambda b, h, i: (b * nq + i, h)),
        out_shape=jax.ShapeDtypeStruct((rows, n_pairs * LANES), F32),
        scratch_shapes=[pltpu.VMEM((2, LANES, LANES), F32)],
        compiler_params=_params(("parallel", "parallel", "arbitrary")),
    )(proj, proj, proj)


def _attn_bwd(name, proj, g_out, n_batch, lp, pad, q_cb, k_cb, v_cb, go_cb, n_pairs):
    rows = proj.shape[0]
    nq = lp // LANES
    scale = SB_HEAD_DIM ** -0.5

    def body(q_ref, k_ref, v_ref, go_ref, gq_ref, gk_ref, gv_ref, ga_s, sz_s, acc_s):
        i = pl.program_id(2)

        @pl.when(i == 0)
        def _():
            gk_ref[...] = jnp.zeros_like(gk_ref)
            gv_ref[...] = jnp.zeros_like(gv_ref)

        hm = _head_masks()
        comb_up = _tri_ones(True)
        comb_lo = _tri_ones(False)
        qs = q_ref[...] * scale
        go = go_ref[...]
        qh = [jnp.where(m, qs, 0.0).astype(BF16) for m in hm]
        goh = [jnp.where(m, go, 0.0).astype(BF16) for m in hm]
        acc_s[...] = jnp.zeros_like(acc_s)
        gq_ref[...] = jnp.zeros_like(gq_ref)

        def make_down(group):
            def fn(jtop):
                pend = []
                for g in range(group):
                    j = jtop - g
                    r0 = pl.multiple_of(j * LANES, LANES)
                    kj = k_ref[pl.ds(r0, LANES), :].astype(BF16)
                    vj = v_ref[pl.ds(r0, LANES), :].astype(BF16)
                    vis = _attn_masks(i, j, pad)
                    for h in range(2):
                        z = lax.dot_general(qh[h], kj, NT, preferred_element_type=F32)
                        lsz = _log_sigmoid(z)
                        cr = _dot_split(jnp.where(vis, lsz - z, 0.0), comb_up, NN)
                        gw = lax.dot_general(goh[h], vj, NT, preferred_element_type=F32)
                        pend.append((h, j, r0, vis, lsz, cr, gw))
                for h, j, r0, vis, lsz, cr, gw in pend:
                    acc = acc_s[h]
                    wgt = jnp.where(vis, jnp.exp(lsz + cr[:, :LANES] + acc), 0.0)
                    acc_s[h] = acc + cr[:, LANES:]
                    ga_s[h, j] = gw * wgt
                    sz_s[h, j] = jnp.exp(lsz)
                    gv_ref[pl.ds(r0, LANES), :] += lax.dot_general(wgt.astype(BF16), goh[h], TN, preferred_element_type=F32)
            return fn

        _run_groups(i + 1, i, -1, make_down)
        acc_s[...] = jnp.zeros_like(acc_s)

        def make_up(group):
            def fn(jbot):
                pend = []
                for g in range(group):
                    j = jbot + g
                    r0 = pl.multiple_of(j * LANES, LANES)
                    kj = k_ref[pl.ds(r0, LANES), :]
                    vis = _attn_masks(i, j, pad)
                    for h in range(2):
                        ga = ga_s[h, j]
                        pend.append((h, j, r0, vis, ga, _dot_split(ga, comb_lo, NN), jnp.where(hm[h], kj, 0.0).astype(BF16)))
                for h, j, r0, vis, ga, cr, kh in pend:
                    pre = acc_s[h]
                    glk = cr[:, :LANES] + pre
                    acc_s[h] = pre + cr[:, LANES:]
                    sz = sz_s[h, j]
                    gz = jnp.where(vis, ga * (1.0 - sz) - glk * sz, 0.0).astype(BF16)
                    gq_ref[...] += lax.dot_general(gz, kh, NN, preferred_element_type=F32)
                    gk_ref[pl.ds(r0, LANES), :] += lax.dot_general(gz, qh[h], TN, preferred_element_type=F32)
            return fn

        _run_groups(i + 1, 0, 1, make_up)
        gq_ref[...] = gq_ref[...] * scale

    blk = lambda cb: pl.BlockSpec((LANES, LANES), lambda b, h, i: (b * nq + i, cb + h))
    full = lambda cb: pl.BlockSpec((lp, LANES), lambda b, h, i: (b, cb + h))
    shp = jax.ShapeDtypeStruct((rows, n_pairs * LANES), F32)
    per_block = pltpu.VMEM((2, nq, LANES, LANES), F32)
    return pl.pallas_call(
        body, name=name, grid=(n_batch, n_pairs, nq),
        in_specs=[blk(q_cb), full(k_cb), full(v_cb), blk(go_cb)],
        out_specs=[blk(0), full(0), full(0)], out_shape=[shp, shp, shp],
        scratch_shapes=[per_block, per_block, pltpu.VMEM((2, LANES, LANES), F32)],
        compiler_params=_params(("parallel", "parallel", "arbitrary")),
    )(proj, proj, proj, g_out)


def _lb_fn(gamma):
    g0, g1 = gamma[0:1, :], gamma[1:2, :]
    mx = jnp.maximum(g0, g1)
    e0, e1 = jnp.exp(g0 - mx), jnp.exp(g1 - mx)
    p0, p1 = e0 / (e0 + e1), e1 / (e0 + e1)
    return (p0 + p1) - p0


def _lower_bound(name, gamma):
    def body(g_ref, o_ref):
        o_ref[...] = _lb_fn(g_ref[...])

    return pl.pallas_call(body, name=name, out_shape=jax.ShapeDtypeStruct((1, gamma.shape[1]), F32))(gamma)


def _lower_bound_bwd(name, gamma, g_lb_parts, g_ng_parts):
    def body(g_ref, glb_ref, gng_ref, o_ref, o2_ref):
        _, vjp = jax.vjp(_lb_fn, g_ref[...])
        o_ref[...] = vjp(jnp.sum(glb_ref[...], axis=0))[0]
        o2_ref[...] = jnp.sum(gng_ref[...], axis=0)

    return pl.pallas_call(
        body, name=name,
        out_shape=[jax.ShapeDtypeStruct(gamma.shape, F32), jax.ShapeDtypeStruct((1, gamma.shape[1]), F32)],
    )(gamma, g_lb_parts, g_ng_parts)


def _hg_gates(fc, lb, rowmask, tril):
    f = lb + (1.0 - lb) * jax.nn.sigmoid(fc)
    bcum = jnp.dot(tril, jnp.log(f) * rowmask, preferred_element_type=F32, precision=lax.Precision.HIGHEST)
    return 1.0 - f, bcum


def _hg_state(fc, ic, lb, st, rowmask, tril):
    k, bcum = _hg_gates(fc, lb, rowmask, tril)
    blast = bcum[HG_CHUNK - 1:, :]
    return jnp.exp(blast) * st + _dot(ic * rowmask, k * jnp.exp(blast - bcum), TN)


def _hg_chunk(qc, fc, ic, gc, lb, ng, st, rowmask, tril):
    k, bcum = _hg_gates(fc, lb, rowmask, tril)
    blast = bcum[HG_CHUNK - 1:, :]
    v = ic * rowmask
    qd = qc * jnp.exp(bcum)
    scores = jnp.where(tril > 0.5, _dot(qd, k * jnp.exp(-bcum), NT), 0.0)
    o = _dot(scores, v, NN) + _dot(qd, st, NT)
    st_new = jnp.exp(blast) * st + _dot(v, k * jnp.exp(blast - bcum), TN)
    o = o * lax.rsqrt(jnp.mean(o * o, axis=-1, keepdims=True) + RMS_EPS) * ng
    return o * (gc * jax.nn.sigmoid(gc)), st_new


def _hg_consts(c, pad):
    r = lax.broadcasted_iota(jnp.int32, (HG_CHUNK, HG_CHUNK), 0)
    cc = lax.broadcasted_iota(jnp.int32, (HG_CHUNK, HG_CHUNK), 1)
    tril = jnp.where(r >= cc, 1.0, 0.0).astype(F32)
    pos = c * HG_CHUNK + lax.broadcasted_iota(jnp.int32, (HG_CHUNK, 1), 0)
    return tril, jnp.where(pos >= pad, 1.0, 0.0).astype(F32)


HG_HEADS_PER_STEP = 2
HG_UNROLL = 2


def _hg_head_cols():
    return [slice(h * HG_DK, (h + 1) * HG_DK) for h in range(HG_HEADS_PER_STEP)]


def _hg_specs(lp, n_heads):
    wide = HG_HEADS_PER_STEP * HG_DK
    groups = n_heads // HG_HEADS_PER_STEP
    col = lambda off: pl.BlockSpec((lp, wide), lambda b, h: (b, off * groups + h))
    vec = pl.BlockSpec((1, wide), lambda b, h: (0, h))
    return groups, col, vec


def _hgrn_fwd(name, proj, lb, ng, n_batch, lp, pad, n_heads):
    rows = proj.shape[0]
    nc = lp // HG_CHUNK
    groups, col, vec = _hg_specs(lp, n_heads)

    def body(q_ref, f_ref, i_ref, g_ref, lb_ref, ng_ref, o_ref):
        def chunk(c, sts):
            sl = pl.ds(pl.multiple_of(c * HG_CHUNK, HG_CHUNK), HG_CHUNK)
            tril, rowmask = _hg_consts(c, pad)
            new = []
            for cols, st in zip(_hg_head_cols(), sts):
                o, st = _hg_chunk(q_ref[sl, cols], f_ref[sl, cols], i_ref[sl, cols], g_ref[sl, cols],
                                  lb_ref[:, cols], ng_ref[:, cols], st, rowmask, tril)
                o_ref[sl, cols] = o
                new.append(st)
            return tuple(new)

        zero = jnp.zeros((HG_DK, HG_DK), F32)
        lax.fori_loop(0, nc, chunk, (zero,) * HG_HEADS_PER_STEP, unroll=HG_UNROLL)

    return pl.pallas_call(
        body, name=name, grid=(n_batch, groups), in_specs=[col(0), col(1), col(2), col(3), vec, vec],
        out_specs=col(0), out_shape=jax.ShapeDtypeStruct((rows, n_heads * HG_DK), F32),
        compiler_params=_params(("parallel", "parallel")),
    )(proj, proj, proj, proj, lb, ng)


def _hgrn_bwd(name, proj, lb, ng, g_out, n_batch, lp, pad, n_heads):
    rows = proj.shape[0]
    width = n_heads * HG_DK
    nc = lp // HG_CHUNK
    groups, col, vec = _hg_specs(lp, n_heads)

    def body(q_ref, f_ref, i_ref, g_ref, lb_ref, ng_ref, go_ref, gq_ref, gf_ref, gi_ref, gg_ref, glb_ref, gng_ref, st_s):
        heads = list(enumerate(_hg_head_cols()))

        def fwd(c, sts):
            sl = pl.ds(pl.multiple_of(c * HG_CHUNK, HG_CHUNK), HG_CHUNK)
            tril, rowmask = _hg_consts(c, pad)
            new = []
            for (h, cols), st in zip(heads, sts):
                st_s[h, c] = st
                new.append(_hg_state(f_ref[sl, cols], i_ref[sl, cols], lb_ref[:, cols], st, rowmask, tril))
            return tuple(new)

        zero = jnp.zeros((HG_DK, HG_DK), F32)
        lax.fori_loop(0, nc, fwd, (zero,) * HG_HEADS_PER_STEP, unroll=HG_UNROLL)

        def bwd(u, carry):
            c = nc - 1 - u
            sl = pl.ds(pl.multiple_of(c * HG_CHUNK, HG_CHUNK), HG_CHUNK)
            tril, rowmask = _hg_consts(c, pad)
            fn = functools.partial(_hg_chunk, rowmask=rowmask, tril=tril)
            new = []
            for (h, cols), (gst, glb, gng) in zip(heads, carry):
                _, vjp = jax.vjp(fn, q_ref[sl, cols], f_ref[sl, cols], i_ref[sl, cols], g_ref[sl, cols],
                                 lb_ref[:, cols], ng_ref[:, cols], st_s[h, c])
                gq, gf, gi, gg, dlb, dng, gst = vjp((go_ref[sl, cols], gst))
                gq_ref[sl, cols] = gq
                gf_ref[sl, cols] = gf
                gi_ref[sl, cols] = gi
                gg_ref[sl, cols] = gg
                new.append((gst, glb + dlb, gng + dng))
            return tuple(new)

        zv = jnp.zeros((1, HG_DK), F32)
        res = lax.fori_loop(0, nc, bwd, ((zero, zv, zv),) * HG_HEADS_PER_STEP, unroll=HG_UNROLL)
        for (_, cols), (_, glb, gng) in zip(heads, res):
            glb_ref[:, cols] = glb
            gng_ref[:, cols] = gng

    part = pl.BlockSpec((None, 1, HG_HEADS_PER_STEP * HG_DK), lambda b, h: (b, 0, h))
    big = jax.ShapeDtypeStruct((rows, width), F32)
    small = jax.ShapeDtypeStruct((n_batch, 1, width), F32)
    return pl.pallas_call(
        body, name=name, grid=(n_batch, groups),
        in_specs=[col(0), col(1), col(2), col(3), vec, vec, col(0)],
        out_specs=[col(0), col(0), col(0), col(0), part, part],
        out_shape=[big, big, big, big, small, small],
        scratch_shapes=[pltpu.VMEM((HG_HEADS_PER_STEP, nc, HG_DK, HG_DK), F32)],
        compiler_params=_params(("parallel", "parallel")),
    )(proj, proj, proj, proj, lb, ng, g_out)


def _exchange(name, srcs, scatter):
    nw = len(srcs)

    def body(*refs):
        src, dst = refs[:nw], refs[nw:2 * nw]
        send, recv, loc = refs[2 * nw:]
        x, y, c = lax.axis_index("x"), lax.axis_index("y"), lax.axis_index("c")
        me = 4 * x + 2 * y + c
        started = []
        for w in range(nw):
            own = pltpu.make_async_copy(src[w].at[me] if scatter else src[w], dst[w].at[me], loc.at[w])
            own.start()
            started.append(own)
        for k in range(1, N_DEV):
            px = 1 - x if k & 4 else x
            py = 1 - y if k & 2 else y
            pc = 1 - c if k & 1 else c
            peer = 4 * px + 2 * py + pc
            for w in range(nw):
                cp = pltpu.make_async_remote_copy(
                    src_ref=src[w].at[peer] if scatter else src[w], dst_ref=dst[w].at[me],
                    send_sem=send.at[w, k - 1], recv_sem=recv.at[w, k - 1],
                    device_id=(px, py, pc), device_id_type=pl.DeviceIdType.MESH)
                cp.start()
                started.append(cp)
        for cp in started:
            cp.wait()

    any_spec = pl.BlockSpec(memory_space=pl.ANY)
    out_shape = [jax.ShapeDtypeStruct(s.shape if scatter else (N_DEV,) + s.shape, s.dtype) for s in srcs]
    return pl.pallas_call(
        body, name=name, in_specs=[any_spec] * nw, out_specs=[any_spec] * nw, out_shape=out_shape,
        scratch_shapes=[pltpu.SemaphoreType.DMA((nw, N_DEV - 1)), pltpu.SemaphoreType.DMA((nw, N_DEV - 1)),
                        pltpu.SemaphoreType.DMA((nw,))],
    )(*srcs)


def _adamw(w, g, m, v):
    m = ADAM_B1 * m + (1.0 - ADAM_B1) * g
    v = ADAM_B2 * v + (1.0 - ADAM_B2) * (g * g)
    m_hat = m / (1.0 - ADAM_B1 ** ADAM_STEP)
    v_hat = v / (1.0 - ADAM_B2 ** ADAM_STEP)
    delta = -ADAM_LR * (m_hat / (jnp.sqrt(v_hat) + ADAM_EPS) + ADAM_WD * w)
    return delta, m, v


def _adamw_summed(name, parts, w, m, v):
    rows, cols = w.shape
    n_parts = parts.shape[0]
    tr = _tile(rows, max(SUBLANES, (1 << 18) // cols))

    def body(p_ref, w_ref, m_ref, v_ref, g_ref, d_ref, nm_ref, nv_ref):
        g = p_ref[0].astype(F32)
        for s in range(1, n_parts):
            g = g + p_ref[s].astype(F32)
        d, nm, nv = _adamw(w_ref[...], g, m_ref[...], v_ref[...])
        g_ref[...] = g
        d_ref[...] = d
        nm_ref[...] = nm
        nv_ref[...] = nv

    spec = pl.BlockSpec((tr, cols), lambda i: (i, 0))
    shp = jax.ShapeDtypeStruct((rows, cols), F32)
    return pl.pallas_call(
        body, name=name, grid=(rows // tr,),
        in_specs=[pl.BlockSpec((n_parts, tr, cols), lambda i: (0, i, 0)), spec, spec, spec],
        out_specs=[spec] * 4, out_shape=[shp] * 4, compiler_params=_params(("parallel",)),
    )(parts, w, m, v)


def _pack_rows(arrays, cols):
    out = []
    for a in arrays:
        flat = a.reshape(-1)
        n = -(-flat.shape[0] // cols) * cols
        out.append(jnp.pad(flat, (0, n - flat.shape[0])).reshape(-1, cols))
    packed = jnp.concatenate(out, axis=0)
    return jnp.pad(packed, ((0, -packed.shape[0] % SUBLANES), (0, 0)))


def _unpack_rows(packed, shapes, cols):
    out, r = [], 0
    for s in shapes:
        n = math.prod(s)
        nr = -(-n // cols)
        out.append(packed[r:r + nr].reshape(-1)[:n].reshape(s))
        r += nr
    return out


def _block_diag(blocks):
    g, a, b = blocks.shape
    eye = jnp.eye(g, dtype=blocks.dtype)
    return (eye[:, None, :, None] * blocks[:, :, None, :]).reshape(g * a, g * b)


def _diag_blocks(dense, g):
    a, b = dense.shape[0] // g, dense.shape[1] // g
    return jnp.einsum("gagb->gab", dense.reshape(g, a, g, b))


def _local_step(x, target, meta, wts, small):
    n_batch, seq, d = x.shape
    n_meta = meta.shape[0]
    pad = -(seq + n_meta) % LANES
    lead = pad + n_meta
    lp = lead + seq
    rows = n_batch * lp
    s5w = wts["glu"].shape[0]
    n_ab = wts["in_ab"].shape[2]
    n_c = wts["in_c"].shape[2]
    n_up = wts["up"][0].shape[2]
    ab_cols = wts["in_ab"].shape[0] * n_ab
    sbw = (ab_cols - s5w) // 3
    dff = wts["up"][0].shape[0] * n_up
    n_pairs = sbw // LANES
    n_hg = d // HG_DK
    s5_cb = s5w // LANES
    sb_cb = sbw // LANES
    tm = _tile(rows, ROW_TILE)
    groups, n_state, grp = small["s5_b_re"].shape[1:]
    ns = groups * n_state
    sw = min(SCAN_LANES, ns)

    h0 = jnp.concatenate(
        [jnp.zeros((n_batch, pad, d), F32), jnp.broadcast_to(meta[None], (n_batch, n_meta, d)), x], axis=1
    ).reshape(rows, d)

    lam_re, lam_im = small["s5_lam_re"][0], small["s5_lam_im"][0]
    log_dt = small["s5_log_dt"][0][:, None]
    b_re_t = small["s5_b_re"][0].transpose(0, 2, 1)
    b_im_t = small["s5_b_im"][0].transpose(0, 2, 1)
    c_re, c_im = small["s5_c_re"][0], small["s5_c_im"][0]
    lbr, lbi, bbr, bbi = _s5_params("s5_params", lam_re, lam_im, log_dt, b_re_t, b_im_t)
    b_blk = _interleave(_block_diag(bbr), _block_diag(bbi), sw).astype(BF16)
    c_blk = _interleave(_block_diag(c_re), _block_diag(-c_im), sw).T.astype(BF16)
    lam_row = _interleave(lbr.reshape(1, ns), lbi.reshape(1, ns), sw)
    d_row = small["s5_d"].reshape(1, s5w)

    def ln_store(outs, acc, res, bias, g, b):
        r = ALPHA * res + acc + bias
        outs[0][...] = r
        outs[1][...] = _ln(r, g, b)

    zero_bias = jnp.zeros((1, d), F32)

    def mix_ln(name, a, w, k_total, tk, res, bias, g, b, a_fn=None):
        return _mm_act(name, a, w, "nat", n_out_cols=d, k_total=k_total, tn=d, tk=tk, a_fn=a_fn,
                       extras=(res, bias, g, b), extra_specs=(_row_spec(tm, d), _vec_spec(d), _vec_spec(d), _vec_spec(d)),
                       store=ln_store, out_shape=[jax.ShapeDtypeStruct((rows, d), F32)] * 2,
                       out_specs=[_row_spec(tm, d)] * 2)

    def two(width):
        return [jax.ShapeDtypeStruct((rows, width), F32)] * 2, [_row_spec(tm, width)] * 2

    proj_ab = _mm_act("in_ab", h0, wts["in_ab"], "stk", n_out_cols=ab_cols, k_total=d, tn=n_ab, tk=d)[0]
    bu = _mm_act("s5_bu", proj_ab, b_blk, "nat", n_out_cols=2 * ns, k_total=s5w, tn=min(2 * ns, 2048), tk=s5w)[0]
    states = _s5_scan("s5_scan", bu, lam_row, n_batch, lp, sw)

    def gelu_store(outs, acc, u, dv):
        ypre = acc + dv * u
        outs[0][...] = ypre
        outs[1][...] = jax.nn.gelu(ypre)

    shp2, spec2 = two(s5w)
    ypre, y = _mm_act(
        "s5_y", states, c_blk, "nat", n_out_cols=s5w, k_total=2 * ns, tn=s5w, tk=min(2 * ns, 1024),
        extras=(proj_ab, d_row), extra_specs=(_row_spec(tm, s5w), _vec_spec(s5w)), store=gelu_store,
        out_shape=shp2, out_specs=spec2)

    def glu_store(outs, acc, yv, bias):
        gate = acc + bias
        outs[0][...] = gate
        outs[1][...] = _glu(yv, gate)

    gate, a_out = _mm_act(
        "s5_glu", y, wts["glu"], "nat", n_out_cols=s5w, k_total=s5w, tn=s5w, tk=s5w,
        extras=(y, small["s5_b_glu"]), extra_specs=(_row_spec(tm, s5w), _vec_spec(s5w)), store=glu_store,
        out_shape=shp2, out_specs=spec2)
    b_out = _attn_fwd("sb_attn", proj_ab, n_batch, lp, pad, s5_cb, s5_cb + sb_cb, s5_cb + 2 * sb_cb, n_pairs)

    def bias_store(outs, acc, bias):
        outs[0][...] = acc + bias

    def mlp_fwd(layer, h_in):
        up = _mm_act(f"up{layer}", h_in, wts["up"][layer], "stk", n_out_cols=dff, k_total=d, tn=n_up, tk=d,
                     extras=(small["mlp_b_up"][layer:layer + 1],), extra_specs=(_vec_spec(n_up),), store=bias_store)[0]
        r, h = mix_ln(f"down{layer}", up, wts["down"][layer], dff, min(dff, 1024), h_in,
                      small["mlp_b_down"][layer:layer + 1], small["ln_mlp_g"][layer:layer + 1],
                      small["ln_mlp_b"][layer:layer + 1], a_fn=_relu2)
        return up, r, h

    r1, h1 = mix_ln("out_ab", [a_out, b_out], wts["out_ab"], s5w + sbw, min(s5w, sbw), h0, zero_bias,
                    small["ln_mix_g"][0:1], small["ln_mix_b"][0:1])
    up0, r2, h2 = mlp_fwd(0, h1)

    lb = _lower_bound("hg_lb", small["hgrn_gamma"])
    proj_c = _mm_act("in_c", h2, wts["in_c"], "stk", n_out_cols=4 * d, k_total=d, tn=n_c, tk=d)[0]
    c_out = _hgrn_fwd("hgrn", proj_c, lb, wts["ng"], n_batch, lp, pad, n_hg)
    r3, h3 = mix_ln("out_c", c_out, wts["out_c"], d, d, h2, zero_bias, small["ln_mix_g"][1:2], small["ln_mix_b"][1:2])
    up1, r4, h4 = mlp_fwd(1, h3)

    g_h4, loss_tile = _loss_grad("loss", h4, target, n_batch, lp, lead)

    gr = {}

    def res_store(outs, acc, g_res):
        outs[0][...] = acc + ALPHA * g_res

    def mlp_bwd(layer, g_h_out, r_out, up, h_in):
        g_r, gr[f"ln_mlp_g{layer}"], gr[f"ln_mlp_b{layer}"] = _ln_bwd(
            f"ln_mlp_bwd{layer}", r_out, small["ln_mlp_g"][layer:layer + 1], small["ln_mlp_b"][layer:layer + 1], g_h_out)

        def gup_store(outs, acc, upv):
            outs[0][...] = acc * (2.0 * jnp.maximum(upv, 0.0))

        tf = min(dff, 1024)
        g_up = _mm_act(f"g_up{layer}", g_r, wts["down"][layer], "natT", n_out_cols=dff, k_total=d, tn=tf, tk=d,
                       extras=(up,), extra_specs=(_row_spec(tm, tf),), store=gup_store)[0]
        gr[f"down{layer}"], gr[f"mlp_b_down{layer}"] = _mm_wgrad(
            f"dw_down{layer}", up, g_r, kw=dff, n=d, tmw=tf, tn=d, a_fn=_relu2, out_dtype=BF16, colsum=True)
        gr[f"up{layer}"], gr[f"mlp_b_up{layer}"] = _mm_wgrad(
            f"dw_up{layer}", h_in, g_up, kw=d, n=dff, tmw=d, tn=min(dff, 2048), shard_cols=n_up, out_dtype=BF16, colsum=True)
        return _mm_act(f"g_hmid{layer}", g_up, wts["up"][layer], "stkT", n_out_cols=d, k_total=dff, tn=d, tk=n_up,
                       extras=(g_r,), extra_specs=(_row_spec(tm, d),), store=res_store)[0]

    g_h3 = mlp_bwd(1, g_h4, r4, up1, h3)
    g_r3, gr["ln_mix_g1"], gr["ln_mix_b1"] = _ln_bwd("ln_mix_bwd1", r3, small["ln_mix_g"][1:2], small["ln_mix_b"][1:2], g_h3)
    g_cout = _mm_act("g_cout", g_r3, wts["out_c"], "natT", n_out_cols=d, k_total=d, tn=d, tk=d)[0]
    gr["out_c"] = _mm_wgrad("dw_out_c", c_out, g_r3, kw=d, n=d, tmw=d, tn=d, out_dtype=BF16)
    gq, gf, gi, gg_, g_lb_parts, g_ng_parts = _hgrn_bwd("hgrn_bwd", proj_c, lb, wts["ng"], g_cout, n_batch, lp, pad, n_hg)
    g_pc = [gq, gf, gi, gg_]
    gr["hgrn_gamma"], gr["ng"] = _lower_bound_bwd("hg_lb_bwd", small["hgrn_gamma"], g_lb_parts, g_ng_parts)
    gr["in_c"] = _mm_wgrad("dw_in_c", h2, g_pc, kw=d, n=4 * d, tmw=d, tn=d, shard_cols=n_c, out_dtype=BF16)
    g_h2 = _mm_act("g_h2", g_pc, wts["in_c"], "stkT", n_out_cols=d, k_total=4 * d, tn=d, tk=n_c,
                   extras=(g_r3,), extra_specs=(_row_spec(tm, d),), store=res_store)[0]

    g_h1 = mlp_bwd(0, g_h2, r2, up0, h1)
    g_r1, gr["ln_mix_g0"], gr["ln_mix_b0"] = _ln_bwd("ln_mix_bwd0", r1, small["ln_mix_g"][0:1], small["ln_mix_b"][0:1], g_h1)
    g_cat = _mm_act("g_cat", g_r1, wts["out_ab"], "natT", n_out_cols=d, k_total=d, tn=d, tk=d)[0]
    gr["out_ab"] = _mm_wgrad("dw_out_ab", [a_out, b_out], g_r1, kw=s5w + sbw, n=d, tmw=min(s5w, sbw), tn=d, out_dtype=BF16)
    g_q, g_k, g_v = _attn_bwd("sb_attn_bwd", proj_ab, g_cat, n_batch, lp, pad, s5_cb, s5_cb + sb_cb, s5_cb + 2 * sb_cb,
                              s5_cb, n_pairs)

    g_y_direct, g_gate = _rowwise("s5_glu_bwd", lambda ga, yv, gt: jax.vjp(_glu, yv, gt)[1](ga),
                                  [(g_cat, 0, s5w), (y, 0, s5w), (gate, 0, s5w)], 2, s5w)

    def gelu_bwd_store(outs, acc, gyd, yp, u, dv):
        gyp = jax.vjp(jax.nn.gelu, yp)[1](acc + gyd)[0]
        outs[0][...] = gyp
        outs[1][...] = dv * gyp
        outs[2][...] = jnp.sum(gyp * u, axis=0, keepdims=True)

    rs = _row_spec(tm, s5w)
    g_ypre, g_u_direct, gd_parts = _mm_act(
        "s5_g_y", g_gate, wts["glu"], "natT", n_out_cols=s5w, k_total=s5w, tn=s5w, tk=s5w,
        extras=(g_y_direct, ypre, proj_ab, d_row), extra_specs=(rs, rs, rs, _vec_spec(s5w)), store=gelu_bwd_store,
        out_shape=[jax.ShapeDtypeStruct((rows, s5w), F32)] * 2 + [jax.ShapeDtypeStruct((rows // tm, 1, s5w), F32)],
        out_specs=[rs, rs, pl.BlockSpec((None, 1, s5w), lambda i, j, k: (i, 0, j))])
    gr["glu"], gr["s5_b_glu"] = _mm_wgrad("dw_glu", y, g_gate, kw=s5w, n=s5w, tmw=s5w, tn=s5w, out_dtype=BF16, colsum=True)
    g_sd = _mm_act("s5_g_states", g_ypre, c_blk, "natT", n_out_cols=2 * ns, k_total=s5w, tn=min(2 * ns, 2048), tk=s5w)[0]
    d_cblk = _mm_wgrad("dw_cblk", states, g_ypre, kw=2 * ns, n=s5w, tmw=min(2 * ns, 1024), tn=s5w)
    gs, gl_parts = _s5_scan_bwd("s5_scan_bwd", g_sd, states, lam_row, n_batch, lp, sw)

    def add_store(outs, acc, other):
        outs[0][...] = acc + other

    g_u = _mm_act("s5_g_u", gs, b_blk, "natT", n_out_cols=s5w, k_total=2 * ns, tn=s5w, tk=min(2 * ns, 1024),
                  extras=(g_u_direct,), extra_specs=(rs,), store=add_store)[0]
    d_bblk = _mm_wgrad("dw_bblk", proj_ab, gs, kw=s5w, n=2 * ns, tmw=s5w, tn=min(2 * ns, 2048))
    db_re, db_im = _deinterleave(d_bblk, sw)
    dc_re, dc_im = _deinterleave(d_cblk.T, sw)
    glr, gli = _deinterleave(gl_parts, sw)
    g_lam_re, g_lam_im, g_log_dt, g_b_re_t, g_b_im_t, g_d = _s5_params_bwd(
        "s5_params_bwd", lam_re, lam_im, log_dt, b_re_t, b_im_t,
        glr.reshape(n_batch, groups, n_state), gli.reshape(n_batch, groups, n_state),
        _diag_blocks(db_re, groups), _diag_blocks(db_im, groups), gd_parts)

    g_pab = [g_u, g_q, g_k, g_v]
    assert s5w == sbw
    gr["in_ab"] = _mm_wgrad("dw_in_ab", h0, g_pab, kw=d, n=ab_cols, tmw=d, tn=s5w, shard_cols=n_ab, out_dtype=BF16)
    g_h0 = _mm_act("g_h0", g_pab, wts["in_ab"], "stkT", n_out_cols=d, k_total=ab_cols, tn=d, tk=n_ab,
                   extras=(g_r1,), extra_specs=(_row_spec(tm, d),), store=res_store)[0]
    grad_x = g_h0.reshape(n_batch, lp, d)[:, lead:, :]
    g_meta = _meta_grad("g_meta", g_h0, n_batch, lp, pad, n_meta)

    cat2 = lambda key: jnp.concatenate([gr[key + "0"], gr[key + "1"]], axis=0)
    small_grads = {
        "s5_lam_re": g_lam_re[None], "s5_lam_im": g_lam_im[None], "s5_log_dt": g_log_dt.reshape(1, groups),
        "s5_b_re": g_b_re_t.transpose(0, 2, 1)[None], "s5_b_im": g_b_im_t.transpose(0, 2, 1)[None],
        "s5_c_re": _diag_blocks(dc_re, groups)[None], "s5_c_im": -_diag_blocks(dc_im, groups)[None],
        "s5_d": g_d.reshape(1, groups, grp), "s5_b_glu": gr["s5_b_glu"], "hgrn_gamma": gr["hgrn_gamma"],
        "ln_mix_g": cat2("ln_mix_g"), "ln_mix_b": cat2("ln_mix_b"), "mlp_b_up": cat2("mlp_b_up"),
        "mlp_b_down": cat2("mlp_b_down"), "ln_mlp_g": cat2("ln_mlp_g"), "ln_mlp_b": cat2("ln_mlp_b"),
    }
    big_grads = {
        "meta": g_meta, "in_ab": gr["in_ab"], "glu": gr["glu"], "out_ab": gr["out_ab"], "in_c": gr["in_c"],
        "ng": gr["ng"], "out_c": gr["out_c"], "up": [gr["up0"], gr["up1"]], "down": [gr["down0"], gr["down1"]],
    }
    return loss_tile, grad_x, big_grads, small_grads


SMALL_NAMES = ("s5_lam_re", "s5_lam_im", "s5_log_dt", "s5_b_re", "s5_b_im", "s5_c_re", "s5_c_im", "s5_d", "s5_b_glu",
               "hgrn_gamma", "ln_mix_g", "ln_mix_b", "mlp_b_up", "mlp_b_down", "ln_mlp_g", "ln_mlp_b")
WEIGHT_ORDER = ("meta", "w_in_ab", "s5_lam_re", "s5_lam_im", "s5_log_dt", "s5_b_re", "s5_b_im", "s5_c_re", "s5_c_im",
                "s5_d", "s5_w_glu", "s5_b_glu", "w_out_ab", "w_in_c", "hgrn_gamma", "hgrn_norm_g", "w_out_c", "ln_mix_g",
                "ln_mix_b", "mlp_w_up", "mlp_b_up", "mlp_w_down", "mlp_b_down", "ln_mlp_g", "ln_mlp_b")


def kernel(x, meta, w_in_ab, s5_lam_re, s5_lam_im, s5_log_dt, s5_b_re, s5_b_im, s5_c_re, s5_c_im, s5_d, s5_w_glu, s5_b_glu, w_out_ab, w_in_c, hgrn_gamma, hgrn_norm_g, w_out_c, ln_mix_g, ln_mix_b, mlp_w_up, mlp_b_up, mlp_w_down, mlp_b_down, ln_mlp_g, ln_mlp_b, loss_target, m_meta, m_w_in_ab, m_s5_lam_re, m_s5_lam_im, m_s5_log_dt, m_s5_b_re, m_s5_b_im, m_s5_c_re, m_s5_c_im, m_s5_d, m_s5_w_glu, m_s5_b_glu, m_w_out_ab, m_w_in_c, m_hgrn_gamma, m_hgrn_norm_g, m_w_out_c, m_ln_mix_g, m_ln_mix_b, m_mlp_w_up, m_mlp_b_up, m_mlp_w_down, m_mlp_b_down, m_ln_mlp_g, m_ln_mlp_b, v_meta, v_w_in_ab, v_s5_lam_re, v_s5_lam_im, v_s5_log_dt, v_s5_b_re, v_s5_b_im, v_s5_c_re, v_s5_c_im, v_s5_d, v_s5_w_glu, v_s5_b_glu, v_w_out_ab, v_w_in_c, v_hgrn_gamma, v_hgrn_norm_g, v_w_out_c, v_ln_mix_g, v_ln_mix_b, v_mlp_w_up, v_mlp_b_up, v_mlp_w_down, v_mlp_b_down, v_ln_mlp_g, v_ln_mlp_b):
    args = dict(locals())
    w = {n: args[n] for n in WEIGHT_ORDER}
    mom = {n: args["m_" + n] for n in WEIGHT_ORDER}
    var = {n: args["v_" + n] for n in WEIGHT_ORDER}
    d = x.shape[2]
    n_meta = meta.shape[0]

    shards = [
        w["meta"], w["hgrn_norm_g"], w["w_in_ab"][0].astype(BF16), w["s5_w_glu"][0].astype(BF16),
        w["w_out_ab"][0].astype(BF16), w["w_in_c"][0].astype(BF16), w["w_out_c"][0].astype(BF16),
        w["mlp_w_up"][0].astype(BF16), w["mlp_w_up"][1].astype(BF16), w["mlp_w_down"][0].astype(BF16),
        w["mlp_w_down"][1].astype(BF16),
    ]
    a_meta, a_ng, a_in_ab, a_glu, a_out_ab, a_in_c, a_out_c, a_up0, a_up1, a_dn0, a_dn1 = _exchange(
        "gather_weights", shards, False)
    wts = {
        "in_ab": a_in_ab, "glu": a_glu.reshape(-1, a_glu.shape[2]), "out_ab": a_out_ab.reshape(-1, d), "in_c": a_in_c,
        "ng": a_ng.transpose(1, 0, 2).reshape(1, d), "out_c": a_out_c.reshape(-1, d), "up": [a_up0, a_up1],
        "down": [a_dn0.reshape(-1, d), a_dn1.reshape(-1, d)],
    }
    meta_full = a_meta.transpose(1, 0, 2).reshape(n_meta, d)
    small = {n: w[n] for n in SMALL_NAMES}

    loss_tile, grad_x, big, sg = _local_step(x, loss_target, meta_full, wts, small)

    n_loc = d // N_DEV
    parts = [
        big["meta"].reshape(n_meta, N_DEV, n_loc).transpose(1, 0, 2),
        big["ng"].reshape(1, N_DEV, n_loc).transpose(1, 0, 2),
        big["in_ab"], big["glu"].reshape(N_DEV, -1, big["glu"].shape[1]), big["out_ab"].reshape(N_DEV, -1, d),
        big["in_c"], big["out_c"].reshape(N_DEV, -1, d), big["up"][0], big["up"][1],
        big["down"][0].reshape(N_DEV, -1, d), big["down"][1].reshape(N_DEV, -1, d),
    ]
    recv = _exchange("scatter_grads", parts, True)
    names = ("meta", "hgrn_norm_g", "w_in_ab", "s5_w_glu", "w_out_ab", "w_in_c", "w_out_c", "mlp_w_up", "mlp_w_up",
             "mlp_w_down", "mlp_w_down")
    layer_of = (None, None, 0, 0, 0, 0, 0, 0, 1, 0, 1)
    res = {}
    for idx, (nm, ly, rc) in enumerate(zip(names, layer_of, recv)):
        sel = (lambda t: t) if ly is None else (lambda t, ly=ly: t[ly])
        res.setdefault(nm, []).append(_adamw_summed(f"adamw_{idx}", rc, sel(w[nm]), sel(mom[nm]), sel(var[nm])))

    shapes = [w[n].shape for n in SMALL_NAMES] + [loss_tile.shape]
    zeros = jnp.zeros_like(loss_tile)
    g_pack = _pack_rows([sg[n] for n in SMALL_NAMES] + [loss_tile], PACK_COLS)
    w_pack = _pack_rows([w[n] for n in SMALL_NAMES] + [zeros], PACK_COLS)
    m_pack = _pack_rows([mom[n] for n in SMALL_NAMES] + [zeros], PACK_COLS)
    v_pack = _pack_rows([var[n] for n in SMALL_NAMES] + [zeros], PACK_COLS)
    g_all = _exchange("gather_small", [g_pack], False)[0]
    packed = _adamw_summed("adamw_small", g_all, w_pack, m_pack, v_pack)
    unpacked = [_unpack_rows(p, shapes, PACK_COLS) for p in packed]
    loss = unpacked[0][-1][0, 0]

    def pick(nm, which):
        if nm in SMALL_NAMES:
            return unpacked[which][SMALL_NAMES.index(nm)]
        outs = [o[which] for o in res[nm]]
        return outs[0] if nm in ("meta", "hgrn_norm_g") else jnp.stack(outs, axis=0)

    return (loss, grad_x, *[pick(n, 0) for n in WEIGHT_ORDER], *[pick(n, 1) for n in WEIGHT_ORDER],
            *[pick(n, 2) for n in WEIGHT_ORDER], *[pick(n, 3) for n in WEIGHT_ORDER])
```

```python
import functools
import math

import jax
import jax.numpy as jnp
from jax import lax
from jax.experimental import pallas as pl
from jax.experimental.pallas import tpu as pltpu

F32 = jnp.float32
BF16 = jnp.bfloat16

N_DEV = 8
DEPTH = 2
ALPHA = (2.0 * DEPTH) ** 0.25
LN_EPS = 1e-5
RMS_EPS = 1e-6
SB_HEAD_DIM = 64
HG_DK = 128
HG_CHUNK = 64
LANES = 128
SUBLANES = 8
VMEM_LIMIT_BYTES = 56 * 1024 * 1024
ROW_TILE = 544
SCAN_LANES = 256
PACK_COLS = 1024

ADAM_LR = 0.001
ADAM_B1 = 0.9
ADAM_B2 = 0.999
ADAM_EPS = 1e-08
ADAM_WD = 0.01
ADAM_STEP = 10

NN = (((1,), (0,)), ((), ()))
NT = (((1,), (1,)), ((), ()))
TN = (((0,), (0,)), ((), ()))


def _tile(n, pref, align=SUBLANES):
    t = min(n, pref)
    t -= t % align
    while t >= align:
        if n % t == 0:
            return t
        t -= align
    return n


def _params(sem):
    return pltpu.CompilerParams(dimension_semantics=sem, vmem_limit_bytes=VMEM_LIMIT_BYTES)


def _dot_raw(a, b, dims):
    return lax.dot_general(a.astype(BF16), b.astype(BF16), dims, preferred_element_type=F32)


def _make_dot(dims, da_rule, db_rule):
    @jax.custom_vjp
    def f(a, b):
        return _dot_raw(a, b, dims)

    def fwd(a, b):
        return _dot_raw(a, b, dims), (a, b)

    def bwd(res, g):
        a, b = res
        return da_rule(g, a, b), db_rule(g, a, b)

    f.defvjp(fwd, bwd)
    return f


_DOTS = {
    NN: _make_dot(NN, lambda g, a, b: _dot_raw(g, b, NT), lambda g, a, b: _dot_raw(a, g, TN)),
    NT: _make_dot(NT, lambda g, a, b: _dot_raw(g, b, NN), lambda g, a, b: _dot_raw(g, a, TN)),
    TN: _make_dot(TN, lambda g, a, b: _dot_raw(b, g, NT), lambda g, a, b: _dot_raw(a, g, NN)),
}


def _dot(a, b, dims):
    return _DOTS[dims](a, b)


def _dot_split(a, b, dims):
    hi = a.astype(BF16)
    lo = (a - hi.astype(F32)).astype(BF16)
    return (lax.dot_general(hi, b, dims, preferred_element_type=F32)
            + lax.dot_general(lo, b, dims, preferred_element_type=F32))


def _piece_specs(pieces, block_rows, block_cols, row_of, col_of, cb0):
    per = pieces[0].shape[1] // block_cols if len(pieces) > 1 else None
    specs = []
    for p in range(len(pieces)):
        if per is None:
            specs.append(pl.BlockSpec((block_rows, block_cols), lambda *g: (row_of(*g), cb0 + col_of(*g))))
        else:
            specs.append(pl.BlockSpec(
                (block_rows, block_cols),
                lambda *g, p=p: (row_of(*g), jnp.clip(col_of(*g) - p * per, 0, per - 1))))
    return specs, per


def _mm_call(name, grid, dims, a_pieces, a_specs, a_sel, b_pieces, b_specs, b_sel, extras, extra_specs,
             out_shape, out_specs, acc_shape, a_fn, store, colsum_width=0):
    na, nb, ne, no = len(a_pieces), len(b_pieces), len(extras), len(out_shape)
    nk = grid[2]

    def body(*refs):
        a_refs, b_refs = refs[:na], refs[na:na + nb]
        extra = refs[na + nb:na + nb + ne]
        outs = refs[na + nb + ne:na + nb + ne + no]
        acc = refs[na + nb + ne + no]
        ids = (pl.program_id(0), pl.program_id(1), pl.program_id(2))
        k = ids[2]

        @pl.when(k == 0)
        def _():
            acc[...] = jnp.zeros_like(acc)

        def run(a_ref, b_ref):
            a = a_ref[...]
            if a_fn is not None:
                a = a_fn(a)
            b = b_ref[...]
            acc[...] += _dot_raw(a, b, dims)
            if colsum_width:
                cs = refs[-1]
                first = ids[1] == 0

                @pl.when(first & (k == 0))
                def _():
                    cs[...] = jnp.zeros_like(cs)

                @pl.when(first)
                def _():
                    cs[...] += jnp.sum(b.astype(F32), axis=0, keepdims=True)

        if na == 1 and nb == 1:
            run(a_refs[0], b_refs[0])
        elif nb == 1:
            per, fn = a_sel
            which = fn(*ids) // per
            for p in range(na):
                pl.when(which == p)(functools.partial(run, a_refs[p], b_refs[0]))
        else:
            assert na == 1
            per, fn = b_sel
            which = fn(*ids) // per
            for p in range(nb):
                pl.when(which == p)(functools.partial(run, a_refs[0], b_refs[p]))

        @pl.when(k == nk - 1)
        def _():
            store(outs, acc[...], *[e[...] for e in extra])
            if colsum_width:
                @pl.when(ids[1] == 0)
                def _():
                    outs[-1][...] = refs[-1][...]

    scratch = [pltpu.VMEM(acc_shape, F32)]
    if colsum_width:
        scratch.append(pltpu.VMEM((1, colsum_width), F32))
    sem = ("parallel", "arbitrary", "arbitrary") if colsum_width else ("parallel", "parallel", "arbitrary")
    return pl.pallas_call(
        body, name=name, grid=grid, in_specs=[*a_specs, *b_specs, *extra_specs], out_specs=out_specs,
        out_shape=out_shape, scratch_shapes=scratch, compiler_params=_params(sem),
    )(*a_pieces, *b_pieces, *extras)


def _store_plain(outs, acc):
    outs[0][...] = acc.astype(outs[0].dtype)


def _row_spec(tm, tn):
    return pl.BlockSpec((tm, tn), lambda i, j, k: (i, j))


def _vec_spec(tn):
    return pl.BlockSpec((1, tn), lambda i, j, k: (0, j))


def _mm_act(name, a, w, wkind, *, n_out_cols, k_total, tn, tk, a_cb0=0, a_fn=None, extras=(), extra_specs=(),
            store=_store_plain, out_shape=None, out_specs=None):
    a_pieces = list(a) if isinstance(a, (list, tuple)) else [a]
    rows = a_pieces[0].shape[0]
    tm = _tile(rows, ROW_TILE)
    grid = (rows // tm, n_out_cols // tn, k_total // tk)
    a_specs, per = _piece_specs(a_pieces, tm, tk, lambda i, j, k: i, lambda i, j, k: k, a_cb0)
    if wkind == "nat":
        b_spec, dims = pl.BlockSpec((tk, tn), lambda i, j, k: (k, j)), NN
    elif wkind == "stk":
        assert tn == w.shape[2]
        b_spec, dims = pl.BlockSpec((None, tk, tn), lambda i, j, k: (j, k, 0)), NN
    elif wkind == "natT":
        b_spec, dims = pl.BlockSpec((tn, tk), lambda i, j, k: (j, k)), NT
    else:
        assert wkind == "stkT" and tk == w.shape[2]
        b_spec, dims = pl.BlockSpec((None, tn, tk), lambda i, j, k: (k, j, 0)), NT
    if out_shape is None:
        out_shape = [jax.ShapeDtypeStruct((rows, n_out_cols), F32)]
        out_specs = [_row_spec(tm, tn)]
    return _mm_call(name, grid, dims, a_pieces, a_specs, (per, lambda i, j, k: k), [w], [b_spec], None,
                    list(extras), list(extra_specs), out_shape, out_specs, (tm, tn), a_fn, store)


def _mm_wgrad(name, a, g, *, kw, n, tmw, tn, a_cb0=0, a_fn=None, shard_cols=0, out_dtype=F32, colsum=False):
    a_pieces = list(a) if isinstance(a, (list, tuple)) else [a]
    g_pieces = list(g) if isinstance(g, (list, tuple)) else [g]
    rows = a_pieces[0].shape[0]
    tr = _tile(rows, ROW_TILE)
    grid = (n // tn, kw // tmw, rows // tr)
    a_specs, a_per = _piece_specs(a_pieces, tr, tmw, lambda j, i, k: k, lambda j, i, k: i, a_cb0)
    g_specs, g_per = _piece_specs(g_pieces, tr, tn, lambda j, i, k: k, lambda j, i, k: j, 0)
    if shard_cols:
        per = tn // shard_cols
        out_shape = [jax.ShapeDtypeStruct((n // shard_cols, kw, shard_cols), out_dtype)]
        out_specs = [pl.BlockSpec((per, tmw, shard_cols), lambda j, i, k: (j, i, 0))]

        def store(outs, acc):
            for q in range(per):
                outs[0][q] = acc[:, q * shard_cols:(q + 1) * shard_cols].astype(out_dtype)
    else:
        out_shape = [jax.ShapeDtypeStruct((kw, n), out_dtype)]
        out_specs = [pl.BlockSpec((tmw, tn), lambda j, i, k: (i, j))]

        def store(outs, acc):
            outs[0][...] = acc.astype(out_dtype)
    if colsum:
        out_shape.append(jax.ShapeDtypeStruct((1, n), F32))
        out_specs.append(pl.BlockSpec((1, tn), lambda j, i, k: (0, j)))
    res = _mm_call(name, grid, TN, a_pieces, a_specs, (a_per, lambda j, i, k: i), g_pieces, g_specs,
                   (g_per, lambda j, i, k: j), [], [], out_shape, out_specs, (tmw, tn), a_fn, store,
                   colsum_width=tn if colsum else 0)
    return res if colsum else res[0]


def _ln(x, g, b):
    mu = jnp.mean(x, axis=-1, keepdims=True)
    xc = x - mu
    var = jnp.mean(xc * xc, axis=-1, keepdims=True)
    return xc * lax.rsqrt(var + LN_EPS) * g + b


def _relu2(x):
    r = jnp.maximum(x, 0.0)
    return r * r


def _glu(y, gate):
    return y * jax.nn.sigmoid(gate)


def _ln_bwd(name, r, g, b, gy, dep=None):
    rows, d = r.shape
    tm = _tile(rows, ROW_TILE)
    deps = [] if dep is None else [dep]

    def body(r_ref, g_ref, b_ref, gy_ref, *rest):
        gr_ref, gg_ref, gb_ref = rest[len(deps):]
        _, vjp = jax.vjp(_ln, r_ref[...], g_ref[...], b_ref[...])
        gr, gg, gb = vjp(gy_ref[...])
        gr_ref[...] = gr

        @pl.when(pl.program_id(0) == 0)
        def _():
            gg_ref[...] = jnp.zeros_like(gg_ref)
            gb_ref[...] = jnp.zeros_like(gb_ref)

        gg_ref[...] += gg
        gb_ref[...] += gb

    row = pl.BlockSpec((tm, d), lambda i: (i, 0))
    vec = pl.BlockSpec((1, d), lambda i: (0, 0))
    return pl.pallas_call(
        body, name=name, grid=(rows // tm,),
        in_specs=[row, vec, vec, row] + [pl.BlockSpec(memory_space=pl.ANY)] * len(deps), out_specs=[row, vec, vec],
        out_shape=[jax.ShapeDtypeStruct((rows, d), F32), jax.ShapeDtypeStruct((1, d), F32),
                   jax.ShapeDtypeStruct((1, d), F32)],
        compiler_params=_params(("arbitrary",)),
    )(r, g, b, gy, *deps)


def _rowwise(name, fn, ins, n_out, width):
    rows = ins[0][0].shape[0]
    tm = _tile(rows, ROW_TILE)

    def body(*refs):
        res = fn(*[r[...] for r in refs[:len(ins)]])
        for o, v in zip(refs[len(ins):], res):
            o[...] = v

    return pl.pallas_call(
        body, name=name, grid=(rows // tm,),
        in_specs=[pl.BlockSpec((tm, wd), lambda i, cb=cb: (i, cb)) for _, cb, wd in ins],
        out_specs=[pl.BlockSpec((tm, width), lambda i: (i, 0))] * n_out,
        out_shape=[jax.ShapeDtypeStruct((rows, width), F32)] * n_out, compiler_params=_params(("parallel",)),
    )(*[a for a, _, _ in ins])


def _loss_grad(name, h, target, n_batch, lp, lead):
    rows, d = h.shape
    nq = lp // LANES
    lead_blocks = lead // LANES

    def body(h_ref, t_ref, g_ref, loss_ref):
        i = pl.program_id(1)

        @pl.when((pl.program_id(0) == 0) & (i == 0))
        def _():
            loss_ref[...] = jnp.zeros_like(loss_ref)

        diff = jnp.where(i >= lead_blocks, h_ref[...] - t_ref[...], 0.0)
        g_ref[...] = diff * (1.0 / d)
        loss_ref[...] += 0.5 * jnp.sum(diff * diff) * (1.0 / d)

    return pl.pallas_call(
        body, name=name, grid=(n_batch, nq),
        in_specs=[pl.BlockSpec((LANES, d), lambda b, i: (b * nq + i, 0)),
                  pl.BlockSpec((None, LANES, d), lambda b, i: (b, jnp.maximum(i - lead_blocks, 0), 0))],
        out_specs=[pl.BlockSpec((LANES, d), lambda b, i: (b * nq + i, 0)),
                   pl.BlockSpec((SUBLANES, LANES), lambda b, i: (0, 0))],
        out_shape=[jax.ShapeDtypeStruct((rows, d), F32), jax.ShapeDtypeStruct((SUBLANES, LANES), F32)],
        compiler_params=_params(("arbitrary", "arbitrary")),
    )(h, target)


def _meta_grad(name, g_h0, n_batch, lp, pad, n_meta):
    d = g_h0.shape[1]
    per = lp // n_meta
    at = pad // n_meta

    def body(g_ref, o_ref):
        @pl.when(pl.program_id(0) == 0)
        def _():
            o_ref[...] = jnp.zeros_like(o_ref)

        o_ref[...] += g_ref[...]

    return pl.pallas_call(
        body, name=name, grid=(n_batch,),
        in_specs=[pl.BlockSpec((n_meta, d), lambda b: (b * per + at, 0))],
        out_specs=pl.BlockSpec((n_meta, d), lambda b: (0, 0)),
        out_shape=jax.ShapeDtypeStruct((n_meta, d), F32),
        compiler_params=_params(("arbitrary",)),
    )(g_h0)


def _s5_param_fn(lr, li, ldt, br, bi):
    dt = jnp.exp(ldt)
    e = jnp.exp(lr * dt)
    w = li * dt
    lbr = e * jnp.cos(w)
    lbi = e * jnp.sin(w)
    nr = lbr - 1.0
    den = lr * lr + li * li
    cr = (nr * lr + lbi * li) / den
    ci = (lbi * lr - nr * li) / den
    bbr = cr[:, None, :] * br - ci[:, None, :] * bi
    bbi = cr[:, None, :] * bi + ci[:, None, :] * br
    return lbr, lbi, bbr, bbi


def _s5_params(name, lr, li, ldt, br, bi):
    def body(lr_ref, li_ref, ldt_ref, br_ref, bi_ref, o1, o2, o3, o4):
        res = _s5_param_fn(lr_ref[...], li_ref[...], ldt_ref[...], br_ref[...], bi_ref[...])
        for o, v in zip((o1, o2, o3, o4), res):
            o[...] = v

    shp = [jax.ShapeDtypeStruct(lr.shape, F32)] * 2 + [jax.ShapeDtypeStruct(br.shape, F32)] * 2
    return pl.pallas_call(body, name=name, out_shape=shp)(lr, li, ldt, br, bi)


def _s5_params_bwd(name, lr, li, ldt, br, bi, g_lbr, g_lbi, g_bbr, g_bbi, gd_parts):
    def body(lr_ref, li_ref, ldt_ref, br_ref, bi_ref, g1, g2, g3, g4, gd_ref, o1, o2, o3, o4, o5, o6):
        _, vjp = jax.vjp(_s5_param_fn, lr_ref[...], li_ref[...], ldt_ref[...], br_ref[...], bi_ref[...])
        res = vjp((jnp.sum(g1[...], axis=0), jnp.sum(g2[...], axis=0), g3[...], g4[...]))
        for o, v in zip((o1, o2, o3, o4, o5), res):
            o[...] = v
        o6[...] = jnp.sum(gd_ref[...], axis=0)

    shp = ([jax.ShapeDtypeStruct(lr.shape, F32)] * 2 + [jax.ShapeDtypeStruct(ldt.shape, F32)]
           + [jax.ShapeDtypeStruct(br.shape, F32)] * 2 + [jax.ShapeDtypeStruct(gd_parts.shape[1:], F32)])
    return pl.pallas_call(body, name=name, out_shape=shp)(lr, li, ldt, br, bi, g_lbr, g_lbi, g_bbr, g_bbi, gd_parts)


def _interleave(re, im, w):
    lead = re.shape[:-1]
    nj = re.shape[-1] // w
    return jnp.stack([re.reshape(*lead, nj, w), im.reshape(*lead, nj, w)], axis=-2).reshape(*lead, 2 * nj * w)


def _deinterleave(x, w):
    lead = x.shape[:-1]
    nj = x.shape[-1] // (2 * w)
    y = x.reshape(*lead, nj, 2, w)
    return y[..., 0, :].reshape(*lead, nj * w), y[..., 1, :].reshape(*lead, nj * w)


def _cmul(ar, ai, br, bi):
    return ar * br - ai * bi, ar * bi + ai * br


def _powers(lr, li):
    p = [(lr, li)]
    p.append(_cmul(*p[0], *p[0]))
    p.append(_cmul(*p[1], *p[0]))
    p.append(_cmul(*p[1], *p[1]))
    p.append(_cmul(*p[3], *p[0]))
    p.append(_cmul(*p[3], *p[1]))
    p.append(_cmul(*p[3], *p[2]))
    p.append(_cmul(*p[3], *p[3]))
    return p


def _scan_tile(xr, xi, steps):
    for sh, br, bi, m in steps:
        rr = jnp.where(m, pltpu.roll(xr, sh, 0), 0.0)
        ri = jnp.where(m, pltpu.roll(xi, sh, 0), 0.0)
        xr, xi = xr + (br * rr - bi * ri), xi + (br * ri + bi * rr)
    return xr, xi


def _s5_scan(name, bu, lam, n_batch, lp, w):
    rows, two_ns = bu.shape
    nj = two_ns // (2 * w)
    nt = lp // SUBLANES

    def body(x_ref, lam_ref, s_ref):
        pw = _powers(lam_ref[:, :w], lam_ref[:, w:])
        tab_r = jnp.concatenate([p[0] for p in pw], axis=0)
        tab_i = jnp.concatenate([p[1] for p in pw], axis=0)
        row = lax.broadcasted_iota(jnp.int32, (SUBLANES, w), 0)
        steps = [(s, jnp.broadcast_to(pw[s - 1][0], (SUBLANES, w)), jnp.broadcast_to(pw[s - 1][1], (SUBLANES, w)),
                  row >= s) for s in (1, 2, 4)]

        def tile(t, carry):
            cr, ci = carry
            r0 = pl.multiple_of(t * SUBLANES, SUBLANES)
            x = x_ref[pl.ds(r0, SUBLANES), :]
            xr, xi = _scan_tile(x[:, :w], x[:, w:], steps)
            sr = xr + (tab_r * cr - tab_i * ci)
            si = xi + (tab_r * ci + tab_i * cr)
            s_ref[pl.ds(r0, SUBLANES), :] = jnp.concatenate([sr, si], axis=1)
            return sr[SUBLANES - 1:, :], si[SUBLANES - 1:, :]

        zero = jnp.zeros((1, w), F32)
        lax.fori_loop(0, nt, tile, (zero, zero))

    spec = pl.BlockSpec((lp, 2 * w), lambda b, j: (b, j))
    return pl.pallas_call(
        body, name=name, grid=(n_batch, nj), in_specs=[spec, pl.BlockSpec((1, 2 * w), lambda b, j: (0, j))],
        out_specs=spec, out_shape=jax.ShapeDtypeStruct((rows, two_ns), F32),
        compiler_params=_params(("parallel", "parallel")),
    )(bu, lam)


def _s5_scan_bwd(name, gd, states, lam, n_batch, lp, w):
    rows, two_ns = gd.shape
    nj = two_ns // (2 * w)
    nt = lp // SUBLANES

    def body(x_ref, s_ref, lam_ref, g_ref, gl_ref):
        pw = _powers(lam_ref[:, :w], -lam_ref[:, w:])
        tab_r = jnp.concatenate([p[0] for p in reversed(pw)], axis=0)
        tab_i = jnp.concatenate([p[1] for p in reversed(pw)], axis=0)
        row = lax.broadcasted_iota(jnp.int32, (SUBLANES, w), 0)
        steps = [(SUBLANES - s, jnp.broadcast_to(pw[s - 1][0], (SUBLANES, w)),
                  jnp.broadcast_to(pw[s - 1][1], (SUBLANES, w)), row < SUBLANES - s) for s in (1, 2, 4)]

        def tile(u, carry):
            cr, ci, ar, ai = carry
            t = nt - 1 - u
            r0 = pl.multiple_of(t * SUBLANES, SUBLANES)
            x = x_ref[pl.ds(r0, SUBLANES), :]
            xr, xi = _scan_tile(x[:, :w], x[:, w:], steps)
            gr = xr + (tab_r * cr - tab_i * ci)
            gi = xi + (tab_r * ci + tab_i * cr)
            g_ref[pl.ds(r0, SUBLANES), :] = jnp.concatenate([gr, gi], axis=1)
            p0 = pl.multiple_of(jnp.maximum(t - 1, 0) * SUBLANES, SUBLANES)
            prev = s_ref[pl.ds(p0, SUBLANES), :][SUBLANES - 1:, :] * jnp.where(t > 0, 1.0, 0.0)
            cur = s_ref[pl.ds(r0, SUBLANES), :]
            spr = jnp.where(row >= 1, pltpu.roll(cur[:, :w], 1, 0), prev[:, :w])
            spi = jnp.where(row >= 1, pltpu.roll(cur[:, w:], 1, 0), prev[:, w:])
            return gr[:1, :], gi[:1, :], ar + gr * spr + gi * spi, ai + gi * spr - gr * spi

        z1 = jnp.zeros((1, w), F32)
        z8 = jnp.zeros((SUBLANES, w), F32)
        _, _, ar, ai = lax.fori_loop(0, nt, tile, (z1, z1, z8, z8))
        gl_ref[...] = jnp.concatenate([jnp.sum(ar, axis=0, keepdims=True), jnp.sum(ai, axis=0, keepdims=True)], axis=1)

    spec = pl.BlockSpec((lp, 2 * w), lambda b, j: (b, j))
    return pl.pallas_call(
        body, name=name, grid=(n_batch, nj),
        in_specs=[spec, spec, pl.BlockSpec((1, 2 * w), lambda b, j: (0, j))],
        out_specs=[spec, pl.BlockSpec((None, 1, 2 * w), lambda b, j: (b, 0, j))],
        out_shape=[jax.ShapeDtypeStruct((rows, two_ns), F32), jax.ShapeDtypeStruct((n_batch, 1, two_ns), F32)],
        compiler_params=_params(("parallel", "parallel")),
    )(gd, states, lam)


def _log_sigmoid(z):
    return jnp.minimum(z, 0.0) - jnp.log(1.0 + jnp.exp(-jnp.abs(z)))


def _attn_masks(i, j, pad):
    rowpos = i * LANES + lax.broadcasted_iota(jnp.int32, (LANES, LANES), 0)
    colpos = j * LANES + lax.broadcasted_iota(jnp.int32, (LANES, LANES), 1)
    return (colpos < rowpos) & (colpos >= pad)


ATTN_GROUP = 4


def _tri_ones(strict_upper):
    r = lax.broadcasted_iota(jnp.int32, (LANES, 2 * LANES), 0)
    c = lax.broadcasted_iota(jnp.int32, (LANES, 2 * LANES), 1)
    tri = (r > c) if strict_upper else (r < c)
    return jnp.where((c >= LANES) | tri, 1.0, 0.0).astype(BF16)


def _head_masks():
    lane = lax.broadcasted_iota(jnp.int32, (1, LANES), 1)
    return [lane < SB_HEAD_DIM, lane >= SB_HEAD_DIM]


def _run_groups(n, first, sign, make):
    j, left, g = first, n, ATTN_GROUP
    while g >= 1:
        shift = g.bit_length() - 1
        count = lax.shift_right_logical(left, shift)
        fn = make(g)

        def loop(_, jcur, fn=fn, g=g):
            fn(jcur)
            return jcur + sign * g

        j = lax.fori_loop(0, count, loop, j)
        left = left - lax.shift_left(count, shift)
        g //= 2


def _attn_fwd(name, proj, n_batch, lp, pad, q_cb, k_cb, v_cb, n_pairs):
    rows = proj.shape[0]
    nq = lp // LANES
    scale = SB_HEAD_DIM ** -0.5

    def body(q_ref, k_ref, v_ref, o_ref, acc_s):
        i = pl.program_id(2)
        hm = _head_masks()
        comb = _tri_ones(True)
        qs = q_ref[...] * scale
        qh = [jnp.where(m, qs, 0.0).astype(BF16) for m in hm]
        acc_s[...] = jnp.zeros_like(acc_s)
        o_ref[...] = jnp.zeros_like(o_ref)

        def make(group):
            def fn(jtop):
                pend = []
                for g in range(group):
                    j = jtop - g
                    r0 = pl.multiple_of(j * LANES, LANES)
                    kj = k_ref[pl.ds(r0, LANES), :].astype(BF16)
                    vj = v_ref[pl.ds(r0, LANES), :]
                    vis = _attn_masks(i, j, pad)
                    for h in range(2):
                        z = lax.dot_general(qh[h], kj, NT, preferred_element_type=F32)
                        lsz = _log_sigmoid(z)
                        cr = _dot_split(jnp.where(vis, lsz - z, 0.0), comb, NN)
                        pend.append((h, vis, lsz, cr, jnp.where(hm[h], vj, 0.0).astype(BF16)))
                for h, vis, lsz, cr, vh in pend:
                    acc = acc_s[h]
                    wgt = jnp.where(vis, jnp.exp(lsz + cr[:, :LANES] + acc), 0.0)
                    acc_s[h] = acc + cr[:, LANES:]
                    o_ref[...] += lax.dot_general(wgt.astype(BF16), vh, NN, preferred_element_type=F32)
            return fn

        _run_groups(i + 1, i, -1, make)

    return pl.pallas_call(
        body, name=name, grid=(n_batch, n_pairs, nq),
        in_specs=[pl.BlockSpec((LANES, LANES), lambda b, h, i: (b * nq + i, q_cb + h)),
                  pl.BlockSpec((lp, LANES), lambda b, h, i: (b, k_cb + h)),
                  pl.BlockSpec((lp, LANES), lambda b, h, i: (b, v_cb + h))],
        out_specs=pl.BlockSpec((LANES, LANES), lambda b, h, i: (b * nq + i, h)),
        out_shape=jax.ShapeDtypeStruct((rows, n_pairs * LANES), F32),
        scratch_shapes=[pltpu.VMEM((2, LANES, LANES), F32)],
        compiler_params=_params(("parallel", "parallel", "arbitrary")),
    )(proj, proj, proj)


def _attn_bwd(name, proj, g_out, n_batch, lp, pad, q_cb, k_cb, v_cb, go_cb, n_pairs):
    rows = proj.shape[0]
    nq = lp // LANES
    scale = SB_HEAD_DIM ** -0.5

    def body(q_ref, k_ref, v_ref, go_ref, gq_ref, gk_ref, gv_ref, ga_s, sz_s, acc_s):
        i = pl.program_id(2)

        @pl.when(i == 0)
        def _():
            gk_ref[...] = jnp.zeros_like(gk_ref)
            gv_ref[...] = jnp.zeros_like(gv_ref)

        hm = _head_masks()
        comb_up = _tri_ones(True)
        comb_lo = _tri_ones(False)
        qs = q_ref[...] * scale
        go = go_ref[...]
        qh = [jnp.where(m, qs, 0.0).astype(BF16) for m in hm]
        goh = [jnp.where(m, go, 0.0).astype(BF16) for m in hm]
        acc_s[...] = jnp.zeros_like(acc_s)
        gq_ref[...] = jnp.zeros_like(gq_ref)

        def make_down(group):
            def fn(jtop):
                pend = []
                for g in range(group):
                    j = jtop - g
                    r0 = pl.multiple_of(j * LANES, LANES)
                    kj = k_ref[pl.ds(r0, LANES), :].astype(BF16)
                    vj = v_ref[pl.ds(r0, LANES), :].astype(BF16)
                    vis = _attn_masks(i, j, pad)
                    for h in range(2):
                        z = lax.dot_general(qh[h], kj, NT, preferred_element_type=F32)
                        lsz = _log_sigmoid(z)
                        cr = _dot_split(jnp.where(vis, lsz - z, 0.0), comb_up, NN)
                        gw = lax.dot_general(goh[h], vj, NT, preferred_element_type=F32)
                        pend.append((h, j, r0, vis, lsz, cr, gw))
                for h, j, r0, vis, lsz, cr, gw in pend:
                    acc = acc_s[h]
                    wgt = jnp.where(vis, jnp.exp(lsz + cr[:, :LANES] + acc), 0.0)
                    acc_s[h] = acc + cr[:, LANES:]
                    ga_s[h, j] = gw * wgt
                    sz_s[h, j] = jnp.exp(lsz)
                    gv_ref[pl.ds(r0, LANES), :] += lax.dot_general(wgt.astype(BF16), goh[h], TN, preferred_element_type=F32)
            return fn

        _run_groups(i + 1, i, -1, make_down)
        acc_s[...] = jnp.zeros_like(acc_s)

        def make_up(group):
            def fn(jbot):
                pend = []
                for g in range(group):
                    j = jbot + g
                    r0 = pl.multiple_of(j * LANES, LANES)
                    kj = k_ref[pl.ds(r0, LANES), :]
                    vis = _attn_masks(i, j, pad)
                    for h in range(2):
                        ga = ga_s[h, j]
                        pend.append((h, j, r0, vis, ga, _dot_split(ga, comb_lo, NN), jnp.where(hm[h], kj, 0.0).astype(BF16)))
                for h, j, r0, vis, ga, cr, kh in pend:
                    pre = acc_s[h]
                    glk = cr[:, :LANES] + pre
                    acc_s[h] = pre + cr[:, LANES:]
                    sz = sz_s[h, j]
                    gz = jnp.where(vis, ga * (1.0 - sz) - glk * sz, 0.0).astype(BF16)
                    gq_ref[...] += lax.dot_general(gz, kh, NN, preferred_element_type=F32)
                    gk_ref[pl.ds(r0, LANES), :] += lax.dot_general(gz, qh[h], TN, preferred_element_type=F32)
            return fn

        _run_groups(i + 1, 0, 1, make_up)
        gq_ref[...] = gq_ref[...] * scale

    blk = lambda cb: pl.BlockSpec((LANES, LANES), lambda b, h, i: (b * nq + i, cb + h))
    full = lambda cb: pl.BlockSpec((lp, LANES), lambda b, h, i: (b, cb + h))
    shp = jax.ShapeDtypeStruct((rows, n_pairs * LANES), F32)
    per_block = pltpu.VMEM((2, nq, LANES, LANES), F32)
    return pl.pallas_call(
        body, name=name, grid=(n_batch, n_pairs, nq),
        in_specs=[blk(q_cb), full(k_cb), full(v_cb), blk(go_cb)],
        out_specs=[blk(0), full(0), full(0)], out_shape=[shp, shp, shp],
        scratch_shapes=[per_block, per_block, pltpu.VMEM((2, LANES, LANES), F32)],
        compiler_params=_params(("parallel", "parallel", "arbitrary")),
    )(proj, proj, proj, g_out)


def _lb_fn(gamma):
    g0, g1 = gamma[0:1, :], gamma[1:2, :]
    mx = jnp.maximum(g0, g1)
    e0, e1 = jnp.exp(g0 - mx), jnp.exp(g1 - mx)
    p0, p1 = e0 / (e0 + e1), e1 / (e0 + e1)
    return (p0 + p1) - p0


def _lower_bound(name, gamma):
    def body(g_ref, o_ref):
        o_ref[...] = _lb_fn(g_ref[...])

    return pl.pallas_call(body, name=name, out_shape=jax.ShapeDtypeStruct((1, gamma.shape[1]), F32))(gamma)


def _lower_bound_bwd(name, gamma, g_lb_parts, g_ng_parts):
    def body(g_ref, glb_ref, gng_ref, o_ref, o2_ref):
        _, vjp = jax.vjp(_lb_fn, g_ref[...])
        o_ref[...] = vjp(jnp.sum(glb_ref[...], axis=0))[0]
        o2_ref[...] = jnp.sum(gng_ref[...], axis=0)

    return pl.pallas_call(
        body, name=name,
        out_shape=[jax.ShapeDtypeStruct(gamma.shape, F32), jax.ShapeDtypeStruct((1, gamma.shape[1]), F32)],
    )(gamma, g_lb_parts, g_ng_parts)


def _hg_gates(fc, lb, rowmask, tril):
    f = lb + (1.0 - lb) * jax.nn.sigmoid(fc)
    bcum = jnp.dot(tril, jnp.log(f) * rowmask, preferred_element_type=F32, precision=lax.Precision.HIGHEST)
    return 1.0 - f, bcum


def _hg_state(fc, ic, lb, st, rowmask, tril):
    k, bcum = _hg_gates(fc, lb, rowmask, tril)
    blast = bcum[HG_CHUNK - 1:, :]
    return jnp.exp(blast) * st + _dot(ic * rowmask, k * jnp.exp(blast - bcum), TN)


def _hg_chunk(qc, fc, ic, gc, lb, ng, st, rowmask, tril):
    k, bcum = _hg_gates(fc, lb, rowmask, tril)
    blast = bcum[HG_CHUNK - 1:, :]
    v = ic * rowmask
    qd = qc * jnp.exp(bcum)
    scores = jnp.where(tril > 0.5, _dot(qd, k * jnp.exp(-bcum), NT), 0.0)
    o = _dot(scores, v, NN) + _dot(qd, st, NT)
    st_new = jnp.exp(blast) * st + _dot(v, k * jnp.exp(blast - bcum), TN)
    o = o * lax.rsqrt(jnp.mean(o * o, axis=-1, keepdims=True) + RMS_EPS) * ng
    return o * (gc * jax.nn.sigmoid(gc)), st_new


def _hg_consts(c, pad):
    r = lax.broadcasted_iota(jnp.int32, (HG_CHUNK, HG_CHUNK), 0)
    cc = lax.broadcasted_iota(jnp.int32, (HG_CHUNK, HG_CHUNK), 1)
    tril = jnp.where(r >= cc, 1.0, 0.0).astype(F32)
    pos = c * HG_CHUNK + lax.broadcasted_iota(jnp.int32, (HG_CHUNK, 1), 0)
    return tril, jnp.where(pos >= pad, 1.0, 0.0).astype(F32)


HG_HEADS_PER_STEP = 2
HG_UNROLL = 2


def _hg_head_cols():
    return [slice(h * HG_DK, (h + 1) * HG_DK) for h in range(HG_HEADS_PER_STEP)]


def _hg_specs(lp, n_heads):
    wide = HG_HEADS_PER_STEP * HG_DK
    groups = n_heads // HG_HEADS_PER_STEP
    col = lambda off: pl.BlockSpec((lp, wide), lambda b, h: (b, off * groups + h))
    vec = pl.BlockSpec((1, wide), lambda b, h: (0, h))
    return groups, col, vec


def _hgrn_fwd(name, proj, lb, ng, n_batch, lp, pad, n_heads):
    rows = proj.shape[0]
    nc = lp // HG_CHUNK
    groups, col, vec = _hg_specs(lp, n_heads)

    def body(q_ref, f_ref, i_ref, g_ref, lb_ref, ng_ref, o_ref):
        def chunk(c, sts):
            sl = pl.ds(pl.multiple_of(c * HG_CHUNK, HG_CHUNK), HG_CHUNK)
            tril, rowmask = _hg_consts(c, pad)
            new = []
            for cols, st in zip(_hg_head_cols(), sts):
                o, st = _hg_chunk(q_ref[sl, cols], f_ref[sl, cols], i_ref[sl, cols], g_ref[sl, cols],
                                  lb_ref[:, cols], ng_ref[:, cols], st, rowmask, tril)
                o_ref[sl, cols] = o
                new.append(st)
            return tuple(new)

        zero = jnp.zeros((HG_DK, HG_DK), F32)
        lax.fori_loop(0, nc, chunk, (zero,) * HG_HEADS_PER_STEP, unroll=HG_UNROLL)

    return pl.pallas_call(
        body, name=name, grid=(n_batch, groups), in_specs=[col(0), col(1), col(2), col(3), vec, vec],
        out_specs=col(0), out_shape=jax.ShapeDtypeStruct((rows, n_heads * HG_DK), F32),
        compiler_params=_params(("parallel", "parallel")),
    )(proj, proj, proj, proj, lb, ng)


def _hgrn_bwd(name, proj, lb, ng, g_out, n_batch, lp, pad, n_heads):
    rows = proj.shape[0]
    width = n_heads * HG_DK
    nc = lp // HG_CHUNK
    groups, col, vec = _hg_specs(lp, n_heads)

    def body(q_ref, f_ref, i_ref, g_ref, lb_ref, ng_ref, go_ref, gq_ref, gf_ref, gi_ref, gg_ref, glb_ref, gng_ref, st_s):
        heads = list(enumerate(_hg_head_cols()))

        def fwd(c, sts):
            sl = pl.ds(pl.multiple_of(c * HG_CHUNK, HG_CHUNK), HG_CHUNK)
            tril, rowmask = _hg_consts(c, pad)
            new = []
            for (h, cols), st in zip(heads, sts):
                st_s[h, c] = st
                new.append(_hg_state(f_ref[sl, cols], i_ref[sl, cols], lb_ref[:, cols], st, rowmask, tril))
            return tuple(new)

        zero = jnp.zeros((HG_DK, HG_DK), F32)
        lax.fori_loop(0, nc, fwd, (zero,) * HG_HEADS_PER_STEP, unroll=HG_UNROLL)

        def bwd(u, carry):
            c = nc - 1 - u
            sl = pl.ds(pl.multiple_of(c * HG_CHUNK, HG_CHUNK), HG_CHUNK)
            tril, rowmask = _hg_consts(c, pad)
            fn = functools.partial(_hg_chunk, rowmask=rowmask, tril=tril)
            new = []
            for (h, cols), (gst, glb, gng) in zip(heads, carry):
                _, vjp = jax.vjp(fn, q_ref[sl, cols], f_ref[sl, cols], i_ref[sl, cols], g_ref[sl, cols],
                                 lb_ref[:, cols], ng_ref[:, cols], st_s[h, c])
                gq, gf, gi, gg, dlb, dng, gst = vjp((go_ref[sl, cols], gst))
                gq_ref[sl, cols] = gq
                gf_ref[sl, cols] = gf
                gi_ref[sl, cols] = gi
                gg_ref[sl, cols] = gg
                new.append((gst, glb + dlb, gng + dng))
            return tuple(new)

        zv = jnp.zeros((1, HG_DK), F32)
        res = lax.fori_loop(0, nc, bwd, ((zero, zv, zv),) * HG_HEADS_PER_STEP, unroll=HG_UNROLL)
        for (_, cols), (_, glb, gng) in zip(heads, res):
            glb_ref[:, cols] = glb
            gng_ref[:, cols] = gng

    part = pl.BlockSpec((None, 1, HG_HEADS_PER_STEP * HG_DK), lambda b, h: (b, 0, h))
    big = jax.ShapeDtypeStruct((rows, width), F32)
    small = jax.ShapeDtypeStruct((n_batch, 1, width), F32)
    return pl.pallas_call(
        body, name=name, grid=(n_batch, groups),
        in_specs=[col(0), col(1), col(2), col(3), vec, vec, col(0)],
        out_specs=[col(0), col(0), col(0), col(0), part, part],
        out_shape=[big, big, big, big, small, small],
        scratch_shapes=[pltpu.VMEM((HG_HEADS_PER_STEP, nc, HG_DK, HG_DK), F32)],
        compiler_params=_params(("parallel", "parallel")),
    )(proj, proj, proj, proj, lb, ng, g_out)


def _exchange(name, srcs, scatter):
    nw = len(srcs)

    def body(*refs):
        local, remote = _exchange_copies(refs[:nw], refs[nw:2 * nw], *refs[2 * nw:], scatter)
        for cp in local + remote:
            cp.start()
        for cp in local + remote:
            cp.wait()

    any_spec = pl.BlockSpec(memory_space=pl.ANY)
    out_shape = [jax.ShapeDtypeStruct(s.shape if scatter else (N_DEV,) + s.shape, s.dtype) for s in srcs]
    return pl.pallas_call(
        body, name=name, in_specs=[any_spec] * nw, out_specs=[any_spec] * nw, out_shape=out_shape,
        scratch_shapes=[pltpu.SemaphoreType.DMA((nw * (N_DEV - 1),)), pltpu.SemaphoreType.DMA((nw * (N_DEV - 1),)),
                        pltpu.SemaphoreType.DMA((nw,))],
    )(*srcs)


def _exchange_copies(src, dst, send, recv, loc, scatter):
    x, y, c = lax.axis_index("x"), lax.axis_index("y"), lax.axis_index("c")
    me = 4 * x + 2 * y + c
    local, remote = [], []
    for w in range(len(src)):
        local.append(pltpu.make_async_copy(src[w].at[me] if scatter else src[w], dst[w].at[me], loc.at[w]))
    for k in range(1, N_DEV):
        px = 1 - x if k & 4 else x
        py = 1 - y if k & 2 else y
        pc = 1 - c if k & 1 else c
        peer = 4 * px + 2 * py + pc
        for w in range(len(src)):
            remote.append(pltpu.make_async_remote_copy(
                src_ref=src[w].at[peer] if scatter else src[w], dst_ref=dst[w].at[me],
                send_sem=send.at[w * (N_DEV - 1) + k - 1], recv_sem=recv.at[w * (N_DEV - 1) + k - 1],
                device_id=(px, py, pc), device_id_type=pl.DeviceIdType.MESH))
    return local, remote


_HBM_SPEC = pl.BlockSpec(memory_space=pltpu.HBM)
_SEM_SPEC = pl.BlockSpec(memory_space=pltpu.SEMAPHORE)
_ANY_SPEC = pl.BlockSpec(memory_space=pl.ANY)
_DATAFLOW = pltpu.SideEffectType.DATAFLOW_SIDE_EFFECTING


def _exchange_start(name, srcs, scatter, dep=None):
    nw = len(srcs)
    srcs = [pltpu.with_memory_space_constraint(s, pltpu.HBM) for s in srcs]
    lands = [pltpu.with_memory_space_constraint(lax.empty(s.shape if scatter else (N_DEV,) + s.shape, s.dtype), pltpu.HBM)
             for s in srcs]
    deps = [] if dep is None else [dep]

    def body(*refs):
        src, dst = refs[:nw], refs[nw:2 * nw]
        send, recv, loc = refs[2 * nw + len(deps):2 * nw + len(deps) + 3]
        token = refs[-1]
        local, remote = _exchange_copies(src, dst, send, recv, loc, scatter)
        for cp in local + remote:
            cp.start()
        token[...] = jnp.zeros_like(token)

    sems = [pltpu.SemaphoreType.DMA((nw * (N_DEV - 1),)), pltpu.SemaphoreType.DMA((nw * (N_DEV - 1),)),
            pltpu.SemaphoreType.DMA((nw,))]
    out = pl.pallas_call(
        body, name=name,
        out_shape=(*sems, *[pltpu.HBM(s.shape, s.dtype) for s in srcs], *[pltpu.HBM(s.shape, s.dtype) for s in lands],
                   jax.ShapeDtypeStruct((SUBLANES, LANES), F32)),
        in_specs=[_HBM_SPEC] * (2 * nw) + [_ANY_SPEC] * len(deps),
        out_specs=(_SEM_SPEC, _SEM_SPEC, _SEM_SPEC, *[_HBM_SPEC] * (2 * nw), pl.BlockSpec(memory_space=pltpu.VMEM)),
        input_output_aliases={i: 3 + i for i in range(2 * nw)},
        compiler_params=pltpu.CompilerParams(has_side_effects=_DATAFLOW),
    )(*srcs, *lands, *deps)
    return {"sems": out[:3], "srcs": out[3:3 + nw], "lands": out[3 + nw:3 + 2 * nw], "token": out[-1], "scatter": scatter}


def _exchange_wait(name, handle, after):
    nw = len(handle["srcs"])
    scatter = handle["scatter"]

    def body(*refs):
        src, dst = refs[:nw], refs[nw:2 * nw]
        send, recv, loc = refs[2 * nw:2 * nw + 3]
        local, remote = _exchange_copies(src, dst, send, recv, loc, scatter)
        for cp in local:
            cp.wait()
        for cp in remote:
            cp.wait_send()
            cp.wait_recv()

    out = pl.pallas_call(
        body, name=name,
        out_shape=(*[pltpu.HBM(s.shape, s.dtype) for s in handle["srcs"]],
                   *[pltpu.HBM(s.shape, s.dtype) for s in handle["lands"]]),
        in_specs=[_HBM_SPEC] * (2 * nw) + [_SEM_SPEC] * 3 + [_ANY_SPEC],
        out_specs=tuple([_HBM_SPEC] * (2 * nw)),
        input_output_aliases={i: i for i in range(2 * nw)},
        compiler_params=pltpu.CompilerParams(has_side_effects=_DATAFLOW),
    )(*handle["srcs"], *handle["lands"], *handle["sems"], after)
    return list(out[nw:])


def _adamw(w, g, m, v):
    m = ADAM_B1 * m + (1.0 - ADAM_B1) * g
    v = ADAM_B2 * v + (1.0 - ADAM_B2) * (g * g)
    m_hat = m / (1.0 - ADAM_B1 ** ADAM_STEP)
    v_hat = v / (1.0 - ADAM_B2 ** ADAM_STEP)
    delta = -ADAM_LR * (m_hat / (jnp.sqrt(v_hat) + ADAM_EPS) + ADAM_WD * w)
    return delta, m, v


def _adamw_summed(name, parts, w, m, v):
    rows, cols = w.shape
    n_parts = parts.shape[0]
    tr = _tile(rows, max(SUBLANES, (1 << 18) // cols))

    def body(p_ref, w_ref, m_ref, v_ref, g_ref, d_ref, nm_ref, nv_ref):
        g = p_ref[0].astype(F32)
        for s in range(1, n_parts):
            g = g + p_ref[s].astype(F32)
        d, nm, nv = _adamw(w_ref[...], g, m_ref[...], v_ref[...])
        g_ref[...] = g
        d_ref[...] = d
        nm_ref[...] = nm
        nv_ref[...] = nv

    spec = pl.BlockSpec((tr, cols), lambda i: (i, 0))
    shp = jax.ShapeDtypeStruct((rows, cols), F32)
    return pl.pallas_call(
        body, name=name, grid=(rows // tr,),
        in_specs=[pl.BlockSpec((n_parts, tr, cols), lambda i: (0, i, 0)), spec, spec, spec],
        out_specs=[spec] * 4, out_shape=[shp] * 4, compiler_params=_params(("parallel",)),
    )(parts, w, m, v)


def _pack_rows(arrays, cols):
    out = []
    for a in arrays:
        flat = a.reshape(-1)
        n = -(-flat.shape[0] // cols) * cols
        out.append(jnp.pad(flat, (0, n - flat.shape[0])).reshape(-1, cols))
    packed = jnp.concatenate(out, axis=0)
    return jnp.pad(packed, ((0, -packed.shape[0] % SUBLANES), (0, 0)))


def _unpack_rows(packed, shapes, cols):
    out, r = [], 0
    for s in shapes:
        n = math.prod(s)
        nr = -(-n // cols)
        out.append(packed[r:r + nr].reshape(-1)[:n].reshape(s))
        r += nr
    return out


def _block_diag(blocks):
    g, a, b = blocks.shape
    eye = jnp.eye(g, dtype=blocks.dtype)
    return (eye[:, None, :, None] * blocks[:, :, None, :]).reshape(g * a, g * b)


def _diag_blocks(dense, g):
    a, b = dense.shape[0] // g, dense.shape[1] // g
    return jnp.einsum("gagb->gab", dense.reshape(g, a, g, b))


def _local_step(x, target, meta, wts, small, late_weights, on_grads):
    n_batch, seq, d = x.shape
    n_meta = meta.shape[0]
    pad = -(seq + n_meta) % LANES
    lead = pad + n_meta
    lp = lead + seq
    rows = n_batch * lp
    s5w = wts["glu"].shape[0]
    n_ab = wts["in_ab"].shape[2]
    ab_cols = wts["in_ab"].shape[0] * n_ab
    sbw = (ab_cols - s5w) // 3
    dff = small["mlp_b_up"].shape[1]
    n_pairs = sbw // LANES
    n_hg = d // HG_DK
    s5_cb = s5w // LANES
    sb_cb = sbw // LANES
    tm = _tile(rows, ROW_TILE)
    groups, n_state, grp = small["s5_b_re"].shape[1:]
    ns = groups * n_state
    sw = min(SCAN_LANES, ns)

    h0 = jnp.concatenate(
        [jnp.zeros((n_batch, pad, d), F32), jnp.broadcast_to(meta[None], (n_batch, n_meta, d)), x], axis=1
    ).reshape(rows, d)

    lam_re, lam_im = small["s5_lam_re"][0], small["s5_lam_im"][0]
    log_dt = small["s5_log_dt"][0][:, None]
    b_re_t = small["s5_b_re"][0].transpose(0, 2, 1)
    b_im_t = small["s5_b_im"][0].transpose(0, 2, 1)
    c_re, c_im = small["s5_c_re"][0], small["s5_c_im"][0]
    lbr, lbi, bbr, bbi = _s5_params("s5_params", lam_re, lam_im, log_dt, b_re_t, b_im_t)
    b_blk = _interleave(_block_diag(bbr), _block_diag(bbi), sw).astype(BF16)
    c_blk = _interleave(_block_diag(c_re), _block_diag(-c_im), sw).T.astype(BF16)
    lam_row = _interleave(lbr.reshape(1, ns), lbi.reshape(1, ns), sw)
    d_row = small["s5_d"].reshape(1, s5w)

    def ln_store(outs, acc, res, bias, g, b):
        r = ALPHA * res + acc + bias
        outs[0][...] = r
        outs[1][...] = _ln(r, g, b)

    zero_bias = jnp.zeros((1, d), F32)

    def mix_ln(name, a, w, k_total, tk, res, bias, g, b, a_fn=None):
        return _mm_act(name, a, w, "nat", n_out_cols=d, k_total=k_total, tn=d, tk=tk, a_fn=a_fn,
                       extras=(res, bias, g, b), extra_specs=(_row_spec(tm, d), _vec_spec(d), _vec_spec(d), _vec_spec(d)),
                       store=ln_store, out_shape=[jax.ShapeDtypeStruct((rows, d), F32)] * 2,
                       out_specs=[_row_spec(tm, d)] * 2)

    def two(width):
        return [jax.ShapeDtypeStruct((rows, width), F32)] * 2, [_row_spec(tm, width)] * 2

    proj_ab = _mm_act("in_ab", h0, wts["in_ab"], "stk", n_out_cols=ab_cols, k_total=d, tn=n_ab, tk=d)[0]
    bu = _mm_act("s5_bu", proj_ab, b_blk, "nat", n_out_cols=2 * ns, k_total=s5w, tn=min(2 * ns, 2048), tk=s5w)[0]
    states = _s5_scan("s5_scan", bu, lam_row, n_batch, lp, sw)

    def gelu_store(outs, acc, u, dv):
        ypre = acc + dv * u
        outs[0][...] = ypre
        outs[1][...] = jax.nn.gelu(ypre)

    shp2, spec2 = two(s5w)
    ypre, y = _mm_act(
        "s5_y", states, c_blk, "nat", n_out_cols=s5w, k_total=2 * ns, tn=s5w, tk=min(2 * ns, 1024),
        extras=(proj_ab, d_row), extra_specs=(_row_spec(tm, s5w), _vec_spec(s5w)), store=gelu_store,
        out_shape=shp2, out_specs=spec2)

    def glu_store(outs, acc, yv, bias):
        gate = acc + bias
        outs[0][...] = gate
        outs[1][...] = _glu(yv, gate)

    gate, a_out = _mm_act(
        "s5_glu", y, wts["glu"], "nat", n_out_cols=s5w, k_total=s5w, tn=s5w, tk=s5w,
        extras=(y, small["s5_b_glu"]), extra_specs=(_row_spec(tm, s5w), _vec_spec(s5w)), store=glu_store,
        out_shape=shp2, out_specs=spec2)
    b_out = _attn_fwd("sb_attn", proj_ab, n_batch, lp, pad, s5_cb, s5_cb + sb_cb, s5_cb + 2 * sb_cb, n_pairs)

    def bias_store(outs, acc, bias):
        outs[0][...] = acc + bias

    def mlp_fwd(layer, h_in):
        up = _mm_act(f"up{layer}", h_in, wts["up"][layer], "stk", n_out_cols=dff, k_total=d, tn=n_up, tk=d,
                     extras=(small["mlp_b_up"][layer:layer + 1],), extra_specs=(_vec_spec(n_up),), store=bias_store)[0]
        r, h = mix_ln(f"down{layer}", up, wts["down"][layer], dff, min(dff, 1024), h_in,
                      small["mlp_b_down"][layer:layer + 1], small["ln_mlp_g"][layer:layer + 1],
                      small["ln_mlp_b"][layer:layer + 1], a_fn=_relu2)
        return up, r, h

    r1, h1 = mix_ln("out_ab", [a_out, b_out], wts["out_ab"], s5w + sbw, min(s5w, sbw), h0, zero_bias,
                    small["ln_mix_g"][0:1], small["ln_mix_b"][0:1])
    wts = {**wts, **late_weights(r1)}
    n_c = wts["in_c"].shape[2]
    n_up = wts["up"][0].shape[2]
    up0, r2, h2 = mlp_fwd(0, h1)

    lb = _lower_bound("hg_lb", small["hgrn_gamma"])
    proj_c = _mm_act("in_c", h2, wts["in_c"], "stk", n_out_cols=4 * d, k_total=d, tn=n_c, tk=d)[0]
    c_out = _hgrn_fwd("hgrn", proj_c, lb, wts["ng"], n_batch, lp, pad, n_hg)
    r3, h3 = mix_ln("out_c", c_out, wts["out_c"], d, d, h2, zero_bias, small["ln_mix_g"][1:2], small["ln_mix_b"][1:2])
    up1, r4, h4 = mlp_fwd(1, h3)

    g_h4, loss_tile = _loss_grad("loss", h4, target, n_batch, lp, lead)

    gr = {}

    def res_store(outs, acc, g_res):
        outs[0][...] = acc + ALPHA * g_res

    def mlp_bwd(layer, g_h_out, r_out, up, h_in, dep=None):
        g_r, gr[f"ln_mlp_g{layer}"], gr[f"ln_mlp_b{layer}"] = _ln_bwd(
            f"ln_mlp_bwd{layer}", r_out, small["ln_mlp_g"][layer:layer + 1], small["ln_mlp_b"][layer:layer + 1], g_h_out,
            dep=dep)

        def gup_store(outs, acc, upv):
            outs[0][...] = acc * (2.0 * jnp.maximum(upv, 0.0))

        tf = min(dff, 1024)
        g_up = _mm_act(f"g_up{layer}", g_r, wts["down"][layer], "natT", n_out_cols=dff, k_total=d, tn=tf, tk=d,
                       extras=(up,), extra_specs=(_row_spec(tm, tf),), store=gup_store)[0]
        gr[f"down{layer}"], gr[f"mlp_b_down{layer}"] = _mm_wgrad(
            f"dw_down{layer}", up, g_r, kw=dff, n=d, tmw=tf, tn=d, a_fn=_relu2, out_dtype=BF16, colsum=True)
        gr[f"up{layer}"], gr[f"mlp_b_up{layer}"] = _mm_wgrad(
            f"dw_up{layer}", h_in, g_up, kw=d, n=dff, tmw=d, tn=min(dff, 2048), shard_cols=n_up, out_dtype=BF16, colsum=True)
        return _mm_act(f"g_hmid{layer}", g_up, wts["up"][layer], "stkT", n_out_cols=d, k_total=dff, tn=d, tk=n_up,
                       extras=(g_r,), extra_specs=(_row_spec(tm, d),), store=res_store)[0]

    g_h3 = mlp_bwd(1, g_h4, r4, up1, h3)
    g_r3, gr["ln_mix_g1"], gr["ln_mix_b1"] = _ln_bwd("ln_mix_bwd1", r3, small["ln_mix_g"][1:2], small["ln_mix_b"][1:2], g_h3)
    g_cout = _mm_act("g_cout", g_r3, wts["out_c"], "natT", n_out_cols=d, k_total=d, tn=d, tk=d)[0]
    gr["out_c"] = _mm_wgrad("dw_out_c", c_out, g_r3, kw=d, n=d, tmw=d, tn=d, out_dtype=BF16)
    gq, gf, gi, gg_, g_lb_parts, g_ng_parts = _hgrn_bwd("hgrn_bwd", proj_c, lb, wts["ng"], g_cout, n_batch, lp, pad, n_hg)
    g_pc = [gq, gf, gi, gg_]
    gr["hgrn_gamma"], gr["ng"] = _lower_bound_bwd("hg_lb_bwd", small["hgrn_gamma"], g_lb_parts, g_ng_parts)
    gr["in_c"] = _mm_wgrad("dw_in_c", h2, g_pc, kw=d, n=4 * d, tmw=d, tn=d, shard_cols=n_c, out_dtype=BF16)
    sent1 = on_grads(1, {"down1": gr["down1"], "up1": gr["up1"], "out_c": gr["out_c"], "in_c": gr["in_c"], "ng": gr["ng"]})
    g_h2 = _mm_act("g_h2", g_pc, wts["in_c"], "stkT", n_out_cols=d, k_total=4 * d, tn=d, tk=n_c,
                   extras=(g_r3,), extra_specs=(_row_spec(tm, d),), store=res_store)[0]

    g_h1 = mlp_bwd(0, g_h2, r2, up0, h1, dep=sent1)
    sent2 = on_grads(2, {"down0": gr["down0"], "up0": gr["up0"]})
    g_r1, gr["ln_mix_g0"], gr["ln_mix_b0"] = _ln_bwd("ln_mix_bwd0", r1, small["ln_mix_g"][0:1], small["ln_mix_b"][0:1], g_h1,
                                                     dep=sent2)
    g_cat = _mm_act("g_cat", g_r1, wts["out_ab"], "natT", n_out_cols=d, k_total=d, tn=d, tk=d)[0]
    gr["out_ab"] = _mm_wgrad("dw_out_ab", [a_out, b_out], g_r1, kw=s5w + sbw, n=d, tmw=min(s5w, sbw), tn=d, out_dtype=BF16)
    g_q, g_k, g_v = _attn_bwd("sb_attn_bwd", proj_ab, g_cat, n_batch, lp, pad, s5_cb, s5_cb + sb_cb, s5_cb + 2 * sb_cb,
                              s5_cb, n_pairs)

    g_y_direct, g_gate = _rowwise("s5_glu_bwd", lambda ga, yv, gt: jax.vjp(_glu, yv, gt)[1](ga),
                                  [(g_cat, 0, s5w), (y, 0, s5w), (gate, 0, s5w)], 2, s5w)

    def gelu_bwd_store(outs, acc, gyd, yp, u, dv):
        gyp = jax.vjp(jax.nn.gelu, yp)[1](acc + gyd)[0]
        outs[0][...] = gyp
        outs[1][...] = dv * gyp
        outs[2][...] = jnp.sum(gyp * u, axis=0, keepdims=True)

    rs = _row_spec(tm, s5w)
    g_ypre, g_u_direct, gd_parts = _mm_act(
        "s5_g_y", g_gate, wts["glu"], "natT", n_out_cols=s5w, k_total=s5w, tn=s5w, tk=s5w,
        extras=(g_y_direct, ypre, proj_ab, d_row), extra_specs=(rs, rs, rs, _vec_spec(s5w)), store=gelu_bwd_store,
        out_shape=[jax.ShapeDtypeStruct((rows, s5w), F32)] * 2 + [jax.ShapeDtypeStruct((rows // tm, 1, s5w), F32)],
        out_specs=[rs, rs, pl.BlockSpec((None, 1, s5w), lambda i, j, k: (i, 0, j))])
    gr["glu"], gr["s5_b_glu"] = _mm_wgrad("dw_glu", y, g_gate, kw=s5w, n=s5w, tmw=s5w, tn=s5w, out_dtype=BF16, colsum=True)
    g_sd = _mm_act("s5_g_states", g_ypre, c_blk, "natT", n_out_cols=2 * ns, k_total=s5w, tn=min(2 * ns, 2048), tk=s5w)[0]
    d_cblk = _mm_wgrad("dw_cblk", states, g_ypre, kw=2 * ns, n=s5w, tmw=min(2 * ns, 1024), tn=s5w)
    gs, gl_parts = _s5_scan_bwd("s5_scan_bwd", g_sd, states, lam_row, n_batch, lp, sw)

    def add_store(outs, acc, other):
        outs[0][...] = acc + other

    g_u = _mm_act("s5_g_u", gs, b_blk, "natT", n_out_cols=s5w, k_total=2 * ns, tn=s5w, tk=min(2 * ns, 1024),
                  extras=(g_u_direct,), extra_specs=(rs,), store=add_store)[0]
    d_bblk = _mm_wgrad("dw_bblk", proj_ab, gs, kw=s5w, n=2 * ns, tmw=s5w, tn=min(2 * ns, 2048))
    db_re, db_im = _deinterleave(d_bblk, sw)
    dc_re, dc_im = _deinterleave(d_cblk.T, sw)
    glr, gli = _deinterleave(gl_parts, sw)
    g_lam_re, g_lam_im, g_log_dt, g_b_re_t, g_b_im_t, g_d = _s5_params_bwd(
        "s5_params_bwd", lam_re, lam_im, log_dt, b_re_t, b_im_t,
        glr.reshape(n_batch, groups, n_state), gli.reshape(n_batch, groups, n_state),
        _diag_blocks(db_re, groups), _diag_blocks(db_im, groups), gd_parts)

    g_pab = [g_u, g_q, g_k, g_v]
    assert s5w == sbw
    gr["in_ab"] = _mm_wgrad("dw_in_ab", h0, g_pab, kw=d, n=ab_cols, tmw=d, tn=s5w, shard_cols=n_ab, out_dtype=BF16)
    g_h0 = _mm_act("g_h0", g_pab, wts["in_ab"], "stkT", n_out_cols=d, k_total=ab_cols, tn=d, tk=n_ab,
                   extras=(g_r1,), extra_specs=(_row_spec(tm, d),), store=res_store)[0]
    grad_x = g_h0.reshape(n_batch, lp, d)[:, lead:, :]
    g_meta = _meta_grad("g_meta", g_h0, n_batch, lp, pad, n_meta)

    cat2 = lambda key: jnp.concatenate([gr[key + "0"], gr[key + "1"]], axis=0)
    small_grads = {
        "s5_lam_re": g_lam_re[None], "s5_lam_im": g_lam_im[None], "s5_log_dt": g_log_dt.reshape(1, groups),
        "s5_b_re": g_b_re_t.transpose(0, 2, 1)[None], "s5_b_im": g_b_im_t.transpose(0, 2, 1)[None],
        "s5_c_re": _diag_blocks(dc_re, groups)[None], "s5_c_im": -_diag_blocks(dc_im, groups)[None],
        "s5_d": g_d.reshape(1, groups, grp), "s5_b_glu": gr["s5_b_glu"], "hgrn_gamma": gr["hgrn_gamma"],
        "ln_mix_g": cat2("ln_mix_g"), "ln_mix_b": cat2("ln_mix_b"), "mlp_b_up": cat2("mlp_b_up"),
        "mlp_b_down": cat2("mlp_b_down"), "ln_mlp_g": cat2("ln_mlp_g"), "ln_mlp_b": cat2("ln_mlp_b"),
    }
    on_grads(3, {"meta": g_meta, "in_ab": gr["in_ab"], "glu": gr["glu"], "out_ab": gr["out_ab"]})
    return loss_tile, grad_x, small_grads


SMALL_NAMES = ("s5_lam_re", "s5_lam_im", "s5_log_dt", "s5_b_re", "s5_b_im", "s5_c_re", "s5_c_im", "s5_d", "s5_b_glu",
               "hgrn_gamma", "ln_mix_g", "ln_mix_b", "mlp_b_up", "mlp_b_down", "ln_mlp_g", "ln_mlp_b")
WEIGHT_ORDER = ("meta", "w_in_ab", "s5_lam_re", "s5_lam_im", "s5_log_dt", "s5_b_re", "s5_b_im", "s5_c_re", "s5_c_im",
                "s5_d", "s5_w_glu", "s5_b_glu", "w_out_ab", "w_in_c", "hgrn_gamma", "hgrn_norm_g", "w_out_c", "ln_mix_g",
                "ln_mix_b", "mlp_w_up", "mlp_b_up", "mlp_w_down", "mlp_b_down", "ln_mlp_g", "ln_mlp_b")


def kernel(x, meta, w_in_ab, s5_lam_re, s5_lam_im, s5_log_dt, s5_b_re, s5_b_im, s5_c_re, s5_c_im, s5_d, s5_w_glu, s5_b_glu, w_out_ab, w_in_c, hgrn_gamma, hgrn_norm_g, w_out_c, ln_mix_g, ln_mix_b, mlp_w_up, mlp_b_up, mlp_w_down, mlp_b_down, ln_mlp_g, ln_mlp_b, loss_target, m_meta, m_w_in_ab, m_s5_lam_re, m_s5_lam_im, m_s5_log_dt, m_s5_b_re, m_s5_b_im, m_s5_c_re, m_s5_c_im, m_s5_d, m_s5_w_glu, m_s5_b_glu, m_w_out_ab, m_w_in_c, m_hgrn_gamma, m_hgrn_norm_g, m_w_out_c, m_ln_mix_g, m_ln_mix_b, m_mlp_w_up, m_mlp_b_up, m_mlp_w_down, m_mlp_b_down, m_ln_mlp_g, m_ln_mlp_b, v_meta, v_w_in_ab, v_s5_lam_re, v_s5_lam_im, v_s5_log_dt, v_s5_b_re, v_s5_b_im, v_s5_c_re, v_s5_c_im, v_s5_d, v_s5_w_glu, v_s5_b_glu, v_w_out_ab, v_w_in_c, v_hgrn_gamma, v_hgrn_norm_g, v_w_out_c, v_ln_mix_g, v_ln_mix_b, v_mlp_w_up, v_mlp_b_up, v_mlp_w_down, v_mlp_b_down, v_ln_mlp_g, v_ln_mlp_b):
    args = dict(locals())
    w = {n: args[n] for n in WEIGHT_ORDER}
    mom = {n: args["m_" + n] for n in WEIGHT_ORDER}
    var = {n: args["v_" + n] for n in WEIGHT_ORDER}
    d = x.shape[2]
    n_meta = meta.shape[0]

    cast = lambda a: a.astype(BF16)
    early = _exchange_start("gather_early_start", [w["meta"], cast(w["w_in_ab"][0]), cast(w["s5_w_glu"][0]),
                                                   cast(w["w_out_ab"][0])], False)
    late = _exchange_start("gather_late_start", [w["hgrn_norm_g"], cast(w["w_in_c"][0]), cast(w["w_out_c"][0]),
                                                 cast(w["mlp_w_up"][0]), cast(w["mlp_w_up"][1]),
                                                 cast(w["mlp_w_down"][0]), cast(w["mlp_w_down"][1])], False, dep=early["token"])
    a_meta, a_in_ab, a_glu, a_out_ab = _exchange_wait("gather_early_wait", early, late["token"])
    wts = {"in_ab": a_in_ab, "glu": a_glu.reshape(-1, a_glu.shape[2]), "out_ab": a_out_ab.reshape(-1, d)}
    meta_full = a_meta.transpose(1, 0, 2).reshape(n_meta, d)
    small = {n: w[n] for n in SMALL_NAMES}

    def late_weights(after):
        a_ng, a_in_c, a_out_c, a_up0, a_up1, a_dn0, a_dn1 = _exchange_wait("gather_late_wait", late, after)
        return {"in_c": a_in_c, "ng": a_ng.transpose(1, 0, 2).reshape(1, d), "out_c": a_out_c.reshape(-1, d),
                "up": [a_up0, a_up1], "down": [a_dn0.reshape(-1, d), a_dn1.reshape(-1, d)]}

    n_loc = d // N_DEV
    rows_of = lambda g: g.reshape(N_DEV, -1, g.shape[-1])
    cols_of = lambda g: g.reshape(g.shape[0], N_DEV, n_loc).transpose(1, 0, 2)
    sent = {}

    def on_grads(stage, g):
        if stage == 1:
            order = (("mlp_w_down", 1), ("mlp_w_up", 1), ("w_out_c", 0), ("w_in_c", 0), ("hgrn_norm_g", None))
            parts = [rows_of(g["down1"]), g["up1"], rows_of(g["out_c"]), g["in_c"], cols_of(g["ng"])]
        elif stage == 2:
            order = (("mlp_w_down", 0), ("mlp_w_up", 0))
            parts = [rows_of(g["down0"]), g["up0"]]
        else:
            order = (("w_out_ab", 0), ("s5_w_glu", 0), ("w_in_ab", 0), ("meta", None))
            parts = [rows_of(g["out_ab"]), rows_of(g["glu"]), g["in_ab"], cols_of(g["meta"])]
        sent[stage] = (order, _exchange_start(f"scatter_start{stage}", parts, True))
        return sent[stage][1]["token"]

    loss_tile, grad_x, sg = _local_step(x, loss_target, meta_full, wts, small, late_weights, on_grads)

    res = {}
    after = sent[3][1]["token"]
    for stage in (1, 2, 3):
        order, handle = sent[stage]
        recv = _exchange_wait(f"scatter_wait{stage}", handle, after)
        for (nm, ly), rc in zip(order, recv):
            sel = (lambda t: t) if ly is None else (lambda t, ly=ly: t[ly])
            res[(nm, ly)] = _adamw_summed(f"adamw_{nm}_{ly}", rc, sel(w[nm]), sel(mom[nm]), sel(var[nm]))
        after = recv[0]

    shapes = [w[n].shape for n in SMALL_NAMES] + [loss_tile.shape]
    zeros = jnp.zeros_like(loss_tile)
    g_pack = _pack_rows([sg[n] for n in SMALL_NAMES] + [loss_tile], PACK_COLS)
    w_pack = _pack_rows([w[n] for n in SMALL_NAMES] + [zeros], PACK_COLS)
    m_pack = _pack_rows([mom[n] for n in SMALL_NAMES] + [zeros], PACK_COLS)
    v_pack = _pack_rows([var[n] for n in SMALL_NAMES] + [zeros], PACK_COLS)
    g_all = _exchange("gather_small", [g_pack], False)[0]
    packed = _adamw_summed("adamw_small", g_all, w_pack, m_pack, v_pack)
    unpacked = [_unpack_rows(p, shapes, PACK_COLS) for p in packed]
    loss = unpacked[0][-1][0, 0]

    def pick(nm, which):
        if nm in SMALL_NAMES:
            return unpacked[which][SMALL_NAMES.index(nm)]
        if (nm, None) in res:
            return res[(nm, None)][which]
        return jnp.stack([res[(nm, ly)][which] for ly in range(w[nm].shape[0])], axis=0)

    return (loss, grad_x, *[pick(n, 0) for n in WEIGHT_ORDER], *[pick(n, 1) for n in WEIGHT_ORDER],
            *[pick(n, 2) for n in WEIGHT_ORDER], *[pick(n, 3) for n in WEIGHT_ORDER])
```

```python
import functools
import math

import jax
import jax.numpy as jnp
from jax import lax
from jax.experimental import pallas as pl
from jax.experimental.pallas import tpu as pltpu

F32 = jnp.float32
BF16 = jnp.bfloat16

N_DEV = 8
DEPTH = 2
ALPHA = (2.0 * DEPTH) ** 0.25
LN_EPS = 1e-5
RMS_EPS = 1e-6
SB_HEAD_DIM = 64
HG_DK = 128
HG_CHUNK = 64
LANES = 128
SUBLANES = 8
VMEM_LIMIT_BYTES = 56 * 1024 * 1024
ROW_TILE = 1088
SCAN_LANES = 256
PACK_COLS = 1024

ADAM_LR = 0.001
ADAM_B1 = 0.9
ADAM_B2 = 0.999
ADAM_EPS = 1e-08
ADAM_WD = 0.01
ADAM_STEP = 10

NN = (((1,), (0,)), ((), ()))
NT = (((1,), (1,)), ((), ()))
TN = (((0,), (0,)), ((), ()))


def _tile(n, pref, align=SUBLANES):
    t = min(n, pref)
    t -= t % align
    while t >= align:
        if n % t == 0:
            return t
        t -= align
    return n


def _params(sem):
    return pltpu.CompilerParams(dimension_semantics=sem, vmem_limit_bytes=VMEM_LIMIT_BYTES)


def _dot_raw(a, b, dims):
    return lax.dot_general(a.astype(BF16), b.astype(BF16), dims, preferred_element_type=F32)


def _make_dot(dims, da_rule, db_rule):
    @jax.custom_vjp
    def f(a, b):
        return _dot_raw(a, b, dims)

    def fwd(a, b):
        return _dot_raw(a, b, dims), (a, b)

    def bwd(res, g):
        a, b = res
        return da_rule(g, a, b), db_rule(g, a, b)

    f.defvjp(fwd, bwd)
    return f


_DOTS = {
    NN: _make_dot(NN, lambda g, a, b: _dot_raw(g, b, NT), lambda g, a, b: _dot_raw(a, g, TN)),
    NT: _make_dot(NT, lambda g, a, b: _dot_raw(g, b, NN), lambda g, a, b: _dot_raw(g, a, TN)),
    TN: _make_dot(TN, lambda g, a, b: _dot_raw(b, g, NT), lambda g, a, b: _dot_raw(a, g, NN)),
}


def _dot(a, b, dims):
    return _DOTS[dims](a, b)


def _running_sums(a, tri_ones, split=False):
    hi = a.astype(BF16)
    out = lax.dot_general(hi, tri_ones, NN, preferred_element_type=F32)
    if split:
        lo = (a - hi.astype(F32)).astype(BF16)
        out = out + lax.dot_general(lo, tri_ones, NN, preferred_element_type=F32)
    return out


def _piece_specs(pieces, block_rows, block_cols, row_of, col_of, cb0):
    per = pieces[0].shape[1] // block_cols if len(pieces) > 1 else None
    specs = []
    for p in range(len(pieces)):
        if per is None:
            specs.append(pl.BlockSpec((block_rows, block_cols), lambda *g: (row_of(*g), cb0 + col_of(*g))))
        else:
            specs.append(pl.BlockSpec(
                (block_rows, block_cols),
                lambda *g, p=p: (row_of(*g), jnp.clip(col_of(*g) - p * per, 0, per - 1))))
    return specs, per


def _mm_call(name, grid, dims, a_pieces, a_specs, a_sel, b_pieces, b_specs, b_sel, extras, extra_specs,
             out_shape, out_specs, acc_shape, a_fn, store, colsum_width=0):
    na, nb, ne, no = len(a_pieces), len(b_pieces), len(extras), len(out_shape)
    nk = grid[2]

    def body(*refs):
        a_refs, b_refs = refs[:na], refs[na:na + nb]
        extra = refs[na + nb:na + nb + ne]
        outs = refs[na + nb + ne:na + nb + ne + no]
        acc = refs[na + nb + ne + no]
        ids = (pl.program_id(0), pl.program_id(1), pl.program_id(2))
        k = ids[2]

        @pl.when(k == 0)
        def _():
            acc[...] = jnp.zeros_like(acc)

        def run(a_ref, b_ref):
            a = a_ref[...]
            if a_fn is not None:
                a = a_fn(a)
            b = b_ref[...]
            acc[...] += _dot_raw(a, b, dims)
            if colsum_width:
                cs = refs[-1]
                first = ids[1] == 0

                @pl.when(first & (k == 0))
                def _():
                    cs[...] = jnp.zeros_like(cs)

                @pl.when(first)
                def _():
                    cs[...] += jnp.sum(b.astype(F32), axis=0, keepdims=True)

        if na == 1 and nb == 1:
            run(a_refs[0], b_refs[0])
        elif nb == 1:
            per, fn = a_sel
            which = fn(*ids) // per
            for p in range(na):
                pl.when(which == p)(functools.partial(run, a_refs[p], b_refs[0]))
        else:
            assert na == 1
            per, fn = b_sel
            which = fn(*ids) // per
            for p in range(nb):
                pl.when(which == p)(functools.partial(run, a_refs[0], b_refs[p]))

        @pl.when(k == nk - 1)
        def _():
            store(outs, acc[...], *[e[...] for e in extra])
            if colsum_width:
                @pl.when(ids[1] == 0)
                def _():
                    outs[-1][...] = refs[-1][...]

    scratch = [pltpu.VMEM(acc_shape, F32)]
    if colsum_width:
        scratch.append(pltpu.VMEM((1, colsum_width), F32))
    sem = ("parallel", "arbitrary", "arbitrary") if colsum_width else ("parallel", "parallel", "arbitrary")
    return pl.pallas_call(
        body, name=name, grid=grid, in_specs=[*a_specs, *b_specs, *extra_specs], out_specs=out_specs,
        out_shape=out_shape, scratch_shapes=scratch, compiler_params=_params(sem),
    )(*a_pieces, *b_pieces, *extras)


def _store_plain(outs, acc):
    outs[0][...] = acc.astype(outs[0].dtype)


def _row_spec(tm, tn):
    return pl.BlockSpec((tm, tn), lambda i, j, k: (i, j))


def _vec_spec(tn):
    return pl.BlockSpec((1, tn), lambda i, j, k: (0, j))


def _mm_act(name, a, w, wkind, *, n_out_cols, k_total, tn, tk, a_cb0=0, a_fn=None, extras=(), extra_specs=(),
            store=_store_plain, out_shape=None, out_specs=None):
    a_pieces = list(a) if isinstance(a, (list, tuple)) else [a]
    rows = a_pieces[0].shape[0]
    tm = _tile(rows, ROW_TILE)
    grid = (rows // tm, n_out_cols // tn, k_total // tk)
    a_specs, per = _piece_specs(a_pieces, tm, tk, lambda i, j, k: i, lambda i, j, k: k, a_cb0)
    if wkind == "nat":
        b_spec, dims = pl.BlockSpec((tk, tn), lambda i, j, k: (k, j)), NN
    elif wkind == "stk":
        assert tn == w.shape[2]
        b_spec, dims = pl.BlockSpec((None, tk, tn), lambda i, j, k: (j, k, 0)), NN
    elif wkind == "natT":
        b_spec, dims = pl.BlockSpec((tn, tk), lambda i, j, k: (j, k)), NT
    else:
        assert wkind == "stkT" and tk == w.shape[2]
        b_spec, dims = pl.BlockSpec((None, tn, tk), lambda i, j, k: (k, j, 0)), NT
    if out_shape is None:
        out_shape = [jax.ShapeDtypeStruct((rows, n_out_cols), F32)]
        out_specs = [_row_spec(tm, tn)]
    return _mm_call(name, grid, dims, a_pieces, a_specs, (per, lambda i, j, k: k), [w], [b_spec], None,
                    list(extras), list(extra_specs), out_shape, out_specs, (tm, tn), a_fn, store)


def _mm_wgrad(name, a, g, *, kw, n, tmw, tn, a_cb0=0, a_fn=None, shard_cols=0, out_dtype=F32, colsum=False):
    a_pieces = list(a) if isinstance(a, (list, tuple)) else [a]
    g_pieces = list(g) if isinstance(g, (list, tuple)) else [g]
    rows = a_pieces[0].shape[0]
    tr = _tile(rows, ROW_TILE)
    grid = (n // tn, kw // tmw, rows // tr)
    a_specs, a_per = _piece_specs(a_pieces, tr, tmw, lambda j, i, k: k, lambda j, i, k: i, a_cb0)
    g_specs, g_per = _piece_specs(g_pieces, tr, tn, lambda j, i, k: k, lambda j, i, k: j, 0)
    if shard_cols:
        per = tn // shard_cols
        out_shape = [jax.ShapeDtypeStruct((n // shard_cols, kw, shard_cols), out_dtype)]
        out_specs = [pl.BlockSpec((per, tmw, shard_cols), lambda j, i, k: (j, i, 0))]

        def store(outs, acc):
            for q in range(per):
                outs[0][q] = acc[:, q * shard_cols:(q + 1) * shard_cols].astype(out_dtype)
    else:
        out_shape = [jax.ShapeDtypeStruct((kw, n), out_dtype)]
        out_specs = [pl.BlockSpec((tmw, tn), lambda j, i, k: (i, j))]

        def store(outs, acc):
            outs[0][...] = acc.astype(out_dtype)
    if colsum:
        out_shape.append(jax.ShapeDtypeStruct((1, n), F32))
        out_specs.append(pl.BlockSpec((1, tn), lambda j, i, k: (0, j)))
    res = _mm_call(name, grid, TN, a_pieces, a_specs, (a_per, lambda j, i, k: i), g_pieces, g_specs,
                   (g_per, lambda j, i, k: j), [], [], out_shape, out_specs, (tmw, tn), a_fn, store,
                   colsum_width=tn if colsum else 0)
    return res if colsum else res[0]


def _ln(x, g, b):
    mu = jnp.mean(x, axis=-1, keepdims=True)
    xc = x - mu
    var = jnp.mean(xc * xc, axis=-1, keepdims=True)
    return xc * lax.rsqrt(var + LN_EPS) * g + b


def _relu2(x):
    r = jnp.maximum(x.astype(F32), 0.0)
    return r * r


def _glu(y, gate):
    return y * jax.nn.sigmoid(gate)


def _ln_bwd(name, r, g, b, gy, dep=None):
    rows, d = r.shape
    tm = _tile(rows, ROW_TILE)
    deps = [] if dep is None else [dep]

    def body(r_ref, g_ref, b_ref, gy_ref, *rest):
        gr_ref, gg_ref, gb_ref = rest[len(deps):]
        _, vjp = jax.vjp(_ln, r_ref[...], g_ref[...], b_ref[...])
        gr, gg, gb = vjp(gy_ref[...])
        gr_ref[...] = gr

        @pl.when(pl.program_id(0) == 0)
        def _():
            gg_ref[...] = jnp.zeros_like(gg_ref)
            gb_ref[...] = jnp.zeros_like(gb_ref)

        gg_ref[...] += gg
        gb_ref[...] += gb

    row = pl.BlockSpec((tm, d), lambda i: (i, 0))
    vec = pl.BlockSpec((1, d), lambda i: (0, 0))
    return pl.pallas_call(
        body, name=name, grid=(rows // tm,),
        in_specs=[row, vec, vec, row] + [pl.BlockSpec(memory_space=pl.ANY)] * len(deps), out_specs=[row, vec, vec],
        out_shape=[jax.ShapeDtypeStruct((rows, d), F32), jax.ShapeDtypeStruct((1, d), F32),
                   jax.ShapeDtypeStruct((1, d), F32)],
        compiler_params=_params(("arbitrary",)),
    )(r, g, b, gy, *deps)


def _rowwise(name, fn, ins, n_out, width):
    rows = ins[0][0].shape[0]
    tm = _tile(rows, ROW_TILE)

    def body(*refs):
        res = fn(*[r[...] for r in refs[:len(ins)]])
        for o, v in zip(refs[len(ins):], res):
            o[...] = v

    return pl.pallas_call(
        body, name=name, grid=(rows // tm,),
        in_specs=[pl.BlockSpec((tm, wd), lambda i, cb=cb: (i, cb)) for _, cb, wd in ins],
        out_specs=[pl.BlockSpec((tm, width), lambda i: (i, 0))] * n_out,
        out_shape=[jax.ShapeDtypeStruct((rows, width), F32)] * n_out, compiler_params=_params(("parallel",)),
    )(*[a for a, _, _ in ins])


def _loss_grad(name, h, target, n_batch, lp, lead):
    rows, d = h.shape
    nq = lp // LANES
    lead_blocks = lead // LANES

    def body(h_ref, t_ref, g_ref, loss_ref):
        i = pl.program_id(1)

        @pl.when((pl.program_id(0) == 0) & (i == 0))
        def _():
            loss_ref[...] = jnp.zeros_like(loss_ref)

        diff = jnp.where(i >= lead_blocks, h_ref[...] - t_ref[...], 0.0)
        g_ref[...] = diff * (1.0 / d)
        loss_ref[...] += 0.5 * jnp.sum(diff * diff) * (1.0 / d)

    return pl.pallas_call(
        body, name=name, grid=(n_batch, nq),
        in_specs=[pl.BlockSpec((LANES, d), lambda b, i: (b * nq + i, 0)),
                  pl.BlockSpec((None, LANES, d), lambda b, i: (b, jnp.maximum(i - lead_blocks, 0), 0))],
        out_specs=[pl.BlockSpec((LANES, d), lambda b, i: (b * nq + i, 0)),
                   pl.BlockSpec((SUBLANES, LANES), lambda b, i: (0, 0))],
        out_shape=[jax.ShapeDtypeStruct((rows, d), F32), jax.ShapeDtypeStruct((SUBLANES, LANES), F32)],
        compiler_params=_params(("arbitrary", "arbitrary")),
    )(h, target)


def _meta_grad(name, g_h0, n_batch, lp, pad, n_meta):
    d = g_h0.shape[1]
    per = lp // n_meta
    at = pad // n_meta

    def body(g_ref, o_ref):
        @pl.when(pl.program_id(0) == 0)
        def _():
            o_ref[...] = jnp.zeros_like(o_ref)

        o_ref[...] += g_ref[...]

    return pl.pallas_call(
        body, name=name, grid=(n_batch,),
        in_specs=[pl.BlockSpec((n_meta, d), lambda b: (b * per + at, 0))],
        out_specs=pl.BlockSpec((n_meta, d), lambda b: (0, 0)),
        out_shape=jax.ShapeDtypeStruct((n_meta, d), F32),
        compiler_params=_params(("arbitrary",)),
    )(g_h0)


def _s5_param_fn(lr, li, ldt, br, bi):
    dt = jnp.exp(ldt)
    e = jnp.exp(lr * dt)
    w = li * dt
    lbr = e * jnp.cos(w)
    lbi = e * jnp.sin(w)
    nr = lbr - 1.0
    den = lr * lr + li * li
    cr = (nr * lr + lbi * li) / den
    ci = (lbi * lr - nr * li) / den
    bbr = cr[:, None, :] * br - ci[:, None, :] * bi
    bbi = cr[:, None, :] * bi + ci[:, None, :] * br
    return lbr, lbi, bbr, bbi


def _s5_params(name, lr, li, ldt, br, bi):
    def body(lr_ref, li_ref, ldt_ref, br_ref, bi_ref, o1, o2, o3, o4):
        res = _s5_param_fn(lr_ref[...], li_ref[...], ldt_ref[...], br_ref[...], bi_ref[...])
        for o, v in zip((o1, o2, o3, o4), res):
            o[...] = v

    shp = [jax.ShapeDtypeStruct(lr.shape, F32)] * 2 + [jax.ShapeDtypeStruct(br.shape, F32)] * 2
    return pl.pallas_call(body, name=name, out_shape=shp)(lr, li, ldt, br, bi)


def _s5_params_bwd(name, lr, li, ldt, br, bi, g_lbr, g_lbi, g_bbr, g_bbi, gd_parts):
    def body(lr_ref, li_ref, ldt_ref, br_ref, bi_ref, g1, g2, g3, g4, gd_ref, o1, o2, o3, o4, o5, o6):
        _, vjp = jax.vjp(_s5_param_fn, lr_ref[...], li_ref[...], ldt_ref[...], br_ref[...], bi_ref[...])
        res = vjp((jnp.sum(g1[...], axis=0), jnp.sum(g2[...], axis=0), g3[...], g4[...]))
        for o, v in zip((o1, o2, o3, o4, o5), res):
            o[...] = v
        o6[...] = jnp.sum(gd_ref[...], axis=0)

    shp = ([jax.ShapeDtypeStruct(lr.shape, F32)] * 2 + [jax.ShapeDtypeStruct(ldt.shape, F32)]
           + [jax.ShapeDtypeStruct(br.shape, F32)] * 2 + [jax.ShapeDtypeStruct(gd_parts.shape[1:], F32)])
    return pl.pallas_call(body, name=name, out_shape=shp)(lr, li, ldt, br, bi, g_lbr, g_lbi, g_bbr, g_bbi, gd_parts)


def _interleave(re, im, w):
    lead = re.shape[:-1]
    nj = re.shape[-1] // w
    return jnp.stack([re.reshape(*lead, nj, w), im.reshape(*lead, nj, w)], axis=-2).reshape(*lead, 2 * nj * w)


def _deinterleave(x, w):
    lead = x.shape[:-1]
    nj = x.shape[-1] // (2 * w)
    y = x.reshape(*lead, nj, 2, w)
    return y[..., 0, :].reshape(*lead, nj * w), y[..., 1, :].reshape(*lead, nj * w)


def _cmul(ar, ai, br, bi):
    return ar * br - ai * bi, ar * bi + ai * br


def _powers(lr, li):
    p = [(lr, li)]
    p.append(_cmul(*p[0], *p[0]))
    p.append(_cmul(*p[1], *p[0]))
    p.append(_cmul(*p[1], *p[1]))
    p.append(_cmul(*p[3], *p[0]))
    p.append(_cmul(*p[3], *p[1]))
    p.append(_cmul(*p[3], *p[2]))
    p.append(_cmul(*p[3], *p[3]))
    return p


def _scan_tile(xr, xi, steps):
    for sh, br, bi, m in steps:
        rr = jnp.where(m, pltpu.roll(xr, sh, 0), 0.0)
        ri = jnp.where(m, pltpu.roll(xi, sh, 0), 0.0)
        xr, xi = xr + (br * rr - bi * ri), xi + (br * ri + bi * rr)
    return xr, xi


def _s5_scan(name, bu, lam, n_batch, lp, w):
    rows, two_ns = bu.shape
    nj = two_ns // (2 * w)
    nt = lp // SUBLANES

    def body(x_ref, lam_ref, s_ref):
        pw = _powers(lam_ref[:, :w], lam_ref[:, w:])
        tab_r = jnp.concatenate([p[0] for p in pw], axis=0)
        tab_i = jnp.concatenate([p[1] for p in pw], axis=0)
        row = lax.broadcasted_iota(jnp.int32, (SUBLANES, w), 0)
        steps = [(s, jnp.broadcast_to(pw[s - 1][0], (SUBLANES, w)), jnp.broadcast_to(pw[s - 1][1], (SUBLANES, w)),
                  row >= s) for s in (1, 2, 4)]

        def tile(t, carry):
            cr, ci = carry
            r0 = pl.multiple_of(t * SUBLANES, SUBLANES)
            x = x_ref[pl.ds(r0, SUBLANES), :]
            xr, xi = _scan_tile(x[:, :w], x[:, w:], steps)
            sr = xr + (tab_r * cr - tab_i * ci)
            si = xi + (tab_r * ci + tab_i * cr)
            s_ref[pl.ds(r0, SUBLANES), :] = jnp.concatenate([sr, si], axis=1)
            return sr[SUBLANES - 1:, :], si[SUBLANES - 1:, :]

        zero = jnp.zeros((1, w), F32)
        lax.fori_loop(0, nt, tile, (zero, zero))

    spec = pl.BlockSpec((lp, 2 * w), lambda b, j: (b, j))
    return pl.pallas_call(
        body, name=name, grid=(n_batch, nj), in_specs=[spec, pl.BlockSpec((1, 2 * w), lambda b, j: (0, j))],
        out_specs=spec, out_shape=jax.ShapeDtypeStruct((rows, two_ns), F32),
        compiler_params=_params(("parallel", "parallel")),
    )(bu, lam)


def _s5_scan_bwd(name, gd, states, lam, n_batch, lp, w):
    rows, two_ns = gd.shape
    nj = two_ns // (2 * w)
    nt = lp // SUBLANES

    def body(x_ref, s_ref, lam_ref, g_ref, gl_ref):
        pw = _powers(lam_ref[:, :w], -lam_ref[:, w:])
        tab_r = jnp.concatenate([p[0] for p in reversed(pw)], axis=0)
        tab_i = jnp.concatenate([p[1] for p in reversed(pw)], axis=0)
        row = lax.broadcasted_iota(jnp.int32, (SUBLANES, w), 0)
        steps = [(SUBLANES - s, jnp.broadcast_to(pw[s - 1][0], (SUBLANES, w)),
                  jnp.broadcast_to(pw[s - 1][1], (SUBLANES, w)), row < SUBLANES - s) for s in (1, 2, 4)]

        def tile(u, carry):
            cr, ci, ar, ai = carry
            t = nt - 1 - u
            r0 = pl.multiple_of(t * SUBLANES, SUBLANES)
            x = x_ref[pl.ds(r0, SUBLANES), :]
            xr, xi = _scan_tile(x[:, :w], x[:, w:], steps)
            gr = xr + (tab_r * cr - tab_i * ci)
            gi = xi + (tab_r * ci + tab_i * cr)
            g_ref[pl.ds(r0, SUBLANES), :] = jnp.concatenate([gr, gi], axis=1)
            p0 = pl.multiple_of(jnp.maximum(t - 1, 0) * SUBLANES, SUBLANES)
            prev = s_ref[pl.ds(p0, SUBLANES), :][SUBLANES - 1:, :] * jnp.where(t > 0, 1.0, 0.0)
            cur = s_ref[pl.ds(r0, SUBLANES), :]
            spr = jnp.where(row >= 1, pltpu.roll(cur[:, :w], 1, 0), prev[:, :w])
            spi = jnp.where(row >= 1, pltpu.roll(cur[:, w:], 1, 0), prev[:, w:])
            return gr[:1, :], gi[:1, :], ar + gr * spr + gi * spi, ai + gi * spr - gr * spi

        z1 = jnp.zeros((1, w), F32)
        z8 = jnp.zeros((SUBLANES, w), F32)
        _, _, ar, ai = lax.fori_loop(0, nt, tile, (z1, z1, z8, z8))
        gl_ref[...] = jnp.concatenate([jnp.sum(ar, axis=0, keepdims=True), jnp.sum(ai, axis=0, keepdims=True)], axis=1)

    spec = pl.BlockSpec((lp, 2 * w), lambda b, j: (b, j))
    return pl.pallas_call(
        body, name=name, grid=(n_batch, nj),
        in_specs=[spec, spec, pl.BlockSpec((1, 2 * w), lambda b, j: (0, j))],
        out_specs=[spec, pl.BlockSpec((None, 1, 2 * w), lambda b, j: (b, 0, j))],
        out_shape=[jax.ShapeDtypeStruct((rows, two_ns), F32), jax.ShapeDtypeStruct((n_batch, 1, two_ns), F32)],
        compiler_params=_params(("parallel", "parallel")),
    )(gd, states, lam)


def _log_sigmoid(z):
    return jnp.minimum(z, 0.0) - jnp.log(1.0 + jnp.exp(-jnp.abs(z)))


def _attn_masks(i, j, pad):
    rowpos = i * LANES + lax.broadcasted_iota(jnp.int32, (LANES, LANES), 0)
    colpos = j * LANES + lax.broadcasted_iota(jnp.int32, (LANES, LANES), 1)
    return (colpos < rowpos) & (colpos >= pad)


ATTN_GROUP = 4


def _tri_ones(strict_upper):
    r = lax.broadcasted_iota(jnp.int32, (LANES, 2 * LANES), 0)
    c = lax.broadcasted_iota(jnp.int32, (LANES, 2 * LANES), 1)
    tri = (r > c) if strict_upper else (r < c)
    return jnp.where((c >= LANES) | tri, 1.0, 0.0).astype(BF16)


def _head_masks():
    lane = lax.broadcasted_iota(jnp.int32, (1, LANES), 1)
    return [lane < SB_HEAD_DIM, lane >= SB_HEAD_DIM]


def _run_groups(n, first, sign, make):
    j, left, g = first, n, ATTN_GROUP
    while g >= 1:
        shift = g.bit_length() - 1
        count = lax.shift_right_logical(left, shift)
        fn = make(g)

        def loop(_, jcur, fn=fn, g=g):
            fn(jcur)
            return jcur + sign * g

        j = lax.fori_loop(0, count, loop, j)
        left = left - lax.shift_left(count, shift)
        g //= 2


def _attn_fwd(name, proj, n_batch, lp, pad, q_cb, k_cb, v_cb, n_pairs):
    rows = proj.shape[0]
    nq = lp // LANES
    scale = SB_HEAD_DIM ** -0.5

    def body(q_ref, k_ref, v_ref, o_ref, acc_s):
        i = pl.program_id(2)
        hm = _head_masks()
        comb = _tri_ones(True)
        qs = q_ref[...] * scale
        qh = [jnp.where(m, qs, 0.0).astype(BF16) for m in hm]
        acc_s[...] = jnp.zeros_like(acc_s)
        o_ref[...] = jnp.zeros_like(o_ref)

        def make(group):
            def fn(jtop):
                pend = []
                for g in range(group):
                    j = jtop - g
                    r0 = pl.multiple_of(j * LANES, LANES)
                    kj = k_ref[pl.ds(r0, LANES), :].astype(BF16)
                    vj = v_ref[pl.ds(r0, LANES), :]
                    vis = _attn_masks(i, j, pad)
                    for h in range(2):
                        z = lax.dot_general(qh[h], kj, NT, preferred_element_type=F32)
                        lsz = _log_sigmoid(z)
                        cr = _running_sums(jnp.where(vis, lsz - z, 0.0), comb, split=True)
                        pend.append((h, vis, lsz, cr, jnp.where(hm[h], vj, 0.0).astype(BF16)))
                for h, vis, lsz, cr, vh in pend:
                    acc = acc_s[h]
                    wgt = jnp.where(vis, jnp.exp(lsz + cr[:, :LANES] + acc), 0.0)
                    acc_s[h] = acc + cr[:, LANES:]
                    o_ref[...] += lax.dot_general(wgt.astype(BF16), vh, NN, preferred_element_type=F32)
            return fn

        _run_groups(i + 1, i, -1, make)

    return pl.pallas_call(
        body, name=name, grid=(n_batch, n_pairs, nq),
        in_specs=[pl.BlockSpec((LANES, LANES), lambda b, h, i: (b * nq + i, q_cb + h)),
                  pl.BlockSpec((lp, LANES), lambda b, h, i: (b, k_cb + h)),
                  pl.BlockSpec((lp, LANES), lambda b, h, i: (b, v_cb + h))],
        out_specs=pl.BlockSpec((LANES, LANES), lambda b, h, i: (b * nq + i, h)),
        out_shape=jax.ShapeDtypeStruct((rows, n_pairs * LANES), F32),
        scratch_shapes=[pltpu.VMEM((2, LANES, LANES), F32)],
        compiler_params=_params(("parallel", "parallel", "arbitrary")),
    )(proj, proj, proj)


def _attn_bwd(name, proj, g_out, n_batch, lp, pad, q_cb, k_cb, v_cb, go_cb, n_pairs):
    rows = proj.shape[0]
    nq = lp // LANES
    scale = SB_HEAD_DIM ** -0.5

    def body(q_ref, k_ref, v_ref, go_ref, gq_ref, gk_ref, gv_ref, ga_s, sz_s, acc_s):
        i = pl.program_id(2)

        @pl.when(i == 0)
        def _():
            gk_ref[...] = jnp.zeros_like(gk_ref)
            gv_ref[...] = jnp.zeros_like(gv_ref)

        hm = _head_masks()
        comb_up = _tri_ones(True)
        comb_lo = _tri_ones(False)
        qs = q_ref[...] * scale
        go = go_ref[...]
        qh = [jnp.where(m, qs, 0.0).astype(BF16) for m in hm]
        goh = [jnp.where(m, go, 0.0).astype(BF16) for m in hm]
        acc_s[...] = jnp.zeros_like(acc_s)
        gq_ref[...] = jnp.zeros_like(gq_ref)

        def make_down(group):
            def fn(jtop):
                pend = []
                for g in range(group):
                    j = jtop - g
                    r0 = pl.multiple_of(j * LANES, LANES)
                    kj = k_ref[pl.ds(r0, LANES), :].astype(BF16)
                    vj = v_ref[pl.ds(r0, LANES), :].astype(BF16)
                    vis = _attn_masks(i, j, pad)
                    for h in range(2):
                        z = lax.dot_general(qh[h], kj, NT, preferred_element_type=F32)
                        lsz = _log_sigmoid(z)
                        cr = _running_sums(jnp.where(vis, lsz - z, 0.0), comb_up)
                        gw = lax.dot_general(goh[h], vj, NT, preferred_element_type=F32)
                        pend.append((h, j, r0, vis, lsz, cr, gw))
                for h, j, r0, vis, lsz, cr, gw in pend:
                    acc = acc_s[h]
                    wgt = jnp.where(vis, jnp.exp(lsz + cr[:, :LANES] + acc), 0.0)
                    acc_s[h] = acc + cr[:, LANES:]
                    ga_s[h, j] = gw * wgt
                    sz_s[h, j] = jnp.exp(lsz)
                    gv_ref[pl.ds(r0, LANES), :] += lax.dot_general(wgt.astype(BF16), goh[h], TN, preferred_element_type=F32)
            return fn

        _run_groups(i + 1, i, -1, make_down)
        acc_s[...] = jnp.zeros_like(acc_s)

        def make_up(group):
            def fn(jbot):
                pend = []
                for g in range(group):
                    j = jbot + g
                    r0 = pl.multiple_of(j * LANES, LANES)
                    kj = k_ref[pl.ds(r0, LANES), :]
                    vis = _attn_masks(i, j, pad)
                    for h in range(2):
                        ga = ga_s[h, j]
                        pend.append((h, j, r0, vis, ga, _running_sums(ga, comb_lo), jnp.where(hm[h], kj, 0.0).astype(BF16)))
                for h, j, r0, vis, ga, cr, kh in pend:
                    pre = acc_s[h]
                    glk = cr[:, :LANES] + pre
                    acc_s[h] = pre + cr[:, LANES:]
                    sz = sz_s[h, j]
                    gz = jnp.where(vis, ga * (1.0 - sz) - glk * sz, 0.0).astype(BF16)
                    gq_ref[...] += lax.dot_general(gz, kh, NN, preferred_element_type=F32)
                    gk_ref[pl.ds(r0, LANES), :] += lax.dot_general(gz, qh[h], TN, preferred_element_type=F32)
            return fn

        _run_groups(i + 1, 0, 1, make_up)
        gq_ref[...] = gq_ref[...] * scale

    blk = lambda cb: pl.BlockSpec((LANES, LANES), lambda b, h, i: (b * nq + i, cb + h))
    full = lambda cb: pl.BlockSpec((lp, LANES), lambda b, h, i: (b, cb + h))
    shp = jax.ShapeDtypeStruct((rows, n_pairs * LANES), F32)
    per_block = pltpu.VMEM((2, nq, LANES, LANES), F32)
    return pl.pallas_call(
        body, name=name, grid=(n_batch, n_pairs, nq),
        in_specs=[blk(q_cb), full(k_cb), full(v_cb), blk(go_cb)],
        out_specs=[blk(0), full(0), full(0)], out_shape=[shp, shp, shp],
        scratch_shapes=[per_block, per_block, pltpu.VMEM((2, LANES, LANES), F32)],
        compiler_params=_params(("parallel", "parallel", "arbitrary")),
    )(proj, proj, proj, g_out)


def _lb_fn(gamma):
    g0, g1 = gamma[0:1, :], gamma[1:2, :]
    mx = jnp.maximum(g0, g1)
    e0, e1 = jnp.exp(g0 - mx), jnp.exp(g1 - mx)
    p0, p1 = e0 / (e0 + e1), e1 / (e0 + e1)
    return (p0 + p1) - p0


def _lower_bound(name, gamma):
    def body(g_ref, o_ref):
        o_ref[...] = _lb_fn(g_ref[...])

    return pl.pallas_call(body, name=name, out_shape=jax.ShapeDtypeStruct((1, gamma.shape[1]), F32))(gamma)


def _lower_bound_bwd(name, gamma, g_lb_parts, g_ng_parts):
    def body(g_ref, glb_ref, gng_ref, o_ref, o2_ref):
        _, vjp = jax.vjp(_lb_fn, g_ref[...])
        o_ref[...] = vjp(jnp.sum(glb_ref[...], axis=0))[0]
        o2_ref[...] = jnp.sum(gng_ref[...], axis=0)

    return pl.pallas_call(
        body, name=name,
        out_shape=[jax.ShapeDtypeStruct(gamma.shape, F32), jax.ShapeDtypeStruct((1, gamma.shape[1]), F32)],
    )(gamma, g_lb_parts, g_ng_parts)


def _tri_times(tril, x, dims):
    hi = x.astype(BF16)
    lo = (x - hi.astype(F32)).astype(BF16)
    t = tril.astype(BF16)
    return (lax.dot_general(t, hi, dims, preferred_element_type=F32)
            + lax.dot_general(t, lo, dims, preferred_element_type=F32))


@jax.custom_vjp
def _cumsum_rows(x, tril):
    return _tri_times(tril, x, NN)


def _cumsum_rows_fwd(x, tril):
    return _tri_times(tril, x, NN), tril


def _cumsum_rows_bwd(tril, g):
    return _tri_times(tril, g, TN), jnp.zeros_like(tril)


_cumsum_rows.defvjp(_cumsum_rows_fwd, _cumsum_rows_bwd)


def _hg_gates(fc, lb, rowmask, tril):
    f = lb + (1.0 - lb) * jax.nn.sigmoid(fc)
    return 1.0 - f, _cumsum_rows(jnp.log(f) * rowmask, tril)


def _hg_state(fc, ic, lb, st, rowmask, tril):
    k, bcum = _hg_gates(fc, lb, rowmask, tril)
    blast = bcum[HG_CHUNK - 1:, :]
    return jnp.exp(blast) * st + _dot(ic * rowmask, k * jnp.exp(blast - bcum), TN)


def _hg_chunk(qc, fc, ic, gc, lb, ng, st, rowmask, tril):
    k, bcum = _hg_gates(fc, lb, rowmask, tril)
    blast = bcum[HG_CHUNK - 1:, :]
    v = ic * rowmask
    qd = qc * jnp.exp(bcum)
    scores = jnp.where(tril > 0.5, _dot(qd, k * jnp.exp(-bcum), NT), 0.0)
    o = _dot(scores, v, NN) + _dot(qd, st, NT)
    st_new = jnp.exp(blast) * st + _dot(v, k * jnp.exp(blast - bcum), TN)
    o = o * lax.rsqrt(jnp.mean(o * o, axis=-1, keepdims=True) + RMS_EPS) * ng
    return o * (gc * jax.nn.sigmoid(gc)), st_new


def _hg_consts(c, pad):
    r = lax.broadcasted_iota(jnp.int32, (HG_CHUNK, HG_CHUNK), 0)
    cc = lax.broadcasted_iota(jnp.int32, (HG_CHUNK, HG_CHUNK), 1)
    tril = jnp.where(r >= cc, 1.0, 0.0).astype(F32)
    pos = c * HG_CHUNK + lax.broadcasted_iota(jnp.int32, (HG_CHUNK, 1), 0)
    return tril, jnp.where(pos >= pad, 1.0, 0.0).astype(F32)


HG_HEADS_PER_STEP = 2
HG_UNROLL = 2


def _chunk_loop(n_chunks, body, init):
    assert n_chunks % HG_UNROLL == 0

    def outer(t, carry):
        for u in range(HG_UNROLL):
            carry = body(t * HG_UNROLL + u, carry)
        return carry

    return lax.fori_loop(0, n_chunks // HG_UNROLL, outer, init)


def _hg_head_cols():
    return [slice(h * HG_DK, (h + 1) * HG_DK) for h in range(HG_HEADS_PER_STEP)]


def _hg_specs(lp, n_heads):
    wide = HG_HEADS_PER_STEP * HG_DK
    groups = n_heads // HG_HEADS_PER_STEP
    col = lambda off: pl.BlockSpec((lp, wide), lambda b, h: (b, off * groups + h))
    vec = pl.BlockSpec((1, wide), lambda b, h: (0, h))
    return groups, col, vec


def _hgrn_fwd(name, proj, lb, ng, n_batch, lp, pad, n_heads):
    rows = proj.shape[0]
    nc = lp // HG_CHUNK
    groups, col, vec = _hg_specs(lp, n_heads)

    def body(q_ref, f_ref, i_ref, g_ref, lb_ref, ng_ref, o_ref):
        def chunk(c, sts):
            sl = pl.ds(pl.multiple_of(c * HG_CHUNK, HG_CHUNK), HG_CHUNK)
            tril, rowmask = _hg_consts(c, pad)
            new = []
            for cols, st in zip(_hg_head_cols(), sts):
                o, st = _hg_chunk(q_ref[sl, cols], f_ref[sl, cols], i_ref[sl, cols], g_ref[sl, cols],
                                  lb_ref[:, cols], ng_ref[:, cols], st, rowmask, tril)
                o_ref[sl, cols] = o
                new.append(st)
            return tuple(new)

        zero = jnp.zeros((HG_DK, HG_DK), F32)
        _chunk_loop(nc, chunk, (zero,) * HG_HEADS_PER_STEP)

    return pl.pallas_call(
        body, name=name, grid=(n_batch, groups), in_specs=[col(0), col(1), col(2), col(3), vec, vec],
        out_specs=col(0), out_shape=jax.ShapeDtypeStruct((rows, n_heads * HG_DK), F32),
        compiler_params=_params(("parallel", "parallel")),
    )(proj, proj, proj, proj, lb, ng)


def _hgrn_bwd(name, proj, lb, ng, g_out, n_batch, lp, pad, n_heads):
    rows = proj.shape[0]
    width = n_heads * HG_DK
    nc = lp // HG_CHUNK
    groups, col, vec = _hg_specs(lp, n_heads)

    def body(q_ref, f_ref, i_ref, g_ref, lb_ref, ng_ref, go_ref, gq_ref, gf_ref, gi_ref, gg_ref, glb_ref, gng_ref, st_s):
        heads = list(enumerate(_hg_head_cols()))

        def fwd(c, sts):
            sl = pl.ds(pl.multiple_of(c * HG_CHUNK, HG_CHUNK), HG_CHUNK)
            tril, rowmask = _hg_consts(c, pad)
            new = []
            for (h, cols), st in zip(heads, sts):
                st_s[h, c] = st
                new.append(_hg_state(f_ref[sl, cols], i_ref[sl, cols], lb_ref[:, cols], st, rowmask, tril))
            return tuple(new)

        zero = jnp.zeros((HG_DK, HG_DK), F32)
        _chunk_loop(nc, fwd, (zero,) * HG_HEADS_PER_STEP)

        def bwd(u, carry):
            c = nc - 1 - u
            sl = pl.ds(pl.multiple_of(c * HG_CHUNK, HG_CHUNK), HG_CHUNK)
            tril, rowmask = _hg_consts(c, pad)
            fn = functools.partial(_hg_chunk, rowmask=rowmask, tril=tril)
            new = []
            for (h, cols), (gst, glb, gng) in zip(heads, carry):
                _, vjp = jax.vjp(fn, q_ref[sl, cols], f_ref[sl, cols], i_ref[sl, cols], g_ref[sl, cols],
                                 lb_ref[:, cols], ng_ref[:, cols], st_s[h, c])
                gq, gf, gi, gg, dlb, dng, gst = vjp((go_ref[sl, cols], gst))
                gq_ref[sl, cols] = gq.astype(BF16)
                gf_ref[sl, cols] = gf.astype(BF16)
                gi_ref[sl, cols] = gi.astype(BF16)
                gg_ref[sl, cols] = gg.astype(BF16)
                new.append((gst, glb + dlb, gng + dng))
            return tuple(new)

        zv = jnp.zeros((1, HG_DK), F32)
        res = _chunk_loop(nc, bwd, ((zero, zv, zv),) * HG_HEADS_PER_STEP)
        for (_, cols), (_, glb, gng) in zip(heads, res):
            glb_ref[:, cols] = glb
            gng_ref[:, cols] = gng

    part = pl.BlockSpec((None, 1, HG_HEADS_PER_STEP * HG_DK), lambda b, h: (b, 0, h))
    big = jax.ShapeDtypeStruct((rows, width), BF16)
    small = jax.ShapeDtypeStruct((n_batch, 1, width), F32)
    return pl.pallas_call(
        body, name=name, grid=(n_batch, groups),
        in_specs=[col(0), col(1), col(2), col(3), vec, vec, col(0)],
        out_specs=[col(0), col(0), col(0), col(0), part, part],
        out_shape=[big, big, big, big, small, small],
        scratch_shapes=[pltpu.VMEM((HG_HEADS_PER_STEP, nc, HG_DK, HG_DK), F32)],
        compiler_params=_params(("parallel", "parallel")),
    )(proj, proj, proj, proj, lb, ng, g_out)


def _exchange_copies(src, dst, send, recv, loc, scatter):
    x, y, c = lax.axis_index("x"), lax.axis_index("y"), lax.axis_index("c")
    me = 4 * x + 2 * y + c
    local, remote = [], []
    for w in range(len(src)):
        local.append(pltpu.make_async_copy(src[w].at[me] if scatter else src[w], dst[w].at[me], loc.at[w]))
    for k in range(1, N_DEV):
        px = 1 - x if k & 4 else x
        py = 1 - y if k & 2 else y
        pc = 1 - c if k & 1 else c
        peer = 4 * px + 2 * py + pc
        for w in range(len(src)):
            remote.append(pltpu.make_async_remote_copy(
                src_ref=src[w].at[peer] if scatter else src[w], dst_ref=dst[w].at[me],
                send_sem=send.at[w * (N_DEV - 1) + k - 1], recv_sem=recv.at[w * (N_DEV - 1) + k - 1],
                device_id=(px, py, pc), device_id_type=pl.DeviceIdType.MESH))
    return local, remote


_HBM_SPEC = pl.BlockSpec(memory_space=pltpu.HBM)
_SEM_SPEC = pl.BlockSpec(memory_space=pltpu.SEMAPHORE)
_ANY_SPEC = pl.BlockSpec(memory_space=pl.ANY)
_DATAFLOW = pltpu.SideEffectType.DATAFLOW_SIDE_EFFECTING


def _exchange_start(name, srcs, scatter, dep=None):
    nw = len(srcs)
    srcs = [pltpu.with_memory_space_constraint(s, pltpu.HBM) for s in srcs]
    lands = [pltpu.with_memory_space_constraint(lax.empty(s.shape if scatter else (N_DEV,) + s.shape, s.dtype), pltpu.HBM)
             for s in srcs]
    deps = [] if dep is None else [dep]

    def body(*refs):
        src, dst = refs[:nw], refs[nw:2 * nw]
        send, recv, loc = refs[2 * nw + len(deps):2 * nw + len(deps) + 3]
        token = refs[-1]
        local, remote = _exchange_copies(src, dst, send, recv, loc, scatter)
        for cp in local + remote:
            cp.start()
        token[...] = jnp.zeros_like(token)

    sems = [pltpu.SemaphoreType.DMA((nw * (N_DEV - 1),)), pltpu.SemaphoreType.DMA((nw * (N_DEV - 1),)),
            pltpu.SemaphoreType.DMA((nw,))]
    out = pl.pallas_call(
        body, name=name,
        out_shape=(*sems, *[pltpu.HBM(s.shape, s.dtype) for s in srcs], *[pltpu.HBM(s.shape, s.dtype) for s in lands],
                   jax.ShapeDtypeStruct((SUBLANES, LANES), F32)),
        in_specs=[_HBM_SPEC] * (2 * nw) + [_ANY_SPEC] * len(deps),
        out_specs=(_SEM_SPEC, _SEM_SPEC, _SEM_SPEC, *[_HBM_SPEC] * (2 * nw), pl.BlockSpec(memory_space=pltpu.VMEM)),
        input_output_aliases={i: 3 + i for i in range(2 * nw)},
        compiler_params=pltpu.CompilerParams(has_side_effects=_DATAFLOW),
    )(*srcs, *lands, *deps)
    return {"sems": out[:3], "srcs": out[3:3 + nw], "lands": out[3 + nw:3 + 2 * nw], "token": out[-1], "scatter": scatter}


def _exchange_wait(name, handle, after):
    nw = len(handle["srcs"])
    scatter = handle["scatter"]

    def body(*refs):
        src, dst = refs[:nw], refs[nw:2 * nw]
        send, recv, loc = refs[2 * nw:2 * nw + 3]
        local, remote = _exchange_copies(src, dst, send, recv, loc, scatter)
        for cp in local:
            cp.wait()
        for cp in remote:
            cp.wait_send()
            cp.wait_recv()

    out = pl.pallas_call(
        body, name=name,
        out_shape=(*[pltpu.HBM(s.shape, s.dtype) for s in handle["srcs"]],
                   *[pltpu.HBM(s.shape, s.dtype) for s in handle["lands"]]),
        in_specs=[_HBM_SPEC] * (2 * nw) + [_SEM_SPEC] * 3 + [_ANY_SPEC],
        out_specs=tuple([_HBM_SPEC] * (2 * nw)),
        input_output_aliases={i: i for i in range(2 * nw)},
        compiler_params=pltpu.CompilerParams(has_side_effects=_DATAFLOW),
    )(*handle["srcs"], *handle["lands"], *handle["sems"], after)
    return list(out[nw:])


def _adamw(w, g, m, v):
    m = ADAM_B1 * m + (1.0 - ADAM_B1) * g
    v = ADAM_B2 * v + (1.0 - ADAM_B2) * (g * g)
    m_hat = m / (1.0 - ADAM_B1 ** ADAM_STEP)
    v_hat = v / (1.0 - ADAM_B2 ** ADAM_STEP)
    delta = -ADAM_LR * (m_hat / (jnp.sqrt(v_hat) + ADAM_EPS) + ADAM_WD * w)
    return delta, m, v


def _adamw_summed(name, parts, w, m, v):
    rows, cols = w.shape
    n_parts = parts.shape[0]
    tr = _tile(rows, max(SUBLANES, (1 << 18) // cols))

    def body(p_ref, w_ref, m_ref, v_ref, g_ref, d_ref, nm_ref, nv_ref):
        g = p_ref[0].astype(F32)
        for s in range(1, n_parts):
            g = g + p_ref[s].astype(F32)
        d, nm, nv = _adamw(w_ref[...], g, m_ref[...], v_ref[...])
        g_ref[...] = g
        d_ref[...] = d
        nm_ref[...] = nm
        nv_ref[...] = nv

    spec = pl.BlockSpec((tr, cols), lambda i: (i, 0))
    shp = jax.ShapeDtypeStruct((rows, cols), F32)
    return pl.pallas_call(
        body, name=name, grid=(rows // tr,),
        in_specs=[pl.BlockSpec((n_parts, tr, cols), lambda i: (0, i, 0)), spec, spec, spec],
        out_specs=[spec] * 4, out_shape=[shp] * 4, compiler_params=_params(("parallel",)),
    )(parts, w, m, v)


def _pack_rows(arrays, cols):
    out = []
    for a in arrays:
        flat = a.reshape(-1)
        n = -(-flat.shape[0] // cols) * cols
        out.append(jnp.pad(flat, (0, n - flat.shape[0])).reshape(-1, cols))
    packed = jnp.concatenate(out, axis=0)
    return jnp.pad(packed, ((0, -packed.shape[0] % SUBLANES), (0, 0)))


def _unpack_rows(packed, shapes, cols):
    out, r = [], 0
    for s in shapes:
        n = math.prod(s)
        nr = -(-n // cols)
        out.append(packed[r:r + nr].reshape(-1)[:n].reshape(s))
        r += nr
    return out


def _block_diag(blocks):
    g, a, b = blocks.shape
    eye = jnp.eye(g, dtype=blocks.dtype)
    return (eye[:, None, :, None] * blocks[:, :, None, :]).reshape(g * a, g * b)


def _diag_blocks(dense, g):
    a, b = dense.shape[0] // g, dense.shape[1] // g
    return jnp.einsum("gagb->gab", dense.reshape(g, a, g, b))


def _local_step(x, target, meta, wts, small, late_weights, on_grads):
    n_batch, seq, d = x.shape
    n_meta = meta.shape[0]
    pad = -(seq + n_meta) % LANES
    lead = pad + n_meta
    lp = lead + seq
    rows = n_batch * lp
    s5w = wts["glu"].shape[0]
    n_ab = wts["in_ab"].shape[2]
    ab_cols = wts["in_ab"].shape[0] * n_ab
    sbw = (ab_cols - s5w) // 3
    dff = small["mlp_b_up"].shape[1]
    n_pairs = sbw // LANES
    n_hg = d // HG_DK
    s5_cb = s5w // LANES
    sb_cb = sbw // LANES
    tm = _tile(rows, ROW_TILE)
    groups, n_state, grp = small["s5_b_re"].shape[1:]
    ns = groups * n_state
    sw = min(SCAN_LANES, ns)

    h0 = jnp.concatenate(
        [jnp.zeros((n_batch, pad, d), F32), jnp.broadcast_to(meta[None], (n_batch, n_meta, d)), x], axis=1
    ).reshape(rows, d)

    lam_re, lam_im = small["s5_lam_re"][0], small["s5_lam_im"][0]
    log_dt = small["s5_log_dt"][0][:, None]
    b_re_t = small["s5_b_re"][0].transpose(0, 2, 1)
    b_im_t = small["s5_b_im"][0].transpose(0, 2, 1)
    c_re, c_im = small["s5_c_re"][0], small["s5_c_im"][0]
    lbr, lbi, bbr, bbi = _s5_params("s5_params", lam_re, lam_im, log_dt, b_re_t, b_im_t)
    b_blk = _interleave(_block_diag(bbr), _block_diag(bbi), sw).astype(BF16)
    c_blk = _interleave(_block_diag(c_re), _block_diag(-c_im), sw).T.astype(BF16)
    lam_row = _interleave(lbr.reshape(1, ns), lbi.reshape(1, ns), sw)
    d_row = small["s5_d"].reshape(1, s5w)

    def ln_store(outs, acc, res, bias, g, b):
        r = ALPHA * res + acc + bias
        outs[0][...] = r
        outs[1][...] = _ln(r, g, b)

    zero_bias = jnp.zeros((1, d), F32)

    def mix_ln(name, a, w, k_total, tk, res, bias, g, b, a_fn=None):
        return _mm_act(name, a, w, "nat", n_out_cols=d, k_total=k_total, tn=d, tk=tk, a_fn=a_fn,
                       extras=(res, bias, g, b), extra_specs=(_row_spec(tm, d), _vec_spec(d), _vec_spec(d), _vec_spec(d)),
                       store=ln_store, out_shape=[jax.ShapeDtypeStruct((rows, d), F32)] * 2,
                       out_specs=[_row_spec(tm, d)] * 2)

    def two(width):
        return [jax.ShapeDtypeStruct((rows, width), F32)] * 2, [_row_spec(tm, width)] * 2

    proj_ab = _mm_act("in_ab", h0, wts["in_ab"], "stk", n_out_cols=ab_cols, k_total=d, tn=n_ab, tk=d)[0]
    bu = _mm_act("s5_bu", proj_ab, b_blk, "nat", n_out_cols=2 * ns, k_total=s5w, tn=min(2 * ns, 2048), tk=s5w)[0]
    states = _s5_scan("s5_scan", bu, lam_row, n_batch, lp, sw)

    def gelu_store(outs, acc, u, dv):
        ypre = acc + dv * u
        outs[0][...] = ypre
        outs[1][...] = jax.nn.gelu(ypre)

    shp2, spec2 = two(s5w)
    ypre, y = _mm_act(
        "s5_y", states, c_blk, "nat", n_out_cols=s5w, k_total=2 * ns, tn=s5w, tk=min(2 * ns, 1024),
        extras=(proj_ab, d_row), extra_specs=(_row_spec(tm, s5w), _vec_spec(s5w)), store=gelu_store,
        out_shape=shp2, out_specs=spec2)

    def glu_store(outs, acc, yv, bias):
        gate = acc + bias
        outs[0][...] = gate
        outs[1][...] = _glu(yv, gate)

    gate, a_out = _mm_act(
        "s5_glu", y, wts["glu"], "nat", n_out_cols=s5w, k_total=s5w, tn=s5w, tk=s5w,
        extras=(y, small["s5_b_glu"]), extra_specs=(_row_spec(tm, s5w), _vec_spec(s5w)), store=glu_store,
        out_shape=shp2, out_specs=spec2)
    b_out = _attn_fwd("sb_attn", proj_ab, n_batch, lp, pad, s5_cb, s5_cb + sb_cb, s5_cb + 2 * sb_cb, n_pairs)

    def bias_store(outs, acc, bias):
        outs[0][...] = (acc + bias).astype(outs[0].dtype)

    def wide(width, dtype):
        return [jax.ShapeDtypeStruct((rows, dff), dtype)], [_row_spec(tm, width)]

    def mlp_fwd(layer, h_in):
        shp, spec = wide(n_up, BF16)
        up = _mm_act(f"up{layer}", h_in, wts["up"][layer], "stk", n_out_cols=dff, k_total=d, tn=n_up, tk=d,
                     extras=(small["mlp_b_up"][layer:layer + 1],), extra_specs=(_vec_spec(n_up),), store=bias_store,
                     out_shape=shp, out_specs=spec)[0]
        r, h = mix_ln(f"down{layer}", up, wts["down"][layer], dff, min(dff, 1024), h_in,
                      small["mlp_b_down"][layer:layer + 1], small["ln_mlp_g"][layer:layer + 1],
                      small["ln_mlp_b"][layer:layer + 1], a_fn=_relu2)
        return up, r, h

    r1, h1 = mix_ln("out_ab", [a_out, b_out], wts["out_ab"], s5w + sbw, min(s5w, sbw), h0, zero_bias,
                    small["ln_mix_g"][0:1], small["ln_mix_b"][0:1])
    wts = {**wts, **late_weights(r1)}
    n_c = wts["in_c"].shape[2]
    n_up = wts["up"][0].shape[2]
    up0, r2, h2 = mlp_fwd(0, h1)

    lb = _lower_bound("hg_lb", small["hgrn_gamma"])
    proj_c = _mm_act("in_c", h2, wts["in_c"], "stk", n_out_cols=4 * d, k_total=d, tn=n_c, tk=d)[0]
    c_out = _hgrn_fwd("hgrn", proj_c, lb, wts["ng"], n_batch, lp, pad, n_hg)
    r3, h3 = mix_ln("out_c", c_out, wts["out_c"], d, d, h2, zero_bias, small["ln_mix_g"][1:2], small["ln_mix_b"][1:2])
    up1, r4, h4 = mlp_fwd(1, h3)

    g_h4, loss_tile = _loss_grad("loss", h4, target, n_batch, lp, lead)

    gr = {}

    def res_store(outs, acc, g_res):
        outs[0][...] = acc + ALPHA * g_res

    def mlp_bwd(layer, g_h_out, r_out, up, h_in, dep=None):
        g_r, gr[f"ln_mlp_g{layer}"], gr[f"ln_mlp_b{layer}"] = _ln_bwd(
            f"ln_mlp_bwd{layer}", r_out, small["ln_mlp_g"][layer:layer + 1], small["ln_mlp_b"][layer:layer + 1], g_h_out,
            dep=dep)

        def gup_store(outs, acc, upv):
            outs[0][...] = (acc * (2.0 * jnp.maximum(upv.astype(F32), 0.0))).astype(outs[0].dtype)

        tf = min(dff, 1024)
        shp, spec = wide(tf, BF16)
        g_up = _mm_act(f"g_up{layer}", g_r, wts["down"][layer], "natT", n_out_cols=dff, k_total=d, tn=tf, tk=d,
                       extras=(up,), extra_specs=(_row_spec(tm, tf),), store=gup_store, out_shape=shp, out_specs=spec)[0]
        gr[f"down{layer}"], gr[f"mlp_b_down{layer}"] = _mm_wgrad(
            f"dw_down{layer}", up, g_r, kw=dff, n=d, tmw=tf, tn=d, a_fn=_relu2, out_dtype=BF16, colsum=True)
        gr[f"up{layer}"], gr[f"mlp_b_up{layer}"] = _mm_wgrad(
            f"dw_up{layer}", h_in, g_up, kw=d, n=dff, tmw=d, tn=min(dff, 2048), shard_cols=n_up, out_dtype=BF16, colsum=True)
        return _mm_act(f"g_hmid{layer}", g_up, wts["up"][layer], "stkT", n_out_cols=d, k_total=dff, tn=d, tk=n_up,
                       extras=(g_r,), extra_specs=(_row_spec(tm, d),), store=res_store)[0]

    g_h3 = mlp_bwd(1, g_h4, r4, up1, h3)
    g_r3, gr["ln_mix_g1"], gr["ln_mix_b1"] = _ln_bwd("ln_mix_bwd1", r3, small["ln_mix_g"][1:2], small["ln_mix_b"][1:2], g_h3)
    g_cout = _mm_act("g_cout", g_r3, wts["out_c"], "natT", n_out_cols=d, k_total=d, tn=d, tk=d)[0]
    gr["out_c"] = _mm_wgrad("dw_out_c", c_out, g_r3, kw=d, n=d, tmw=d, tn=d, out_dtype=BF16)
    gq, gf, gi, gg_, g_lb_parts, g_ng_parts = _hgrn_bwd("hgrn_bwd", proj_c, lb, wts["ng"], g_cout, n_batch, lp, pad, n_hg)
    g_pc = [gq, gf, gi, gg_]
    gr["hgrn_gamma"], gr["ng"] = _lower_bound_bwd("hg_lb_bwd", small["hgrn_gamma"], g_lb_parts, g_ng_parts)
    gr["in_c"] = _mm_wgrad("dw_in_c", h2, g_pc, kw=d, n=4 * d, tmw=d, tn=d, shard_cols=n_c, out_dtype=BF16)
    sent1 = on_grads(1, {"down1": gr["down1"], "up1": gr["up1"], "out_c": gr["out_c"], "in_c": gr["in_c"], "ng": gr["ng"]})
    g_h2 = _mm_act("g_h2", g_pc, wts["in_c"], "stkT", n_out_cols=d, k_total=4 * d, tn=d, tk=n_c,
                   extras=(g_r3,), extra_specs=(_row_spec(tm, d),), store=res_store)[0]

    g_h1 = mlp_bwd(0, g_h2, r2, up0, h1, dep=sent1)
    sent2 = on_grads(2, {"down0": gr["down0"], "up0": gr["up0"]})
    g_r1, gr["ln_mix_g0"], gr["ln_mix_b0"] = _ln_bwd("ln_mix_bwd0", r1, small["ln_mix_g"][0:1], small["ln_mix_b"][0:1], g_h1,
                                                     dep=sent2)
    g_cat = _mm_act("g_cat", g_r1, wts["out_ab"], "natT", n_out_cols=d, k_total=d, tn=d, tk=d)[0]
    gr["out_ab"] = _mm_wgrad("dw_out_ab", [a_out, b_out], g_r1, kw=s5w + sbw, n=d, tmw=min(s5w, sbw), tn=d, out_dtype=BF16)
    g_q, g_k, g_v = _attn_bwd("sb_attn_bwd", proj_ab, g_cat, n_batch, lp, pad, s5_cb, s5_cb + sb_cb, s5_cb + 2 * sb_cb,
                              s5_cb, n_pairs)

    g_y_direct, g_gate = _rowwise("s5_glu_bwd", lambda ga, yv, gt: jax.vjp(_glu, yv, gt)[1](ga),
                                  [(g_cat, 0, s5w), (y, 0, s5w), (gate, 0, s5w)], 2, s5w)

    def gelu_bwd_store(outs, acc, gyd, yp, u, dv):
        gyp = jax.vjp(jax.nn.gelu, yp)[1](acc + gyd)[0]
        outs[0][...] = gyp
        outs[1][...] = dv * gyp
        outs[2][...] = jnp.sum(gyp * u, axis=0, keepdims=True)

    rs = _row_spec(tm, s5w)
    g_ypre, g_u_direct, gd_parts = _mm_act(
        "s5_g_y", g_gate, wts["glu"], "natT", n_out_cols=s5w, k_total=s5w, tn=s5w, tk=s5w,
        extras=(g_y_direct, ypre, proj_ab, d_row), extra_specs=(rs, rs, rs, _vec_spec(s5w)), store=gelu_bwd_store,
        out_shape=[jax.ShapeDtypeStruct((rows, s5w), F32)] * 2 + [jax.ShapeDtypeStruct((rows // tm, 1, s5w), F32)],
        out_specs=[rs, rs, pl.BlockSpec((None, 1, s5w), lambda i, j, k: (i, 0, j))])
    gr["glu"], gr["s5_b_glu"] = _mm_wgrad("dw_glu", y, g_gate, kw=s5w, n=s5w, tmw=s5w, tn=s5w, out_dtype=BF16, colsum=True)
    g_sd = _mm_act("s5_g_states", g_ypre, c_blk, "natT", n_out_cols=2 * ns, k_total=s5w, tn=min(2 * ns, 2048), tk=s5w)[0]
    d_cblk = _mm_wgrad("dw_cblk", states, g_ypre, kw=2 * ns, n=s5w, tmw=min(2 * ns, 1024), tn=s5w)
    gs, gl_parts = _s5_scan_bwd("s5_scan_bwd", g_sd, states, lam_row, n_batch, lp, sw)

    def add_store(outs, acc, other):
        outs[0][...] = acc + other

    g_u = _mm_act("s5_g_u", gs, b_blk, "natT", n_out_cols=s5w, k_total=2 * ns, tn=s5w, tk=min(2 * ns, 1024),
                  extras=(g_u_direct,), extra_specs=(rs,), store=add_store)[0]
    d_bblk = _mm_wgrad("dw_bblk", proj_ab, gs, kw=s5w, n=2 * ns, tmw=s5w, tn=min(2 * ns, 2048))
    db_re, db_im = _deinterleave(d_bblk, sw)
    dc_re, dc_im = _deinterleave(d_cblk.T, sw)
    glr, gli = _deinterleave(gl_parts, sw)
    g_lam_re, g_lam_im, g_log_dt, g_b_re_t, g_b_im_t, g_d = _s5_params_bwd(
        "s5_params_bwd", lam_re, lam_im, log_dt, b_re_t, b_im_t,
        glr.reshape(n_batch, groups, n_state), gli.reshape(n_batch, groups, n_state),
        _diag_blocks(db_re, groups), _diag_blocks(db_im, groups), gd_parts)

    g_pab = [g_u, g_q, g_k, g_v]
    assert s5w == sbw
    gr["in_ab"] = _mm_wgrad("dw_in_ab", h0, g_pab, kw=d, n=ab_cols, tmw=d, tn=s5w, shard_cols=n_ab, out_dtype=BF16)
    g_h0 = _mm_act("g_h0", g_pab, wts["in_ab"], "stkT", n_out_cols=d, k_total=ab_cols, tn=d, tk=n_ab,
                   extras=(g_r1,), extra_specs=(_row_spec(tm, d),), store=res_store)[0]
    grad_x = g_h0.reshape(n_batch, lp, d)[:, lead:, :]
    g_meta = _meta_grad("g_meta", g_h0, n_batch, lp, pad, n_meta)

    cat2 = lambda key: jnp.concatenate([gr[key + "0"], gr[key + "1"]], axis=0)
    small_grads = {
        "s5_lam_re": g_lam_re[None], "s5_lam_im": g_lam_im[None], "s5_log_dt": g_log_dt.reshape(1, groups),
        "s5_b_re": g_b_re_t.transpose(0, 2, 1)[None], "s5_b_im": g_b_im_t.transpose(0, 2, 1)[None],
        "s5_c_re": _diag_blocks(dc_re, groups)[None], "s5_c_im": -_diag_blocks(dc_im, groups)[None],
        "s5_d": g_d.reshape(1, groups, grp), "s5_b_glu": gr["s5_b_glu"], "hgrn_gamma": gr["hgrn_gamma"],
        "ln_mix_g": cat2("ln_mix_g"), "ln_mix_b": cat2("ln_mix_b"), "mlp_b_up": cat2("mlp_b_up"),
        "mlp_b_down": cat2("mlp_b_down"), "ln_mlp_g": cat2("ln_mlp_g"), "ln_mlp_b": cat2("ln_mlp_b"),
    }
    on_grads(3, {"meta": g_meta, "in_ab": gr["in_ab"], "glu": gr["glu"], "out_ab": gr["out_ab"]})
    return loss_tile, grad_x, small_grads


SMALL_NAMES = ("s5_lam_re", "s5_lam_im", "s5_log_dt", "s5_b_re", "s5_b_im", "s5_c_re", "s5_c_im", "s5_d", "s5_b_glu",
               "hgrn_gamma", "ln_mix_g", "ln_mix_b", "mlp_b_up", "mlp_b_down", "ln_mlp_g", "ln_mlp_b")
WEIGHT_ORDER = ("meta", "w_in_ab", "s5_lam_re", "s5_lam_im", "s5_log_dt", "s5_b_re", "s5_b_im", "s5_c_re", "s5_c_im",
                "s5_d", "s5_w_glu", "s5_b_glu", "w_out_ab", "w_in_c", "hgrn_gamma", "hgrn_norm_g", "w_out_c", "ln_mix_g",
                "ln_mix_b", "mlp_w_up", "mlp_b_up", "mlp_w_down", "mlp_b_down", "ln_mlp_g", "ln_mlp_b")


def kernel(x, meta, w_in_ab, s5_lam_re, s5_lam_im, s5_log_dt, s5_b_re, s5_b_im, s5_c_re, s5_c_im, s5_d, s5_w_glu, s5_b_glu, w_out_ab, w_in_c, hgrn_gamma, hgrn_norm_g, w_out_c, ln_mix_g, ln_mix_b, mlp_w_up, mlp_b_up, mlp_w_down, mlp_b_down, ln_mlp_g, ln_mlp_b, loss_target, m_meta, m_w_in_ab, m_s5_lam_re, m_s5_lam_im, m_s5_log_dt, m_s5_b_re, m_s5_b_im, m_s5_c_re, m_s5_c_im, m_s5_d, m_s5_w_glu, m_s5_b_glu, m_w_out_ab, m_w_in_c, m_hgrn_gamma, m_hgrn_norm_g, m_w_out_c, m_ln_mix_g, m_ln_mix_b, m_mlp_w_up, m_mlp_b_up, m_mlp_w_down, m_mlp_b_down, m_ln_mlp_g, m_ln_mlp_b, v_meta, v_w_in_ab, v_s5_lam_re, v_s5_lam_im, v_s5_log_dt, v_s5_b_re, v_s5_b_im, v_s5_c_re, v_s5_c_im, v_s5_d, v_s5_w_glu, v_s5_b_glu, v_w_out_ab, v_w_in_c, v_hgrn_gamma, v_hgrn_norm_g, v_w_out_c, v_ln_mix_g, v_ln_mix_b, v_mlp_w_up, v_mlp_b_up, v_mlp_w_down, v_mlp_b_down, v_ln_mlp_g, v_ln_mlp_b):
    args = dict(locals())
    w = {n: args[n] for n in WEIGHT_ORDER}
    mom = {n: args["m_" + n] for n in WEIGHT_ORDER}
    var = {n: args["v_" + n] for n in WEIGHT_ORDER}
    d = x.shape[2]
    n_meta = meta.shape[0]

    cast = lambda a: a.astype(BF16)
    early = _exchange_start("gather_early_start", [w["meta"], cast(w["w_in_ab"][0]), cast(w["s5_w_glu"][0]),
                                                   cast(w["w_out_ab"][0])], False)
    late = _exchange_start("gather_late_start", [w["hgrn_norm_g"], cast(w["w_in_c"][0]), cast(w["w_out_c"][0]),
                                                 cast(w["mlp_w_up"][0]), cast(w["mlp_w_up"][1]),
                                                 cast(w["mlp_w_down"][0]), cast(w["mlp_w_down"][1])], False, dep=early["token"])
    a_meta, a_in_ab, a_glu, a_out_ab = _exchange_wait("gather_early_wait", early, late["token"])
    wts = {"in_ab": a_in_ab, "glu": a_glu.reshape(-1, a_glu.shape[2]), "out_ab": a_out_ab.reshape(-1, d)}
    meta_full = a_meta.transpose(1, 0, 2).reshape(n_meta, d)
    small = {n: w[n] for n in SMALL_NAMES}

    def late_weights(after):
        a_ng, a_in_c, a_out_c, a_up0, a_up1, a_dn0, a_dn1 = _exchange_wait("gather_late_wait", late, after)
        return {"in_c": a_in_c, "ng": a_ng.transpose(1, 0, 2).reshape(1, d), "out_c": a_out_c.reshape(-1, d),
                "up": [a_up0, a_up1], "down": [a_dn0.reshape(-1, d), a_dn1.reshape(-1, d)]}

    n_loc = d // N_DEV
    rows_of = lambda g: g.reshape(N_DEV, -1, g.shape[-1])
    cols_of = lambda g: g.reshape(g.shape[0], N_DEV, n_loc).transpose(1, 0, 2)
    sent = {}

    def on_grads(stage, g):
        if stage == 1:
            order = (("mlp_w_down", 1), ("mlp_w_up", 1), ("w_out_c", 0), ("w_in_c", 0), ("hgrn_norm_g", None))
            parts = [rows_of(g["down1"]), g["up1"], rows_of(g["out_c"]), g["in_c"], cols_of(g["ng"])]
        elif stage == 2:
            order = (("mlp_w_down", 0), ("mlp_w_up", 0))
            parts = [rows_of(g["down0"]), g["up0"]]
        else:
            order = (("w_out_ab", 0), ("s5_w_glu", 0), ("w_in_ab", 0), ("meta", None))
            parts = [rows_of(g["out_ab"]), rows_of(g["glu"]), g["in_ab"], cols_of(g["meta"])]
        sent[stage] = (order, _exchange_start(f"scatter_start{stage}", parts, True))
        return sent[stage][1]["token"]

    loss_tile, grad_x, sg = _local_step(x, loss_target, meta_full, wts, small, late_weights, on_grads)

    shapes = [w[n].shape for n in SMALL_NAMES] + [loss_tile.shape]
    zeros = jnp.zeros_like(loss_tile)
    g_pack = _pack_rows([sg[n] for n in SMALL_NAMES] + [loss_tile], PACK_COLS)
    w_pack = _pack_rows([w[n] for n in SMALL_NAMES] + [zeros], PACK_COLS)
    m_pack = _pack_rows([mom[n] for n in SMALL_NAMES] + [zeros], PACK_COLS)
    v_pack = _pack_rows([var[n] for n in SMALL_NAMES] + [zeros], PACK_COLS)
    small_sent = _exchange_start("gather_small_start", [g_pack], False, dep=sent[3][1]["token"])

    def apply(stage, after):
        order, handle = sent[stage]
        recv = _exchange_wait(f"scatter_wait{stage}", handle, after)
        for (nm, ly), rc in zip(order, recv):
            sel = (lambda t: t) if ly is None else (lambda t, ly=ly: t[ly])
            res[(nm, ly)] = _adamw_summed(f"adamw_{nm}_{ly}", rc, sel(w[nm]), sel(mom[nm]), sel(var[nm]))
        return res[order[0]][0]

    res = {}
    done = apply(2, apply(1, small_sent["token"]))
    g_all = _exchange_wait("gather_small_wait", small_sent, done)[0]
    packed = _adamw_summed("adamw_small", g_all, w_pack, m_pack, v_pack)
    apply(3, packed[0])
    unpacked = [_unpack_rows(p, shapes, PACK_COLS) for p in packed]
    loss = unpacked[0][-1][0, 0]

    def pick(nm, which):
        if nm in SMALL_NAMES:
            return unpacked[which][SMALL_NAMES.index(nm)]
        if (nm, None) in res:
            return res[(nm, None)][which]
        return jnp.stack([res[(nm, ly)][which] for ly in range(w[nm].shape[0])], axis=0)

    return (loss, grad_x, *[pick(n, 0) for n in WEIGHT_ORDER], *[pick(n, 1) for n in WEIGHT_ORDER],
            *[pick(n, 2) for n in WEIGHT_ORDER], *[pick(n, 3) for n in WEIGHT_ORDER])
```

```python
import functools
import math

import jax
import jax.numpy as jnp
from jax import lax
from jax.experimental import pallas as pl
from jax.experimental.pallas import tpu as pltpu

F32 = jnp.float32
BF16 = jnp.bfloat16

N_DEV = 8
DEPTH = 2
ALPHA = (2.0 * DEPTH) ** 0.25
LN_EPS = 1e-5
RMS_EPS = 1e-6
SB_HEAD_DIM = 64
HG_DK = 128
HG_CHUNK = 64
LANES = 128
SUBLANES = 8
VMEM_LIMIT_BYTES = 56 * 1024 * 1024
ROW_TILE = 1088
SCAN_LANES = 256
PACK_COLS = 1024

ADAM_LR = 0.001
ADAM_B1 = 0.9
ADAM_B2 = 0.999
ADAM_EPS = 1e-08
ADAM_WD = 0.01
ADAM_STEP = 10

NN = (((1,), (0,)), ((), ()))
NT = (((1,), (1,)), ((), ()))
TN = (((0,), (0,)), ((), ()))


def _tile(n, pref, align=SUBLANES):
    t = min(n, pref)
    t -= t % align
    while t >= align:
        if n % t == 0:
            return t
        t -= align
    return n


def _params(sem):
    return pltpu.CompilerParams(dimension_semantics=sem, vmem_limit_bytes=VMEM_LIMIT_BYTES)


def _dot_raw(a, b, dims):
    return lax.dot_general(a.astype(BF16), b.astype(BF16), dims, preferred_element_type=F32)


def _make_dot(dims, da_rule, db_rule):
    @jax.custom_vjp
    def f(a, b):
        return _dot_raw(a, b, dims)

    def fwd(a, b):
        return _dot_raw(a, b, dims), (a, b)

    def bwd(res, g):
        a, b = res
        return da_rule(g, a, b), db_rule(g, a, b)

    f.defvjp(fwd, bwd)
    return f


_DOTS = {
    NN: _make_dot(NN, lambda g, a, b: _dot_raw(g, b, NT), lambda g, a, b: _dot_raw(a, g, TN)),
    NT: _make_dot(NT, lambda g, a, b: _dot_raw(g, b, NN), lambda g, a, b: _dot_raw(g, a, TN)),
    TN: _make_dot(TN, lambda g, a, b: _dot_raw(b, g, NT), lambda g, a, b: _dot_raw(a, g, NN)),
}


def _dot(a, b, dims):
    return _DOTS[dims](a, b)


def _running_sums(a, tri_ones, split=False):
    hi = a.astype(BF16)
    out = lax.dot_general(hi, tri_ones, NN, preferred_element_type=F32)
    if split:
        lo = (a - hi.astype(F32)).astype(BF16)
        out = out + lax.dot_general(lo, tri_ones, NN, preferred_element_type=F32)
    return out


def _piece_specs(pieces, block_rows, block_cols, row_of, col_of, cb0):
    per = pieces[0].shape[1] // block_cols if len(pieces) > 1 else None
    specs = []
    for p in range(len(pieces)):
        if per is None:
            specs.append(pl.BlockSpec((block_rows, block_cols), lambda *g: (row_of(*g), cb0 + col_of(*g))))
        else:
            specs.append(pl.BlockSpec(
                (block_rows, block_cols),
                lambda *g, p=p: (row_of(*g), jnp.clip(col_of(*g) - p * per, 0, per - 1))))
    return specs, per


def _mm_call(name, grid, dims, a_pieces, a_specs, a_sel, b_pieces, b_specs, b_sel, extras, extra_specs,
             out_shape, out_specs, acc_shape, a_fn, store, colsum_width=0):
    na, nb, ne, no = len(a_pieces), len(b_pieces), len(extras), len(out_shape)
    nk = grid[2]

    def body(*refs):
        a_refs, b_refs = refs[:na], refs[na:na + nb]
        extra = refs[na + nb:na + nb + ne]
        outs = refs[na + nb + ne:na + nb + ne + no]
        acc = refs[na + nb + ne + no]
        ids = (pl.program_id(0), pl.program_id(1), pl.program_id(2))
        k = ids[2]

        @pl.when(k == 0)
        def _():
            acc[...] = jnp.zeros_like(acc)

        def run(a_ref, b_ref):
            a = a_ref[...]
            if a_fn is not None:
                a = a_fn(a)
            b = b_ref[...]
            acc[...] += _dot_raw(a, b, dims)
            if colsum_width:
                cs = refs[-1]
                first = ids[1] == 0

                @pl.when(first & (k == 0))
                def _():
                    cs[...] = jnp.zeros_like(cs)

                @pl.when(first)
                def _():
                    cs[...] += jnp.sum(b.astype(F32), axis=0, keepdims=True)

        if na == 1 and nb == 1:
            run(a_refs[0], b_refs[0])
        elif nb == 1:
            per, fn = a_sel
            which = fn(*ids) // per
            for p in range(na):
                pl.when(which == p)(functools.partial(run, a_refs[p], b_refs[0]))
        else:
            assert na == 1
            per, fn = b_sel
            which = fn(*ids) // per
            for p in range(nb):
                pl.when(which == p)(functools.partial(run, a_refs[0], b_refs[p]))

        @pl.when(k == nk - 1)
        def _():
            store(outs, acc[...], *[e[...] for e in extra])
            if colsum_width:
                @pl.when(ids[1] == 0)
                def _():
                    outs[-1][...] = refs[-1][...]

    scratch = [pltpu.VMEM(acc_shape, F32)]
    if colsum_width:
        scratch.append(pltpu.VMEM((1, colsum_width), F32))
    sem = ("parallel", "arbitrary", "arbitrary") if colsum_width else ("parallel", "parallel", "arbitrary")
    return pl.pallas_call(
        body, name=name, grid=grid, in_specs=[*a_specs, *b_specs, *extra_specs], out_specs=out_specs,
        out_shape=out_shape, scratch_shapes=scratch, compiler_params=_params(sem),
    )(*a_pieces, *b_pieces, *extras)


def _store_plain(outs, acc):
    outs[0][...] = acc.astype(outs[0].dtype)


def _row_spec(tm, tn):
    return pl.BlockSpec((tm, tn), lambda i, j, k: (i, j))


def _vec_spec(tn):
    return pl.BlockSpec((1, tn), lambda i, j, k: (0, j))


def _mm_act(name, a, w, wkind, *, n_out_cols, k_total, tn, tk, a_cb0=0, a_fn=None, extras=(), extra_specs=(),
            store=_store_plain, out_shape=None, out_specs=None):
    a_pieces = list(a) if isinstance(a, (list, tuple)) else [a]
    rows = a_pieces[0].shape[0]
    tm = _tile(rows, ROW_TILE)
    grid = (rows // tm, n_out_cols // tn, k_total // tk)
    a_specs, per = _piece_specs(a_pieces, tm, tk, lambda i, j, k: i, lambda i, j, k: k, a_cb0)
    if wkind == "nat":
        b_spec, dims = pl.BlockSpec((tk, tn), lambda i, j, k: (k, j)), NN
    elif wkind == "stk":
        assert tn == w.shape[2]
        b_spec, dims = pl.BlockSpec((None, tk, tn), lambda i, j, k: (j, k, 0)), NN
    elif wkind == "natT":
        b_spec, dims = pl.BlockSpec((tn, tk), lambda i, j, k: (j, k)), NT
    else:
        assert wkind == "stkT" and tk == w.shape[2]
        b_spec, dims = pl.BlockSpec((None, tn, tk), lambda i, j, k: (k, j, 0)), NT
    if out_shape is None:
        out_shape = [jax.ShapeDtypeStruct((rows, n_out_cols), F32)]
        out_specs = [_row_spec(tm, tn)]
    return _mm_call(name, grid, dims, a_pieces, a_specs, (per, lambda i, j, k: k), [w], [b_spec], None,
                    list(extras), list(extra_specs), out_shape, out_specs, (tm, tn), a_fn, store)


def _mm_wgrad(name, a, g, *, kw, n, tmw, tn, a_cb0=0, a_fn=None, shard_cols=0, out_dtype=F32, colsum=False):
    a_pieces = list(a) if isinstance(a, (list, tuple)) else [a]
    g_pieces = list(g) if isinstance(g, (list, tuple)) else [g]
    rows = a_pieces[0].shape[0]
    tr = _tile(rows, ROW_TILE)
    grid = (n // tn, kw // tmw, rows // tr)
    a_specs, a_per = _piece_specs(a_pieces, tr, tmw, lambda j, i, k: k, lambda j, i, k: i, a_cb0)
    g_specs, g_per = _piece_specs(g_pieces, tr, tn, lambda j, i, k: k, lambda j, i, k: j, 0)
    if shard_cols:
        per = tn // shard_cols
        out_shape = [jax.ShapeDtypeStruct((n // shard_cols, kw, shard_cols), out_dtype)]
        out_specs = [pl.BlockSpec((per, tmw, shard_cols), lambda j, i, k: (j, i, 0))]

        def store(outs, acc):
            for q in range(per):
                outs[0][q] = acc[:, q * shard_cols:(q + 1) * shard_cols].astype(out_dtype)
    else:
        out_shape = [jax.ShapeDtypeStruct((kw, n), out_dtype)]
        out_specs = [pl.BlockSpec((tmw, tn), lambda j, i, k: (i, j))]

        def store(outs, acc):
            outs[0][...] = acc.astype(out_dtype)
    if colsum:
        out_shape.append(jax.ShapeDtypeStruct((1, n), F32))
        out_specs.append(pl.BlockSpec((1, tn), lambda j, i, k: (0, j)))
    res = _mm_call(name, grid, TN, a_pieces, a_specs, (a_per, lambda j, i, k: i), g_pieces, g_specs,
                   (g_per, lambda j, i, k: j), [], [], out_shape, out_specs, (tmw, tn), a_fn, store,
                   colsum_width=tn if colsum else 0)
    return res if colsum else res[0]


def _ln(x, g, b):
    mu = jnp.mean(x, axis=-1, keepdims=True)
    xc = x - mu
    var = jnp.mean(xc * xc, axis=-1, keepdims=True)
    return xc * lax.rsqrt(var + LN_EPS) * g + b


def _relu2(x):
    r = jnp.maximum(x.astype(F32), 0.0)
    return r * r


def _glu(y, gate):
    return y * jax.nn.sigmoid(gate)


def _ln_bwd(name, r, g, b, gy, dep=None):
    rows, d = r.shape
    tm = _tile(rows, ROW_TILE)
    deps = [] if dep is None else [dep]

    def body(r_ref, g_ref, b_ref, gy_ref, *rest):
        gr_ref, gg_ref, gb_ref = rest[len(deps):]
        _, vjp = jax.vjp(_ln, r_ref[...], g_ref[...], b_ref[...])
        gr, gg, gb = vjp(gy_ref[...])
        gr_ref[...] = gr

        @pl.when(pl.program_id(0) == 0)
        def _():
            gg_ref[...] = jnp.zeros_like(gg_ref)
            gb_ref[...] = jnp.zeros_like(gb_ref)

        gg_ref[...] += gg
        gb_ref[...] += gb

    row = pl.BlockSpec((tm, d), lambda i: (i, 0))
    vec = pl.BlockSpec((1, d), lambda i: (0, 0))
    return pl.pallas_call(
        body, name=name, grid=(rows // tm,),
        in_specs=[row, vec, vec, row] + [pl.BlockSpec(memory_space=pl.ANY)] * len(deps), out_specs=[row, vec, vec],
        out_shape=[jax.ShapeDtypeStruct((rows, d), F32), jax.ShapeDtypeStruct((1, d), F32),
                   jax.ShapeDtypeStruct((1, d), F32)],
        compiler_params=_params(("arbitrary",)),
    )(r, g, b, gy, *deps)


def _rowwise(name, fn, ins, n_out, width):
    rows = ins[0][0].shape[0]
    tm = _tile(rows, ROW_TILE)

    def body(*refs):
        res = fn(*[r[...] for r in refs[:len(ins)]])
        for o, v in zip(refs[len(ins):], res):
            o[...] = v

    return pl.pallas_call(
        body, name=name, grid=(rows // tm,),
        in_specs=[pl.BlockSpec((tm, wd), lambda i, cb=cb: (i, cb)) for _, cb, wd in ins],
        out_specs=[pl.BlockSpec((tm, width), lambda i: (i, 0))] * n_out,
        out_shape=[jax.ShapeDtypeStruct((rows, width), F32)] * n_out, compiler_params=_params(("parallel",)),
    )(*[a for a, _, _ in ins])


def _loss_grad(name, h, target, n_batch, lp, lead):
    rows, d = h.shape
    nq = lp // LANES
    lead_blocks = lead // LANES

    def body(h_ref, t_ref, g_ref, loss_ref):
        i = pl.program_id(1)

        @pl.when((pl.program_id(0) == 0) & (i == 0))
        def _():
            loss_ref[...] = jnp.zeros_like(loss_ref)

        diff = jnp.where(i >= lead_blocks, h_ref[...] - t_ref[...], 0.0)
        g_ref[...] = diff * (1.0 / d)
        loss_ref[...] += 0.5 * jnp.sum(diff * diff) * (1.0 / d)

    return pl.pallas_call(
        body, name=name, grid=(n_batch, nq),
        in_specs=[pl.BlockSpec((LANES, d), lambda b, i: (b * nq + i, 0)),
                  pl.BlockSpec((None, LANES, d), lambda b, i: (b, jnp.maximum(i - lead_blocks, 0), 0))],
        out_specs=[pl.BlockSpec((LANES, d), lambda b, i: (b * nq + i, 0)),
                   pl.BlockSpec((SUBLANES, LANES), lambda b, i: (0, 0))],
        out_shape=[jax.ShapeDtypeStruct((rows, d), F32), jax.ShapeDtypeStruct((SUBLANES, LANES), F32)],
        compiler_params=_params(("arbitrary", "arbitrary")),
    )(h, target)


def _meta_grad(name, g_h0, n_batch, lp, pad, n_meta):
    d = g_h0.shape[1]
    per = lp // n_meta
    at = pad // n_meta

    def body(g_ref, o_ref):
        @pl.when(pl.program_id(0) == 0)
        def _():
            o_ref[...] = jnp.zeros_like(o_ref)

        o_ref[...] += g_ref[...]

    return pl.pallas_call(
        body, name=name, grid=(n_batch,),
        in_specs=[pl.BlockSpec((n_meta, d), lambda b: (b * per + at, 0))],
        out_specs=pl.BlockSpec((n_meta, d), lambda b: (0, 0)),
        out_shape=jax.ShapeDtypeStruct((n_meta, d), F32),
        compiler_params=_params(("arbitrary",)),
    )(g_h0)


def _s5_param_fn(lr, li, ldt, br, bi):
    dt = jnp.exp(ldt)
    e = jnp.exp(lr * dt)
    w = li * dt
    lbr = e * jnp.cos(w)
    lbi = e * jnp.sin(w)
    nr = lbr - 1.0
    den = lr * lr + li * li
    cr = (nr * lr + lbi * li) / den
    ci = (lbi * lr - nr * li) / den
    bbr = cr[:, None, :] * br - ci[:, None, :] * bi
    bbi = cr[:, None, :] * bi + ci[:, None, :] * br
    return lbr, lbi, bbr, bbi


def _s5_params(name, lr, li, ldt, br, bi):
    def body(lr_ref, li_ref, ldt_ref, br_ref, bi_ref, o1, o2, o3, o4):
        res = _s5_param_fn(lr_ref[...], li_ref[...], ldt_ref[...], br_ref[...], bi_ref[...])
        for o, v in zip((o1, o2, o3, o4), res):
            o[...] = v

    shp = [jax.ShapeDtypeStruct(lr.shape, F32)] * 2 + [jax.ShapeDtypeStruct(br.shape, F32)] * 2
    return pl.pallas_call(body, name=name, out_shape=shp)(lr, li, ldt, br, bi)


def _s5_params_bwd(name, lr, li, ldt, br, bi, g_lbr, g_lbi, g_bbr, g_bbi, gd_parts):
    def body(lr_ref, li_ref, ldt_ref, br_ref, bi_ref, g1, g2, g3, g4, gd_ref, o1, o2, o3, o4, o5, o6):
        _, vjp = jax.vjp(_s5_param_fn, lr_ref[...], li_ref[...], ldt_ref[...], br_ref[...], bi_ref[...])
        res = vjp((jnp.sum(g1[...], axis=0), jnp.sum(g2[...], axis=0), g3[...], g4[...]))
        for o, v in zip((o1, o2, o3, o4, o5), res):
            o[...] = v
        o6[...] = jnp.sum(gd_ref[...], axis=0)

    shp = ([jax.ShapeDtypeStruct(lr.shape, F32)] * 2 + [jax.ShapeDtypeStruct(ldt.shape, F32)]
           + [jax.ShapeDtypeStruct(br.shape, F32)] * 2 + [jax.ShapeDtypeStruct(gd_parts.shape[1:], F32)])
    return pl.pallas_call(body, name=name, out_shape=shp)(lr, li, ldt, br, bi, g_lbr, g_lbi, g_bbr, g_bbi, gd_parts)


def _interleave(re, im, w):
    lead = re.shape[:-1]
    nj = re.shape[-1] // w
    return jnp.stack([re.reshape(*lead, nj, w), im.reshape(*lead, nj, w)], axis=-2).reshape(*lead, 2 * nj * w)


def _deinterleave(x, w):
    lead = x.shape[:-1]
    nj = x.shape[-1] // (2 * w)
    y = x.reshape(*lead, nj, 2, w)
    return y[..., 0, :].reshape(*lead, nj * w), y[..., 1, :].reshape(*lead, nj * w)


def _cmul(ar, ai, br, bi):
    return ar * br - ai * bi, ar * bi + ai * br


def _powers(lr, li):
    p = [(lr, li)]
    p.append(_cmul(*p[0], *p[0]))
    p.append(_cmul(*p[1], *p[0]))
    p.append(_cmul(*p[1], *p[1]))
    p.append(_cmul(*p[3], *p[0]))
    p.append(_cmul(*p[3], *p[1]))
    p.append(_cmul(*p[3], *p[2]))
    p.append(_cmul(*p[3], *p[3]))
    return p


def _scan_tile(xr, xi, steps):
    for sh, br, bi, m in steps:
        rr = jnp.where(m, pltpu.roll(xr, sh, 0), 0.0)
        ri = jnp.where(m, pltpu.roll(xi, sh, 0), 0.0)
        xr, xi = xr + (br * rr - bi * ri), xi + (br * ri + bi * rr)
    return xr, xi


def _s5_scan(name, bu, lam, n_batch, lp, w):
    rows, two_ns = bu.shape
    nj = two_ns // (2 * w)
    nt = lp // SUBLANES

    def body(x_ref, lam_ref, s_ref):
        pw = _powers(lam_ref[:, :w], lam_ref[:, w:])
        tab_r = jnp.concatenate([p[0] for p in pw], axis=0)
        tab_i = jnp.concatenate([p[1] for p in pw], axis=0)
        row = lax.broadcasted_iota(jnp.int32, (SUBLANES, w), 0)
        steps = [(s, jnp.broadcast_to(pw[s - 1][0], (SUBLANES, w)), jnp.broadcast_to(pw[s - 1][1], (SUBLANES, w)),
                  row >= s) for s in (1, 2, 4)]

        def tile(t, carry):
            cr, ci = carry
            r0 = pl.multiple_of(t * SUBLANES, SUBLANES)
            x = x_ref[pl.ds(r0, SUBLANES), :]
            xr, xi = _scan_tile(x[:, :w], x[:, w:], steps)
            sr = xr + (tab_r * cr - tab_i * ci)
            si = xi + (tab_r * ci + tab_i * cr)
            s_ref[pl.ds(r0, SUBLANES), :] = jnp.concatenate([sr, si], axis=1)
            return sr[SUBLANES - 1:, :], si[SUBLANES - 1:, :]

        zero = jnp.zeros((1, w), F32)
        lax.fori_loop(0, nt, tile, (zero, zero))

    spec = pl.BlockSpec((lp, 2 * w), lambda b, j: (b, j))
    return pl.pallas_call(
        body, name=name, grid=(n_batch, nj), in_specs=[spec, pl.BlockSpec((1, 2 * w), lambda b, j: (0, j))],
        out_specs=spec, out_shape=jax.ShapeDtypeStruct((rows, two_ns), F32),
        compiler_params=_params(("parallel", "parallel")),
    )(bu, lam)


def _s5_scan_bwd(name, gd, states, lam, n_batch, lp, w):
    rows, two_ns = gd.shape
    nj = two_ns // (2 * w)
    nt = lp // SUBLANES

    def body(x_ref, s_ref, lam_ref, g_ref, gl_ref):
        pw = _powers(lam_ref[:, :w], -lam_ref[:, w:])
        tab_r = jnp.concatenate([p[0] for p in reversed(pw)], axis=0)
        tab_i = jnp.concatenate([p[1] for p in reversed(pw)], axis=0)
        row = lax.broadcasted_iota(jnp.int32, (SUBLANES, w), 0)
        steps = [(SUBLANES - s, jnp.broadcast_to(pw[s - 1][0], (SUBLANES, w)),
                  jnp.broadcast_to(pw[s - 1][1], (SUBLANES, w)), row < SUBLANES - s) for s in (1, 2, 4)]

        def tile(u, carry):
            cr, ci, ar, ai = carry
            t = nt - 1 - u
            r0 = pl.multiple_of(t * SUBLANES, SUBLANES)
            x = x_ref[pl.ds(r0, SUBLANES), :]
            xr, xi = _scan_tile(x[:, :w], x[:, w:], steps)
            gr = xr + (tab_r * cr - tab_i * ci)
            gi = xi + (tab_r * ci + tab_i * cr)
            g_ref[pl.ds(r0, SUBLANES), :] = jnp.concatenate([gr, gi], axis=1)
            p0 = pl.multiple_of(jnp.maximum(t - 1, 0) * SUBLANES, SUBLANES)
            prev = s_ref[pl.ds(p0, SUBLANES), :][SUBLANES - 1:, :] * jnp.where(t > 0, 1.0, 0.0)
            cur = s_ref[pl.ds(r0, SUBLANES), :]
            spr = jnp.where(row >= 1, pltpu.roll(cur[:, :w], 1, 0), prev[:, :w])
            spi = jnp.where(row >= 1, pltpu.roll(cur[:, w:], 1, 0), prev[:, w:])
            return gr[:1, :], gi[:1, :], ar + gr * spr + gi * spi, ai + gi * spr - gr * spi

        z1 = jnp.zeros((1, w), F32)
        z8 = jnp.zeros((SUBLANES, w), F32)
        _, _, ar, ai = lax.fori_loop(0, nt, tile, (z1, z1, z8, z8))
        gl_ref[...] = jnp.concatenate([jnp.sum(ar, axis=0, keepdims=True), jnp.sum(ai, axis=0, keepdims=True)], axis=1)

    spec = pl.BlockSpec((lp, 2 * w), lambda b, j: (b, j))
    return pl.pallas_call(
        body, name=name, grid=(n_batch, nj),
        in_specs=[spec, spec, pl.BlockSpec((1, 2 * w), lambda b, j: (0, j))],
        out_specs=[spec, pl.BlockSpec((None, 1, 2 * w), lambda b, j: (b, 0, j))],
        out_shape=[jax.ShapeDtypeStruct((rows, two_ns), F32), jax.ShapeDtypeStruct((n_batch, 1, two_ns), F32)],
        compiler_params=_params(("parallel", "parallel")),
    )(gd, states, lam)


def _log_sigmoid(z):
    return jnp.minimum(z, 0.0) - jnp.log(1.0 + jnp.exp(-jnp.abs(z)))


ATTN_KEYS = 256
ATTN_GROUP = 4


def _attn_block(i, jb, lp, pad):
    start = jb * ATTN_KEYS
    r0 = pl.multiple_of(jnp.minimum(start, lp - ATTN_KEYS), LANES)
    rowpos = i * LANES + lax.broadcasted_iota(jnp.int32, (LANES, ATTN_KEYS), 0)
    keypos = r0 + lax.broadcasted_iota(jnp.int32, (LANES, ATTN_KEYS), 1)
    return r0, (keypos < rowpos) & (keypos >= jnp.maximum(start, pad))


def _tri_ones(strict_upper):
    r = lax.broadcasted_iota(jnp.int32, (ATTN_KEYS, ATTN_KEYS + LANES), 0)
    c = lax.broadcasted_iota(jnp.int32, (ATTN_KEYS, ATTN_KEYS + LANES), 1)
    tri = (r > c) if strict_upper else (r < c)
    return jnp.where((c >= ATTN_KEYS) | tri, 1.0, 0.0).astype(BF16)


def _split_sums(cr):
    rs = cr[:, ATTN_KEYS:]
    return cr[:, :ATTN_KEYS], jnp.concatenate([rs] * (ATTN_KEYS // LANES), axis=1)


def _head_masks():
    lane = lax.broadcasted_iota(jnp.int32, (1, LANES), 1)
    return [lane < SB_HEAD_DIM, lane >= SB_HEAD_DIM]


def _run_groups(n, first, sign, make):
    j, left, g = first, n, ATTN_GROUP
    while g >= 1:
        shift = g.bit_length() - 1
        count = lax.shift_right_logical(left, shift)
        fn = make(g)

        def loop(_, jcur, fn=fn, g=g):
            fn(jcur)
            return jcur + sign * g

        j = lax.fori_loop(0, count, loop, j)
        left = left - lax.shift_left(count, shift)
        g //= 2


def _attn_fwd(name, proj, n_batch, lp, pad, q_cb, k_cb, v_cb, n_pairs):
    rows = proj.shape[0]
    nq = lp // LANES
    scale = SB_HEAD_DIM ** -0.5

    def body(q_ref, k_ref, v_ref, o_ref, acc_s):
        i = pl.program_id(2)
        hm = _head_masks()
        comb = _tri_ones(True)
        qs = q_ref[...] * scale
        qh = [jnp.where(m, qs, 0.0).astype(BF16) for m in hm]
        acc_s[...] = jnp.zeros_like(acc_s)
        o_ref[...] = jnp.zeros_like(o_ref)

        def make(group):
            def fn(jtop):
                pend = []
                for g in range(group):
                    r0, vis = _attn_block(i, jtop - g, lp, pad)
                    kj = k_ref[pl.ds(r0, ATTN_KEYS), :].astype(BF16)
                    vj = v_ref[pl.ds(r0, ATTN_KEYS), :]
                    for h in range(2):
                        z = lax.dot_general(qh[h], kj, NT, preferred_element_type=F32)
                        lsz = _log_sigmoid(z)
                        cr = _running_sums(jnp.where(vis, lsz - z, 0.0), comb, split=True)
                        pend.append((h, vis, lsz, cr, jnp.where(hm[h], vj, 0.0).astype(BF16)))
                for h, vis, lsz, cr, vh in pend:
                    later, rs = _split_sums(cr)
                    acc = acc_s[h]
                    wgt = jnp.where(vis, jnp.exp(lsz + later + acc), 0.0)
                    acc_s[h] = acc + rs
                    o_ref[...] += lax.dot_general(wgt.astype(BF16), vh, NN, preferred_element_type=F32)
            return fn

        n_blocks = lax.shift_right_logical(i + ATTN_KEYS // LANES, (ATTN_KEYS // LANES).bit_length() - 1)
        _run_groups(n_blocks, n_blocks - 1, -1, make)

    return pl.pallas_call(
        body, name=name, grid=(n_batch, n_pairs, nq),
        in_specs=[pl.BlockSpec((LANES, LANES), lambda b, h, i: (b * nq + i, q_cb + h)),
                  pl.BlockSpec((lp, LANES), lambda b, h, i: (b, k_cb + h)),
                  pl.BlockSpec((lp, LANES), lambda b, h, i: (b, v_cb + h))],
        out_specs=pl.BlockSpec((LANES, LANES), lambda b, h, i: (b * nq + i, h)),
        out_shape=jax.ShapeDtypeStruct((rows, n_pairs * LANES), F32),
        scratch_shapes=[pltpu.VMEM((2, LANES, ATTN_KEYS), F32)],
        compiler_params=_params(("parallel", "parallel", "arbitrary")),
    )(proj, proj, proj)


def _attn_bwd(name, proj, g_out, n_batch, lp, pad, q_cb, k_cb, v_cb, go_cb, n_pairs):
    rows = proj.shape[0]
    nq = lp // LANES
    scale = SB_HEAD_DIM ** -0.5

    def body(q_ref, k_ref, v_ref, go_ref, gq_ref, gk_ref, gv_ref, ga_s, sz_s, acc_s):
        i = pl.program_id(2)

        @pl.when(i == 0)
        def _():
            gk_ref[...] = jnp.zeros_like(gk_ref)
            gv_ref[...] = jnp.zeros_like(gv_ref)

        hm = _head_masks()
        comb_up = _tri_ones(True)
        comb_lo = _tri_ones(False)
        qs = q_ref[...] * scale
        go = go_ref[...]
        qh = [jnp.where(m, qs, 0.0).astype(BF16) for m in hm]
        goh = [jnp.where(m, go, 0.0).astype(BF16) for m in hm]
        acc_s[...] = jnp.zeros_like(acc_s)
        gq_ref[...] = jnp.zeros_like(gq_ref)

        def make_down(group):
            def fn(jtop):
                pend = []
                for g in range(group):
                    j = jtop - g
                    r0, vis = _attn_block(i, j, lp, pad)
                    kj = k_ref[pl.ds(r0, ATTN_KEYS), :].astype(BF16)
                    vj = v_ref[pl.ds(r0, ATTN_KEYS), :].astype(BF16)
                    for h in range(2):
                        z = lax.dot_general(qh[h], kj, NT, preferred_element_type=F32)
                        lsz = _log_sigmoid(z)
                        cr = _running_sums(jnp.where(vis, lsz - z, 0.0), comb_up)
                        gw = lax.dot_general(goh[h], vj, NT, preferred_element_type=F32)
                        pend.append((h, j, r0, vis, lsz, cr, gw))
                for h, j, r0, vis, lsz, cr, gw in pend:
                    later, rs = _split_sums(cr)
                    acc = acc_s[h]
                    wgt = jnp.where(vis, jnp.exp(lsz + later + acc), 0.0)
                    acc_s[h] = acc + rs
                    ga_s[h, j] = gw * wgt
                    sz_s[h, j] = jnp.exp(lsz)
                    gv_ref[pl.ds(r0, ATTN_KEYS), :] += lax.dot_general(wgt.astype(BF16), goh[h], TN, preferred_element_type=F32)
            return fn

        n_blocks = lax.shift_right_logical(i + ATTN_KEYS // LANES, (ATTN_KEYS // LANES).bit_length() - 1)
        _run_groups(n_blocks, n_blocks - 1, -1, make_down)
        acc_s[...] = jnp.zeros_like(acc_s)

        def make_up(group):
            def fn(jbot):
                pend = []
                for g in range(group):
                    j = jbot + g
                    r0, vis = _attn_block(i, j, lp, pad)
                    kj = k_ref[pl.ds(r0, ATTN_KEYS), :]
                    for h in range(2):
                        ga = ga_s[h, j]
                        pend.append((h, j, r0, vis, ga, _running_sums(ga, comb_lo), jnp.where(hm[h], kj, 0.0).astype(BF16)))
                for h, j, r0, vis, ga, cr, kh in pend:
                    before, rs = _split_sums(cr)
                    pre = acc_s[h]
                    glk = before + pre
                    acc_s[h] = pre + rs
                    sz = sz_s[h, j]
                    gz = jnp.where(vis, ga * (1.0 - sz) - glk * sz, 0.0).astype(BF16)
                    gq_ref[...] += lax.dot_general(gz, kh, NN, preferred_element_type=F32)
                    gk_ref[pl.ds(r0, ATTN_KEYS), :] += lax.dot_general(gz, qh[h], TN, preferred_element_type=F32)
            return fn

        _run_groups(n_blocks, 0, 1, make_up)
        gq_ref[...] = gq_ref[...] * scale

    blk = lambda cb: pl.BlockSpec((LANES, LANES), lambda b, h, i: (b * nq + i, cb + h))
    full = lambda cb: pl.BlockSpec((lp, LANES), lambda b, h, i: (b, cb + h))
    shp = jax.ShapeDtypeStruct((rows, n_pairs * LANES), F32)
    per_block = pltpu.VMEM((2, -(-lp // ATTN_KEYS), LANES, ATTN_KEYS), F32)
    return pl.pallas_call(
        body, name=name, grid=(n_batch, n_pairs, nq),
        in_specs=[blk(q_cb), full(k_cb), full(v_cb), blk(go_cb)],
        out_specs=[blk(0), full(0), full(0)], out_shape=[shp, shp, shp],
        scratch_shapes=[per_block, per_block, pltpu.VMEM((2, LANES, ATTN_KEYS), F32)],
        compiler_params=_params(("parallel", "parallel", "arbitrary")),
    )(proj, proj, proj, g_out)


def _lb_fn(gamma):
    g0, g1 = gamma[0:1, :], gamma[1:2, :]
    mx = jnp.maximum(g0, g1)
    e0, e1 = jnp.exp(g0 - mx), jnp.exp(g1 - mx)
    p0, p1 = e0 / (e0 + e1), e1 / (e0 + e1)
    return (p0 + p1) - p0


def _lower_bound(name, gamma):
    def body(g_ref, o_ref):
        o_ref[...] = _lb_fn(g_ref[...])

    return pl.pallas_call(body, name=name, out_shape=jax.ShapeDtypeStruct((1, gamma.shape[1]), F32))(gamma)


def _lower_bound_bwd(name, gamma, g_lb_parts, g_ng_parts):
    def body(g_ref, glb_ref, gng_ref, o_ref, o2_ref):
        _, vjp = jax.vjp(_lb_fn, g_ref[...])
        o_ref[...] = vjp(jnp.sum(glb_ref[...], axis=0))[0]
        o2_ref[...] = jnp.sum(gng_ref[...], axis=0)

    return pl.pallas_call(
        body, name=name,
        out_shape=[jax.ShapeDtypeStruct(gamma.shape, F32), jax.ShapeDtypeStruct((1, gamma.shape[1]), F32)],
    )(gamma, g_lb_parts, g_ng_parts)


def _tri_times(tril, x, dims):
    hi = x.astype(BF16)
    lo = (x - hi.astype(F32)).astype(BF16)
    t = tril.astype(BF16)
    return (lax.dot_general(t, hi, dims, preferred_element_type=F32)
            + lax.dot_general(t, lo, dims, preferred_element_type=F32))


@jax.custom_vjp
def _cumsum_rows(x, tril):
    return _tri_times(tril, x, NN)


def _cumsum_rows_fwd(x, tril):
    return _tri_times(tril, x, NN), tril


def _cumsum_rows_bwd(tril, g):
    return _tri_times(tril, g, TN), jnp.zeros_like(tril)


_cumsum_rows.defvjp(_cumsum_rows_fwd, _cumsum_rows_bwd)


def _hg_gates(fc, lb, rowmask, tril):
    f = lb + (1.0 - lb) * jax.nn.sigmoid(fc)
    return 1.0 - f, _cumsum_rows(jnp.log(f) * rowmask, tril)


def _hg_state(fc, ic, lb, st, rowmask, tril):
    k, bcum = _hg_gates(fc, lb, rowmask, tril)
    blast = bcum[HG_CHUNK - 1:, :]
    return jnp.exp(blast) * st + _dot(ic * rowmask, k * jnp.exp(blast - bcum), TN)


def _hg_chunk(qc, fc, ic, gc, lb, ng, st, rowmask, tril):
    k, bcum = _hg_gates(fc, lb, rowmask, tril)
    blast = bcum[HG_CHUNK - 1:, :]
    v = ic * rowmask
    qd = qc * jnp.exp(bcum)
    scores = jnp.where(tril > 0.5, _dot(qd, k * jnp.exp(-bcum), NT), 0.0)
    o = _dot(scores, v, NN) + _dot(qd, st, NT)
    st_new = jnp.exp(blast) * st + _dot(v, k * jnp.exp(blast - bcum), TN)
    o = o * lax.rsqrt(jnp.mean(o * o, axis=-1, keepdims=True) + RMS_EPS) * ng
    return o * (gc * jax.nn.sigmoid(gc)), st_new


def _hg_consts(c, pad):
    r = lax.broadcasted_iota(jnp.int32, (HG_CHUNK, HG_CHUNK), 0)
    cc = lax.broadcasted_iota(jnp.int32, (HG_CHUNK, HG_CHUNK), 1)
    tril = jnp.where(r >= cc, 1.0, 0.0).astype(F32)
    pos = c * HG_CHUNK + lax.broadcasted_iota(jnp.int32, (HG_CHUNK, 1), 0)
    return tril, jnp.where(pos >= pad, 1.0, 0.0).astype(F32)


HG_HEADS_PER_STEP = 2
HG_UNROLL = 2


def _chunk_loop(n_chunks, body, init):
    assert n_chunks % HG_UNROLL == 0

    def outer(t, carry):
        for u in range(HG_UNROLL):
            carry = body(t * HG_UNROLL + u, carry)
        return carry

    return lax.fori_loop(0, n_chunks // HG_UNROLL, outer, init)


def _hg_head_cols():
    return [slice(h * HG_DK, (h + 1) * HG_DK) for h in range(HG_HEADS_PER_STEP)]


def _hg_specs(lp, n_heads):
    wide = HG_HEADS_PER_STEP * HG_DK
    groups = n_heads // HG_HEADS_PER_STEP
    col = lambda off: pl.BlockSpec((lp, wide), lambda b, h: (b, off * groups + h))
    vec = pl.BlockSpec((1, wide), lambda b, h: (0, h))
    return groups, col, vec


def _hgrn_fwd(name, proj, lb, ng, n_batch, lp, pad, n_heads):
    rows = proj.shape[0]
    nc = lp // HG_CHUNK
    groups, col, vec = _hg_specs(lp, n_heads)

    def body(q_ref, f_ref, i_ref, g_ref, lb_ref, ng_ref, o_ref):
        def chunk(c, sts):
            sl = pl.ds(pl.multiple_of(c * HG_CHUNK, HG_CHUNK), HG_CHUNK)
            tril, rowmask = _hg_consts(c, pad)
            new = []
            for cols, st in zip(_hg_head_cols(), sts):
                o, st = _hg_chunk(q_ref[sl, cols], f_ref[sl, cols], i_ref[sl, cols], g_ref[sl, cols],
                                  lb_ref[:, cols], ng_ref[:, cols], st, rowmask, tril)
                o_ref[sl, cols] = o
                new.append(st)
            return tuple(new)

        zero = jnp.zeros((HG_DK, HG_DK), F32)
        _chunk_loop(nc, chunk, (zero,) * HG_HEADS_PER_STEP)

    return pl.pallas_call(
        body, name=name, grid=(n_batch, groups), in_specs=[col(0), col(1), col(2), col(3), vec, vec],
        out_specs=col(0), out_shape=jax.ShapeDtypeStruct((rows, n_heads * HG_DK), F32),
        compiler_params=_params(("parallel", "parallel")),
    )(proj, proj, proj, proj, lb, ng)


def _hgrn_bwd(name, proj, lb, ng, g_out, n_batch, lp, pad, n_heads):
    rows = proj.shape[0]
    width = n_heads * HG_DK
    nc = lp // HG_CHUNK
    groups, col, vec = _hg_specs(lp, n_heads)

    def body(q_ref, f_ref, i_ref, g_ref, lb_ref, ng_ref, go_ref, gq_ref, gf_ref, gi_ref, gg_ref, glb_ref, gng_ref, st_s):
        heads = list(enumerate(_hg_head_cols()))

        def fwd(c, sts):
            sl = pl.ds(pl.multiple_of(c * HG_CHUNK, HG_CHUNK), HG_CHUNK)
            tril, rowmask = _hg_consts(c, pad)
            new = []
            for (h, cols), st in zip(heads, sts):
                st_s[h, c] = st
                new.append(_hg_state(f_ref[sl, cols], i_ref[sl, cols], lb_ref[:, cols], st, rowmask, tril))
            return tuple(new)

        zero = jnp.zeros((HG_DK, HG_DK), F32)
        _chunk_loop(nc, fwd, (zero,) * HG_HEADS_PER_STEP)

        def bwd(u, carry):
            c = nc - 1 - u
            sl = pl.ds(pl.multiple_of(c * HG_CHUNK, HG_CHUNK), HG_CHUNK)
            tril, rowmask = _hg_consts(c, pad)
            fn = functools.partial(_hg_chunk, rowmask=rowmask, tril=tril)
            new = []
            for (h, cols), (gst, glb, gng) in zip(heads, carry):
                _, vjp = jax.vjp(fn, q_ref[sl, cols], f_ref[sl, cols], i_ref[sl, cols], g_ref[sl, cols],
                                 lb_ref[:, cols], ng_ref[:, cols], st_s[h, c])
                gq, gf, gi, gg, dlb, dng, gst = vjp((go_ref[sl, cols], gst))
                gq_ref[sl, cols] = gq.astype(BF16)
                gf_ref[sl, cols] = gf.astype(BF16)
                gi_ref[sl, cols] = gi.astype(BF16)
                gg_ref[sl, cols] = gg.astype(BF16)
                new.append((gst, glb + dlb, gng + dng))
            return tuple(new)

        zv = jnp.zeros((1, HG_DK), F32)
        res = _chunk_loop(nc, bwd, ((zero, zv, zv),) * HG_HEADS_PER_STEP)
        for (_, cols), (_, glb, gng) in zip(heads, res):
            glb_ref[:, cols] = glb
            gng_ref[:, cols] = gng

    part = pl.BlockSpec((None, 1, HG_HEADS_PER_STEP * HG_DK), lambda b, h: (b, 0, h))
    big = jax.ShapeDtypeStruct((rows, width), BF16)
    small = jax.ShapeDtypeStruct((n_batch, 1, width), F32)
    return pl.pallas_call(
        body, name=name, grid=(n_batch, groups),
        in_specs=[col(0), col(1), col(2), col(3), vec, vec, col(0)],
        out_specs=[col(0), col(0), col(0), col(0), part, part],
        out_shape=[big, big, big, big, small, small],
        scratch_shapes=[pltpu.VMEM((HG_HEADS_PER_STEP, nc, HG_DK, HG_DK), F32)],
        compiler_params=_params(("parallel", "parallel")),
    )(proj, proj, proj, proj, lb, ng, g_out)


def _exchange_copies(src, dst, send, recv, loc, scatter):
    x, y, c = lax.axis_index("x"), lax.axis_index("y"), lax.axis_index("c")
    me = 4 * x + 2 * y + c
    local, remote = [], []
    for w in range(len(src)):
        local.append(pltpu.make_async_copy(src[w].at[me] if scatter else src[w], dst[w].at[me], loc.at[w]))
    for k in range(1, N_DEV):
        px = 1 - x if k & 4 else x
        py = 1 - y if k & 2 else y
        pc = 1 - c if k & 1 else c
        peer = 4 * px + 2 * py + pc
        for w in range(len(src)):
            remote.append(pltpu.make_async_remote_copy(
                src_ref=src[w].at[peer] if scatter else src[w], dst_ref=dst[w].at[me],
                send_sem=send.at[w * (N_DEV - 1) + k - 1], recv_sem=recv.at[w * (N_DEV - 1) + k - 1],
                device_id=(px, py, pc), device_id_type=pl.DeviceIdType.MESH))
    return local, remote


_HBM_SPEC = pl.BlockSpec(memory_space=pltpu.HBM)
_SEM_SPEC = pl.BlockSpec(memory_space=pltpu.SEMAPHORE)
_ANY_SPEC = pl.BlockSpec(memory_space=pl.ANY)
_DATAFLOW = pltpu.SideEffectType.DATAFLOW_SIDE_EFFECTING


def _exchange_start(name, srcs, scatter, dep=None):
    nw = len(srcs)
    srcs = [pltpu.with_memory_space_constraint(s, pltpu.HBM) for s in srcs]
    lands = [pltpu.with_memory_space_constraint(lax.empty(s.shape if scatter else (N_DEV,) + s.shape, s.dtype), pltpu.HBM)
             for s in srcs]
    deps = [] if dep is None else [dep]

    def body(*refs):
        src, dst = refs[:nw], refs[nw:2 * nw]
        send, recv, loc = refs[2 * nw + len(deps):2 * nw + len(deps) + 3]
        token = refs[-1]
        local, remote = _exchange_copies(src, dst, send, recv, loc, scatter)
        for cp in local + remote:
            cp.start()
        token[...] = jnp.zeros_like(token)

    sems = [pltpu.SemaphoreType.DMA((nw * (N_DEV - 1),)), pltpu.SemaphoreType.DMA((nw * (N_DEV - 1),)),
            pltpu.SemaphoreType.DMA((nw,))]
    out = pl.pallas_call(
        body, name=name,
        out_shape=(*sems, *[pltpu.HBM(s.shape, s.dtype) for s in srcs], *[pltpu.HBM(s.shape, s.dtype) for s in lands],
                   jax.ShapeDtypeStruct((SUBLANES, LANES), F32)),
        in_specs=[_HBM_SPEC] * (2 * nw) + [_ANY_SPEC] * len(deps),
        out_specs=(_SEM_SPEC, _SEM_SPEC, _SEM_SPEC, *[_HBM_SPEC] * (2 * nw), pl.BlockSpec(memory_space=pltpu.VMEM)),
        input_output_aliases={i: 3 + i for i in range(2 * nw)},
        compiler_params=pltpu.CompilerParams(has_side_effects=_DATAFLOW),
    )(*srcs, *lands, *deps)
    return {"sems": out[:3], "srcs": out[3:3 + nw], "lands": out[3 + nw:3 + 2 * nw], "token": out[-1], "scatter": scatter}


def _exchange_wait(name, handle, after):
    nw = len(handle["srcs"])
    scatter = handle["scatter"]

    def body(*refs):
        src, dst = refs[:nw], refs[nw:2 * nw]
        send, recv, loc = refs[2 * nw:2 * nw + 3]
        local, remote = _exchange_copies(src, dst, send, recv, loc, scatter)
        for cp in local:
            cp.wait()
        for cp in remote:
            cp.wait_send()
            cp.wait_recv()

    out = pl.pallas_call(
        body, name=name,
        out_shape=(*[pltpu.HBM(s.shape, s.dtype) for s in handle["srcs"]],
                   *[pltpu.HBM(s.shape, s.dtype) for s in handle["lands"]]),
        in_specs=[_HBM_SPEC] * (2 * nw) + [_SEM_SPEC] * 3 + [_ANY_SPEC],
        out_specs=tuple([_HBM_SPEC] * (2 * nw)),
        input_output_aliases={i: i for i in range(2 * nw)},
        compiler_params=pltpu.CompilerParams(has_side_effects=_DATAFLOW),
    )(*handle["srcs"], *handle["lands"], *handle["sems"], after)
    return list(out[nw:])


def _adamw(w, g, m, v):
    m = ADAM_B1 * m + (1.0 - ADAM_B1) * g
    v = ADAM_B2 * v + (1.0 - ADAM_B2) * (g * g)
    m_hat = m / (1.0 - ADAM_B1 ** ADAM_STEP)
    v_hat = v / (1.0 - ADAM_B2 ** ADAM_STEP)
    delta = -ADAM_LR * (m_hat / (jnp.sqrt(v_hat) + ADAM_EPS) + ADAM_WD * w)
    return delta, m, v


def _adamw_summed(name, parts, w, m, v):
    rows, cols = w.shape
    n_parts = parts.shape[0]
    tr = _tile(rows, max(SUBLANES, (1 << 18) // cols))

    def body(p_ref, w_ref, m_ref, v_ref, g_ref, d_ref, nm_ref, nv_ref):
        g = p_ref[0].astype(F32)
        for s in range(1, n_parts):
            g = g + p_ref[s].astype(F32)
        d, nm, nv = _adamw(w_ref[...], g, m_ref[...], v_ref[...])
        g_ref[...] = g
        d_ref[...] = d
        nm_ref[...] = nm
        nv_ref[...] = nv

    spec = pl.BlockSpec((tr, cols), lambda i: (i, 0))
    shp = jax.ShapeDtypeStruct((rows, cols), F32)
    return pl.pallas_call(
        body, name=name, grid=(rows // tr,),
        in_specs=[pl.BlockSpec((n_parts, tr, cols), lambda i: (0, i, 0)), spec, spec, spec],
        out_specs=[spec] * 4, out_shape=[shp] * 4, compiler_params=_params(("parallel",)),
    )(parts, w, m, v)


def _pack_rows(arrays, cols):
    out = []
    for a in arrays:
        flat = a.reshape(-1)
        n = -(-flat.shape[0] // cols) * cols
        out.append(jnp.pad(flat, (0, n - flat.shape[0])).reshape(-1, cols))
    packed = jnp.concatenate(out, axis=0)
    return jnp.pad(packed, ((0, -packed.shape[0] % SUBLANES), (0, 0)))


def _unpack_rows(packed, shapes, cols):
    out, r = [], 0
    for s in shapes:
        n = math.prod(s)
        nr = -(-n // cols)
        out.append(packed[r:r + nr].reshape(-1)[:n].reshape(s))
        r += nr
    return out


def _block_diag(blocks):
    g, a, b = blocks.shape
    eye = jnp.eye(g, dtype=blocks.dtype)
    return (eye[:, None, :, None] * blocks[:, :, None, :]).reshape(g * a, g * b)


def _diag_blocks(dense, g):
    a, b = dense.shape[0] // g, dense.shape[1] // g
    return jnp.einsum("gagb->gab", dense.reshape(g, a, g, b))


def _local_step(x, target, meta, wts, small, late_weights, on_grads, on_small):
    n_batch, seq, d = x.shape
    n_meta = meta.shape[0]
    pad = -(seq + n_meta) % LANES
    lead = pad + n_meta
    lp = lead + seq
    rows = n_batch * lp
    s5w = wts["glu"].shape[0]
    n_ab = wts["in_ab"].shape[2]
    ab_cols = wts["in_ab"].shape[0] * n_ab
    sbw = (ab_cols - s5w) // 3
    dff = small["mlp_b_up"].shape[1]
    n_pairs = sbw // LANES
    n_hg = d // HG_DK
    s5_cb = s5w // LANES
    sb_cb = sbw // LANES
    tm = _tile(rows, ROW_TILE)
    groups, n_state, grp = small["s5_b_re"].shape[1:]
    ns = groups * n_state
    sw = min(SCAN_LANES, ns)

    h0 = jnp.concatenate(
        [jnp.zeros((n_batch, pad, d), F32), jnp.broadcast_to(meta[None], (n_batch, n_meta, d)), x], axis=1
    ).reshape(rows, d)

    lam_re, lam_im = small["s5_lam_re"][0], small["s5_lam_im"][0]
    log_dt = small["s5_log_dt"][0][:, None]
    b_re_t = small["s5_b_re"][0].transpose(0, 2, 1)
    b_im_t = small["s5_b_im"][0].transpose(0, 2, 1)
    c_re, c_im = small["s5_c_re"][0], small["s5_c_im"][0]
    lbr, lbi, bbr, bbi = _s5_params("s5_params", lam_re, lam_im, log_dt, b_re_t, b_im_t)
    b_blk = _interleave(_block_diag(bbr), _block_diag(bbi), sw).astype(BF16)
    c_blk = _interleave(_block_diag(c_re), _block_diag(-c_im), sw).T.astype(BF16)
    lam_row = _interleave(lbr.reshape(1, ns), lbi.reshape(1, ns), sw)
    d_row = small["s5_d"].reshape(1, s5w)

    def ln_store(outs, acc, res, bias, g, b):
        r = ALPHA * res + acc + bias
        outs[0][...] = r
        outs[1][...] = _ln(r, g, b)

    zero_bias = jnp.zeros((1, d), F32)

    def mix_ln(name, a, w, k_total, tk, res, bias, g, b, a_fn=None):
        return _mm_act(name, a, w, "nat", n_out_cols=d, k_total=k_total, tn=d, tk=tk, a_fn=a_fn,
                       extras=(res, bias, g, b), extra_specs=(_row_spec(tm, d), _vec_spec(d), _vec_spec(d), _vec_spec(d)),
                       store=ln_store, out_shape=[jax.ShapeDtypeStruct((rows, d), F32)] * 2,
                       out_specs=[_row_spec(tm, d)] * 2)

    def two(width):
        return [jax.ShapeDtypeStruct((rows, width), F32)] * 2, [_row_spec(tm, width)] * 2

    proj_ab = _mm_act("in_ab", h0, wts["in_ab"], "stk", n_out_cols=ab_cols, k_total=d, tn=n_ab, tk=d)[0]
    bu = _mm_act("s5_bu", proj_ab, b_blk, "nat", n_out_cols=2 * ns, k_total=s5w, tn=min(2 * ns, 2048), tk=s5w)[0]
    states = _s5_scan("s5_scan", bu, lam_row, n_batch, lp, sw)

    def gelu_store(outs, acc, u, dv):
        ypre = acc + dv * u
        outs[0][...] = ypre
        outs[1][...] = jax.nn.gelu(ypre)

    shp2, spec2 = two(s5w)
    ypre, y = _mm_act(
        "s5_y", states, c_blk, "nat", n_out_cols=s5w, k_total=2 * ns, tn=s5w, tk=min(2 * ns, 1024),
        extras=(proj_ab, d_row), extra_specs=(_row_spec(tm, s5w), _vec_spec(s5w)), store=gelu_store,
        out_shape=shp2, out_specs=spec2)

    def glu_store(outs, acc, yv, bias):
        gate = acc + bias
        outs[0][...] = gate
        outs[1][...] = _glu(yv, gate)

    gate, a_out = _mm_act(
        "s5_glu", y, wts["glu"], "nat", n_out_cols=s5w, k_total=s5w, tn=s5w, tk=s5w,
        extras=(y, small["s5_b_glu"]), extra_specs=(_row_spec(tm, s5w), _vec_spec(s5w)), store=glu_store,
        out_shape=shp2, out_specs=spec2)
    b_out = _attn_fwd("sb_attn", proj_ab, n_batch, lp, pad, s5_cb, s5_cb + sb_cb, s5_cb + 2 * sb_cb, n_pairs)

    def bias_store(outs, acc, bias):
        outs[0][...] = (acc + bias).astype(outs[0].dtype)

    def wide(width, dtype):
        return [jax.ShapeDtypeStruct((rows, dff), dtype)], [_row_spec(tm, width)]

    def mlp_fwd(layer, h_in):
        shp, spec = wide(n_up, BF16)
        up = _mm_act(f"up{layer}", h_in, wts["up"][layer], "stk", n_out_cols=dff, k_total=d, tn=n_up, tk=d,
                     extras=(small["mlp_b_up"][layer:layer + 1],), extra_specs=(_vec_spec(n_up),), store=bias_store,
                     out_shape=shp, out_specs=spec)[0]
        r, h = mix_ln(f"down{layer}", up, wts["down"][layer], dff, min(dff, 1024), h_in,
                      small["mlp_b_down"][layer:layer + 1], small["ln_mlp_g"][layer:layer + 1],
                      small["ln_mlp_b"][layer:layer + 1], a_fn=_relu2)
        return up, r, h

    r1, h1 = mix_ln("out_ab", [a_out, b_out], wts["out_ab"], s5w + sbw, min(s5w, sbw), h0, zero_bias,
                    small["ln_mix_g"][0:1], small["ln_mix_b"][0:1])
    wts = {**wts, **late_weights(r1)}
    n_c = wts["in_c"].shape[2]
    n_up = wts["up"][0].shape[2]
    up0, r2, h2 = mlp_fwd(0, h1)

    lb = _lower_bound("hg_lb", small["hgrn_gamma"])
    proj_c = _mm_act("in_c", h2, wts["in_c"], "stk", n_out_cols=4 * d, k_total=d, tn=n_c, tk=d)[0]
    c_out = _hgrn_fwd("hgrn", proj_c, lb, wts["ng"], n_batch, lp, pad, n_hg)
    r3, h3 = mix_ln("out_c", c_out, wts["out_c"], d, d, h2, zero_bias, small["ln_mix_g"][1:2], small["ln_mix_b"][1:2])
    up1, r4, h4 = mlp_fwd(1, h3)

    g_h4, loss_tile = _loss_grad("loss", h4, target, n_batch, lp, lead)

    gr = {}

    def res_store(outs, acc, g_res):
        outs[0][...] = acc + ALPHA * g_res

    def mlp_bwd(layer, g_h_out, r_out, up, h_in, dep=None):
        g_r, gr[f"ln_mlp_g{layer}"], gr[f"ln_mlp_b{layer}"] = _ln_bwd(
            f"ln_mlp_bwd{layer}", r_out, small["ln_mlp_g"][layer:layer + 1], small["ln_mlp_b"][layer:layer + 1], g_h_out,
            dep=dep)

        def gup_store(outs, acc, upv):
            outs[0][...] = (acc * (2.0 * jnp.maximum(upv.astype(F32), 0.0))).astype(outs[0].dtype)

        tf = min(dff, 1024)
        shp, spec = wide(tf, BF16)
        g_up = _mm_act(f"g_up{layer}", g_r, wts["down"][layer], "natT", n_out_cols=dff, k_total=d, tn=tf, tk=d,
                       extras=(up,), extra_specs=(_row_spec(tm, tf),), store=gup_store, out_shape=shp, out_specs=spec)[0]
        gr[f"down{layer}"], gr[f"mlp_b_down{layer}"] = _mm_wgrad(
            f"dw_down{layer}", up, g_r, kw=dff, n=d, tmw=tf, tn=d, a_fn=_relu2, out_dtype=BF16, colsum=True)
        gr[f"up{layer}"], gr[f"mlp_b_up{layer}"] = _mm_wgrad(
            f"dw_up{layer}", h_in, g_up, kw=d, n=dff, tmw=d, tn=min(dff, 2048), shard_cols=n_up, out_dtype=BF16, colsum=True)
        return _mm_act(f"g_hmid{layer}", g_up, wts["up"][layer], "stkT", n_out_cols=d, k_total=dff, tn=d, tk=n_up,
                       extras=(g_r,), extra_specs=(_row_spec(tm, d),), store=res_store)[0]

    g_h3 = mlp_bwd(1, g_h4, r4, up1, h3)
    g_r3, gr["ln_mix_g1"], gr["ln_mix_b1"] = _ln_bwd("ln_mix_bwd1", r3, small["ln_mix_g"][1:2], small["ln_mix_b"][1:2], g_h3)
    g_cout = _mm_act("g_cout", g_r3, wts["out_c"], "natT", n_out_cols=d, k_total=d, tn=d, tk=d)[0]
    gr["out_c"] = _mm_wgrad("dw_out_c", c_out, g_r3, kw=d, n=d, tmw=d, tn=d, out_dtype=BF16)
    gq, gf, gi, gg_, g_lb_parts, g_ng_parts = _hgrn_bwd("hgrn_bwd", proj_c, lb, wts["ng"], g_cout, n_batch, lp, pad, n_hg)
    g_pc = [gq, gf, gi, gg_]
    gr["hgrn_gamma"], gr["ng"] = _lower_bound_bwd("hg_lb_bwd", small["hgrn_gamma"], g_lb_parts, g_ng_parts)
    gr["in_c"] = _mm_wgrad("dw_in_c", h2, g_pc, kw=d, n=4 * d, tmw=d, tn=d, shard_cols=n_c, out_dtype=BF16)
    sent1 = on_grads(1, {"down1": gr["down1"], "up1": gr["up1"], "out_c": gr["out_c"], "in_c": gr["in_c"], "ng": gr["ng"]})
    g_h2 = _mm_act("g_h2", g_pc, wts["in_c"], "stkT", n_out_cols=d, k_total=4 * d, tn=d, tk=n_c,
                   extras=(g_r3,), extra_specs=(_row_spec(tm, d),), store=res_store)[0]

    g_h1 = mlp_bwd(0, g_h2, r2, up0, h1, dep=sent1)
    sent2 = on_grads(2, {"down0": gr["down0"], "up0": gr["up0"]})
    g_r1, gr["ln_mix_g0"], gr["ln_mix_b0"] = _ln_bwd("ln_mix_bwd0", r1, small["ln_mix_g"][0:1], small["ln_mix_b"][0:1], g_h1,
                                                     dep=sent2)
    g_cat = _mm_act("g_cat", g_r1, wts["out_ab"], "natT", n_out_cols=d, k_total=d, tn=d, tk=d)[0]
    gr["out_ab"] = _mm_wgrad("dw_out_ab", [a_out, b_out], g_r1, kw=s5w + sbw, n=d, tmw=min(s5w, sbw), tn=d, out_dtype=BF16)
    g_q, g_k, g_v = _attn_bwd("sb_attn_bwd", proj_ab, g_cat, n_batch, lp, pad, s5_cb, s5_cb + sb_cb, s5_cb + 2 * sb_cb,
                              s5_cb, n_pairs)

    g_y_direct, g_gate = _rowwise("s5_glu_bwd", lambda ga, yv, gt: jax.vjp(_glu, yv, gt)[1](ga),
                                  [(g_cat, 0, s5w), (y, 0, s5w), (gate, 0, s5w)], 2, s5w)

    def gelu_bwd_store(outs, acc, gyd, yp, u, dv):
        gyp = jax.vjp(jax.nn.gelu, yp)[1](acc + gyd)[0]
        outs[0][...] = gyp
        outs[1][...] = dv * gyp
        outs[2][...] = jnp.sum(gyp * u, axis=0, keepdims=True)

    rs = _row_spec(tm, s5w)
    g_ypre, g_u_direct, gd_parts = _mm_act(
        "s5_g_y", g_gate, wts["glu"], "natT", n_out_cols=s5w, k_total=s5w, tn=s5w, tk=s5w,
        extras=(g_y_direct, ypre, proj_ab, d_row), extra_specs=(rs, rs, rs, _vec_spec(s5w)), store=gelu_bwd_store,
        out_shape=[jax.ShapeDtypeStruct((rows, s5w), F32)] * 2 + [jax.ShapeDtypeStruct((rows // tm, 1, s5w), F32)],
        out_specs=[rs, rs, pl.BlockSpec((None, 1, s5w), lambda i, j, k: (i, 0, j))])
    gr["glu"], gr["s5_b_glu"] = _mm_wgrad("dw_glu", y, g_gate, kw=s5w, n=s5w, tmw=s5w, tn=s5w, out_dtype=BF16, colsum=True)
    g_sd = _mm_act("s5_g_states", g_ypre, c_blk, "natT", n_out_cols=2 * ns, k_total=s5w, tn=min(2 * ns, 2048), tk=s5w)[0]
    d_cblk = _mm_wgrad("dw_cblk", states, g_ypre, kw=2 * ns, n=s5w, tmw=min(2 * ns, 1024), tn=s5w)
    gs, gl_parts = _s5_scan_bwd("s5_scan_bwd", g_sd, states, lam_row, n_batch, lp, sw)

    def add_store(outs, acc, other):
        outs[0][...] = acc + other

    g_u = _mm_act("s5_g_u", gs, b_blk, "natT", n_out_cols=s5w, k_total=2 * ns, tn=s5w, tk=min(2 * ns, 1024),
                  extras=(g_u_direct,), extra_specs=(rs,), store=add_store)[0]
    d_bblk = _mm_wgrad("dw_bblk", proj_ab, gs, kw=s5w, n=2 * ns, tmw=s5w, tn=min(2 * ns, 2048))
    db_re, db_im = _deinterleave(d_bblk, sw)
    dc_re, dc_im = _deinterleave(d_cblk.T, sw)
    glr, gli = _deinterleave(gl_parts, sw)
    g_lam_re, g_lam_im, g_log_dt, g_b_re_t, g_b_im_t, g_d = _s5_params_bwd(
        "s5_params_bwd", lam_re, lam_im, log_dt, b_re_t, b_im_t,
        glr.reshape(n_batch, groups, n_state), gli.reshape(n_batch, groups, n_state),
        _diag_blocks(db_re, groups), _diag_blocks(db_im, groups), gd_parts)

    cat2 = lambda key: jnp.concatenate([gr[key + "0"], gr[key + "1"]], axis=0)
    on_small({
        "s5_lam_re": g_lam_re[None], "s5_lam_im": g_lam_im[None], "s5_log_dt": g_log_dt.reshape(1, groups),
        "s5_b_re": g_b_re_t.transpose(0, 2, 1)[None], "s5_b_im": g_b_im_t.transpose(0, 2, 1)[None],
        "s5_c_re": _diag_blocks(dc_re, groups)[None], "s5_c_im": -_diag_blocks(dc_im, groups)[None],
        "s5_d": g_d.reshape(1, groups, grp), "s5_b_glu": gr["s5_b_glu"], "hgrn_gamma": gr["hgrn_gamma"],
        "ln_mix_g": cat2("ln_mix_g"), "ln_mix_b": cat2("ln_mix_b"), "mlp_b_up": cat2("mlp_b_up"),
        "mlp_b_down": cat2("mlp_b_down"), "ln_mlp_g": cat2("ln_mlp_g"), "ln_mlp_b": cat2("ln_mlp_b"),
    }, loss_tile)

    g_pab = [g_u, g_q, g_k, g_v]
    assert s5w == sbw
    gr["in_ab"] = _mm_wgrad("dw_in_ab", h0, g_pab, kw=d, n=ab_cols, tmw=d, tn=s5w, shard_cols=n_ab, out_dtype=BF16)
    g_h0 = _mm_act("g_h0", g_pab, wts["in_ab"], "stkT", n_out_cols=d, k_total=ab_cols, tn=d, tk=n_ab,
                   extras=(g_r1,), extra_specs=(_row_spec(tm, d),), store=res_store)[0]
    grad_x = g_h0.reshape(n_batch, lp, d)[:, lead:, :]
    g_meta = _meta_grad("g_meta", g_h0, n_batch, lp, pad, n_meta)
    on_grads(3, {"meta": g_meta, "in_ab": gr["in_ab"], "glu": gr["glu"], "out_ab": gr["out_ab"]})
    return grad_x


SMALL_NAMES = ("s5_lam_re", "s5_lam_im", "s5_log_dt", "s5_b_re", "s5_b_im", "s5_c_re", "s5_c_im", "s5_d", "s5_b_glu",
               "hgrn_gamma", "ln_mix_g", "ln_mix_b", "mlp_b_up", "mlp_b_down", "ln_mlp_g", "ln_mlp_b")
WEIGHT_ORDER = ("meta", "w_in_ab", "s5_lam_re", "s5_lam_im", "s5_log_dt", "s5_b_re", "s5_b_im", "s5_c_re", "s5_c_im",
                "s5_d", "s5_w_glu", "s5_b_glu", "w_out_ab", "w_in_c", "hgrn_gamma", "hgrn_norm_g", "w_out_c", "ln_mix_g",
                "ln_mix_b", "mlp_w_up", "mlp_b_up", "mlp_w_down", "mlp_b_down", "ln_mlp_g", "ln_mlp_b")


def kernel(x, meta, w_in_ab, s5_lam_re, s5_lam_im, s5_log_dt, s5_b_re, s5_b_im, s5_c_re, s5_c_im, s5_d, s5_w_glu, s5_b_glu, w_out_ab, w_in_c, hgrn_gamma, hgrn_norm_g, w_out_c, ln_mix_g, ln_mix_b, mlp_w_up, mlp_b_up, mlp_w_down, mlp_b_down, ln_mlp_g, ln_mlp_b, loss_target, m_meta, m_w_in_ab, m_s5_lam_re, m_s5_lam_im, m_s5_log_dt, m_s5_b_re, m_s5_b_im, m_s5_c_re, m_s5_c_im, m_s5_d, m_s5_w_glu, m_s5_b_glu, m_w_out_ab, m_w_in_c, m_hgrn_gamma, m_hgrn_norm_g, m_w_out_c, m_ln_mix_g, m_ln_mix_b, m_mlp_w_up, m_mlp_b_up, m_mlp_w_down, m_mlp_b_down, m_ln_mlp_g, m_ln_mlp_b, v_meta, v_w_in_ab, v_s5_lam_re, v_s5_lam_im, v_s5_log_dt, v_s5_b_re, v_s5_b_im, v_s5_c_re, v_s5_c_im, v_s5_d, v_s5_w_glu, v_s5_b_glu, v_w_out_ab, v_w_in_c, v_hgrn_gamma, v_hgrn_norm_g, v_w_out_c, v_ln_mix_g, v_ln_mix_b, v_mlp_w_up, v_mlp_b_up, v_mlp_w_down, v_mlp_b_down, v_ln_mlp_g, v_ln_mlp_b):
    args = dict(locals())
    w = {n: args[n] for n in WEIGHT_ORDER}
    mom = {n: args["m_" + n] for n in WEIGHT_ORDER}
    var = {n: args["v_" + n] for n in WEIGHT_ORDER}
    d = x.shape[2]
    n_meta = meta.shape[0]

    cast = lambda a: a.astype(BF16)
    early = _exchange_start("gather_early_start", [w["meta"], cast(w["w_in_ab"][0]), cast(w["s5_w_glu"][0]),
                                                   cast(w["w_out_ab"][0])], False)
    late = _exchange_start("gather_late_start", [w["hgrn_norm_g"], cast(w["w_in_c"][0]), cast(w["w_out_c"][0]),
                                                 cast(w["mlp_w_up"][0]), cast(w["mlp_w_up"][1]),
                                                 cast(w["mlp_w_down"][0]), cast(w["mlp_w_down"][1])], False, dep=early["token"])
    a_meta, a_in_ab, a_glu, a_out_ab = _exchange_wait("gather_early_wait", early, late["token"])
    wts = {"in_ab": a_in_ab, "glu": a_glu.reshape(-1, a_glu.shape[2]), "out_ab": a_out_ab.reshape(-1, d)}
    meta_full = a_meta.transpose(1, 0, 2).reshape(n_meta, d)
    small = {n: w[n] for n in SMALL_NAMES}

    def late_weights(after):
        a_ng, a_in_c, a_out_c, a_up0, a_up1, a_dn0, a_dn1 = _exchange_wait("gather_late_wait", late, after)
        return {"in_c": a_in_c, "ng": a_ng.transpose(1, 0, 2).reshape(1, d), "out_c": a_out_c.reshape(-1, d),
                "up": [a_up0, a_up1], "down": [a_dn0.reshape(-1, d), a_dn1.reshape(-1, d)]}

    n_loc = d // N_DEV
    rows_of = lambda g: g.reshape(N_DEV, -1, g.shape[-1])
    cols_of = lambda g: g.reshape(g.shape[0], N_DEV, n_loc).transpose(1, 0, 2)
    sent = {}

    def on_grads(stage, g):
        if stage == 1:
            order = (("mlp_w_down", 1), ("mlp_w_up", 1), ("w_out_c", 0), ("w_in_c", 0), ("hgrn_norm_g", None))
            parts = [rows_of(g["down1"]), g["up1"], rows_of(g["out_c"]), g["in_c"], cols_of(g["ng"])]
        elif stage == 2:
            order = (("mlp_w_down", 0), ("mlp_w_up", 0))
            parts = [rows_of(g["down0"]), g["up0"]]
        else:
            order = (("w_out_ab", 0), ("s5_w_glu", 0), ("w_in_ab", 0), ("meta", None))
            parts = [rows_of(g["out_ab"]), rows_of(g["glu"]), g["in_ab"], cols_of(g["meta"])]
        sent[stage] = (order, _exchange_start(f"scatter_start{stage}", parts, True))
        return sent[stage][1]["token"]

    def on_small(sg, loss_tile):
        g_pack = _pack_rows([sg[n] for n in SMALL_NAMES] + [loss_tile], PACK_COLS)
        sent["small"] = _exchange_start("gather_small_start", [g_pack], False)

    grad_x = _local_step(x, loss_target, meta_full, wts, small, late_weights, on_grads, on_small)
    small_sent = sent["small"]
    tile = (SUBLANES, LANES)
    shapes = [w[n].shape for n in SMALL_NAMES] + [tile]
    zeros = jnp.zeros(tile, F32)
    w_pack = _pack_rows([w[n] for n in SMALL_NAMES] + [zeros], PACK_COLS)
    m_pack = _pack_rows([mom[n] for n in SMALL_NAMES] + [zeros], PACK_COLS)
    v_pack = _pack_rows([var[n] for n in SMALL_NAMES] + [zeros], PACK_COLS)

    def apply(stage, after):
        order, handle = sent[stage]
        recv = _exchange_wait(f"scatter_wait{stage}", handle, after)
        for (nm, ly), rc in zip(order, recv):
            sel = (lambda t: t) if ly is None else (lambda t, ly=ly: t[ly])
            res[(nm, ly)] = _adamw_summed(f"adamw_{nm}_{ly}", rc, sel(w[nm]), sel(mom[nm]), sel(var[nm]))
        return res[order[0]][0]

    res = {}
    done = apply(2, apply(1, sent[3][1]["token"]))
    g_all = _exchange_wait("gather_small_wait", small_sent, done)[0]
    packed = _adamw_summed("adamw_small", g_all, w_pack, m_pack, v_pack)
    apply(3, packed[0])
    unpacked = [_unpack_rows(p, shapes, PACK_COLS) for p in packed]
    loss = unpacked[0][-1][0, 0]

    def pick(nm, which):
        if nm in SMALL_NAMES:
            return unpacked[which][SMALL_NAMES.index(nm)]
        if (nm, None) in res:
            return res[(nm, None)][which]
        return jnp.stack([res[(nm, ly)][which] for ly in range(w[nm].shape[0])], axis=0)

    return (loss, grad_x, *[pick(n, 0) for n in WEIGHT_ORDER], *[pick(n, 1) for n in WEIGHT_ORDER],
            *[pick(n, 2) for n in WEIGHT_ORDER], *[pick(n, 3) for n in WEIGHT_ORDER])
```

```python
import functools
import math

import jax
import jax.numpy as jnp
from jax import lax
from jax.experimental import pallas as pl
from jax.experimental.pallas import tpu as pltpu

F32 = jnp.float32
BF16 = jnp.bfloat16

N_DEV = 8
DEPTH = 2
ALPHA = (2.0 * DEPTH) ** 0.25
LN_EPS = 1e-5
RMS_EPS = 1e-6
SB_HEAD_DIM = 64
HG_DK = 128
HG_CHUNK = 64
LANES = 128
SUBLANES = 8
VMEM_LIMIT_BYTES = 56 * 1024 * 1024
ROW_TILE = 1088
SCAN_LANES = 256
SCAN_UNROLL = 4
PACK_COLS = 1024

ADAM_LR = 0.001
ADAM_B1 = 0.9
ADAM_B2 = 0.999
ADAM_EPS = 1e-08
ADAM_WD = 0.01
ADAM_STEP = 10

NN = (((1,), (0,)), ((), ()))
NT = (((1,), (1,)), ((), ()))
TN = (((0,), (0,)), ((), ()))


def _tile(n, pref, align=SUBLANES):
    t = min(n, pref)
    t -= t % align
    while t >= align:
        if n % t == 0:
            return t
        t -= align
    return n


def _unrolled_loop(n, body, init, unroll):
    assert n % unroll == 0

    def outer(t, carry):
        for u in range(unroll):
            carry = body(t * unroll + u, carry)
        return carry

    return lax.fori_loop(0, n // unroll, outer, init)


def _params(sem):
    return pltpu.CompilerParams(dimension_semantics=sem, vmem_limit_bytes=VMEM_LIMIT_BYTES)


def _dot_raw(a, b, dims):
    return lax.dot_general(a.astype(BF16), b.astype(BF16), dims, preferred_element_type=F32)


def _make_dot(dims, da_rule, db_rule):
    @jax.custom_vjp
    def f(a, b):
        return _dot_raw(a, b, dims)

    def fwd(a, b):
        return _dot_raw(a, b, dims), (a, b)

    def bwd(res, g):
        a, b = res
        return da_rule(g, a, b), db_rule(g, a, b)

    f.defvjp(fwd, bwd)
    return f


_DOTS = {
    NN: _make_dot(NN, lambda g, a, b: _dot_raw(g, b, NT), lambda g, a, b: _dot_raw(a, g, TN)),
    NT: _make_dot(NT, lambda g, a, b: _dot_raw(g, b, NN), lambda g, a, b: _dot_raw(g, a, TN)),
    TN: _make_dot(TN, lambda g, a, b: _dot_raw(b, g, NT), lambda g, a, b: _dot_raw(a, g, NN)),
}


def _dot(a, b, dims):
    return _DOTS[dims](a, b)


def _running_sums(a, tri_ones, split=False):
    hi = a.astype(BF16)
    out = lax.dot_general(hi, tri_ones, NN, preferred_element_type=F32)
    if split:
        lo = (a - hi.astype(F32)).astype(BF16)
        out = out + lax.dot_general(lo, tri_ones, NN, preferred_element_type=F32)
    return out


def _piece_specs(pieces, block_rows, block_cols, row_of, col_of, cb0):
    per = pieces[0].shape[1] // block_cols if len(pieces) > 1 else None
    specs = []
    for p in range(len(pieces)):
        if per is None:
            specs.append(pl.BlockSpec((block_rows, block_cols), lambda *g: (row_of(*g), cb0 + col_of(*g))))
        else:
            specs.append(pl.BlockSpec(
                (block_rows, block_cols),
                lambda *g, p=p: (row_of(*g), jnp.clip(col_of(*g) - p * per, 0, per - 1))))
    return specs, per


def _mm_call(name, grid, dims, a_pieces, a_specs, a_sel, b_pieces, b_specs, b_sel, extras, extra_specs,
             out_shape, out_specs, acc_shape, a_fn, store, colsum_width=0, sequential=False, deps=()):
    na, nb, ne, no, nd = len(a_pieces), len(b_pieces), len(extras), len(out_shape), len(deps)
    nk = grid[2]

    def body(*refs):
        a_refs, b_refs = refs[:na], refs[na:na + nb]
        extra = refs[na + nb:na + nb + ne]
        outs = refs[na + nb + ne + nd:na + nb + ne + nd + no]
        acc = refs[na + nb + ne + nd + no]
        ids = (pl.program_id(0), pl.program_id(1), pl.program_id(2))
        k = ids[2]

        @pl.when(k == 0)
        def _():
            acc[...] = jnp.zeros_like(acc)

        def run(a_ref, b_ref):
            a = a_ref[...]
            if a_fn is not None:
                a = a_fn(a)
            b = b_ref[...]
            acc[...] += _dot_raw(a, b, dims)
            if colsum_width:
                cs = refs[-1]
                first = ids[1] == 0

                @pl.when(first & (k == 0))
                def _():
                    cs[...] = jnp.zeros_like(cs)

                @pl.when(first)
                def _():
                    cs[...] += jnp.sum(b.astype(F32), axis=0, keepdims=True)

        if na == 1 and nb == 1:
            run(a_refs[0], b_refs[0])
        elif nb == 1:
            per, fn = a_sel
            which = fn(*ids) // per
            for p in range(na):
                pl.when(which == p)(functools.partial(run, a_refs[p], b_refs[0]))
        else:
            assert na == 1
            per, fn = b_sel
            which = fn(*ids) // per
            for p in range(nb):
                pl.when(which == p)(functools.partial(run, a_refs[0], b_refs[p]))

        @pl.when(k == nk - 1)
        def _():
            if sequential:
                store(outs, acc[...], *[e[...] for e in extra], first_step=(ids[0] == 0) & (ids[1] == 0))
            else:
                store(outs, acc[...], *[e[...] for e in extra])
            if colsum_width:
                @pl.when(ids[1] == 0)
                def _():
                    outs[-1][...] = refs[-1][...]

    scratch = [pltpu.VMEM(acc_shape, F32)]
    if colsum_width:
        scratch.append(pltpu.VMEM((1, colsum_width), F32))
    sem = ("parallel", "arbitrary", "arbitrary") if colsum_width else ("parallel", "parallel", "arbitrary")
    if sequential:
        sem = ("arbitrary",) * 3
    return pl.pallas_call(
        body, name=name, grid=grid,
        in_specs=[*a_specs, *b_specs, *extra_specs, *[pl.BlockSpec(memory_space=pl.ANY)] * nd], out_specs=out_specs,
        out_shape=out_shape, scratch_shapes=scratch, compiler_params=_params(sem),
    )(*a_pieces, *b_pieces, *extras, *deps)


def _store_plain(outs, acc):
    outs[0][...] = acc.astype(outs[0].dtype)


def _row_spec(tm, tn):
    return pl.BlockSpec((tm, tn), lambda i, j, k: (i, j))


def _vec_spec(tn):
    return pl.BlockSpec((1, tn), lambda i, j, k: (0, j))


def _mm_act(name, a, w, wkind, *, n_out_cols, k_total, tn, tk, a_cb0=0, a_fn=None, extras=(), extra_specs=(),
            store=_store_plain, out_shape=None, out_specs=None, sequential=False, dep=None):
    a_pieces = list(a) if isinstance(a, (list, tuple)) else [a]
    rows = a_pieces[0].shape[0]
    tm = _tile(rows, ROW_TILE)
    grid = (rows // tm, n_out_cols // tn, k_total // tk)
    a_specs, per = _piece_specs(a_pieces, tm, tk, lambda i, j, k: i, lambda i, j, k: k, a_cb0)
    if wkind == "nat":
        b_spec, dims = pl.BlockSpec((tk, tn), lambda i, j, k: (k, j)), NN
    elif wkind == "stk":
        assert tn == w.shape[2]
        b_spec, dims = pl.BlockSpec((None, tk, tn), lambda i, j, k: (j, k, 0)), NN
    elif wkind == "natT":
        b_spec, dims = pl.BlockSpec((tn, tk), lambda i, j, k: (j, k)), NT
    else:
        assert wkind == "stkT" and tk == w.shape[2]
        b_spec, dims = pl.BlockSpec((None, tn, tk), lambda i, j, k: (k, j, 0)), NT
    if out_shape is None:
        out_shape = [jax.ShapeDtypeStruct((rows, n_out_cols), F32)]
        out_specs = [_row_spec(tm, tn)]
    return _mm_call(name, grid, dims, a_pieces, a_specs, (per, lambda i, j, k: k), [w], [b_spec], None,
                    list(extras), list(extra_specs), out_shape, out_specs, (tm, tn), a_fn, store,
                    sequential=sequential, deps=() if dep is None else (dep,))


def _mm_wgrad(name, a, g, *, kw, n, tmw, tn, a_cb0=0, a_fn=None, shard_cols=0, out_dtype=F32, colsum=False):
    a_pieces = list(a) if isinstance(a, (list, tuple)) else [a]
    g_pieces = list(g) if isinstance(g, (list, tuple)) else [g]
    rows = a_pieces[0].shape[0]
    tr = _tile(rows, ROW_TILE)
    grid = (n // tn, kw // tmw, rows // tr)
    a_specs, a_per = _piece_specs(a_pieces, tr, tmw, lambda j, i, k: k, lambda j, i, k: i, a_cb0)
    g_specs, g_per = _piece_specs(g_pieces, tr, tn, lambda j, i, k: k, lambda j, i, k: j, 0)
    if shard_cols:
        per = tn // shard_cols
        out_shape = [jax.ShapeDtypeStruct((n // shard_cols, kw, shard_cols), out_dtype)]
        out_specs = [pl.BlockSpec((per, tmw, shard_cols), lambda j, i, k: (j, i, 0))]

        def store(outs, acc):
            for q in range(per):
                outs[0][q] = acc[:, q * shard_cols:(q + 1) * shard_cols].astype(out_dtype)
    else:
        out_shape = [jax.ShapeDtypeStruct((kw, n), out_dtype)]
        out_specs = [pl.BlockSpec((tmw, tn), lambda j, i, k: (i, j))]

        def store(outs, acc):
            outs[0][...] = acc.astype(out_dtype)
    if colsum:
        out_shape.append(jax.ShapeDtypeStruct((1, n), F32))
        out_specs.append(pl.BlockSpec((1, tn), lambda j, i, k: (0, j)))
    res = _mm_call(name, grid, TN, a_pieces, a_specs, (a_per, lambda j, i, k: i), g_pieces, g_specs,
                   (g_per, lambda j, i, k: j), [], [], out_shape, out_specs, (tmw, tn), a_fn, store,
                   colsum_width=tn if colsum else 0)
    return res if colsum else res[0]


def _ln(x, g, b):
    mu = jnp.mean(x, axis=-1, keepdims=True)
    xc = x - mu
    var = jnp.mean(xc * xc, axis=-1, keepdims=True)
    return xc * lax.rsqrt(var + LN_EPS) * g + b


def _relu2(x):
    r = jnp.maximum(x.astype(F32), 0.0)
    return r * r


def _glu(y, gate):
    return y * jax.nn.sigmoid(gate)


def _rowwise(name, fn, ins, n_out, width):
    rows = ins[0][0].shape[0]
    tm = _tile(rows, ROW_TILE)

    def body(*refs):
        res = fn(*[r[...] for r in refs[:len(ins)]])
        for o, v in zip(refs[len(ins):], res):
            o[...] = v

    return pl.pallas_call(
        body, name=name, grid=(rows // tm,),
        in_specs=[pl.BlockSpec((tm, wd), lambda i, cb=cb: (i, cb)) for _, cb, wd in ins],
        out_specs=[pl.BlockSpec((tm, width), lambda i: (i, 0))] * n_out,
        out_shape=[jax.ShapeDtypeStruct((rows, width), F32)] * n_out, compiler_params=_params(("parallel",)),
    )(*[a for a, _, _ in ins])


def _loss_grad(name, r, g, b, target, n_batch, lp, lead):
    rows, d = r.shape
    nq = lp // LANES
    lead_blocks = lead // LANES

    def body(r_ref, g_ref, b_ref, t_ref, gr_ref, gg_ref, gb_ref, loss_ref):
        i = pl.program_id(1)

        @pl.when((pl.program_id(0) == 0) & (i == 0))
        def _():
            loss_ref[...] = jnp.zeros_like(loss_ref)
            gg_ref[...] = jnp.zeros_like(gg_ref)
            gb_ref[...] = jnp.zeros_like(gb_ref)

        h, vjp = jax.vjp(_ln, r_ref[...], g_ref[...], b_ref[...])
        diff = jnp.where(i >= lead_blocks, h - t_ref[...], 0.0)
        gr, gg, gb = vjp(diff * (1.0 / d))
        gr_ref[...] = gr
        gg_ref[...] += gg
        gb_ref[...] += gb
        loss_ref[...] += 0.5 * jnp.sum(diff * diff) * (1.0 / d)

    vec = pl.BlockSpec((1, d), lambda b, i: (0, 0))
    row = pl.BlockSpec((LANES, d), lambda b, i: (b * nq + i, 0))
    return pl.pallas_call(
        body, name=name, grid=(n_batch, nq),
        in_specs=[row, vec, vec, pl.BlockSpec((None, LANES, d), lambda b, i: (b, jnp.maximum(i - lead_blocks, 0), 0))],
        out_specs=[row, vec, vec, pl.BlockSpec((SUBLANES, LANES), lambda b, i: (0, 0))],
        out_shape=[jax.ShapeDtypeStruct((rows, d), F32), jax.ShapeDtypeStruct((1, d), F32),
                   jax.ShapeDtypeStruct((1, d), F32), jax.ShapeDtypeStruct((SUBLANES, LANES), F32)],
        compiler_params=_params(("arbitrary", "arbitrary")),
    )(r, g, b, target)


def _meta_grad(name, g_h0, n_batch, lp, pad, n_meta):
    d = g_h0.shape[1]
    per = lp // n_meta
    at = pad // n_meta

    def body(g_ref, o_ref):
        @pl.when(pl.program_id(0) == 0)
        def _():
            o_ref[...] = jnp.zeros_like(o_ref)

        o_ref[...] += g_ref[...]

    return pl.pallas_call(
        body, name=name, grid=(n_batch,),
        in_specs=[pl.BlockSpec((n_meta, d), lambda b: (b * per + at, 0))],
        out_specs=pl.BlockSpec((n_meta, d), lambda b: (0, 0)),
        out_shape=jax.ShapeDtypeStruct((n_meta, d), F32),
        compiler_params=_params(("arbitrary",)),
    )(g_h0)


def _s5_param_fn(lr, li, ldt, br, bi):
    dt = jnp.exp(ldt)
    e = jnp.exp(lr * dt)
    w = li * dt
    lbr = e * jnp.cos(w)
    lbi = e * jnp.sin(w)
    nr = lbr - 1.0
    den = lr * lr + li * li
    cr = (nr * lr + lbi * li) / den
    ci = (lbi * lr - nr * li) / den
    bbr = cr[:, None, :] * br - ci[:, None, :] * bi
    bbi = cr[:, None, :] * bi + ci[:, None, :] * br
    return lbr, lbi, bbr, bbi


def _s5_params(name, lr, li, ldt, br, bi):
    def body(lr_ref, li_ref, ldt_ref, br_ref, bi_ref, o1, o2, o3, o4):
        res = _s5_param_fn(lr_ref[...], li_ref[...], ldt_ref[...], br_ref[...], bi_ref[...])
        for o, v in zip((o1, o2, o3, o4), res):
            o[...] = v

    shp = [jax.ShapeDtypeStruct(lr.shape, F32)] * 2 + [jax.ShapeDtypeStruct(br.shape, F32)] * 2
    return pl.pallas_call(body, name=name, out_shape=shp)(lr, li, ldt, br, bi)


def _s5_params_bwd(name, lr, li, ldt, br, bi, g_lbr, g_lbi, g_bbr, g_bbi, gd_parts):
    def body(lr_ref, li_ref, ldt_ref, br_ref, bi_ref, g1, g2, g3, g4, gd_ref, o1, o2, o3, o4, o5, o6):
        _, vjp = jax.vjp(_s5_param_fn, lr_ref[...], li_ref[...], ldt_ref[...], br_ref[...], bi_ref[...])
        res = vjp((jnp.sum(g1[...], axis=0), jnp.sum(g2[...], axis=0), g3[...], g4[...]))
        for o, v in zip((o1, o2, o3, o4, o5), res):
            o[...] = v
        o6[...] = jnp.sum(gd_ref[...], axis=0)

    shp = ([jax.ShapeDtypeStruct(lr.shape, F32)] * 2 + [jax.ShapeDtypeStruct(ldt.shape, F32)]
           + [jax.ShapeDtypeStruct(br.shape, F32)] * 2 + [jax.ShapeDtypeStruct(gd_parts.shape[1:], F32)])
    return pl.pallas_call(body, name=name, out_shape=shp)(lr, li, ldt, br, bi, g_lbr, g_lbi, g_bbr, g_bbi, gd_parts)


def _interleave(re, im, w):
    lead = re.shape[:-1]
    nj = re.shape[-1] // w
    return jnp.stack([re.reshape(*lead, nj, w), im.reshape(*lead, nj, w)], axis=-2).reshape(*lead, 2 * nj * w)


def _deinterleave(x, w):
    lead = x.shape[:-1]
    nj = x.shape[-1] // (2 * w)
    y = x.reshape(*lead, nj, 2, w)
    return y[..., 0, :].reshape(*lead, nj * w), y[..., 1, :].reshape(*lead, nj * w)


def _cmul(ar, ai, br, bi):
    return ar * br - ai * bi, ar * bi + ai * br


def _powers(lr, li):
    p = [(lr, li)]
    p.append(_cmul(*p[0], *p[0]))
    p.append(_cmul(*p[1], *p[0]))
    p.append(_cmul(*p[1], *p[1]))
    p.append(_cmul(*p[3], *p[0]))
    p.append(_cmul(*p[3], *p[1]))
    p.append(_cmul(*p[3], *p[2]))
    p.append(_cmul(*p[3], *p[3]))
    return p


def _scan_tile(xr, xi, steps):
    for sh, br, bi, m in steps:
        rr = jnp.where(m, pltpu.roll(xr, sh, 0), 0.0)
        ri = jnp.where(m, pltpu.roll(xi, sh, 0), 0.0)
        xr, xi = xr + (br * rr - bi * ri), xi + (br * ri + bi * rr)
    return xr, xi


def _s5_scan(name, bu, lam, n_batch, lp, w):
    rows, two_ns = bu.shape
    nj = two_ns // (2 * w)
    nt = lp // SUBLANES

    def body(x_ref, lam_ref, s_ref):
        pw = _powers(lam_ref[:, :w], lam_ref[:, w:])
        tab_r = jnp.concatenate([p[0] for p in pw], axis=0)
        tab_i = jnp.concatenate([p[1] for p in pw], axis=0)
        row = lax.broadcasted_iota(jnp.int32, (SUBLANES, w), 0)
        steps = [(s, jnp.broadcast_to(pw[s - 1][0], (SUBLANES, w)), jnp.broadcast_to(pw[s - 1][1], (SUBLANES, w)),
                  row >= s) for s in (1, 2, 4)]

        def tile(t, carry):
            cr, ci = carry
            r0 = pl.multiple_of(t * SUBLANES, SUBLANES)
            x = x_ref[pl.ds(r0, SUBLANES), :]
            xr, xi = _scan_tile(x[:, :w], x[:, w:], steps)
            sr = xr + (tab_r * cr - tab_i * ci)
            si = xi + (tab_r * ci + tab_i * cr)
            s_ref[pl.ds(r0, SUBLANES), :] = jnp.concatenate([sr, si], axis=1)
            return sr[SUBLANES - 1:, :], si[SUBLANES - 1:, :]

        zero = jnp.zeros((1, w), F32)
        _unrolled_loop(nt, tile, (zero, zero), SCAN_UNROLL)

    spec = pl.BlockSpec((lp, 2 * w), lambda b, j: (b, j))
    return pl.pallas_call(
        body, name=name, grid=(n_batch, nj), in_specs=[spec, pl.BlockSpec((1, 2 * w), lambda b, j: (0, j))],
        out_specs=spec, out_shape=jax.ShapeDtypeStruct((rows, two_ns), F32),
        compiler_params=_params(("parallel", "parallel")),
    )(bu, lam)


def _s5_scan_bwd(name, gd, states, lam, n_batch, lp, w):
    rows, two_ns = gd.shape
    nj = two_ns // (2 * w)
    nt = lp // SUBLANES

    def body(x_ref, s_ref, lam_ref, g_ref, gl_ref):
        pw = _powers(lam_ref[:, :w], -lam_ref[:, w:])
        tab_r = jnp.concatenate([p[0] for p in reversed(pw)], axis=0)
        tab_i = jnp.concatenate([p[1] for p in reversed(pw)], axis=0)
        row = lax.broadcasted_iota(jnp.int32, (SUBLANES, w), 0)
        steps = [(SUBLANES - s, jnp.broadcast_to(pw[s - 1][0], (SUBLANES, w)),
                  jnp.broadcast_to(pw[s - 1][1], (SUBLANES, w)), row < SUBLANES - s) for s in (1, 2, 4)]

        def tile(u, carry):
            cr, ci, ar, ai = carry
            t = nt - 1 - u
            r0 = pl.multiple_of(t * SUBLANES, SUBLANES)
            x = x_ref[pl.ds(r0, SUBLANES), :]
            xr, xi = _scan_tile(x[:, :w], x[:, w:], steps)
            gr = xr + (tab_r * cr - tab_i * ci)
            gi = xi + (tab_r * ci + tab_i * cr)
            g_ref[pl.ds(r0, SUBLANES), :] = jnp.concatenate([gr, gi], axis=1)
            p0 = pl.multiple_of(jnp.maximum(t - 1, 0) * SUBLANES, SUBLANES)
            prev = s_ref[pl.ds(p0, SUBLANES), :][SUBLANES - 1:, :] * jnp.where(t > 0, 1.0, 0.0)
            cur = s_ref[pl.ds(r0, SUBLANES), :]
            spr = jnp.where(row >= 1, pltpu.roll(cur[:, :w], 1, 0), prev[:, :w])
            spi = jnp.where(row >= 1, pltpu.roll(cur[:, w:], 1, 0), prev[:, w:])
            return gr[:1, :], gi[:1, :], ar + gr * spr + gi * spi, ai + gi * spr - gr * spi

        z1 = jnp.zeros((1, w), F32)
        z8 = jnp.zeros((SUBLANES, w), F32)
        _, _, ar, ai = _unrolled_loop(nt, tile, (z1, z1, z8, z8), SCAN_UNROLL)
        gl_ref[...] = jnp.concatenate([jnp.sum(ar, axis=0, keepdims=True), jnp.sum(ai, axis=0, keepdims=True)], axis=1)

    spec = pl.BlockSpec((lp, 2 * w), lambda b, j: (b, j))
    return pl.pallas_call(
        body, name=name, grid=(n_batch, nj),
        in_specs=[spec, spec, pl.BlockSpec((1, 2 * w), lambda b, j: (0, j))],
        out_specs=[spec, pl.BlockSpec((None, 1, 2 * w), lambda b, j: (b, 0, j))],
        out_shape=[jax.ShapeDtypeStruct((rows, two_ns), F32), jax.ShapeDtypeStruct((n_batch, 1, two_ns), F32)],
        compiler_params=_params(("parallel", "parallel")),
    )(gd, states, lam)


def _log_sigmoid(z):
    return jnp.minimum(z, 0.0) - jnp.log(1.0 + jnp.exp(-jnp.abs(z)))


ATTN_KEYS = 256
ATTN_GROUP = 4


def _attn_block(i, jb, lp, pad):
    start = jb * ATTN_KEYS
    r0 = pl.multiple_of(jnp.minimum(start, lp - ATTN_KEYS), LANES)
    rowpos = i * LANES + lax.broadcasted_iota(jnp.int32, (LANES, ATTN_KEYS), 0)
    keypos = r0 + lax.broadcasted_iota(jnp.int32, (LANES, ATTN_KEYS), 1)
    return r0, (keypos < rowpos) & (keypos >= jnp.maximum(start, pad))


def _tri_ones(strict_upper):
    r = lax.broadcasted_iota(jnp.int32, (ATTN_KEYS, ATTN_KEYS + LANES), 0)
    c = lax.broadcasted_iota(jnp.int32, (ATTN_KEYS, ATTN_KEYS + LANES), 1)
    tri = (r > c) if strict_upper else (r < c)
    return jnp.where((c >= ATTN_KEYS) | tri, 1.0, 0.0).astype(BF16)


def _split_sums(cr):
    rs = cr[:, ATTN_KEYS:]
    return cr[:, :ATTN_KEYS], jnp.concatenate([rs] * (ATTN_KEYS // LANES), axis=1)


def _head_masks():
    lane = lax.broadcasted_iota(jnp.int32, (1, LANES), 1)
    return [lane < SB_HEAD_DIM, lane >= SB_HEAD_DIM]


def _run_groups(n, first, sign, make):
    j, left, g = first, n, ATTN_GROUP
    while g >= 1:
        shift = g.bit_length() - 1
        count = lax.shift_right_logical(left, shift)
        fn = make(g)

        def loop(_, jcur, fn=fn, g=g):
            fn(jcur)
            return jcur + sign * g

        j = lax.fori_loop(0, count, loop, j)
        left = left - lax.shift_left(count, shift)
        g //= 2


def _attn_fwd(name, proj, n_batch, lp, pad, q_cb, k_cb, v_cb, n_pairs):
    rows = proj.shape[0]
    nq = lp // LANES
    scale = SB_HEAD_DIM ** -0.5

    def body(q_ref, k_ref, v_ref, o_ref, acc_s):
        i = pl.program_id(2)
        hm = _head_masks()
        comb = _tri_ones(True)
        qs = q_ref[...] * scale
        qh = [jnp.where(m, qs, 0.0).astype(BF16) for m in hm]
        acc_s[...] = jnp.zeros_like(acc_s)
        o_ref[...] = jnp.zeros_like(o_ref)

        def make(group):
            def fn(jtop):
                pend = []
                for g in range(group):
                    r0, vis = _attn_block(i, jtop - g, lp, pad)
                    kj = k_ref[pl.ds(r0, ATTN_KEYS), :].astype(BF16)
                    vj = v_ref[pl.ds(r0, ATTN_KEYS), :]
                    for h in range(2):
                        z = lax.dot_general(qh[h], kj, NT, preferred_element_type=F32)
                        lsz = _log_sigmoid(z)
                        cr = _running_sums(jnp.where(vis, lsz - z, 0.0), comb, split=True)
                        pend.append((h, vis, lsz, cr, jnp.where(hm[h], vj, 0.0).astype(BF16)))
                for h, vis, lsz, cr, vh in pend:
                    later, rs = _split_sums(cr)
                    acc = acc_s[h]
                    wgt = jnp.where(vis, jnp.exp(lsz + later + acc), 0.0)
                    acc_s[h] = acc + rs
                    o_ref[...] += lax.dot_general(wgt.astype(BF16), vh, NN, preferred_element_type=F32)
            return fn

        n_blocks = lax.shift_right_logical(i + ATTN_KEYS // LANES, (ATTN_KEYS // LANES).bit_length() - 1)
        _run_groups(n_blocks, n_blocks - 1, -1, make)

    return pl.pallas_call(
        body, name=name, grid=(n_batch, n_pairs, nq),
        in_specs=[pl.BlockSpec((LANES, LANES), lambda b, h, i: (b * nq + i, q_cb + h)),
                  pl.BlockSpec((lp, LANES), lambda b, h, i: (b, k_cb + h)),
                  pl.BlockSpec((lp, LANES), lambda b, h, i: (b, v_cb + h))],
        out_specs=pl.BlockSpec((LANES, LANES), lambda b, h, i: (b * nq + i, h)),
        out_shape=jax.ShapeDtypeStruct((rows, n_pairs * LANES), F32),
        scratch_shapes=[pltpu.VMEM((2, LANES, ATTN_KEYS), F32)],
        compiler_params=_params(("parallel", "parallel", "arbitrary")),
    )(proj, proj, proj)


def _attn_bwd(name, proj, g_out, n_batch, lp, pad, q_cb, k_cb, v_cb, go_cb, n_pairs):
    rows = proj.shape[0]
    nq = lp // LANES
    scale = SB_HEAD_DIM ** -0.5

    def body(q_ref, k_ref, v_ref, go_ref, gq_ref, gk_ref, gv_ref, ga_s, sz_s, acc_s):
        i = pl.program_id(2)

        @pl.when(i == 0)
        def _():
            gk_ref[...] = jnp.zeros_like(gk_ref)
            gv_ref[...] = jnp.zeros_like(gv_ref)

        hm = _head_masks()
        comb_up = _tri_ones(True)
        comb_lo = _tri_ones(False)
        qs = q_ref[...] * scale
        go = go_ref[...]
        qh = [jnp.where(m, qs, 0.0).astype(BF16) for m in hm]
        goh = [jnp.where(m, go, 0.0).astype(BF16) for m in hm]
        acc_s[...] = jnp.zeros_like(acc_s)
        gq_ref[...] = jnp.zeros_like(gq_ref)

        def make_down(group):
            def fn(jtop):
                pend = []
                for g in range(group):
                    j = jtop - g
                    r0, vis = _attn_block(i, j, lp, pad)
                    kj = k_ref[pl.ds(r0, ATTN_KEYS), :].astype(BF16)
                    vj = v_ref[pl.ds(r0, ATTN_KEYS), :].astype(BF16)
                    for h in range(2):
                        z = lax.dot_general(qh[h], kj, NT, preferred_element_type=F32)
                        lsz = _log_sigmoid(z)
                        cr = _running_sums(jnp.where(vis, lsz - z, 0.0), comb_up)
                        gw = lax.dot_general(goh[h], vj, NT, preferred_element_type=F32)
                        pend.append((h, j, r0, vis, lsz, cr, gw))
                for h, j, r0, vis, lsz, cr, gw in pend:
                    later, rs = _split_sums(cr)
                    acc = acc_s[h]
                    wgt = jnp.where(vis, jnp.exp(lsz + later + acc), 0.0)
                    acc_s[h] = acc + rs
                    ga_s[h, j] = gw * wgt
                    sz_s[h, j] = jnp.exp(lsz)
                    gv_ref[pl.ds(r0, ATTN_KEYS), :] += lax.dot_general(wgt.astype(BF16), goh[h], TN, preferred_element_type=F32)
            return fn

        n_blocks = lax.shift_right_logical(i + ATTN_KEYS // LANES, (ATTN_KEYS // LANES).bit_length() - 1)
        _run_groups(n_blocks, n_blocks - 1, -1, make_down)
        acc_s[...] = jnp.zeros_like(acc_s)

        def make_up(group):
            def fn(jbot):
                pend = []
                for g in range(group):
                    j = jbot + g
                    r0, vis = _attn_block(i, j, lp, pad)
                    kj = k_ref[pl.ds(r0, ATTN_KEYS), :]
                    for h in range(2):
                        ga = ga_s[h, j]
                        pend.append((h, j, r0, vis, ga, _running_sums(ga, comb_lo), jnp.where(hm[h], kj, 0.0).astype(BF16)))
                for h, j, r0, vis, ga, cr, kh in pend:
                    before, rs = _split_sums(cr)
                    pre = acc_s[h]
                    glk = before + pre
                    acc_s[h] = pre + rs
                    sz = sz_s[h, j]
                    gz = jnp.where(vis, ga * (1.0 - sz) - glk * sz, 0.0).astype(BF16)
                    gq_ref[...] += lax.dot_general(gz, kh, NN, preferred_element_type=F32)
                    gk_ref[pl.ds(r0, ATTN_KEYS), :] += lax.dot_general(gz, qh[h], TN, preferred_element_type=F32)
            return fn

        _run_groups(n_blocks, 0, 1, make_up)
        gq_ref[...] = gq_ref[...] * scale

    blk = lambda cb: pl.BlockSpec((LANES, LANES), lambda b, h, i: (b * nq + i, cb + h))
    full = lambda cb: pl.BlockSpec((lp, LANES), lambda b, h, i: (b, cb + h))
    shp = jax.ShapeDtypeStruct((rows, n_pairs * LANES), F32)
    per_block = pltpu.VMEM((2, -(-lp // ATTN_KEYS), LANES, ATTN_KEYS), F32)
    return pl.pallas_call(
        body, name=name, grid=(n_batch, n_pairs, nq),
        in_specs=[blk(q_cb), full(k_cb), full(v_cb), blk(go_cb)],
        out_specs=[blk(0), full(0), full(0)], out_shape=[shp, shp, shp],
        scratch_shapes=[per_block, per_block, pltpu.VMEM((2, LANES, ATTN_KEYS), F32)],
        compiler_params=_params(("parallel", "parallel", "arbitrary")),
    )(proj, proj, proj, g_out)


def _lb_fn(gamma):
    g0, g1 = gamma[0:1, :], gamma[1:2, :]
    mx = jnp.maximum(g0, g1)
    e0, e1 = jnp.exp(g0 - mx), jnp.exp(g1 - mx)
    p0, p1 = e0 / (e0 + e1), e1 / (e0 + e1)
    return (p0 + p1) - p0


def _lower_bound(name, gamma):
    def body(g_ref, o_ref):
        o_ref[...] = _lb_fn(g_ref[...])

    return pl.pallas_call(body, name=name, out_shape=jax.ShapeDtypeStruct((1, gamma.shape[1]), F32))(gamma)


def _lower_bound_bwd(name, gamma, g_lb_parts, g_ng_parts):
    def body(g_ref, glb_ref, gng_ref, o_ref, o2_ref):
        _, vjp = jax.vjp(_lb_fn, g_ref[...])
        o_ref[...] = vjp(jnp.sum(glb_ref[...], axis=0))[0]
        o2_ref[...] = jnp.sum(gng_ref[...], axis=0)

    return pl.pallas_call(
        body, name=name,
        out_shape=[jax.ShapeDtypeStruct(gamma.shape, F32), jax.ShapeDtypeStruct((1, gamma.shape[1]), F32)],
    )(gamma, g_lb_parts, g_ng_parts)


def _tri_times(tril, x, dims):
    hi = x.astype(BF16)
    lo = (x - hi.astype(F32)).astype(BF16)
    t = tril.astype(BF16)
    return (lax.dot_general(t, hi, dims, preferred_element_type=F32)
            + lax.dot_general(t, lo, dims, preferred_element_type=F32))


@jax.custom_vjp
def _cumsum_rows(x, tril):
    return _tri_times(tril, x, NN)


def _cumsum_rows_fwd(x, tril):
    return _tri_times(tril, x, NN), tril


def _cumsum_rows_bwd(tril, g):
    return _tri_times(tril, g, TN), jnp.zeros_like(tril)


_cumsum_rows.defvjp(_cumsum_rows_fwd, _cumsum_rows_bwd)


def _hg_gates(fc, lb, rowmask, tril):
    f = lb + (1.0 - lb) * jax.nn.sigmoid(fc)
    return 1.0 - f, _cumsum_rows(jnp.log(f) * rowmask, tril)


def _hg_state(fc, ic, lb, st, rowmask, tril):
    k, bcum = _hg_gates(fc, lb, rowmask, tril)
    blast = bcum[HG_CHUNK - 1:, :]
    return jnp.exp(blast) * st + _dot(ic * rowmask, k * jnp.exp(blast - bcum), TN)


def _hg_chunk(qc, fc, ic, gc, lb, ng, st, rowmask, tril):
    k, bcum = _hg_gates(fc, lb, rowmask, tril)
    blast = bcum[HG_CHUNK - 1:, :]
    v = ic * rowmask
    qd = qc * jnp.exp(bcum)
    scores = jnp.where(tril > 0.5, _dot(qd, k * jnp.exp(-bcum), NT), 0.0)
    o = _dot(scores, v, NN) + _dot(qd, st, NT)
    st_new = jnp.exp(blast) * st + _dot(v, k * jnp.exp(blast - bcum), TN)
    o = o * lax.rsqrt(jnp.mean(o * o, axis=-1, keepdims=True) + RMS_EPS) * ng
    return o * (gc * jax.nn.sigmoid(gc)), st_new


def _hg_consts(c, pad):
    r = lax.broadcasted_iota(jnp.int32, (HG_CHUNK, HG_CHUNK), 0)
    cc = lax.broadcasted_iota(jnp.int32, (HG_CHUNK, HG_CHUNK), 1)
    tril = jnp.where(r >= cc, 1.0, 0.0).astype(F32)
    pos = c * HG_CHUNK + lax.broadcasted_iota(jnp.int32, (HG_CHUNK, 1), 0)
    return tril, jnp.where(pos >= pad, 1.0, 0.0).astype(F32)


HG_HEADS_PER_STEP = 2
HG_UNROLL = 2


def _chunk_loop(n_chunks, body, init):
    return _unrolled_loop(n_chunks, body, init, HG_UNROLL)


def _hg_head_cols():
    return [slice(h * HG_DK, (h + 1) * HG_DK) for h in range(HG_HEADS_PER_STEP)]


def _hg_specs(lp, n_heads):
    wide = HG_HEADS_PER_STEP * HG_DK
    groups = n_heads // HG_HEADS_PER_STEP
    col = lambda off: pl.BlockSpec((lp, wide), lambda b, h: (b, off * groups + h))
    vec = pl.BlockSpec((1, wide), lambda b, h: (0, h))
    return groups, col, vec


def _hgrn_fwd(name, proj, lb, ng, n_batch, lp, pad, n_heads):
    rows = proj.shape[0]
    nc = lp // HG_CHUNK
    groups, col, vec = _hg_specs(lp, n_heads)

    def body(q_ref, f_ref, i_ref, g_ref, lb_ref, ng_ref, o_ref):
        def chunk(c, sts):
            sl = pl.ds(pl.multiple_of(c * HG_CHUNK, HG_CHUNK), HG_CHUNK)
            tril, rowmask = _hg_consts(c, pad)
            new = []
            for cols, st in zip(_hg_head_cols(), sts):
                o, st = _hg_chunk(q_ref[sl, cols], f_ref[sl, cols], i_ref[sl, cols], g_ref[sl, cols],
                                  lb_ref[:, cols], ng_ref[:, cols], st, rowmask, tril)
                o_ref[sl, cols] = o
                new.append(st)
            return tuple(new)

        zero = jnp.zeros((HG_DK, HG_DK), F32)
        _chunk_loop(nc, chunk, (zero,) * HG_HEADS_PER_STEP)

    return pl.pallas_call(
        body, name=name, grid=(n_batch, groups), in_specs=[col(0), col(1), col(2), col(3), vec, vec],
        out_specs=col(0), out_shape=jax.ShapeDtypeStruct((rows, n_heads * HG_DK), F32),
        compiler_params=_params(("parallel", "parallel")),
    )(proj, proj, proj, proj, lb, ng)


def _hgrn_bwd(name, proj, lb, ng, g_out, n_batch, lp, pad, n_heads):
    rows = proj.shape[0]
    width = n_heads * HG_DK
    nc = lp // HG_CHUNK
    groups, col, vec = _hg_specs(lp, n_heads)

    def body(q_ref, f_ref, i_ref, g_ref, lb_ref, ng_ref, go_ref, gq_ref, gf_ref, gi_ref, gg_ref, glb_ref, gng_ref, st_s):
        heads = list(enumerate(_hg_head_cols()))

        def fwd(c, sts):
            sl = pl.ds(pl.multiple_of(c * HG_CHUNK, HG_CHUNK), HG_CHUNK)
            tril, rowmask = _hg_consts(c, pad)
            new = []
            for (h, cols), st in zip(heads, sts):
                st_s[h, c] = st
                new.append(_hg_state(f_ref[sl, cols], i_ref[sl, cols], lb_ref[:, cols], st, rowmask, tril))
            return tuple(new)

        zero = jnp.zeros((HG_DK, HG_DK), F32)
        _chunk_loop(nc, fwd, (zero,) * HG_HEADS_PER_STEP)

        def bwd(u, carry):
            c = nc - 1 - u
            sl = pl.ds(pl.multiple_of(c * HG_CHUNK, HG_CHUNK), HG_CHUNK)
            tril, rowmask = _hg_consts(c, pad)
            fn = functools.partial(_hg_chunk, rowmask=rowmask, tril=tril)
            new = []
            for (h, cols), (gst, glb, gng) in zip(heads, carry):
                _, vjp = jax.vjp(fn, q_ref[sl, cols], f_ref[sl, cols], i_ref[sl, cols], g_ref[sl, cols],
                                 lb_ref[:, cols], ng_ref[:, cols], st_s[h, c])
                gq, gf, gi, gg, dlb, dng, gst = vjp((go_ref[sl, cols], gst))
                gq_ref[sl, cols] = gq.astype(BF16)
                gf_ref[sl, cols] = gf.astype(BF16)
                gi_ref[sl, cols] = gi.astype(BF16)
                gg_ref[sl, cols] = gg.astype(BF16)
                new.append((gst, glb + dlb, gng + dng))
            return tuple(new)

        zv = jnp.zeros((1, HG_DK), F32)
        res = _chunk_loop(nc, bwd, ((zero, zv, zv),) * HG_HEADS_PER_STEP)
        for (_, cols), (_, glb, gng) in zip(heads, res):
            glb_ref[:, cols] = glb
            gng_ref[:, cols] = gng

    part = pl.BlockSpec((None, 1, HG_HEADS_PER_STEP * HG_DK), lambda b, h: (b, 0, h))
    big = jax.ShapeDtypeStruct((rows, width), BF16)
    small = jax.ShapeDtypeStruct((n_batch, 1, width), F32)
    return pl.pallas_call(
        body, name=name, grid=(n_batch, groups),
        in_specs=[col(0), col(1), col(2), col(3), vec, vec, col(0)],
        out_specs=[col(0), col(0), col(0), col(0), part, part],
        out_shape=[big, big, big, big, small, small],
        scratch_shapes=[pltpu.VMEM((HG_HEADS_PER_STEP, nc, HG_DK, HG_DK), F32)],
        compiler_params=_params(("parallel", "parallel")),
    )(proj, proj, proj, proj, lb, ng, g_out)


def _exchange_copies(src, dst, send, recv, loc, scatter):
    x, y, c = lax.axis_index("x"), lax.axis_index("y"), lax.axis_index("c")
    me = 4 * x + 2 * y + c
    local, remote = [], []
    for w in range(len(src)):
        local.append(pltpu.make_async_copy(src[w].at[me] if scatter else src[w], dst[w].at[me], loc.at[w]))
    for k in range(1, N_DEV):
        px = 1 - x if k & 4 else x
        py = 1 - y if k & 2 else y
        pc = 1 - c if k & 1 else c
        peer = 4 * px + 2 * py + pc
        for w in range(len(src)):
            remote.append(pltpu.make_async_remote_copy(
                src_ref=src[w].at[peer] if scatter else src[w], dst_ref=dst[w].at[me],
                send_sem=send.at[w * (N_DEV - 1) + k - 1], recv_sem=recv.at[w * (N_DEV - 1) + k - 1],
                device_id=(px, py, pc), device_id_type=pl.DeviceIdType.MESH))
    return local, remote


_HBM_SPEC = pl.BlockSpec(memory_space=pltpu.HBM)
_SEM_SPEC = pl.BlockSpec(memory_space=pltpu.SEMAPHORE)
_ANY_SPEC = pl.BlockSpec(memory_space=pl.ANY)
_DATAFLOW = pltpu.SideEffectType.DATAFLOW_SIDE_EFFECTING


def _exchange_start(name, srcs, scatter, dep=None):
    nw = len(srcs)
    srcs = [pltpu.with_memory_space_constraint(s, pltpu.HBM) for s in srcs]
    lands = [pltpu.with_memory_space_constraint(lax.empty(s.shape if scatter else (N_DEV,) + s.shape, s.dtype), pltpu.HBM)
             for s in srcs]
    deps = [] if dep is None else [dep]

    def body(*refs):
        src, dst = refs[:nw], refs[nw:2 * nw]
        send, recv, loc = refs[2 * nw + len(deps):2 * nw + len(deps) + 3]
        token = refs[-1]
        local, remote = _exchange_copies(src, dst, send, recv, loc, scatter)
        for cp in local + remote:
            cp.start()
        token[...] = jnp.zeros_like(token)

    sems = [pltpu.SemaphoreType.DMA((nw * (N_DEV - 1),)), pltpu.SemaphoreType.DMA((nw * (N_DEV - 1),)),
            pltpu.SemaphoreType.DMA((nw,))]
    out = pl.pallas_call(
        body, name=name,
        out_shape=(*sems, *[pltpu.HBM(s.shape, s.dtype) for s in srcs], *[pltpu.HBM(s.shape, s.dtype) for s in lands],
                   jax.ShapeDtypeStruct((SUBLANES, LANES), F32)),
        in_specs=[_HBM_SPEC] * (2 * nw) + [_ANY_SPEC] * len(deps),
        out_specs=(_SEM_SPEC, _SEM_SPEC, _SEM_SPEC, *[_HBM_SPEC] * (2 * nw), pl.BlockSpec(memory_space=pltpu.VMEM)),
        input_output_aliases={i: 3 + i for i in range(2 * nw)},
        compiler_params=pltpu.CompilerParams(has_side_effects=_DATAFLOW),
    )(*srcs, *lands, *deps)
    return {"sems": out[:3], "srcs": out[3:3 + nw], "lands": out[3 + nw:3 + 2 * nw], "token": out[-1], "scatter": scatter}


def _exchange_wait(name, handle, after):
    nw = len(handle["srcs"])
    scatter = handle["scatter"]

    def body(*refs):
        src, dst = refs[:nw], refs[nw:2 * nw]
        send, recv, loc = refs[2 * nw:2 * nw + 3]
        local, remote = _exchange_copies(src, dst, send, recv, loc, scatter)
        for cp in local:
            cp.wait()
        for cp in remote:
            cp.wait_send()
            cp.wait_recv()

    out = pl.pallas_call(
        body, name=name,
        out_shape=(*[pltpu.HBM(s.shape, s.dtype) for s in handle["srcs"]],
                   *[pltpu.HBM(s.shape, s.dtype) for s in handle["lands"]]),
        in_specs=[_HBM_SPEC] * (2 * nw) + [_SEM_SPEC] * 3 + [_ANY_SPEC],
        out_specs=tuple([_HBM_SPEC] * (2 * nw)),
        input_output_aliases={i: i for i in range(2 * nw)},
        compiler_params=pltpu.CompilerParams(has_side_effects=_DATAFLOW),
    )(*handle["srcs"], *handle["lands"], *handle["sems"], after)
    return list(out[nw:])


def _adamw(w, g, m, v):
    m = ADAM_B1 * m + (1.0 - ADAM_B1) * g
    v = ADAM_B2 * v + (1.0 - ADAM_B2) * (g * g)
    m_hat = m / (1.0 - ADAM_B1 ** ADAM_STEP)
    v_hat = v / (1.0 - ADAM_B2 ** ADAM_STEP)
    delta = -ADAM_LR * (m_hat / (jnp.sqrt(v_hat) + ADAM_EPS) + ADAM_WD * w)
    return delta, m, v


def _adamw_summed(name, parts, w, m, v):
    rows, cols = w.shape
    n_parts = parts.shape[0]
    tr = _tile(rows, max(SUBLANES, (1 << 18) // cols))

    def body(p_ref, w_ref, m_ref, v_ref, g_ref, d_ref, nm_ref, nv_ref):
        g = p_ref[0].astype(F32)
        for s in range(1, n_parts):
            g = g + p_ref[s].astype(F32)
        d, nm, nv = _adamw(w_ref[...], g, m_ref[...], v_ref[...])
        g_ref[...] = g
        d_ref[...] = d
        nm_ref[...] = nm
        nv_ref[...] = nv

    spec = pl.BlockSpec((tr, cols), lambda i: (i, 0))
    shp = jax.ShapeDtypeStruct((rows, cols), F32)
    return pl.pallas_call(
        body, name=name, grid=(rows // tr,),
        in_specs=[pl.BlockSpec((n_parts, tr, cols), lambda i: (0, i, 0)), spec, spec, spec],
        out_specs=[spec] * 4, out_shape=[shp] * 4, compiler_params=_params(("parallel",)),
    )(parts, w, m, v)


def _pack_rows(arrays, cols):
    out = []
    for a in arrays:
        flat = a.reshape(-1)
        n = -(-flat.shape[0] // cols) * cols
        out.append(jnp.pad(flat, (0, n - flat.shape[0])).reshape(-1, cols))
    packed = jnp.concatenate(out, axis=0)
    return jnp.pad(packed, ((0, -packed.shape[0] % SUBLANES), (0, 0)))


def _unpack_rows(packed, shapes, cols):
    out, r = [], 0
    for s in shapes:
        n = math.prod(s)
        nr = -(-n // cols)
        out.append(packed[r:r + nr].reshape(-1)[:n].reshape(s))
        r += nr
    return out


def _block_diag(blocks):
    g, a, b = blocks.shape
    eye = jnp.eye(g, dtype=blocks.dtype)
    return (eye[:, None, :, None] * blocks[:, :, None, :]).reshape(g * a, g * b)


def _diag_blocks(dense, g):
    a, b = dense.shape[0] // g, dense.shape[1] // g
    return jnp.einsum("gagb->gab", dense.reshape(g, a, g, b))


def _local_step(x, target, meta, wts, small, late_weights, on_grads, on_small):
    n_batch, seq, d = x.shape
    n_meta = meta.shape[0]
    pad = -(seq + n_meta) % LANES
    lead = pad + n_meta
    lp = lead + seq
    rows = n_batch * lp
    s5w = wts["glu"].shape[0]
    n_ab = wts["in_ab"].shape[2]
    ab_cols = wts["in_ab"].shape[0] * n_ab
    sbw = (ab_cols - s5w) // 3
    dff = small["mlp_b_up"].shape[1]
    n_pairs = sbw // LANES
    n_hg = d // HG_DK
    s5_cb = s5w // LANES
    sb_cb = sbw // LANES
    tm = _tile(rows, ROW_TILE)
    groups, n_state, grp = small["s5_b_re"].shape[1:]
    ns = groups * n_state
    sw = min(SCAN_LANES, ns)

    h0 = jnp.concatenate(
        [jnp.zeros((n_batch, pad, d), F32), jnp.broadcast_to(meta[None], (n_batch, n_meta, d)), x], axis=1
    ).reshape(rows, d)

    lam_re, lam_im = small["s5_lam_re"][0], small["s5_lam_im"][0]
    log_dt = small["s5_log_dt"][0][:, None]
    b_re_t = small["s5_b_re"][0].transpose(0, 2, 1)
    b_im_t = small["s5_b_im"][0].transpose(0, 2, 1)
    c_re, c_im = small["s5_c_re"][0], small["s5_c_im"][0]
    lbr, lbi, bbr, bbi = _s5_params("s5_params", lam_re, lam_im, log_dt, b_re_t, b_im_t)
    b_blk = _interleave(_block_diag(bbr), _block_diag(bbi), sw).astype(BF16)
    c_blk = _interleave(_block_diag(c_re), _block_diag(-c_im), sw).T.astype(BF16)
    lam_row = _interleave(lbr.reshape(1, ns), lbi.reshape(1, ns), sw)
    d_row = small["s5_d"].reshape(1, s5w)

    def ln_store(outs, acc, res, bias, g, b):
        r = ALPHA * res + acc + bias
        outs[0][...] = r
        if len(outs) > 1:
            outs[1][...] = _ln(r, g, b)

    zero_bias = jnp.zeros((1, d), F32)

    def mix_ln(name, a, w, k_total, tk, res, bias, g, b, a_fn=None, emit_h=True):
        n_out = 2 if emit_h else 1
        return _mm_act(name, a, w, "nat", n_out_cols=d, k_total=k_total, tn=d, tk=tk, a_fn=a_fn,
                       extras=(res, bias, g, b), extra_specs=(_row_spec(tm, d), _vec_spec(d), _vec_spec(d), _vec_spec(d)),
                       store=ln_store, out_shape=[jax.ShapeDtypeStruct((rows, d), F32)] * n_out,
                       out_specs=[_row_spec(tm, d)] * n_out)

    def two(width):
        return [jax.ShapeDtypeStruct((rows, width), F32)] * 2, [_row_spec(tm, width)] * 2

    proj_ab = _mm_act("in_ab", h0, wts["in_ab"], "stk", n_out_cols=ab_cols, k_total=d, tn=n_ab, tk=d)[0]
    bu = _mm_act("s5_bu", proj_ab, b_blk, "nat", n_out_cols=2 * ns, k_total=s5w, tn=min(2 * ns, 2048), tk=s5w)[0]
    states = _s5_scan("s5_scan", bu, lam_row, n_batch, lp, sw)

    def gelu_store(outs, acc, u, dv):
        ypre = acc + dv * u
        outs[0][...] = ypre
        outs[1][...] = jax.nn.gelu(ypre)

    shp2, spec2 = two(s5w)
    ypre, y = _mm_act(
        "s5_y", states, c_blk, "nat", n_out_cols=s5w, k_total=2 * ns, tn=s5w, tk=min(2 * ns, 1024),
        extras=(proj_ab, d_row), extra_specs=(_row_spec(tm, s5w), _vec_spec(s5w)), store=gelu_store,
        out_shape=shp2, out_specs=spec2)

    def glu_store(outs, acc, yv, bias):
        gate = acc + bias
        outs[0][...] = gate
        outs[1][...] = _glu(yv, gate)

    gate, a_out = _mm_act(
        "s5_glu", y, wts["glu"], "nat", n_out_cols=s5w, k_total=s5w, tn=s5w, tk=s5w,
        extras=(y, small["s5_b_glu"]), extra_specs=(_row_spec(tm, s5w), _vec_spec(s5w)), store=glu_store,
        out_shape=shp2, out_specs=spec2)
    b_out = _attn_fwd("sb_attn", proj_ab, n_batch, lp, pad, s5_cb, s5_cb + sb_cb, s5_cb + 2 * sb_cb, n_pairs)

    def bias_store(outs, acc, bias):
        outs[0][...] = (acc + bias).astype(outs[0].dtype)

    def wide(width, dtype):
        return [jax.ShapeDtypeStruct((rows, dff), dtype)], [_row_spec(tm, width)]

    def mlp_fwd(layer, h_in, emit_h=True):
        shp, spec = wide(n_up, BF16)
        up = _mm_act(f"up{layer}", h_in, wts["up"][layer], "stk", n_out_cols=dff, k_total=d, tn=n_up, tk=d,
                     extras=(small["mlp_b_up"][layer:layer + 1],), extra_specs=(_vec_spec(n_up),), store=bias_store,
                     out_shape=shp, out_specs=spec)[0]
        return (up, *mix_ln(f"down{layer}", up, wts["down"][layer], dff, min(dff, 1024), h_in,
                            small["mlp_b_down"][layer:layer + 1], small["ln_mlp_g"][layer:layer + 1],
                            small["ln_mlp_b"][layer:layer + 1], a_fn=_relu2, emit_h=emit_h))

    r1, h1 = mix_ln("out_ab", [a_out, b_out], wts["out_ab"], s5w + sbw, min(s5w, sbw), h0, zero_bias,
                    small["ln_mix_g"][0:1], small["ln_mix_b"][0:1])
    wts = {**wts, **late_weights(r1)}
    n_c = wts["in_c"].shape[2]
    n_up = wts["up"][0].shape[2]
    up0, r2, h2 = mlp_fwd(0, h1)

    lb = _lower_bound("hg_lb", small["hgrn_gamma"])
    proj_c = _mm_act("in_c", h2, wts["in_c"], "stk", n_out_cols=4 * d, k_total=d, tn=n_c, tk=d)[0]
    c_out = _hgrn_fwd("hgrn", proj_c, lb, wts["ng"], n_batch, lp, pad, n_hg)
    r3, h3 = mix_ln("out_c", c_out, wts["out_c"], d, d, h2, zero_bias, small["ln_mix_g"][1:2], small["ln_mix_b"][1:2])
    up1, r4 = mlp_fwd(1, h3, emit_h=False)

    gr = {}
    g_r4, gr["ln_mlp_g1"], gr["ln_mlp_b1"], loss_tile = _loss_grad(
        "loss", r4, small["ln_mlp_g"][1:2], small["ln_mlp_b"][1:2], target, n_batch, lp, lead)

    def res_store(outs, acc, g_res):
        outs[0][...] = acc + ALPHA * g_res

    def ln_bwd_store(outs, acc, g_res, r_in, g, b, first_step):
        gr_in, gg, gb = jax.vjp(_ln, r_in, g, b)[1](acc + ALPHA * g_res)
        outs[0][...] = gr_in

        @pl.when(first_step)
        def _():
            outs[1][...] = jnp.zeros_like(outs[1])
            outs[2][...] = jnp.zeros_like(outs[2])

        outs[1][...] += gg
        outs[2][...] += gb

    def through_ln(name, a, w, k_total, tk, g_res, r_in, g, b, dep=None):
        vec = pl.BlockSpec((1, d), lambda i, j, k: (0, 0))
        return _mm_act(name, a, w, "stkT", n_out_cols=d, k_total=k_total, tn=d, tk=tk,
                       extras=(g_res, r_in, g, b), extra_specs=(_row_spec(tm, d), _row_spec(tm, d), vec, vec),
                       store=ln_bwd_store, sequential=True, dep=dep,
                       out_shape=[jax.ShapeDtypeStruct((rows, d), F32)] + [jax.ShapeDtypeStruct((1, d), F32)] * 2,
                       out_specs=[_row_spec(tm, d), vec, vec])

    def mlp_bwd(layer, g_r, up, h_in, r_in, send=None):
        def gup_store(outs, acc, upv):
            outs[0][...] = (acc * (2.0 * jnp.maximum(upv.astype(F32), 0.0))).astype(outs[0].dtype)

        tf = min(dff, 1024)
        shp, spec = wide(tf, BF16)
        g_up = _mm_act(f"g_up{layer}", g_r, wts["down"][layer], "natT", n_out_cols=dff, k_total=d, tn=tf, tk=d,
                       extras=(up,), extra_specs=(_row_spec(tm, tf),), store=gup_store, out_shape=shp, out_specs=spec)[0]
        gr[f"down{layer}"], gr[f"mlp_b_down{layer}"] = _mm_wgrad(
            f"dw_down{layer}", up, g_r, kw=dff, n=d, tmw=tf, tn=d, a_fn=_relu2, out_dtype=BF16, colsum=True)
        gr[f"up{layer}"], gr[f"mlp_b_up{layer}"] = _mm_wgrad(
            f"dw_up{layer}", h_in, g_up, kw=d, n=dff, tmw=d, tn=min(dff, 2048), shard_cols=n_up, out_dtype=BF16, colsum=True)
        dep = send() if send is not None else None
        g_r_in, gr[f"ln_mix_g{layer}"], gr[f"ln_mix_b{layer}"] = through_ln(
            f"g_hmid{layer}", g_up, wts["up"][layer], dff, n_up, g_r, r_in,
            small["ln_mix_g"][layer:layer + 1], small["ln_mix_b"][layer:layer + 1], dep=dep)
        return g_r_in

    g_r3 = mlp_bwd(1, g_r4, up1, h3, r3)
    g_cout = _mm_act("g_cout", g_r3, wts["out_c"], "natT", n_out_cols=d, k_total=d, tn=d, tk=d)[0]
    gr["out_c"] = _mm_wgrad("dw_out_c", c_out, g_r3, kw=d, n=d, tmw=d, tn=d, out_dtype=BF16)
    gq, gf, gi, gg_, g_lb_parts, g_ng_parts = _hgrn_bwd("hgrn_bwd", proj_c, lb, wts["ng"], g_cout, n_batch, lp, pad, n_hg)
    g_pc = [gq, gf, gi, gg_]
    gr["hgrn_gamma"], gr["ng"] = _lower_bound_bwd("hg_lb_bwd", small["hgrn_gamma"], g_lb_parts, g_ng_parts)
    gr["in_c"] = _mm_wgrad("dw_in_c", h2, g_pc, kw=d, n=4 * d, tmw=d, tn=d, shard_cols=n_c, out_dtype=BF16)
    sent1 = on_grads(1, {"down1": gr["down1"], "up1": gr["up1"], "out_c": gr["out_c"], "in_c": gr["in_c"], "ng": gr["ng"]})
    g_r2, gr["ln_mlp_g0"], gr["ln_mlp_b0"] = through_ln(
        "g_h2", g_pc, wts["in_c"], 4 * d, n_c, g_r3, r2, small["ln_mlp_g"][0:1], small["ln_mlp_b"][0:1], dep=sent1)

    g_r1 = mlp_bwd(0, g_r2, up0, h1, r1, send=lambda: on_grads(2, {"down0": gr["down0"], "up0": gr["up0"]}))
    g_cat = _mm_act("g_cat", g_r1, wts["out_ab"], "natT", n_out_cols=d, k_total=d, tn=d, tk=d)[0]
    gr["out_ab"] = _mm_wgrad("dw_out_ab", [a_out, b_out], g_r1, kw=s5w + sbw, n=d, tmw=min(s5w, sbw), tn=d, out_dtype=BF16)
    g_q, g_k, g_v = _attn_bwd("sb_attn_bwd", proj_ab, g_cat, n_batch, lp, pad, s5_cb, s5_cb + sb_cb, s5_cb + 2 * sb_cb,
                              s5_cb, n_pairs)

    g_y_direct, g_gate = _rowwise("s5_glu_bwd", lambda ga, yv, gt: jax.vjp(_glu, yv, gt)[1](ga),
                                  [(g_cat, 0, s5w), (y, 0, s5w), (gate, 0, s5w)], 2, s5w)

    def gelu_bwd_store(outs, acc, gyd, yp, u, dv):
        gyp = jax.vjp(jax.nn.gelu, yp)[1](acc + gyd)[0]
        outs[0][...] = gyp
        outs[1][...] = dv * gyp
        outs[2][...] = jnp.sum(gyp * u, axis=0, keepdims=True)

    rs = _row_spec(tm, s5w)
    g_ypre, g_u_direct, gd_parts = _mm_act(
        "s5_g_y", g_gate, wts["glu"], "natT", n_out_cols=s5w, k_total=s5w, tn=s5w, tk=s5w,
        extras=(g_y_direct, ypre, proj_ab, d_row), extra_specs=(rs, rs, rs, _vec_spec(s5w)), store=gelu_bwd_store,
        out_shape=[jax.ShapeDtypeStruct((rows, s5w), F32)] * 2 + [jax.ShapeDtypeStruct((rows // tm, 1, s5w), F32)],
        out_specs=[rs, rs, pl.BlockSpec((None, 1, s5w), lambda i, j, k: (i, 0, j))])
    gr["glu"], gr["s5_b_glu"] = _mm_wgrad("dw_glu", y, g_gate, kw=s5w, n=s5w, tmw=s5w, tn=s5w, out_dtype=BF16, colsum=True)
    g_sd = _mm_act("s5_g_states", g_ypre, c_blk, "natT", n_out_cols=2 * ns, k_total=s5w, tn=min(2 * ns, 2048), tk=s5w)[0]
    d_cblk = _mm_wgrad("dw_cblk", states, g_ypre, kw=2 * ns, n=s5w, tmw=min(2 * ns, 1024), tn=s5w)
    gs, gl_parts = _s5_scan_bwd("s5_scan_bwd", g_sd, states, lam_row, n_batch, lp, sw)

    def add_store(outs, acc, other):
        outs[0][...] = acc + other

    g_u = _mm_act("s5_g_u", gs, b_blk, "natT", n_out_cols=s5w, k_total=2 * ns, tn=s5w, tk=min(2 * ns, 1024),
                  extras=(g_u_direct,), extra_specs=(rs,), store=add_store)[0]
    d_bblk = _mm_wgrad("dw_bblk", proj_ab, gs, kw=s5w, n=2 * ns, tmw=s5w, tn=min(2 * ns, 2048))
    db_re, db_im = _deinterleave(d_bblk, sw)
    dc_re, dc_im = _deinterleave(d_cblk.T, sw)
    glr, gli = _deinterleave(gl_parts, sw)
    g_lam_re, g_lam_im, g_log_dt, g_b_re_t, g_b_im_t, g_d = _s5_params_bwd(
        "s5_params_bwd", lam_re, lam_im, log_dt, b_re_t, b_im_t,
        glr.reshape(n_batch, groups, n_state), gli.reshape(n_batch, groups, n_state),
        _diag_blocks(db_re, groups), _diag_blocks(db_im, groups), gd_parts)

    cat2 = lambda key: jnp.concatenate([gr[key + "0"], gr[key + "1"]], axis=0)
    on_small({
        "s5_lam_re": g_lam_re[None], "s5_lam_im": g_lam_im[None], "s5_log_dt": g_log_dt.reshape(1, groups),
        "s5_b_re": g_b_re_t.transpose(0, 2, 1)[None], "s5_b_im": g_b_im_t.transpose(0, 2, 1)[None],
        "s5_c_re": _diag_blocks(dc_re, groups)[None], "s5_c_im": -_diag_blocks(dc_im, groups)[None],
        "s5_d": g_d.reshape(1, groups, grp), "s5_b_glu": gr["s5_b_glu"], "hgrn_gamma": gr["hgrn_gamma"],
        "ln_mix_g": cat2("ln_mix_g"), "ln_mix_b": cat2("ln_mix_b"), "mlp_b_up": cat2("mlp_b_up"),
        "mlp_b_down": cat2("mlp_b_down"), "ln_mlp_g": cat2("ln_mlp_g"), "ln_mlp_b": cat2("ln_mlp_b"),
    }, loss_tile)

    g_pab = [g_u, g_q, g_k, g_v]
    assert s5w == sbw
    gr["in_ab"] = _mm_wgrad("dw_in_ab", h0, g_pab, kw=d, n=ab_cols, tmw=d, tn=s5w, shard_cols=n_ab, out_dtype=BF16)
    g_h0 = _mm_act("g_h0", g_pab, wts["in_ab"], "stkT", n_out_cols=d, k_total=ab_cols, tn=d, tk=n_ab,
                   extras=(g_r1,), extra_specs=(_row_spec(tm, d),), store=res_store)[0]
    grad_x = g_h0.reshape(n_batch, lp, d)[:, lead:, :]
    g_meta = _meta_grad("g_meta", g_h0, n_batch, lp, pad, n_meta)
    on_grads(3, {"meta": g_meta, "in_ab": gr["in_ab"], "glu": gr["glu"], "out_ab": gr["out_ab"]})
    return grad_x


SMALL_NAMES = ("s5_lam_re", "s5_lam_im", "s5_log_dt", "s5_b_re", "s5_b_im", "s5_c_re", "s5_c_im", "s5_d", "s5_b_glu",
               "hgrn_gamma", "ln_mix_g", "ln_mix_b", "mlp_b_up", "mlp_b_down", "ln_mlp_g", "ln_mlp_b")
WEIGHT_ORDER = ("meta", "w_in_ab", "s5_lam_re", "s5_lam_im", "s5_log_dt", "s5_b_re", "s5_b_im", "s5_c_re", "s5_c_im",
                "s5_d", "s5_w_glu", "s5_b_glu", "w_out_ab", "w_in_c", "hgrn_gamma", "hgrn_norm_g", "w_out_c", "ln_mix_g",
                "ln_mix_b", "mlp_w_up", "mlp_b_up", "mlp_w_down", "mlp_b_down", "ln_mlp_g", "ln_mlp_b")


def kernel(x, meta, w_in_ab, s5_lam_re, s5_lam_im, s5_log_dt, s5_b_re, s5_b_im, s5_c_re, s5_c_im, s5_d, s5_w_glu, s5_b_glu, w_out_ab, w_in_c, hgrn_gamma, hgrn_norm_g, w_out_c, ln_mix_g, ln_mix_b, mlp_w_up, mlp_b_up, mlp_w_down, mlp_b_down, ln_mlp_g, ln_mlp_b, loss_target, m_meta, m_w_in_ab, m_s5_lam_re, m_s5_lam_im, m_s5_log_dt, m_s5_b_re, m_s5_b_im, m_s5_c_re, m_s5_c_im, m_s5_d, m_s5_w_glu, m_s5_b_glu, m_w_out_ab, m_w_in_c, m_hgrn_gamma, m_hgrn_norm_g, m_w_out_c, m_ln_mix_g, m_ln_mix_b, m_mlp_w_up, m_mlp_b_up, m_mlp_w_down, m_mlp_b_down, m_ln_mlp_g, m_ln_mlp_b, v_meta, v_w_in_ab, v_s5_lam_re, v_s5_lam_im, v_s5_log_dt, v_s5_b_re, v_s5_b_im, v_s5_c_re, v_s5_c_im, v_s5_d, v_s5_w_glu, v_s5_b_glu, v_w_out_ab, v_w_in_c, v_hgrn_gamma, v_hgrn_norm_g, v_w_out_c, v_ln_mix_g, v_ln_mix_b, v_mlp_w_up, v_mlp_b_up, v_mlp_w_down, v_mlp_b_down, v_ln_mlp_g, v_ln_mlp_b):
    args = dict(locals())
    w = {n: args[n] for n in WEIGHT_ORDER}
    mom = {n: args["m_" + n] for n in WEIGHT_ORDER}
    var = {n: args["v_" + n] for n in WEIGHT_ORDER}
    d = x.shape[2]
    n_meta = meta.shape[0]

    cast = lambda a: a.astype(BF16)
    early = _exchange_start("gather_early_start", [w["meta"], cast(w["w_in_ab"][0]), cast(w["s5_w_glu"][0]),
                                                   cast(w["w_out_ab"][0])], False)
    late = _exchange_start("gather_late_start", [w["hgrn_norm_g"], cast(w["w_in_c"][0]), cast(w["w_out_c"][0]),
                                                 cast(w["mlp_w_up"][0]), cast(w["mlp_w_up"][1]),
                                                 cast(w["mlp_w_down"][0]), cast(w["mlp_w_down"][1])], False, dep=early["token"])
    a_meta, a_in_ab, a_glu, a_out_ab = _exchange_wait("gather_early_wait", early, late["token"])
    wts = {"in_ab": a_in_ab, "glu": a_glu.reshape(-1, a_glu.shape[2]), "out_ab": a_out_ab.reshape(-1, d)}
    meta_full = a_meta.transpose(1, 0, 2).reshape(n_meta, d)
    small = {n: w[n] for n in SMALL_NAMES}

    def late_weights(after):
        a_ng, a_in_c, a_out_c, a_up0, a_up1, a_dn0, a_dn1 = _exchange_wait("gather_late_wait", late, after)
        return {"in_c": a_in_c, "ng": a_ng.transpose(1, 0, 2).reshape(1, d), "out_c": a_out_c.reshape(-1, d),
                "up": [a_up0, a_up1], "down": [a_dn0.reshape(-1, d), a_dn1.reshape(-1, d)]}

    n_loc = d // N_DEV
    rows_of = lambda g: g.reshape(N_DEV, -1, g.shape[-1])
    cols_of = lambda g: g.reshape(g.shape[0], N_DEV, n_loc).transpose(1, 0, 2)
    sent = {}

    def on_grads(stage, g):
        if stage == 1:
            order = (("mlp_w_down", 1), ("mlp_w_up", 1), ("w_out_c", 0), ("w_in_c", 0), ("hgrn_norm_g", None))
            parts = [rows_of(g["down1"]), g["up1"], rows_of(g["out_c"]), g["in_c"], cols_of(g["ng"])]
        elif stage == 2:
            order = (("mlp_w_down", 0), ("mlp_w_up", 0))
            parts = [rows_of(g["down0"]), g["up0"]]
        else:
            order = (("w_out_ab", 0), ("s5_w_glu", 0), ("w_in_ab", 0), ("meta", None))
            parts = [rows_of(g["out_ab"]), rows_of(g["glu"]), g["in_ab"], cols_of(g["meta"])]
        sent[stage] = (order, _exchange_start(f"scatter_start{stage}", parts, True))
        return sent[stage][1]["token"]

    def on_small(sg, loss_tile):
        g_pack = _pack_rows([sg[n] for n in SMALL_NAMES] + [loss_tile], PACK_COLS)
        sent["small"] = _exchange_start("gather_small_start", [g_pack], False)

    grad_x = _local_step(x, loss_target, meta_full, wts, small, late_weights, on_grads, on_small)
    small_sent = sent["small"]
    tile = (SUBLANES, LANES)
    shapes = [w[n].shape for n in SMALL_NAMES] + [tile]
    zeros = jnp.zeros(tile, F32)
    w_pack = _pack_rows([w[n] for n in SMALL_NAMES] + [zeros], PACK_COLS)
    m_pack = _pack_rows([mom[n] for n in SMALL_NAMES] + [zeros], PACK_COLS)
    v_pack = _pack_rows([var[n] for n in SMALL_NAMES] + [zeros], PACK_COLS)

    def apply(stage, after):
        order, handle = sent[stage]
        recv = _exchange_wait(f"scatter_wait{stage}", handle, after)
        for (nm, ly), rc in zip(order, recv):
            sel = (lambda t: t) if ly is None else (lambda t, ly=ly: t[ly])
            res[(nm, ly)] = _adamw_summed(f"adamw_{nm}_{ly}", rc, sel(w[nm]), sel(mom[nm]), sel(var[nm]))
        return res[order[0]][0]

    res = {}
    done = apply(2, apply(1, sent[3][1]["token"]))
    g_all = _exchange_wait("gather_small_wait", small_sent, done)[0]
    packed = _adamw_summed("adamw_small", g_all, w_pack, m_pack, v_pack)
    apply(3, packed[0])
    unpacked = [_unpack_rows(p, shapes, PACK_COLS) for p in packed]
    loss = unpacked[0][-1][0, 0]

    def pick(nm, which):
        if nm in SMALL_NAMES:
            return unpacked[which][SMALL_NAMES.index(nm)]
        if (nm, None) in res:
            return res[(nm, None)][which]
        return jnp.stack([res[(nm, ly)][which] for ly in range(w[nm].shape[0])], axis=0)

    return (loss, grad_x, *[pick(n, 0) for n in WEIGHT_ORDER], *[pick(n, 1) for n in WEIGHT_ORDER],
            *[pick(n, 2) for n in WEIGHT_ORDER], *[pick(n, 3) for n in WEIGHT_ORDER])
```

```python
import functools
import math

import jax
import jax.numpy as jnp
from jax import lax
from jax.experimental import pallas as pl
from jax.experimental.pallas import tpu as pltpu

F32 = jnp.float32
BF16 = jnp.bfloat16

N_DEV = 8
DEPTH = 2
ALPHA = (2.0 * DEPTH) ** 0.25
LN_EPS = 1e-5
RMS_EPS = 1e-6
SB_HEAD_DIM = 64
HG_DK = 128
HG_CHUNK = 64
LANES = 128
SUBLANES = 8
VMEM_LIMIT_BYTES = 56 * 1024 * 1024
ROW_TILE = 1088
SCAN_LANES = 256
SCAN_UNROLL = 4
PACK_COLS = 1024

ADAM_LR = 0.001
ADAM_B1 = 0.9
ADAM_B2 = 0.999
ADAM_EPS = 1e-08
ADAM_WD = 0.01
ADAM_STEP = 10

NN = (((1,), (0,)), ((), ()))
NT = (((1,), (1,)), ((), ()))
TN = (((0,), (0,)), ((), ()))


def _tile(n, pref, align=SUBLANES):
    t = min(n, pref)
    t -= t % align
    while t >= align:
        if n % t == 0:
            return t
        t -= align
    return n


def _unrolled_loop(n, body, init, unroll):
    assert n % unroll == 0

    def outer(t, carry):
        for u in range(unroll):
            carry = body(t * unroll + u, carry)
        return carry

    return lax.fori_loop(0, n // unroll, outer, init)


def _params(sem):
    return pltpu.CompilerParams(dimension_semantics=sem, vmem_limit_bytes=VMEM_LIMIT_BYTES)


def _dot_raw(a, b, dims):
    return lax.dot_general(a.astype(BF16), b.astype(BF16), dims, preferred_element_type=F32)


def _make_dot(dims, da_rule, db_rule):
    @jax.custom_vjp
    def f(a, b):
        return _dot_raw(a, b, dims)

    def fwd(a, b):
        return _dot_raw(a, b, dims), (a, b)

    def bwd(res, g):
        a, b = res
        return da_rule(g, a, b), db_rule(g, a, b)

    f.defvjp(fwd, bwd)
    return f


_DOTS = {
    NN: _make_dot(NN, lambda g, a, b: _dot_raw(g, b, NT), lambda g, a, b: _dot_raw(a, g, TN)),
    NT: _make_dot(NT, lambda g, a, b: _dot_raw(g, b, NN), lambda g, a, b: _dot_raw(g, a, TN)),
    TN: _make_dot(TN, lambda g, a, b: _dot_raw(b, g, NT), lambda g, a, b: _dot_raw(a, g, NN)),
}


def _dot(a, b, dims):
    return _DOTS[dims](a, b)


def _running_sums(a, tri_ones, split=False):
    hi = a.astype(BF16)
    out = lax.dot_general(hi, tri_ones, NN, preferred_element_type=F32)
    if split:
        lo = (a - hi.astype(F32)).astype(BF16)
        out = out + lax.dot_general(lo, tri_ones, NN, preferred_element_type=F32)
    return out


def _piece_specs(pieces, block_rows, block_cols, row_of, col_of, cb0):
    per = pieces[0].shape[1] // block_cols if len(pieces) > 1 else None
    specs = []
    for p in range(len(pieces)):
        if per is None:
            specs.append(pl.BlockSpec((block_rows, block_cols), lambda *g: (row_of(*g), cb0 + col_of(*g))))
        else:
            specs.append(pl.BlockSpec(
                (block_rows, block_cols),
                lambda *g, p=p: (row_of(*g), jnp.clip(col_of(*g) - p * per, 0, per - 1))))
    return specs, per


def _mm_call(name, grid, dims, a_pieces, a_specs, a_sel, b_pieces, b_specs, b_sel, extras, extra_specs,
             out_shape, out_specs, acc_shape, a_fn, store, colsum_width=0, sequential=False, deps=()):
    na, nb, ne, no, nd = len(a_pieces), len(b_pieces), len(extras), len(out_shape), len(deps)
    nk = grid[2]

    def body(*refs):
        a_refs, b_refs = refs[:na], refs[na:na + nb]
        extra = refs[na + nb:na + nb + ne]
        outs = refs[na + nb + ne + nd:na + nb + ne + nd + no]
        acc = refs[na + nb + ne + nd + no]
        ids = (pl.program_id(0), pl.program_id(1), pl.program_id(2))
        k = ids[2]

        @pl.when(k == 0)
        def _():
            acc[...] = jnp.zeros_like(acc)

        def run(a_ref, b_ref):
            a = a_ref[...]
            if a_fn is not None:
                a = a_fn(a)
            b = b_ref[...]
            acc[...] += _dot_raw(a, b, dims)
            if colsum_width:
                cs = refs[-1]
                first = ids[1] == 0

                @pl.when(first & (k == 0))
                def _():
                    cs[...] = jnp.zeros_like(cs)

                @pl.when(first)
                def _():
                    cs[...] += jnp.sum(b.astype(F32), axis=0, keepdims=True)

        if na == 1 and nb == 1:
            run(a_refs[0], b_refs[0])
        elif nb == 1:
            per, fn = a_sel
            which = fn(*ids) // per
            for p in range(na):
                pl.when(which == p)(functools.partial(run, a_refs[p], b_refs[0]))
        else:
            assert na == 1
            per, fn = b_sel
            which = fn(*ids) // per
            for p in range(nb):
                pl.when(which == p)(functools.partial(run, a_refs[0], b_refs[p]))

        @pl.when(k == nk - 1)
        def _():
            if sequential:
                store(outs, acc[...], *[e[...] for e in extra], first_step=(ids[0] == 0) & (ids[1] == 0))
            else:
                store(outs, acc[...], *[e[...] for e in extra])
            if colsum_width:
                @pl.when(ids[1] == 0)
                def _():
                    outs[-1][...] = refs[-1][...]

    scratch = [pltpu.VMEM(acc_shape, F32)]
    if colsum_width:
        scratch.append(pltpu.VMEM((1, colsum_width), F32))
    sem = ("parallel", "arbitrary", "arbitrary") if colsum_width else ("parallel", "parallel", "arbitrary")
    if sequential:
        sem = ("arbitrary",) * 3
    return pl.pallas_call(
        body, name=name, grid=grid,
        in_specs=[*a_specs, *b_specs, *extra_specs, *[pl.BlockSpec(memory_space=pl.ANY)] * nd], out_specs=out_specs,
        out_shape=out_shape, scratch_shapes=scratch, compiler_params=_params(sem),
    )(*a_pieces, *b_pieces, *extras, *deps)


def _store_plain(outs, acc):
    outs[0][...] = acc.astype(outs[0].dtype)


def _row_spec(tm, tn):
    return pl.BlockSpec((tm, tn), lambda i, j, k: (i, j))


def _vec_spec(tn):
    return pl.BlockSpec((1, tn), lambda i, j, k: (0, j))


def _mm_act(name, a, w, wkind, *, n_out_cols, k_total, tn, tk, a_cb0=0, a_fn=None, extras=(), extra_specs=(),
            store=_store_plain, out_shape=None, out_specs=None, sequential=False, dep=None):
    a_pieces = list(a) if isinstance(a, (list, tuple)) else [a]
    rows = a_pieces[0].shape[0]
    tm = _tile(rows, ROW_TILE)
    grid = (rows // tm, n_out_cols // tn, k_total // tk)
    a_specs, per = _piece_specs(a_pieces, tm, tk, lambda i, j, k: i, lambda i, j, k: k, a_cb0)
    if wkind == "nat":
        b_spec, dims = pl.BlockSpec((tk, tn), lambda i, j, k: (k, j)), NN
    elif wkind == "stk":
        assert tn == w.shape[2]
        b_spec, dims = pl.BlockSpec((None, tk, tn), lambda i, j, k: (j, k, 0)), NN
    elif wkind == "natT":
        b_spec, dims = pl.BlockSpec((tn, tk), lambda i, j, k: (j, k)), NT
    else:
        assert wkind == "stkT" and tk == w.shape[2]
        b_spec, dims = pl.BlockSpec((None, tn, tk), lambda i, j, k: (k, j, 0)), NT
    if out_shape is None:
        out_shape = [jax.ShapeDtypeStruct((rows, n_out_cols), F32)]
        out_specs = [_row_spec(tm, tn)]
    return _mm_call(name, grid, dims, a_pieces, a_specs, (per, lambda i, j, k: k), [w], [b_spec], None,
                    list(extras), list(extra_specs), out_shape, out_specs, (tm, tn), a_fn, store,
                    sequential=sequential, deps=() if dep is None else (dep,))


def _mm_wgrad(name, a, g, *, kw, n, tmw, tn, a_cb0=0, a_fn=None, shard_cols=0, out_dtype=F32, colsum=False, dep=None):
    a_pieces = list(a) if isinstance(a, (list, tuple)) else [a]
    g_pieces = list(g) if isinstance(g, (list, tuple)) else [g]
    rows = a_pieces[0].shape[0]
    tr = _tile(rows, ROW_TILE)
    grid = (n // tn, kw // tmw, rows // tr)
    a_specs, a_per = _piece_specs(a_pieces, tr, tmw, lambda j, i, k: k, lambda j, i, k: i, a_cb0)
    g_specs, g_per = _piece_specs(g_pieces, tr, tn, lambda j, i, k: k, lambda j, i, k: j, 0)
    if shard_cols:
        per = tn // shard_cols
        out_shape = [jax.ShapeDtypeStruct((n // shard_cols, kw, shard_cols), out_dtype)]
        out_specs = [pl.BlockSpec((per, tmw, shard_cols), lambda j, i, k: (j, i, 0))]

        def store(outs, acc):
            for q in range(per):
                outs[0][q] = acc[:, q * shard_cols:(q + 1) * shard_cols].astype(out_dtype)
    else:
        out_shape = [jax.ShapeDtypeStruct((kw, n), out_dtype)]
        out_specs = [pl.BlockSpec((tmw, tn), lambda j, i, k: (i, j))]

        def store(outs, acc):
            outs[0][...] = acc.astype(out_dtype)
    if colsum:
        out_shape.append(jax.ShapeDtypeStruct((1, n), F32))
        out_specs.append(pl.BlockSpec((1, tn), lambda j, i, k: (0, j)))
    res = _mm_call(name, grid, TN, a_pieces, a_specs, (a_per, lambda j, i, k: i), g_pieces, g_specs,
                   (g_per, lambda j, i, k: j), [], [], out_shape, out_specs, (tmw, tn), a_fn, store,
                   colsum_width=tn if colsum else 0, deps=() if dep is None else (dep,))
    return res if colsum else res[0]


def _ln(x, g, b):
    mu = jnp.mean(x, axis=-1, keepdims=True)
    xc = x - mu
    var = jnp.mean(xc * xc, axis=-1, keepdims=True)
    return xc * lax.rsqrt(var + LN_EPS) * g + b


def _relu2(x):
    r = jnp.maximum(x.astype(F32), 0.0)
    return r * r


def _glu(y, gate):
    return y * jax.nn.sigmoid(gate)


def _rowwise(name, fn, ins, n_out, width):
    rows = ins[0][0].shape[0]
    tm = _tile(rows, ROW_TILE)

    def body(*refs):
        res = fn(*[r[...] for r in refs[:len(ins)]])
        for o, v in zip(refs[len(ins):], res):
            o[...] = v

    return pl.pallas_call(
        body, name=name, grid=(rows // tm,),
        in_specs=[pl.BlockSpec((tm, wd), lambda i, cb=cb: (i, cb)) for _, cb, wd in ins],
        out_specs=[pl.BlockSpec((tm, width), lambda i: (i, 0))] * n_out,
        out_shape=[jax.ShapeDtypeStruct((rows, width), F32)] * n_out, compiler_params=_params(("parallel",)),
    )(*[a for a, _, _ in ins])


def _loss_grad(name, r, g, b, target, n_batch, lp, lead):
    rows, d = r.shape
    nq = lp // LANES
    lead_blocks = lead // LANES

    def body(r_ref, g_ref, b_ref, t_ref, gr_ref, gg_ref, gb_ref, loss_ref):
        i = pl.program_id(1)

        @pl.when((pl.program_id(0) == 0) & (i == 0))
        def _():
            loss_ref[...] = jnp.zeros_like(loss_ref)
            gg_ref[...] = jnp.zeros_like(gg_ref)
            gb_ref[...] = jnp.zeros_like(gb_ref)

        h, vjp = jax.vjp(_ln, r_ref[...], g_ref[...], b_ref[...])
        diff = jnp.where(i >= lead_blocks, h - t_ref[...], 0.0)
        gr, gg, gb = vjp(diff * (1.0 / d))
        gr_ref[...] = gr
        gg_ref[...] += gg
        gb_ref[...] += gb
        loss_ref[...] += 0.5 * jnp.sum(diff * diff) * (1.0 / d)

    vec = pl.BlockSpec((1, d), lambda b, i: (0, 0))
    row = pl.BlockSpec((LANES, d), lambda b, i: (b * nq + i, 0))
    return pl.pallas_call(
        body, name=name, grid=(n_batch, nq),
        in_specs=[row, vec, vec, pl.BlockSpec((None, LANES, d), lambda b, i: (b, jnp.maximum(i - lead_blocks, 0), 0))],
        out_specs=[row, vec, vec, pl.BlockSpec((SUBLANES, LANES), lambda b, i: (0, 0))],
        out_shape=[jax.ShapeDtypeStruct((rows, d), F32), jax.ShapeDtypeStruct((1, d), F32),
                   jax.ShapeDtypeStruct((1, d), F32), jax.ShapeDtypeStruct((SUBLANES, LANES), F32)],
        compiler_params=_params(("arbitrary", "arbitrary")),
    )(r, g, b, target)


def _meta_grad(name, g_h0, n_batch, lp, pad, n_meta):
    d = g_h0.shape[1]
    per = lp // n_meta
    at = pad // n_meta

    def body(g_ref, o_ref):
        @pl.when(pl.program_id(0) == 0)
        def _():
            o_ref[...] = jnp.zeros_like(o_ref)

        o_ref[...] += g_ref[...]

    return pl.pallas_call(
        body, name=name, grid=(n_batch,),
        in_specs=[pl.BlockSpec((n_meta, d), lambda b: (b * per + at, 0))],
        out_specs=pl.BlockSpec((n_meta, d), lambda b: (0, 0)),
        out_shape=jax.ShapeDtypeStruct((n_meta, d), F32),
        compiler_params=_params(("arbitrary",)),
    )(g_h0)


def _s5_param_fn(lr, li, ldt, br, bi):
    dt = jnp.exp(ldt)
    e = jnp.exp(lr * dt)
    w = li * dt
    lbr = e * jnp.cos(w)
    lbi = e * jnp.sin(w)
    nr = lbr - 1.0
    den = lr * lr + li * li
    cr = (nr * lr + lbi * li) / den
    ci = (lbi * lr - nr * li) / den
    bbr = cr[:, None, :] * br - ci[:, None, :] * bi
    bbi = cr[:, None, :] * bi + ci[:, None, :] * br
    return lbr, lbi, bbr, bbi


def _s5_params(name, lr, li, ldt, br, bi):
    def body(lr_ref, li_ref, ldt_ref, br_ref, bi_ref, o1, o2, o3, o4):
        res = _s5_param_fn(lr_ref[...], li_ref[...], ldt_ref[...], br_ref[...], bi_ref[...])
        for o, v in zip((o1, o2, o3, o4), res):
            o[...] = v

    shp = [jax.ShapeDtypeStruct(lr.shape, F32)] * 2 + [jax.ShapeDtypeStruct(br.shape, F32)] * 2
    return pl.pallas_call(body, name=name, out_shape=shp)(lr, li, ldt, br, bi)


def _s5_params_bwd(name, lr, li, ldt, br, bi, g_lbr, g_lbi, g_bbr, g_bbi, gd_parts):
    def body(lr_ref, li_ref, ldt_ref, br_ref, bi_ref, g1, g2, g3, g4, gd_ref, o1, o2, o3, o4, o5, o6):
        _, vjp = jax.vjp(_s5_param_fn, lr_ref[...], li_ref[...], ldt_ref[...], br_ref[...], bi_ref[...])
        res = vjp((jnp.sum(g1[...], axis=0), jnp.sum(g2[...], axis=0), g3[...], g4[...]))
        for o, v in zip((o1, o2, o3, o4, o5), res):
            o[...] = v
        o6[...] = jnp.sum(gd_ref[...], axis=0)

    shp = ([jax.ShapeDtypeStruct(lr.shape, F32)] * 2 + [jax.ShapeDtypeStruct(ldt.shape, F32)]
           + [jax.ShapeDtypeStruct(br.shape, F32)] * 2 + [jax.ShapeDtypeStruct(gd_parts.shape[1:], F32)])
    return pl.pallas_call(body, name=name, out_shape=shp)(lr, li, ldt, br, bi, g_lbr, g_lbi, g_bbr, g_bbi, gd_parts)


def _interleave(re, im, w):
    lead = re.shape[:-1]
    nj = re.shape[-1] // w
    return jnp.stack([re.reshape(*lead, nj, w), im.reshape(*lead, nj, w)], axis=-2).reshape(*lead, 2 * nj * w)


def _deinterleave(x, w):
    lead = x.shape[:-1]
    nj = x.shape[-1] // (2 * w)
    y = x.reshape(*lead, nj, 2, w)
    return y[..., 0, :].reshape(*lead, nj * w), y[..., 1, :].reshape(*lead, nj * w)


def _cmul(ar, ai, br, bi):
    return ar * br - ai * bi, ar * bi + ai * br


def _powers(lr, li):
    p = [(lr, li)]
    p.append(_cmul(*p[0], *p[0]))
    p.append(_cmul(*p[1], *p[0]))
    p.append(_cmul(*p[1], *p[1]))
    p.append(_cmul(*p[3], *p[0]))
    p.append(_cmul(*p[3], *p[1]))
    p.append(_cmul(*p[3], *p[2]))
    p.append(_cmul(*p[3], *p[3]))
    return p


def _scan_tile(xr, xi, steps):
    for sh, br, bi, m in steps:
        rr = jnp.where(m, pltpu.roll(xr, sh, 0), 0.0)
        ri = jnp.where(m, pltpu.roll(xi, sh, 0), 0.0)
        xr, xi = xr + (br * rr - bi * ri), xi + (br * ri + bi * rr)
    return xr, xi


def _s5_scan(name, bu, lam, n_batch, lp, w):
    rows, two_ns = bu.shape
    nj = two_ns // (2 * w)
    nt = lp // SUBLANES

    def body(x_ref, lam_ref, s_ref):
        pw = _powers(lam_ref[:, :w], lam_ref[:, w:])
        tab_r = jnp.concatenate([p[0] for p in pw], axis=0)
        tab_i = jnp.concatenate([p[1] for p in pw], axis=0)
        row = lax.broadcasted_iota(jnp.int32, (SUBLANES, w), 0)
        steps = [(s, jnp.broadcast_to(pw[s - 1][0], (SUBLANES, w)), jnp.broadcast_to(pw[s - 1][1], (SUBLANES, w)),
                  row >= s) for s in (1, 2, 4)]

        def tile(t, carry):
            cr, ci = carry
            r0 = pl.multiple_of(t * SUBLANES, SUBLANES)
            x = x_ref[pl.ds(r0, SUBLANES), :]
            xr, xi = _scan_tile(x[:, :w], x[:, w:], steps)
            sr = xr + (tab_r * cr - tab_i * ci)
            si = xi + (tab_r * ci + tab_i * cr)
            s_ref[pl.ds(r0, SUBLANES), :] = jnp.concatenate([sr, si], axis=1)
            return sr[SUBLANES - 1:, :], si[SUBLANES - 1:, :]

        zero = jnp.zeros((1, w), F32)
        _unrolled_loop(nt, tile, (zero, zero), SCAN_UNROLL)

    spec = pl.BlockSpec((lp, 2 * w), lambda b, j: (b, j))
    return pl.pallas_call(
        body, name=name, grid=(n_batch, nj), in_specs=[spec, pl.BlockSpec((1, 2 * w), lambda b, j: (0, j))],
        out_specs=spec, out_shape=jax.ShapeDtypeStruct((rows, two_ns), F32),
        compiler_params=_params(("parallel", "parallel")),
    )(bu, lam)


def _s5_scan_bwd(name, gd, states, lam, n_batch, lp, w):
    rows, two_ns = gd.shape
    nj = two_ns // (2 * w)
    nt = lp // SUBLANES

    def body(x_ref, s_ref, lam_ref, g_ref, gl_ref):
        pw = _powers(lam_ref[:, :w], -lam_ref[:, w:])
        tab_r = jnp.concatenate([p[0] for p in reversed(pw)], axis=0)
        tab_i = jnp.concatenate([p[1] for p in reversed(pw)], axis=0)
        row = lax.broadcasted_iota(jnp.int32, (SUBLANES, w), 0)
        steps = [(SUBLANES - s, jnp.broadcast_to(pw[s - 1][0], (SUBLANES, w)),
                  jnp.broadcast_to(pw[s - 1][1], (SUBLANES, w)), row < SUBLANES - s) for s in (1, 2, 4)]

        def tile(u, carry):
            cr, ci, ar, ai = carry
            t = nt - 1 - u
            r0 = pl.multiple_of(t * SUBLANES, SUBLANES)
            x = x_ref[pl.ds(r0, SUBLANES), :]
            xr, xi = _scan_tile(x[:, :w], x[:, w:], steps)
            gr = xr + (tab_r * cr - tab_i * ci)
            gi = xi + (tab_r * ci + tab_i * cr)
            g_ref[pl.ds(r0, SUBLANES), :] = jnp.concatenate([gr, gi], axis=1)
            p0 = pl.multiple_of(jnp.maximum(t - 1, 0) * SUBLANES, SUBLANES)
            prev = s_ref[pl.ds(p0, SUBLANES), :][SUBLANES - 1:, :] * jnp.where(t > 0, 1.0, 0.0)
            cur = s_ref[pl.ds(r0, SUBLANES), :]
            spr = jnp.where(row >= 1, pltpu.roll(cur[:, :w], 1, 0), prev[:, :w])
            spi = jnp.where(row >= 1, pltpu.roll(cur[:, w:], 1, 0), prev[:, w:])
            return gr[:1, :], gi[:1, :], ar + gr * spr + gi * spi, ai + gi * spr - gr * spi

        z1 = jnp.zeros((1, w), F32)
        z8 = jnp.zeros((SUBLANES, w), F32)
        _, _, ar, ai = _unrolled_loop(nt, tile, (z1, z1, z8, z8), SCAN_UNROLL)
        gl_ref[...] = jnp.concatenate([jnp.sum(ar, axis=0, keepdims=True), jnp.sum(ai, axis=0, keepdims=True)], axis=1)

    spec = pl.BlockSpec((lp, 2 * w), lambda b, j: (b, j))
    return pl.pallas_call(
        body, name=name, grid=(n_batch, nj),
        in_specs=[spec, spec, pl.BlockSpec((1, 2 * w), lambda b, j: (0, j))],
        out_specs=[spec, pl.BlockSpec((None, 1, 2 * w), lambda b, j: (b, 0, j))],
        out_shape=[jax.ShapeDtypeStruct((rows, two_ns), F32), jax.ShapeDtypeStruct((n_batch, 1, two_ns), F32)],
        compiler_params=_params(("parallel", "parallel")),
    )(gd, states, lam)


def _log_sigmoid(z):
    return jnp.minimum(z, 0.0) - jnp.log(1.0 + jnp.exp(-jnp.abs(z)))


ATTN_KEYS = 256
ATTN_GROUP = 4


def _attn_block(i, jb, lp, pad):
    start = jb * ATTN_KEYS
    r0 = pl.multiple_of(jnp.minimum(start, lp - ATTN_KEYS), LANES)
    rowpos = i * LANES + lax.broadcasted_iota(jnp.int32, (LANES, ATTN_KEYS), 0)
    keypos = r0 + lax.broadcasted_iota(jnp.int32, (LANES, ATTN_KEYS), 1)
    return r0, (keypos < rowpos) & (keypos >= jnp.maximum(start, pad))


def _tri_ones(strict_upper):
    r = lax.broadcasted_iota(jnp.int32, (ATTN_KEYS, ATTN_KEYS + LANES), 0)
    c = lax.broadcasted_iota(jnp.int32, (ATTN_KEYS, ATTN_KEYS + LANES), 1)
    tri = (r > c) if strict_upper else (r < c)
    return jnp.where((c >= ATTN_KEYS) | tri, 1.0, 0.0).astype(BF16)


def _split_sums(cr):
    rs = cr[:, ATTN_KEYS:]
    return cr[:, :ATTN_KEYS], jnp.concatenate([rs] * (ATTN_KEYS // LANES), axis=1)


def _head_masks():
    lane = lax.broadcasted_iota(jnp.int32, (1, LANES), 1)
    return [lane < SB_HEAD_DIM, lane >= SB_HEAD_DIM]


def _run_groups(n, first, sign, make):
    j, left, g = first, n, ATTN_GROUP
    while g >= 1:
        shift = g.bit_length() - 1
        count = lax.shift_right_logical(left, shift)
        fn = make(g)

        def loop(_, jcur, fn=fn, g=g):
            fn(jcur)
            return jcur + sign * g

        j = lax.fori_loop(0, count, loop, j)
        left = left - lax.shift_left(count, shift)
        g //= 2


def _attn_fwd(name, proj, n_batch, lp, pad, q_cb, k_cb, v_cb, n_pairs):
    rows = proj.shape[0]
    nq = lp // LANES
    scale = SB_HEAD_DIM ** -0.5

    def body(q_ref, k_ref, v_ref, o_ref, acc_s):
        i = pl.program_id(2)
        hm = _head_masks()
        comb = _tri_ones(True)
        qs = q_ref[...] * scale
        qh = [jnp.where(m, qs, 0.0).astype(BF16) for m in hm]
        acc_s[...] = jnp.zeros_like(acc_s)
        o_ref[...] = jnp.zeros_like(o_ref)

        def make(group):
            def fn(jtop):
                chains = []
                for g in range(group):
                    r0, vis = _attn_block(i, jtop - g, lp, pad)
                    kj = k_ref[pl.ds(r0, ATTN_KEYS), :].astype(BF16)
                    vj = v_ref[pl.ds(r0, ATTN_KEYS), :]
                    for h in range(2):
                        z = lax.dot_general(qh[h], kj, NT, preferred_element_type=F32)
                        chains.append((h, vis, z, jnp.where(hm[h], vj, 0.0).astype(BF16)))
                staged = []
                for h, vis, z, vh in chains:
                    lsz = _log_sigmoid(z)
                    staged.append((h, vis, lsz, _running_sums(jnp.where(vis, lsz - z, 0.0), comb, split=True), vh))
                out = o_ref[...]
                for h, vis, lsz, cr, vh in staged:
                    later, rs = _split_sums(cr)
                    acc = acc_s[h]
                    wgt = jnp.where(vis, jnp.exp(lsz + later + acc), 0.0)
                    acc_s[h] = acc + rs
                    out = out + lax.dot_general(wgt.astype(BF16), vh, NN, preferred_element_type=F32)
                o_ref[...] = out
            return fn

        n_blocks = lax.shift_right_logical(i + ATTN_KEYS // LANES, (ATTN_KEYS // LANES).bit_length() - 1)
        _run_groups(n_blocks, n_blocks - 1, -1, make)

    return pl.pallas_call(
        body, name=name, grid=(n_batch, n_pairs, nq),
        in_specs=[pl.BlockSpec((LANES, LANES), lambda b, h, i: (b * nq + i, q_cb + h)),
                  pl.BlockSpec((lp, LANES), lambda b, h, i: (b, k_cb + h)),
                  pl.BlockSpec((lp, LANES), lambda b, h, i: (b, v_cb + h))],
        out_specs=pl.BlockSpec((LANES, LANES), lambda b, h, i: (b * nq + i, h)),
        out_shape=jax.ShapeDtypeStruct((rows, n_pairs * LANES), F32),
        scratch_shapes=[pltpu.VMEM((2, LANES, ATTN_KEYS), F32)],
        compiler_params=_params(("parallel", "parallel", "arbitrary")),
    )(proj, proj, proj)


def _attn_bwd(name, proj, g_out, n_batch, lp, pad, q_cb, k_cb, v_cb, go_cb, n_pairs):
    rows = proj.shape[0]
    nq = lp // LANES
    scale = SB_HEAD_DIM ** -0.5

    def body(q_ref, k_ref, v_ref, go_ref, gq_ref, gk_ref, gv_ref, ga_s, sz_s, acc_s):
        i = pl.program_id(2)

        @pl.when(i == 0)
        def _():
            gk_ref[...] = jnp.zeros_like(gk_ref)
            gv_ref[...] = jnp.zeros_like(gv_ref)

        hm = _head_masks()
        comb_up = _tri_ones(True)
        comb_lo = _tri_ones(False)
        qs = q_ref[...] * scale
        go = go_ref[...]
        qh = [jnp.where(m, qs, 0.0).astype(BF16) for m in hm]
        goh = [jnp.where(m, go, 0.0).astype(BF16) for m in hm]
        acc_s[...] = jnp.zeros_like(acc_s)
        gq_ref[...] = jnp.zeros_like(gq_ref)

        def make_down(group):
            def fn(jtop):
                chains = []
                for g in range(group):
                    j = jtop - g
                    r0, vis = _attn_block(i, j, lp, pad)
                    kj = k_ref[pl.ds(r0, ATTN_KEYS), :].astype(BF16)
                    vj = v_ref[pl.ds(r0, ATTN_KEYS), :].astype(BF16)
                    for h in range(2):
                        z = lax.dot_general(qh[h], kj, NT, preferred_element_type=F32)
                        gw = lax.dot_general(goh[h], vj, NT, preferred_element_type=F32)
                        chains.append((h, j, r0, vis, z, gw))
                staged = []
                for h, j, r0, vis, z, gw in chains:
                    lsz = _log_sigmoid(z)
                    staged.append((h, j, r0, vis, lsz, _running_sums(jnp.where(vis, lsz - z, 0.0), comb_up), gw))
                for h, j, r0, vis, lsz, cr, gw in staged:
                    later, rs = _split_sums(cr)
                    acc = acc_s[h]
                    wgt = jnp.where(vis, jnp.exp(lsz + later + acc), 0.0)
                    acc_s[h] = acc + rs
                    ga_s[h, j] = gw * wgt
                    sz_s[h, j] = jnp.exp(lsz)
                    gv_ref[pl.ds(r0, ATTN_KEYS), :] += lax.dot_general(wgt.astype(BF16), goh[h], TN, preferred_element_type=F32)
            return fn

        n_blocks = lax.shift_right_logical(i + ATTN_KEYS // LANES, (ATTN_KEYS // LANES).bit_length() - 1)
        _run_groups(n_blocks, n_blocks - 1, -1, make_down)
        acc_s[...] = jnp.zeros_like(acc_s)

        def make_up(group):
            def fn(jbot):
                pend = []
                for g in range(group):
                    j = jbot + g
                    r0, vis = _attn_block(i, j, lp, pad)
                    kj = k_ref[pl.ds(r0, ATTN_KEYS), :]
                    for h in range(2):
                        ga = ga_s[h, j]
                        pend.append((h, j, r0, vis, ga, _running_sums(ga, comb_lo), jnp.where(hm[h], kj, 0.0).astype(BF16)))
                for h, j, r0, vis, ga, cr, kh in pend:
                    before, rs = _split_sums(cr)
                    pre = acc_s[h]
                    glk = before + pre
                    acc_s[h] = pre + rs
                    sz = sz_s[h, j]
                    gz = jnp.where(vis, ga * (1.0 - sz) - glk * sz, 0.0).astype(BF16)
                    gq_ref[...] += lax.dot_general(gz, kh, NN, preferred_element_type=F32)
                    gk_ref[pl.ds(r0, ATTN_KEYS), :] += lax.dot_general(gz, qh[h], TN, preferred_element_type=F32)
            return fn

        _run_groups(n_blocks, 0, 1, make_up)
        gq_ref[...] = gq_ref[...] * scale

    blk = lambda cb: pl.BlockSpec((LANES, LANES), lambda b, h, i: (b * nq + i, cb + h))
    full = lambda cb: pl.BlockSpec((lp, LANES), lambda b, h, i: (b, cb + h))
    shp = jax.ShapeDtypeStruct((rows, n_pairs * LANES), F32)
    per_block = pltpu.VMEM((2, -(-lp // ATTN_KEYS), LANES, ATTN_KEYS), F32)
    return pl.pallas_call(
        body, name=name, grid=(n_batch, n_pairs, nq),
        in_specs=[blk(q_cb), full(k_cb), full(v_cb), blk(go_cb)],
        out_specs=[blk(0), full(0), full(0)], out_shape=[shp, shp, shp],
        scratch_shapes=[per_block, per_block, pltpu.VMEM((2, LANES, ATTN_KEYS), F32)],
        compiler_params=_params(("parallel", "parallel", "arbitrary")),
    )(proj, proj, proj, g_out)


def _lb_fn(gamma):
    g0, g1 = gamma[0:1, :], gamma[1:2, :]
    mx = jnp.maximum(g0, g1)
    e0, e1 = jnp.exp(g0 - mx), jnp.exp(g1 - mx)
    p0, p1 = e0 / (e0 + e1), e1 / (e0 + e1)
    return (p0 + p1) - p0


def _lower_bound(name, gamma):
    def body(g_ref, o_ref):
        o_ref[...] = _lb_fn(g_ref[...])

    return pl.pallas_call(body, name=name, out_shape=jax.ShapeDtypeStruct((1, gamma.shape[1]), F32))(gamma)


def _lower_bound_bwd(name, gamma, g_lb_parts, g_ng_parts):
    def body(g_ref, glb_ref, gng_ref, o_ref, o2_ref):
        _, vjp = jax.vjp(_lb_fn, g_ref[...])
        o_ref[...] = vjp(jnp.sum(glb_ref[...], axis=0))[0]
        o2_ref[...] = jnp.sum(gng_ref[...], axis=0)

    return pl.pallas_call(
        body, name=name,
        out_shape=[jax.ShapeDtypeStruct(gamma.shape, F32), jax.ShapeDtypeStruct((1, gamma.shape[1]), F32)],
    )(gamma, g_lb_parts, g_ng_parts)


def _tri_times(tril, x, dims):
    hi = x.astype(BF16)
    lo = (x - hi.astype(F32)).astype(BF16)
    t = tril.astype(BF16)
    return (lax.dot_general(t, hi, dims, preferred_element_type=F32)
            + lax.dot_general(t, lo, dims, preferred_element_type=F32))


@jax.custom_vjp
def _cumsum_rows(x, tril):
    return _tri_times(tril, x, NN)


def _cumsum_rows_fwd(x, tril):
    return _tri_times(tril, x, NN), tril


def _cumsum_rows_bwd(tril, g):
    return _tri_times(tril, g, TN), jnp.zeros_like(tril)


_cumsum_rows.defvjp(_cumsum_rows_fwd, _cumsum_rows_bwd)


def _hg_decays(f_pre, lbs, masks, tril):
    f = [[lb + (1.0 - lb) * jax.nn.sigmoid(fc) for fc, lb in zip(row, lbs)] for row in f_pre]
    bcum = [[_cumsum_rows(jnp.log(x) * m, tril) for x in row] for row, m in zip(f, masks)]
    return [[1.0 - x for x in row] for row in f], bcum


def _hg_states_step(f_pre, i_in, lbs, sts, masks, tril):
    k, bcum = _hg_decays(f_pre, lbs, masks, tril)
    add = [[_dot(ic * m, kk * jnp.exp(b[HG_CHUNK - 1:, :] - b), TN) for ic, kk, b in zip(ir, kr, br)]
           for ir, kr, br, m in zip(i_in, k, bcum, masks)]
    for br, ar in zip(bcum, add):
        sts = [jnp.exp(b[HG_CHUNK - 1:, :]) * st + a for b, a, st in zip(br, ar, sts)]
    return sts


def _hg_step(q, f_pre, i_in, g, lbs, ngs, sts, masks, tril):
    k, bcum = _hg_decays(f_pre, lbs, masks, tril)
    v = [[ic * m for ic in row] for row, m in zip(i_in, masks)]
    qd = [[qc * jnp.exp(b) for qc, b in zip(qr, br)] for qr, br in zip(q, bcum)]
    scores = [[jnp.where(tril > 0.5, _dot(a, kk * jnp.exp(-b), NT), 0.0) for a, kk, b in zip(ar, kr, br)]
              for ar, kr, br in zip(qd, k, bcum)]
    inner = [[_dot(s, x, NN) for s, x in zip(sr, vr)] for sr, vr in zip(scores, v)]
    add = [[_dot(x, kk * jnp.exp(b[HG_CHUNK - 1:, :] - b), TN) for x, kk, b in zip(vr, kr, br)]
           for vr, kr, br in zip(v, k, bcum)]
    outs = []
    for qr, br, nr, ar, gr in zip(qd, bcum, inner, add, g):
        o = [n + _dot(a, st, NT) for n, a, st in zip(nr, qr, sts)]
        sts = [jnp.exp(b[HG_CHUNK - 1:, :]) * st + a for b, a, st in zip(br, ar, sts)]
        o = [x * lax.rsqrt(jnp.mean(x * x, axis=-1, keepdims=True) + RMS_EPS) * ng for x, ng in zip(o, ngs)]
        outs.append([x * (gc * jax.nn.sigmoid(gc)) for x, gc in zip(o, gr)])
    return outs, sts


def _hg_consts(c, pad):
    r = lax.broadcasted_iota(jnp.int32, (HG_CHUNK, HG_CHUNK), 0)
    cc = lax.broadcasted_iota(jnp.int32, (HG_CHUNK, HG_CHUNK), 1)
    tril = jnp.where(r >= cc, 1.0, 0.0).astype(F32)
    pos = c * HG_CHUNK + lax.broadcasted_iota(jnp.int32, (HG_CHUNK, 1), 0)
    return tril, jnp.where(pos >= pad, 1.0, 0.0).astype(F32)


HG_HEADS_PER_STEP = 4
HG_CHUNKS_PER_STEP = 2


def _hg_layout(lp, n_heads):
    step_rows = HG_CHUNKS_PER_STEP * HG_CHUNK
    per = min(HG_HEADS_PER_STEP, n_heads)
    assert lp % step_rows == 0 and n_heads % per == 0
    heads = [(h, slice(h * HG_DK, (h + 1) * HG_DK)) for h in range(per)]
    return n_heads // per, lp // step_rows, step_rows, per * HG_DK, heads


def _hg_step_views(step, pad, heads):
    slices = [slice(u * HG_CHUNK, (u + 1) * HG_CHUNK) for u in range(HG_CHUNKS_PER_STEP)]
    consts = [_hg_consts(step * HG_CHUNKS_PER_STEP + u, pad) for u in range(HG_CHUNKS_PER_STEP)]
    load = lambda ref: [[ref[sl, cols] for _, cols in heads] for sl in slices]
    return slices, [m for _, m in consts], consts[0][0], load


def _hgrn_fwd(name, proj, lb, ng, n_batch, lp, pad, n_heads):
    rows = proj.shape[0]
    groups, steps, step_rows, wide, heads = _hg_layout(lp, n_heads)

    def body(q_ref, f_ref, i_ref, g_ref, lb_ref, ng_ref, o_ref, st_s):
        t = pl.program_id(2)

        @pl.when(t == 0)
        def _():
            st_s[...] = jnp.zeros_like(st_s)

        slices, masks, tril, load = _hg_step_views(t, pad, heads)
        outs, sts = _hg_step(load(q_ref), load(f_ref), load(i_ref), load(g_ref),
                             [lb_ref[:, cols] for _, cols in heads], [ng_ref[:, cols] for _, cols in heads],
                             [st_s[h] for h, _ in heads], masks, tril)
        for sl, row in zip(slices, outs):
            for (_, cols), o in zip(heads, row):
                o_ref[sl, cols] = o
        for (h, _), st in zip(heads, sts):
            st_s[h] = st

    col = lambda off: pl.BlockSpec((step_rows, wide), lambda b, h, t: (b * steps + t, off * groups + h))
    vec = pl.BlockSpec((1, wide), lambda b, h, t: (0, h))
    return pl.pallas_call(
        body, name=name, grid=(n_batch, groups, steps), in_specs=[col(0), col(1), col(2), col(3), vec, vec],
        out_specs=col(0), out_shape=jax.ShapeDtypeStruct((rows, n_heads * HG_DK), F32),
        scratch_shapes=[pltpu.VMEM((len(heads), HG_DK, HG_DK), F32)],
        compiler_params=_params(("parallel", "parallel", "arbitrary")),
    )(proj, proj, proj, proj, lb, ng)


def _hgrn_states(name, proj, lb, n_batch, lp, pad, n_heads):
    groups, steps, step_rows, wide, heads = _hg_layout(lp, n_heads)

    def body(f_ref, i_ref, lb_ref, s_ref, st_s):
        t = pl.program_id(2)

        @pl.when(t == 0)
        def _():
            st_s[...] = jnp.zeros_like(st_s)

        _, masks, tril, load = _hg_step_views(t, pad, heads)
        sts = [st_s[h] for h, _ in heads]
        for (_, cols), st in zip(heads, sts):
            s_ref[:, cols] = st
        sts = _hg_states_step(load(f_ref), load(i_ref), [lb_ref[:, cols] for _, cols in heads], sts, masks, tril)
        for (h, _), st in zip(heads, sts):
            st_s[h] = st

    col = lambda off: pl.BlockSpec((step_rows, wide), lambda b, h, t: (b * steps + t, off * groups + h))
    return pl.pallas_call(
        body, name=name, grid=(n_batch, groups, steps),
        in_specs=[col(1), col(2), pl.BlockSpec((1, wide), lambda b, h, t: (0, h))],
        out_specs=pl.BlockSpec((HG_DK, wide), lambda b, h, t: (b * steps + t, h)),
        out_shape=jax.ShapeDtypeStruct((n_batch * steps * HG_DK, n_heads * HG_DK), F32),
        scratch_shapes=[pltpu.VMEM((len(heads), HG_DK, HG_DK), F32)],
        compiler_params=_params(("parallel", "parallel", "arbitrary")),
    )(proj, proj, lb)


def _hgrn_bwd(name, proj, lb, ng, g_out, n_batch, lp, pad, n_heads):
    rows = proj.shape[0]
    width = n_heads * HG_DK
    groups, steps, step_rows, wide, heads = _hg_layout(lp, n_heads)
    states = _hgrn_states(name + "_states", proj, lb, n_batch, lp, pad, n_heads)

    def body(q_ref, f_ref, i_ref, g_ref, lb_ref, ng_ref, go_ref, s_ref, gq_ref, gf_ref, gi_ref, gg_ref, glb_ref, gng_ref, gst_s):
        t = pl.program_id(2)

        @pl.when(t == 0)
        def _():
            gst_s[...] = jnp.zeros_like(gst_s)
            glb_ref[...] = jnp.zeros_like(glb_ref)
            gng_ref[...] = jnp.zeros_like(gng_ref)

        slices, masks, tril, load = _hg_step_views(steps - 1 - t, pad, heads)
        fn = functools.partial(_hg_step, masks=masks, tril=tril)
        _, vjp = jax.vjp(fn, load(q_ref), load(f_ref), load(i_ref), load(g_ref),
                         [lb_ref[:, cols] for _, cols in heads], [ng_ref[:, cols] for _, cols in heads],
                         [s_ref[:, cols] for _, cols in heads])
        gq, gf, gi, gg, glb, gng, gst = vjp((load(go_ref), [gst_s[h] for h, _ in heads]))
        for ref, grads in ((gq_ref, gq), (gf_ref, gf), (gi_ref, gi), (gg_ref, gg)):
            for sl, row in zip(slices, grads):
                for (_, cols), x in zip(heads, row):
                    ref[sl, cols] = x.astype(BF16)
        for (h, cols), a, b, c in zip(heads, gst, glb, gng):
            gst_s[h] = a
            glb_ref[:, cols] += b
            gng_ref[:, cols] += c

    col = lambda off: pl.BlockSpec((step_rows, wide), lambda b, h, t: (b * steps + steps - 1 - t, off * groups + h))
    vec = pl.BlockSpec((1, wide), lambda b, h, t: (0, h))
    part = pl.BlockSpec((None, 1, wide), lambda b, h, t: (b, 0, h))
    big = jax.ShapeDtypeStruct((rows, width), BF16)
    small = jax.ShapeDtypeStruct((n_batch, 1, width), F32)
    return pl.pallas_call(
        body, name=name, grid=(n_batch, groups, steps),
        in_specs=[col(0), col(1), col(2), col(3), vec, vec, col(0),
                  pl.BlockSpec((HG_DK, wide), lambda b, h, t: (b * steps + steps - 1 - t, h))],
        out_specs=[col(0), col(0), col(0), col(0), part, part],
        out_shape=[big, big, big, big, small, small],
        scratch_shapes=[pltpu.VMEM((len(heads), HG_DK, HG_DK), F32)],
        compiler_params=_params(("parallel", "parallel", "arbitrary")),
    )(proj, proj, proj, proj, lb, ng, g_out, states)


def _exchange_copies(src, dst, send, recv, loc, scatter):
    x, y, c = lax.axis_index("x"), lax.axis_index("y"), lax.axis_index("c")
    me = 4 * x + 2 * y + c
    local, remote = [], []
    for w in range(len(src)):
        local.append(pltpu.make_async_copy(src[w].at[me] if scatter else src[w], dst[w].at[me], loc.at[w]))
    for k in range(1, N_DEV):
        px = 1 - x if k & 4 else x
        py = 1 - y if k & 2 else y
        pc = 1 - c if k & 1 else c
        peer = 4 * px + 2 * py + pc
        for w in range(len(src)):
            remote.append(pltpu.make_async_remote_copy(
                src_ref=src[w].at[peer] if scatter else src[w], dst_ref=dst[w].at[me],
                send_sem=send.at[w * (N_DEV - 1) + k - 1], recv_sem=recv.at[w * (N_DEV - 1) + k - 1],
                device_id=(px, py, pc), device_id_type=pl.DeviceIdType.MESH))
    return local, remote


_HBM_SPEC = pl.BlockSpec(memory_space=pltpu.HBM)
_SEM_SPEC = pl.BlockSpec(memory_space=pltpu.SEMAPHORE)
_ANY_SPEC = pl.BlockSpec(memory_space=pl.ANY)
_DATAFLOW = pltpu.SideEffectType.DATAFLOW_SIDE_EFFECTING


def _exchange_start(name, srcs, scatter, dep=None):
    nw = len(srcs)
    srcs = [pltpu.with_memory_space_constraint(s, pltpu.HBM) for s in srcs]
    lands = [pltpu.with_memory_space_constraint(lax.empty(s.shape if scatter else (N_DEV,) + s.shape, s.dtype), pltpu.HBM)
             for s in srcs]
    deps = [] if dep is None else [dep]

    def body(*refs):
        src, dst = refs[:nw], refs[nw:2 * nw]
        send, recv, loc = refs[2 * nw + len(deps):2 * nw + len(deps) + 3]
        token = refs[-1]
        local, remote = _exchange_copies(src, dst, send, recv, loc, scatter)
        for cp in local + remote:
            cp.start()
        token[...] = jnp.zeros_like(token)

    sems = [pltpu.SemaphoreType.DMA((nw * (N_DEV - 1),)), pltpu.SemaphoreType.DMA((nw * (N_DEV - 1),)),
            pltpu.SemaphoreType.DMA((nw,))]
    out = pl.pallas_call(
        body, name=name,
        out_shape=(*sems, *[pltpu.HBM(s.shape, s.dtype) for s in srcs], *[pltpu.HBM(s.shape, s.dtype) for s in lands],
                   jax.ShapeDtypeStruct((SUBLANES, LANES), F32)),
        in_specs=[_HBM_SPEC] * (2 * nw) + [_ANY_SPEC] * len(deps),
        out_specs=(_SEM_SPEC, _SEM_SPEC, _SEM_SPEC, *[_HBM_SPEC] * (2 * nw), pl.BlockSpec(memory_space=pltpu.VMEM)),
        input_output_aliases={i: 3 + i for i in range(2 * nw)},
        compiler_params=pltpu.CompilerParams(has_side_effects=_DATAFLOW),
    )(*srcs, *lands, *deps)
    return {"sems": out[:3], "srcs": out[3:3 + nw], "lands": out[3 + nw:3 + 2 * nw], "token": out[-1], "scatter": scatter}


def _exchange_wait(name, handle, after):
    nw = len(handle["srcs"])
    scatter = handle["scatter"]

    def body(*refs):
        src, dst = refs[:nw], refs[nw:2 * nw]
        send, recv, loc = refs[2 * nw:2 * nw + 3]
        local, remote = _exchange_copies(src, dst, send, recv, loc, scatter)
        for cp in local:
            cp.wait()
        for cp in remote:
            cp.wait_send()
            cp.wait_recv()

    out = pl.pallas_call(
        body, name=name,
        out_shape=(*[pltpu.HBM(s.shape, s.dtype) for s in handle["srcs"]],
                   *[pltpu.HBM(s.shape, s.dtype) for s in handle["lands"]]),
        in_specs=[_HBM_SPEC] * (2 * nw) + [_SEM_SPEC] * 3 + [_ANY_SPEC],
        out_specs=tuple([_HBM_SPEC] * (2 * nw)),
        input_output_aliases={i: i for i in range(2 * nw)},
        compiler_params=pltpu.CompilerParams(has_side_effects=_DATAFLOW),
    )(*handle["srcs"], *handle["lands"], *handle["sems"], after)
    return list(out[nw:])


def _adamw(w, g, m, v):
    m = ADAM_B1 * m + (1.0 - ADAM_B1) * g
    v = ADAM_B2 * v + (1.0 - ADAM_B2) * (g * g)
    m_hat = m / (1.0 - ADAM_B1 ** ADAM_STEP)
    v_hat = v / (1.0 - ADAM_B2 ** ADAM_STEP)
    delta = -ADAM_LR * (m_hat / (jnp.sqrt(v_hat) + ADAM_EPS) + ADAM_WD * w)
    return delta, m, v


def _adamw_summed(name, parts, w, m, v):
    rows, cols = w.shape
    n_parts = parts.shape[0]
    tr = _tile(rows, max(SUBLANES, (1 << 18) // cols))

    def body(p_ref, w_ref, m_ref, v_ref, g_ref, d_ref, nm_ref, nv_ref):
        g = p_ref[0].astype(F32)
        for s in range(1, n_parts):
            g = g + p_ref[s].astype(F32)
        d, nm, nv = _adamw(w_ref[...], g, m_ref[...], v_ref[...])
        g_ref[...] = g
        d_ref[...] = d
        nm_ref[...] = nm
        nv_ref[...] = nv

    spec = pl.BlockSpec((tr, cols), lambda i: (i, 0))
    shp = jax.ShapeDtypeStruct((rows, cols), F32)
    return pl.pallas_call(
        body, name=name, grid=(rows // tr,),
        in_specs=[pl.BlockSpec((n_parts, tr, cols), lambda i: (0, i, 0)), spec, spec, spec],
        out_specs=[spec] * 4, out_shape=[shp] * 4, compiler_params=_params(("parallel",)),
    )(parts, w, m, v)


def _pack_rows(arrays, cols):
    out = []
    for a in arrays:
        flat = a.reshape(-1)
        n = -(-flat.shape[0] // cols) * cols
        out.append(jnp.pad(flat, (0, n - flat.shape[0])).reshape(-1, cols))
    packed = jnp.concatenate(out, axis=0)
    return jnp.pad(packed, ((0, -packed.shape[0] % SUBLANES), (0, 0)))


def _unpack_rows(packed, shapes, cols):
    out, r = [], 0
    for s in shapes:
        n = math.prod(s)
        nr = -(-n // cols)
        out.append(packed[r:r + nr].reshape(-1)[:n].reshape(s))
        r += nr
    return out


def _block_diag(blocks):
    g, a, b = blocks.shape
    eye = jnp.eye(g, dtype=blocks.dtype)
    return (eye[:, None, :, None] * blocks[:, :, None, :]).reshape(g * a, g * b)


def _diag_blocks(dense, g):
    a, b = dense.shape[0] // g, dense.shape[1] // g
    return jnp.einsum("gagb->gab", dense.reshape(g, a, g, b))


def _local_step(x, target, meta, wts, small, late_weights, on_grads, on_small):
    n_batch, seq, d = x.shape
    n_meta = meta.shape[0]
    pad = -(seq + n_meta) % LANES
    lead = pad + n_meta
    lp = lead + seq
    rows = n_batch * lp
    s5w = wts["glu"].shape[0]
    n_ab = wts["in_ab"].shape[2]
    ab_cols = wts["in_ab"].shape[0] * n_ab
    sbw = (ab_cols - s5w) // 3
    dff = small["mlp_b_up"].shape[1]
    n_pairs = sbw // LANES
    n_hg = d // HG_DK
    s5_cb = s5w // LANES
    sb_cb = sbw // LANES
    tm = _tile(rows, ROW_TILE)
    groups, n_state, grp = small["s5_b_re"].shape[1:]
    ns = groups * n_state
    sw = min(SCAN_LANES, ns)

    h0 = jnp.concatenate(
        [jnp.zeros((n_batch, pad, d), F32), jnp.broadcast_to(meta[None], (n_batch, n_meta, d)), x], axis=1
    ).reshape(rows, d)

    lam_re, lam_im = small["s5_lam_re"][0], small["s5_lam_im"][0]
    log_dt = small["s5_log_dt"][0][:, None]
    b_re_t = small["s5_b_re"][0].transpose(0, 2, 1)
    b_im_t = small["s5_b_im"][0].transpose(0, 2, 1)
    c_re, c_im = small["s5_c_re"][0], small["s5_c_im"][0]
    lbr, lbi, bbr, bbi = _s5_params("s5_params", lam_re, lam_im, log_dt, b_re_t, b_im_t)
    b_blk = _interleave(_block_diag(bbr), _block_diag(bbi), sw).astype(BF16)
    c_blk = _interleave(_block_diag(c_re), _block_diag(-c_im), sw).T.astype(BF16)
    lam_row = _interleave(lbr.reshape(1, ns), lbi.reshape(1, ns), sw)
    d_row = small["s5_d"].reshape(1, s5w)

    def ln_store(outs, acc, res, bias, g, b):
        r = ALPHA * res + acc + bias
        outs[0][...] = r
        if len(outs) > 1:
            outs[1][...] = _ln(r, g, b)

    zero_bias = jnp.zeros((1, d), F32)

    def mix_ln(name, a, w, k_total, tk, res, bias, g, b, a_fn=None, emit_h=True):
        n_out = 2 if emit_h else 1
        return _mm_act(name, a, w, "nat", n_out_cols=d, k_total=k_total, tn=d, tk=tk, a_fn=a_fn,
                       extras=(res, bias, g, b), extra_specs=(_row_spec(tm, d), _vec_spec(d), _vec_spec(d), _vec_spec(d)),
                       store=ln_store, out_shape=[jax.ShapeDtypeStruct((rows, d), F32)] * n_out,
                       out_specs=[_row_spec(tm, d)] * n_out)

    def two(width):
        return [jax.ShapeDtypeStruct((rows, width), F32)] * 2, [_row_spec(tm, width)] * 2

    proj_ab = _mm_act("in_ab", h0, wts["in_ab"], "stk", n_out_cols=ab_cols, k_total=d, tn=n_ab, tk=d)[0]
    bu = _mm_act("s5_bu", proj_ab, b_blk, "nat", n_out_cols=2 * ns, k_total=s5w, tn=min(2 * ns, 2048), tk=s5w)[0]
    states = _s5_scan("s5_scan", bu, lam_row, n_batch, lp, sw)

    def gelu_store(outs, acc, u, dv):
        ypre = acc + dv * u
        outs[0][...] = ypre
        outs[1][...] = jax.nn.gelu(ypre)

    shp2, spec2 = two(s5w)
    ypre, y = _mm_act(
        "s5_y", states, c_blk, "nat", n_out_cols=s5w, k_total=2 * ns, tn=s5w, tk=min(2 * ns, 1024),
        extras=(proj_ab, d_row), extra_specs=(_row_spec(tm, s5w), _vec_spec(s5w)), store=gelu_store,
        out_shape=shp2, out_specs=spec2)

    def glu_store(outs, acc, yv, bias):
        gate = acc + bias
        outs[0][...] = gate
        outs[1][...] = _glu(yv, gate)

    gate, a_out = _mm_act(
        "s5_glu", y, wts["glu"], "nat", n_out_cols=s5w, k_total=s5w, tn=s5w, tk=s5w,
        extras=(y, small["s5_b_glu"]), extra_specs=(_row_spec(tm, s5w), _vec_spec(s5w)), store=glu_store,
        out_shape=shp2, out_specs=spec2)
    b_out = _attn_fwd("sb_attn", proj_ab, n_batch, lp, pad, s5_cb, s5_cb + sb_cb, s5_cb + 2 * sb_cb, n_pairs)

    def bias_store(outs, acc, bias):
        outs[0][...] = (acc + bias).astype(outs[0].dtype)

    def wide(width, dtype):
        return [jax.ShapeDtypeStruct((rows, dff), dtype)], [_row_spec(tm, width)]

    def mlp_fwd(layer, h_in, emit_h=True):
        shp, spec = wide(n_up, BF16)
        up = _mm_act(f"up{layer}", h_in, wts["up"][layer], "stk", n_out_cols=dff, k_total=d, tn=n_up, tk=d,
                     extras=(small["mlp_b_up"][layer:layer + 1],), extra_specs=(_vec_spec(n_up),), store=bias_store,
                     out_shape=shp, out_specs=spec)[0]
        return (up, *mix_ln(f"down{layer}", up, wts["down"][layer], dff, min(dff, 1024), h_in,
                            small["mlp_b_down"][layer:layer + 1], small["ln_mlp_g"][layer:layer + 1],
                            small["ln_mlp_b"][layer:layer + 1], a_fn=_relu2, emit_h=emit_h))

    r1, h1 = mix_ln("out_ab", [a_out, b_out], wts["out_ab"], s5w + sbw, min(s5w, sbw), h0, zero_bias,
                    small["ln_mix_g"][0:1], small["ln_mix_b"][0:1])
    wts = {**wts, **late_weights(r1)}
    n_c = wts["in_c"].shape[2]
    n_up = wts["up"][0].shape[2]
    up0, r2, h2 = mlp_fwd(0, h1)

    lb = _lower_bound("hg_lb", small["hgrn_gamma"])
    proj_c = _mm_act("in_c", h2, wts["in_c"], "stk", n_out_cols=4 * d, k_total=d, tn=n_c, tk=d)[0]
    c_out = _hgrn_fwd("hgrn", proj_c, lb, wts["ng"], n_batch, lp, pad, n_hg)
    r3, h3 = mix_ln("out_c", c_out, wts["out_c"], d, d, h2, zero_bias, small["ln_mix_g"][1:2], small["ln_mix_b"][1:2])
    up1, r4 = mlp_fwd(1, h3, emit_h=False)

    gr = {}
    g_r4, gr["ln_mlp_g1"], gr["ln_mlp_b1"], loss_tile = _loss_grad(
        "loss", r4, small["ln_mlp_g"][1:2], small["ln_mlp_b"][1:2], target, n_batch, lp, lead)

    def res_store(outs, acc, g_res):
        outs[0][...] = acc + ALPHA * g_res

    def ln_bwd_store(outs, acc, g_res, r_in, g, b, first_step):
        gr_in, gg, gb = jax.vjp(_ln, r_in, g, b)[1](acc + ALPHA * g_res)
        outs[0][...] = gr_in

        @pl.when(first_step)
        def _():
            outs[1][...] = jnp.zeros_like(outs[1])
            outs[2][...] = jnp.zeros_like(outs[2])

        outs[1][...] += gg
        outs[2][...] += gb

    def through_ln(name, a, w, k_total, tk, g_res, r_in, g, b, dep=None):
        vec = pl.BlockSpec((1, d), lambda i, j, k: (0, 0))
        return _mm_act(name, a, w, "stkT", n_out_cols=d, k_total=k_total, tn=d, tk=tk,
                       extras=(g_res, r_in, g, b), extra_specs=(_row_spec(tm, d), _row_spec(tm, d), vec, vec),
                       store=ln_bwd_store, sequential=True, dep=dep,
                       out_shape=[jax.ShapeDtypeStruct((rows, d), F32)] + [jax.ShapeDtypeStruct((1, d), F32)] * 2,
                       out_specs=[_row_spec(tm, d), vec, vec])

    def mlp_bwd(layer, g_r, up, h_in, r_in, send=None):
        def gup_store(outs, acc, upv):
            outs[0][...] = (acc * (2.0 * jnp.maximum(upv.astype(F32), 0.0))).astype(outs[0].dtype)

        tf = min(dff, 1024)
        shp, spec = wide(tf, BF16)
        g_up = _mm_act(f"g_up{layer}", g_r, wts["down"][layer], "natT", n_out_cols=dff, k_total=d, tn=tf, tk=d,
                       extras=(up,), extra_specs=(_row_spec(tm, tf),), store=gup_store, out_shape=shp, out_specs=spec)[0]
        gr[f"down{layer}"], gr[f"mlp_b_down{layer}"] = _mm_wgrad(
            f"dw_down{layer}", up, g_r, kw=dff, n=d, tmw=tf, tn=d, a_fn=_relu2, out_dtype=BF16, colsum=True)
        gr[f"up{layer}"], gr[f"mlp_b_up{layer}"] = _mm_wgrad(
            f"dw_up{layer}", h_in, g_up, kw=d, n=dff, tmw=d, tn=min(dff, 2048), shard_cols=n_up, out_dtype=BF16, colsum=True)
        dep = send() if send is not None else None
        g_r_in, gr[f"ln_mix_g{layer}"], gr[f"ln_mix_b{layer}"] = through_ln(
            f"g_hmid{layer}", g_up, wts["up"][layer], dff, n_up, g_r, r_in,
            small["ln_mix_g"][layer:layer + 1], small["ln_mix_b"][layer:layer + 1], dep=dep)
        return g_r_in

    g_r3 = mlp_bwd(1, g_r4, up1, h3, r3)
    g_cout = _mm_act("g_cout", g_r3, wts["out_c"], "natT", n_out_cols=d, k_total=d, tn=d, tk=d)[0]
    gr["out_c"] = _mm_wgrad("dw_out_c", c_out, g_r3, kw=d, n=d, tmw=d, tn=d, out_dtype=BF16)
    gq, gf, gi, gg_, g_lb_parts, g_ng_parts = _hgrn_bwd("hgrn_bwd", proj_c, lb, wts["ng"], g_cout, n_batch, lp, pad, n_hg)
    g_pc = [gq, gf, gi, gg_]
    gr["hgrn_gamma"], gr["ng"] = _lower_bound_bwd("hg_lb_bwd", small["hgrn_gamma"], g_lb_parts, g_ng_parts)
    gr["in_c"] = _mm_wgrad("dw_in_c", h2, g_pc, kw=d, n=4 * d, tmw=d, tn=d, shard_cols=n_c, out_dtype=BF16)
    sent1 = on_grads(1, {"down1": gr["down1"], "up1": gr["up1"], "out_c": gr["out_c"], "in_c": gr["in_c"], "ng": gr["ng"]})
    g_r2, gr["ln_mlp_g0"], gr["ln_mlp_b0"] = through_ln(
        "g_h2", g_pc, wts["in_c"], 4 * d, n_c, g_r3, r2, small["ln_mlp_g"][0:1], small["ln_mlp_b"][0:1], dep=sent1)

    g_r1 = mlp_bwd(0, g_r2, up0, h1, r1, send=lambda: on_grads(2, {"down0": gr["down0"], "up0": gr["up0"]}))
    g_cat = _mm_act("g_cat", g_r1, wts["out_ab"], "natT", n_out_cols=d, k_total=d, tn=d, tk=d)[0]
    gr["out_ab"] = _mm_wgrad("dw_out_ab", [a_out, b_out], g_r1, kw=s5w + sbw, n=d, tmw=min(s5w, sbw), tn=d, out_dtype=BF16)
    g_q, g_k, g_v = _attn_bwd("sb_attn_bwd", proj_ab, g_cat, n_batch, lp, pad, s5_cb, s5_cb + sb_cb, s5_cb + 2 * sb_cb,
                              s5_cb, n_pairs)

    g_y_direct, g_gate = _rowwise("s5_glu_bwd", lambda ga, yv, gt: jax.vjp(_glu, yv, gt)[1](ga),
                                  [(g_cat, 0, s5w), (y, 0, s5w), (gate, 0, s5w)], 2, s5w)

    def gelu_bwd_store(outs, acc, gyd, yp, u, dv):
        gyp = jax.vjp(jax.nn.gelu, yp)[1](acc + gyd)[0]
        outs[0][...] = gyp
        outs[1][...] = dv * gyp
        outs[2][...] = jnp.sum(gyp * u, axis=0, keepdims=True)

    rs = _row_spec(tm, s5w)
    g_ypre, g_u_direct, gd_parts = _mm_act(
        "s5_g_y", g_gate, wts["glu"], "natT", n_out_cols=s5w, k_total=s5w, tn=s5w, tk=s5w,
        extras=(g_y_direct, ypre, proj_ab, d_row), extra_specs=(rs, rs, rs, _vec_spec(s5w)), store=gelu_bwd_store,
        out_shape=[jax.ShapeDtypeStruct((rows, s5w), F32)] * 2 + [jax.ShapeDtypeStruct((rows // tm, 1, s5w), F32)],
        out_specs=[rs, rs, pl.BlockSpec((None, 1, s5w), lambda i, j, k: (i, 0, j))])
    gr["glu"], gr["s5_b_glu"] = _mm_wgrad("dw_glu", y, g_gate, kw=s5w, n=s5w, tmw=s5w, tn=s5w, out_dtype=BF16, colsum=True)
    g_sd = _mm_act("s5_g_states", g_ypre, c_blk, "natT", n_out_cols=2 * ns, k_total=s5w, tn=min(2 * ns, 2048), tk=s5w)[0]
    d_cblk = _mm_wgrad("dw_cblk", states, g_ypre, kw=2 * ns, n=s5w, tmw=min(2 * ns, 1024), tn=s5w)
    gs, gl_parts = _s5_scan_bwd("s5_scan_bwd", g_sd, states, lam_row, n_batch, lp, sw)

    def add_store(outs, acc, other):
        outs[0][...] = acc + other

    g_u = _mm_act("s5_g_u", gs, b_blk, "natT", n_out_cols=s5w, k_total=2 * ns, tn=s5w, tk=min(2 * ns, 1024),
                  extras=(g_u_direct,), extra_specs=(rs,), store=add_store)[0]
    d_bblk = _mm_wgrad("dw_bblk", proj_ab, gs, kw=s5w, n=2 * ns, tmw=s5w, tn=min(2 * ns, 2048))
    db_re, db_im = _deinterleave(d_bblk, sw)
    dc_re, dc_im = _deinterleave(d_cblk.T, sw)
    glr, gli = _deinterleave(gl_parts, sw)
    g_lam_re, g_lam_im, g_log_dt, g_b_re_t, g_b_im_t, g_d = _s5_params_bwd(
        "s5_params_bwd", lam_re, lam_im, log_dt, b_re_t, b_im_t,
        glr.reshape(n_batch, groups, n_state), gli.reshape(n_batch, groups, n_state),
        _diag_blocks(db_re, groups), _diag_blocks(db_im, groups), gd_parts)

    cat2 = lambda key: jnp.concatenate([gr[key + "0"], gr[key + "1"]], axis=0)
    small_sent = on_small({
        "s5_lam_re": g_lam_re[None], "s5_lam_im": g_lam_im[None], "s5_log_dt": g_log_dt.reshape(1, groups),
        "s5_b_re": g_b_re_t.transpose(0, 2, 1)[None], "s5_b_im": g_b_im_t.transpose(0, 2, 1)[None],
        "s5_c_re": _diag_blocks(dc_re, groups)[None], "s5_c_im": -_diag_blocks(dc_im, groups)[None],
        "s5_d": g_d.reshape(1, groups, grp), "s5_b_glu": gr["s5_b_glu"], "hgrn_gamma": gr["hgrn_gamma"],
        "ln_mix_g": cat2("ln_mix_g"), "ln_mix_b": cat2("ln_mix_b"), "mlp_b_up": cat2("mlp_b_up"),
        "mlp_b_down": cat2("mlp_b_down"), "ln_mlp_g": cat2("ln_mlp_g"), "ln_mlp_b": cat2("ln_mlp_b"),
    }, loss_tile)

    g_pab = [g_u, g_q, g_k, g_v]
    assert s5w == sbw
    gr["in_ab"] = _mm_wgrad("dw_in_ab", h0, g_pab, kw=d, n=ab_cols, tmw=d, tn=s5w, shard_cols=n_ab, out_dtype=BF16,
                            dep=small_sent)
    g_h0 = _mm_act("g_h0", g_pab, wts["in_ab"], "stkT", n_out_cols=d, k_total=ab_cols, tn=d, tk=n_ab,
                   extras=(g_r1,), extra_specs=(_row_spec(tm, d),), store=res_store)[0]
    grad_x = g_h0.reshape(n_batch, lp, d)[:, lead:, :]
    g_meta = _meta_grad("g_meta", g_h0, n_batch, lp, pad, n_meta)
    on_grads(3, {"meta": g_meta, "in_ab": gr["in_ab"], "glu": gr["glu"], "out_ab": gr["out_ab"]})
    return grad_x


SMALL_NAMES = ("s5_lam_re", "s5_lam_im", "s5_log_dt", "s5_b_re", "s5_b_im", "s5_c_re", "s5_c_im", "s5_d", "s5_b_glu",
               "hgrn_gamma", "ln_mix_g", "ln_mix_b", "mlp_b_up", "mlp_b_down", "ln_mlp_g", "ln_mlp_b")
WEIGHT_ORDER = ("meta", "w_in_ab", "s5_lam_re", "s5_lam_im", "s5_log_dt", "s5_b_re", "s5_b_im", "s5_c_re", "s5_c_im",
                "s5_d", "s5_w_glu", "s5_b_glu", "w_out_ab", "w_in_c", "hgrn_gamma", "hgrn_norm_g", "w_out_c", "ln_mix_g",
                "ln_mix_b", "mlp_w_up", "mlp_b_up", "mlp_w_down", "mlp_b_down", "ln_mlp_g", "ln_mlp_b")


def kernel(x, meta, w_in_ab, s5_lam_re, s5_lam_im, s5_log_dt, s5_b_re, s5_b_im, s5_c_re, s5_c_im, s5_d, s5_w_glu, s5_b_glu, w_out_ab, w_in_c, hgrn_gamma, hgrn_norm_g, w_out_c, ln_mix_g, ln_mix_b, mlp_w_up, mlp_b_up, mlp_w_down, mlp_b_down, ln_mlp_g, ln_mlp_b, loss_target, m_meta, m_w_in_ab, m_s5_lam_re, m_s5_lam_im, m_s5_log_dt, m_s5_b_re, m_s5_b_im, m_s5_c_re, m_s5_c_im, m_s5_d, m_s5_w_glu, m_s5_b_glu, m_w_out_ab, m_w_in_c, m_hgrn_gamma, m_hgrn_norm_g, m_w_out_c, m_ln_mix_g, m_ln_mix_b, m_mlp_w_up, m_mlp_b_up, m_mlp_w_down, m_mlp_b_down, m_ln_mlp_g, m_ln_mlp_b, v_meta, v_w_in_ab, v_s5_lam_re, v_s5_lam_im, v_s5_log_dt, v_s5_b_re, v_s5_b_im, v_s5_c_re, v_s5_c_im, v_s5_d, v_s5_w_glu, v_s5_b_glu, v_w_out_ab, v_w_in_c, v_hgrn_gamma, v_hgrn_norm_g, v_w_out_c, v_ln_mix_g, v_ln_mix_b, v_mlp_w_up, v_mlp_b_up, v_mlp_w_down, v_mlp_b_down, v_ln_mlp_g, v_ln_mlp_b):
    args = dict(locals())
    w = {n: args[n] for n in WEIGHT_ORDER}
    mom = {n: args["m_" + n] for n in WEIGHT_ORDER}
    var = {n: args["v_" + n] for n in WEIGHT_ORDER}
    d = x.shape[2]
    n_meta = meta.shape[0]

    cast = lambda a: a.astype(BF16)
    early = _exchange_start("gather_early_start", [w["meta"], cast(w["w_in_ab"][0]), cast(w["s5_w_glu"][0]),
                                                   cast(w["w_out_ab"][0])], False)
    late = _exchange_start("gather_late_start", [w["hgrn_norm_g"], cast(w["w_in_c"][0]), cast(w["w_out_c"][0]),
                                                 cast(w["mlp_w_up"][0]), cast(w["mlp_w_up"][1]),
                                                 cast(w["mlp_w_down"][0]), cast(w["mlp_w_down"][1])], False, dep=early["token"])
    a_meta, a_in_ab, a_glu, a_out_ab = _exchange_wait("gather_early_wait", early, late["token"])
    wts = {"in_ab": a_in_ab, "glu": a_glu.reshape(-1, a_glu.shape[2]), "out_ab": a_out_ab.reshape(-1, d)}
    meta_full = a_meta.transpose(1, 0, 2).reshape(n_meta, d)
    small = {n: w[n] for n in SMALL_NAMES}

    def late_weights(after):
        a_ng, a_in_c, a_out_c, a_up0, a_up1, a_dn0, a_dn1 = _exchange_wait("gather_late_wait", late, after)
        return {"in_c": a_in_c, "ng": a_ng.transpose(1, 0, 2).reshape(1, d), "out_c": a_out_c.reshape(-1, d),
                "up": [a_up0, a_up1], "down": [a_dn0.reshape(-1, d), a_dn1.reshape(-1, d)]}

    n_loc = d // N_DEV
    rows_of = lambda g: g.reshape(N_DEV, -1, g.shape[-1])
    cols_of = lambda g: g.reshape(g.shape[0], N_DEV, n_loc).transpose(1, 0, 2)
    sent = {}

    def on_grads(stage, g):
        if stage == 1:
            order = (("mlp_w_down", 1), ("mlp_w_up", 1), ("w_out_c", 0), ("w_in_c", 0), ("hgrn_norm_g", None))
            parts = [rows_of(g["down1"]), g["up1"], rows_of(g["out_c"]), g["in_c"], cols_of(g["ng"])]
        elif stage == 2:
            order = (("mlp_w_down", 0), ("mlp_w_up", 0))
            parts = [rows_of(g["down0"]), g["up0"]]
        else:
            order = (("w_out_ab", 0), ("s5_w_glu", 0), ("w_in_ab", 0), ("meta", None))
            parts = [rows_of(g["out_ab"]), rows_of(g["glu"]), g["in_ab"], cols_of(g["meta"])]
        sent[stage] = (order, _exchange_start(f"scatter_start{stage}", parts, True))
        return sent[stage][1]["token"]

    def on_small(sg, loss_tile):
        g_pack = _pack_rows([sg[n] for n in SMALL_NAMES] + [loss_tile], PACK_COLS)
        sent["small"] = _exchange_start("gather_small_start", [g_pack], False)
        return sent["small"]["token"]

    grad_x = _local_step(x, loss_target, meta_full, wts, small, late_weights, on_grads, on_small)
    small_sent = sent["small"]
    tile = (SUBLANES, LANES)
    shapes = [w[n].shape for n in SMALL_NAMES] + [tile]
    zeros = jnp.zeros(tile, F32)
    w_pack = _pack_rows([w[n] for n in SMALL_NAMES] + [zeros], PACK_COLS)
    m_pack = _pack_rows([mom[n] for n in SMALL_NAMES] + [zeros], PACK_COLS)
    v_pack = _pack_rows([var[n] for n in SMALL_NAMES] + [zeros], PACK_COLS)

    def apply(stage, after):
        order, handle = sent[stage]
        recv = _exchange_wait(f"scatter_wait{stage}", handle, after)
        for (nm, ly), rc in zip(order, recv):
            sel = (lambda t: t) if ly is None else (lambda t, ly=ly: t[ly])
            res[(nm, ly)] = _adamw_summed(f"adamw_{nm}_{ly}", rc, sel(w[nm]), sel(mom[nm]), sel(var[nm]))
        return res[order[0]][0]

    res = {}
    done = apply(2, apply(1, sent[3][1]["token"]))
    g_all = _exchange_wait("gather_small_wait", small_sent, done)[0]
    packed = _adamw_summed("adamw_small", g_all, w_pack, m_pack, v_pack)
    apply(3, packed[0])
    unpacked = [_unpack_rows(p, shapes, PACK_COLS) for p in packed]
    loss = unpacked[0][-1][0, 0]

    def pick(nm, which):
        if nm in SMALL_NAMES:
            return unpacked[which][SMALL_NAMES.index(nm)]
        if (nm, None) in res:
            return res[(nm, None)][which]
        return jnp.stack([res[(nm, ly)][which] for ly in range(w[nm].shape[0])], axis=0)

    return (loss, grad_x, *[pick(n, 0) for n in WEIGHT_ORDER], *[pick(n, 1) for n in WEIGHT_ORDER],
            *[pick(n, 2) for n in WEIGHT_ORDER], *[pick(n, 3) for n in WEIGHT_ORDER])
```

```python
import functools
import math

import jax
import jax.numpy as jnp
from jax import lax
from jax.experimental import pallas as pl
from jax.experimental.pallas import tpu as pltpu

F32 = jnp.float32
BF16 = jnp.bfloat16

N_DEV = 8
DEPTH = 2
ALPHA = (2.0 * DEPTH) ** 0.25
LN_EPS = 1e-5
RMS_EPS = 1e-6
SB_HEAD_DIM = 64
HG_DK = 128
HG_CHUNK = 64
LANES = 128
SUBLANES = 8
VMEM_LIMIT_BYTES = 56 * 1024 * 1024
ROW_TILE = 1088
SCAN_LANES = 256
SCAN_UNROLL = 4
PACK_COLS = 1024

ADAM_LR = 0.001
ADAM_B1 = 0.9
ADAM_B2 = 0.999
ADAM_EPS = 1e-08
ADAM_WD = 0.01
ADAM_STEP = 10

NN = (((1,), (0,)), ((), ()))
NT = (((1,), (1,)), ((), ()))
TN = (((0,), (0,)), ((), ()))


def _tile(n, pref, align=SUBLANES):
    t = min(n, pref)
    t -= t % align
    while t >= align:
        if n % t == 0:
            return t
        t -= align
    return n


def _unrolled_loop(n, body, init, unroll):
    assert n % unroll == 0

    def outer(t, carry):
        for u in range(unroll):
            carry = body(t * unroll + u, carry)
        return carry

    return lax.fori_loop(0, n // unroll, outer, init)


def _params(sem):
    return pltpu.CompilerParams(dimension_semantics=sem, vmem_limit_bytes=VMEM_LIMIT_BYTES)


def _dot_raw(a, b, dims):
    return lax.dot_general(a.astype(BF16), b.astype(BF16), dims, preferred_element_type=F32)


def _make_dot(dims, da_rule, db_rule):
    @jax.custom_vjp
    def f(a, b):
        return _dot_raw(a, b, dims)

    def fwd(a, b):
        return _dot_raw(a, b, dims), (a, b)

    def bwd(res, g):
        a, b = res
        return da_rule(g, a, b), db_rule(g, a, b)

    f.defvjp(fwd, bwd)
    return f


_DOTS = {
    NN: _make_dot(NN, lambda g, a, b: _dot_raw(g, b, NT), lambda g, a, b: _dot_raw(a, g, TN)),
    NT: _make_dot(NT, lambda g, a, b: _dot_raw(g, b, NN), lambda g, a, b: _dot_raw(g, a, TN)),
    TN: _make_dot(TN, lambda g, a, b: _dot_raw(b, g, NT), lambda g, a, b: _dot_raw(a, g, NN)),
}


def _dot(a, b, dims):
    return _DOTS[dims](a, b)


def _running_sums(a, tri_ones, split=False):
    hi = a.astype(BF16)
    out = lax.dot_general(hi, tri_ones, NN, preferred_element_type=F32)
    if split:
        lo = (a - hi.astype(F32)).astype(BF16)
        out = out + lax.dot_general(lo, tri_ones, NN, preferred_element_type=F32)
    return out


def _piece_specs(pieces, block_rows, block_cols, row_of, col_of, cb0):
    per = pieces[0].shape[1] // block_cols if len(pieces) > 1 else None
    specs = []
    for p in range(len(pieces)):
        if per is None:
            specs.append(pl.BlockSpec((block_rows, block_cols), lambda *g: (row_of(*g), cb0 + col_of(*g))))
        else:
            specs.append(pl.BlockSpec(
                (block_rows, block_cols),
                lambda *g, p=p: (row_of(*g), jnp.clip(col_of(*g) - p * per, 0, per - 1))))
    return specs, per


def _mm_call(name, grid, dims, a_pieces, a_specs, a_sel, b_pieces, b_specs, b_sel, extras, extra_specs,
             out_shape, out_specs, acc_shape, a_fn, store, colsum_width=0, sequential=False, deps=()):
    na, nb, ne, no, nd = len(a_pieces), len(b_pieces), len(extras), len(out_shape), len(deps)
    nk = grid[2]

    def body(*refs):
        a_refs, b_refs = refs[:na], refs[na:na + nb]
        extra = refs[na + nb:na + nb + ne]
        outs = refs[na + nb + ne + nd:na + nb + ne + nd + no]
        acc = refs[na + nb + ne + nd + no]
        ids = (pl.program_id(0), pl.program_id(1), pl.program_id(2))
        k = ids[2]

        @pl.when(k == 0)
        def _():
            acc[...] = jnp.zeros_like(acc)

        def run(a_ref, b_ref):
            a = a_ref[...]
            if a_fn is not None:
                a = a_fn(a)
            b = b_ref[...]
            if b.ndim == 3 and dims == NN:
                n = b.shape[2]
                for q in range(b.shape[0]):
                    acc[:, q * n:(q + 1) * n] += _dot_raw(a, b[q], dims)
            elif b.ndim == 3:
                n = b.shape[2]
                for q in range(b.shape[0]):
                    acc[...] += _dot_raw(a[:, q * n:(q + 1) * n], b[q], dims)
            else:
                acc[...] += _dot_raw(a, b, dims)
            if colsum_width:
                cs = refs[-1]
                first = ids[1] == 0

                @pl.when(first & (k == 0))
                def _():
                    cs[...] = jnp.zeros_like(cs)

                @pl.when(first)
                def _():
                    cs[...] += jnp.sum(b.astype(F32), axis=0, keepdims=True)

        if na == 1 and nb == 1:
            run(a_refs[0], b_refs[0])
        elif nb == 1:
            per, fn = a_sel
            which = fn(*ids) // per
            for p in range(na):
                pl.when(which == p)(functools.partial(run, a_refs[p], b_refs[0]))
        else:
            assert na == 1
            per, fn = b_sel
            which = fn(*ids) // per
            for p in range(nb):
                pl.when(which == p)(functools.partial(run, a_refs[0], b_refs[p]))

        @pl.when(k == nk - 1)
        def _():
            if sequential:
                store(outs, acc[...], *[e[...] for e in extra], first_step=(ids[0] == 0) & (ids[1] == 0))
            else:
                store(outs, acc[...], *[e[...] for e in extra])
            if colsum_width:
                @pl.when(ids[1] == 0)
                def _():
                    outs[-1][...] = refs[-1][...]

    scratch = [pltpu.VMEM(acc_shape, F32)]
    if colsum_width:
        scratch.append(pltpu.VMEM((1, colsum_width), F32))
    sem = ("parallel", "arbitrary", "arbitrary") if colsum_width else ("parallel", "parallel", "arbitrary")
    if sequential:
        sem = ("arbitrary",) * 3
    return pl.pallas_call(
        body, name=name, grid=grid,
        in_specs=[*a_specs, *b_specs, *extra_specs, *[pl.BlockSpec(memory_space=pl.ANY)] * nd], out_specs=out_specs,
        out_shape=out_shape, scratch_shapes=scratch, compiler_params=_params(sem),
    )(*a_pieces, *b_pieces, *extras, *deps)


def _store_plain(outs, acc):
    outs[0][...] = acc.astype(outs[0].dtype)


def _row_spec(tm, tn):
    return pl.BlockSpec((tm, tn), lambda i, j, k: (i, j))


def _vec_spec(tn):
    return pl.BlockSpec((1, tn), lambda i, j, k: (0, j))


def _mm_act(name, a, w, wkind, *, n_out_cols, k_total, tn, tk, a_cb0=0, a_fn=None, extras=(), extra_specs=(),
            store=_store_plain, out_shape=None, out_specs=None, sequential=False, dep=None):
    a_pieces = list(a) if isinstance(a, (list, tuple)) else [a]
    rows = a_pieces[0].shape[0]
    tm = _tile(rows, ROW_TILE)
    grid = (rows // tm, n_out_cols // tn, k_total // tk)
    a_specs, per = _piece_specs(a_pieces, tm, tk, lambda i, j, k: i, lambda i, j, k: k, a_cb0)
    if wkind == "nat":
        b_spec, dims = pl.BlockSpec((tk, tn), lambda i, j, k: (k, j)), NN
    elif wkind == "stk":
        n = w.shape[2]
        assert tn % n == 0
        b_spec, dims = pl.BlockSpec((tn // n, tk, n), lambda i, j, k: (j, k, 0)), NN
    elif wkind == "natT":
        b_spec, dims = pl.BlockSpec((tn, tk), lambda i, j, k: (j, k)), NT
    else:
        n = w.shape[2]
        assert wkind == "stkT" and tk % n == 0
        b_spec, dims = pl.BlockSpec((tk // n, tn, n), lambda i, j, k: (k, j, 0)), NT
    if out_shape is None:
        out_shape = [jax.ShapeDtypeStruct((rows, n_out_cols), F32)]
        out_specs = [_row_spec(tm, tn)]
    return _mm_call(name, grid, dims, a_pieces, a_specs, (per, lambda i, j, k: k), [w], [b_spec], None,
                    list(extras), list(extra_specs), out_shape, out_specs, (tm, tn), a_fn, store,
                    sequential=sequential, deps=() if dep is None else (dep,))


def _mm_wgrad(name, a, g, *, kw, n, tmw, tn, a_cb0=0, a_fn=None, shard_cols=0, out_dtype=F32, colsum=False, dep=None):
    a_pieces = list(a) if isinstance(a, (list, tuple)) else [a]
    g_pieces = list(g) if isinstance(g, (list, tuple)) else [g]
    rows = a_pieces[0].shape[0]
    tr = _tile(rows, ROW_TILE)
    grid = (n // tn, kw // tmw, rows // tr)
    a_specs, a_per = _piece_specs(a_pieces, tr, tmw, lambda j, i, k: k, lambda j, i, k: i, a_cb0)
    g_specs, g_per = _piece_specs(g_pieces, tr, tn, lambda j, i, k: k, lambda j, i, k: j, 0)
    if shard_cols:
        per = tn // shard_cols
        out_shape = [jax.ShapeDtypeStruct((n // shard_cols, kw, shard_cols), out_dtype)]
        out_specs = [pl.BlockSpec((per, tmw, shard_cols), lambda j, i, k: (j, i, 0))]

        def store(outs, acc):
            for q in range(per):
                outs[0][q] = acc[:, q * shard_cols:(q + 1) * shard_cols].astype(out_dtype)
    else:
        out_shape = [jax.ShapeDtypeStruct((kw, n), out_dtype)]
        out_specs = [pl.BlockSpec((tmw, tn), lambda j, i, k: (i, j))]

        def store(outs, acc):
            outs[0][...] = acc.astype(out_dtype)
    if colsum:
        out_shape.append(jax.ShapeDtypeStruct((1, n), F32))
        out_specs.append(pl.BlockSpec((1, tn), lambda j, i, k: (0, j)))
    res = _mm_call(name, grid, TN, a_pieces, a_specs, (a_per, lambda j, i, k: i), g_pieces, g_specs,
                   (g_per, lambda j, i, k: j), [], [], out_shape, out_specs, (tmw, tn), a_fn, store,
                   colsum_width=tn if colsum else 0, deps=() if dep is None else (dep,))
    return res if colsum else res[0]


def _ln(x, g, b):
    mu = jnp.mean(x, axis=-1, keepdims=True)
    xc = x - mu
    var = jnp.mean(xc * xc, axis=-1, keepdims=True)
    return xc * lax.rsqrt(var + LN_EPS) * g + b


def _relu2(x):
    r = jnp.maximum(x.astype(F32), 0.0)
    return r * r


def _glu(y, gate):
    return y * jax.nn.sigmoid(gate)


def _rowwise(name, fn, ins, n_out, width):
    rows = ins[0][0].shape[0]
    tm = _tile(rows, ROW_TILE)

    def body(*refs):
        res = fn(*[r[...] for r in refs[:len(ins)]])
        for o, v in zip(refs[len(ins):], res):
            o[...] = v

    return pl.pallas_call(
        body, name=name, grid=(rows // tm,),
        in_specs=[pl.BlockSpec((tm, wd), lambda i, cb=cb: (i, cb)) for _, cb, wd in ins],
        out_specs=[pl.BlockSpec((tm, width), lambda i: (i, 0))] * n_out,
        out_shape=[jax.ShapeDtypeStruct((rows, width), F32)] * n_out, compiler_params=_params(("parallel",)),
    )(*[a for a, _, _ in ins])


def _loss_grad(name, r, g, b, target, n_batch, lp, lead):
    rows, d = r.shape
    nq = lp // LANES
    lead_blocks = lead // LANES

    def body(r_ref, g_ref, b_ref, t_ref, gr_ref, gg_ref, gb_ref, loss_ref):
        i = pl.program_id(1)

        @pl.when((pl.program_id(0) == 0) & (i == 0))
        def _():
            loss_ref[...] = jnp.zeros_like(loss_ref)
            gg_ref[...] = jnp.zeros_like(gg_ref)
            gb_ref[...] = jnp.zeros_like(gb_ref)

        h, vjp = jax.vjp(_ln, r_ref[...], g_ref[...], b_ref[...])
        diff = jnp.where(i >= lead_blocks, h - t_ref[...], 0.0)
        gr, gg, gb = vjp(diff * (1.0 / d))
        gr_ref[...] = gr
        gg_ref[...] += gg
        gb_ref[...] += gb
        loss_ref[...] += 0.5 * jnp.sum(diff * diff) * (1.0 / d)

    vec = pl.BlockSpec((1, d), lambda b, i: (0, 0))
    row = pl.BlockSpec((LANES, d), lambda b, i: (b * nq + i, 0))
    return pl.pallas_call(
        body, name=name, grid=(n_batch, nq),
        in_specs=[row, vec, vec, pl.BlockSpec((None, LANES, d), lambda b, i: (b, jnp.maximum(i - lead_blocks, 0), 0))],
        out_specs=[row, vec, vec, pl.BlockSpec((SUBLANES, LANES), lambda b, i: (0, 0))],
        out_shape=[jax.ShapeDtypeStruct((rows, d), F32), jax.ShapeDtypeStruct((1, d), F32),
                   jax.ShapeDtypeStruct((1, d), F32), jax.ShapeDtypeStruct((SUBLANES, LANES), F32)],
        compiler_params=_params(("arbitrary", "arbitrary")),
    )(r, g, b, target)


def _meta_grad(name, g_h0, n_batch, lp, pad, n_meta):
    d = g_h0.shape[1]
    per = lp // n_meta
    at = pad // n_meta

    def body(g_ref, o_ref):
        @pl.when(pl.program_id(0) == 0)
        def _():
            o_ref[...] = jnp.zeros_like(o_ref)

        o_ref[...] += g_ref[...]

    return pl.pallas_call(
        body, name=name, grid=(n_batch,),
        in_specs=[pl.BlockSpec((n_meta, d), lambda b: (b * per + at, 0))],
        out_specs=pl.BlockSpec((n_meta, d), lambda b: (0, 0)),
        out_shape=jax.ShapeDtypeStruct((n_meta, d), F32),
        compiler_params=_params(("arbitrary",)),
    )(g_h0)


def _s5_param_fn(lr, li, ldt, br, bi):
    dt = jnp.exp(ldt)
    e = jnp.exp(lr * dt)
    w = li * dt
    lbr = e * jnp.cos(w)
    lbi = e * jnp.sin(w)
    nr = lbr - 1.0
    den = lr * lr + li * li
    cr = (nr * lr + lbi * li) / den
    ci = (lbi * lr - nr * li) / den
    bbr = cr[:, None, :] * br - ci[:, None, :] * bi
    bbi = cr[:, None, :] * bi + ci[:, None, :] * br
    return lbr, lbi, bbr, bbi


def _s5_params(name, lr, li, ldt, br, bi):
    def body(lr_ref, li_ref, ldt_ref, br_ref, bi_ref, o1, o2, o3, o4):
        res = _s5_param_fn(lr_ref[...], li_ref[...], ldt_ref[...], br_ref[...], bi_ref[...])
        for o, v in zip((o1, o2, o3, o4), res):
            o[...] = v

    shp = [jax.ShapeDtypeStruct(lr.shape, F32)] * 2 + [jax.ShapeDtypeStruct(br.shape, F32)] * 2
    return pl.pallas_call(body, name=name, out_shape=shp)(lr, li, ldt, br, bi)


def _s5_params_bwd(name, lr, li, ldt, br, bi, g_lbr, g_lbi, g_bbr, g_bbi, gd_parts):
    def body(lr_ref, li_ref, ldt_ref, br_ref, bi_ref, g1, g2, g3, g4, gd_ref, o1, o2, o3, o4, o5, o6):
        _, vjp = jax.vjp(_s5_param_fn, lr_ref[...], li_ref[...], ldt_ref[...], br_ref[...], bi_ref[...])
        res = vjp((jnp.sum(g1[...], axis=0), jnp.sum(g2[...], axis=0), g3[...], g4[...]))
        for o, v in zip((o1, o2, o3, o4, o5), res):
            o[...] = v
        o6[...] = jnp.sum(gd_ref[...], axis=0)

    shp = ([jax.ShapeDtypeStruct(lr.shape, F32)] * 2 + [jax.ShapeDtypeStruct(ldt.shape, F32)]
           + [jax.ShapeDtypeStruct(br.shape, F32)] * 2 + [jax.ShapeDtypeStruct(gd_parts.shape[1:], F32)])
    return pl.pallas_call(body, name=name, out_shape=shp)(lr, li, ldt, br, bi, g_lbr, g_lbi, g_bbr, g_bbi, gd_parts)


def _interleave(re, im, w):
    lead = re.shape[:-1]
    nj = re.shape[-1] // w
    return jnp.stack([re.reshape(*lead, nj, w), im.reshape(*lead, nj, w)], axis=-2).reshape(*lead, 2 * nj * w)


def _deinterleave(x, w):
    lead = x.shape[:-1]
    nj = x.shape[-1] // (2 * w)
    y = x.reshape(*lead, nj, 2, w)
    return y[..., 0, :].reshape(*lead, nj * w), y[..., 1, :].reshape(*lead, nj * w)


def _cmul(ar, ai, br, bi):
    return ar * br - ai * bi, ar * bi + ai * br


def _powers(lr, li):
    p = [(lr, li)]
    p.append(_cmul(*p[0], *p[0]))
    p.append(_cmul(*p[1], *p[0]))
    p.append(_cmul(*p[1], *p[1]))
    p.append(_cmul(*p[3], *p[0]))
    p.append(_cmul(*p[3], *p[1]))
    p.append(_cmul(*p[3], *p[2]))
    p.append(_cmul(*p[3], *p[3]))
    return p


def _scan_tile(xr, xi, steps):
    for sh, br, bi, m in steps:
        rr = jnp.where(m, pltpu.roll(xr, sh, 0), 0.0)
        ri = jnp.where(m, pltpu.roll(xi, sh, 0), 0.0)
        xr, xi = xr + (br * rr - bi * ri), xi + (br * ri + bi * rr)
    return xr, xi


def _s5_scan(name, bu, lam, n_batch, lp, w):
    rows, two_ns = bu.shape
    nj = two_ns // (2 * w)
    nt = lp // SUBLANES

    def body(x_ref, lam_ref, s_ref):
        pw = _powers(lam_ref[:, :w], lam_ref[:, w:])
        tab_r = jnp.concatenate([p[0] for p in pw], axis=0)
        tab_i = jnp.concatenate([p[1] for p in pw], axis=0)
        row = lax.broadcasted_iota(jnp.int32, (SUBLANES, w), 0)
        steps = [(s, jnp.broadcast_to(pw[s - 1][0], (SUBLANES, w)), jnp.broadcast_to(pw[s - 1][1], (SUBLANES, w)),
                  row >= s) for s in (1, 2, 4)]

        def tile(t, carry):
            cr, ci = carry
            r0 = pl.multiple_of(t * SUBLANES, SUBLANES)
            x = x_ref[pl.ds(r0, SUBLANES), :]
            xr, xi = _scan_tile(x[:, :w], x[:, w:], steps)
            sr = xr + (tab_r * cr - tab_i * ci)
            si = xi + (tab_r * ci + tab_i * cr)
            s_ref[pl.ds(r0, SUBLANES), :] = jnp.concatenate([sr, si], axis=1)
            return sr[SUBLANES - 1:, :], si[SUBLANES - 1:, :]

        zero = jnp.zeros((1, w), F32)
        _unrolled_loop(nt, tile, (zero, zero), SCAN_UNROLL)

    spec = pl.BlockSpec((lp, 2 * w), lambda b, j: (b, j))
    return pl.pallas_call(
        body, name=name, grid=(n_batch, nj), in_specs=[spec, pl.BlockSpec((1, 2 * w), lambda b, j: (0, j))],
        out_specs=spec, out_shape=jax.ShapeDtypeStruct((rows, two_ns), F32),
        compiler_params=_params(("parallel", "parallel")),
    )(bu, lam)


def _s5_scan_bwd(name, gd, states, lam, n_batch, lp, w):
    rows, two_ns = gd.shape
    nj = two_ns // (2 * w)
    nt = lp // SUBLANES

    def body(x_ref, s_ref, lam_ref, g_ref, gl_ref):
        pw = _powers(lam_ref[:, :w], -lam_ref[:, w:])
        tab_r = jnp.concatenate([p[0] for p in reversed(pw)], axis=0)
        tab_i = jnp.concatenate([p[1] for p in reversed(pw)], axis=0)
        row = lax.broadcasted_iota(jnp.int32, (SUBLANES, w), 0)
        steps = [(SUBLANES - s, jnp.broadcast_to(pw[s - 1][0], (SUBLANES, w)),
                  jnp.broadcast_to(pw[s - 1][1], (SUBLANES, w)), row < SUBLANES - s) for s in (1, 2, 4)]

        def tile(u, carry):
            cr, ci, ar, ai = carry
            t = nt - 1 - u
            r0 = pl.multiple_of(t * SUBLANES, SUBLANES)
            x = x_ref[pl.ds(r0, SUBLANES), :]
            xr, xi = _scan_tile(x[:, :w], x[:, w:], steps)
            gr = xr + (tab_r * cr - tab_i * ci)
            gi = xi + (tab_r * ci + tab_i * cr)
            g_ref[pl.ds(r0, SUBLANES), :] = jnp.concatenate([gr, gi], axis=1)
            p0 = pl.multiple_of(jnp.maximum(t - 1, 0) * SUBLANES, SUBLANES)
            prev = s_ref[pl.ds(p0, SUBLANES), :][SUBLANES - 1:, :] * jnp.where(t > 0, 1.0, 0.0)
            cur = s_ref[pl.ds(r0, SUBLANES), :]
            spr = jnp.where(row >= 1, pltpu.roll(cur[:, :w], 1, 0), prev[:, :w])
            spi = jnp.where(row >= 1, pltpu.roll(cur[:, w:], 1, 0), prev[:, w:])
            return gr[:1, :], gi[:1, :], ar + gr * spr + gi * spi, ai + gi * spr - gr * spi

        z1 = jnp.zeros((1, w), F32)
        z8 = jnp.zeros((SUBLANES, w), F32)
        _, _, ar, ai = _unrolled_loop(nt, tile, (z1, z1, z8, z8), SCAN_UNROLL)
        gl_ref[...] = jnp.concatenate([jnp.sum(ar, axis=0, keepdims=True), jnp.sum(ai, axis=0, keepdims=True)], axis=1)

    spec = pl.BlockSpec((lp, 2 * w), lambda b, j: (b, j))
    return pl.pallas_call(
        body, name=name, grid=(n_batch, nj),
        in_specs=[spec, spec, pl.BlockSpec((1, 2 * w), lambda b, j: (0, j))],
        out_specs=[spec, pl.BlockSpec((None, 1, 2 * w), lambda b, j: (b, 0, j))],
        out_shape=[jax.ShapeDtypeStruct((rows, two_ns), F32), jax.ShapeDtypeStruct((n_batch, 1, two_ns), F32)],
        compiler_params=_params(("parallel", "parallel")),
    )(gd, states, lam)


def _log_sigmoid(z):
    return jnp.minimum(z, 0.0) - jnp.log(1.0 + jnp.exp(-jnp.abs(z)))


ATTN_KEYS = 256
ATTN_GROUP = 4


def _attn_block(i, jb, lp, pad):
    start = jb * ATTN_KEYS
    r0 = pl.multiple_of(jnp.minimum(start, lp - ATTN_KEYS), LANES)
    rowpos = i * LANES + lax.broadcasted_iota(jnp.int32, (LANES, ATTN_KEYS), 0)
    keypos = r0 + lax.broadcasted_iota(jnp.int32, (LANES, ATTN_KEYS), 1)
    return r0, (keypos < rowpos) & (keypos >= jnp.maximum(start, pad))


def _tri_ones(strict_upper):
    r = lax.broadcasted_iota(jnp.int32, (ATTN_KEYS, ATTN_KEYS + LANES), 0)
    c = lax.broadcasted_iota(jnp.int32, (ATTN_KEYS, ATTN_KEYS + LANES), 1)
    tri = (r > c) if strict_upper else (r < c)
    return jnp.where((c >= ATTN_KEYS) | tri, 1.0, 0.0).astype(BF16)


def _split_sums(cr):
    rs = cr[:, ATTN_KEYS:]
    return cr[:, :ATTN_KEYS], jnp.concatenate([rs] * (ATTN_KEYS // LANES), axis=1)


def _head_masks():
    lane = lax.broadcasted_iota(jnp.int32, (1, LANES), 1)
    return [lane < SB_HEAD_DIM, lane >= SB_HEAD_DIM]


def _run_groups(n, first, sign, make):
    j, left, g = first, n, ATTN_GROUP
    while g >= 1:
        shift = g.bit_length() - 1
        count = lax.shift_right_logical(left, shift)
        fn = make(g)

        def loop(_, jcur, fn=fn, g=g):
            fn(jcur)
            return jcur + sign * g

        j = lax.fori_loop(0, count, loop, j)
        left = left - lax.shift_left(count, shift)
        g //= 2


def _attn_fwd(name, proj, n_batch, lp, pad, q_cb, k_cb, v_cb, n_pairs):
    rows = proj.shape[0]
    nq = lp // LANES
    scale = SB_HEAD_DIM ** -0.5

    def body(q_ref, k_ref, v_ref, o_ref, acc_s):
        i = pl.program_id(2)
        hm = _head_masks()
        comb = _tri_ones(True)
        qs = q_ref[...] * scale
        qh = [jnp.where(m, qs, 0.0).astype(BF16) for m in hm]
        acc_s[...] = jnp.zeros_like(acc_s)
        o_ref[...] = jnp.zeros_like(o_ref)

        def make(group):
            def fn(jtop):
                chains = []
                for g in range(group):
                    r0, vis = _attn_block(i, jtop - g, lp, pad)
                    kj = k_ref[pl.ds(r0, ATTN_KEYS), :].astype(BF16)
                    vj = v_ref[pl.ds(r0, ATTN_KEYS), :]
                    for h in range(2):
                        z = lax.dot_general(qh[h], kj, NT, preferred_element_type=F32)
                        chains.append((h, vis, z, jnp.where(hm[h], vj, 0.0).astype(BF16)))
                staged = []
                for h, vis, z, vh in chains:
                    lsz = _log_sigmoid(z)
                    staged.append((h, vis, lsz, _running_sums(jnp.where(vis, lsz - z, 0.0), comb, split=True), vh))
                out = o_ref[...]
                for h, vis, lsz, cr, vh in staged:
                    later, rs = _split_sums(cr)
                    acc = acc_s[h]
                    wgt = jnp.where(vis, jnp.exp(lsz + later + acc), 0.0)
                    acc_s[h] = acc + rs
                    out = out + lax.dot_general(wgt.astype(BF16), vh, NN, preferred_element_type=F32)
                o_ref[...] = out
            return fn

        n_blocks = lax.shift_right_logical(i + ATTN_KEYS // LANES, (ATTN_KEYS // LANES).bit_length() - 1)
        _run_groups(n_blocks, n_blocks - 1, -1, make)

    return pl.pallas_call(
        body, name=name, grid=(n_batch, n_pairs, nq),
        in_specs=[pl.BlockSpec((LANES, LANES), lambda b, h, i: (b * nq + i, q_cb + h)),
                  pl.BlockSpec((lp, LANES), lambda b, h, i: (b, k_cb + h)),
                  pl.BlockSpec((lp, LANES), lambda b, h, i: (b, v_cb + h))],
        out_specs=pl.BlockSpec((LANES, LANES), lambda b, h, i: (b * nq + i, h)),
        out_shape=jax.ShapeDtypeStruct((rows, n_pairs * LANES), F32),
        scratch_shapes=[pltpu.VMEM((2, LANES, ATTN_KEYS), F32)],
        compiler_params=_params(("parallel", "parallel", "arbitrary")),
    )(proj, proj, proj)


def _attn_bwd(name, proj, g_out, n_batch, lp, pad, q_cb, k_cb, v_cb, go_cb, n_pairs):
    rows = proj.shape[0]
    nq = lp // LANES
    scale = SB_HEAD_DIM ** -0.5

    def body(q_ref, k_ref, v_ref, go_ref, gq_ref, gk_ref, gv_ref, ga_s, sz_s, acc_s):
        i = pl.program_id(2)

        @pl.when(i == 0)
        def _():
            gk_ref[...] = jnp.zeros_like(gk_ref)
            gv_ref[...] = jnp.zeros_like(gv_ref)

        hm = _head_masks()
        comb_up = _tri_ones(True)
        comb_lo = _tri_ones(False)
        qs = q_ref[...] * scale
        go = go_ref[...]
        qh = [jnp.where(m, qs, 0.0).astype(BF16) for m in hm]
        goh = [jnp.where(m, go, 0.0).astype(BF16) for m in hm]
        acc_s[...] = jnp.zeros_like(acc_s)
        gq_ref[...] = jnp.zeros_like(gq_ref)

        def make_down(group):
            def fn(jtop):
                chains = []
                for g in range(group):
                    j = jtop - g
                    r0, vis = _attn_block(i, j, lp, pad)
                    kj = k_ref[pl.ds(r0, ATTN_KEYS), :].astype(BF16)
                    vj = v_ref[pl.ds(r0, ATTN_KEYS), :].astype(BF16)
                    for h in range(2):
                        z = lax.dot_general(qh[h], kj, NT, preferred_element_type=F32)
                        gw = lax.dot_general(goh[h], vj, NT, preferred_element_type=F32)
                        chains.append((h, j, r0, vis, z, gw))
                staged = []
                for h, j, r0, vis, z, gw in chains:
                    lsz = _log_sigmoid(z)
                    staged.append((h, j, r0, vis, lsz, _running_sums(jnp.where(vis, lsz - z, 0.0), comb_up), gw))
                for h, j, r0, vis, lsz, cr, gw in staged:
                    later, rs = _split_sums(cr)
                    acc = acc_s[h]
                    wgt = jnp.where(vis, jnp.exp(lsz + later + acc), 0.0)
                    acc_s[h] = acc + rs
                    ga_s[h, j] = gw * wgt
                    sz_s[h, j] = jnp.exp(lsz)
                    gv_ref[pl.ds(r0, ATTN_KEYS), :] += lax.dot_general(wgt.astype(BF16), goh[h], TN, preferred_element_type=F32)
            return fn

        n_blocks = lax.shift_right_logical(i + ATTN_KEYS // LANES, (ATTN_KEYS // LANES).bit_length() - 1)
        _run_groups(n_blocks, n_blocks - 1, -1, make_down)
        acc_s[...] = jnp.zeros_like(acc_s)

        def make_up(group):
            def fn(jbot):
                pend = []
                for g in range(group):
                    j = jbot + g
                    r0, vis = _attn_block(i, j, lp, pad)
                    kj = k_ref[pl.ds(r0, ATTN_KEYS), :]
                    for h in range(2):
                        ga = ga_s[h, j]
                        pend.append((h, j, r0, vis, ga, _running_sums(ga, comb_lo), jnp.where(hm[h], kj, 0.0).astype(BF16)))
                for h, j, r0, vis, ga, cr, kh in pend:
                    before, rs = _split_sums(cr)
                    pre = acc_s[h]
                    glk = before + pre
                    acc_s[h] = pre + rs
                    sz = sz_s[h, j]
                    gz = jnp.where(vis, ga * (1.0 - sz) - glk * sz, 0.0).astype(BF16)
                    gq_ref[...] += lax.dot_general(gz, kh, NN, preferred_element_type=F32)
                    gk_ref[pl.ds(r0, ATTN_KEYS), :] += lax.dot_general(gz, qh[h], TN, preferred_element_type=F32)
            return fn

        _run_groups(n_blocks, 0, 1, make_up)
        gq_ref[...] = gq_ref[...] * scale

    blk = lambda cb: pl.BlockSpec((LANES, LANES), lambda b, h, i: (b * nq + i, cb + h))
    full = lambda cb: pl.BlockSpec((lp, LANES), lambda b, h, i: (b, cb + h))
    shp = jax.ShapeDtypeStruct((rows, n_pairs * LANES), F32)
    per_block = pltpu.VMEM((2, -(-lp // ATTN_KEYS), LANES, ATTN_KEYS), F32)
    return pl.pallas_call(
        body, name=name, grid=(n_batch, n_pairs, nq),
        in_specs=[blk(q_cb), full(k_cb), full(v_cb), blk(go_cb)],
        out_specs=[blk(0), full(0), full(0)], out_shape=[shp, shp, shp],
        scratch_shapes=[per_block, per_block, pltpu.VMEM((2, LANES, ATTN_KEYS), F32)],
        compiler_params=_params(("parallel", "parallel", "arbitrary")),
    )(proj, proj, proj, g_out)


def _lb_fn(gamma):
    g0, g1 = gamma[0:1, :], gamma[1:2, :]
    mx = jnp.maximum(g0, g1)
    e0, e1 = jnp.exp(g0 - mx), jnp.exp(g1 - mx)
    p0, p1 = e0 / (e0 + e1), e1 / (e0 + e1)
    return (p0 + p1) - p0


def _lower_bound(name, gamma):
    def body(g_ref, o_ref):
        o_ref[...] = _lb_fn(g_ref[...])

    return pl.pallas_call(body, name=name, out_shape=jax.ShapeDtypeStruct((1, gamma.shape[1]), F32))(gamma)


def _lower_bound_bwd(name, gamma, g_lb_parts, g_ng_parts):
    def body(g_ref, glb_ref, gng_ref, o_ref, o2_ref):
        _, vjp = jax.vjp(_lb_fn, g_ref[...])
        o_ref[...] = vjp(jnp.sum(glb_ref[...], axis=0))[0]
        o2_ref[...] = jnp.sum(gng_ref[...], axis=0)

    return pl.pallas_call(
        body, name=name,
        out_shape=[jax.ShapeDtypeStruct(gamma.shape, F32), jax.ShapeDtypeStruct((1, gamma.shape[1]), F32)],
    )(gamma, g_lb_parts, g_ng_parts)


def _tri_times(tril, x, dims):
    hi = x.astype(BF16)
    lo = (x - hi.astype(F32)).astype(BF16)
    t = tril.astype(BF16)
    return (lax.dot_general(t, hi, dims, preferred_element_type=F32)
            + lax.dot_general(t, lo, dims, preferred_element_type=F32))


@jax.custom_vjp
def _cumsum_rows(x, tril):
    return _tri_times(tril, x, NN)


def _cumsum_rows_fwd(x, tril):
    return _tri_times(tril, x, NN), tril


def _cumsum_rows_bwd(tril, g):
    return _tri_times(tril, g, TN), jnp.zeros_like(tril)


_cumsum_rows.defvjp(_cumsum_rows_fwd, _cumsum_rows_bwd)


def _hg_decays(f_pre, lbs, masks, tril):
    f = [[lb + (1.0 - lb) * jax.nn.sigmoid(fc) for fc, lb in zip(row, lbs)] for row in f_pre]
    bcum = [[_cumsum_rows(jnp.log(x) * m, tril) for x in row] for row, m in zip(f, masks)]
    return [[1.0 - x for x in row] for row in f], bcum


def _hg_step(q, f_pre, i_in, g, lbs, ngs, sts, masks, tril):
    k, bcum = _hg_decays(f_pre, lbs, masks, tril)
    v = [[ic * m for ic in row] for row, m in zip(i_in, masks)]
    qd = [[qc * jnp.exp(b) for qc, b in zip(qr, br)] for qr, br in zip(q, bcum)]
    scores = [[jnp.where(tril > 0.5, _dot(a, kk * jnp.exp(-b), NT), 0.0) for a, kk, b in zip(ar, kr, br)]
              for ar, kr, br in zip(qd, k, bcum)]
    inner = [[_dot(s, x, NN) for s, x in zip(sr, vr)] for sr, vr in zip(scores, v)]
    add = [[_dot(x, kk * jnp.exp(b[HG_CHUNK - 1:, :] - b), TN) for x, kk, b in zip(vr, kr, br)]
           for vr, kr, br in zip(v, k, bcum)]
    outs = []
    for qr, br, nr, ar, gr in zip(qd, bcum, inner, add, g):
        o = [n + _dot(a, st, NT) for n, a, st in zip(nr, qr, sts)]
        sts = [jnp.exp(b[HG_CHUNK - 1:, :]) * st + a for b, a, st in zip(br, ar, sts)]
        o = [x * lax.rsqrt(jnp.mean(x * x, axis=-1, keepdims=True) + RMS_EPS) * ng for x, ng in zip(o, ngs)]
        outs.append([x * (gc * jax.nn.sigmoid(gc)) for x, gc in zip(o, gr)])
    return outs, sts


def _hg_consts(c, pad):
    r = lax.broadcasted_iota(jnp.int32, (HG_CHUNK, HG_CHUNK), 0)
    cc = lax.broadcasted_iota(jnp.int32, (HG_CHUNK, HG_CHUNK), 1)
    tril = jnp.where(r >= cc, 1.0, 0.0).astype(F32)
    pos = c * HG_CHUNK + lax.broadcasted_iota(jnp.int32, (HG_CHUNK, 1), 0)
    return tril, jnp.where(pos >= pad, 1.0, 0.0).astype(F32)


HG_HEADS_PER_STEP = 4
HG_CHUNKS_PER_STEP = 2


def _hg_layout(lp, n_heads):
    step_rows = HG_CHUNKS_PER_STEP * HG_CHUNK
    per = min(HG_HEADS_PER_STEP, n_heads)
    assert lp % step_rows == 0 and n_heads % per == 0
    heads = [(h, slice(h * HG_DK, (h + 1) * HG_DK)) for h in range(per)]
    return n_heads // per, lp // step_rows, step_rows, per * HG_DK, heads


def _hg_step_views(step, pad, heads):
    slices = [slice(u * HG_CHUNK, (u + 1) * HG_CHUNK) for u in range(HG_CHUNKS_PER_STEP)]
    consts = [_hg_consts(step * HG_CHUNKS_PER_STEP + u, pad) for u in range(HG_CHUNKS_PER_STEP)]
    load = lambda ref: [[ref[sl, cols] for _, cols in heads] for sl in slices]
    return slices, [m for _, m in consts], consts[0][0], load


def _hgrn_fwd(name, proj, lb, ng, n_batch, lp, pad, n_heads):
    rows = proj.shape[0]
    groups, steps, step_rows, wide, heads = _hg_layout(lp, n_heads)

    def body(q_ref, f_ref, i_ref, g_ref, lb_ref, ng_ref, o_ref, s_ref, st_s):
        t = pl.program_id(2)

        @pl.when(t == 0)
        def _():
            st_s[...] = jnp.zeros_like(st_s)

        slices, masks, tril, load = _hg_step_views(t, pad, heads)
        sts = [st_s[h] for h, _ in heads]
        for (_, cols), st in zip(heads, sts):
            s_ref[:, cols] = st
        outs, sts = _hg_step(load(q_ref), load(f_ref), load(i_ref), load(g_ref),
                             [lb_ref[:, cols] for _, cols in heads], [ng_ref[:, cols] for _, cols in heads],
                             sts, masks, tril)
        for sl, row in zip(slices, outs):
            for (_, cols), o in zip(heads, row):
                o_ref[sl, cols] = o
        for (h, _), st in zip(heads, sts):
            st_s[h] = st

    col = lambda off: pl.BlockSpec((step_rows, wide), lambda b, h, t: (b * steps + t, off * groups + h))
    vec = pl.BlockSpec((1, wide), lambda b, h, t: (0, h))
    return pl.pallas_call(
        body, name=name, grid=(n_batch, groups, steps), in_specs=[col(0), col(1), col(2), col(3), vec, vec],
        out_specs=[col(0), pl.BlockSpec((HG_DK, wide), lambda b, h, t: (b * steps + t, h))],
        out_shape=[jax.ShapeDtypeStruct((rows, n_heads * HG_DK), F32),
                   jax.ShapeDtypeStruct((n_batch * steps * HG_DK, n_heads * HG_DK), F32)],
        scratch_shapes=[pltpu.VMEM((len(heads), HG_DK, HG_DK), F32)],
        compiler_params=_params(("parallel", "parallel", "arbitrary")),
    )(proj, proj, proj, proj, lb, ng)


def _hgrn_bwd(name, proj, lb, ng, g_out, states, n_batch, lp, pad, n_heads):
    rows = proj.shape[0]
    width = n_heads * HG_DK
    groups, steps, step_rows, wide, heads = _hg_layout(lp, n_heads)

    def body(q_ref, f_ref, i_ref, g_ref, lb_ref, ng_ref, go_ref, s_ref, gq_ref, gf_ref, gi_ref, gg_ref, glb_ref, gng_ref, gst_s):
        t = pl.program_id(2)

        @pl.when(t == 0)
        def _():
            gst_s[...] = jnp.zeros_like(gst_s)
            glb_ref[...] = jnp.zeros_like(glb_ref)
            gng_ref[...] = jnp.zeros_like(gng_ref)

        slices, masks, tril, load = _hg_step_views(steps - 1 - t, pad, heads)
        fn = functools.partial(_hg_step, masks=masks, tril=tril)
        _, vjp = jax.vjp(fn, load(q_ref), load(f_ref), load(i_ref), load(g_ref),
                         [lb_ref[:, cols] for _, cols in heads], [ng_ref[:, cols] for _, cols in heads],
                         [s_ref[:, cols] for _, cols in heads])
        gq, gf, gi, gg, glb, gng, gst = vjp((load(go_ref), [gst_s[h] for h, _ in heads]))
        for ref, grads in ((gq_ref, gq), (gf_ref, gf), (gi_ref, gi), (gg_ref, gg)):
            for sl, row in zip(slices, grads):
                for (_, cols), x in zip(heads, row):
                    ref[sl, cols] = x.astype(BF16)
        for (h, cols), a, b, c in zip(heads, gst, glb, gng):
            gst_s[h] = a
            glb_ref[:, cols] += b
            gng_ref[:, cols] += c

    col = lambda off: pl.BlockSpec((step_rows, wide), lambda b, h, t: (b * steps + steps - 1 - t, off * groups + h))
    vec = pl.BlockSpec((1, wide), lambda b, h, t: (0, h))
    part = pl.BlockSpec((None, 1, wide), lambda b, h, t: (b, 0, h))
    big = jax.ShapeDtypeStruct((rows, width), BF16)
    small = jax.ShapeDtypeStruct((n_batch, 1, width), F32)
    return pl.pallas_call(
        body, name=name, grid=(n_batch, groups, steps),
        in_specs=[col(0), col(1), col(2), col(3), vec, vec, col(0),
                  pl.BlockSpec((HG_DK, wide), lambda b, h, t: (b * steps + steps - 1 - t, h))],
        out_specs=[col(0), col(0), col(0), col(0), part, part],
        out_shape=[big, big, big, big, small, small],
        scratch_shapes=[pltpu.VMEM((len(heads), HG_DK, HG_DK), F32)],
        compiler_params=_params(("parallel", "parallel", "arbitrary")),
    )(proj, proj, proj, proj, lb, ng, g_out, states)


def _exchange_copies(src, dst, send, recv, loc, scatter):
    x, y, c = lax.axis_index("x"), lax.axis_index("y"), lax.axis_index("c")
    me = 4 * x + 2 * y + c
    local, remote = [], []
    for w in range(len(src)):
        local.append(pltpu.make_async_copy(src[w].at[me] if scatter else src[w], dst[w].at[me], loc.at[w]))
    for k in range(1, N_DEV):
        px = 1 - x if k & 4 else x
        py = 1 - y if k & 2 else y
        pc = 1 - c if k & 1 else c
        peer = 4 * px + 2 * py + pc
        for w in range(len(src)):
            remote.append(pltpu.make_async_remote_copy(
                src_ref=src[w].at[peer] if scatter else src[w], dst_ref=dst[w].at[me],
                send_sem=send.at[w * (N_DEV - 1) + k - 1], recv_sem=recv.at[w * (N_DEV - 1) + k - 1],
                device_id=(px, py, pc), device_id_type=pl.DeviceIdType.MESH))
    return local, remote


_HBM_SPEC = pl.BlockSpec(memory_space=pltpu.HBM)
_SEM_SPEC = pl.BlockSpec(memory_space=pltpu.SEMAPHORE)
_ANY_SPEC = pl.BlockSpec(memory_space=pl.ANY)
_DATAFLOW = pltpu.SideEffectType.DATAFLOW_SIDE_EFFECTING


def _exchange_start(name, srcs, scatter, dep=None):
    nw = len(srcs)
    srcs = [pltpu.with_memory_space_constraint(s, pltpu.HBM) for s in srcs]
    lands = [pltpu.with_memory_space_constraint(lax.empty(s.shape if scatter else (N_DEV,) + s.shape, s.dtype), pltpu.HBM)
             for s in srcs]
    deps = [] if dep is None else [dep]

    def body(*refs):
        src, dst = refs[:nw], refs[nw:2 * nw]
        send, recv, loc = refs[2 * nw + len(deps):2 * nw + len(deps) + 3]
        token = refs[-1]
        local, remote = _exchange_copies(src, dst, send, recv, loc, scatter)
        for cp in local + remote:
            cp.start()
        token[...] = jnp.zeros_like(token)

    sems = [pltpu.SemaphoreType.DMA((nw * (N_DEV - 1),)), pltpu.SemaphoreType.DMA((nw * (N_DEV - 1),)),
            pltpu.SemaphoreType.DMA((nw,))]
    out = pl.pallas_call(
        body, name=name,
        out_shape=(*sems, *[pltpu.HBM(s.shape, s.dtype) for s in srcs], *[pltpu.HBM(s.shape, s.dtype) for s in lands],
                   jax.ShapeDtypeStruct((SUBLANES, LANES), F32)),
        in_specs=[_HBM_SPEC] * (2 * nw) + [_ANY_SPEC] * len(deps),
        out_specs=(_SEM_SPEC, _SEM_SPEC, _SEM_SPEC, *[_HBM_SPEC] * (2 * nw), pl.BlockSpec(memory_space=pltpu.VMEM)),
        input_output_aliases={i: 3 + i for i in range(2 * nw)},
        compiler_params=pltpu.CompilerParams(has_side_effects=_DATAFLOW),
    )(*srcs, *lands, *deps)
    return {"sems": out[:3], "srcs": out[3:3 + nw], "lands": out[3 + nw:3 + 2 * nw], "token": out[-1], "scatter": scatter}


def _exchange_wait(name, handle, after):
    nw = len(handle["srcs"])
    scatter = handle["scatter"]

    def body(*refs):
        src, dst = refs[:nw], refs[nw:2 * nw]
        send, recv, loc = refs[2 * nw:2 * nw + 3]
        local, remote = _exchange_copies(src, dst, send, recv, loc, scatter)
        for cp in local:
            cp.wait()
        for cp in remote:
            cp.wait_send()
            cp.wait_recv()

    out = pl.pallas_call(
        body, name=name,
        out_shape=(*[pltpu.HBM(s.shape, s.dtype) for s in handle["srcs"]],
                   *[pltpu.HBM(s.shape, s.dtype) for s in handle["lands"]]),
        in_specs=[_HBM_SPEC] * (2 * nw) + [_SEM_SPEC] * 3 + [_ANY_SPEC],
        out_specs=tuple([_HBM_SPEC] * (2 * nw)),
        input_output_aliases={i: i for i in range(2 * nw)},
        compiler_params=pltpu.CompilerParams(has_side_effects=_DATAFLOW),
    )(*handle["srcs"], *handle["lands"], *handle["sems"], after)
    return list(out[nw:])


def _adamw(w, g, m, v):
    m = ADAM_B1 * m + (1.0 - ADAM_B1) * g
    v = ADAM_B2 * v + (1.0 - ADAM_B2) * (g * g)
    m_hat = m / (1.0 - ADAM_B1 ** ADAM_STEP)
    v_hat = v / (1.0 - ADAM_B2 ** ADAM_STEP)
    delta = -ADAM_LR * (m_hat / (jnp.sqrt(v_hat) + ADAM_EPS) + ADAM_WD * w)
    return delta, m, v


def _adamw_summed(name, parts, w, m, v):
    rows, cols = w.shape
    n_parts = parts.shape[0]
    tr = _tile(rows, max(SUBLANES, (1 << 18) // cols))

    def body(p_ref, w_ref, m_ref, v_ref, g_ref, d_ref, nm_ref, nv_ref):
        g = p_ref[0].astype(F32)
        for s in range(1, n_parts):
            g = g + p_ref[s].astype(F32)
        d, nm, nv = _adamw(w_ref[...], g, m_ref[...], v_ref[...])
        g_ref[...] = g
        d_ref[...] = d
        nm_ref[...] = nm
        nv_ref[...] = nv

    spec = pl.BlockSpec((tr, cols), lambda i: (i, 0))
    shp = jax.ShapeDtypeStruct((rows, cols), F32)
    return pl.pallas_call(
        body, name=name, grid=(rows // tr,),
        in_specs=[pl.BlockSpec((n_parts, tr, cols), lambda i: (0, i, 0)), spec, spec, spec],
        out_specs=[spec] * 4, out_shape=[shp] * 4, compiler_params=_params(("parallel",)),
    )(parts, w, m, v)


def _pack_rows(arrays, cols):
    out = []
    for a in arrays:
        flat = a.reshape(-1)
        n = -(-flat.shape[0] // cols) * cols
        out.append(jnp.pad(flat, (0, n - flat.shape[0])).reshape(-1, cols))
    packed = jnp.concatenate(out, axis=0)
    return jnp.pad(packed, ((0, -packed.shape[0] % SUBLANES), (0, 0)))


def _unpack_rows(packed, shapes, cols):
    out, r = [], 0
    for s in shapes:
        n = math.prod(s)
        nr = -(-n // cols)
        out.append(packed[r:r + nr].reshape(-1)[:n].reshape(s))
        r += nr
    return out


def _block_diag(blocks):
    g, a, b = blocks.shape
    eye = jnp.eye(g, dtype=blocks.dtype)
    return (eye[:, None, :, None] * blocks[:, :, None, :]).reshape(g * a, g * b)


def _diag_blocks(dense, g):
    a, b = dense.shape[0] // g, dense.shape[1] // g
    return jnp.einsum("gagb->gab", dense.reshape(g, a, g, b))


def _local_step(x, target, meta, wts, small, late_weights, on_grads, on_small):
    n_batch, seq, d = x.shape
    n_meta = meta.shape[0]
    pad = -(seq + n_meta) % LANES
    lead = pad + n_meta
    lp = lead + seq
    rows = n_batch * lp
    s5w = wts["glu"].shape[0]
    n_ab = wts["in_ab"].shape[2]
    ab_cols = wts["in_ab"].shape[0] * n_ab
    sbw = (ab_cols - s5w) // 3
    dff = small["mlp_b_up"].shape[1]
    n_pairs = sbw // LANES
    n_hg = d // HG_DK
    s5_cb = s5w // LANES
    sb_cb = sbw // LANES
    tm = _tile(rows, ROW_TILE)
    groups, n_state, grp = small["s5_b_re"].shape[1:]
    ns = groups * n_state
    sw = min(SCAN_LANES, ns)

    h0 = jnp.concatenate(
        [jnp.zeros((n_batch, pad, d), F32), jnp.broadcast_to(meta[None], (n_batch, n_meta, d)), x], axis=1
    ).reshape(rows, d)

    lam_re, lam_im = small["s5_lam_re"][0], small["s5_lam_im"][0]
    log_dt = small["s5_log_dt"][0][:, None]
    b_re_t = small["s5_b_re"][0].transpose(0, 2, 1)
    b_im_t = small["s5_b_im"][0].transpose(0, 2, 1)
    c_re, c_im = small["s5_c_re"][0], small["s5_c_im"][0]
    lbr, lbi, bbr, bbi = _s5_params("s5_params", lam_re, lam_im, log_dt, b_re_t, b_im_t)
    b_blk = _interleave(_block_diag(bbr), _block_diag(bbi), sw).astype(BF16)
    c_blk = _interleave(_block_diag(c_re), _block_diag(-c_im), sw).T.astype(BF16)
    lam_row = _interleave(lbr.reshape(1, ns), lbi.reshape(1, ns), sw)
    d_row = small["s5_d"].reshape(1, s5w)

    def ln_store(outs, acc, res, bias, g, b):
        r = ALPHA * res + acc + bias
        outs[0][...] = r
        if len(outs) > 1:
            outs[1][...] = _ln(r, g, b)

    zero_bias = jnp.zeros((1, d), F32)

    def mix_ln(name, a, w, k_total, tk, res, bias, g, b, a_fn=None, emit_h=True):
        n_out = 2 if emit_h else 1
        return _mm_act(name, a, w, "nat", n_out_cols=d, k_total=k_total, tn=d, tk=tk, a_fn=a_fn,
                       extras=(res, bias, g, b), extra_specs=(_row_spec(tm, d), _vec_spec(d), _vec_spec(d), _vec_spec(d)),
                       store=ln_store, out_shape=[jax.ShapeDtypeStruct((rows, d), F32)] * n_out,
                       out_specs=[_row_spec(tm, d)] * n_out)

    def two(width):
        return [jax.ShapeDtypeStruct((rows, width), F32)] * 2, [_row_spec(tm, width)] * 2

    def shard_tile(total, shard, cap=1024):
        t = max(shard, cap - cap % shard)
        while total % t:
            t -= shard
        return t

    proj_ab = _mm_act("in_ab", h0, wts["in_ab"], "stk", n_out_cols=ab_cols, k_total=d, tn=shard_tile(ab_cols, n_ab), tk=d)[0]
    bu = _mm_act("s5_bu", proj_ab, b_blk, "nat", n_out_cols=2 * ns, k_total=s5w, tn=min(2 * ns, 2048), tk=s5w)[0]
    states = _s5_scan("s5_scan", bu, lam_row, n_batch, lp, sw)

    def gelu_store(outs, acc, u, dv):
        ypre = acc + dv * u
        outs[0][...] = ypre
        outs[1][...] = jax.nn.gelu(ypre)

    shp2, spec2 = two(s5w)
    ypre, y = _mm_act(
        "s5_y", states, c_blk, "nat", n_out_cols=s5w, k_total=2 * ns, tn=s5w, tk=min(2 * ns, 1024),
        extras=(proj_ab, d_row), extra_specs=(_row_spec(tm, s5w), _vec_spec(s5w)), store=gelu_store,
        out_shape=shp2, out_specs=spec2)

    def glu_store(outs, acc, yv, bias):
        gate = acc + bias
        outs[0][...] = gate
        outs[1][...] = _glu(yv, gate)

    gate, a_out = _mm_act(
        "s5_glu", y, wts["glu"], "nat", n_out_cols=s5w, k_total=s5w, tn=s5w, tk=s5w,
        extras=(y, small["s5_b_glu"]), extra_specs=(_row_spec(tm, s5w), _vec_spec(s5w)), store=glu_store,
        out_shape=shp2, out_specs=spec2)
    b_out = _attn_fwd("sb_attn", proj_ab, n_batch, lp, pad, s5_cb, s5_cb + sb_cb, s5_cb + 2 * sb_cb, n_pairs)

    def bias_store(outs, acc, bias):
        outs[0][...] = (acc + bias).astype(outs[0].dtype)

    def wide(width, dtype):
        return [jax.ShapeDtypeStruct((rows, dff), dtype)], [_row_spec(tm, width)]

    def mlp_fwd(layer, h_in, emit_h=True):
        tn = shard_tile(dff, n_up)
        shp, spec = wide(tn, BF16)
        up = _mm_act(f"up{layer}", h_in, wts["up"][layer], "stk", n_out_cols=dff, k_total=d, tn=tn, tk=d,
                     extras=(small["mlp_b_up"][layer:layer + 1],), extra_specs=(_vec_spec(tn),), store=bias_store,
                     out_shape=shp, out_specs=spec)[0]
        return (up, *mix_ln(f"down{layer}", up, wts["down"][layer], dff, min(dff, 1024), h_in,
                            small["mlp_b_down"][layer:layer + 1], small["ln_mlp_g"][layer:layer + 1],
                            small["ln_mlp_b"][layer:layer + 1], a_fn=_relu2, emit_h=emit_h))

    r1, h1 = mix_ln("out_ab", [a_out, b_out], wts["out_ab"], s5w + sbw, min(s5w, sbw), h0, zero_bias,
                    small["ln_mix_g"][0:1], small["ln_mix_b"][0:1])
    wts = {**wts, **late_weights(r1)}
    n_c = wts["in_c"].shape[2]
    n_up = wts["up"][0].shape[2]
    up0, r2, h2 = mlp_fwd(0, h1)

    lb = _lower_bound("hg_lb", small["hgrn_gamma"])
    proj_c = _mm_act("in_c", h2, wts["in_c"], "stk", n_out_cols=4 * d, k_total=d, tn=shard_tile(4 * d, n_c), tk=d)[0]
    c_out, hg_states = _hgrn_fwd("hgrn", proj_c, lb, wts["ng"], n_batch, lp, pad, n_hg)
    r3, h3 = mix_ln("out_c", c_out, wts["out_c"], d, d, h2, zero_bias, small["ln_mix_g"][1:2], small["ln_mix_b"][1:2])
    up1, r4 = mlp_fwd(1, h3, emit_h=False)

    gr = {}
    g_r4, gr["ln_mlp_g1"], gr["ln_mlp_b1"], loss_tile = _loss_grad(
        "loss", r4, small["ln_mlp_g"][1:2], small["ln_mlp_b"][1:2], target, n_batch, lp, lead)

    def res_store(outs, acc, g_res):
        outs[0][...] = acc + ALPHA * g_res

    def ln_bwd_store(outs, acc, g_res, r_in, g, b, first_step):
        gr_in, gg, gb = jax.vjp(_ln, r_in, g, b)[1](acc + ALPHA * g_res)
        outs[0][...] = gr_in

        @pl.when(first_step)
        def _():
            outs[1][...] = jnp.zeros_like(outs[1])
            outs[2][...] = jnp.zeros_like(outs[2])

        outs[1][...] += gg
        outs[2][...] += gb

    def through_ln(name, a, w, k_total, tk, g_res, r_in, g, b, dep=None):
        vec = pl.BlockSpec((1, d), lambda i, j, k: (0, 0))
        return _mm_act(name, a, w, "stkT", n_out_cols=d, k_total=k_total, tn=d, tk=tk,
                       extras=(g_res, r_in, g, b), extra_specs=(_row_spec(tm, d), _row_spec(tm, d), vec, vec),
                       store=ln_bwd_store, sequential=True, dep=dep,
                       out_shape=[jax.ShapeDtypeStruct((rows, d), F32)] + [jax.ShapeDtypeStruct((1, d), F32)] * 2,
                       out_specs=[_row_spec(tm, d), vec, vec])

    def mlp_bwd(layer, g_r, up, h_in, r_in, send=None):
        def gup_store(outs, acc, upv):
            outs[0][...] = (acc * (2.0 * jnp.maximum(upv.astype(F32), 0.0))).astype(outs[0].dtype)

        tf = min(dff, 1024)
        shp, spec = wide(tf, BF16)
        g_up = _mm_act(f"g_up{layer}", g_r, wts["down"][layer], "natT", n_out_cols=dff, k_total=d, tn=tf, tk=d,
                       extras=(up,), extra_specs=(_row_spec(tm, tf),), store=gup_store, out_shape=shp, out_specs=spec)[0]
        gr[f"down{layer}"], gr[f"mlp_b_down{layer}"] = _mm_wgrad(
            f"dw_down{layer}", up, g_r, kw=dff, n=d, tmw=tf, tn=d, a_fn=_relu2, out_dtype=BF16, colsum=True)
        gr[f"up{layer}"], gr[f"mlp_b_up{layer}"] = _mm_wgrad(
            f"dw_up{layer}", h_in, g_up, kw=d, n=dff, tmw=d, tn=min(dff, 2048), shard_cols=n_up, out_dtype=BF16, colsum=True)
        dep = send() if send is not None else None
        g_r_in, gr[f"ln_mix_g{layer}"], gr[f"ln_mix_b{layer}"] = through_ln(
            f"g_hmid{layer}", g_up, wts["up"][layer], dff, shard_tile(dff, n_up), g_r, r_in,
            small["ln_mix_g"][layer:layer + 1], small["ln_mix_b"][layer:layer + 1], dep=dep)
        return g_r_in

    g_r3 = mlp_bwd(1, g_r4, up1, h3, r3)
    g_cout = _mm_act("g_cout", g_r3, wts["out_c"], "natT", n_out_cols=d, k_total=d, tn=d, tk=d)[0]
    gr["out_c"] = _mm_wgrad("dw_out_c", c_out, g_r3, kw=d, n=d, tmw=d, tn=d, out_dtype=BF16)
    gq, gf, gi, gg_, g_lb_parts, g_ng_parts = _hgrn_bwd("hgrn_bwd", proj_c, lb, wts["ng"], g_cout, hg_states,
                                                        n_batch, lp, pad, n_hg)
    g_pc = [gq, gf, gi, gg_]
    gr["hgrn_gamma"], gr["ng"] = _lower_bound_bwd("hg_lb_bwd", small["hgrn_gamma"], g_lb_parts, g_ng_parts)
    gr["in_c"] = _mm_wgrad("dw_in_c", h2, g_pc, kw=d, n=4 * d, tmw=d, tn=d, shard_cols=n_c, out_dtype=BF16)
    sent1 = on_grads(1, {"down1": gr["down1"], "up1": gr["up1"], "out_c": gr["out_c"], "in_c": gr["in_c"], "ng": gr["ng"]})
    g_r2, gr["ln_mlp_g0"], gr["ln_mlp_b0"] = through_ln(
        "g_h2", g_pc, wts["in_c"], 4 * d, n_c, g_r3, r2, small["ln_mlp_g"][0:1], small["ln_mlp_b"][0:1], dep=sent1)

    g_r1 = mlp_bwd(0, g_r2, up0, h1, r1, send=lambda: on_grads(2, {"down0": gr["down0"], "up0": gr["up0"]}))
    g_cat = _mm_act("g_cat", g_r1, wts["out_ab"], "natT", n_out_cols=d, k_total=d, tn=d, tk=d)[0]
    gr["out_ab"] = _mm_wgrad("dw_out_ab", [a_out, b_out], g_r1, kw=s5w + sbw, n=d, tmw=min(s5w, sbw), tn=d, out_dtype=BF16)
    g_q, g_k, g_v = _attn_bwd("sb_attn_bwd", proj_ab, g_cat, n_batch, lp, pad, s5_cb, s5_cb + sb_cb, s5_cb + 2 * sb_cb,
                              s5_cb, n_pairs)

    g_y_direct, g_gate = _rowwise("s5_glu_bwd", lambda ga, yv, gt: jax.vjp(_glu, yv, gt)[1](ga),
                                  [(g_cat, 0, s5w), (y, 0, s5w), (gate, 0, s5w)], 2, s5w)

    def gelu_bwd_store(outs, acc, gyd, yp, u, dv):
        gyp = jax.vjp(jax.nn.gelu, yp)[1](acc + gyd)[0]
        outs[0][...] = gyp
        outs[1][...] = dv * gyp
        outs[2][...] = jnp.sum(gyp * u, axis=0, keepdims=True)

    rs = _row_spec(tm, s5w)
    g_ypre, g_u_direct, gd_parts = _mm_act(
        "s5_g_y", g_gate, wts["glu"], "natT", n_out_cols=s5w, k_total=s5w, tn=s5w, tk=s5w,
        extras=(g_y_direct, ypre, proj_ab, d_row), extra_specs=(rs, rs, rs, _vec_spec(s5w)), store=gelu_bwd_store,
        out_shape=[jax.ShapeDtypeStruct((rows, s5w), F32)] * 2 + [jax.ShapeDtypeStruct((rows // tm, 1, s5w), F32)],
        out_specs=[rs, rs, pl.BlockSpec((None, 1, s5w), lambda i, j, k: (i, 0, j))])
    gr["glu"], gr["s5_b_glu"] = _mm_wgrad("dw_glu", y, g_gate, kw=s5w, n=s5w, tmw=s5w, tn=s5w, out_dtype=BF16, colsum=True)
    g_sd = _mm_act("s5_g_states", g_ypre, c_blk, "natT", n_out_cols=2 * ns, k_total=s5w, tn=min(2 * ns, 2048), tk=s5w)[0]
    d_cblk = _mm_wgrad("dw_cblk", states, g_ypre, kw=2 * ns, n=s5w, tmw=min(2 * ns, 1024), tn=s5w)
    gs, gl_parts = _s5_scan_bwd("s5_scan_bwd", g_sd, states, lam_row, n_batch, lp, sw)

    def add_store(outs, acc, other):
        outs[0][...] = acc + other

    g_u = _mm_act("s5_g_u", gs, b_blk, "natT", n_out_cols=s5w, k_total=2 * ns, tn=s5w, tk=min(2 * ns, 1024),
                  extras=(g_u_direct,), extra_specs=(rs,), store=add_store)[0]
    d_bblk = _mm_wgrad("dw_bblk", proj_ab, gs, kw=s5w, n=2 * ns, tmw=s5w, tn=min(2 * ns, 2048))
    db_re, db_im = _deinterleave(d_bblk, sw)
    dc_re, dc_im = _deinterleave(d_cblk.T, sw)
    glr, gli = _deinterleave(gl_parts, sw)
    g_lam_re, g_lam_im, g_log_dt, g_b_re_t, g_b_im_t, g_d = _s5_params_bwd(
        "s5_params_bwd", lam_re, lam_im, log_dt, b_re_t, b_im_t,
        glr.reshape(n_batch, groups, n_state), gli.reshape(n_batch, groups, n_state),
        _diag_blocks(db_re, groups), _diag_blocks(db_im, groups), gd_parts)

    cat2 = lambda key: jnp.concatenate([gr[key + "0"], gr[key + "1"]], axis=0)
    small_sent = on_small({
        "s5_lam_re": g_lam_re[None], "s5_lam_im": g_lam_im[None], "s5_log_dt": g_log_dt.reshape(1, groups),
        "s5_b_re": g_b_re_t.transpose(0, 2, 1)[None], "s5_b_im": g_b_im_t.transpose(0, 2, 1)[None],
        "s5_c_re": _diag_blocks(dc_re, groups)[None], "s5_c_im": -_diag_blocks(dc_im, groups)[None],
        "s5_d": g_d.reshape(1, groups, grp), "s5_b_glu": gr["s5_b_glu"], "hgrn_gamma": gr["hgrn_gamma"],
        "ln_mix_g": cat2("ln_mix_g"), "ln_mix_b": cat2("ln_mix_b"), "mlp_b_up": cat2("mlp_b_up"),
        "mlp_b_down": cat2("mlp_b_down"), "ln_mlp_g": cat2("ln_mlp_g"), "ln_mlp_b": cat2("ln_mlp_b"),
    }, loss_tile)

    g_pab = [g_u, g_q, g_k, g_v]
    assert s5w == sbw
    gr["in_ab"] = _mm_wgrad("dw_in_ab", h0, g_pab, kw=d, n=ab_cols, tmw=d, tn=s5w, shard_cols=n_ab, out_dtype=BF16,
                            dep=small_sent)
    g_h0 = _mm_act("g_h0", g_pab, wts["in_ab"], "stkT", n_out_cols=d, k_total=ab_cols, tn=d, tk=shard_tile(s5w, n_ab),
                   extras=(g_r1,), extra_specs=(_row_spec(tm, d),), store=res_store)[0]
    grad_x = g_h0.reshape(n_batch, lp, d)[:, lead:, :]
    g_meta = _meta_grad("g_meta", g_h0, n_batch, lp, pad, n_meta)
    on_grads(3, {"meta": g_meta, "in_ab": gr["in_ab"], "glu": gr["glu"], "out_ab": gr["out_ab"]})
    return grad_x


SMALL_NAMES = ("s5_lam_re", "s5_lam_im", "s5_log_dt", "s5_b_re", "s5_b_im", "s5_c_re", "s5_c_im", "s5_d", "s5_b_glu",
               "hgrn_gamma", "ln_mix_g", "ln_mix_b", "mlp_b_up", "mlp_b_down", "ln_mlp_g", "ln_mlp_b")
WEIGHT_ORDER = ("meta", "w_in_ab", "s5_lam_re", "s5_lam_im", "s5_log_dt", "s5_b_re", "s5_b_im", "s5_c_re", "s5_c_im",
                "s5_d", "s5_w_glu", "s5_b_glu", "w_out_ab", "w_in_c", "hgrn_gamma", "hgrn_norm_g", "w_out_c", "ln_mix_g",
                "ln_mix_b", "mlp_w_up", "mlp_b_up", "mlp_w_down", "mlp_b_down", "ln_mlp_g", "ln_mlp_b")


def kernel(x, meta, w_in_ab, s5_lam_re, s5_lam_im, s5_log_dt, s5_b_re, s5_b_im, s5_c_re, s5_c_im, s5_d, s5_w_glu, s5_b_glu, w_out_ab, w_in_c, hgrn_gamma, hgrn_norm_g, w_out_c, ln_mix_g, ln_mix_b, mlp_w_up, mlp_b_up, mlp_w_down, mlp_b_down, ln_mlp_g, ln_mlp_b, loss_target, m_meta, m_w_in_ab, m_s5_lam_re, m_s5_lam_im, m_s5_log_dt, m_s5_b_re, m_s5_b_im, m_s5_c_re, m_s5_c_im, m_s5_d, m_s5_w_glu, m_s5_b_glu, m_w_out_ab, m_w_in_c, m_hgrn_gamma, m_hgrn_norm_g, m_w_out_c, m_ln_mix_g, m_ln_mix_b, m_mlp_w_up, m_mlp_b_up, m_mlp_w_down, m_mlp_b_down, m_ln_mlp_g, m_ln_mlp_b, v_meta, v_w_in_ab, v_s5_lam_re, v_s5_lam_im, v_s5_log_dt, v_s5_b_re, v_s5_b_im, v_s5_c_re, v_s5_c_im, v_s5_d, v_s5_w_glu, v_s5_b_glu, v_w_out_ab, v_w_in_c, v_hgrn_gamma, v_hgrn_norm_g, v_w_out_c, v_ln_mix_g, v_ln_mix_b, v_mlp_w_up, v_mlp_b_up, v_mlp_w_down, v_mlp_b_down, v_ln_mlp_g, v_ln_mlp_b):
    args = dict(locals())
    w = {n: args[n] for n in WEIGHT_ORDER}
    mom = {n: args["m_" + n] for n in WEIGHT_ORDER}
    var = {n: args["v_" + n] for n in WEIGHT_ORDER}
    d = x.shape[2]
    n_meta = meta.shape[0]

    cast = lambda a: a.astype(BF16)
    early = _exchange_start("gather_early_start", [w["meta"], cast(w["w_in_ab"][0]), cast(w["s5_w_glu"][0]),
                                                   cast(w["w_out_ab"][0])], False)
    late = _exchange_start("gather_late_start", [w["hgrn_norm_g"], cast(w["w_in_c"][0]), cast(w["w_out_c"][0]),
                                                 cast(w["mlp_w_up"][0]), cast(w["mlp_w_up"][1]),
                                                 cast(w["mlp_w_down"][0]), cast(w["mlp_w_down"][1])], False, dep=early["token"])
    a_meta, a_in_ab, a_glu, a_out_ab = _exchange_wait("gather_early_wait", early, late["token"])
    wts = {"in_ab": a_in_ab, "glu": a_glu.reshape(-1, a_glu.shape[2]), "out_ab": a_out_ab.reshape(-1, d)}
    meta_full = a_meta.transpose(1, 0, 2).reshape(n_meta, d)
    small = {n: w[n] for n in SMALL_NAMES}

    def late_weights(after):
        a_ng, a_in_c, a_out_c, a_up0, a_up1, a_dn0, a_dn1 = _exchange_wait("gather_late_wait", late, after)
        return {"in_c": a_in_c, "ng": a_ng.transpose(1, 0, 2).reshape(1, d), "out_c": a_out_c.reshape(-1, d),
                "up": [a_up0, a_up1], "down": [a_dn0.reshape(-1, d), a_dn1.reshape(-1, d)]}

    n_loc = d // N_DEV
    rows_of = lambda g: g.reshape(N_DEV, -1, g.shape[-1])
    cols_of = lambda g: g.reshape(g.shape[0], N_DEV, n_loc).transpose(1, 0, 2)
    sent = {}

    def on_grads(stage, g):
        if stage == 1:
            order = (("mlp_w_down", 1), ("mlp_w_up", 1), ("w_out_c", 0), ("w_in_c", 0), ("hgrn_norm_g", None))
            parts = [rows_of(g["down1"]), g["up1"], rows_of(g["out_c"]), g["in_c"], cols_of(g["ng"])]
        elif stage == 2:
            order = (("mlp_w_down", 0), ("mlp_w_up", 0))
            parts = [rows_of(g["down0"]), g["up0"]]
        else:
            order = (("w_out_ab", 0), ("s5_w_glu", 0), ("w_in_ab", 0), ("meta", None))
            parts = [rows_of(g["out_ab"]), rows_of(g["glu"]), g["in_ab"], cols_of(g["meta"])]
        sent[stage] = (order, _exchange_start(f"scatter_start{stage}", parts, True))
        return sent[stage][1]["token"]

    def on_small(sg, loss_tile):
        g_pack = _pack_rows([sg[n] for n in SMALL_NAMES] + [loss_tile], PACK_COLS)
        sent["small"] = _exchange_start("gather_small_start", [g_pack], False)
        return sent["small"]["token"]

    grad_x = _local_step(x, loss_target, meta_full, wts, small, late_weights, on_grads, on_small)
    small_sent = sent["small"]
    tile = (SUBLANES, LANES)
    shapes = [w[n].shape for n in SMALL_NAMES] + [tile]
    zeros = jnp.zeros(tile, F32)
    w_pack = _pack_rows([w[n] for n in SMALL_NAMES] + [zeros], PACK_COLS)
    m_pack = _pack_rows([mom[n] for n in SMALL_NAMES] + [zeros], PACK_COLS)
    v_pack = _pack_rows([var[n] for n in SMALL_NAMES] + [zeros], PACK_COLS)

    def apply(stage, after):
        order, handle = sent[stage]
        recv = _exchange_wait(f"scatter_wait{stage}", handle, after)
        for (nm, ly), rc in zip(order, recv):
            sel = (lambda t: t) if ly is None else (lambda t, ly=ly: t[ly])
            res[(nm, ly)] = _adamw_summed(f"adamw_{nm}_{ly}", rc, sel(w[nm]), sel(mom[nm]), sel(var[nm]))
        return res[order[0]][0]

    res = {}
    done = apply(2, apply(1, sent[3][1]["token"]))
    g_all = _exchange_wait("gather_small_wait", small_sent, done)[0]
    packed = _adamw_summed("adamw_small", g_all, w_pack, m_pack, v_pack)
    apply(3, packed[0])
    unpacked = [_unpack_rows(p, shapes, PACK_COLS) for p in packed]
    loss = unpacked[0][-1][0, 0]

    def pick(nm, which):
        if nm in SMALL_NAMES:
            return unpacked[which][SMALL_NAMES.index(nm)]
        if (nm, None) in res:
            return res[(nm, None)][which]
        return jnp.stack([res[(nm, ly)][which] for ly in range(w[nm].shape[0])], axis=0)

    return (loss, grad_x, *[pick(n, 0) for n in WEIGHT_ORDER], *[pick(n, 1) for n in WEIGHT_ORDER],
            *[pick(n, 2) for n in WEIGHT_ORDER], *[pick(n, 3) for n in WEIGHT_ORDER])
```

```python
import functools
import math

import jax
import jax.numpy as jnp
from jax import lax
from jax.experimental import pallas as pl
from jax.experimental.pallas import tpu as pltpu

F32 = jnp.float32
BF16 = jnp.bfloat16

N_DEV = 8
DEPTH = 2
ALPHA = (2.0 * DEPTH) ** 0.25
LN_EPS = 1e-5
RMS_EPS = 1e-6
SB_HEAD_DIM = 64
HG_DK = 128
HG_CHUNK = 64
LANES = 128
SUBLANES = 8
VMEM_LIMIT_BYTES = 56 * 1024 * 1024
ROW_TILE = 1088
SCAN_LANES = 256
SCAN_UNROLL = 4
PACK_COLS = 1024

ADAM_LR = 0.001
ADAM_B1 = 0.9
ADAM_B2 = 0.999
ADAM_EPS = 1e-08
ADAM_WD = 0.01
ADAM_STEP = 10

NN = (((1,), (0,)), ((), ()))
NT = (((1,), (1,)), ((), ()))
TN = (((0,), (0,)), ((), ()))


def _tile(n, pref, align=SUBLANES):
    t = min(n, pref)
    t -= t % align
    while t >= align:
        if n % t == 0:
            return t
        t -= align
    return n


def _unrolled_loop(n, body, init, unroll):
    assert n % unroll == 0

    def outer(t, carry):
        for u in range(unroll):
            carry = body(t * unroll + u, carry)
        return carry

    return lax.fori_loop(0, n // unroll, outer, init)


def _params(sem):
    return pltpu.CompilerParams(dimension_semantics=sem, vmem_limit_bytes=VMEM_LIMIT_BYTES)


def _dot_raw(a, b, dims):
    return lax.dot_general(a.astype(BF16), b.astype(BF16), dims, preferred_element_type=F32)


def _make_dot(dims, da_rule, db_rule):
    @jax.custom_vjp
    def f(a, b):
        return _dot_raw(a, b, dims)

    def fwd(a, b):
        return _dot_raw(a, b, dims), (a, b)

    def bwd(res, g):
        a, b = res
        return da_rule(g, a, b), db_rule(g, a, b)

    f.defvjp(fwd, bwd)
    return f


_DOTS = {
    NN: _make_dot(NN, lambda g, a, b: _dot_raw(g, b, NT), lambda g, a, b: _dot_raw(a, g, TN)),
    NT: _make_dot(NT, lambda g, a, b: _dot_raw(g, b, NN), lambda g, a, b: _dot_raw(g, a, TN)),
    TN: _make_dot(TN, lambda g, a, b: _dot_raw(b, g, NT), lambda g, a, b: _dot_raw(a, g, NN)),
}


def _dot(a, b, dims):
    return _DOTS[dims](a, b)


def _running_sums(a, tri_ones, split=False):
    hi = a.astype(BF16)
    out = lax.dot_general(hi, tri_ones, NN, preferred_element_type=F32)
    if split:
        lo = (a - hi.astype(F32)).astype(BF16)
        out = out + lax.dot_general(lo, tri_ones, NN, preferred_element_type=F32)
    return out


def _piece_specs(pieces, block_rows, block_cols, row_of, col_of, cb0):
    per = pieces[0].shape[1] // block_cols if len(pieces) > 1 else None
    specs = []
    for p in range(len(pieces)):
        if per is None:
            specs.append(pl.BlockSpec((block_rows, block_cols), lambda *g: (row_of(*g), cb0 + col_of(*g))))
        else:
            specs.append(pl.BlockSpec(
                (block_rows, block_cols),
                lambda *g, p=p: (row_of(*g), jnp.clip(col_of(*g) - p * per, 0, per - 1))))
    return specs, per


def _mm_call(name, grid, dims, a_pieces, a_specs, a_sel, b_pieces, b_specs, b_sel, extras, extra_specs,
             out_shape, out_specs, acc_shape, a_fn, store, colsum_width=0, sequential=False, deps=()):
    na, nb, ne, no, nd = len(a_pieces), len(b_pieces), len(extras), len(out_shape), len(deps)
    nk = grid[2]

    def body(*refs):
        a_refs, b_refs = refs[:na], refs[na:na + nb]
        extra = refs[na + nb:na + nb + ne]
        outs = refs[na + nb + ne + nd:na + nb + ne + nd + no]
        acc = refs[na + nb + ne + nd + no]
        ids = (pl.program_id(0), pl.program_id(1), pl.program_id(2))
        k = ids[2]

        @pl.when(k == 0)
        def _():
            acc[...] = jnp.zeros_like(acc)

        def run(a_ref, b_ref):
            a = a_ref[...]
            if a_fn is not None:
                a = a_fn(a)
            b = b_ref[...]
            if b.ndim == 3 and dims == NN:
                n = b.shape[2]
                for q in range(b.shape[0]):
                    acc[:, q * n:(q + 1) * n] += _dot_raw(a, b[q], dims)
            elif b.ndim == 3:
                n = b.shape[2]
                for q in range(b.shape[0]):
                    acc[...] += _dot_raw(a[:, q * n:(q + 1) * n], b[q], dims)
            else:
                acc[...] += _dot_raw(a, b, dims)
            if colsum_width:
                cs = refs[-1]
                first = ids[1] == 0

                @pl.when(first & (k == 0))
                def _():
                    cs[...] = jnp.zeros_like(cs)

                @pl.when(first)
                def _():
                    cs[...] += jnp.sum(b.astype(F32), axis=0, keepdims=True)

        if na == 1 and nb == 1:
            run(a_refs[0], b_refs[0])
        elif nb == 1:
            per, fn = a_sel
            which = fn(*ids) // per
            for p in range(na):
                pl.when(which == p)(functools.partial(run, a_refs[p], b_refs[0]))
        else:
            assert na == 1
            per, fn = b_sel
            which = fn(*ids) // per
            for p in range(nb):
                pl.when(which == p)(functools.partial(run, a_refs[0], b_refs[p]))

        @pl.when(k == nk - 1)
        def _():
            if sequential:
                store(outs, acc[...], *[e[...] for e in extra], first_step=(ids[0] == 0) & (ids[1] == 0))
            else:
                store(outs, acc[...], *[e[...] for e in extra])
            if colsum_width:
                @pl.when(ids[1] == 0)
                def _():
                    outs[-1][...] = refs[-1][...]

    scratch = [pltpu.VMEM(acc_shape, F32)]
    if colsum_width:
        scratch.append(pltpu.VMEM((1, colsum_width), F32))
    sem = ("parallel", "arbitrary", "arbitrary") if colsum_width else ("parallel", "parallel", "arbitrary")
    if sequential:
        sem = ("arbitrary",) * 3
    return pl.pallas_call(
        body, name=name, grid=grid,
        in_specs=[*a_specs, *b_specs, *extra_specs, *[pl.BlockSpec(memory_space=pl.ANY)] * nd], out_specs=out_specs,
        out_shape=out_shape, scratch_shapes=scratch, compiler_params=_params(sem),
    )(*a_pieces, *b_pieces, *extras, *deps)


def _store_plain(outs, acc):
    outs[0][...] = acc.astype(outs[0].dtype)


def _row_spec(tm, tn):
    return pl.BlockSpec((tm, tn), lambda i, j, k: (i, j))


def _vec_spec(tn):
    return pl.BlockSpec((1, tn), lambda i, j, k: (0, j))


def _mm_act(name, a, w, wkind, *, n_out_cols, k_total, tn, tk, a_cb0=0, a_fn=None, extras=(), extra_specs=(),
            store=_store_plain, out_shape=None, out_specs=None, sequential=False, dep=None):
    a_pieces = list(a) if isinstance(a, (list, tuple)) else [a]
    rows = a_pieces[0].shape[0]
    tm = _tile(rows, ROW_TILE)
    grid = (rows // tm, n_out_cols // tn, k_total // tk)
    a_specs, per = _piece_specs(a_pieces, tm, tk, lambda i, j, k: i, lambda i, j, k: k, a_cb0)
    if wkind == "nat":
        b_spec, dims = pl.BlockSpec((tk, tn), lambda i, j, k: (k, j)), NN
    elif wkind == "stk":
        n = w.shape[2]
        assert tn % n == 0
        b_spec, dims = pl.BlockSpec((tn // n, tk, n), lambda i, j, k: (j, k, 0)), NN
    elif wkind == "natT":
        b_spec, dims = pl.BlockSpec((tn, tk), lambda i, j, k: (j, k)), NT
    else:
        n = w.shape[2]
        assert wkind == "stkT" and tk % n == 0
        b_spec, dims = pl.BlockSpec((tk // n, tn, n), lambda i, j, k: (k, j, 0)), NT
    if out_shape is None:
        out_shape = [jax.ShapeDtypeStruct((rows, n_out_cols), F32)]
        out_specs = [_row_spec(tm, tn)]
    return _mm_call(name, grid, dims, a_pieces, a_specs, (per, lambda i, j, k: k), [w], [b_spec], None,
                    list(extras), list(extra_specs), out_shape, out_specs, (tm, tn), a_fn, store,
                    sequential=sequential, deps=() if dep is None else (dep,))


def _mm_wgrad(name, a, g, *, kw, n, tmw, tn, a_cb0=0, a_fn=None, shard_cols=0, out_dtype=F32, colsum=False, dep=None):
    a_pieces = list(a) if isinstance(a, (list, tuple)) else [a]
    g_pieces = list(g) if isinstance(g, (list, tuple)) else [g]
    rows = a_pieces[0].shape[0]
    tr = _tile(rows, ROW_TILE)
    grid = (n // tn, kw // tmw, rows // tr)
    a_specs, a_per = _piece_specs(a_pieces, tr, tmw, lambda j, i, k: k, lambda j, i, k: i, a_cb0)
    g_specs, g_per = _piece_specs(g_pieces, tr, tn, lambda j, i, k: k, lambda j, i, k: j, 0)
    if shard_cols:
        per = tn // shard_cols
        out_shape = [jax.ShapeDtypeStruct((n // shard_cols, kw, shard_cols), out_dtype)]
        out_specs = [pl.BlockSpec((per, tmw, shard_cols), lambda j, i, k: (j, i, 0))]

        def store(outs, acc):
            for q in range(per):
                outs[0][q] = acc[:, q * shard_cols:(q + 1) * shard_cols].astype(out_dtype)
    else:
        out_shape = [jax.ShapeDtypeStruct((kw, n), out_dtype)]
        out_specs = [pl.BlockSpec((tmw, tn), lambda j, i, k: (i, j))]

        def store(outs, acc):
            outs[0][...] = acc.astype(out_dtype)
    if colsum:
        out_shape.append(jax.ShapeDtypeStruct((1, n), F32))
        out_specs.append(pl.BlockSpec((1, tn), lambda j, i, k: (0, j)))
    res = _mm_call(name, grid, TN, a_pieces, a_specs, (a_per, lambda j, i, k: i), g_pieces, g_specs,
                   (g_per, lambda j, i, k: j), [], [], out_shape, out_specs, (tmw, tn), a_fn, store,
                   colsum_width=tn if colsum else 0, deps=() if dep is None else (dep,))
    return res if colsum else res[0]


def _ln(x, g, b):
    mu = jnp.mean(x, axis=-1, keepdims=True)
    xc = x - mu
    var = jnp.mean(xc * xc, axis=-1, keepdims=True)
    return xc * lax.rsqrt(var + LN_EPS) * g + b


def _relu2(x):
    r = jnp.maximum(x.astype(F32), 0.0)
    return r * r


def _glu(y, gate):
    return y * jax.nn.sigmoid(gate)


def _rowwise(name, fn, ins, n_out, width):
    rows = ins[0][0].shape[0]
    tm = _tile(rows, ROW_TILE)

    def body(*refs):
        res = fn(*[r[...] for r in refs[:len(ins)]])
        for o, v in zip(refs[len(ins):], res):
            o[...] = v

    return pl.pallas_call(
        body, name=name, grid=(rows // tm,),
        in_specs=[pl.BlockSpec((tm, wd), lambda i, cb=cb: (i, cb)) for _, cb, wd in ins],
        out_specs=[pl.BlockSpec((tm, width), lambda i: (i, 0))] * n_out,
        out_shape=[jax.ShapeDtypeStruct((rows, width), F32)] * n_out, compiler_params=_params(("parallel",)),
    )(*[a for a, _, _ in ins])


def _loss_grad(name, r, g, b, target, n_batch, lp, lead):
    rows, d = r.shape
    nq = lp // LANES
    lead_blocks = lead // LANES

    def body(r_ref, g_ref, b_ref, t_ref, gr_ref, gg_ref, gb_ref, loss_ref):
        i = pl.program_id(1)

        @pl.when((pl.program_id(0) == 0) & (i == 0))
        def _():
            loss_ref[...] = jnp.zeros_like(loss_ref)
            gg_ref[...] = jnp.zeros_like(gg_ref)
            gb_ref[...] = jnp.zeros_like(gb_ref)

        h, vjp = jax.vjp(_ln, r_ref[...], g_ref[...], b_ref[...])
        diff = jnp.where(i >= lead_blocks, h - t_ref[...], 0.0)
        gr, gg, gb = vjp(diff * (1.0 / d))
        gr_ref[...] = gr
        gg_ref[...] += gg
        gb_ref[...] += gb
        loss_ref[...] += 0.5 * jnp.sum(diff * diff) * (1.0 / d)

    vec = pl.BlockSpec((1, d), lambda b, i: (0, 0))
    row = pl.BlockSpec((LANES, d), lambda b, i: (b * nq + i, 0))
    return pl.pallas_call(
        body, name=name, grid=(n_batch, nq),
        in_specs=[row, vec, vec, pl.BlockSpec((None, LANES, d), lambda b, i: (b, jnp.maximum(i - lead_blocks, 0), 0))],
        out_specs=[row, vec, vec, pl.BlockSpec((SUBLANES, LANES), lambda b, i: (0, 0))],
        out_shape=[jax.ShapeDtypeStruct((rows, d), F32), jax.ShapeDtypeStruct((1, d), F32),
                   jax.ShapeDtypeStruct((1, d), F32), jax.ShapeDtypeStruct((SUBLANES, LANES), F32)],
        compiler_params=_params(("arbitrary", "arbitrary")),
    )(r, g, b, target)


def _meta_grad(name, g_h0, n_batch, lp, pad, n_meta):
    d = g_h0.shape[1]
    per = lp // n_meta
    at = pad // n_meta

    def body(g_ref, o_ref):
        @pl.when(pl.program_id(0) == 0)
        def _():
            o_ref[...] = jnp.zeros_like(o_ref)

        o_ref[...] += g_ref[...]

    return pl.pallas_call(
        body, name=name, grid=(n_batch,),
        in_specs=[pl.BlockSpec((n_meta, d), lambda b: (b * per + at, 0))],
        out_specs=pl.BlockSpec((n_meta, d), lambda b: (0, 0)),
        out_shape=jax.ShapeDtypeStruct((n_meta, d), F32),
        compiler_params=_params(("arbitrary",)),
    )(g_h0)


def _s5_param_fn(lr, li, ldt, br, bi):
    dt = jnp.exp(ldt)
    e = jnp.exp(lr * dt)
    w = li * dt
    lbr = e * jnp.cos(w)
    lbi = e * jnp.sin(w)
    nr = lbr - 1.0
    den = lr * lr + li * li
    cr = (nr * lr + lbi * li) / den
    ci = (lbi * lr - nr * li) / den
    bbr = cr[:, None, :] * br - ci[:, None, :] * bi
    bbi = cr[:, None, :] * bi + ci[:, None, :] * br
    return lbr, lbi, bbr, bbi


def _s5_params(name, lr, li, ldt, br, bi):
    def body(lr_ref, li_ref, ldt_ref, br_ref, bi_ref, o1, o2, o3, o4):
        res = _s5_param_fn(lr_ref[...], li_ref[...], ldt_ref[...], br_ref[...], bi_ref[...])
        for o, v in zip((o1, o2, o3, o4), res):
            o[...] = v

    shp = [jax.ShapeDtypeStruct(lr.shape, F32)] * 2 + [jax.ShapeDtypeStruct(br.shape, F32)] * 2
    return pl.pallas_call(body, name=name, out_shape=shp)(lr, li, ldt, br, bi)


def _s5_params_bwd(name, lr, li, ldt, br, bi, g_lbr, g_lbi, g_bbr, g_bbi, gd_parts):
    def body(lr_ref, li_ref, ldt_ref, br_ref, bi_ref, g1, g2, g3, g4, gd_ref, o1, o2, o3, o4, o5, o6):
        _, vjp = jax.vjp(_s5_param_fn, lr_ref[...], li_ref[...], ldt_ref[...], br_ref[...], bi_ref[...])
        res = vjp((jnp.sum(g1[...], axis=0), jnp.sum(g2[...], axis=0), g3[...], g4[...]))
        for o, v in zip((o1, o2, o3, o4, o5), res):
            o[...] = v
        o6[...] = jnp.sum(gd_ref[...], axis=0)

    shp = ([jax.ShapeDtypeStruct(lr.shape, F32)] * 2 + [jax.ShapeDtypeStruct(ldt.shape, F32)]
           + [jax.ShapeDtypeStruct(br.shape, F32)] * 2 + [jax.ShapeDtypeStruct(gd_parts.shape[1:], F32)])
    return pl.pallas_call(body, name=name, out_shape=shp)(lr, li, ldt, br, bi, g_lbr, g_lbi, g_bbr, g_bbi, gd_parts)


def _interleave(re, im, w):
    nj = re.shape[-1] // w
    return jnp.concatenate([x[..., j * w:(j + 1) * w] for j in range(nj) for x in (re, im)], axis=-1)


def _deinterleave(x, w):
    nj = x.shape[-1] // (2 * w)
    return (jnp.concatenate([x[..., 2 * j * w:(2 * j + 1) * w] for j in range(nj)], axis=-1),
            jnp.concatenate([x[..., (2 * j + 1) * w:(2 * j + 2) * w] for j in range(nj)], axis=-1))


def _cmul(ar, ai, br, bi):
    return ar * br - ai * bi, ar * bi + ai * br


def _powers(lr, li):
    p = [(lr, li)]
    p.append(_cmul(*p[0], *p[0]))
    p.append(_cmul(*p[1], *p[0]))
    p.append(_cmul(*p[1], *p[1]))
    p.append(_cmul(*p[3], *p[0]))
    p.append(_cmul(*p[3], *p[1]))
    p.append(_cmul(*p[3], *p[2]))
    p.append(_cmul(*p[3], *p[3]))
    return p


def _scan_tile(xr, xi, steps):
    for sh, br, bi, m in steps:
        rr = jnp.where(m, pltpu.roll(xr, sh, 0), 0.0)
        ri = jnp.where(m, pltpu.roll(xi, sh, 0), 0.0)
        xr, xi = xr + (br * rr - bi * ri), xi + (br * ri + bi * rr)
    return xr, xi


def _s5_scan(name, bu, lam, n_batch, lp, w):
    rows, two_ns = bu.shape
    nj = two_ns // (2 * w)
    nt = lp // SUBLANES

    def body(x_ref, lam_ref, s_ref):
        pw = _powers(lam_ref[:, :w], lam_ref[:, w:])
        tab_r = jnp.concatenate([p[0] for p in pw], axis=0)
        tab_i = jnp.concatenate([p[1] for p in pw], axis=0)
        row = lax.broadcasted_iota(jnp.int32, (SUBLANES, w), 0)
        steps = [(s, jnp.broadcast_to(pw[s - 1][0], (SUBLANES, w)), jnp.broadcast_to(pw[s - 1][1], (SUBLANES, w)),
                  row >= s) for s in (1, 2, 4)]

        def tile(t, carry):
            cr, ci = carry
            r0 = pl.multiple_of(t * SUBLANES, SUBLANES)
            x = x_ref[pl.ds(r0, SUBLANES), :]
            xr, xi = _scan_tile(x[:, :w], x[:, w:], steps)
            sr = xr + (tab_r * cr - tab_i * ci)
            si = xi + (tab_r * ci + tab_i * cr)
            s_ref[pl.ds(r0, SUBLANES), :] = jnp.concatenate([sr, si], axis=1)
            return sr[SUBLANES - 1:, :], si[SUBLANES - 1:, :]

        zero = jnp.zeros((1, w), F32)
        _unrolled_loop(nt, tile, (zero, zero), SCAN_UNROLL)

    spec = pl.BlockSpec((lp, 2 * w), lambda b, j: (b, j))
    return pl.pallas_call(
        body, name=name, grid=(n_batch, nj), in_specs=[spec, pl.BlockSpec((1, 2 * w), lambda b, j: (0, j))],
        out_specs=spec, out_shape=jax.ShapeDtypeStruct((rows, two_ns), F32),
        compiler_params=_params(("parallel", "parallel")),
    )(bu, lam)


def _s5_scan_bwd(name, gd, states, lam, n_batch, lp, w):
    rows, two_ns = gd.shape
    nj = two_ns // (2 * w)
    nt = lp // SUBLANES

    def body(x_ref, s_ref, lam_ref, g_ref, gl_ref):
        pw = _powers(lam_ref[:, :w], -lam_ref[:, w:])
        tab_r = jnp.concatenate([p[0] for p in reversed(pw)], axis=0)
        tab_i = jnp.concatenate([p[1] for p in reversed(pw)], axis=0)
        row = lax.broadcasted_iota(jnp.int32, (SUBLANES, w), 0)
        steps = [(SUBLANES - s, jnp.broadcast_to(pw[s - 1][0], (SUBLANES, w)),
                  jnp.broadcast_to(pw[s - 1][1], (SUBLANES, w)), row < SUBLANES - s) for s in (1, 2, 4)]

        def tile(u, carry):
            cr, ci, ar, ai = carry
            t = nt - 1 - u
            r0 = pl.multiple_of(t * SUBLANES, SUBLANES)
            x = x_ref[pl.ds(r0, SUBLANES), :]
            xr, xi = _scan_tile(x[:, :w], x[:, w:], steps)
            gr = xr + (tab_r * cr - tab_i * ci)
            gi = xi + (tab_r * ci + tab_i * cr)
            g_ref[pl.ds(r0, SUBLANES), :] = jnp.concatenate([gr, gi], axis=1)
            p0 = pl.multiple_of(jnp.maximum(t - 1, 0) * SUBLANES, SUBLANES)
            prev = s_ref[pl.ds(p0, SUBLANES), :][SUBLANES - 1:, :] * jnp.where(t > 0, 1.0, 0.0)
            cur = s_ref[pl.ds(r0, SUBLANES), :]
            spr = jnp.where(row >= 1, pltpu.roll(cur[:, :w], 1, 0), prev[:, :w])
            spi = jnp.where(row >= 1, pltpu.roll(cur[:, w:], 1, 0), prev[:, w:])
            return gr[:1, :], gi[:1, :], ar + gr * spr + gi * spi, ai + gi * spr - gr * spi

        z1 = jnp.zeros((1, w), F32)
        z8 = jnp.zeros((SUBLANES, w), F32)
        _, _, ar, ai = _unrolled_loop(nt, tile, (z1, z1, z8, z8), SCAN_UNROLL)
        gl_ref[...] = jnp.concatenate([jnp.sum(ar, axis=0, keepdims=True), jnp.sum(ai, axis=0, keepdims=True)], axis=1)

    spec = pl.BlockSpec((lp, 2 * w), lambda b, j: (b, j))
    return pl.pallas_call(
        body, name=name, grid=(n_batch, nj),
        in_specs=[spec, spec, pl.BlockSpec((1, 2 * w), lambda b, j: (0, j))],
        out_specs=[spec, pl.BlockSpec((None, 1, 2 * w), lambda b, j: (b, 0, j))],
        out_shape=[jax.ShapeDtypeStruct((rows, two_ns), F32), jax.ShapeDtypeStruct((n_batch, 1, two_ns), F32)],
        compiler_params=_params(("parallel", "parallel")),
    )(gd, states, lam)


def _log_sigmoid(z):
    return jnp.minimum(z, 0.0) - jnp.log(1.0 + jnp.exp(-jnp.abs(z)))


ATTN_KEYS = 256
ATTN_GROUP = 4


def _attn_block(i, jb, lp, pad):
    start = jb * ATTN_KEYS
    r0 = pl.multiple_of(jnp.minimum(start, lp - ATTN_KEYS), LANES)
    rowpos = i * LANES + lax.broadcasted_iota(jnp.int32, (LANES, ATTN_KEYS), 0)
    keypos = r0 + lax.broadcasted_iota(jnp.int32, (LANES, ATTN_KEYS), 1)
    return r0, (keypos < rowpos) & (keypos >= jnp.maximum(start, pad))


def _tri_ones(strict_upper):
    r = lax.broadcasted_iota(jnp.int32, (ATTN_KEYS, ATTN_KEYS + LANES), 0)
    c = lax.broadcasted_iota(jnp.int32, (ATTN_KEYS, ATTN_KEYS + LANES), 1)
    tri = (r > c) if strict_upper else (r < c)
    return jnp.where((c >= ATTN_KEYS) | tri, 1.0, 0.0).astype(BF16)


def _split_sums(cr):
    rs = cr[:, ATTN_KEYS:]
    return cr[:, :ATTN_KEYS], jnp.concatenate([rs] * (ATTN_KEYS // LANES), axis=1)


def _head_masks():
    lane = lax.broadcasted_iota(jnp.int32, (1, LANES), 1)
    return [lane < SB_HEAD_DIM, lane >= SB_HEAD_DIM]


def _run_groups(n, first, sign, make):
    j, left, g = first, n, ATTN_GROUP
    while g >= 1:
        shift = g.bit_length() - 1
        count = lax.shift_right_logical(left, shift)
        fn = make(g)

        def loop(_, jcur, fn=fn, g=g):
            fn(jcur)
            return jcur + sign * g

        j = lax.fori_loop(0, count, loop, j)
        left = left - lax.shift_left(count, shift)
        g //= 2


def _attn_fwd(name, proj, n_batch, lp, pad, q_cb, k_cb, v_cb, n_pairs):
    rows = proj.shape[0]
    nq = lp // LANES
    scale = SB_HEAD_DIM ** -0.5

    def body(q_ref, k_ref, v_ref, o_ref, acc_s):
        i = pl.program_id(1)
        hm = _head_masks()
        comb = _tri_ones(True)
        n_blocks = lax.shift_right_logical(i + ATTN_KEYS // LANES, (ATTN_KEYS // LANES).bit_length() - 1)

        def pair(hp, carry):
            lanes = pl.ds(pl.multiple_of(hp * LANES, LANES), LANES)
            qs = q_ref[:, lanes] * scale
            qh = [jnp.where(m, qs, 0.0).astype(BF16) for m in hm]
            acc_s[...] = jnp.zeros_like(acc_s)
            o_ref[:, lanes] = jnp.zeros((LANES, LANES), F32)

            def make(group):
                def fn(jtop):
                    chains = []
                    for g in range(group):
                        r0, vis = _attn_block(i, jtop - g, lp, pad)
                        kj = k_ref[pl.ds(r0, ATTN_KEYS), lanes].astype(BF16)
                        vj = v_ref[pl.ds(r0, ATTN_KEYS), lanes]
                        for h in range(2):
                            z = lax.dot_general(qh[h], kj, NT, preferred_element_type=F32)
                            chains.append((h, vis, z, jnp.where(hm[h], vj, 0.0).astype(BF16)))
                    staged = []
                    for h, vis, z, vh in chains:
                        lsz = _log_sigmoid(z)
                        staged.append((h, vis, lsz, _running_sums(jnp.where(vis, lsz - z, 0.0), comb, split=True), vh))
                    out = o_ref[:, lanes]
                    for h, vis, lsz, cr, vh in staged:
                        later, rs = _split_sums(cr)
                        acc = acc_s[h]
                        wgt = jnp.where(vis, jnp.exp(lsz + later + acc), 0.0)
                        acc_s[h] = acc + rs
                        out = out + lax.dot_general(wgt.astype(BF16), vh, NN, preferred_element_type=F32)
                    o_ref[:, lanes] = out
                return fn

            _run_groups(n_blocks, n_blocks - 1, -1, make)
            return carry

        lax.fori_loop(0, n_pairs, pair, 0)

    wide = n_pairs * LANES
    assert q_cb % n_pairs == 0 and k_cb % n_pairs == 0 and v_cb % n_pairs == 0
    return pl.pallas_call(
        body, name=name, grid=(n_batch, nq),
        in_specs=[pl.BlockSpec((LANES, wide), lambda b, i: (b * nq + i, q_cb // n_pairs)),
                  pl.BlockSpec((lp, wide), lambda b, i: (b, k_cb // n_pairs)),
                  pl.BlockSpec((lp, wide), lambda b, i: (b, v_cb // n_pairs))],
        out_specs=pl.BlockSpec((LANES, wide), lambda b, i: (b * nq + i, 0)),
        out_shape=jax.ShapeDtypeStruct((rows, wide), F32),
        scratch_shapes=[pltpu.VMEM((2, LANES, ATTN_KEYS), F32)],
        compiler_params=_params(("parallel", "arbitrary")),
    )(proj, proj, proj)


def _attn_bwd(name, proj, g_out, n_batch, lp, pad, q_cb, k_cb, v_cb, go_cb, n_pairs):
    rows = proj.shape[0]
    nq = lp // LANES
    scale = SB_HEAD_DIM ** -0.5

    def body(q_ref, k_ref, v_ref, go_ref, gq_ref, gk_ref, gv_ref, ga_s, sz_s, acc_s):
        i = pl.program_id(1)

        @pl.when(i == 0)
        def _():
            gk_ref[...] = jnp.zeros_like(gk_ref)
            gv_ref[...] = jnp.zeros_like(gv_ref)

        hm = _head_masks()
        comb_up = _tri_ones(True)
        comb_lo = _tri_ones(False)
        n_blocks = lax.shift_right_logical(i + ATTN_KEYS // LANES, (ATTN_KEYS // LANES).bit_length() - 1)

        def pair(hp, carry):
            lanes = pl.ds(pl.multiple_of(hp * LANES, LANES), LANES)
            qs = q_ref[:, lanes] * scale
            go = go_ref[:, lanes]
            qh = [jnp.where(m, qs, 0.0).astype(BF16) for m in hm]
            goh = [jnp.where(m, go, 0.0).astype(BF16) for m in hm]
            acc_s[...] = jnp.zeros_like(acc_s)

            def make_down(group):
                def fn(jtop):
                    chains = []
                    for g in range(group):
                        j = jtop - g
                        r0, vis = _attn_block(i, j, lp, pad)
                        kj = k_ref[pl.ds(r0, ATTN_KEYS), lanes].astype(BF16)
                        vj = v_ref[pl.ds(r0, ATTN_KEYS), lanes].astype(BF16)
                        for h in range(2):
                            z = lax.dot_general(qh[h], kj, NT, preferred_element_type=F32)
                            gw = lax.dot_general(goh[h], vj, NT, preferred_element_type=F32)
                            chains.append((h, j, r0, vis, z, gw))
                    staged = []
                    for h, j, r0, vis, z, gw in chains:
                        lsz = _log_sigmoid(z)
                        staged.append((h, j, r0, vis, lsz, _running_sums(jnp.where(vis, lsz - z, 0.0), comb_up), gw))
                    for h, j, r0, vis, lsz, cr, gw in staged:
                        later, rs = _split_sums(cr)
                        acc = acc_s[h]
                        wgt = jnp.where(vis, jnp.exp(lsz + later + acc), 0.0)
                        acc_s[h] = acc + rs
                        ga_s[h, j] = gw * wgt
                        sz_s[h, j] = jnp.exp(lsz)
                        gv_ref[pl.ds(r0, ATTN_KEYS), lanes] += lax.dot_general(
                            wgt.astype(BF16), goh[h], TN, preferred_element_type=F32)
                return fn

            _run_groups(n_blocks, n_blocks - 1, -1, make_down)
            acc_s[...] = jnp.zeros_like(acc_s)

            def make_up(group):
                def fn(jbot):
                    pend = []
                    for g in range(group):
                        j = jbot + g
                        r0, vis = _attn_block(i, j, lp, pad)
                        kj = k_ref[pl.ds(r0, ATTN_KEYS), lanes]
                        for h in range(2):
                            ga = ga_s[h, j]
                            pend.append((h, j, r0, vis, ga, _running_sums(ga, comb_lo),
                                         jnp.where(hm[h], kj, 0.0).astype(BF16)))
                    gq = jnp.zeros((LANES, LANES), F32)
                    for h, j, r0, vis, ga, cr, kh in pend:
                        before, rs = _split_sums(cr)
                        pre = acc_s[h]
                        glk = before + pre
                        acc_s[h] = pre + rs
                        sz = sz_s[h, j]
                        gz = jnp.where(vis, ga * (1.0 - sz) - glk * sz, 0.0).astype(BF16)
                        gq = gq + lax.dot_general(gz, kh, NN, preferred_element_type=F32)
                        gk_ref[pl.ds(r0, ATTN_KEYS), lanes] += lax.dot_general(gz, qh[h], TN, preferred_element_type=F32)
                    gq_ref[:, lanes] += gq * scale
                return fn

            gq_ref[:, lanes] = jnp.zeros((LANES, LANES), F32)
            _run_groups(n_blocks, 0, 1, make_up)
            return carry

        lax.fori_loop(0, n_pairs, pair, 0)

    wide = n_pairs * LANES
    assert q_cb % n_pairs == 0 and k_cb % n_pairs == 0 and v_cb % n_pairs == 0 and go_cb % n_pairs == 0
    blk = lambda cb: pl.BlockSpec((LANES, wide), lambda b, i: (b * nq + i, cb // n_pairs))
    full = lambda cb: pl.BlockSpec((lp, wide), lambda b, i: (b, cb // n_pairs))
    shp = jax.ShapeDtypeStruct((rows, wide), F32)
    per_block = pltpu.VMEM((2, -(-lp // ATTN_KEYS), LANES, ATTN_KEYS), F32)
    return pl.pallas_call(
        body, name=name, grid=(n_batch, nq),
        in_specs=[blk(q_cb), full(k_cb), full(v_cb), blk(go_cb)],
        out_specs=[blk(0), full(0), full(0)], out_shape=[shp, shp, shp],
        scratch_shapes=[per_block, per_block, pltpu.VMEM((2, LANES, ATTN_KEYS), F32)],
        compiler_params=_params(("parallel", "arbitrary")),
    )(proj, proj, proj, g_out)


def _lb_fn(gamma):
    g0, g1 = gamma[0:1, :], gamma[1:2, :]
    mx = jnp.maximum(g0, g1)
    e0, e1 = jnp.exp(g0 - mx), jnp.exp(g1 - mx)
    p0, p1 = e0 / (e0 + e1), e1 / (e0 + e1)
    return (p0 + p1) - p0


def _lower_bound(name, gamma):
    def body(g_ref, o_ref):
        o_ref[...] = _lb_fn(g_ref[...])

    return pl.pallas_call(body, name=name, out_shape=jax.ShapeDtypeStruct((1, gamma.shape[1]), F32))(gamma)


def _lower_bound_bwd(name, gamma, g_lb_parts, g_ng_parts):
    def body(g_ref, glb_ref, gng_ref, o_ref, o2_ref):
        _, vjp = jax.vjp(_lb_fn, g_ref[...])
        o_ref[...] = vjp(jnp.sum(glb_ref[...], axis=0))[0]
        o2_ref[...] = jnp.sum(gng_ref[...], axis=0)

    return pl.pallas_call(
        body, name=name,
        out_shape=[jax.ShapeDtypeStruct(gamma.shape, F32), jax.ShapeDtypeStruct((1, gamma.shape[1]), F32)],
    )(gamma, g_lb_parts, g_ng_parts)


def _tri_times(tril, x, dims):
    hi = x.astype(BF16)
    lo = (x - hi.astype(F32)).astype(BF16)
    t = tril.astype(BF16)
    return (lax.dot_general(t, hi, dims, preferred_element_type=F32)
            + lax.dot_general(t, lo, dims, preferred_element_type=F32))


@jax.custom_vjp
def _cumsum_rows(x, tril):
    return _tri_times(tril, x, NN)


def _cumsum_rows_fwd(x, tril):
    return _tri_times(tril, x, NN), tril


def _cumsum_rows_bwd(tril, g):
    return _tri_times(tril, g, TN), jnp.zeros_like(tril)


_cumsum_rows.defvjp(_cumsum_rows_fwd, _cumsum_rows_bwd)


def _hg_decays(f_pre, lbs, masks, tril):
    f = [[lb + (1.0 - lb) * jax.nn.sigmoid(fc) for fc, lb in zip(row, lbs)] for row in f_pre]
    bcum = [[_cumsum_rows(jnp.log(x) * m, tril) for x in row] for row, m in zip(f, masks)]
    return [[1.0 - x for x in row] for row in f], bcum


def _hg_step(q, f_pre, i_in, g, lbs, ngs, sts, masks, tril):
    k, bcum = _hg_decays(f_pre, lbs, masks, tril)
    v = [[ic * m for ic in row] for row, m in zip(i_in, masks)]
    qd = [[qc * jnp.exp(b) for qc, b in zip(qr, br)] for qr, br in zip(q, bcum)]
    scores = [[jnp.where(tril > 0.5, _dot(a, kk * jnp.exp(-b), NT), 0.0) for a, kk, b in zip(ar, kr, br)]
              for ar, kr, br in zip(qd, k, bcum)]
    inner = [[_dot(s, x, NN) for s, x in zip(sr, vr)] for sr, vr in zip(scores, v)]
    add = [[_dot(x, kk * jnp.exp(b[HG_CHUNK - 1:, :] - b), TN) for x, kk, b in zip(vr, kr, br)]
           for vr, kr, br in zip(v, k, bcum)]
    outs = []
    for qr, br, nr, ar, gr in zip(qd, bcum, inner, add, g):
        o = [n + _dot(a, st, NT) for n, a, st in zip(nr, qr, sts)]
        sts = [jnp.exp(b[HG_CHUNK - 1:, :]) * st + a for b, a, st in zip(br, ar, sts)]
        o = [x * lax.rsqrt(jnp.mean(x * x, axis=-1, keepdims=True) + RMS_EPS) * ng for x, ng in zip(o, ngs)]
        outs.append([x * (gc * jax.nn.sigmoid(gc)) for x, gc in zip(o, gr)])
    return outs, sts


def _hg_consts(c, pad):
    r = lax.broadcasted_iota(jnp.int32, (HG_CHUNK, HG_CHUNK), 0)
    cc = lax.broadcasted_iota(jnp.int32, (HG_CHUNK, HG_CHUNK), 1)
    tril = jnp.where(r >= cc, 1.0, 0.0).astype(F32)
    pos = c * HG_CHUNK + lax.broadcasted_iota(jnp.int32, (HG_CHUNK, 1), 0)
    return tril, jnp.where(pos >= pad, 1.0, 0.0).astype(F32)


HG_HEADS_PER_STEP = 4
HG_CHUNKS_PER_STEP = 2


def _hg_layout(lp, n_heads):
    step_rows = HG_CHUNKS_PER_STEP * HG_CHUNK
    per = min(HG_HEADS_PER_STEP, n_heads)
    assert lp % step_rows == 0 and n_heads % per == 0
    heads = [(h, slice(h * HG_DK, (h + 1) * HG_DK)) for h in range(per)]
    return n_heads // per, lp // step_rows, step_rows, per * HG_DK, heads


def _hg_step_views(step, pad, heads):
    slices = [slice(u * HG_CHUNK, (u + 1) * HG_CHUNK) for u in range(HG_CHUNKS_PER_STEP)]
    consts = [_hg_consts(step * HG_CHUNKS_PER_STEP + u, pad) for u in range(HG_CHUNKS_PER_STEP)]
    load = lambda ref: [[ref[sl, cols] for _, cols in heads] for sl in slices]
    return slices, [m for _, m in consts], consts[0][0], load


def _hgrn_fwd(name, proj, lb, ng, n_batch, lp, pad, n_heads):
    rows = proj.shape[0]
    groups, steps, step_rows, wide, heads = _hg_layout(lp, n_heads)

    def body(q_ref, f_ref, i_ref, g_ref, lb_ref, ng_ref, o_ref, s_ref, st_s):
        t = pl.program_id(2)

        @pl.when(t == 0)
        def _():
            st_s[...] = jnp.zeros_like(st_s)

        slices, masks, tril, load = _hg_step_views(t, pad, heads)
        sts = [st_s[h] for h, _ in heads]
        for (_, cols), st in zip(heads, sts):
            s_ref[:, cols] = st
        outs, sts = _hg_step(load(q_ref), load(f_ref), load(i_ref), load(g_ref),
                             [lb_ref[:, cols] for _, cols in heads], [ng_ref[:, cols] for _, cols in heads],
                             sts, masks, tril)
        for sl, row in zip(slices, outs):
            for (_, cols), o in zip(heads, row):
                o_ref[sl, cols] = o
        for (h, _), st in zip(heads, sts):
            st_s[h] = st

    col = lambda off: pl.BlockSpec((step_rows, wide), lambda b, h, t: (b * steps + t, off * groups + h))
    vec = pl.BlockSpec((1, wide), lambda b, h, t: (0, h))
    return pl.pallas_call(
        body, name=name, grid=(n_batch, groups, steps), in_specs=[col(0), col(1), col(2), col(3), vec, vec],
        out_specs=[col(0), pl.BlockSpec((HG_DK, wide), lambda b, h, t: (b * steps + t, h))],
        out_shape=[jax.ShapeDtypeStruct((rows, n_heads * HG_DK), F32),
                   jax.ShapeDtypeStruct((n_batch * steps * HG_DK, n_heads * HG_DK), F32)],
        scratch_shapes=[pltpu.VMEM((len(heads), HG_DK, HG_DK), F32)],
        compiler_params=_params(("parallel", "parallel", "arbitrary")),
    )(proj, proj, proj, proj, lb, ng)


def _hgrn_bwd(name, proj, lb, ng, g_out, states, n_batch, lp, pad, n_heads):
    rows = proj.shape[0]
    width = n_heads * HG_DK
    groups, steps, step_rows, wide, heads = _hg_layout(lp, n_heads)

    def body(q_ref, f_ref, i_ref, g_ref, lb_ref, ng_ref, go_ref, s_ref, gq_ref, gf_ref, gi_ref, gg_ref, glb_ref, gng_ref, gst_s):
        t = pl.program_id(2)

        @pl.when(t == 0)
        def _():
            gst_s[...] = jnp.zeros_like(gst_s)
            glb_ref[...] = jnp.zeros_like(glb_ref)
            gng_ref[...] = jnp.zeros_like(gng_ref)

        slices, masks, tril, load = _hg_step_views(steps - 1 - t, pad, heads)
        fn = functools.partial(_hg_step, masks=masks, tril=tril)
        _, vjp = jax.vjp(fn, load(q_ref), load(f_ref), load(i_ref), load(g_ref),
                         [lb_ref[:, cols] for _, cols in heads], [ng_ref[:, cols] for _, cols in heads],
                         [s_ref[:, cols] for _, cols in heads])
        gq, gf, gi, gg, glb, gng, gst = vjp((load(go_ref), [gst_s[h] for h, _ in heads]))
        for ref, grads in ((gq_ref, gq), (gf_ref, gf), (gi_ref, gi), (gg_ref, gg)):
            for sl, row in zip(slices, grads):
                for (_, cols), x in zip(heads, row):
                    ref[sl, cols] = x.astype(BF16)
        for (h, cols), a, b, c in zip(heads, gst, glb, gng):
            gst_s[h] = a
            glb_ref[:, cols] += b
            gng_ref[:, cols] += c

    col = lambda off: pl.BlockSpec((step_rows, wide), lambda b, h, t: (b * steps + steps - 1 - t, off * groups + h))
    vec = pl.BlockSpec((1, wide), lambda b, h, t: (0, h))
    part = pl.BlockSpec((None, 1, wide), lambda b, h, t: (b, 0, h))
    big = jax.ShapeDtypeStruct((rows, width), BF16)
    small = jax.ShapeDtypeStruct((n_batch, 1, width), F32)
    return pl.pallas_call(
        body, name=name, grid=(n_batch, groups, steps),
        in_specs=[col(0), col(1), col(2), col(3), vec, vec, col(0),
                  pl.BlockSpec((HG_DK, wide), lambda b, h, t: (b * steps + steps - 1 - t, h))],
        out_specs=[col(0), col(0), col(0), col(0), part, part],
        out_shape=[big, big, big, big, small, small],
        scratch_shapes=[pltpu.VMEM((len(heads), HG_DK, HG_DK), F32)],
        compiler_params=_params(("parallel", "parallel", "arbitrary")),
    )(proj, proj, proj, proj, lb, ng, g_out, states)


def _exchange_copies(src, dst, send, recv, loc, scatter):
    x, y, c = lax.axis_index("x"), lax.axis_index("y"), lax.axis_index("c")
    me = 4 * x + 2 * y + c
    local, remote = [], []
    for w in range(len(src)):
        local.append(pltpu.make_async_copy(src[w].at[me] if scatter else src[w], dst[w].at[me], loc.at[w]))
    for k in range(1, N_DEV):
        px = 1 - x if k & 4 else x
        py = 1 - y if k & 2 else y
        pc = 1 - c if k & 1 else c
        peer = 4 * px + 2 * py + pc
        for w in range(len(src)):
            remote.append(pltpu.make_async_remote_copy(
                src_ref=src[w].at[peer] if scatter else src[w], dst_ref=dst[w].at[me],
                send_sem=send.at[w * (N_DEV - 1) + k - 1], recv_sem=recv.at[w * (N_DEV - 1) + k - 1],
                device_id=(px, py, pc), device_id_type=pl.DeviceIdType.MESH))
    return local, remote


_HBM_SPEC = pl.BlockSpec(memory_space=pltpu.HBM)
_SEM_SPEC = pl.BlockSpec(memory_space=pltpu.SEMAPHORE)
_ANY_SPEC = pl.BlockSpec(memory_space=pl.ANY)
_DATAFLOW = pltpu.SideEffectType.DATAFLOW_SIDE_EFFECTING


def _exchange_start(name, srcs, scatter, dep=None):
    nw = len(srcs)
    srcs = [pltpu.with_memory_space_constraint(s, pltpu.HBM) for s in srcs]
    lands = [pltpu.with_memory_space_constraint(lax.empty(s.shape if scatter else (N_DEV,) + s.shape, s.dtype), pltpu.HBM)
             for s in srcs]
    deps = [] if dep is None else [dep]

    def body(*refs):
        src, dst = refs[:nw], refs[nw:2 * nw]
        send, recv, loc = refs[2 * nw + len(deps):2 * nw + len(deps) + 3]
        token = refs[-1]
        local, remote = _exchange_copies(src, dst, send, recv, loc, scatter)
        for cp in local + remote:
            cp.start()
        token[...] = jnp.zeros_like(token)

    sems = [pltpu.SemaphoreType.DMA((nw * (N_DEV - 1),)), pltpu.SemaphoreType.DMA((nw * (N_DEV - 1),)),
            pltpu.SemaphoreType.DMA((nw,))]
    out = pl.pallas_call(
        body, name=name,
        out_shape=(*sems, *[pltpu.HBM(s.shape, s.dtype) for s in srcs], *[pltpu.HBM(s.shape, s.dtype) for s in lands],
                   jax.ShapeDtypeStruct((SUBLANES, LANES), F32)),
        in_specs=[_HBM_SPEC] * (2 * nw) + [_ANY_SPEC] * len(deps),
        out_specs=(_SEM_SPEC, _SEM_SPEC, _SEM_SPEC, *[_HBM_SPEC] * (2 * nw), pl.BlockSpec(memory_space=pltpu.VMEM)),
        input_output_aliases={i: 3 + i for i in range(2 * nw)},
        compiler_params=pltpu.CompilerParams(has_side_effects=_DATAFLOW),
    )(*srcs, *lands, *deps)
    return {"sems": out[:3], "srcs": out[3:3 + nw], "lands": out[3 + nw:3 + 2 * nw], "token": out[-1], "scatter": scatter}


def _exchange_wait(name, handle, after):
    nw = len(handle["srcs"])
    scatter = handle["scatter"]

    def body(*refs):
        src, dst = refs[:nw], refs[nw:2 * nw]
        send, recv, loc = refs[2 * nw:2 * nw + 3]
        local, remote = _exchange_copies(src, dst, send, recv, loc, scatter)
        for cp in local:
            cp.wait()
        for cp in remote:
            cp.wait_send()
            cp.wait_recv()

    out = pl.pallas_call(
        body, name=name,
        out_shape=(*[pltpu.HBM(s.shape, s.dtype) for s in handle["srcs"]],
                   *[pltpu.HBM(s.shape, s.dtype) for s in handle["lands"]]),
        in_specs=[_HBM_SPEC] * (2 * nw) + [_SEM_SPEC] * 3 + [_ANY_SPEC],
        out_specs=tuple([_HBM_SPEC] * (2 * nw)),
        input_output_aliases={i: i for i in range(2 * nw)},
        compiler_params=pltpu.CompilerParams(has_side_effects=_DATAFLOW),
    )(*handle["srcs"], *handle["lands"], *handle["sems"], after)
    return list(out[nw:])


def _adamw(w, g, m, v):
    m = ADAM_B1 * m + (1.0 - ADAM_B1) * g
    v = ADAM_B2 * v + (1.0 - ADAM_B2) * (g * g)
    m_hat = m / (1.0 - ADAM_B1 ** ADAM_STEP)
    v_hat = v / (1.0 - ADAM_B2 ** ADAM_STEP)
    delta = -ADAM_LR * (m_hat / (jnp.sqrt(v_hat) + ADAM_EPS) + ADAM_WD * w)
    return delta, m, v


def _adamw_summed(name, parts, w, m, v):
    layered = w.ndim == 3
    parts = list(parts) if layered else [parts]
    n_layers = len(parts)
    rows, cols = w.shape[-2:]
    n_parts = parts[0].shape[0]
    tr = _tile(rows, max(SUBLANES, (1 << 18) // cols))

    def body(*refs):
        p_refs = refs[:n_layers]
        w_ref, m_ref, v_ref, g_ref, d_ref, nm_ref, nv_ref = refs[n_layers:]
        layer = pl.program_id(0)

        def run(p_ref):
            g = p_ref[0].astype(F32)
            for s in range(1, n_parts):
                g = g + p_ref[s].astype(F32)
            d, nm, nv = _adamw(w_ref[...], g, m_ref[...], v_ref[...])
            g_ref[...] = g
            d_ref[...] = d
            nm_ref[...] = nm
            nv_ref[...] = nv

        for l in range(n_layers):
            pl.when(layer == l)(functools.partial(run, p_refs[l]))

    if layered:
        spec = pl.BlockSpec((None, tr, cols), lambda l, i: (l, i, 0))
    else:
        spec = pl.BlockSpec((tr, cols), lambda l, i: (i, 0))
    p_specs = [pl.BlockSpec((n_parts, tr, cols), lambda l, i, q=q: (0, jnp.where(l == q, i, 0), 0))
               for q in range(n_layers)]
    shp = jax.ShapeDtypeStruct(w.shape, F32)
    return pl.pallas_call(
        body, name=name, grid=(n_layers, rows // tr), in_specs=[*p_specs, spec, spec, spec],
        out_specs=[spec] * 4, out_shape=[shp] * 4, compiler_params=_params(("parallel", "parallel")),
    )(*parts, w, m, v)


def _pack_rows(arrays, cols):
    out = []
    for a in arrays:
        flat = a.reshape(-1)
        n = -(-flat.shape[0] // cols) * cols
        out.append(jnp.pad(flat, (0, n - flat.shape[0])).reshape(-1, cols))
    packed = jnp.concatenate(out, axis=0)
    return jnp.pad(packed, ((0, -packed.shape[0] % SUBLANES), (0, 0)))


def _unpack_rows(packed, shapes, cols):
    out, r = [], 0
    for s in shapes:
        n = math.prod(s)
        nr = -(-n // cols)
        out.append(packed[r:r + nr].reshape(-1)[:n].reshape(s))
        r += nr
    return out


def _block_diag(blocks):
    g, a, b = blocks.shape
    eye = jnp.eye(g, dtype=blocks.dtype)
    return (eye[:, None, :, None] * blocks[:, :, None, :]).reshape(g * a, g * b)


def _diag_blocks(dense, g):
    a, b = dense.shape[0] // g, dense.shape[1] // g
    return jnp.einsum("gagb->gab", dense.reshape(g, a, g, b))


def _local_step(x, target, meta, wts, small, late_weights, on_grads, on_small):
    n_batch, seq, d = x.shape
    n_meta = meta.shape[0]
    pad = -(seq + n_meta) % LANES
    lead = pad + n_meta
    lp = lead + seq
    rows = n_batch * lp
    s5w = wts["glu"].shape[0]
    n_ab = wts["in_ab"].shape[2]
    ab_cols = wts["in_ab"].shape[0] * n_ab
    sbw = (ab_cols - s5w) // 3
    dff = small["mlp_b_up"].shape[1]
    n_pairs = sbw // LANES
    n_hg = d // HG_DK
    s5_cb = s5w // LANES
    sb_cb = sbw // LANES
    tm = _tile(rows, ROW_TILE)
    groups, n_state, grp = small["s5_b_re"].shape[1:]
    ns = groups * n_state
    sw = min(SCAN_LANES, ns)

    h0 = jnp.concatenate(
        [jnp.zeros((n_batch, pad, d), F32), jnp.broadcast_to(meta[None], (n_batch, n_meta, d)), x], axis=1
    ).reshape(rows, d)

    lam_re, lam_im = small["s5_lam_re"][0], small["s5_lam_im"][0]
    log_dt = small["s5_log_dt"][0][:, None]
    b_re_t = small["s5_b_re"][0].transpose(0, 2, 1)
    b_im_t = small["s5_b_im"][0].transpose(0, 2, 1)
    c_re, c_im = small["s5_c_re"][0], small["s5_c_im"][0]
    lbr, lbi, bbr, bbi = _s5_params("s5_params", lam_re, lam_im, log_dt, b_re_t, b_im_t)
    b_blk = _interleave(_block_diag(bbr), _block_diag(bbi), sw).astype(BF16)
    c_blk = _interleave(_block_diag(c_re), _block_diag(-c_im), sw).T.astype(BF16)
    lam_row = _interleave(lbr.reshape(1, ns), lbi.reshape(1, ns), sw)
    d_row = small["s5_d"].reshape(1, s5w)

    def ln_store(outs, acc, res, bias, g, b):
        r = ALPHA * res + acc + bias
        outs[0][...] = r
        if len(outs) > 1:
            outs[1][...] = _ln(r, g, b)

    zero_bias = jnp.zeros((1, d), F32)

    def mix_ln(name, a, w, k_total, tk, res, bias, g, b, a_fn=None, emit_h=True):
        n_out = 2 if emit_h else 1
        return _mm_act(name, a, w, "nat", n_out_cols=d, k_total=k_total, tn=d, tk=tk, a_fn=a_fn,
                       extras=(res, bias, g, b), extra_specs=(_row_spec(tm, d), _vec_spec(d), _vec_spec(d), _vec_spec(d)),
                       store=ln_store, out_shape=[jax.ShapeDtypeStruct((rows, d), F32)] * n_out,
                       out_specs=[_row_spec(tm, d)] * n_out)

    def two(width):
        return [jax.ShapeDtypeStruct((rows, width), F32)] * 2, [_row_spec(tm, width)] * 2

    def shard_tile(total, shard, cap=1024):
        t = max(shard, cap - cap % shard)
        while total % t:
            t -= shard
        return t

    proj_ab = _mm_act("in_ab", h0, wts["in_ab"], "stk", n_out_cols=ab_cols, k_total=d, tn=shard_tile(ab_cols, n_ab), tk=d)[0]
    bu = _mm_act("s5_bu", proj_ab, b_blk, "nat", n_out_cols=2 * ns, k_total=s5w, tn=min(2 * ns, 2048), tk=s5w)[0]
    states = _s5_scan("s5_scan", bu, lam_row, n_batch, lp, sw)

    def gelu_store(outs, acc, u, dv):
        ypre = acc + dv * u
        outs[0][...] = ypre
        outs[1][...] = jax.nn.gelu(ypre)

    shp2, spec2 = two(s5w)
    ypre, y = _mm_act(
        "s5_y", states, c_blk, "nat", n_out_cols=s5w, k_total=2 * ns, tn=s5w, tk=min(2 * ns, 1024),
        extras=(proj_ab, d_row), extra_specs=(_row_spec(tm, s5w), _vec_spec(s5w)), store=gelu_store,
        out_shape=shp2, out_specs=spec2)

    def glu_store(outs, acc, yv, bias):
        gate = acc + bias
        outs[0][...] = gate
        outs[1][...] = _glu(yv, gate)

    gate, a_out = _mm_act(
        "s5_glu", y, wts["glu"], "nat", n_out_cols=s5w, k_total=s5w, tn=s5w, tk=s5w,
        extras=(y, small["s5_b_glu"]), extra_specs=(_row_spec(tm, s5w), _vec_spec(s5w)), store=glu_store,
        out_shape=shp2, out_specs=spec2)
    b_out = _attn_fwd("sb_attn", proj_ab, n_batch, lp, pad, s5_cb, s5_cb + sb_cb, s5_cb + 2 * sb_cb, n_pairs)

    def bias_store(outs, acc, bias):
        outs[0][...] = (acc + bias).astype(outs[0].dtype)

    def wide(width, dtype):
        return [jax.ShapeDtypeStruct((rows, dff), dtype)], [_row_spec(tm, width)]

    def mlp_fwd(layer, h_in, emit_h=True):
        tn = shard_tile(dff, n_up)
        shp, spec = wide(tn, BF16)
        up = _mm_act(f"up{layer}", h_in, wts["up"][layer], "stk", n_out_cols=dff, k_total=d, tn=tn, tk=d,
                     extras=(small["mlp_b_up"][layer:layer + 1],), extra_specs=(_vec_spec(tn),), store=bias_store,
                     out_shape=shp, out_specs=spec)[0]
        return (up, *mix_ln(f"down{layer}", up, wts["down"][layer], dff, min(dff, 1024), h_in,
                            small["mlp_b_down"][layer:layer + 1], small["ln_mlp_g"][layer:layer + 1],
                            small["ln_mlp_b"][layer:layer + 1], a_fn=_relu2, emit_h=emit_h))

    r1, h1 = mix_ln("out_ab", [a_out, b_out], wts["out_ab"], s5w + sbw, min(s5w, sbw), h0, zero_bias,
                    small["ln_mix_g"][0:1], small["ln_mix_b"][0:1])
    wts = {**wts, **late_weights(r1)}
    n_c = wts["in_c"].shape[2]
    n_up = wts["up"][0].shape[2]
    up0, r2, h2 = mlp_fwd(0, h1)

    lb = _lower_bound("hg_lb", small["hgrn_gamma"])
    proj_c = _mm_act("in_c", h2, wts["in_c"], "stk", n_out_cols=4 * d, k_total=d, tn=shard_tile(4 * d, n_c), tk=d)[0]
    c_out, hg_states = _hgrn_fwd("hgrn", proj_c, lb, wts["ng"], n_batch, lp, pad, n_hg)
    r3, h3 = mix_ln("out_c", c_out, wts["out_c"], d, d, h2, zero_bias, small["ln_mix_g"][1:2], small["ln_mix_b"][1:2])
    up1, r4 = mlp_fwd(1, h3, emit_h=False)

    gr = {}
    g_r4, gr["ln_mlp_g1"], gr["ln_mlp_b1"], loss_tile = _loss_grad(
        "loss", r4, small["ln_mlp_g"][1:2], small["ln_mlp_b"][1:2], target, n_batch, lp, lead)

    def res_store(outs, acc, g_res):
        outs[0][...] = acc + ALPHA * g_res

    def ln_bwd_store(outs, acc, g_res, r_in, g, b, first_step):
        gr_in, gg, gb = jax.vjp(_ln, r_in, g, b)[1](acc + ALPHA * g_res)
        outs[0][...] = gr_in

        @pl.when(first_step)
        def _():
            outs[1][...] = jnp.zeros_like(outs[1])
            outs[2][...] = jnp.zeros_like(outs[2])

        outs[1][...] += gg
        outs[2][...] += gb

    def through_ln(name, a, w, k_total, tk, g_res, r_in, g, b, dep=None):
        vec = pl.BlockSpec((1, d), lambda i, j, k: (0, 0))
        return _mm_act(name, a, w, "stkT", n_out_cols=d, k_total=k_total, tn=d, tk=tk,
                       extras=(g_res, r_in, g, b), extra_specs=(_row_spec(tm, d), _row_spec(tm, d), vec, vec),
                       store=ln_bwd_store, sequential=True, dep=dep,
                       out_shape=[jax.ShapeDtypeStruct((rows, d), F32)] + [jax.ShapeDtypeStruct((1, d), F32)] * 2,
                       out_specs=[_row_spec(tm, d), vec, vec])

    def mlp_bwd(layer, g_r, up, h_in, r_in, send=None):
        def gup_store(outs, acc, upv):
            outs[0][...] = (acc * (2.0 * jnp.maximum(upv.astype(F32), 0.0))).astype(outs[0].dtype)

        tf = min(dff, 1024)
        shp, spec = wide(tf, BF16)
        g_up = _mm_act(f"g_up{layer}", g_r, wts["down"][layer], "natT", n_out_cols=dff, k_total=d, tn=tf, tk=d,
                       extras=(up,), extra_specs=(_row_spec(tm, tf),), store=gup_store, out_shape=shp, out_specs=spec)[0]
        gr[f"down{layer}"], gr[f"mlp_b_down{layer}"] = _mm_wgrad(
            f"dw_down{layer}", up, g_r, kw=dff, n=d, tmw=tf, tn=d, a_fn=_relu2, out_dtype=BF16, colsum=True)
        gr[f"up{layer}"], gr[f"mlp_b_up{layer}"] = _mm_wgrad(
            f"dw_up{layer}", h_in, g_up, kw=d, n=dff, tmw=d, tn=min(dff, 2048), shard_cols=n_up, out_dtype=BF16, colsum=True)
        dep = send() if send is not None else None
        g_r_in, gr[f"ln_mix_g{layer}"], gr[f"ln_mix_b{layer}"] = through_ln(
            f"g_hmid{layer}", g_up, wts["up"][layer], dff, shard_tile(dff, n_up), g_r, r_in,
            small["ln_mix_g"][layer:layer + 1], small["ln_mix_b"][layer:layer + 1], dep=dep)
        return g_r_in

    g_r3 = mlp_bwd(1, g_r4, up1, h3, r3)
    g_cout = _mm_act("g_cout", g_r3, wts["out_c"], "natT", n_out_cols=d, k_total=d, tn=d, tk=d)[0]
    gr["out_c"] = _mm_wgrad("dw_out_c", c_out, g_r3, kw=d, n=d, tmw=d, tn=d, out_dtype=BF16)
    gq, gf, gi, gg_, g_lb_parts, g_ng_parts = _hgrn_bwd("hgrn_bwd", proj_c, lb, wts["ng"], g_cout, hg_states,
                                                        n_batch, lp, pad, n_hg)
    g_pc = [gq, gf, gi, gg_]
    gr["hgrn_gamma"], gr["ng"] = _lower_bound_bwd("hg_lb_bwd", small["hgrn_gamma"], g_lb_parts, g_ng_parts)
    gr["in_c"] = _mm_wgrad("dw_in_c", h2, g_pc, kw=d, n=4 * d, tmw=d, tn=d, shard_cols=n_c, out_dtype=BF16)
    sent1 = on_grads(1, {"down1": gr["down1"], "up1": gr["up1"], "out_c": gr["out_c"], "in_c": gr["in_c"], "ng": gr["ng"]})
    g_r2, gr["ln_mlp_g0"], gr["ln_mlp_b0"] = through_ln(
        "g_h2", g_pc, wts["in_c"], 4 * d, n_c, g_r3, r2, small["ln_mlp_g"][0:1], small["ln_mlp_b"][0:1], dep=sent1)

    g_r1 = mlp_bwd(0, g_r2, up0, h1, r1, send=lambda: on_grads(2, {"down0": gr["down0"], "up0": gr["up0"]}))
    g_cat = _mm_act("g_cat", g_r1, wts["out_ab"], "natT", n_out_cols=d, k_total=d, tn=d, tk=d)[0]
    gr["out_ab"] = _mm_wgrad("dw_out_ab", [a_out, b_out], g_r1, kw=s5w + sbw, n=d, tmw=min(s5w, sbw), tn=d, out_dtype=BF16)
    g_q, g_k, g_v = _attn_bwd("sb_attn_bwd", proj_ab, g_cat, n_batch, lp, pad, s5_cb, s5_cb + sb_cb, s5_cb + 2 * sb_cb,
                              s5_cb, n_pairs)

    g_y_direct, g_gate = _rowwise("s5_glu_bwd", lambda ga, yv, gt: jax.vjp(_glu, yv, gt)[1](ga),
                                  [(g_cat, 0, s5w), (y, 0, s5w), (gate, 0, s5w)], 2, s5w)

    def gelu_bwd_store(outs, acc, gyd, yp, u, dv):
        gyp = jax.vjp(jax.nn.gelu, yp)[1](acc + gyd)[0]
        outs[0][...] = gyp
        outs[1][...] = dv * gyp
        outs[2][...] = jnp.sum(gyp * u, axis=0, keepdims=True)

    rs = _row_spec(tm, s5w)
    g_ypre, g_u_direct, gd_parts = _mm_act(
        "s5_g_y", g_gate, wts["glu"], "natT", n_out_cols=s5w, k_total=s5w, tn=s5w, tk=s5w,
        extras=(g_y_direct, ypre, proj_ab, d_row), extra_specs=(rs, rs, rs, _vec_spec(s5w)), store=gelu_bwd_store,
        out_shape=[jax.ShapeDtypeStruct((rows, s5w), F32)] * 2 + [jax.ShapeDtypeStruct((rows // tm, 1, s5w), F32)],
        out_specs=[rs, rs, pl.BlockSpec((None, 1, s5w), lambda i, j, k: (i, 0, j))])
    gr["glu"], gr["s5_b_glu"] = _mm_wgrad("dw_glu", y, g_gate, kw=s5w, n=s5w, tmw=s5w, tn=s5w, out_dtype=BF16, colsum=True)
    g_sd = _mm_act("s5_g_states", g_ypre, c_blk, "natT", n_out_cols=2 * ns, k_total=s5w, tn=min(2 * ns, 2048), tk=s5w)[0]
    d_cblk = _mm_wgrad("dw_cblk", states, g_ypre, kw=2 * ns, n=s5w, tmw=min(2 * ns, 1024), tn=s5w)
    gs, gl_parts = _s5_scan_bwd("s5_scan_bwd", g_sd, states, lam_row, n_batch, lp, sw)

    def add_store(outs, acc, other):
        outs[0][...] = acc + other

    g_u = _mm_act("s5_g_u", gs, b_blk, "natT", n_out_cols=s5w, k_total=2 * ns, tn=s5w, tk=min(2 * ns, 1024),
                  extras=(g_u_direct,), extra_specs=(rs,), store=add_store)[0]
    d_bblk = _mm_wgrad("dw_bblk", proj_ab, gs, kw=s5w, n=2 * ns, tmw=s5w, tn=min(2 * ns, 2048))
    db_re, db_im = _deinterleave(d_bblk, sw)
    dc_re, dc_im = _deinterleave(d_cblk.T, sw)
    glr, gli = _deinterleave(gl_parts, sw)
    g_lam_re, g_lam_im, g_log_dt, g_b_re_t, g_b_im_t, g_d = _s5_params_bwd(
        "s5_params_bwd", lam_re, lam_im, log_dt, b_re_t, b_im_t,
        glr.reshape(n_batch, groups, n_state), gli.reshape(n_batch, groups, n_state),
        _diag_blocks(db_re, groups), _diag_blocks(db_im, groups), gd_parts)

    cat2 = lambda key: jnp.concatenate([gr[key + "0"], gr[key + "1"]], axis=0)
    small_sent = on_small({
        "s5_lam_re": g_lam_re[None], "s5_lam_im": g_lam_im[None], "s5_log_dt": g_log_dt.reshape(1, groups),
        "s5_b_re": g_b_re_t.transpose(0, 2, 1)[None], "s5_b_im": g_b_im_t.transpose(0, 2, 1)[None],
        "s5_c_re": _diag_blocks(dc_re, groups)[None], "s5_c_im": -_diag_blocks(dc_im, groups)[None],
        "s5_d": g_d.reshape(1, groups, grp), "s5_b_glu": gr["s5_b_glu"], "hgrn_gamma": gr["hgrn_gamma"],
        "ln_mix_g": cat2("ln_mix_g"), "ln_mix_b": cat2("ln_mix_b"), "mlp_b_up": cat2("mlp_b_up"),
        "mlp_b_down": cat2("mlp_b_down"), "ln_mlp_g": cat2("ln_mlp_g"), "ln_mlp_b": cat2("ln_mlp_b"),
    }, loss_tile)

    g_pab = [g_u, g_q, g_k, g_v]
    assert s5w == sbw
    gr["in_ab"] = _mm_wgrad("dw_in_ab", h0, g_pab, kw=d, n=ab_cols, tmw=d, tn=s5w, shard_cols=n_ab, out_dtype=BF16,
                            dep=small_sent)
    g_h0 = _mm_act("g_h0", g_pab, wts["in_ab"], "stkT", n_out_cols=d, k_total=ab_cols, tn=d, tk=shard_tile(s5w, n_ab),
                   extras=(g_r1,), extra_specs=(_row_spec(tm, d),), store=res_store)[0]
    grad_x = g_h0.reshape(n_batch, lp, d)[:, lead:, :]
    g_meta = _meta_grad("g_meta", g_h0, n_batch, lp, pad, n_meta)
    on_grads(3, {"meta": g_meta, "in_ab": gr["in_ab"], "glu": gr["glu"], "out_ab": gr["out_ab"]})
    return grad_x


SMALL_NAMES = ("s5_lam_re", "s5_lam_im", "s5_log_dt", "s5_b_re", "s5_b_im", "s5_c_re", "s5_c_im", "s5_d", "s5_b_glu",
               "hgrn_gamma", "ln_mix_g", "ln_mix_b", "mlp_b_up", "mlp_b_down", "ln_mlp_g", "ln_mlp_b")
WEIGHT_ORDER = ("meta", "w_in_ab", "s5_lam_re", "s5_lam_im", "s5_log_dt", "s5_b_re", "s5_b_im", "s5_c_re", "s5_c_im",
                "s5_d", "s5_w_glu", "s5_b_glu", "w_out_ab", "w_in_c", "hgrn_gamma", "hgrn_norm_g", "w_out_c", "ln_mix_g",
                "ln_mix_b", "mlp_w_up", "mlp_b_up", "mlp_w_down", "mlp_b_down", "ln_mlp_g", "ln_mlp_b")


def kernel(x, meta, w_in_ab, s5_lam_re, s5_lam_im, s5_log_dt, s5_b_re, s5_b_im, s5_c_re, s5_c_im, s5_d, s5_w_glu, s5_b_glu, w_out_ab, w_in_c, hgrn_gamma, hgrn_norm_g, w_out_c, ln_mix_g, ln_mix_b, mlp_w_up, mlp_b_up, mlp_w_down, mlp_b_down, ln_mlp_g, ln_mlp_b, loss_target, m_meta, m_w_in_ab, m_s5_lam_re, m_s5_lam_im, m_s5_log_dt, m_s5_b_re, m_s5_b_im, m_s5_c_re, m_s5_c_im, m_s5_d, m_s5_w_glu, m_s5_b_glu, m_w_out_ab, m_w_in_c, m_hgrn_gamma, m_hgrn_norm_g, m_w_out_c, m_ln_mix_g, m_ln_mix_b, m_mlp_w_up, m_mlp_b_up, m_mlp_w_down, m_mlp_b_down, m_ln_mlp_g, m_ln_mlp_b, v_meta, v_w_in_ab, v_s5_lam_re, v_s5_lam_im, v_s5_log_dt, v_s5_b_re, v_s5_b_im, v_s5_c_re, v_s5_c_im, v_s5_d, v_s5_w_glu, v_s5_b_glu, v_w_out_ab, v_w_in_c, v_hgrn_gamma, v_hgrn_norm_g, v_w_out_c, v_ln_mix_g, v_ln_mix_b, v_mlp_w_up, v_mlp_b_up, v_mlp_w_down, v_mlp_b_down, v_ln_mlp_g, v_ln_mlp_b):
    args = dict(locals())
    w = {n: args[n] for n in WEIGHT_ORDER}
    mom = {n: args["m_" + n] for n in WEIGHT_ORDER}
    var = {n: args["v_" + n] for n in WEIGHT_ORDER}
    d = x.shape[2]
    n_meta = meta.shape[0]

    cast = lambda a: a.astype(BF16)
    early = _exchange_start("gather_early_start", [w["meta"], cast(w["w_in_ab"][0]), cast(w["s5_w_glu"][0]),
                                                   cast(w["w_out_ab"][0])], False)
    late = _exchange_start("gather_late_start", [w["hgrn_norm_g"], cast(w["w_in_c"][0]), cast(w["w_out_c"][0]),
                                                 cast(w["mlp_w_up"][0]), cast(w["mlp_w_up"][1]),
                                                 cast(w["mlp_w_down"][0]), cast(w["mlp_w_down"][1])], False, dep=early["token"])
    a_meta, a_in_ab, a_glu, a_out_ab = _exchange_wait("gather_early_wait", early, late["token"])
    wts = {"in_ab": a_in_ab, "glu": a_glu.reshape(-1, a_glu.shape[2]), "out_ab": a_out_ab.reshape(-1, d)}
    meta_full = a_meta.transpose(1, 0, 2).reshape(n_meta, d)
    small = {n: w[n] for n in SMALL_NAMES}

    def late_weights(after):
        a_ng, a_in_c, a_out_c, a_up0, a_up1, a_dn0, a_dn1 = _exchange_wait("gather_late_wait", late, after)
        return {"in_c": a_in_c, "ng": a_ng.transpose(1, 0, 2).reshape(1, d), "out_c": a_out_c.reshape(-1, d),
                "up": [a_up0, a_up1], "down": [a_dn0.reshape(-1, d), a_dn1.reshape(-1, d)]}

    n_loc = d // N_DEV
    rows_of = lambda g: g.reshape(N_DEV, -1, g.shape[-1])
    cols_of = lambda g: g.reshape(g.shape[0], N_DEV, n_loc).transpose(1, 0, 2)
    sent = {}

    def on_grads(stage, g):
        if stage == 1:
            order = (("mlp_w_down", 1), ("mlp_w_up", 1), ("w_out_c", 0), ("w_in_c", 0), ("hgrn_norm_g", None))
            parts = [rows_of(g["down1"]), g["up1"], rows_of(g["out_c"]), g["in_c"], cols_of(g["ng"])]
        elif stage == 2:
            order = (("mlp_w_down", 0), ("mlp_w_up", 0))
            parts = [rows_of(g["down0"]), g["up0"]]
        else:
            order = (("w_out_ab", 0), ("s5_w_glu", 0), ("w_in_ab", 0), ("meta", None))
            parts = [rows_of(g["out_ab"]), rows_of(g["glu"]), g["in_ab"], cols_of(g["meta"])]
        sent[stage] = (order, _exchange_start(f"scatter_start{stage}", parts, True))
        return sent[stage][1]["token"]

    def on_small(sg, loss_tile):
        g_pack = _pack_rows([sg[n] for n in SMALL_NAMES] + [loss_tile], PACK_COLS)
        sent["small"] = _exchange_start("gather_small_start", [g_pack], False)
        return sent["small"]["token"]

    grad_x = _local_step(x, loss_target, meta_full, wts, small, late_weights, on_grads, on_small)
    small_sent = sent["small"]
    tile = (SUBLANES, LANES)
    shapes = [w[n].shape for n in SMALL_NAMES] + [tile]
    zeros = jnp.zeros(tile, F32)
    w_pack = _pack_rows([w[n] for n in SMALL_NAMES] + [zeros], PACK_COLS)
    m_pack = _pack_rows([mom[n] for n in SMALL_NAMES] + [zeros], PACK_COLS)
    v_pack = _pack_rows([var[n] for n in SMALL_NAMES] + [zeros], PACK_COLS)

    received, res = {}, {}

    def wait(stage, after):
        order, handle = sent[stage]
        for key, rc in zip(order, _exchange_wait(f"scatter_wait{stage}", handle, after)):
            received[key] = rc

    def update(nm):
        layered = w[nm].ndim == 3
        parts = [received[(nm, l)] for l in range(w[nm].shape[0])] if layered else received[(nm, None)]
        res[nm] = _adamw_summed(f"adamw_{nm}", parts, w[nm], mom[nm], var[nm])
        return res[nm][0]

    wait(1, sent[3][1]["token"])
    done = [update(nm) for nm in ("w_out_c", "w_in_c", "hgrn_norm_g")]
    wait(2, done[0])
    done = [update(nm) for nm in ("mlp_w_up", "mlp_w_down")]
    g_all = _exchange_wait("gather_small_wait", small_sent, done[0])[0]
    packed = _adamw_summed("adamw_small", g_all, w_pack, m_pack, v_pack)
    wait(3, packed[0])
    for nm in ("w_out_ab", "s5_w_glu", "w_in_ab", "meta"):
        update(nm)
    unpacked = [_unpack_rows(p, shapes, PACK_COLS) for p in packed]
    loss = unpacked[0][-1][0, 0]

    def pick(nm, which):
        return unpacked[which][SMALL_NAMES.index(nm)] if nm in SMALL_NAMES else res[nm][which]

    return (loss, grad_x, *[pick(n, 0) for n in WEIGHT_ORDER], *[pick(n, 1) for n in WEIGHT_ORDER],
            *[pick(n, 2) for n in WEIGHT_ORDER], *[pick(n, 3) for n in WEIGHT_ORDER])
```

```python
import functools
import math

import jax
import jax.numpy as jnp
from jax import lax
from jax.experimental import pallas as pl
from jax.experimental.pallas import tpu as pltpu

F32 = jnp.float32
BF16 = jnp.bfloat16

N_DEV = 8
DEPTH = 2
ALPHA = (2.0 * DEPTH) ** 0.25
LN_EPS = 1e-5
RMS_EPS = 1e-6
SB_HEAD_DIM = 64
HG_DK = 128
HG_CHUNK = 64
LANES = 128
SUBLANES = 8
PACKED_ROWS = 16
VMEM_LIMIT_BYTES = 56 * 1024 * 1024
ROW_TILE = 1088
SCAN_LANES = 256
SCAN_UNROLL = 2
PACK_COLS = 1024

ADAM_LR = 0.001
ADAM_B1 = 0.9
ADAM_B2 = 0.999
ADAM_EPS = 1e-08
ADAM_WD = 0.01
ADAM_STEP = 10

NN = (((1,), (0,)), ((), ()))
NT = (((1,), (1,)), ((), ()))
TN = (((0,), (0,)), ((), ()))


def _tile(n, pref, align=SUBLANES):
    t = min(n, pref)
    t -= t % align
    while t >= align:
        if n % t == 0:
            return t
        t -= align
    return n


def _unrolled_loop(n, body, init, unroll):
    assert n % unroll == 0

    def outer(t, carry):
        for u in range(unroll):
            carry = body(t * unroll + u, carry)
        return carry

    return lax.fori_loop(0, n // unroll, outer, init)


def _params(sem):
    return pltpu.CompilerParams(dimension_semantics=sem, vmem_limit_bytes=VMEM_LIMIT_BYTES)


def _dot_raw(a, b, dims):
    return lax.dot_general(a.astype(BF16), b.astype(BF16), dims, preferred_element_type=F32)


def _make_dot(dims, da_rule, db_rule):
    @jax.custom_vjp
    def f(a, b):
        return _dot_raw(a, b, dims)

    def fwd(a, b):
        return _dot_raw(a, b, dims), (a, b)

    def bwd(res, g):
        a, b = res
        return da_rule(g, a, b), db_rule(g, a, b)

    f.defvjp(fwd, bwd)
    return f


_DOTS = {
    NN: _make_dot(NN, lambda g, a, b: _dot_raw(g, b, NT), lambda g, a, b: _dot_raw(a, g, TN)),
    NT: _make_dot(NT, lambda g, a, b: _dot_raw(g, b, NN), lambda g, a, b: _dot_raw(g, a, TN)),
    TN: _make_dot(TN, lambda g, a, b: _dot_raw(b, g, NT), lambda g, a, b: _dot_raw(a, g, NN)),
}


def _dot(a, b, dims):
    return _DOTS[dims](a, b)


def _running_sums(a, tri_ones, split=False):
    hi = a.astype(BF16)
    out = lax.dot_general(hi, tri_ones, NN, preferred_element_type=F32)
    if split:
        lo = (a - hi.astype(F32)).astype(BF16)
        out = out + lax.dot_general(lo, tri_ones, NN, preferred_element_type=F32)
    return out


def _piece_specs(pieces, block_rows, block_cols, row_of, col_of, cb0):
    per = pieces[0].shape[1] // block_cols if len(pieces) > 1 else None
    specs = []
    for p in range(len(pieces)):
        if per is None:
            specs.append(pl.BlockSpec((block_rows, block_cols), lambda *g: (row_of(*g), cb0 + col_of(*g))))
        else:
            specs.append(pl.BlockSpec(
                (block_rows, block_cols),
                lambda *g, p=p: (row_of(*g), jnp.clip(col_of(*g) - p * per, 0, per - 1))))
    return specs, per


def _mm_call(name, grid, dims, a_pieces, a_specs, a_sel, b_pieces, b_specs, b_sel, extras, extra_specs,
             out_shape, out_specs, acc_shape, a_fn, store, colsum_width=0, sequential=False, deps=()):
    na, nb, ne, no, nd = len(a_pieces), len(b_pieces), len(extras), len(out_shape), len(deps)
    nk = grid[2]

    def body(*refs):
        a_refs, b_refs = refs[:na], refs[na:na + nb]
        extra = refs[na + nb:na + nb + ne]
        outs = refs[na + nb + ne + nd:na + nb + ne + nd + no]
        acc = refs[na + nb + ne + nd + no]
        ids = (pl.program_id(0), pl.program_id(1), pl.program_id(2))
        k = ids[2]

        @pl.when(k == 0)
        def _():
            acc[...] = jnp.zeros_like(acc)

        def run(a_ref, b_ref):
            a = a_ref[...]
            if a_fn is not None:
                a = a_fn(a)
            b = b_ref[...]
            if b.ndim == 3 and dims == NN:
                n = b.shape[2]
                for q in range(b.shape[0]):
                    acc[:, q * n:(q + 1) * n] += _dot_raw(a, b[q], dims)
            elif b.ndim == 3:
                n = b.shape[2]
                for q in range(b.shape[0]):
                    acc[...] += _dot_raw(a[:, q * n:(q + 1) * n], b[q], dims)
            else:
                acc[...] += _dot_raw(a, b, dims)
            if colsum_width:
                cs = refs[-1]
                first = ids[1] == 0

                @pl.when(first & (k == 0))
                def _():
                    cs[...] = jnp.zeros_like(cs)

                @pl.when(first)
                def _():
                    cs[...] += jnp.sum(b.astype(F32), axis=0, keepdims=True)

        if na == 1 and nb == 1:
            run(a_refs[0], b_refs[0])
        elif nb == 1:
            per, fn = a_sel
            which = fn(*ids) // per
            for p in range(na):
                pl.when(which == p)(functools.partial(run, a_refs[p], b_refs[0]))
        else:
            assert na == 1
            per, fn = b_sel
            which = fn(*ids) // per
            for p in range(nb):
                pl.when(which == p)(functools.partial(run, a_refs[0], b_refs[p]))

        @pl.when(k == nk - 1)
        def _():
            if sequential:
                store(outs, acc[...], *[e[...] for e in extra], first_step=(ids[0] == 0) & (ids[1] == 0))
            else:
                store(outs, acc[...], *[e[...] for e in extra])
            if colsum_width:
                @pl.when(ids[1] == 0)
                def _():
                    outs[-1][...] = refs[-1][...]

    scratch = [pltpu.VMEM(acc_shape, F32)]
    if colsum_width:
        scratch.append(pltpu.VMEM((1, colsum_width), F32))
    sem = ("parallel", "arbitrary", "arbitrary") if colsum_width else ("parallel", "parallel", "arbitrary")
    if sequential:
        sem = ("arbitrary",) * 3
    return pl.pallas_call(
        body, name=name, grid=grid,
        in_specs=[*a_specs, *b_specs, *extra_specs, *[pl.BlockSpec(memory_space=pl.ANY)] * nd], out_specs=out_specs,
        out_shape=out_shape, scratch_shapes=scratch, compiler_params=_params(sem),
    )(*a_pieces, *b_pieces, *extras, *deps)


def _store_plain(outs, acc):
    outs[0][...] = acc.astype(outs[0].dtype)


def _row_spec(tm, tn):
    return pl.BlockSpec((tm, tn), lambda i, j, k: (i, j))


def _vec_spec(tn):
    return pl.BlockSpec((1, tn), lambda i, j, k: (0, j))


def _mm_act(name, a, w, wkind, *, n_out_cols, k_total, tn, tk, a_cb0=0, a_fn=None, extras=(), extra_specs=(),
            store=_store_plain, out_shape=None, out_specs=None, sequential=False, dep=None):
    a_pieces = list(a) if isinstance(a, (list, tuple)) else [a]
    rows = a_pieces[0].shape[0]
    tm = _tile(rows, ROW_TILE)
    grid = (rows // tm, n_out_cols // tn, k_total // tk)
    a_specs, per = _piece_specs(a_pieces, tm, tk, lambda i, j, k: i, lambda i, j, k: k, a_cb0)
    if wkind == "nat":
        b_spec, dims = pl.BlockSpec((tk, tn), lambda i, j, k: (k, j)), NN
    elif wkind == "stk":
        n = w.shape[2]
        assert tn % n == 0
        b_spec, dims = pl.BlockSpec((tn // n, tk, n), lambda i, j, k: (j, k, 0)), NN
    elif wkind == "natT":
        b_spec, dims = pl.BlockSpec((tn, tk), lambda i, j, k: (j, k)), NT
    else:
        n = w.shape[2]
        assert wkind == "stkT" and tk % n == 0
        b_spec, dims = pl.BlockSpec((tk // n, tn, n), lambda i, j, k: (k, j, 0)), NT
    if out_shape is None:
        out_shape = [jax.ShapeDtypeStruct((rows, n_out_cols), F32)]
        out_specs = [_row_spec(tm, tn)]
    return _mm_call(name, grid, dims, a_pieces, a_specs, (per, lambda i, j, k: k), [w], [b_spec], None,
                    list(extras), list(extra_specs), out_shape, out_specs, (tm, tn), a_fn, store,
                    sequential=sequential, deps=() if dep is None else (dep,))


def _mm_wgrad(name, a, g, *, kw, n, tmw, tn, a_cb0=0, a_fn=None, shard_cols=0, out_dtype=F32, colsum=False, dep=None):
    a_pieces = list(a) if isinstance(a, (list, tuple)) else [a]
    g_pieces = list(g) if isinstance(g, (list, tuple)) else [g]
    rows = a_pieces[0].shape[0]
    tr = _tile(rows, ROW_TILE)
    grid = (n // tn, kw // tmw, rows // tr)
    a_specs, a_per = _piece_specs(a_pieces, tr, tmw, lambda j, i, k: k, lambda j, i, k: i, a_cb0)
    g_specs, g_per = _piece_specs(g_pieces, tr, tn, lambda j, i, k: k, lambda j, i, k: j, 0)
    if shard_cols:
        per = tn // shard_cols
        out_shape = [jax.ShapeDtypeStruct((n // shard_cols, kw, shard_cols), out_dtype)]
        out_specs = [pl.BlockSpec((per, tmw, shard_cols), lambda j, i, k: (j, i, 0))]

        def store(outs, acc):
            for q in range(per):
                outs[0][q] = acc[:, q * shard_cols:(q + 1) * shard_cols].astype(out_dtype)
    else:
        out_shape = [jax.ShapeDtypeStruct((kw, n), out_dtype)]
        out_specs = [pl.BlockSpec((tmw, tn), lambda j, i, k: (i, j))]

        def store(outs, acc):
            outs[0][...] = acc.astype(out_dtype)
    if colsum:
        out_shape.append(jax.ShapeDtypeStruct((1, n), F32))
        out_specs.append(pl.BlockSpec((1, tn), lambda j, i, k: (0, j)))
    res = _mm_call(name, grid, TN, a_pieces, a_specs, (a_per, lambda j, i, k: i), g_pieces, g_specs,
                   (g_per, lambda j, i, k: j), [], [], out_shape, out_specs, (tmw, tn), a_fn, store,
                   colsum_width=tn if colsum else 0, deps=() if dep is None else (dep,))
    return res if colsum else res[0]


def _ln(x, g, b):
    mu = jnp.mean(x, axis=-1, keepdims=True)
    xc = x - mu
    var = jnp.mean(xc * xc, axis=-1, keepdims=True)
    return xc * lax.rsqrt(var + LN_EPS) * g + b


def _relu2(x):
    r = jnp.maximum(x.astype(F32), 0.0)
    return r * r


def _glu(y, gate):
    return y * jax.nn.sigmoid(gate)


def _rowwise(name, fn, ins, n_out, width):
    rows = ins[0][0].shape[0]
    tm = _tile(rows, ROW_TILE)

    def body(*refs):
        res = fn(*[r[...] for r in refs[:len(ins)]])
        for o, v in zip(refs[len(ins):], res):
            o[...] = v

    return pl.pallas_call(
        body, name=name, grid=(rows // tm,),
        in_specs=[pl.BlockSpec((tm, wd), lambda i, cb=cb: (i, cb)) for _, cb, wd in ins],
        out_specs=[pl.BlockSpec((tm, width), lambda i: (i, 0))] * n_out,
        out_shape=[jax.ShapeDtypeStruct((rows, width), F32)] * n_out, compiler_params=_params(("parallel",)),
    )(*[a for a, _, _ in ins])


def _loss_grad(name, r, g, b, target, n_batch, lp, lead):
    rows, d = r.shape
    nq = lp // LANES
    lead_blocks = lead // LANES

    def body(r_ref, g_ref, b_ref, t_ref, gr_ref, gg_ref, gb_ref, loss_ref):
        i = pl.program_id(1)

        @pl.when((pl.program_id(0) == 0) & (i == 0))
        def _():
            loss_ref[...] = jnp.zeros_like(loss_ref)
            gg_ref[...] = jnp.zeros_like(gg_ref)
            gb_ref[...] = jnp.zeros_like(gb_ref)

        h, vjp = jax.vjp(_ln, r_ref[...], g_ref[...], b_ref[...])
        diff = jnp.where(i >= lead_blocks, h - t_ref[...], 0.0)
        gr, gg, gb = vjp(diff * (1.0 / d))
        gr_ref[...] = gr
        gg_ref[...] += gg
        gb_ref[...] += gb
        loss_ref[...] += 0.5 * jnp.sum(diff * diff) * (1.0 / d)

    vec = pl.BlockSpec((1, d), lambda b, i: (0, 0))
    row = pl.BlockSpec((LANES, d), lambda b, i: (b * nq + i, 0))
    return pl.pallas_call(
        body, name=name, grid=(n_batch, nq),
        in_specs=[row, vec, vec, pl.BlockSpec((None, LANES, d), lambda b, i: (b, jnp.maximum(i - lead_blocks, 0), 0))],
        out_specs=[row, vec, vec, pl.BlockSpec((SUBLANES, LANES), lambda b, i: (0, 0))],
        out_shape=[jax.ShapeDtypeStruct((rows, d), F32), jax.ShapeDtypeStruct((1, d), F32),
                   jax.ShapeDtypeStruct((1, d), F32), jax.ShapeDtypeStruct((SUBLANES, LANES), F32)],
        compiler_params=_params(("arbitrary", "arbitrary")),
    )(r, g, b, target)


def _meta_grad(name, g_h0, n_batch, lp, pad, n_meta):
    d = g_h0.shape[1]
    per = lp // n_meta
    at = pad // n_meta

    def body(g_ref, o_ref):
        @pl.when(pl.program_id(0) == 0)
        def _():
            o_ref[...] = jnp.zeros_like(o_ref)

        o_ref[...] += g_ref[...]

    return pl.pallas_call(
        body, name=name, grid=(n_batch,),
        in_specs=[pl.BlockSpec((n_meta, d), lambda b: (b * per + at, 0))],
        out_specs=pl.BlockSpec((n_meta, d), lambda b: (0, 0)),
        out_shape=jax.ShapeDtypeStruct((n_meta, d), F32),
        compiler_params=_params(("arbitrary",)),
    )(g_h0)


def _s5_param_fn(lr, li, ldt, br, bi):
    dt = jnp.exp(ldt)
    e = jnp.exp(lr * dt)
    w = li * dt
    lbr = e * jnp.cos(w)
    lbi = e * jnp.sin(w)
    nr = lbr - 1.0
    den = lr * lr + li * li
    cr = (nr * lr + lbi * li) / den
    ci = (lbi * lr - nr * li) / den
    bbr = cr[:, None, :] * br - ci[:, None, :] * bi
    bbi = cr[:, None, :] * bi + ci[:, None, :] * br
    return lbr, lbi, bbr, bbi


def _s5_params(name, lr, li, ldt, br, bi):
    def body(lr_ref, li_ref, ldt_ref, br_ref, bi_ref, o1, o2, o3, o4):
        res = _s5_param_fn(lr_ref[...], li_ref[...], ldt_ref[...], br_ref[...], bi_ref[...])
        for o, v in zip((o1, o2, o3, o4), res):
            o[...] = v

    shp = [jax.ShapeDtypeStruct(lr.shape, F32)] * 2 + [jax.ShapeDtypeStruct(br.shape, F32)] * 2
    return pl.pallas_call(body, name=name, out_shape=shp)(lr, li, ldt, br, bi)


def _s5_params_bwd(name, lr, li, ldt, br, bi, g_lbr, g_lbi, g_bbr, g_bbi, gd_parts):
    def body(lr_ref, li_ref, ldt_ref, br_ref, bi_ref, g1, g2, g3, g4, gd_ref, o1, o2, o3, o4, o5, o6):
        _, vjp = jax.vjp(_s5_param_fn, lr_ref[...], li_ref[...], ldt_ref[...], br_ref[...], bi_ref[...])
        res = vjp((jnp.sum(g1[...], axis=0), jnp.sum(g2[...], axis=0), g3[...], g4[...]))
        for o, v in zip((o1, o2, o3, o4, o5), res):
            o[...] = v
        o6[...] = jnp.sum(gd_ref[...], axis=0)

    shp = ([jax.ShapeDtypeStruct(lr.shape, F32)] * 2 + [jax.ShapeDtypeStruct(ldt.shape, F32)]
           + [jax.ShapeDtypeStruct(br.shape, F32)] * 2 + [jax.ShapeDtypeStruct(gd_parts.shape[1:], F32)])
    return pl.pallas_call(body, name=name, out_shape=shp)(lr, li, ldt, br, bi, g_lbr, g_lbi, g_bbr, g_bbi, gd_parts)


def _interleave(re, im, w):
    nj = re.shape[-1] // w
    return jnp.concatenate([x[..., j * w:(j + 1) * w] for j in range(nj) for x in (re, im)], axis=-1)


def _deinterleave(x, w):
    nj = x.shape[-1] // (2 * w)
    return (jnp.concatenate([x[..., 2 * j * w:(2 * j + 1) * w] for j in range(nj)], axis=-1),
            jnp.concatenate([x[..., (2 * j + 1) * w:(2 * j + 2) * w] for j in range(nj)], axis=-1))


def _cmul(ar, ai, br, bi):
    return ar * br - ai * bi, ar * bi + ai * br


def _powers(lr, li):
    p = [(lr, li)]
    p.append(_cmul(*p[0], *p[0]))
    p.append(_cmul(*p[1], *p[0]))
    p.append(_cmul(*p[1], *p[1]))
    p.append(_cmul(*p[3], *p[0]))
    p.append(_cmul(*p[3], *p[1]))
    p.append(_cmul(*p[3], *p[2]))
    p.append(_cmul(*p[3], *p[3]))
    return p


def _scan_tile(xr, xi, steps):
    for sh, br, bi, m in steps:
        rr = jnp.where(m, pltpu.roll(xr, sh, 0), 0.0)
        ri = jnp.where(m, pltpu.roll(xi, sh, 0), 0.0)
        xr, xi = xr + (br * rr - bi * ri), xi + (br * ri + bi * rr)
    return xr, xi


def _s5_scan(name, bu, lam, n_batch, lp, w):
    rows, two_ns = bu.shape
    nj = two_ns // (2 * w)

    def body(x_ref, lam_ref, s_ref):
        pw = _powers(lam_ref[:, :w], lam_ref[:, w:])
        tab_r = jnp.concatenate([p[0] for p in pw], axis=0)
        tab_i = jnp.concatenate([p[1] for p in pw], axis=0)
        row = lax.broadcasted_iota(jnp.int32, (SUBLANES, w), 0)
        steps = [(s, jnp.broadcast_to(pw[s - 1][0], (SUBLANES, w)), jnp.broadcast_to(pw[s - 1][1], (SUBLANES, w)),
                  row >= s) for s in (1, 2, 4)]

        def packed_tile(t, carry):
            cr, ci = carry
            r0 = pl.multiple_of(t * PACKED_ROWS, PACKED_ROWS)
            x = x_ref[pl.ds(r0, PACKED_ROWS), :].astype(F32)
            done = []
            for half in range(PACKED_ROWS // SUBLANES):
                xt = x[half * SUBLANES:(half + 1) * SUBLANES, :]
                xr, xi = _scan_tile(xt[:, :w], xt[:, w:], steps)
                sr = xr + (tab_r * cr - tab_i * ci)
                si = xi + (tab_r * ci + tab_i * cr)
                done.append(jnp.concatenate([sr, si], axis=1))
                cr, ci = sr[SUBLANES - 1:, :], si[SUBLANES - 1:, :]
            s_ref[pl.ds(r0, PACKED_ROWS), :] = jnp.concatenate(done, axis=0).astype(s_ref.dtype)
            return cr, ci

        zero = jnp.zeros((1, w), F32)
        _unrolled_loop(lp // PACKED_ROWS, packed_tile, (zero, zero), SCAN_UNROLL)

    spec = pl.BlockSpec((lp, 2 * w), lambda b, j: (b, j))
    return pl.pallas_call(
        body, name=name, grid=(n_batch, nj), in_specs=[spec, pl.BlockSpec((1, 2 * w), lambda b, j: (0, j))],
        out_specs=spec, out_shape=jax.ShapeDtypeStruct((rows, two_ns), BF16),
        compiler_params=_params(("parallel", "parallel")),
    )(bu, lam)


def _s5_scan_bwd(name, gd, states, lam, n_batch, lp, w):
    rows, two_ns = gd.shape
    nj = two_ns // (2 * w)

    def body(x_ref, s_ref, lam_ref, g_ref, gl_ref):
        pw = _powers(lam_ref[:, :w], -lam_ref[:, w:])
        tab_r = jnp.concatenate([p[0] for p in reversed(pw)], axis=0)
        tab_i = jnp.concatenate([p[1] for p in reversed(pw)], axis=0)
        row = lax.broadcasted_iota(jnp.int32, (SUBLANES, w), 0)
        steps = [(SUBLANES - s, jnp.broadcast_to(pw[s - 1][0], (SUBLANES, w)),
                  jnp.broadcast_to(pw[s - 1][1], (SUBLANES, w)), row < SUBLANES - s) for s in (1, 2, 4)]

        n_packed = lp // PACKED_ROWS
        halves = PACKED_ROWS // SUBLANES

        def packed_tile(u, carry):
            cr, ci, ar, ai = carry
            t = n_packed - 1 - u
            r0 = pl.multiple_of(t * PACKED_ROWS, PACKED_ROWS)
            x = x_ref[pl.ds(r0, PACKED_ROWS), :].astype(F32)
            cur = s_ref[pl.ds(r0, PACKED_ROWS), :].astype(F32)
            p0 = pl.multiple_of(jnp.maximum(t - 1, 0) * PACKED_ROWS, PACKED_ROWS)
            before = s_ref[pl.ds(p0, PACKED_ROWS), :].astype(F32)[PACKED_ROWS - 1:, :] * jnp.where(t > 0, 1.0, 0.0)
            done = [None] * halves
            for half in reversed(range(halves)):
                rows_h = slice(half * SUBLANES, (half + 1) * SUBLANES)
                xt, st = x[rows_h, :], cur[rows_h, :]
                xr, xi = _scan_tile(xt[:, :w], xt[:, w:], steps)
                gr = xr + (tab_r * cr - tab_i * ci)
                gi = xi + (tab_r * ci + tab_i * cr)
                done[half] = jnp.concatenate([gr, gi], axis=1)
                prev = before if half == 0 else cur[half * SUBLANES - 1:half * SUBLANES, :]
                spr = jnp.where(row >= 1, pltpu.roll(st[:, :w], 1, 0), prev[:, :w])
                spi = jnp.where(row >= 1, pltpu.roll(st[:, w:], 1, 0), prev[:, w:])
                cr, ci, ar, ai = gr[:1, :], gi[:1, :], ar + gr * spr + gi * spi, ai + gi * spr - gr * spi
            g_ref[pl.ds(r0, PACKED_ROWS), :] = jnp.concatenate(done, axis=0).astype(g_ref.dtype)
            return cr, ci, ar, ai

        z1 = jnp.zeros((1, w), F32)
        z8 = jnp.zeros((SUBLANES, w), F32)
        _, _, ar, ai = _unrolled_loop(n_packed, packed_tile, (z1, z1, z8, z8), SCAN_UNROLL)
        gl_ref[...] = jnp.concatenate([jnp.sum(ar, axis=0, keepdims=True), jnp.sum(ai, axis=0, keepdims=True)], axis=1)

    spec = pl.BlockSpec((lp, 2 * w), lambda b, j: (b, j))
    return pl.pallas_call(
        body, name=name, grid=(n_batch, nj),
        in_specs=[spec, spec, pl.BlockSpec((1, 2 * w), lambda b, j: (0, j))],
        out_specs=[spec, pl.BlockSpec((None, 1, 2 * w), lambda b, j: (b, 0, j))],
        out_shape=[jax.ShapeDtypeStruct((rows, two_ns), BF16), jax.ShapeDtypeStruct((n_batch, 1, two_ns), F32)],
        compiler_params=_params(("parallel", "parallel")),
    )(gd, states, lam)


def _log_sigmoid(z):
    return jnp.minimum(z, 0.0) - jnp.log(1.0 + jnp.exp(-jnp.abs(z)))


ATTN_KEYS = 256
ATTN_GROUP = 4


def _attn_block(i, jb, lp, pad):
    start = jb * ATTN_KEYS
    r0 = pl.multiple_of(jnp.minimum(start, lp - ATTN_KEYS), LANES)
    rowpos = i * LANES + lax.broadcasted_iota(jnp.int32, (LANES, ATTN_KEYS), 0)
    keypos = r0 + lax.broadcasted_iota(jnp.int32, (LANES, ATTN_KEYS), 1)
    return r0, (keypos < rowpos) & (keypos >= jnp.maximum(start, pad))


def _tri_ones(strict_upper):
    r = lax.broadcasted_iota(jnp.int32, (ATTN_KEYS, ATTN_KEYS + LANES), 0)
    c = lax.broadcasted_iota(jnp.int32, (ATTN_KEYS, ATTN_KEYS + LANES), 1)
    tri = (r > c) if strict_upper else (r < c)
    return jnp.where((c >= ATTN_KEYS) | tri, 1.0, 0.0).astype(BF16)


def _split_sums(cr):
    rs = cr[:, ATTN_KEYS:]
    return cr[:, :ATTN_KEYS], jnp.concatenate([rs] * (ATTN_KEYS // LANES), axis=1)


def _head_masks():
    lane = lax.broadcasted_iota(jnp.int32, (1, LANES), 1)
    return [lane < SB_HEAD_DIM, lane >= SB_HEAD_DIM]


def _run_groups(n, first, sign, make):
    j, left, g = first, n, ATTN_GROUP
    while g >= 1:
        shift = g.bit_length() - 1
        count = lax.shift_right_logical(left, shift)
        fn = make(g)

        def loop(_, jcur, fn=fn, g=g):
            fn(jcur)
            return jcur + sign * g

        j = lax.fori_loop(0, count, loop, j)
        left = left - lax.shift_left(count, shift)
        g //= 2


def _attn_fwd(name, proj, n_batch, lp, pad, q_cb, k_cb, v_cb, n_pairs):
    rows = proj.shape[0]
    nq = lp // LANES
    scale = SB_HEAD_DIM ** -0.5

    def body(q_ref, k_ref, v_ref, o_ref, acc_s):
        i = pl.program_id(1)
        hm = _head_masks()
        comb = _tri_ones(True)
        n_blocks = lax.shift_right_logical(i + ATTN_KEYS // LANES, (ATTN_KEYS // LANES).bit_length() - 1)

        def pair(hp, carry):
            lanes = pl.ds(pl.multiple_of(hp * LANES, LANES), LANES)
            qs = q_ref[:, lanes] * scale
            qh = [jnp.where(m, qs, 0.0).astype(BF16) for m in hm]
            acc_s[...] = jnp.zeros_like(acc_s)
            o_ref[:, lanes] = jnp.zeros((LANES, LANES), F32)

            def make(group):
                def fn(jtop):
                    chains = []
                    for g in range(group):
                        r0, vis = _attn_block(i, jtop - g, lp, pad)
                        kj = k_ref[pl.ds(r0, ATTN_KEYS), lanes].astype(BF16)
                        vj = v_ref[pl.ds(r0, ATTN_KEYS), lanes]
                        for h in range(2):
                            z = lax.dot_general(qh[h], kj, NT, preferred_element_type=F32)
                            chains.append((h, vis, z, jnp.where(hm[h], vj, 0.0).astype(BF16)))
                    staged = []
                    for h, vis, z, vh in chains:
                        lsz = _log_sigmoid(z)
                        staged.append((h, vis, lsz, _running_sums(jnp.where(vis, lsz - z, 0.0), comb, split=True), vh))
                    out = o_ref[:, lanes]
                    for h, vis, lsz, cr, vh in staged:
                        later, rs = _split_sums(cr)
                        acc = acc_s[h]
                        wgt = jnp.where(vis, jnp.exp(lsz + later + acc), 0.0)
                        acc_s[h] = acc + rs
                        out = out + lax.dot_general(wgt.astype(BF16), vh, NN, preferred_element_type=F32)
                    o_ref[:, lanes] = out
                return fn

            _run_groups(n_blocks, n_blocks - 1, -1, make)
            return carry

        lax.fori_loop(0, n_pairs, pair, 0)

    wide = n_pairs * LANES
    assert q_cb % n_pairs == 0 and k_cb % n_pairs == 0 and v_cb % n_pairs == 0
    return pl.pallas_call(
        body, name=name, grid=(n_batch, nq),
        in_specs=[pl.BlockSpec((LANES, wide), lambda b, i: (b * nq + i, q_cb // n_pairs)),
                  pl.BlockSpec((lp, wide), lambda b, i: (b, k_cb // n_pairs)),
                  pl.BlockSpec((lp, wide), lambda b, i: (b, v_cb // n_pairs))],
        out_specs=pl.BlockSpec((LANES, wide), lambda b, i: (b * nq + i, 0)),
        out_shape=jax.ShapeDtypeStruct((rows, wide), F32),
        scratch_shapes=[pltpu.VMEM((2, LANES, ATTN_KEYS), F32)],
        compiler_params=_params(("parallel", "arbitrary")),
    )(proj, proj, proj)


def _attn_bwd(name, proj, g_out, n_batch, lp, pad, q_cb, k_cb, v_cb, go_cb, n_pairs):
    rows = proj.shape[0]
    nq = lp // LANES
    scale = SB_HEAD_DIM ** -0.5

    def body(q_ref, k_ref, v_ref, go_ref, gq_ref, gk_ref, gv_ref, ga_s, sz_s, acc_s):
        i = pl.program_id(1)

        @pl.when(i == 0)
        def _():
            gk_ref[...] = jnp.zeros_like(gk_ref)
            gv_ref[...] = jnp.zeros_like(gv_ref)

        hm = _head_masks()
        comb_up = _tri_ones(True)
        comb_lo = _tri_ones(False)
        n_blocks = lax.shift_right_logical(i + ATTN_KEYS // LANES, (ATTN_KEYS // LANES).bit_length() - 1)

        def pair(hp, carry):
            lanes = pl.ds(pl.multiple_of(hp * LANES, LANES), LANES)
            qs = q_ref[:, lanes] * scale
            go = go_ref[:, lanes]
            qh = [jnp.where(m, qs, 0.0).astype(BF16) for m in hm]
            goh = [jnp.where(m, go, 0.0).astype(BF16) for m in hm]
            acc_s[...] = jnp.zeros_like(acc_s)

            def make_down(group):
                def fn(jtop):
                    chains = []
                    for g in range(group):
                        j = jtop - g
                        r0, vis = _attn_block(i, j, lp, pad)
                        kj = k_ref[pl.ds(r0, ATTN_KEYS), lanes].astype(BF16)
                        vj = v_ref[pl.ds(r0, ATTN_KEYS), lanes].astype(BF16)
                        for h in range(2):
                            z = lax.dot_general(qh[h], kj, NT, preferred_element_type=F32)
                            gw = lax.dot_general(goh[h], vj, NT, preferred_element_type=F32)
                            chains.append((h, j, r0, vis, z, gw))
                    staged = []
                    for h, j, r0, vis, z, gw in chains:
                        lsz = _log_sigmoid(z)
                        staged.append((h, j, r0, vis, lsz, _running_sums(jnp.where(vis, lsz - z, 0.0), comb_up), gw))
                    for h, j, r0, vis, lsz, cr, gw in staged:
                        later, rs = _split_sums(cr)
                        acc = acc_s[h]
                        wgt = jnp.where(vis, jnp.exp(lsz + later + acc), 0.0)
                        acc_s[h] = acc + rs
                        ga_s[h, j] = gw * wgt
                        sz_s[h, j] = jnp.exp(lsz)
                        gv_ref[pl.ds(r0, ATTN_KEYS), lanes] += lax.dot_general(
                            wgt.astype(BF16), goh[h], TN, preferred_element_type=F32)
                return fn

            _run_groups(n_blocks, n_blocks - 1, -1, make_down)
            acc_s[...] = jnp.zeros_like(acc_s)

            def make_up(group):
                def fn(jbot):
                    pend = []
                    for g in range(group):
                        j = jbot + g
                        r0, vis = _attn_block(i, j, lp, pad)
                        kj = k_ref[pl.ds(r0, ATTN_KEYS), lanes]
                        for h in range(2):
                            ga = ga_s[h, j]
                            pend.append((h, j, r0, vis, ga, _running_sums(ga, comb_lo),
                                         jnp.where(hm[h], kj, 0.0).astype(BF16)))
                    gq = jnp.zeros((LANES, LANES), F32)
                    for h, j, r0, vis, ga, cr, kh in pend:
                        before, rs = _split_sums(cr)
                        pre = acc_s[h]
                        glk = before + pre
                        acc_s[h] = pre + rs
                        sz = sz_s[h, j]
                        gz = jnp.where(vis, ga * (1.0 - sz) - glk * sz, 0.0).astype(BF16)
                        gq = gq + lax.dot_general(gz, kh, NN, preferred_element_type=F32)
                        gk_ref[pl.ds(r0, ATTN_KEYS), lanes] += lax.dot_general(gz, qh[h], TN, preferred_element_type=F32)
                    gq_ref[:, lanes] += gq * scale
                return fn

            gq_ref[:, lanes] = jnp.zeros((LANES, LANES), F32)
            _run_groups(n_blocks, 0, 1, make_up)
            return carry

        lax.fori_loop(0, n_pairs, pair, 0)

    wide = n_pairs * LANES
    assert q_cb % n_pairs == 0 and k_cb % n_pairs == 0 and v_cb % n_pairs == 0 and go_cb % n_pairs == 0
    blk = lambda cb: pl.BlockSpec((LANES, wide), lambda b, i: (b * nq + i, cb // n_pairs))
    full = lambda cb: pl.BlockSpec((lp, wide), lambda b, i: (b, cb // n_pairs))
    shp = jax.ShapeDtypeStruct((rows, wide), F32)
    per_block = pltpu.VMEM((2, -(-lp // ATTN_KEYS), LANES, ATTN_KEYS), F32)
    return pl.pallas_call(
        body, name=name, grid=(n_batch, nq),
        in_specs=[blk(q_cb), full(k_cb), full(v_cb), blk(go_cb)],
        out_specs=[blk(0), full(0), full(0)], out_shape=[shp, shp, shp],
        scratch_shapes=[per_block, per_block, pltpu.VMEM((2, LANES, ATTN_KEYS), F32)],
        compiler_params=_params(("parallel", "arbitrary")),
    )(proj, proj, proj, g_out)


def _lb_fn(gamma):
    g0, g1 = gamma[0:1, :], gamma[1:2, :]
    mx = jnp.maximum(g0, g1)
    e0, e1 = jnp.exp(g0 - mx), jnp.exp(g1 - mx)
    p0, p1 = e0 / (e0 + e1), e1 / (e0 + e1)
    return (p0 + p1) - p0


def _lower_bound(name, gamma):
    def body(g_ref, o_ref):
        o_ref[...] = _lb_fn(g_ref[...])

    return pl.pallas_call(body, name=name, out_shape=jax.ShapeDtypeStruct((1, gamma.shape[1]), F32))(gamma)


def _lower_bound_bwd(name, gamma, g_lb_parts, g_ng_parts):
    def body(g_ref, glb_ref, gng_ref, o_ref, o2_ref):
        _, vjp = jax.vjp(_lb_fn, g_ref[...])
        o_ref[...] = vjp(jnp.sum(glb_ref[...], axis=0))[0]
        o2_ref[...] = jnp.sum(gng_ref[...], axis=0)

    return pl.pallas_call(
        body, name=name,
        out_shape=[jax.ShapeDtypeStruct(gamma.shape, F32), jax.ShapeDtypeStruct((1, gamma.shape[1]), F32)],
    )(gamma, g_lb_parts, g_ng_parts)


def _tri_times(tril, x, dims):
    hi = x.astype(BF16)
    lo = (x - hi.astype(F32)).astype(BF16)
    t = tril.astype(BF16)
    return (lax.dot_general(t, hi, dims, preferred_element_type=F32)
            + lax.dot_general(t, lo, dims, preferred_element_type=F32))


@jax.custom_vjp
def _cumsum_rows(x, tril):
    return _tri_times(tril, x, NN)


def _cumsum_rows_fwd(x, tril):
    return _tri_times(tril, x, NN), tril


def _cumsum_rows_bwd(tril, g):
    return _tri_times(tril, g, TN), jnp.zeros_like(tril)


_cumsum_rows.defvjp(_cumsum_rows_fwd, _cumsum_rows_bwd)


def _hg_decays(f_pre, lbs, masks, tril):
    f = [[lb + (1.0 - lb) * jax.nn.sigmoid(fc) for fc, lb in zip(row, lbs)] for row in f_pre]
    bcum = [[_cumsum_rows(jnp.log(x) * m, tril) for x in row] for row, m in zip(f, masks)]
    return [[1.0 - x for x in row] for row in f], bcum


def _hg_step(q, f_pre, i_in, g, lbs, ngs, sts, masks, tril):
    k, bcum = _hg_decays(f_pre, lbs, masks, tril)
    v = [[ic * m for ic in row] for row, m in zip(i_in, masks)]
    qd = [[qc * jnp.exp(b) for qc, b in zip(qr, br)] for qr, br in zip(q, bcum)]
    scores = [[jnp.where(tril > 0.5, _dot(a, kk * jnp.exp(-b), NT), 0.0) for a, kk, b in zip(ar, kr, br)]
              for ar, kr, br in zip(qd, k, bcum)]
    inner = [[_dot(s, x, NN) for s, x in zip(sr, vr)] for sr, vr in zip(scores, v)]
    add = [[_dot(x, kk * jnp.exp(b[HG_CHUNK - 1:, :] - b), TN) for x, kk, b in zip(vr, kr, br)]
           for vr, kr, br in zip(v, k, bcum)]
    outs = []
    for qr, br, nr, ar, gr in zip(qd, bcum, inner, add, g):
        o = [n + _dot(a, st, NT) for n, a, st in zip(nr, qr, sts)]
        sts = [jnp.exp(b[HG_CHUNK - 1:, :]) * st + a for b, a, st in zip(br, ar, sts)]
        o = [x * lax.rsqrt(jnp.mean(x * x, axis=-1, keepdims=True) + RMS_EPS) * ng for x, ng in zip(o, ngs)]
        outs.append([x * (gc * jax.nn.sigmoid(gc)) for x, gc in zip(o, gr)])
    return outs, sts


def _hg_consts(c, pad):
    r = lax.broadcasted_iota(jnp.int32, (HG_CHUNK, HG_CHUNK), 0)
    cc = lax.broadcasted_iota(jnp.int32, (HG_CHUNK, HG_CHUNK), 1)
    tril = jnp.where(r >= cc, 1.0, 0.0).astype(F32)
    pos = c * HG_CHUNK + lax.broadcasted_iota(jnp.int32, (HG_CHUNK, 1), 0)
    return tril, jnp.where(pos >= pad, 1.0, 0.0).astype(F32)


HG_HEADS_PER_STEP = 4
HG_CHUNKS_PER_STEP = 2


def _hg_layout(lp, n_heads):
    step_rows = HG_CHUNKS_PER_STEP * HG_CHUNK
    per = min(HG_HEADS_PER_STEP, n_heads)
    assert lp % step_rows == 0 and n_heads % per == 0
    heads = [(h, slice(h * HG_DK, (h + 1) * HG_DK)) for h in range(per)]
    return n_heads // per, lp // step_rows, step_rows, per * HG_DK, heads


def _hg_step_views(step, pad, heads):
    slices = [slice(u * HG_CHUNK, (u + 1) * HG_CHUNK) for u in range(HG_CHUNKS_PER_STEP)]
    consts = [_hg_consts(step * HG_CHUNKS_PER_STEP + u, pad) for u in range(HG_CHUNKS_PER_STEP)]
    load = lambda ref: [[ref[sl, cols] for _, cols in heads] for sl in slices]
    return slices, [m for _, m in consts], consts[0][0], load


def _hgrn_fwd(name, proj, lb, ng, n_batch, lp, pad, n_heads):
    rows = proj.shape[0]
    groups, steps, step_rows, wide, heads = _hg_layout(lp, n_heads)

    def body(q_ref, f_ref, i_ref, g_ref, lb_ref, ng_ref, o_ref, s_ref, st_s):
        t = pl.program_id(2)

        @pl.when(t == 0)
        def _():
            st_s[...] = jnp.zeros_like(st_s)

        slices, masks, tril, load = _hg_step_views(t, pad, heads)
        sts = [st_s[h] for h, _ in heads]
        for (_, cols), st in zip(heads, sts):
            s_ref[:, cols] = st
        outs, sts = _hg_step(load(q_ref), load(f_ref), load(i_ref), load(g_ref),
                             [lb_ref[:, cols] for _, cols in heads], [ng_ref[:, cols] for _, cols in heads],
                             sts, masks, tril)
        for sl, row in zip(slices, outs):
            for (_, cols), o in zip(heads, row):
                o_ref[sl, cols] = o
        for (h, _), st in zip(heads, sts):
            st_s[h] = st

    col = lambda off: pl.BlockSpec((step_rows, wide), lambda b, h, t: (b * steps + t, off * groups + h))
    vec = pl.BlockSpec((1, wide), lambda b, h, t: (0, h))
    return pl.pallas_call(
        body, name=name, grid=(n_batch, groups, steps), in_specs=[col(0), col(1), col(2), col(3), vec, vec],
        out_specs=[col(0), pl.BlockSpec((HG_DK, wide), lambda b, h, t: (b * steps + t, h))],
        out_shape=[jax.ShapeDtypeStruct((rows, n_heads * HG_DK), F32),
                   jax.ShapeDtypeStruct((n_batch * steps * HG_DK, n_heads * HG_DK), F32)],
        scratch_shapes=[pltpu.VMEM((len(heads), HG_DK, HG_DK), F32)],
        compiler_params=_params(("parallel", "parallel", "arbitrary")),
    )(proj, proj, proj, proj, lb, ng)


def _hgrn_bwd(name, proj, lb, ng, g_out, states, n_batch, lp, pad, n_heads):
    rows = proj.shape[0]
    width = n_heads * HG_DK
    groups, steps, step_rows, wide, heads = _hg_layout(lp, n_heads)

    def body(q_ref, f_ref, i_ref, g_ref, lb_ref, ng_ref, go_ref, s_ref, gq_ref, gf_ref, gi_ref, gg_ref, glb_ref, gng_ref, gst_s):
        t = pl.program_id(2)

        @pl.when(t == 0)
        def _():
            gst_s[...] = jnp.zeros_like(gst_s)
            glb_ref[...] = jnp.zeros_like(glb_ref)
            gng_ref[...] = jnp.zeros_like(gng_ref)

        slices, masks, tril, load = _hg_step_views(steps - 1 - t, pad, heads)
        fn = functools.partial(_hg_step, masks=masks, tril=tril)
        _, vjp = jax.vjp(fn, load(q_ref), load(f_ref), load(i_ref), load(g_ref),
                         [lb_ref[:, cols] for _, cols in heads], [ng_ref[:, cols] for _, cols in heads],
                         [s_ref[:, cols] for _, cols in heads])
        gq, gf, gi, gg, glb, gng, gst = vjp((load(go_ref), [gst_s[h] for h, _ in heads]))
        for ref, grads in ((gq_ref, gq), (gf_ref, gf), (gi_ref, gi), (gg_ref, gg)):
            for sl, row in zip(slices, grads):
                for (_, cols), x in zip(heads, row):
                    ref[sl, cols] = x.astype(BF16)
        for (h, cols), a, b, c in zip(heads, gst, glb, gng):
            gst_s[h] = a
            glb_ref[:, cols] += b
            gng_ref[:, cols] += c

    col = lambda off: pl.BlockSpec((step_rows, wide), lambda b, h, t: (b * steps + steps - 1 - t, off * groups + h))
    vec = pl.BlockSpec((1, wide), lambda b, h, t: (0, h))
    part = pl.BlockSpec((None, 1, wide), lambda b, h, t: (b, 0, h))
    big = jax.ShapeDtypeStruct((rows, width), BF16)
    small = jax.ShapeDtypeStruct((n_batch, 1, width), F32)
    return pl.pallas_call(
        body, name=name, grid=(n_batch, groups, steps),
        in_specs=[col(0), col(1), col(2), col(3), vec, vec, col(0),
                  pl.BlockSpec((HG_DK, wide), lambda b, h, t: (b * steps + steps - 1 - t, h))],
        out_specs=[col(0), col(0), col(0), col(0), part, part],
        out_shape=[big, big, big, big, small, small],
        scratch_shapes=[pltpu.VMEM((len(heads), HG_DK, HG_DK), F32)],
        compiler_params=_params(("parallel", "parallel", "arbitrary")),
    )(proj, proj, proj, proj, lb, ng, g_out, states)


def _exchange_copies(src, dst, send, recv, loc, scatter):
    x, y, c = lax.axis_index("x"), lax.axis_index("y"), lax.axis_index("c")
    me = 4 * x + 2 * y + c
    local, remote = [], []
    for w in range(len(src)):
        local.append(pltpu.make_async_copy(src[w].at[me] if scatter else src[w], dst[w].at[me], loc.at[w]))
    for k in range(1, N_DEV):
        px = 1 - x if k & 4 else x
        py = 1 - y if k & 2 else y
        pc = 1 - c if k & 1 else c
        peer = 4 * px + 2 * py + pc
        for w in range(len(src)):
            remote.append(pltpu.make_async_remote_copy(
                src_ref=src[w].at[peer] if scatter else src[w], dst_ref=dst[w].at[me],
                send_sem=send.at[w * (N_DEV - 1) + k - 1], recv_sem=recv.at[w * (N_DEV - 1) + k - 1],
                device_id=(px, py, pc), device_id_type=pl.DeviceIdType.MESH))
    return local, remote


_HBM_SPEC = pl.BlockSpec(memory_space=pltpu.HBM)
_SEM_SPEC = pl.BlockSpec(memory_space=pltpu.SEMAPHORE)
_ANY_SPEC = pl.BlockSpec(memory_space=pl.ANY)
_DATAFLOW = pltpu.SideEffectType.DATAFLOW_SIDE_EFFECTING


def _exchange_start(name, srcs, scatter, dep=None):
    nw = len(srcs)
    srcs = [pltpu.with_memory_space_constraint(s, pltpu.HBM) for s in srcs]
    lands = [pltpu.with_memory_space_constraint(lax.empty(s.shape if scatter else (N_DEV,) + s.shape, s.dtype), pltpu.HBM)
             for s in srcs]
    deps = [] if dep is None else [dep]

    def body(*refs):
        src, dst = refs[:nw], refs[nw:2 * nw]
        send, recv, loc = refs[2 * nw + len(deps):2 * nw + len(deps) + 3]
        token = refs[-1]
        local, remote = _exchange_copies(src, dst, send, recv, loc, scatter)
        for cp in local + remote:
            cp.start()
        token[...] = jnp.zeros_like(token)

    sems = [pltpu.SemaphoreType.DMA((nw * (N_DEV - 1),)), pltpu.SemaphoreType.DMA((nw * (N_DEV - 1),)),
            pltpu.SemaphoreType.DMA((nw,))]
    out = pl.pallas_call(
        body, name=name,
        out_shape=(*sems, *[pltpu.HBM(s.shape, s.dtype) for s in srcs], *[pltpu.HBM(s.shape, s.dtype) for s in lands],
                   jax.ShapeDtypeStruct((SUBLANES, LANES), F32)),
        in_specs=[_HBM_SPEC] * (2 * nw) + [_ANY_SPEC] * len(deps),
        out_specs=(_SEM_SPEC, _SEM_SPEC, _SEM_SPEC, *[_HBM_SPEC] * (2 * nw), pl.BlockSpec(memory_space=pltpu.VMEM)),
        input_output_aliases={i: 3 + i for i in range(2 * nw)},
        compiler_params=pltpu.CompilerParams(has_side_effects=_DATAFLOW),
    )(*srcs, *lands, *deps)
    return {"sems": out[:3], "srcs": out[3:3 + nw], "lands": out[3 + nw:3 + 2 * nw], "token": out[-1], "scatter": scatter}


def _exchange_wait(name, handle, after):
    nw = len(handle["srcs"])
    scatter = handle["scatter"]

    def body(*refs):
        src, dst = refs[:nw], refs[nw:2 * nw]
        send, recv, loc = refs[2 * nw:2 * nw + 3]
        local, remote = _exchange_copies(src, dst, send, recv, loc, scatter)
        for cp in local:
            cp.wait()
        for cp in remote:
            cp.wait_send()
            cp.wait_recv()

    out = pl.pallas_call(
        body, name=name,
        out_shape=(*[pltpu.HBM(s.shape, s.dtype) for s in handle["srcs"]],
                   *[pltpu.HBM(s.shape, s.dtype) for s in handle["lands"]]),
        in_specs=[_HBM_SPEC] * (2 * nw) + [_SEM_SPEC] * 3 + [_ANY_SPEC],
        out_specs=tuple([_HBM_SPEC] * (2 * nw)),
        input_output_aliases={i: i for i in range(2 * nw)},
        compiler_params=pltpu.CompilerParams(has_side_effects=_DATAFLOW),
    )(*handle["srcs"], *handle["lands"], *handle["sems"], after)
    return list(out[nw:])


def _adamw(w, g, m, v):
    m = ADAM_B1 * m + (1.0 - ADAM_B1) * g
    v = ADAM_B2 * v + (1.0 - ADAM_B2) * (g * g)
    m_hat = m / (1.0 - ADAM_B1 ** ADAM_STEP)
    v_hat = v / (1.0 - ADAM_B2 ** ADAM_STEP)
    delta = -ADAM_LR * (m_hat / (jnp.sqrt(v_hat) + ADAM_EPS) + ADAM_WD * w)
    return delta, m, v


def _adamw_summed(name, parts, w, m, v):
    layered = w.ndim == 3
    parts = list(parts) if layered else [parts]
    n_layers = len(parts)
    rows, cols = w.shape[-2:]
    n_parts = parts[0].shape[0]
    tr = _tile(rows, max(SUBLANES, (1 << 18) // cols))

    def body(*refs):
        p_refs = refs[:n_layers]
        w_ref, m_ref, v_ref, g_ref, d_ref, nm_ref, nv_ref = refs[n_layers:]
        layer = pl.program_id(0)

        def run(p_ref):
            g = p_ref[0].astype(F32)
            for s in range(1, n_parts):
                g = g + p_ref[s].astype(F32)
            d, nm, nv = _adamw(w_ref[...], g, m_ref[...], v_ref[...])
            g_ref[...] = g
            d_ref[...] = d
            nm_ref[...] = nm
            nv_ref[...] = nv

        for l in range(n_layers):
            pl.when(layer == l)(functools.partial(run, p_refs[l]))

    if layered:
        spec = pl.BlockSpec((None, tr, cols), lambda l, i: (l, i, 0))
    else:
        spec = pl.BlockSpec((tr, cols), lambda l, i: (i, 0))
    p_specs = [pl.BlockSpec((n_parts, tr, cols), lambda l, i, q=q: (0, jnp.where(l == q, i, 0), 0))
               for q in range(n_layers)]
    shp = jax.ShapeDtypeStruct(w.shape, F32)
    return pl.pallas_call(
        body, name=name, grid=(n_layers, rows // tr), in_specs=[*p_specs, spec, spec, spec],
        out_specs=[spec] * 4, out_shape=[shp] * 4, compiler_params=_params(("parallel", "parallel")),
    )(*parts, w, m, v)


def _pack_rows(arrays, cols):
    out = []
    for a in arrays:
        flat = a.reshape(-1)
        n = -(-flat.shape[0] // cols) * cols
        out.append(jnp.pad(flat, (0, n - flat.shape[0])).reshape(-1, cols))
    packed = jnp.concatenate(out, axis=0)
    return jnp.pad(packed, ((0, -packed.shape[0] % SUBLANES), (0, 0)))


def _unpack_rows(packed, shapes, cols):
    out, r = [], 0
    for s in shapes:
        n = math.prod(s)
        nr = -(-n // cols)
        out.append(packed[r:r + nr].reshape(-1)[:n].reshape(s))
        r += nr
    return out


def _block_diag(blocks):
    g, a, b = blocks.shape
    eye = jnp.eye(g, dtype=blocks.dtype)
    return (eye[:, None, :, None] * blocks[:, :, None, :]).reshape(g * a, g * b)


def _diag_blocks(dense, g):
    a, b = dense.shape[0] // g, dense.shape[1] // g
    return jnp.einsum("gagb->gab", dense.reshape(g, a, g, b))


def _local_step(x, target, meta, wts, small, late_weights, on_grads, on_small):
    n_batch, seq, d = x.shape
    n_meta = meta.shape[0]
    pad = -(seq + n_meta) % LANES
    lead = pad + n_meta
    lp = lead + seq
    rows = n_batch * lp
    s5w = wts["glu"].shape[0]
    n_ab = wts["in_ab"].shape[2]
    ab_cols = wts["in_ab"].shape[0] * n_ab
    sbw = (ab_cols - s5w) // 3
    dff = small["mlp_b_up"].shape[1]
    n_pairs = sbw // LANES
    n_hg = d // HG_DK
    s5_cb = s5w // LANES
    sb_cb = sbw // LANES
    tm = _tile(rows, ROW_TILE)
    groups, n_state, grp = small["s5_b_re"].shape[1:]
    ns = groups * n_state
    sw = min(SCAN_LANES, ns)

    h0 = jnp.concatenate(
        [jnp.zeros((n_batch, pad, d), F32), jnp.broadcast_to(meta[None], (n_batch, n_meta, d)), x], axis=1
    ).reshape(rows, d)

    lam_re, lam_im = small["s5_lam_re"][0], small["s5_lam_im"][0]
    log_dt = small["s5_log_dt"][0][:, None]
    b_re_t = small["s5_b_re"][0].transpose(0, 2, 1)
    b_im_t = small["s5_b_im"][0].transpose(0, 2, 1)
    c_re, c_im = small["s5_c_re"][0], small["s5_c_im"][0]
    lbr, lbi, bbr, bbi = _s5_params("s5_params", lam_re, lam_im, log_dt, b_re_t, b_im_t)
    b_blk = _interleave(_block_diag(bbr), _block_diag(bbi), sw).astype(BF16)
    c_blk = _interleave(_block_diag(c_re), _block_diag(-c_im), sw).T.astype(BF16)
    lam_row = _interleave(lbr.reshape(1, ns), lbi.reshape(1, ns), sw)
    d_row = small["s5_d"].reshape(1, s5w)

    def ln_store(outs, acc, res, bias, g, b):
        r = ALPHA * res + acc + bias
        outs[0][...] = r
        if len(outs) > 1:
            outs[1][...] = _ln(r, g, b)

    zero_bias = jnp.zeros((1, d), F32)

    def mix_ln(name, a, w, k_total, tk, res, bias, g, b, a_fn=None, emit_h=True):
        n_out = 2 if emit_h else 1
        return _mm_act(name, a, w, "nat", n_out_cols=d, k_total=k_total, tn=d, tk=tk, a_fn=a_fn,
                       extras=(res, bias, g, b), extra_specs=(_row_spec(tm, d), _vec_spec(d), _vec_spec(d), _vec_spec(d)),
                       store=ln_store, out_shape=[jax.ShapeDtypeStruct((rows, d), F32)] * n_out,
                       out_specs=[_row_spec(tm, d)] * n_out)

    def two(width):
        return [jax.ShapeDtypeStruct((rows, width), F32)] * 2, [_row_spec(tm, width)] * 2

    def shard_tile(total, shard, cap=1024):
        t = max(shard, cap - cap % shard)
        while total % t:
            t -= shard
        return t

    proj_ab = _mm_act("in_ab", h0, wts["in_ab"], "stk", n_out_cols=ab_cols, k_total=d, tn=shard_tile(ab_cols, n_ab), tk=d)[0]
    bu = _mm_act("s5_bu", proj_ab, b_blk, "nat", n_out_cols=2 * ns, k_total=s5w, tn=min(2 * ns, 2048), tk=s5w)[0]
    states = _s5_scan("s5_scan", bu, lam_row, n_batch, lp, sw)

    def gelu_store(outs, acc, u, dv):
        ypre = acc + dv * u
        outs[0][...] = ypre
        outs[1][...] = jax.nn.gelu(ypre)

    shp2, spec2 = two(s5w)
    ypre, y = _mm_act(
        "s5_y", states, c_blk, "nat", n_out_cols=s5w, k_total=2 * ns, tn=s5w, tk=min(2 * ns, 1024),
        extras=(proj_ab, d_row), extra_specs=(_row_spec(tm, s5w), _vec_spec(s5w)), store=gelu_store,
        out_shape=shp2, out_specs=spec2)

    def glu_store(outs, acc, yv, bias):
        gate = acc + bias
        outs[0][...] = gate
        outs[1][...] = _glu(yv, gate)

    gate, a_out = _mm_act(
        "s5_glu", y, wts["glu"], "nat", n_out_cols=s5w, k_total=s5w, tn=s5w, tk=s5w,
        extras=(y, small["s5_b_glu"]), extra_specs=(_row_spec(tm, s5w), _vec_spec(s5w)), store=glu_store,
        out_shape=shp2, out_specs=spec2)
    b_out = _attn_fwd("sb_attn", proj_ab, n_batch, lp, pad, s5_cb, s5_cb + sb_cb, s5_cb + 2 * sb_cb, n_pairs)

    def bias_store(outs, acc, bias):
        outs[0][...] = (acc + bias).astype(outs[0].dtype)

    def wide(width, dtype):
        return [jax.ShapeDtypeStruct((rows, dff), dtype)], [_row_spec(tm, width)]

    def mlp_fwd(layer, h_in, emit_h=True):
        tn = shard_tile(dff, n_up)
        shp, spec = wide(tn, BF16)
        up = _mm_act(f"up{layer}", h_in, wts["up"][layer], "stk", n_out_cols=dff, k_total=d, tn=tn, tk=d,
                     extras=(small["mlp_b_up"][layer:layer + 1],), extra_specs=(_vec_spec(tn),), store=bias_store,
                     out_shape=shp, out_specs=spec)[0]
        return (up, *mix_ln(f"down{layer}", up, wts["down"][layer], dff, min(dff, 1024), h_in,
                            small["mlp_b_down"][layer:layer + 1], small["ln_mlp_g"][layer:layer + 1],
                            small["ln_mlp_b"][layer:layer + 1], a_fn=_relu2, emit_h=emit_h))

    r1, h1 = mix_ln("out_ab", [a_out, b_out], wts["out_ab"], s5w + sbw, min(s5w, sbw), h0, zero_bias,
                    small["ln_mix_g"][0:1], small["ln_mix_b"][0:1])
    wts = {**wts, **late_weights(r1)}
    n_c = wts["in_c"].shape[2]
    n_up = wts["up"][0].shape[2]
    up0, r2, h2 = mlp_fwd(0, h1)

    lb = _lower_bound("hg_lb", small["hgrn_gamma"])
    proj_c = _mm_act("in_c", h2, wts["in_c"], "stk", n_out_cols=4 * d, k_total=d, tn=shard_tile(4 * d, n_c), tk=d)[0]
    c_out, hg_states = _hgrn_fwd("hgrn", proj_c, lb, wts["ng"], n_batch, lp, pad, n_hg)
    r3, h3 = mix_ln("out_c", c_out, wts["out_c"], d, d, h2, zero_bias, small["ln_mix_g"][1:2], small["ln_mix_b"][1:2])
    up1, r4 = mlp_fwd(1, h3, emit_h=False)

    gr = {}
    g_r4, gr["ln_mlp_g1"], gr["ln_mlp_b1"], loss_tile = _loss_grad(
        "loss", r4, small["ln_mlp_g"][1:2], small["ln_mlp_b"][1:2], target, n_batch, lp, lead)

    def res_store(outs, acc, g_res):
        outs[0][...] = acc + ALPHA * g_res

    def ln_bwd_store(outs, acc, g_res, r_in, g, b, first_step):
        gr_in, gg, gb = jax.vjp(_ln, r_in, g, b)[1](acc + ALPHA * g_res)
        outs[0][...] = gr_in

        @pl.when(first_step)
        def _():
            outs[1][...] = jnp.zeros_like(outs[1])
            outs[2][...] = jnp.zeros_like(outs[2])

        outs[1][...] += gg
        outs[2][...] += gb

    def through_ln(name, a, w, k_total, tk, g_res, r_in, g, b, dep=None):
        vec = pl.BlockSpec((1, d), lambda i, j, k: (0, 0))
        return _mm_act(name, a, w, "stkT", n_out_cols=d, k_total=k_total, tn=d, tk=tk,
                       extras=(g_res, r_in, g, b), extra_specs=(_row_spec(tm, d), _row_spec(tm, d), vec, vec),
                       store=ln_bwd_store, sequential=True, dep=dep,
                       out_shape=[jax.ShapeDtypeStruct((rows, d), F32)] + [jax.ShapeDtypeStruct((1, d), F32)] * 2,
                       out_specs=[_row_spec(tm, d), vec, vec])

    def mlp_bwd(layer, g_r, up, h_in, r_in, send=None):
        def gup_store(outs, acc, upv):
            outs[0][...] = (acc * (2.0 * jnp.maximum(upv.astype(F32), 0.0))).astype(outs[0].dtype)

        tf = min(dff, 1024)
        shp, spec = wide(tf, BF16)
        g_up = _mm_act(f"g_up{layer}", g_r, wts["down"][layer], "natT", n_out_cols=dff, k_total=d, tn=tf, tk=d,
                       extras=(up,), extra_specs=(_row_spec(tm, tf),), store=gup_store, out_shape=shp, out_specs=spec)[0]
        gr[f"down{layer}"], gr[f"mlp_b_down{layer}"] = _mm_wgrad(
            f"dw_down{layer}", up, g_r, kw=dff, n=d, tmw=tf, tn=d, a_fn=_relu2, out_dtype=BF16, colsum=True)
        gr[f"up{layer}"], gr[f"mlp_b_up{layer}"] = _mm_wgrad(
            f"dw_up{layer}", h_in, g_up, kw=d, n=dff, tmw=d, tn=min(dff, 2048), shard_cols=n_up, out_dtype=BF16, colsum=True)
        dep = send() if send is not None else None
        g_r_in, gr[f"ln_mix_g{layer}"], gr[f"ln_mix_b{layer}"] = through_ln(
            f"g_hmid{layer}", g_up, wts["up"][layer], dff, shard_tile(dff, n_up), g_r, r_in,
            small["ln_mix_g"][layer:layer + 1], small["ln_mix_b"][layer:layer + 1], dep=dep)
        return g_r_in

    g_r3 = mlp_bwd(1, g_r4, up1, h3, r3)
    g_cout = _mm_act("g_cout", g_r3, wts["out_c"], "natT", n_out_cols=d, k_total=d, tn=d, tk=d)[0]
    gr["out_c"] = _mm_wgrad("dw_out_c", c_out, g_r3, kw=d, n=d, tmw=d, tn=d, out_dtype=BF16)
    gq, gf, gi, gg_, g_lb_parts, g_ng_parts = _hgrn_bwd("hgrn_bwd", proj_c, lb, wts["ng"], g_cout, hg_states,
                                                        n_batch, lp, pad, n_hg)
    g_pc = [gq, gf, gi, gg_]
    gr["hgrn_gamma"], gr["ng"] = _lower_bound_bwd("hg_lb_bwd", small["hgrn_gamma"], g_lb_parts, g_ng_parts)
    gr["in_c"] = _mm_wgrad("dw_in_c", h2, g_pc, kw=d, n=4 * d, tmw=d, tn=d, shard_cols=n_c, out_dtype=BF16)
    sent1 = on_grads(1, {"down1": gr["down1"], "up1": gr["up1"], "out_c": gr["out_c"], "in_c": gr["in_c"], "ng": gr["ng"]})
    g_r2, gr["ln_mlp_g0"], gr["ln_mlp_b0"] = through_ln(
        "g_h2", g_pc, wts["in_c"], 4 * d, n_c, g_r3, r2, small["ln_mlp_g"][0:1], small["ln_mlp_b"][0:1], dep=sent1)

    g_r1 = mlp_bwd(0, g_r2, up0, h1, r1, send=lambda: on_grads(2, {"down0": gr["down0"], "up0": gr["up0"]}))
    g_cat = _mm_act("g_cat", g_r1, wts["out_ab"], "natT", n_out_cols=d, k_total=d, tn=d, tk=d)[0]
    gr["out_ab"] = _mm_wgrad("dw_out_ab", [a_out, b_out], g_r1, kw=s5w + sbw, n=d, tmw=min(s5w, sbw), tn=d, out_dtype=BF16)
    g_q, g_k, g_v = _attn_bwd("sb_attn_bwd", proj_ab, g_cat, n_batch, lp, pad, s5_cb, s5_cb + sb_cb, s5_cb + 2 * sb_cb,
                              s5_cb, n_pairs)

    g_y_direct, g_gate = _rowwise("s5_glu_bwd", lambda ga, yv, gt: jax.vjp(_glu, yv, gt)[1](ga),
                                  [(g_cat, 0, s5w), (y, 0, s5w), (gate, 0, s5w)], 2, s5w)

    def gelu_bwd_store(outs, acc, gyd, yp, u, dv):
        gyp = jax.vjp(jax.nn.gelu, yp)[1](acc + gyd)[0]
        outs[0][...] = gyp
        outs[1][...] = dv * gyp
        outs[2][...] = jnp.sum(gyp * u, axis=0, keepdims=True)

    rs = _row_spec(tm, s5w)
    g_ypre, g_u_direct, gd_parts = _mm_act(
        "s5_g_y", g_gate, wts["glu"], "natT", n_out_cols=s5w, k_total=s5w, tn=s5w, tk=s5w,
        extras=(g_y_direct, ypre, proj_ab, d_row), extra_specs=(rs, rs, rs, _vec_spec(s5w)), store=gelu_bwd_store,
        out_shape=[jax.ShapeDtypeStruct((rows, s5w), F32)] * 2 + [jax.ShapeDtypeStruct((rows // tm, 1, s5w), F32)],
        out_specs=[rs, rs, pl.BlockSpec((None, 1, s5w), lambda i, j, k: (i, 0, j))])
    gr["glu"], gr["s5_b_glu"] = _mm_wgrad("dw_glu", y, g_gate, kw=s5w, n=s5w, tmw=s5w, tn=s5w, out_dtype=BF16, colsum=True)
    g_sd = _mm_act("s5_g_states", g_ypre, c_blk, "natT", n_out_cols=2 * ns, k_total=s5w, tn=min(2 * ns, 2048), tk=s5w)[0]
    d_cblk = _mm_wgrad("dw_cblk", states, g_ypre, kw=2 * ns, n=s5w, tmw=min(2 * ns, 1024), tn=s5w)
    gs, gl_parts = _s5_scan_bwd("s5_scan_bwd", g_sd, states, lam_row, n_batch, lp, sw)

    def add_store(outs, acc, other):
        outs[0][...] = acc + other

    g_u = _mm_act("s5_g_u", gs, b_blk, "natT", n_out_cols=s5w, k_total=2 * ns, tn=s5w, tk=min(2 * ns, 1024),
                  extras=(g_u_direct,), extra_specs=(rs,), store=add_store)[0]
    d_bblk = _mm_wgrad("dw_bblk", proj_ab, gs, kw=s5w, n=2 * ns, tmw=s5w, tn=min(2 * ns, 2048))
    db_re, db_im = _deinterleave(d_bblk, sw)
    dc_re, dc_im = _deinterleave(d_cblk.T, sw)
    glr, gli = _deinterleave(gl_parts, sw)
    g_lam_re, g_lam_im, g_log_dt, g_b_re_t, g_b_im_t, g_d = _s5_params_bwd(
        "s5_params_bwd", lam_re, lam_im, log_dt, b_re_t, b_im_t,
        glr.reshape(n_batch, groups, n_state), gli.reshape(n_batch, groups, n_state),
        _diag_blocks(db_re, groups), _diag_blocks(db_im, groups), gd_parts)

    cat2 = lambda key: jnp.concatenate([gr[key + "0"], gr[key + "1"]], axis=0)
    small_sent = on_small({
        "s5_lam_re": g_lam_re[None], "s5_lam_im": g_lam_im[None], "s5_log_dt": g_log_dt.reshape(1, groups),
        "s5_b_re": g_b_re_t.transpose(0, 2, 1)[None], "s5_b_im": g_b_im_t.transpose(0, 2, 1)[None],
        "s5_c_re": _diag_blocks(dc_re, groups)[None], "s5_c_im": -_diag_blocks(dc_im, groups)[None],
        "s5_d": g_d.reshape(1, groups, grp), "s5_b_glu": gr["s5_b_glu"], "hgrn_gamma": gr["hgrn_gamma"],
        "ln_mix_g": cat2("ln_mix_g"), "ln_mix_b": cat2("ln_mix_b"), "mlp_b_up": cat2("mlp_b_up"),
        "mlp_b_down": cat2("mlp_b_down"), "ln_mlp_g": cat2("ln_mlp_g"), "ln_mlp_b": cat2("ln_mlp_b"),
    }, loss_tile)

    g_pab = [g_u, g_q, g_k, g_v]
    assert s5w == sbw
    gr["in_ab"] = _mm_wgrad("dw_in_ab", h0, g_pab, kw=d, n=ab_cols, tmw=d, tn=s5w, shard_cols=n_ab, out_dtype=BF16,
                            dep=small_sent)
    g_h0 = _mm_act("g_h0", g_pab, wts["in_ab"], "stkT", n_out_cols=d, k_total=ab_cols, tn=d, tk=shard_tile(s5w, n_ab),
                   extras=(g_r1,), extra_specs=(_row_spec(tm, d),), store=res_store)[0]
    grad_x = g_h0.reshape(n_batch, lp, d)[:, lead:, :]
    g_meta = _meta_grad("g_meta", g_h0, n_batch, lp, pad, n_meta)
    on_grads(3, {"meta": g_meta, "in_ab": gr["in_ab"], "glu": gr["glu"], "out_ab": gr["out_ab"]})
    return grad_x


SMALL_NAMES = ("s5_lam_re", "s5_lam_im", "s5_log_dt", "s5_b_re", "s5_b_im", "s5_c_re", "s5_c_im", "s5_d", "s5_b_glu",
               "hgrn_gamma", "ln_mix_g", "ln_mix_b", "mlp_b_up", "mlp_b_down", "ln_mlp_g", "ln_mlp_b")
WEIGHT_ORDER = ("meta", "w_in_ab", "s5_lam_re", "s5_lam_im", "s5_log_dt", "s5_b_re", "s5_b_im", "s5_c_re", "s5_c_im",
                "s5_d", "s5_w_glu", "s5_b_glu", "w_out_ab", "w_in_c", "hgrn_gamma", "hgrn_norm_g", "w_out_c", "ln_mix_g",
                "ln_mix_b", "mlp_w_up", "mlp_b_up", "mlp_w_down", "mlp_b_down", "ln_mlp_g", "ln_mlp_b")


def kernel(x, meta, w_in_ab, s5_lam_re, s5_lam_im, s5_log_dt, s5_b_re, s5_b_im, s5_c_re, s5_c_im, s5_d, s5_w_glu, s5_b_glu, w_out_ab, w_in_c, hgrn_gamma, hgrn_norm_g, w_out_c, ln_mix_g, ln_mix_b, mlp_w_up, mlp_b_up, mlp_w_down, mlp_b_down, ln_mlp_g, ln_mlp_b, loss_target, m_meta, m_w_in_ab, m_s5_lam_re, m_s5_lam_im, m_s5_log_dt, m_s5_b_re, m_s5_b_im, m_s5_c_re, m_s5_c_im, m_s5_d, m_s5_w_glu, m_s5_b_glu, m_w_out_ab, m_w_in_c, m_hgrn_gamma, m_hgrn_norm_g, m_w_out_c, m_ln_mix_g, m_ln_mix_b, m_mlp_w_up, m_mlp_b_up, m_mlp_w_down, m_mlp_b_down, m_ln_mlp_g, m_ln_mlp_b, v_meta, v_w_in_ab, v_s5_lam_re, v_s5_lam_im, v_s5_log_dt, v_s5_b_re, v_s5_b_im, v_s5_c_re, v_s5_c_im, v_s5_d, v_s5_w_glu, v_s5_b_glu, v_w_out_ab, v_w_in_c, v_hgrn_gamma, v_hgrn_norm_g, v_w_out_c, v_ln_mix_g, v_ln_mix_b, v_mlp_w_up, v_mlp_b_up, v_mlp_w_down, v_mlp_b_down, v_ln_mlp_g, v_ln_mlp_b):
    args = dict(locals())
    w = {n: args[n] for n in WEIGHT_ORDER}
    mom = {n: args["m_" + n] for n in WEIGHT_ORDER}
    var = {n: args["v_" + n] for n in WEIGHT_ORDER}
    d = x.shape[2]
    n_meta = meta.shape[0]

    cast = lambda a: a.astype(BF16)
    early = _exchange_start("gather_early_start", [w["meta"], cast(w["w_in_ab"][0]), cast(w["s5_w_glu"][0]),
                                                   cast(w["w_out_ab"][0])], False)
    late = _exchange_start("gather_late_start", [w["hgrn_norm_g"], cast(w["w_in_c"][0]), cast(w["w_out_c"][0]),
                                                 cast(w["mlp_w_up"][0]), cast(w["mlp_w_up"][1]),
                                                 cast(w["mlp_w_down"][0]), cast(w["mlp_w_down"][1])], False, dep=early["token"])
    a_meta, a_in_ab, a_glu, a_out_ab = _exchange_wait("gather_early_wait", early, late["token"])
    wts = {"in_ab": a_in_ab, "glu": a_glu.reshape(-1, a_glu.shape[2]), "out_ab": a_out_ab.reshape(-1, d)}
    meta_full = a_meta.transpose(1, 0, 2).reshape(n_meta, d)
    small = {n: w[n] for n in SMALL_NAMES}

    def late_weights(after):
        a_ng, a_in_c, a_out_c, a_up0, a_up1, a_dn0, a_dn1 = _exchange_wait("gather_late_wait", late, after)
        return {"in_c": a_in_c, "ng": a_ng.transpose(1, 0, 2).reshape(1, d), "out_c": a_out_c.reshape(-1, d),
                "up": [a_up0, a_up1], "down": [a_dn0.reshape(-1, d), a_dn1.reshape(-1, d)]}

    n_loc = d // N_DEV
    rows_of = lambda g: g.reshape(N_DEV, -1, g.shape[-1])
    cols_of = lambda g: g.reshape(g.shape[0], N_DEV, n_loc).transpose(1, 0, 2)
    sent = {}

    def on_grads(stage, g):
        if stage == 1:
            order = (("mlp_w_down", 1), ("mlp_w_up", 1), ("w_out_c", 0), ("w_in_c", 0), ("hgrn_norm_g", None))
            parts = [rows_of(g["down1"]), g["up1"], rows_of(g["out_c"]), g["in_c"], cols_of(g["ng"])]
        elif stage == 2:
            order = (("mlp_w_down", 0), ("mlp_w_up", 0))
            parts = [rows_of(g["down0"]), g["up0"]]
        else:
            order = (("w_out_ab", 0), ("s5_w_glu", 0), ("w_in_ab", 0), ("meta", None))
            parts = [rows_of(g["out_ab"]), rows_of(g["glu"]), g["in_ab"], cols_of(g["meta"])]
        sent[stage] = (order, _exchange_start(f"scatter_start{stage}", parts, True))
        return sent[stage][1]["token"]

    def on_small(sg, loss_tile):
        g_pack = _pack_rows([sg[n] for n in SMALL_NAMES] + [loss_tile], PACK_COLS)
        sent["small"] = _exchange_start("gather_small_start", [g_pack], False)
        return sent["small"]["token"]

    grad_x = _local_step(x, loss_target, meta_full, wts, small, late_weights, on_grads, on_small)
    small_sent = sent["small"]
    tile = (SUBLANES, LANES)
    shapes = [w[n].shape for n in SMALL_NAMES] + [tile]
    zeros = jnp.zeros(tile, F32)
    w_pack = _pack_rows([w[n] for n in SMALL_NAMES] + [zeros], PACK_COLS)
    m_pack = _pack_rows([mom[n] for n in SMALL_NAMES] + [zeros], PACK_COLS)
    v_pack = _pack_rows([var[n] for n in SMALL_NAMES] + [zeros], PACK_COLS)

    received, res = {}, {}

    def wait(stage, after):
        order, handle = sent[stage]
        for key, rc in zip(order, _exchange_wait(f"scatter_wait{stage}", handle, after)):
            received[key] = rc

    def update(nm):
        layered = w[nm].ndim == 3
        parts = [received[(nm, l)] for l in range(w[nm].shape[0])] if layered else received[(nm, None)]
        res[nm] = _adamw_summed(f"adamw_{nm}", parts, w[nm], mom[nm], var[nm])
        return res[nm][0]

    wait(1, sent[3][1]["token"])
    done = [update(nm) for nm in ("w_out_c", "w_in_c", "hgrn_norm_g")]
    wait(2, done[0])
    done = [update(nm) for nm in ("mlp_w_up", "mlp_w_down")]
    g_all = _exchange_wait("gather_small_wait", small_sent, done[0])[0]
    packed = _adamw_summed("adamw_small", g_all, w_pack, m_pack, v_pack)
    wait(3, packed[0])
    for nm in ("w_out_ab", "s5_w_glu", "w_in_ab", "meta"):
        update(nm)
    unpacked = [_unpack_rows(p, shapes, PACK_COLS) for p in packed]
    loss = unpacked[0][-1][0, 0]

    def pick(nm, which):
        return unpacked[which][SMALL_NAMES.index(nm)] if nm in SMALL_NAMES else res[nm][which]

    return (loss, grad_x, *[pick(n, 0) for n in WEIGHT_ORDER], *[pick(n, 1) for n in WEIGHT_ORDER],
            *[pick(n, 2) for n in WEIGHT_ORDER], *[pick(n, 3) for n in WEIGHT_ORDER])
```

```python
import functools
import math

import jax
import jax.numpy as jnp
from jax import lax
from jax.experimental import pallas as pl
from jax.experimental.pallas import tpu as pltpu

F32 = jnp.float32
BF16 = jnp.bfloat16

N_DEV = 8
DEPTH = 2
ALPHA = (2.0 * DEPTH) ** 0.25
LN_EPS = 1e-5
RMS_EPS = 1e-6
SB_HEAD_DIM = 64
HG_DK = 128
HG_CHUNK = 64
LANES = 128
SUBLANES = 8
PACKED_ROWS = 16
VMEM_LIMIT_BYTES = 56 * 1024 * 1024
ROW_TILE = 1088
SCAN_LANES = 256
SCAN_UNROLL = 2
PACK_COLS = 1024

ADAM_LR = 0.001
ADAM_B1 = 0.9
ADAM_B2 = 0.999
ADAM_EPS = 1e-08
ADAM_WD = 0.01
ADAM_STEP = 10

NN = (((1,), (0,)), ((), ()))
NT = (((1,), (1,)), ((), ()))
TN = (((0,), (0,)), ((), ()))


def _tile(n, pref, align=SUBLANES):
    t = min(n, pref)
    t -= t % align
    while t >= align:
        if n % t == 0:
            return t
        t -= align
    return n


def _unrolled_loop(n, body, init, unroll):
    assert n % unroll == 0

    def outer(t, carry):
        for u in range(unroll):
            carry = body(t * unroll + u, carry)
        return carry

    return lax.fori_loop(0, n // unroll, outer, init)


def _params(sem):
    return pltpu.CompilerParams(dimension_semantics=sem, vmem_limit_bytes=VMEM_LIMIT_BYTES)


def _dot_raw(a, b, dims):
    return lax.dot_general(a.astype(BF16), b.astype(BF16), dims, preferred_element_type=F32)


def _make_dot(dims, da_rule, db_rule):
    @jax.custom_vjp
    def f(a, b):
        return _dot_raw(a, b, dims)

    def fwd(a, b):
        return _dot_raw(a, b, dims), (a, b)

    def bwd(res, g):
        a, b = res
        return da_rule(g, a, b), db_rule(g, a, b)

    f.defvjp(fwd, bwd)
    return f


_DOTS = {
    NN: _make_dot(NN, lambda g, a, b: _dot_raw(g, b, NT), lambda g, a, b: _dot_raw(a, g, TN)),
    NT: _make_dot(NT, lambda g, a, b: _dot_raw(g, b, NN), lambda g, a, b: _dot_raw(g, a, TN)),
    TN: _make_dot(TN, lambda g, a, b: _dot_raw(b, g, NT), lambda g, a, b: _dot_raw(a, g, NN)),
}


def _dot(a, b, dims):
    return _DOTS[dims](a, b)


def _running_sums(a, tri_ones, split=False):
    hi = a.astype(BF16)
    out = lax.dot_general(hi, tri_ones, NN, preferred_element_type=F32)
    if split:
        lo = (a - hi.astype(F32)).astype(BF16)
        out = out + lax.dot_general(lo, tri_ones, NN, preferred_element_type=F32)
    return out


def _piece_specs(pieces, block_rows, block_cols, row_of, col_of, cb0):
    per = pieces[0].shape[1] // block_cols if len(pieces) > 1 else None
    specs = []
    for p in range(len(pieces)):
        if per is None:
            specs.append(pl.BlockSpec((block_rows, block_cols), lambda *g: (row_of(*g), cb0 + col_of(*g))))
        else:
            specs.append(pl.BlockSpec(
                (block_rows, block_cols),
                lambda *g, p=p: (row_of(*g), jnp.clip(col_of(*g) - p * per, 0, per - 1))))
    return specs, per


def _mm_call(name, grid, dims, a_pieces, a_specs, a_sel, b_pieces, b_specs, b_sel, extras, extra_specs,
             out_shape, out_specs, acc_shape, a_fn, store, colsum_width=0, sequential=False, deps=()):
    na, nb, ne, no, nd = len(a_pieces), len(b_pieces), len(extras), len(out_shape), len(deps)
    nk = grid[2]

    def body(*refs):
        a_refs, b_refs = refs[:na], refs[na:na + nb]
        extra = refs[na + nb:na + nb + ne]
        outs = refs[na + nb + ne + nd:na + nb + ne + nd + no]
        acc = refs[na + nb + ne + nd + no]
        ids = (pl.program_id(0), pl.program_id(1), pl.program_id(2))
        k = ids[2]

        @pl.when(k == 0)
        def _():
            acc[...] = jnp.zeros_like(acc)

        def run(a_ref, b_ref):
            a = a_ref[...]
            if a_fn is not None:
                a = a_fn(a)
            b = b_ref[...]
            if b.ndim == 3 and dims == NN:
                n = b.shape[2]
                for q in range(b.shape[0]):
                    acc[:, q * n:(q + 1) * n] += _dot_raw(a, b[q], dims)
            elif b.ndim == 3:
                n = b.shape[2]
                for q in range(b.shape[0]):
                    acc[...] += _dot_raw(a[:, q * n:(q + 1) * n], b[q], dims)
            else:
                acc[...] += _dot_raw(a, b, dims)
            if colsum_width:
                cs = refs[-1]
                first = ids[1] == 0

                @pl.when(first & (k == 0))
                def _():
                    cs[...] = jnp.zeros_like(cs)

                @pl.when(first)
                def _():
                    cs[...] += jnp.sum(b.astype(F32), axis=0, keepdims=True)

        if na == 1 and nb == 1:
            run(a_refs[0], b_refs[0])
        elif nb == 1:
            per, fn = a_sel
            which = fn(*ids) // per
            for p in range(na):
                pl.when(which == p)(functools.partial(run, a_refs[p], b_refs[0]))
        else:
            assert na == 1
            per, fn = b_sel
            which = fn(*ids) // per
            for p in range(nb):
                pl.when(which == p)(functools.partial(run, a_refs[0], b_refs[p]))

        @pl.when(k == nk - 1)
        def _():
            if sequential:
                store(outs, acc[...], *[e[...] for e in extra], first_step=(ids[0] == 0) & (ids[1] == 0))
            else:
                store(outs, acc[...], *[e[...] for e in extra])
            if colsum_width:
                @pl.when(ids[1] == 0)
                def _():
                    outs[-1][...] = refs[-1][...]

    scratch = [pltpu.VMEM(acc_shape, F32)]
    if colsum_width:
        scratch.append(pltpu.VMEM((1, colsum_width), F32))
    sem = ("parallel", "arbitrary", "arbitrary") if colsum_width else ("parallel", "parallel", "arbitrary")
    if sequential:
        sem = ("arbitrary",) * 3
    return pl.pallas_call(
        body, name=name, grid=grid,
        in_specs=[*a_specs, *b_specs, *extra_specs, *[pl.BlockSpec(memory_space=pl.ANY)] * nd], out_specs=out_specs,
        out_shape=out_shape, scratch_shapes=scratch, compiler_params=_params(sem),
    )(*a_pieces, *b_pieces, *extras, *deps)


def _store_plain(outs, acc):
    outs[0][...] = acc.astype(outs[0].dtype)


def _row_spec(tm, tn):
    return pl.BlockSpec((tm, tn), lambda i, j, k: (i, j))


def _vec_spec(tn):
    return pl.BlockSpec((1, tn), lambda i, j, k: (0, j))


def _mm_act(name, a, w, wkind, *, n_out_cols, k_total, tn, tk, a_cb0=0, a_fn=None, extras=(), extra_specs=(),
            store=_store_plain, out_shape=None, out_specs=None, sequential=False, dep=None):
    a_pieces = list(a) if isinstance(a, (list, tuple)) else [a]
    rows = a_pieces[0].shape[0]
    tm = _tile(rows, ROW_TILE)
    grid = (rows // tm, n_out_cols // tn, k_total // tk)
    a_specs, per = _piece_specs(a_pieces, tm, tk, lambda i, j, k: i, lambda i, j, k: k, a_cb0)
    if wkind == "nat":
        b_spec, dims = pl.BlockSpec((tk, tn), lambda i, j, k: (k, j)), NN
    elif wkind == "stk":
        n = w.shape[2]
        assert tn % n == 0
        b_spec, dims = pl.BlockSpec((tn // n, tk, n), lambda i, j, k: (j, k, 0)), NN
    elif wkind == "natT":
        b_spec, dims = pl.BlockSpec((tn, tk), lambda i, j, k: (j, k)), NT
    else:
        n = w.shape[2]
        assert wkind == "stkT" and tk % n == 0
        b_spec, dims = pl.BlockSpec((tk // n, tn, n), lambda i, j, k: (k, j, 0)), NT
    if out_shape is None:
        out_shape = [jax.ShapeDtypeStruct((rows, n_out_cols), F32)]
        out_specs = [_row_spec(tm, tn)]
    return _mm_call(name, grid, dims, a_pieces, a_specs, (per, lambda i, j, k: k), [w], [b_spec], None,
                    list(extras), list(extra_specs), out_shape, out_specs, (tm, tn), a_fn, store,
                    sequential=sequential, deps=() if dep is None else (dep,))


def _mm_wgrad(name, a, g, *, kw, n, tmw, tn, a_cb0=0, a_fn=None, shard_cols=0, out_dtype=F32, colsum=False, dep=None):
    a_pieces = list(a) if isinstance(a, (list, tuple)) else [a]
    g_pieces = list(g) if isinstance(g, (list, tuple)) else [g]
    rows = a_pieces[0].shape[0]
    tr = _tile(rows, ROW_TILE)
    grid = (n // tn, kw // tmw, rows // tr)
    a_specs, a_per = _piece_specs(a_pieces, tr, tmw, lambda j, i, k: k, lambda j, i, k: i, a_cb0)
    g_specs, g_per = _piece_specs(g_pieces, tr, tn, lambda j, i, k: k, lambda j, i, k: j, 0)
    if shard_cols:
        per = tn // shard_cols
        out_shape = [jax.ShapeDtypeStruct((n // shard_cols, kw, shard_cols), out_dtype)]
        out_specs = [pl.BlockSpec((per, tmw, shard_cols), lambda j, i, k: (j, i, 0))]

        def store(outs, acc):
            for q in range(per):
                outs[0][q] = acc[:, q * shard_cols:(q + 1) * shard_cols].astype(out_dtype)
    else:
        out_shape = [jax.ShapeDtypeStruct((kw, n), out_dtype)]
        out_specs = [pl.BlockSpec((tmw, tn), lambda j, i, k: (i, j))]

        def store(outs, acc):
            outs[0][...] = acc.astype(out_dtype)
    if colsum:
        out_shape.append(jax.ShapeDtypeStruct((1, n), F32))
        out_specs.append(pl.BlockSpec((1, tn), lambda j, i, k: (0, j)))
    res = _mm_call(name, grid, TN, a_pieces, a_specs, (a_per, lambda j, i, k: i), g_pieces, g_specs,
                   (g_per, lambda j, i, k: j), [], [], out_shape, out_specs, (tmw, tn), a_fn, store,
                   colsum_width=tn if colsum else 0, deps=() if dep is None else (dep,))
    return res if colsum else res[0]


def _ln(x, g, b):
    mu = jnp.mean(x, axis=-1, keepdims=True)
    xc = x - mu
    var = jnp.mean(xc * xc, axis=-1, keepdims=True)
    return xc * lax.rsqrt(var + LN_EPS) * g + b


def _relu2(x):
    r = jnp.maximum(x.astype(F32), 0.0)
    return r * r


def _glu(y, gate):
    return y * jax.nn.sigmoid(gate)


def _rowwise(name, fn, ins, n_out, width):
    rows = ins[0][0].shape[0]
    tm = _tile(rows, ROW_TILE)

    def body(*refs):
        res = fn(*[r[...] for r in refs[:len(ins)]])
        for o, v in zip(refs[len(ins):], res):
            o[...] = v

    return pl.pallas_call(
        body, name=name, grid=(rows // tm,),
        in_specs=[pl.BlockSpec((tm, wd), lambda i, cb=cb: (i, cb)) for _, cb, wd in ins],
        out_specs=[pl.BlockSpec((tm, width), lambda i: (i, 0))] * n_out,
        out_shape=[jax.ShapeDtypeStruct((rows, width), F32)] * n_out, compiler_params=_params(("parallel",)),
    )(*[a for a, _, _ in ins])


def _loss_grad(name, r, g, b, target, n_batch, lp, lead):
    rows, d = r.shape
    nq = lp // LANES
    lead_blocks = lead // LANES

    def body(r_ref, g_ref, b_ref, t_ref, gr_ref, gg_ref, gb_ref, loss_ref):
        i = pl.program_id(1)

        @pl.when((pl.program_id(0) == 0) & (i == 0))
        def _():
            loss_ref[...] = jnp.zeros_like(loss_ref)
            gg_ref[...] = jnp.zeros_like(gg_ref)
            gb_ref[...] = jnp.zeros_like(gb_ref)

        h, vjp = jax.vjp(_ln, r_ref[...], g_ref[...], b_ref[...])
        diff = jnp.where(i >= lead_blocks, h - t_ref[...], 0.0)
        gr, gg, gb = vjp(diff * (1.0 / d))
        gr_ref[...] = gr
        gg_ref[...] += gg
        gb_ref[...] += gb
        loss_ref[...] += 0.5 * jnp.sum(diff * diff) * (1.0 / d)

    vec = pl.BlockSpec((1, d), lambda b, i: (0, 0))
    row = pl.BlockSpec((LANES, d), lambda b, i: (b * nq + i, 0))
    return pl.pallas_call(
        body, name=name, grid=(n_batch, nq),
        in_specs=[row, vec, vec, pl.BlockSpec((None, LANES, d), lambda b, i: (b, jnp.maximum(i - lead_blocks, 0), 0))],
        out_specs=[row, vec, vec, pl.BlockSpec((SUBLANES, LANES), lambda b, i: (0, 0))],
        out_shape=[jax.ShapeDtypeStruct((rows, d), F32), jax.ShapeDtypeStruct((1, d), F32),
                   jax.ShapeDtypeStruct((1, d), F32), jax.ShapeDtypeStruct((SUBLANES, LANES), F32)],
        compiler_params=_params(("arbitrary", "arbitrary")),
    )(r, g, b, target)


def _meta_grad(name, g_h0, n_batch, lp, pad, n_meta):
    d = g_h0.shape[1]
    per = lp // n_meta
    at = pad // n_meta

    def body(g_ref, o_ref):
        @pl.when(pl.program_id(0) == 0)
        def _():
            o_ref[...] = jnp.zeros_like(o_ref)

        o_ref[...] += g_ref[...]

    return pl.pallas_call(
        body, name=name, grid=(n_batch,),
        in_specs=[pl.BlockSpec((n_meta, d), lambda b: (b * per + at, 0))],
        out_specs=pl.BlockSpec((n_meta, d), lambda b: (0, 0)),
        out_shape=jax.ShapeDtypeStruct((n_meta, d), F32),
        compiler_params=_params(("arbitrary",)),
    )(g_h0)


def _s5_param_fn(lr, li, ldt, br, bi):
    dt = jnp.exp(ldt)
    e = jnp.exp(lr * dt)
    w = li * dt
    lbr = e * jnp.cos(w)
    lbi = e * jnp.sin(w)
    nr = lbr - 1.0
    den = lr * lr + li * li
    cr = (nr * lr + lbi * li) / den
    ci = (lbi * lr - nr * li) / den
    bbr = cr[:, None, :] * br - ci[:, None, :] * bi
    bbi = cr[:, None, :] * bi + ci[:, None, :] * br
    return lbr, lbi, bbr, bbi


def _s5_params(name, lr, li, ldt, br, bi):
    def body(lr_ref, li_ref, ldt_ref, br_ref, bi_ref, o1, o2, o3, o4):
        res = _s5_param_fn(lr_ref[...], li_ref[...], ldt_ref[...], br_ref[...], bi_ref[...])
        for o, v in zip((o1, o2, o3, o4), res):
            o[...] = v

    shp = [jax.ShapeDtypeStruct(lr.shape, F32)] * 2 + [jax.ShapeDtypeStruct(br.shape, F32)] * 2
    return pl.pallas_call(body, name=name, out_shape=shp)(lr, li, ldt, br, bi)


def _s5_params_bwd(name, lr, li, ldt, br, bi, g_lbr, g_lbi, g_bbr, g_bbi, gd_parts):
    def body(lr_ref, li_ref, ldt_ref, br_ref, bi_ref, g1, g2, g3, g4, gd_ref, o1, o2, o3, o4, o5, o6):
        _, vjp = jax.vjp(_s5_param_fn, lr_ref[...], li_ref[...], ldt_ref[...], br_ref[...], bi_ref[...])
        res = vjp((jnp.sum(g1[...], axis=0), jnp.sum(g2[...], axis=0), g3[...], g4[...]))
        for o, v in zip((o1, o2, o3, o4, o5), res):
            o[...] = v
        o6[...] = jnp.sum(gd_ref[...], axis=0)

    shp = ([jax.ShapeDtypeStruct(lr.shape, F32)] * 2 + [jax.ShapeDtypeStruct(ldt.shape, F32)]
           + [jax.ShapeDtypeStruct(br.shape, F32)] * 2 + [jax.ShapeDtypeStruct(gd_parts.shape[1:], F32)])
    return pl.pallas_call(body, name=name, out_shape=shp)(lr, li, ldt, br, bi, g_lbr, g_lbi, g_bbr, g_bbi, gd_parts)


def _interleave(re, im, w):
    nj = re.shape[-1] // w
    return jnp.concatenate([x[..., j * w:(j + 1) * w] for j in range(nj) for x in (re, im)], axis=-1)


def _deinterleave(x, w):
    nj = x.shape[-1] // (2 * w)
    return (jnp.concatenate([x[..., 2 * j * w:(2 * j + 1) * w] for j in range(nj)], axis=-1),
            jnp.concatenate([x[..., (2 * j + 1) * w:(2 * j + 2) * w] for j in range(nj)], axis=-1))


def _cmul(ar, ai, br, bi):
    return ar * br - ai * bi, ar * bi + ai * br


def _powers(lr, li):
    p = [(lr, li)]
    p.append(_cmul(*p[0], *p[0]))
    p.append(_cmul(*p[1], *p[0]))
    p.append(_cmul(*p[1], *p[1]))
    p.append(_cmul(*p[3], *p[0]))
    p.append(_cmul(*p[3], *p[1]))
    p.append(_cmul(*p[3], *p[2]))
    p.append(_cmul(*p[3], *p[3]))
    return p


def _scan_steps(pw, shifts, keep):
    return [(sh, jnp.where(m, pw[s - 1][0], 0.0), jnp.where(m, pw[s - 1][1], 0.0))
            for s, sh, m in zip((1, 2, 4), shifts, keep)]


def _scan_tile(xr, xi, steps):
    for sh, br, bi in steps:
        rr = pltpu.roll(xr, sh, 0)
        ri = pltpu.roll(xi, sh, 0)
        xr, xi = xr + (br * rr - bi * ri), xi + (br * ri + bi * rr)
    return xr, xi


def _s5_scan(name, bu, lam, n_batch, lp, w):
    rows, two_ns = bu.shape
    nj = two_ns // (2 * w)

    def body(x_ref, lam_ref, s_ref):
        pw = _powers(lam_ref[:, :w], lam_ref[:, w:])
        tab_r = jnp.concatenate([p[0] for p in pw], axis=0)
        tab_i = jnp.concatenate([p[1] for p in pw], axis=0)
        row = lax.broadcasted_iota(jnp.int32, (SUBLANES, w), 0)
        steps = _scan_steps(pw, (1, 2, 4), [row >= s for s in (1, 2, 4)])

        def packed_tile(t, carry):
            cr, ci = carry
            r0 = pl.multiple_of(t * PACKED_ROWS, PACKED_ROWS)
            x = x_ref[pl.ds(r0, PACKED_ROWS), :].astype(F32)
            done = []
            for half in range(PACKED_ROWS // SUBLANES):
                xt = x[half * SUBLANES:(half + 1) * SUBLANES, :]
                xr, xi = _scan_tile(xt[:, :w], xt[:, w:], steps)
                sr = xr + (tab_r * cr - tab_i * ci)
                si = xi + (tab_r * ci + tab_i * cr)
                done.append(jnp.concatenate([sr, si], axis=1))
                cr, ci = sr[SUBLANES - 1:, :], si[SUBLANES - 1:, :]
            s_ref[pl.ds(r0, PACKED_ROWS), :] = jnp.concatenate(done, axis=0).astype(s_ref.dtype)
            return cr, ci

        zero = jnp.zeros((1, w), F32)
        _unrolled_loop(lp // PACKED_ROWS, packed_tile, (zero, zero), SCAN_UNROLL)

    spec = pl.BlockSpec((lp, 2 * w), lambda b, j: (b, j))
    return pl.pallas_call(
        body, name=name, grid=(n_batch, nj), in_specs=[spec, pl.BlockSpec((1, 2 * w), lambda b, j: (0, j))],
        out_specs=spec, out_shape=jax.ShapeDtypeStruct((rows, two_ns), BF16),
        compiler_params=_params(("parallel", "parallel")),
    )(bu, lam)


def _s5_scan_bwd(name, gd, states, lam, n_batch, lp, w):
    rows, two_ns = gd.shape
    nj = two_ns // (2 * w)

    def body(x_ref, s_ref, lam_ref, g_ref, gl_ref):
        pw = _powers(lam_ref[:, :w], -lam_ref[:, w:])
        tab_r = jnp.concatenate([p[0] for p in reversed(pw)], axis=0)
        tab_i = jnp.concatenate([p[1] for p in reversed(pw)], axis=0)
        row = lax.broadcasted_iota(jnp.int32, (SUBLANES, w), 0)
        steps = _scan_steps(pw, [SUBLANES - s for s in (1, 2, 4)], [row < SUBLANES - s for s in (1, 2, 4)])

        n_packed = lp // PACKED_ROWS
        halves = PACKED_ROWS // SUBLANES

        def packed_tile(u, carry):
            cr, ci, ar, ai = carry
            t = n_packed - 1 - u
            r0 = pl.multiple_of(t * PACKED_ROWS, PACKED_ROWS)
            x = x_ref[pl.ds(r0, PACKED_ROWS), :].astype(F32)
            cur = s_ref[pl.ds(r0, PACKED_ROWS), :].astype(F32)
            p0 = pl.multiple_of(jnp.maximum(t - 1, 0) * PACKED_ROWS, PACKED_ROWS)
            before = s_ref[pl.ds(p0, PACKED_ROWS), :].astype(F32)[PACKED_ROWS - 1:, :] * jnp.where(t > 0, 1.0, 0.0)
            done = [None] * halves
            for half in reversed(range(halves)):
                rows_h = slice(half * SUBLANES, (half + 1) * SUBLANES)
                xt, st = x[rows_h, :], cur[rows_h, :]
                xr, xi = _scan_tile(xt[:, :w], xt[:, w:], steps)
                gr = xr + (tab_r * cr - tab_i * ci)
                gi = xi + (tab_r * ci + tab_i * cr)
                done[half] = jnp.concatenate([gr, gi], axis=1)
                prev = before if half == 0 else cur[half * SUBLANES - 1:half * SUBLANES, :]
                spr = jnp.where(row >= 1, pltpu.roll(st[:, :w], 1, 0), prev[:, :w])
                spi = jnp.where(row >= 1, pltpu.roll(st[:, w:], 1, 0), prev[:, w:])
                cr, ci, ar, ai = gr[:1, :], gi[:1, :], ar + gr * spr + gi * spi, ai + gi * spr - gr * spi
            g_ref[pl.ds(r0, PACKED_ROWS), :] = jnp.concatenate(done, axis=0).astype(g_ref.dtype)
            return cr, ci, ar, ai

        z1 = jnp.zeros((1, w), F32)
        z8 = jnp.zeros((SUBLANES, w), F32)
        _, _, ar, ai = _unrolled_loop(n_packed, packed_tile, (z1, z1, z8, z8), SCAN_UNROLL)
        gl_ref[...] = jnp.concatenate([jnp.sum(ar, axis=0, keepdims=True), jnp.sum(ai, axis=0, keepdims=True)], axis=1)

    spec = pl.BlockSpec((lp, 2 * w), lambda b, j: (b, j))
    return pl.pallas_call(
        body, name=name, grid=(n_batch, nj),
        in_specs=[spec, spec, pl.BlockSpec((1, 2 * w), lambda b, j: (0, j))],
        out_specs=[spec, pl.BlockSpec((None, 1, 2 * w), lambda b, j: (b, 0, j))],
        out_shape=[jax.ShapeDtypeStruct((rows, two_ns), BF16), jax.ShapeDtypeStruct((n_batch, 1, two_ns), F32)],
        compiler_params=_params(("parallel", "parallel")),
    )(gd, states, lam)


def _log_sigmoid(z):
    return jnp.minimum(z, 0.0) - jnp.log(1.0 + jnp.exp(-jnp.abs(z)))


ATTN_KEYS = 256
ATTN_GROUP = 4


def _attn_block(i, jb, lp, pad):
    start = jb * ATTN_KEYS
    r0 = pl.multiple_of(jnp.minimum(start, lp - ATTN_KEYS), LANES)
    rowpos = i * LANES + lax.broadcasted_iota(jnp.int32, (LANES, ATTN_KEYS), 0)
    keypos = r0 + lax.broadcasted_iota(jnp.int32, (LANES, ATTN_KEYS), 1)
    return r0, (keypos < rowpos) & (keypos >= jnp.maximum(start, pad))


def _tri_ones(strict_upper):
    r = lax.broadcasted_iota(jnp.int32, (ATTN_KEYS, ATTN_KEYS + LANES), 0)
    c = lax.broadcasted_iota(jnp.int32, (ATTN_KEYS, ATTN_KEYS + LANES), 1)
    tri = (r > c) if strict_upper else (r < c)
    return jnp.where((c >= ATTN_KEYS) | tri, 1.0, 0.0).astype(BF16)


def _split_sums(cr):
    rs = cr[:, ATTN_KEYS:]
    return cr[:, :ATTN_KEYS], jnp.concatenate([rs] * (ATTN_KEYS // LANES), axis=1)


def _head_masks():
    lane = lax.broadcasted_iota(jnp.int32, (1, LANES), 1)
    return [lane < SB_HEAD_DIM, lane >= SB_HEAD_DIM]


def _run_groups(n, first, sign, make):
    j, left, g = first, n, ATTN_GROUP
    while g >= 1:
        shift = g.bit_length() - 1
        count = lax.shift_right_logical(left, shift)
        fn = make(g)

        def loop(_, jcur, fn=fn, g=g):
            fn(jcur)
            return jcur + sign * g

        j = lax.fori_loop(0, count, loop, j)
        left = left - lax.shift_left(count, shift)
        g //= 2


def _attn_fwd(name, proj, n_batch, lp, pad, q_cb, k_cb, v_cb, n_pairs):
    rows = proj.shape[0]
    nq = lp // LANES
    scale = SB_HEAD_DIM ** -0.5

    def body(q_ref, k_ref, v_ref, o_ref, acc_s):
        i = pl.program_id(1)
        hm = _head_masks()
        comb = _tri_ones(True)
        n_blocks = lax.shift_right_logical(i + ATTN_KEYS // LANES, (ATTN_KEYS // LANES).bit_length() - 1)

        def pair(hp, carry):
            lanes = pl.ds(pl.multiple_of(hp * LANES, LANES), LANES)
            qs = q_ref[:, lanes] * scale
            qh = [jnp.where(m, qs, 0.0).astype(BF16) for m in hm]
            acc_s[...] = jnp.zeros_like(acc_s)
            o_ref[:, lanes] = jnp.zeros((LANES, LANES), F32)

            def make(group):
                def fn(jtop):
                    chains = []
                    for g in range(group):
                        r0, vis = _attn_block(i, jtop - g, lp, pad)
                        kj = k_ref[pl.ds(r0, ATTN_KEYS), lanes].astype(BF16)
                        vj = v_ref[pl.ds(r0, ATTN_KEYS), lanes]
                        for h in range(2):
                            z = lax.dot_general(qh[h], kj, NT, preferred_element_type=F32)
                            chains.append((h, vis, z, jnp.where(hm[h], vj, 0.0).astype(BF16)))
                    staged = []
                    for h, vis, z, vh in chains:
                        lsz = _log_sigmoid(z)
                        staged.append((h, vis, lsz, _running_sums(jnp.where(vis, lsz - z, 0.0), comb, split=True), vh))
                    out = o_ref[:, lanes]
                    for h, vis, lsz, cr, vh in staged:
                        later, rs = _split_sums(cr)
                        acc = acc_s[h]
                        wgt = jnp.where(vis, jnp.exp(lsz + later + acc), 0.0)
                        acc_s[h] = acc + rs
                        out = out + lax.dot_general(wgt.astype(BF16), vh, NN, preferred_element_type=F32)
                    o_ref[:, lanes] = out
                return fn

            _run_groups(n_blocks, n_blocks - 1, -1, make)
            return carry

        lax.fori_loop(0, n_pairs, pair, 0)

    wide = n_pairs * LANES
    assert q_cb % n_pairs == 0 and k_cb % n_pairs == 0 and v_cb % n_pairs == 0
    return pl.pallas_call(
        body, name=name, grid=(n_batch, nq),
        in_specs=[pl.BlockSpec((LANES, wide), lambda b, i: (b * nq + i, q_cb // n_pairs)),
                  pl.BlockSpec((lp, wide), lambda b, i: (b, k_cb // n_pairs)),
                  pl.BlockSpec((lp, wide), lambda b, i: (b, v_cb // n_pairs))],
        out_specs=pl.BlockSpec((LANES, wide), lambda b, i: (b * nq + i, 0)),
        out_shape=jax.ShapeDtypeStruct((rows, wide), F32),
        scratch_shapes=[pltpu.VMEM((2, LANES, ATTN_KEYS), F32)],
        compiler_params=_params(("parallel", "arbitrary")),
    )(proj, proj, proj)


def _attn_bwd(name, proj, g_out, n_batch, lp, pad, q_cb, k_cb, v_cb, go_cb, n_pairs):
    rows = proj.shape[0]
    nq = lp // LANES
    scale = SB_HEAD_DIM ** -0.5

    def body(q_ref, k_ref, v_ref, go_ref, gq_ref, gk_ref, gv_ref, ga_s, sz_s, acc_s):
        i = pl.program_id(1)

        @pl.when(i == 0)
        def _():
            gk_ref[...] = jnp.zeros_like(gk_ref)
            gv_ref[...] = jnp.zeros_like(gv_ref)

        hm = _head_masks()
        comb_up = _tri_ones(True)
        comb_lo = _tri_ones(False)
        n_blocks = lax.shift_right_logical(i + ATTN_KEYS // LANES, (ATTN_KEYS // LANES).bit_length() - 1)

        def pair(hp, carry):
            lanes = pl.ds(pl.multiple_of(hp * LANES, LANES), LANES)
            qs = q_ref[:, lanes] * scale
            go = go_ref[:, lanes]
            qh = [jnp.where(m, qs, 0.0).astype(BF16) for m in hm]
            goh = [jnp.where(m, go, 0.0).astype(BF16) for m in hm]
            acc_s[...] = jnp.zeros_like(acc_s)

            def make_down(group):
                def fn(jtop):
                    chains = []
                    for g in range(group):
                        j = jtop - g
                        r0, vis = _attn_block(i, j, lp, pad)
                        kj = k_ref[pl.ds(r0, ATTN_KEYS), lanes].astype(BF16)
                        vj = v_ref[pl.ds(r0, ATTN_KEYS), lanes].astype(BF16)
                        for h in range(2):
                            z = lax.dot_general(qh[h], kj, NT, preferred_element_type=F32)
                            gw = lax.dot_general(goh[h], vj, NT, preferred_element_type=F32)
                            chains.append((h, j, r0, vis, z, gw))
                    staged = []
                    for h, j, r0, vis, z, gw in chains:
                        lsz = _log_sigmoid(z)
                        staged.append((h, j, r0, vis, lsz, _running_sums(jnp.where(vis, lsz - z, 0.0), comb_up), gw))
                    for h, j, r0, vis, lsz, cr, gw in staged:
                        later, rs = _split_sums(cr)
                        acc = acc_s[h]
                        wgt = jnp.where(vis, jnp.exp(lsz + later + acc), 0.0)
                        acc_s[h] = acc + rs
                        ga_s[h, j] = gw * wgt
                        sz_s[h, j] = jnp.exp(lsz)
                        gv_ref[pl.ds(r0, ATTN_KEYS), lanes] += lax.dot_general(
                            wgt.astype(BF16), goh[h], TN, preferred_element_type=F32)
                return fn

            _run_groups(n_blocks, n_blocks - 1, -1, make_down)
            acc_s[...] = jnp.zeros_like(acc_s)

            def make_up(group):
                def fn(jbot):
                    pend = []
                    for g in range(group):
                        j = jbot + g
                        r0, vis = _attn_block(i, j, lp, pad)
                        kj = k_ref[pl.ds(r0, ATTN_KEYS), lanes]
                        for h in range(2):
                            ga = ga_s[h, j]
                            pend.append((h, j, r0, vis, ga, _running_sums(ga, comb_lo),
                                         jnp.where(hm[h], kj, 0.0).astype(BF16)))
                    gq = jnp.zeros((LANES, LANES), F32)
                    for h, j, r0, vis, ga, cr, kh in pend:
                        before, rs = _split_sums(cr)
                        pre = acc_s[h]
                        glk = before + pre
                        acc_s[h] = pre + rs
                        sz = sz_s[h, j]
                        gz = jnp.where(vis, ga * (1.0 - sz) - glk * sz, 0.0).astype(BF16)
                        gq = gq + lax.dot_general(gz, kh, NN, preferred_element_type=F32)
                        gk_ref[pl.ds(r0, ATTN_KEYS), lanes] += lax.dot_general(gz, qh[h], TN, preferred_element_type=F32)
                    gq_ref[:, lanes] += gq * scale
                return fn

            gq_ref[:, lanes] = jnp.zeros((LANES, LANES), F32)
            _run_groups(n_blocks, 0, 1, make_up)
            return carry

        lax.fori_loop(0, n_pairs, pair, 0)

    wide = n_pairs * LANES
    assert q_cb % n_pairs == 0 and k_cb % n_pairs == 0 and v_cb % n_pairs == 0 and go_cb % n_pairs == 0
    blk = lambda cb: pl.BlockSpec((LANES, wide), lambda b, i: (b * nq + i, cb // n_pairs))
    full = lambda cb: pl.BlockSpec((lp, wide), lambda b, i: (b, cb // n_pairs))
    shp = jax.ShapeDtypeStruct((rows, wide), F32)
    per_block = pltpu.VMEM((2, -(-lp // ATTN_KEYS), LANES, ATTN_KEYS), F32)
    return pl.pallas_call(
        body, name=name, grid=(n_batch, nq),
        in_specs=[blk(q_cb), full(k_cb), full(v_cb), blk(go_cb)],
        out_specs=[blk(0), full(0), full(0)], out_shape=[shp, shp, shp],
        scratch_shapes=[per_block, per_block, pltpu.VMEM((2, LANES, ATTN_KEYS), F32)],
        compiler_params=_params(("parallel", "arbitrary")),
    )(proj, proj, proj, g_out)


def _lb_fn(gamma):
    g0, g1 = gamma[0:1, :], gamma[1:2, :]
    mx = jnp.maximum(g0, g1)
    e0, e1 = jnp.exp(g0 - mx), jnp.exp(g1 - mx)
    p0, p1 = e0 / (e0 + e1), e1 / (e0 + e1)
    return (p0 + p1) - p0


def _lower_bound(name, gamma):
    def body(g_ref, o_ref):
        o_ref[...] = _lb_fn(g_ref[...])

    return pl.pallas_call(body, name=name, out_shape=jax.ShapeDtypeStruct((1, gamma.shape[1]), F32))(gamma)


def _lower_bound_bwd(name, gamma, g_lb_parts, g_ng_parts):
    def body(g_ref, glb_ref, gng_ref, o_ref, o2_ref):
        _, vjp = jax.vjp(_lb_fn, g_ref[...])
        o_ref[...] = vjp(jnp.sum(glb_ref[...], axis=0))[0]
        o2_ref[...] = jnp.sum(gng_ref[...], axis=0)

    return pl.pallas_call(
        body, name=name,
        out_shape=[jax.ShapeDtypeStruct(gamma.shape, F32), jax.ShapeDtypeStruct((1, gamma.shape[1]), F32)],
    )(gamma, g_lb_parts, g_ng_parts)


def _tri_times(tril, x, dims):
    hi = x.astype(BF16)
    lo = (x - hi.astype(F32)).astype(BF16)
    t = tril.astype(BF16)
    return (lax.dot_general(t, hi, dims, preferred_element_type=F32)
            + lax.dot_general(t, lo, dims, preferred_element_type=F32))


@jax.custom_vjp
def _cumsum_rows(x, tril):
    return _tri_times(tril, x, NN)


def _cumsum_rows_fwd(x, tril):
    return _tri_times(tril, x, NN), tril


def _cumsum_rows_bwd(tril, g):
    return _tri_times(tril, g, TN), jnp.zeros_like(tril)


_cumsum_rows.defvjp(_cumsum_rows_fwd, _cumsum_rows_bwd)


def _hg_decays(f_pre, lbs, masks, tril):
    f = [[lb + (1.0 - lb) * jax.nn.sigmoid(fc) for fc, lb in zip(row, lbs)] for row in f_pre]
    bcum = [[_cumsum_rows(jnp.log(x) * m, tril) for x in row] for row, m in zip(f, masks)]
    return [[1.0 - x for x in row] for row in f], bcum


def _hg_step(q, f_pre, i_in, g, lbs, ngs, sts, masks, tril):
    k, bcum = _hg_decays(f_pre, lbs, masks, tril)
    v = [[ic * m for ic in row] for row, m in zip(i_in, masks)]
    qd = [[qc * jnp.exp(b) for qc, b in zip(qr, br)] for qr, br in zip(q, bcum)]
    scores = [[jnp.where(tril > 0.5, _dot(a, kk * jnp.exp(-b), NT), 0.0) for a, kk, b in zip(ar, kr, br)]
              for ar, kr, br in zip(qd, k, bcum)]
    inner = [[_dot(s, x, NN) for s, x in zip(sr, vr)] for sr, vr in zip(scores, v)]
    add = [[_dot(x, kk * jnp.exp(b[HG_CHUNK - 1:, :] - b), TN) for x, kk, b in zip(vr, kr, br)]
           for vr, kr, br in zip(v, k, bcum)]
    outs = []
    for qr, br, nr, ar, gr in zip(qd, bcum, inner, add, g):
        o = [n + _dot(a, st, NT) for n, a, st in zip(nr, qr, sts)]
        sts = [jnp.exp(b[HG_CHUNK - 1:, :]) * st + a for b, a, st in zip(br, ar, sts)]
        o = [x * lax.rsqrt(jnp.mean(x * x, axis=-1, keepdims=True) + RMS_EPS) * ng for x, ng in zip(o, ngs)]
        outs.append([x * (gc * jax.nn.sigmoid(gc)) for x, gc in zip(o, gr)])
    return outs, sts


def _hg_consts(c, pad):
    r = lax.broadcasted_iota(jnp.int32, (HG_CHUNK, HG_CHUNK), 0)
    cc = lax.broadcasted_iota(jnp.int32, (HG_CHUNK, HG_CHUNK), 1)
    tril = jnp.where(r >= cc, 1.0, 0.0).astype(F32)
    pos = c * HG_CHUNK + lax.broadcasted_iota(jnp.int32, (HG_CHUNK, 1), 0)
    return tril, jnp.where(pos >= pad, 1.0, 0.0).astype(F32)


HG_HEADS_PER_STEP = 8
HG_CHUNKS_PER_STEP = 2


def _hg_layout(lp, n_heads):
    step_rows = HG_CHUNKS_PER_STEP * HG_CHUNK
    per = min(HG_HEADS_PER_STEP, n_heads)
    assert lp % step_rows == 0 and n_heads % per == 0
    heads = [(h, slice(h * HG_DK, (h + 1) * HG_DK)) for h in range(per)]
    return n_heads // per, lp // step_rows, step_rows, per * HG_DK, heads


def _hg_step_views(step, pad, heads):
    slices = [slice(u * HG_CHUNK, (u + 1) * HG_CHUNK) for u in range(HG_CHUNKS_PER_STEP)]
    consts = [_hg_consts(step * HG_CHUNKS_PER_STEP + u, pad) for u in range(HG_CHUNKS_PER_STEP)]
    load = lambda ref: [[ref[sl, cols] for _, cols in heads] for sl in slices]
    return slices, [m for _, m in consts], consts[0][0], load


def _hgrn_fwd(name, proj, lb, ng, n_batch, lp, pad, n_heads):
    rows = proj.shape[0]
    groups, steps, step_rows, wide, heads = _hg_layout(lp, n_heads)

    def body(q_ref, f_ref, i_ref, g_ref, lb_ref, ng_ref, o_ref, s_ref, st_s):
        t = pl.program_id(2)

        @pl.when(t == 0)
        def _():
            st_s[...] = jnp.zeros_like(st_s)

        slices, masks, tril, load = _hg_step_views(t, pad, heads)
        sts = [st_s[h] for h, _ in heads]
        for (_, cols), st in zip(heads, sts):
            s_ref[:, cols] = st
        outs, sts = _hg_step(load(q_ref), load(f_ref), load(i_ref), load(g_ref),
                             [lb_ref[:, cols] for _, cols in heads], [ng_ref[:, cols] for _, cols in heads],
                             sts, masks, tril)
        for sl, row in zip(slices, outs):
            for (_, cols), o in zip(heads, row):
                o_ref[sl, cols] = o
        for (h, _), st in zip(heads, sts):
            st_s[h] = st

    col = lambda off: pl.BlockSpec((step_rows, wide), lambda b, h, t: (b * steps + t, off * groups + h))
    vec = pl.BlockSpec((1, wide), lambda b, h, t: (0, h))
    return pl.pallas_call(
        body, name=name, grid=(n_batch, groups, steps), in_specs=[col(0), col(1), col(2), col(3), vec, vec],
        out_specs=[col(0), pl.BlockSpec((HG_DK, wide), lambda b, h, t: (b * steps + t, h))],
        out_shape=[jax.ShapeDtypeStruct((rows, n_heads * HG_DK), F32),
                   jax.ShapeDtypeStruct((n_batch * steps * HG_DK, n_heads * HG_DK), F32)],
        scratch_shapes=[pltpu.VMEM((len(heads), HG_DK, HG_DK), F32)],
        compiler_params=_params(("parallel", "parallel", "arbitrary")),
    )(proj, proj, proj, proj, lb, ng)


def _hgrn_bwd(name, proj, lb, ng, g_out, states, n_batch, lp, pad, n_heads):
    rows = proj.shape[0]
    width = n_heads * HG_DK
    groups, steps, step_rows, wide, heads = _hg_layout(lp, n_heads)

    def body(q_ref, f_ref, i_ref, g_ref, lb_ref, ng_ref, go_ref, s_ref, gq_ref, gf_ref, gi_ref, gg_ref, glb_ref, gng_ref, gst_s):
        t = pl.program_id(2)

        @pl.when(t == 0)
        def _():
            gst_s[...] = jnp.zeros_like(gst_s)
            glb_ref[...] = jnp.zeros_like(glb_ref)
            gng_ref[...] = jnp.zeros_like(gng_ref)

        slices, masks, tril, load = _hg_step_views(steps - 1 - t, pad, heads)
        fn = functools.partial(_hg_step, masks=masks, tril=tril)
        _, vjp = jax.vjp(fn, load(q_ref), load(f_ref), load(i_ref), load(g_ref),
                         [lb_ref[:, cols] for _, cols in heads], [ng_ref[:, cols] for _, cols in heads],
                         [s_ref[:, cols] for _, cols in heads])
        gq, gf, gi, gg, glb, gng, gst = vjp((load(go_ref), [gst_s[h] for h, _ in heads]))
        for ref, grads in ((gq_ref, gq), (gf_ref, gf), (gi_ref, gi), (gg_ref, gg)):
            for sl, row in zip(slices, grads):
                for (_, cols), x in zip(heads, row):
                    ref[sl, cols] = x.astype(BF16)
        for (h, cols), a, b, c in zip(heads, gst, glb, gng):
            gst_s[h] = a
            glb_ref[:, cols] += b
            gng_ref[:, cols] += c

    col = lambda off: pl.BlockSpec((step_rows, wide), lambda b, h, t: (b * steps + steps - 1 - t, off * groups + h))
    vec = pl.BlockSpec((1, wide), lambda b, h, t: (0, h))
    part = pl.BlockSpec((None, 1, wide), lambda b, h, t: (b, 0, h))
    big = jax.ShapeDtypeStruct((rows, width), BF16)
    small = jax.ShapeDtypeStruct((n_batch, 1, width), F32)
    return pl.pallas_call(
        body, name=name, grid=(n_batch, groups, steps),
        in_specs=[col(0), col(1), col(2), col(3), vec, vec, col(0),
                  pl.BlockSpec((HG_DK, wide), lambda b, h, t: (b * steps + steps - 1 - t, h))],
        out_specs=[col(0), col(0), col(0), col(0), part, part],
        out_shape=[big, big, big, big, small, small],
        scratch_shapes=[pltpu.VMEM((len(heads), HG_DK, HG_DK), F32)],
        compiler_params=_params(("parallel", "parallel", "arbitrary")),
    )(proj, proj, proj, proj, lb, ng, g_out, states)


def _exchange_copies(src, dst, send, recv, loc, scatter):
    x, y, c = lax.axis_index("x"), lax.axis_index("y"), lax.axis_index("c")
    me = 4 * x + 2 * y + c
    local, remote = [], []
    for w in range(len(src)):
        local.append(pltpu.make_async_copy(src[w].at[me] if scatter else src[w], dst[w].at[me], loc.at[w]))
    for k in range(1, N_DEV):
        px = 1 - x if k & 4 else x
        py = 1 - y if k & 2 else y
        pc = 1 - c if k & 1 else c
        peer = 4 * px + 2 * py + pc
        for w in range(len(src)):
            remote.append(pltpu.make_async_remote_copy(
                src_ref=src[w].at[peer] if scatter else src[w], dst_ref=dst[w].at[me],
                send_sem=send.at[w * (N_DEV - 1) + k - 1], recv_sem=recv.at[w * (N_DEV - 1) + k - 1],
                device_id=(px, py, pc), device_id_type=pl.DeviceIdType.MESH))
    return local, remote


_HBM_SPEC = pl.BlockSpec(memory_space=pltpu.HBM)
_SEM_SPEC = pl.BlockSpec(memory_space=pltpu.SEMAPHORE)
_ANY_SPEC = pl.BlockSpec(memory_space=pl.ANY)
_DATAFLOW = pltpu.SideEffectType.DATAFLOW_SIDE_EFFECTING


def _exchange_start(name, srcs, scatter, dep=None):
    nw = len(srcs)
    srcs = [pltpu.with_memory_space_constraint(s, pltpu.HBM) for s in srcs]
    lands = [pltpu.with_memory_space_constraint(lax.empty(s.shape if scatter else (N_DEV,) + s.shape, s.dtype), pltpu.HBM)
             for s in srcs]
    deps = [] if dep is None else [dep]

    def body(*refs):
        src, dst = refs[:nw], refs[nw:2 * nw]
        send, recv, loc = refs[2 * nw + len(deps):2 * nw + len(deps) + 3]
        token = refs[-1]
        local, remote = _exchange_copies(src, dst, send, recv, loc, scatter)
        for cp in local + remote:
            cp.start()
        token[...] = jnp.zeros_like(token)

    sems = [pltpu.SemaphoreType.DMA((nw * (N_DEV - 1),)), pltpu.SemaphoreType.DMA((nw * (N_DEV - 1),)),
            pltpu.SemaphoreType.DMA((nw,))]
    out = pl.pallas_call(
        body, name=name,
        out_shape=(*sems, *[pltpu.HBM(s.shape, s.dtype) for s in srcs], *[pltpu.HBM(s.shape, s.dtype) for s in lands],
                   jax.ShapeDtypeStruct((SUBLANES, LANES), F32)),
        in_specs=[_HBM_SPEC] * (2 * nw) + [_ANY_SPEC] * len(deps),
        out_specs=(_SEM_SPEC, _SEM_SPEC, _SEM_SPEC, *[_HBM_SPEC] * (2 * nw), pl.BlockSpec(memory_space=pltpu.VMEM)),
        input_output_aliases={i: 3 + i for i in range(2 * nw)},
        compiler_params=pltpu.CompilerParams(has_side_effects=_DATAFLOW),
    )(*srcs, *lands, *deps)
    return {"sems": out[:3], "srcs": out[3:3 + nw], "lands": out[3 + nw:3 + 2 * nw], "token": out[-1], "scatter": scatter}


def _exchange_wait(name, handle, after):
    nw = len(handle["srcs"])
    scatter = handle["scatter"]

    def body(*refs):
        src, dst = refs[:nw], refs[nw:2 * nw]
        send, recv, loc = refs[2 * nw:2 * nw + 3]
        local, remote = _exchange_copies(src, dst, send, recv, loc, scatter)
        for cp in local:
            cp.wait()
        for cp in remote:
            cp.wait_send()
            cp.wait_recv()

    out = pl.pallas_call(
        body, name=name,
        out_shape=(*[pltpu.HBM(s.shape, s.dtype) for s in handle["srcs"]],
                   *[pltpu.HBM(s.shape, s.dtype) for s in handle["lands"]]),
        in_specs=[_HBM_SPEC] * (2 * nw) + [_SEM_SPEC] * 3 + [_ANY_SPEC],
        out_specs=tuple([_HBM_SPEC] * (2 * nw)),
        input_output_aliases={i: i for i in range(2 * nw)},
        compiler_params=pltpu.CompilerParams(has_side_effects=_DATAFLOW),
    )(*handle["srcs"], *handle["lands"], *handle["sems"], after)
    return list(out[nw:])


def _adamw(w, g, m, v):
    m = ADAM_B1 * m + (1.0 - ADAM_B1) * g
    v = ADAM_B2 * v + (1.0 - ADAM_B2) * (g * g)
    m_hat = m / (1.0 - ADAM_B1 ** ADAM_STEP)
    v_hat = v / (1.0 - ADAM_B2 ** ADAM_STEP)
    delta = -ADAM_LR * (m_hat / (jnp.sqrt(v_hat) + ADAM_EPS) + ADAM_WD * w)
    return delta, m, v


def _adamw_summed(name, parts, w, m, v):
    layered = w.ndim == 3
    parts = list(parts) if layered else [parts]
    n_layers = len(parts)
    rows, cols = w.shape[-2:]
    n_parts = parts[0].shape[0]
    tr = _tile(rows, max(SUBLANES, (1 << 18) // cols))

    def body(*refs):
        p_refs = refs[:n_layers]
        w_ref, m_ref, v_ref, g_ref, d_ref, nm_ref, nv_ref = refs[n_layers:]
        layer = pl.program_id(0)

        def run(p_ref):
            g = p_ref[0].astype(F32)
            for s in range(1, n_parts):
                g = g + p_ref[s].astype(F32)
            d, nm, nv = _adamw(w_ref[...], g, m_ref[...], v_ref[...])
            g_ref[...] = g
            d_ref[...] = d
            nm_ref[...] = nm
            nv_ref[...] = nv

        for l in range(n_layers):
            pl.when(layer == l)(functools.partial(run, p_refs[l]))

    if layered:
        spec = pl.BlockSpec((None, tr, cols), lambda l, i: (l, i, 0))
    else:
        spec = pl.BlockSpec((tr, cols), lambda l, i: (i, 0))
    p_specs = [pl.BlockSpec((n_parts, tr, cols), lambda l, i, q=q: (0, jnp.where(l == q, i, 0), 0))
               for q in range(n_layers)]
    shp = jax.ShapeDtypeStruct(w.shape, F32)
    return pl.pallas_call(
        body, name=name, grid=(n_layers, rows // tr), in_specs=[*p_specs, spec, spec, spec],
        out_specs=[spec] * 4, out_shape=[shp] * 4, compiler_params=_params(("parallel", "parallel")),
    )(*parts, w, m, v)


def _pack_rows(arrays, cols):
    out = []
    for a in arrays:
        flat = a.reshape(-1)
        n = -(-flat.shape[0] // cols) * cols
        out.append(jnp.pad(flat, (0, n - flat.shape[0])).reshape(-1, cols))
    packed = jnp.concatenate(out, axis=0)
    return jnp.pad(packed, ((0, -packed.shape[0] % SUBLANES), (0, 0)))


def _unpack_rows(packed, shapes, cols):
    out, r = [], 0
    for s in shapes:
        n = math.prod(s)
        nr = -(-n // cols)
        out.append(packed[r:r + nr].reshape(-1)[:n].reshape(s))
        r += nr
    return out


def _block_diag(blocks):
    g, a, b = blocks.shape
    eye = jnp.eye(g, dtype=blocks.dtype)
    return (eye[:, None, :, None] * blocks[:, :, None, :]).reshape(g * a, g * b)


def _diag_blocks(dense, g):
    a, b = dense.shape[0] // g, dense.shape[1] // g
    return jnp.einsum("gagb->gab", dense.reshape(g, a, g, b))


def _local_step(x, target, meta, wts, small, late_weights, on_grads, on_small):
    n_batch, seq, d = x.shape
    n_meta = meta.shape[0]
    pad = -(seq + n_meta) % LANES
    lead = pad + n_meta
    lp = lead + seq
    rows = n_batch * lp
    s5w = wts["glu"].shape[0]
    n_ab = wts["in_ab"].shape[2]
    ab_cols = wts["in_ab"].shape[0] * n_ab
    sbw = (ab_cols - s5w) // 3
    dff = small["mlp_b_up"].shape[1]
    n_pairs = sbw // LANES
    n_hg = d // HG_DK
    s5_cb = s5w // LANES
    sb_cb = sbw // LANES
    tm = _tile(rows, ROW_TILE)
    groups, n_state, grp = small["s5_b_re"].shape[1:]
    ns = groups * n_state
    sw = min(SCAN_LANES, ns)

    h0 = jnp.concatenate(
        [jnp.zeros((n_batch, pad, d), F32), jnp.broadcast_to(meta[None], (n_batch, n_meta, d)), x], axis=1
    ).reshape(rows, d)

    lam_re, lam_im = small["s5_lam_re"][0], small["s5_lam_im"][0]
    log_dt = small["s5_log_dt"][0][:, None]
    b_re_t = small["s5_b_re"][0].transpose(0, 2, 1)
    b_im_t = small["s5_b_im"][0].transpose(0, 2, 1)
    c_re, c_im = small["s5_c_re"][0], small["s5_c_im"][0]
    lbr, lbi, bbr, bbi = _s5_params("s5_params", lam_re, lam_im, log_dt, b_re_t, b_im_t)
    b_blk = _interleave(_block_diag(bbr), _block_diag(bbi), sw).astype(BF16)
    c_blk = _interleave(_block_diag(c_re), _block_diag(-c_im), sw).T.astype(BF16)
    lam_row = _interleave(lbr.reshape(1, ns), lbi.reshape(1, ns), sw)
    d_row = small["s5_d"].reshape(1, s5w)

    def ln_store(outs, acc, res, bias, g, b):
        r = ALPHA * res + acc + bias
        outs[0][...] = r
        if len(outs) > 1:
            outs[1][...] = _ln(r, g, b)

    zero_bias = jnp.zeros((1, d), F32)

    def mix_ln(name, a, w, k_total, tk, res, bias, g, b, a_fn=None, emit_h=True):
        n_out = 2 if emit_h else 1
        return _mm_act(name, a, w, "nat", n_out_cols=d, k_total=k_total, tn=d, tk=tk, a_fn=a_fn,
                       extras=(res, bias, g, b), extra_specs=(_row_spec(tm, d), _vec_spec(d), _vec_spec(d), _vec_spec(d)),
                       store=ln_store, out_shape=[jax.ShapeDtypeStruct((rows, d), F32)] * n_out,
                       out_specs=[_row_spec(tm, d)] * n_out)

    def two(width):
        return [jax.ShapeDtypeStruct((rows, width), F32)] * 2, [_row_spec(tm, width)] * 2

    def shard_tile(total, shard, cap=1024):
        t = max(shard, cap - cap % shard)
        while total % t:
            t -= shard
        return t

    proj_ab = _mm_act("in_ab", h0, wts["in_ab"], "stk", n_out_cols=ab_cols, k_total=d, tn=shard_tile(ab_cols, n_ab), tk=d)[0]
    bu = _mm_act("s5_bu", proj_ab, b_blk, "nat", n_out_cols=2 * ns, k_total=s5w, tn=min(2 * ns, 2048), tk=s5w)[0]
    states = _s5_scan("s5_scan", bu, lam_row, n_batch, lp, sw)

    def gelu_store(outs, acc, u, dv):
        ypre = acc + dv * u
        outs[0][...] = ypre
        outs[1][...] = jax.nn.gelu(ypre)

    shp2, spec2 = two(s5w)
    ypre, y = _mm_act(
        "s5_y", states, c_blk, "nat", n_out_cols=s5w, k_total=2 * ns, tn=s5w, tk=min(2 * ns, 1024),
        extras=(proj_ab, d_row), extra_specs=(_row_spec(tm, s5w), _vec_spec(s5w)), store=gelu_store,
        out_shape=shp2, out_specs=spec2)

    def glu_store(outs, acc, yv, bias):
        gate = acc + bias
        outs[0][...] = gate
        outs[1][...] = _glu(yv, gate)

    gate, a_out = _mm_act(
        "s5_glu", y, wts["glu"], "nat", n_out_cols=s5w, k_total=s5w, tn=s5w, tk=s5w,
        extras=(y, small["s5_b_glu"]), extra_specs=(_row_spec(tm, s5w), _vec_spec(s5w)), store=glu_store,
        out_shape=shp2, out_specs=spec2)
    b_out = _attn_fwd("sb_attn", proj_ab, n_batch, lp, pad, s5_cb, s5_cb + sb_cb, s5_cb + 2 * sb_cb, n_pairs)

    def bias_store(outs, acc, bias):
        outs[0][...] = (acc + bias).astype(outs[0].dtype)

    def wide(width, dtype):
        return [jax.ShapeDtypeStruct((rows, dff), dtype)], [_row_spec(tm, width)]

    def mlp_fwd(layer, h_in, emit_h=True):
        tn = shard_tile(dff, n_up)
        shp, spec = wide(tn, BF16)
        up = _mm_act(f"up{layer}", h_in, wts["up"][layer], "stk", n_out_cols=dff, k_total=d, tn=tn, tk=d,
                     extras=(small["mlp_b_up"][layer:layer + 1],), extra_specs=(_vec_spec(tn),), store=bias_store,
                     out_shape=shp, out_specs=spec)[0]
        return (up, *mix_ln(f"down{layer}", up, wts["down"][layer], dff, min(dff, 1024), h_in,
                            small["mlp_b_down"][layer:layer + 1], small["ln_mlp_g"][layer:layer + 1],
                            small["ln_mlp_b"][layer:layer + 1], a_fn=_relu2, emit_h=emit_h))

    r1, h1 = mix_ln("out_ab", [a_out, b_out], wts["out_ab"], s5w + sbw, min(s5w, sbw), h0, zero_bias,
                    small["ln_mix_g"][0:1], small["ln_mix_b"][0:1])
    wts = {**wts, **late_weights(r1)}
    n_c = wts["in_c"].shape[2]
    n_up = wts["up"][0].shape[2]
    up0, r2, h2 = mlp_fwd(0, h1)

    lb = _lower_bound("hg_lb", small["hgrn_gamma"])
    proj_c = _mm_act("in_c", h2, wts["in_c"], "stk", n_out_cols=4 * d, k_total=d, tn=shard_tile(4 * d, n_c), tk=d)[0]
    c_out, hg_states = _hgrn_fwd("hgrn", proj_c, lb, wts["ng"], n_batch, lp, pad, n_hg)
    r3, h3 = mix_ln("out_c", c_out, wts["out_c"], d, d, h2, zero_bias, small["ln_mix_g"][1:2], small["ln_mix_b"][1:2])
    up1, r4 = mlp_fwd(1, h3, emit_h=False)

    gr = {}
    g_r4, gr["ln_mlp_g1"], gr["ln_mlp_b1"], loss_tile = _loss_grad(
        "loss", r4, small["ln_mlp_g"][1:2], small["ln_mlp_b"][1:2], target, n_batch, lp, lead)

    def res_store(outs, acc, g_res):
        outs[0][...] = acc + ALPHA * g_res

    def ln_bwd_store(outs, acc, g_res, r_in, g, b, first_step):
        gr_in, gg, gb = jax.vjp(_ln, r_in, g, b)[1](acc + ALPHA * g_res)
        outs[0][...] = gr_in

        @pl.when(first_step)
        def _():
            outs[1][...] = jnp.zeros_like(outs[1])
            outs[2][...] = jnp.zeros_like(outs[2])

        outs[1][...] += gg
        outs[2][...] += gb

    def through_ln(name, a, w, k_total, tk, g_res, r_in, g, b, dep=None):
        vec = pl.BlockSpec((1, d), lambda i, j, k: (0, 0))
        return _mm_act(name, a, w, "stkT", n_out_cols=d, k_total=k_total, tn=d, tk=tk,
                       extras=(g_res, r_in, g, b), extra_specs=(_row_spec(tm, d), _row_spec(tm, d), vec, vec),
                       store=ln_bwd_store, sequential=True, dep=dep,
                       out_shape=[jax.ShapeDtypeStruct((rows, d), F32)] + [jax.ShapeDtypeStruct((1, d), F32)] * 2,
                       out_specs=[_row_spec(tm, d), vec, vec])

    def mlp_bwd(layer, g_r, up, h_in, r_in, send=None):
        def gup_store(outs, acc, upv):
            outs[0][...] = (acc * (2.0 * jnp.maximum(upv.astype(F32), 0.0))).astype(outs[0].dtype)

        tf = min(dff, 1024)
        shp, spec = wide(tf, BF16)
        g_up = _mm_act(f"g_up{layer}", g_r, wts["down"][layer], "natT", n_out_cols=dff, k_total=d, tn=tf, tk=d,
                       extras=(up,), extra_specs=(_row_spec(tm, tf),), store=gup_store, out_shape=shp, out_specs=spec)[0]
        gr[f"down{layer}"], gr[f"mlp_b_down{layer}"] = _mm_wgrad(
            f"dw_down{layer}", up, g_r, kw=dff, n=d, tmw=tf, tn=d, a_fn=_relu2, out_dtype=BF16, colsum=True)
        gr[f"up{layer}"], gr[f"mlp_b_up{layer}"] = _mm_wgrad(
            f"dw_up{layer}", h_in, g_up, kw=d, n=dff, tmw=d, tn=min(dff, 2048), shard_cols=n_up, out_dtype=BF16, colsum=True)
        dep = send() if send is not None else None
        g_r_in, gr[f"ln_mix_g{layer}"], gr[f"ln_mix_b{layer}"] = through_ln(
            f"g_hmid{layer}", g_up, wts["up"][layer], dff, shard_tile(dff, n_up), g_r, r_in,
            small["ln_mix_g"][layer:layer + 1], small["ln_mix_b"][layer:layer + 1], dep=dep)
        return g_r_in

    g_r3 = mlp_bwd(1, g_r4, up1, h3, r3)
    g_cout = _mm_act("g_cout", g_r3, wts["out_c"], "natT", n_out_cols=d, k_total=d, tn=d, tk=d)[0]
    gr["out_c"] = _mm_wgrad("dw_out_c", c_out, g_r3, kw=d, n=d, tmw=d, tn=d, out_dtype=BF16)
    gq, gf, gi, gg_, g_lb_parts, g_ng_parts = _hgrn_bwd("hgrn_bwd", proj_c, lb, wts["ng"], g_cout, hg_states,
                                                        n_batch, lp, pad, n_hg)
    g_pc = [gq, gf, gi, gg_]
    gr["hgrn_gamma"], gr["ng"] = _lower_bound_bwd("hg_lb_bwd", small["hgrn_gamma"], g_lb_parts, g_ng_parts)
    gr["in_c"] = _mm_wgrad("dw_in_c", h2, g_pc, kw=d, n=4 * d, tmw=d, tn=d, shard_cols=n_c, out_dtype=BF16)
    sent1 = on_grads(1, {"down1": gr["down1"], "up1": gr["up1"], "out_c": gr["out_c"], "in_c": gr["in_c"], "ng": gr["ng"]})
    g_r2, gr["ln_mlp_g0"], gr["ln_mlp_b0"] = through_ln(
        "g_h2", g_pc, wts["in_c"], 4 * d, n_c, g_r3, r2, small["ln_mlp_g"][0:1], small["ln_mlp_b"][0:1], dep=sent1)

    g_r1 = mlp_bwd(0, g_r2, up0, h1, r1, send=lambda: on_grads(2, {"down0": gr["down0"], "up0": gr["up0"]}))
    g_cat = _mm_act("g_cat", g_r1, wts["out_ab"], "natT", n_out_cols=d, k_total=d, tn=d, tk=d)[0]
    gr["out_ab"] = _mm_wgrad("dw_out_ab", [a_out, b_out], g_r1, kw=s5w + sbw, n=d, tmw=min(s5w, sbw), tn=d, out_dtype=BF16)
    g_q, g_k, g_v = _attn_bwd("sb_attn_bwd", proj_ab, g_cat, n_batch, lp, pad, s5_cb, s5_cb + sb_cb, s5_cb + 2 * sb_cb,
                              s5_cb, n_pairs)

    g_y_direct, g_gate = _rowwise("s5_glu_bwd", lambda ga, yv, gt: jax.vjp(_glu, yv, gt)[1](ga),
                                  [(g_cat, 0, s5w), (y, 0, s5w), (gate, 0, s5w)], 2, s5w)

    def gelu_bwd_store(outs, acc, gyd, yp, u, dv):
        gyp = jax.vjp(jax.nn.gelu, yp)[1](acc + gyd)[0]
        outs[0][...] = gyp
        outs[1][...] = dv * gyp
        outs[2][...] = jnp.sum(gyp * u, axis=0, keepdims=True)

    rs = _row_spec(tm, s5w)
    g_ypre, g_u_direct, gd_parts = _mm_act(
        "s5_g_y", g_gate, wts["glu"], "natT", n_out_cols=s5w, k_total=s5w, tn=s5w, tk=s5w,
        extras=(g_y_direct, ypre, proj_ab, d_row), extra_specs=(rs, rs, rs, _vec_spec(s5w)), store=gelu_bwd_store,
        out_shape=[jax.ShapeDtypeStruct((rows, s5w), F32)] * 2 + [jax.ShapeDtypeStruct((rows // tm, 1, s5w), F32)],
        out_specs=[rs, rs, pl.BlockSpec((None, 1, s5w), lambda i, j, k: (i, 0, j))])
    gr["glu"], gr["s5_b_glu"] = _mm_wgrad("dw_glu", y, g_gate, kw=s5w, n=s5w, tmw=s5w, tn=s5w, out_dtype=BF16, colsum=True)
    g_sd = _mm_act("s5_g_states", g_ypre, c_blk, "natT", n_out_cols=2 * ns, k_total=s5w, tn=min(2 * ns, 2048), tk=s5w)[0]
    d_cblk = _mm_wgrad("dw_cblk", states, g_ypre, kw=2 * ns, n=s5w, tmw=min(2 * ns, 1024), tn=s5w)
    gs, gl_parts = _s5_scan_bwd("s5_scan_bwd", g_sd, states, lam_row, n_batch, lp, sw)

    def add_store(outs, acc, other):
        outs[0][...] = acc + other

    g_u = _mm_act("s5_g_u", gs, b_blk, "natT", n_out_cols=s5w, k_total=2 * ns, tn=s5w, tk=min(2 * ns, 1024),
                  extras=(g_u_direct,), extra_specs=(rs,), store=add_store)[0]
    d_bblk = _mm_wgrad("dw_bblk", proj_ab, gs, kw=s5w, n=2 * ns, tmw=s5w, tn=min(2 * ns, 2048))
    db_re, db_im = _deinterleave(d_bblk, sw)
    dc_re, dc_im = _deinterleave(d_cblk.T, sw)
    glr, gli = _deinterleave(gl_parts, sw)
    g_lam_re, g_lam_im, g_log_dt, g_b_re_t, g_b_im_t, g_d = _s5_params_bwd(
        "s5_params_bwd", lam_re, lam_im, log_dt, b_re_t, b_im_t,
        glr.reshape(n_batch, groups, n_state), gli.reshape(n_batch, groups, n_state),
        _diag_blocks(db_re, groups), _diag_blocks(db_im, groups), gd_parts)

    cat2 = lambda key: jnp.concatenate([gr[key + "0"], gr[key + "1"]], axis=0)
    small_sent = on_small({
        "s5_lam_re": g_lam_re[None], "s5_lam_im": g_lam_im[None], "s5_log_dt": g_log_dt.reshape(1, groups),
        "s5_b_re": g_b_re_t.transpose(0, 2, 1)[None], "s5_b_im": g_b_im_t.transpose(0, 2, 1)[None],
        "s5_c_re": _diag_blocks(dc_re, groups)[None], "s5_c_im": -_diag_blocks(dc_im, groups)[None],
        "s5_d": g_d.reshape(1, groups, grp), "s5_b_glu": gr["s5_b_glu"], "hgrn_gamma": gr["hgrn_gamma"],
        "ln_mix_g": cat2("ln_mix_g"), "ln_mix_b": cat2("ln_mix_b"), "mlp_b_up": cat2("mlp_b_up"),
        "mlp_b_down": cat2("mlp_b_down"), "ln_mlp_g": cat2("ln_mlp_g"), "ln_mlp_b": cat2("ln_mlp_b"),
    }, loss_tile)

    g_pab = [g_u, g_q, g_k, g_v]
    assert s5w == sbw
    gr["in_ab"] = _mm_wgrad("dw_in_ab", h0, g_pab, kw=d, n=ab_cols, tmw=d, tn=s5w, shard_cols=n_ab, out_dtype=BF16,
                            dep=small_sent)
    g_h0 = _mm_act("g_h0", g_pab, wts["in_ab"], "stkT", n_out_cols=d, k_total=ab_cols, tn=d, tk=shard_tile(s5w, n_ab),
                   extras=(g_r1,), extra_specs=(_row_spec(tm, d),), store=res_store)[0]
    grad_x = g_h0.reshape(n_batch, lp, d)[:, lead:, :]
    g_meta = _meta_grad("g_meta", g_h0, n_batch, lp, pad, n_meta)
    on_grads(3, {"meta": g_meta, "in_ab": gr["in_ab"], "glu": gr["glu"], "out_ab": gr["out_ab"]})
    return grad_x


SMALL_NAMES = ("s5_lam_re", "s5_lam_im", "s5_log_dt", "s5_b_re", "s5_b_im", "s5_c_re", "s5_c_im", "s5_d", "s5_b_glu",
               "hgrn_gamma", "ln_mix_g", "ln_mix_b", "mlp_b_up", "mlp_b_down", "ln_mlp_g", "ln_mlp_b")
WEIGHT_ORDER = ("meta", "w_in_ab", "s5_lam_re", "s5_lam_im", "s5_log_dt", "s5_b_re", "s5_b_im", "s5_c_re", "s5_c_im",
                "s5_d", "s5_w_glu", "s5_b_glu", "w_out_ab", "w_in_c", "hgrn_gamma", "hgrn_norm_g", "w_out_c", "ln_mix_g",
                "ln_mix_b", "mlp_w_up", "mlp_b_up", "mlp_w_down", "mlp_b_down", "ln_mlp_g", "ln_mlp_b")


def kernel(x, meta, w_in_ab, s5_lam_re, s5_lam_im, s5_log_dt, s5_b_re, s5_b_im, s5_c_re, s5_c_im, s5_d, s5_w_glu, s5_b_glu, w_out_ab, w_in_c, hgrn_gamma, hgrn_norm_g, w_out_c, ln_mix_g, ln_mix_b, mlp_w_up, mlp_b_up, mlp_w_down, mlp_b_down, ln_mlp_g, ln_mlp_b, loss_target, m_meta, m_w_in_ab, m_s5_lam_re, m_s5_lam_im, m_s5_log_dt, m_s5_b_re, m_s5_b_im, m_s5_c_re, m_s5_c_im, m_s5_d, m_s5_w_glu, m_s5_b_glu, m_w_out_ab, m_w_in_c, m_hgrn_gamma, m_hgrn_norm_g, m_w_out_c, m_ln_mix_g, m_ln_mix_b, m_mlp_w_up, m_mlp_b_up, m_mlp_w_down, m_mlp_b_down, m_ln_mlp_g, m_ln_mlp_b, v_meta, v_w_in_ab, v_s5_lam_re, v_s5_lam_im, v_s5_log_dt, v_s5_b_re, v_s5_b_im, v_s5_c_re, v_s5_c_im, v_s5_d, v_s5_w_glu, v_s5_b_glu, v_w_out_ab, v_w_in_c, v_hgrn_gamma, v_hgrn_norm_g, v_w_out_c, v_ln_mix_g, v_ln_mix_b, v_mlp_w_up, v_mlp_b_up, v_mlp_w_down, v_mlp_b_down, v_ln_mlp_g, v_ln_mlp_b):
    args = dict(locals())
    w = {n: args[n] for n in WEIGHT_ORDER}
    mom = {n: args["m_" + n] for n in WEIGHT_ORDER}
    var = {n: args["v_" + n] for n in WEIGHT_ORDER}
    d = x.shape[2]
    n_meta = meta.shape[0]

    cast = lambda a: a.astype(BF16)
    early = _exchange_start("gather_early_start", [w["meta"], cast(w["w_in_ab"][0]), cast(w["s5_w_glu"][0]),
                                                   cast(w["w_out_ab"][0])], False)
    late = _exchange_start("gather_late_start", [w["hgrn_norm_g"], cast(w["w_in_c"][0]), cast(w["w_out_c"][0]),
                                                 cast(w["mlp_w_up"][0]), cast(w["mlp_w_up"][1]),
                                                 cast(w["mlp_w_down"][0]), cast(w["mlp_w_down"][1])], False, dep=early["token"])
    a_meta, a_in_ab, a_glu, a_out_ab = _exchange_wait("gather_early_wait", early, late["token"])
    wts = {"in_ab": a_in_ab, "glu": a_glu.reshape(-1, a_glu.shape[2]), "out_ab": a_out_ab.reshape(-1, d)}
    meta_full = a_meta.transpose(1, 0, 2).reshape(n_meta, d)
    small = {n: w[n] for n in SMALL_NAMES}

    def late_weights(after):
        a_ng, a_in_c, a_out_c, a_up0, a_up1, a_dn0, a_dn1 = _exchange_wait("gather_late_wait", late, after)
        return {"in_c": a_in_c, "ng": a_ng.transpose(1, 0, 2).reshape(1, d), "out_c": a_out_c.reshape(-1, d),
                "up": [a_up0, a_up1], "down": [a_dn0.reshape(-1, d), a_dn1.reshape(-1, d)]}

    n_loc = d // N_DEV
    rows_of = lambda g: g.reshape(N_DEV, -1, g.shape[-1])
    cols_of = lambda g: g.reshape(g.shape[0], N_DEV, n_loc).transpose(1, 0, 2)
    sent = {}

    def on_grads(stage, g):
        if stage == 1:
            order = (("mlp_w_down", 1), ("mlp_w_up", 1), ("w_out_c", 0), ("w_in_c", 0), ("hgrn_norm_g", None))
            parts = [rows_of(g["down1"]), g["up1"], rows_of(g["out_c"]), g["in_c"], cols_of(g["ng"])]
        elif stage == 2:
            order = (("mlp_w_down", 0), ("mlp_w_up", 0))
            parts = [rows_of(g["down0"]), g["up0"]]
        else:
            order = (("w_out_ab", 0), ("s5_w_glu", 0), ("w_in_ab", 0), ("meta", None))
            parts = [rows_of(g["out_ab"]), rows_of(g["glu"]), g["in_ab"], cols_of(g["meta"])]
        sent[stage] = (order, _exchange_start(f"scatter_start{stage}", parts, True))
        return sent[stage][1]["token"]

    def on_small(sg, loss_tile):
        g_pack = _pack_rows([sg[n] for n in SMALL_NAMES] + [loss_tile], PACK_COLS)
        sent["small"] = _exchange_start("gather_small_start", [g_pack], False)
        return sent["small"]["token"]

    grad_x = _local_step(x, loss_target, meta_full, wts, small, late_weights, on_grads, on_small)
    small_sent = sent["small"]
    tile = (SUBLANES, LANES)
    shapes = [w[n].shape for n in SMALL_NAMES] + [tile]
    zeros = jnp.zeros(tile, F32)
    w_pack = _pack_rows([w[n] for n in SMALL_NAMES] + [zeros], PACK_COLS)
    m_pack = _pack_rows([mom[n] for n in SMALL_NAMES] + [zeros], PACK_COLS)
    v_pack = _pack_rows([var[n] for n in SMALL_NAMES] + [zeros], PACK_COLS)

    received, res = {}, {}

    def wait(stage, after):
        order, handle = sent[stage]
        for key, rc in zip(order, _exchange_wait(f"scatter_wait{stage}", handle, after)):
            received[key] = rc

    def update(nm):
        layered = w[nm].ndim == 3
        parts = [received[(nm, l)] for l in range(w[nm].shape[0])] if layered else received[(nm, None)]
        res[nm] = _adamw_summed(f"adamw_{nm}", parts, w[nm], mom[nm], var[nm])
        return res[nm][0]

    wait(1, sent[3][1]["token"])
    done = [update(nm) for nm in ("w_out_c", "w_in_c", "hgrn_norm_g")]
    wait(2, done[0])
    done = [update(nm) for nm in ("mlp_w_up", "mlp_w_down")]
    g_all = _exchange_wait("gather_small_wait", small_sent, done[0])[0]
    packed = _adamw_summed("adamw_small", g_all, w_pack, m_pack, v_pack)
    wait(3, packed[0])
    for nm in ("w_out_ab", "s5_w_glu", "w_in_ab", "meta"):
        update(nm)
    unpacked = [_unpack_rows(p, shapes, PACK_COLS) for p in packed]
    loss = unpacked[0][-1][0, 0]

    def pick(nm, which):
        return unpacked[which][SMALL_NAMES.index(nm)] if nm in SMALL_NAMES else res[nm][which]

    return (loss, grad_x, *[pick(n, 0) for n in WEIGHT_ORDER], *[pick(n, 1) for n in WEIGHT_ORDER],
            *[pick(n, 2) for n in WEIGHT_ORDER], *[pick(n, 3) for n in WEIGHT_ORDER])
```

```python
import functools
import math

import jax
import jax.numpy as jnp
from jax import lax
from jax.experimental import pallas as pl
from jax.experimental.pallas import tpu as pltpu

F32 = jnp.float32
BF16 = jnp.bfloat16

N_DEV = 8
DEPTH = 2
ALPHA = (2.0 * DEPTH) ** 0.25
LN_EPS = 1e-5
RMS_EPS = 1e-6
SB_HEAD_DIM = 64
HG_DK = 128
HG_CHUNK = 64
LANES = 128
SUBLANES = 8
PACKED_ROWS = 16
VMEM_LIMIT_BYTES = 56 * 1024 * 1024
ROW_TILE = 1088
SCAN_LANES = 256
SCAN_UNROLL = 2
PACK_COLS = 1024

ADAM_LR = 0.001
ADAM_B1 = 0.9
ADAM_B2 = 0.999
ADAM_EPS = 1e-08
ADAM_WD = 0.01
ADAM_STEP = 10

NN = (((1,), (0,)), ((), ()))
NT = (((1,), (1,)), ((), ()))
TN = (((0,), (0,)), ((), ()))


def _tile(n, pref, align=SUBLANES):
    t = min(n, pref)
    t -= t % align
    while t >= align:
        if n % t == 0:
            return t
        t -= align
    return n


def _unrolled_loop(n, body, init, unroll):
    assert n % unroll == 0

    def outer(t, carry):
        for u in range(unroll):
            carry = body(t * unroll + u, carry)
        return carry

    return lax.fori_loop(0, n // unroll, outer, init)


def _params(sem):
    return pltpu.CompilerParams(dimension_semantics=sem, vmem_limit_bytes=VMEM_LIMIT_BYTES)


def _dot_raw(a, b, dims):
    return lax.dot_general(a.astype(BF16), b.astype(BF16), dims, preferred_element_type=F32)


def _make_dot(dims, da_rule, db_rule):
    @jax.custom_vjp
    def f(a, b):
        return _dot_raw(a, b, dims)

    def fwd(a, b):
        return _dot_raw(a, b, dims), (a, b)

    def bwd(res, g):
        a, b = res
        return da_rule(g, a, b), db_rule(g, a, b)

    f.defvjp(fwd, bwd)
    return f


_DOTS = {
    NN: _make_dot(NN, lambda g, a, b: _dot_raw(g, b, NT), lambda g, a, b: _dot_raw(a, g, TN)),
    NT: _make_dot(NT, lambda g, a, b: _dot_raw(g, b, NN), lambda g, a, b: _dot_raw(g, a, TN)),
    TN: _make_dot(TN, lambda g, a, b: _dot_raw(b, g, NT), lambda g, a, b: _dot_raw(a, g, NN)),
}


def _dot(a, b, dims):
    return _DOTS[dims](a, b)


def _running_sums(a, tri_ones, split=False):
    hi = a.astype(BF16)
    out = lax.dot_general(hi, tri_ones, NN, preferred_element_type=F32)
    if split:
        lo = (a - hi.astype(F32)).astype(BF16)
        out = out + lax.dot_general(lo, tri_ones, NN, preferred_element_type=F32)
    return out


def _piece_specs(pieces, block_rows, block_cols, row_of, col_of, cb0):
    per = pieces[0].shape[1] // block_cols if len(pieces) > 1 else None
    specs = []
    for p in range(len(pieces)):
        if per is None:
            specs.append(pl.BlockSpec((block_rows, block_cols), lambda *g: (row_of(*g), cb0 + col_of(*g))))
        else:
            specs.append(pl.BlockSpec(
                (block_rows, block_cols),
                lambda *g, p=p: (row_of(*g), jnp.clip(col_of(*g) - p * per, 0, per - 1))))
    return specs, per


def _mm_call(name, grid, dims, a_pieces, a_specs, a_sel, b_pieces, b_specs, b_sel, extras, extra_specs,
             out_shape, out_specs, acc_shape, a_fn, store, colsum_width=0, sequential=False, deps=()):
    na, nb, ne, no, nd = len(a_pieces), len(b_pieces), len(extras), len(out_shape), len(deps)
    nk = grid[2]

    def body(*refs):
        a_refs, b_refs = refs[:na], refs[na:na + nb]
        extra = refs[na + nb:na + nb + ne]
        outs = refs[na + nb + ne + nd:na + nb + ne + nd + no]
        acc = refs[na + nb + ne + nd + no]
        ids = (pl.program_id(0), pl.program_id(1), pl.program_id(2))
        k = ids[2]

        @pl.when(k == 0)
        def _():
            acc[...] = jnp.zeros_like(acc)

        def run(a_ref, b_ref):
            a = a_ref[...]
            if a_fn is not None:
                a = a_fn(a)
            b = b_ref[...]
            if b.ndim == 3 and dims == NN:
                n = b.shape[2]
                for q in range(b.shape[0]):
                    acc[:, q * n:(q + 1) * n] += _dot_raw(a, b[q], dims)
            elif b.ndim == 3:
                n = b.shape[2]
                for q in range(b.shape[0]):
                    acc[...] += _dot_raw(a[:, q * n:(q + 1) * n], b[q], dims)
            else:
                acc[...] += _dot_raw(a, b, dims)
            if colsum_width:
                cs = refs[-1]
                first = ids[1] == 0

                @pl.when(first & (k == 0))
                def _():
                    cs[...] = jnp.zeros_like(cs)

                @pl.when(first)
                def _():
                    cs[...] += jnp.sum(b.astype(F32), axis=0, keepdims=True)

        if na == 1 and nb == 1:
            run(a_refs[0], b_refs[0])
        elif nb == 1:
            per, fn = a_sel
            which = fn(*ids) // per
            for p in range(na):
                pl.when(which == p)(functools.partial(run, a_refs[p], b_refs[0]))
        else:
            assert na == 1
            per, fn = b_sel
            which = fn(*ids) // per
            for p in range(nb):
                pl.when(which == p)(functools.partial(run, a_refs[0], b_refs[p]))

        @pl.when(k == nk - 1)
        def _():
            if sequential:
                store(outs, acc[...], *[e[...] for e in extra], first_step=(ids[0] == 0) & (ids[1] == 0))
            else:
                store(outs, acc[...], *[e[...] for e in extra])
            if colsum_width:
                @pl.when(ids[1] == 0)
                def _():
                    outs[-1][...] = refs[-1][...]

    scratch = [pltpu.VMEM(acc_shape, F32)]
    if colsum_width:
        scratch.append(pltpu.VMEM((1, colsum_width), F32))
    sem = ("parallel", "arbitrary", "arbitrary") if colsum_width else ("parallel", "parallel", "arbitrary")
    if sequential:
        sem = ("arbitrary",) * 3
    return pl.pallas_call(
        body, name=name, grid=grid,
        in_specs=[*a_specs, *b_specs, *extra_specs, *[pl.BlockSpec(memory_space=pl.ANY)] * nd], out_specs=out_specs,
        out_shape=out_shape, scratch_shapes=scratch, compiler_params=_params(sem),
    )(*a_pieces, *b_pieces, *extras, *deps)


def _store_plain(outs, acc):
    outs[0][...] = acc.astype(outs[0].dtype)


def _row_spec(tm, tn):
    return pl.BlockSpec((tm, tn), lambda i, j, k: (i, j))


def _vec_spec(tn):
    return pl.BlockSpec((1, tn), lambda i, j, k: (0, j))


def _mm_act(name, a, w, wkind, *, n_out_cols, k_total, tn, tk, a_cb0=0, a_fn=None, extras=(), extra_specs=(),
            store=_store_plain, out_shape=None, out_specs=None, sequential=False, dep=None):
    a_pieces = list(a) if isinstance(a, (list, tuple)) else [a]
    rows = a_pieces[0].shape[0]
    tm = _tile(rows, ROW_TILE)
    grid = (rows // tm, n_out_cols // tn, k_total // tk)
    a_specs, per = _piece_specs(a_pieces, tm, tk, lambda i, j, k: i, lambda i, j, k: k, a_cb0)
    if wkind == "nat":
        b_spec, dims = pl.BlockSpec((tk, tn), lambda i, j, k: (k, j)), NN
    elif wkind == "stk":
        n = w.shape[2]
        assert tn % n == 0
        b_spec, dims = pl.BlockSpec((tn // n, tk, n), lambda i, j, k: (j, k, 0)), NN
    elif wkind == "natT":
        b_spec, dims = pl.BlockSpec((tn, tk), lambda i, j, k: (j, k)), NT
    else:
        n = w.shape[2]
        assert wkind == "stkT" and tk % n == 0
        b_spec, dims = pl.BlockSpec((tk // n, tn, n), lambda i, j, k: (k, j, 0)), NT
    if out_shape is None:
        out_shape = [jax.ShapeDtypeStruct((rows, n_out_cols), F32)]
        out_specs = [_row_spec(tm, tn)]
    return _mm_call(name, grid, dims, a_pieces, a_specs, (per, lambda i, j, k: k), [w], [b_spec], None,
                    list(extras), list(extra_specs), out_shape, out_specs, (tm, tn), a_fn, store,
                    sequential=sequential, deps=() if dep is None else (dep,))


def _mm_wgrad(name, a, g, *, kw, n, tmw, tn, a_cb0=0, a_fn=None, shard_cols=0, out_dtype=F32, colsum=False, dep=None):
    a_pieces = list(a) if isinstance(a, (list, tuple)) else [a]
    g_pieces = list(g) if isinstance(g, (list, tuple)) else [g]
    rows = a_pieces[0].shape[0]
    tr = _tile(rows, ROW_TILE)
    grid = (n // tn, kw // tmw, rows // tr)
    a_specs, a_per = _piece_specs(a_pieces, tr, tmw, lambda j, i, k: k, lambda j, i, k: i, a_cb0)
    g_specs, g_per = _piece_specs(g_pieces, tr, tn, lambda j, i, k: k, lambda j, i, k: j, 0)
    if shard_cols:
        per = tn // shard_cols
        out_shape = [jax.ShapeDtypeStruct((n // shard_cols, kw, shard_cols), out_dtype)]
        out_specs = [pl.BlockSpec((per, tmw, shard_cols), lambda j, i, k: (j, i, 0))]

        def store(outs, acc):
            for q in range(per):
                outs[0][q] = acc[:, q * shard_cols:(q + 1) * shard_cols].astype(out_dtype)
    else:
        out_shape = [jax.ShapeDtypeStruct((kw, n), out_dtype)]
        out_specs = [pl.BlockSpec((tmw, tn), lambda j, i, k: (i, j))]

        def store(outs, acc):
            outs[0][...] = acc.astype(out_dtype)
    if colsum:
        out_shape.append(jax.ShapeDtypeStruct((1, n), F32))
        out_specs.append(pl.BlockSpec((1, tn), lambda j, i, k: (0, j)))
    res = _mm_call(name, grid, TN, a_pieces, a_specs, (a_per, lambda j, i, k: i), g_pieces, g_specs,
                   (g_per, lambda j, i, k: j), [], [], out_shape, out_specs, (tmw, tn), a_fn, store,
                   colsum_width=tn if colsum else 0, deps=() if dep is None else (dep,))
    return res if colsum else res[0]


def _ln(x, g, b):
    mu = jnp.mean(x, axis=-1, keepdims=True)
    xc = x - mu
    var = jnp.mean(xc * xc, axis=-1, keepdims=True)
    return xc * lax.rsqrt(var + LN_EPS) * g + b


def _relu2(x):
    r = jnp.maximum(x.astype(F32), 0.0)
    return r * r


def _glu(y, gate):
    return y * jax.nn.sigmoid(gate)


def _rowwise(name, fn, ins, n_out, width):
    rows = ins[0][0].shape[0]
    tm = _tile(rows, ROW_TILE)

    def body(*refs):
        res = fn(*[r[...] for r in refs[:len(ins)]])
        for o, v in zip(refs[len(ins):], res):
            o[...] = v

    return pl.pallas_call(
        body, name=name, grid=(rows // tm,),
        in_specs=[pl.BlockSpec((tm, wd), lambda i, cb=cb: (i, cb)) for _, cb, wd in ins],
        out_specs=[pl.BlockSpec((tm, width), lambda i: (i, 0))] * n_out,
        out_shape=[jax.ShapeDtypeStruct((rows, width), F32)] * n_out, compiler_params=_params(("parallel",)),
    )(*[a for a, _, _ in ins])


def _loss_grad(name, r, g, b, target, n_batch, lp, lead):
    rows, d = r.shape
    nq = lp // LANES
    lead_blocks = lead // LANES

    def body(r_ref, g_ref, b_ref, t_ref, gr_ref, gg_ref, gb_ref, loss_ref):
        i = pl.program_id(1)

        @pl.when((pl.program_id(0) == 0) & (i == 0))
        def _():
            loss_ref[...] = jnp.zeros_like(loss_ref)
            gg_ref[...] = jnp.zeros_like(gg_ref)
            gb_ref[...] = jnp.zeros_like(gb_ref)

        h, vjp = jax.vjp(_ln, r_ref[...], g_ref[...], b_ref[...])
        diff = jnp.where(i >= lead_blocks, h - t_ref[...], 0.0)
        gr, gg, gb = vjp(diff * (1.0 / d))
        gr_ref[...] = gr
        gg_ref[...] += gg
        gb_ref[...] += gb
        loss_ref[...] += 0.5 * jnp.sum(diff * diff) * (1.0 / d)

    vec = pl.BlockSpec((1, d), lambda b, i: (0, 0))
    row = pl.BlockSpec((LANES, d), lambda b, i: (b * nq + i, 0))
    return pl.pallas_call(
        body, name=name, grid=(n_batch, nq),
        in_specs=[row, vec, vec, pl.BlockSpec((None, LANES, d), lambda b, i: (b, jnp.maximum(i - lead_blocks, 0), 0))],
        out_specs=[row, vec, vec, pl.BlockSpec((SUBLANES, LANES), lambda b, i: (0, 0))],
        out_shape=[jax.ShapeDtypeStruct((rows, d), F32), jax.ShapeDtypeStruct((1, d), F32),
                   jax.ShapeDtypeStruct((1, d), F32), jax.ShapeDtypeStruct((SUBLANES, LANES), F32)],
        compiler_params=_params(("arbitrary", "arbitrary")),
    )(r, g, b, target)


def _meta_grad(name, g_h0, n_batch, lp, pad, n_meta):
    d = g_h0.shape[1]
    per = lp // n_meta
    at = pad // n_meta

    def body(g_ref, o_ref):
        @pl.when(pl.program_id(0) == 0)
        def _():
            o_ref[...] = jnp.zeros_like(o_ref)

        o_ref[...] += g_ref[...]

    return pl.pallas_call(
        body, name=name, grid=(n_batch,),
        in_specs=[pl.BlockSpec((n_meta, d), lambda b: (b * per + at, 0))],
        out_specs=pl.BlockSpec((n_meta, d), lambda b: (0, 0)),
        out_shape=jax.ShapeDtypeStruct((n_meta, d), F32),
        compiler_params=_params(("arbitrary",)),
    )(g_h0)


def _s5_param_fn(lr, li, ldt, br, bi):
    dt = jnp.exp(ldt)
    e = jnp.exp(lr * dt)
    w = li * dt
    lbr = e * jnp.cos(w)
    lbi = e * jnp.sin(w)
    nr = lbr - 1.0
    den = lr * lr + li * li
    cr = (nr * lr + lbi * li) / den
    ci = (lbi * lr - nr * li) / den
    bbr = cr[:, None, :] * br - ci[:, None, :] * bi
    bbi = cr[:, None, :] * bi + ci[:, None, :] * br
    return lbr, lbi, bbr, bbi


def _s5_params(name, lr, li, ldt, br, bi):
    def body(lr_ref, li_ref, ldt_ref, br_ref, bi_ref, o1, o2, o3, o4):
        res = _s5_param_fn(lr_ref[...], li_ref[...], ldt_ref[...], br_ref[...], bi_ref[...])
        for o, v in zip((o1, o2, o3, o4), res):
            o[...] = v

    shp = [jax.ShapeDtypeStruct(lr.shape, F32)] * 2 + [jax.ShapeDtypeStruct(br.shape, F32)] * 2
    return pl.pallas_call(body, name=name, out_shape=shp)(lr, li, ldt, br, bi)


def _s5_params_bwd(name, lr, li, ldt, br, bi, g_lbr, g_lbi, g_bbr, g_bbi, gd_parts):
    def body(lr_ref, li_ref, ldt_ref, br_ref, bi_ref, g1, g2, g3, g4, gd_ref, o1, o2, o3, o4, o5, o6):
        _, vjp = jax.vjp(_s5_param_fn, lr_ref[...], li_ref[...], ldt_ref[...], br_ref[...], bi_ref[...])
        res = vjp((jnp.sum(g1[...], axis=0), jnp.sum(g2[...], axis=0), g3[...], g4[...]))
        for o, v in zip((o1, o2, o3, o4, o5), res):
            o[...] = v
        o6[...] = jnp.sum(gd_ref[...], axis=0)

    shp = ([jax.ShapeDtypeStruct(lr.shape, F32)] * 2 + [jax.ShapeDtypeStruct(ldt.shape, F32)]
           + [jax.ShapeDtypeStruct(br.shape, F32)] * 2 + [jax.ShapeDtypeStruct(gd_parts.shape[1:], F32)])
    return pl.pallas_call(body, name=name, out_shape=shp)(lr, li, ldt, br, bi, g_lbr, g_lbi, g_bbr, g_bbi, gd_parts)


def _interleave(re, im, w):
    nj = re.shape[-1] // w
    return jnp.concatenate([x[..., j * w:(j + 1) * w] for j in range(nj) for x in (re, im)], axis=-1)


def _deinterleave(x, w):
    nj = x.shape[-1] // (2 * w)
    return (jnp.concatenate([x[..., 2 * j * w:(2 * j + 1) * w] for j in range(nj)], axis=-1),
            jnp.concatenate([x[..., (2 * j + 1) * w:(2 * j + 2) * w] for j in range(nj)], axis=-1))


def _cmul(ar, ai, br, bi):
    return ar * br - ai * bi, ar * bi + ai * br


def _powers(lr, li):
    p = [(lr, li)]
    p.append(_cmul(*p[0], *p[0]))
    p.append(_cmul(*p[1], *p[0]))
    p.append(_cmul(*p[1], *p[1]))
    p.append(_cmul(*p[3], *p[0]))
    p.append(_cmul(*p[3], *p[1]))
    p.append(_cmul(*p[3], *p[2]))
    p.append(_cmul(*p[3], *p[3]))
    return p


def _scan_steps(pw, shifts, keep):
    return [(sh, jnp.where(m, pw[s - 1][0], 0.0), jnp.where(m, pw[s - 1][1], 0.0))
            for s, sh, m in zip((1, 2, 4), shifts, keep)]


def _scan_tile(xr, xi, steps):
    for sh, br, bi in steps:
        rr = pltpu.roll(xr, sh, 0)
        ri = pltpu.roll(xi, sh, 0)
        xr, xi = xr + (br * rr - bi * ri), xi + (br * ri + bi * rr)
    return xr, xi


def _s5_scan(name, bu, lam, n_batch, lp, w):
    rows, two_ns = bu.shape
    nj = two_ns // (2 * w)

    def body(x_ref, lam_ref, s_ref):
        pw = _powers(lam_ref[:, :w], lam_ref[:, w:])
        tab_r = jnp.concatenate([p[0] for p in pw], axis=0)
        tab_i = jnp.concatenate([p[1] for p in pw], axis=0)
        row = lax.broadcasted_iota(jnp.int32, (SUBLANES, w), 0)
        steps = _scan_steps(pw, (1, 2, 4), [row >= s for s in (1, 2, 4)])

        def packed_tile(t, carry):
            cr, ci = carry
            r0 = pl.multiple_of(t * PACKED_ROWS, PACKED_ROWS)
            x = x_ref[pl.ds(r0, PACKED_ROWS), :].astype(F32)
            done = []
            for half in range(PACKED_ROWS // SUBLANES):
                xt = x[half * SUBLANES:(half + 1) * SUBLANES, :]
                xr, xi = _scan_tile(xt[:, :w], xt[:, w:], steps)
                sr = xr + (tab_r * cr - tab_i * ci)
                si = xi + (tab_r * ci + tab_i * cr)
                done.append(jnp.concatenate([sr, si], axis=1))
                cr, ci = sr[SUBLANES - 1:, :], si[SUBLANES - 1:, :]
            s_ref[pl.ds(r0, PACKED_ROWS), :] = jnp.concatenate(done, axis=0).astype(s_ref.dtype)
            return cr, ci

        zero = jnp.zeros((1, w), F32)
        _unrolled_loop(lp // PACKED_ROWS, packed_tile, (zero, zero), SCAN_UNROLL)

    spec = pl.BlockSpec((lp, 2 * w), lambda b, j: (b, j))
    return pl.pallas_call(
        body, name=name, grid=(n_batch, nj), in_specs=[spec, pl.BlockSpec((1, 2 * w), lambda b, j: (0, j))],
        out_specs=spec, out_shape=jax.ShapeDtypeStruct((rows, two_ns), BF16),
        compiler_params=_params(("parallel", "parallel")),
    )(bu, lam)


def _s5_scan_bwd(name, gd, states, lam, n_batch, lp, w):
    rows, two_ns = gd.shape
    nj = two_ns // (2 * w)

    def body(x_ref, s_ref, lam_ref, g_ref, gl_ref):
        pw = _powers(lam_ref[:, :w], -lam_ref[:, w:])
        tab_r = jnp.concatenate([p[0] for p in reversed(pw)], axis=0)
        tab_i = jnp.concatenate([p[1] for p in reversed(pw)], axis=0)
        row = lax.broadcasted_iota(jnp.int32, (SUBLANES, w), 0)
        steps = _scan_steps(pw, [SUBLANES - s for s in (1, 2, 4)], [row < SUBLANES - s for s in (1, 2, 4)])

        n_packed = lp // PACKED_ROWS
        halves = PACKED_ROWS // SUBLANES

        def packed_tile(u, carry):
            cr, ci, ar, ai = carry
            t = n_packed - 1 - u
            r0 = pl.multiple_of(t * PACKED_ROWS, PACKED_ROWS)
            x = x_ref[pl.ds(r0, PACKED_ROWS), :].astype(F32)
            cur = s_ref[pl.ds(r0, PACKED_ROWS), :].astype(F32)
            p0 = pl.multiple_of(jnp.maximum(t - 1, 0) * PACKED_ROWS, PACKED_ROWS)
            before = s_ref[pl.ds(p0, PACKED_ROWS), :].astype(F32)[PACKED_ROWS - 1:, :] * jnp.where(t > 0, 1.0, 0.0)
            done = [None] * halves
            for half in reversed(range(halves)):
                rows_h = slice(half * SUBLANES, (half + 1) * SUBLANES)
                xt, st = x[rows_h, :], cur[rows_h, :]
                xr, xi = _scan_tile(xt[:, :w], xt[:, w:], steps)
                gr = xr + (tab_r * cr - tab_i * ci)
                gi = xi + (tab_r * ci + tab_i * cr)
                done[half] = jnp.concatenate([gr, gi], axis=1)
                prev = before if half == 0 else cur[half * SUBLANES - 1:half * SUBLANES, :]
                spr = jnp.where(row >= 1, pltpu.roll(st[:, :w], 1, 0), prev[:, :w])
                spi = jnp.where(row >= 1, pltpu.roll(st[:, w:], 1, 0), prev[:, w:])
                cr, ci, ar, ai = gr[:1, :], gi[:1, :], ar + gr * spr + gi * spi, ai + gi * spr - gr * spi
            g_ref[pl.ds(r0, PACKED_ROWS), :] = jnp.concatenate(done, axis=0).astype(g_ref.dtype)
            return cr, ci, ar, ai

        z1 = jnp.zeros((1, w), F32)
        z8 = jnp.zeros((SUBLANES, w), F32)
        _, _, ar, ai = _unrolled_loop(n_packed, packed_tile, (z1, z1, z8, z8), SCAN_UNROLL)
        gl_ref[...] = jnp.concatenate([jnp.sum(ar, axis=0, keepdims=True), jnp.sum(ai, axis=0, keepdims=True)], axis=1)

    spec = pl.BlockSpec((lp, 2 * w), lambda b, j: (b, j))
    return pl.pallas_call(
        body, name=name, grid=(n_batch, nj),
        in_specs=[spec, spec, pl.BlockSpec((1, 2 * w), lambda b, j: (0, j))],
        out_specs=[spec, pl.BlockSpec((None, 1, 2 * w), lambda b, j: (b, 0, j))],
        out_shape=[jax.ShapeDtypeStruct((rows, two_ns), BF16), jax.ShapeDtypeStruct((n_batch, 1, two_ns), F32)],
        compiler_params=_params(("parallel", "parallel")),
    )(gd, states, lam)


def _log_sigmoid(z):
    return jnp.minimum(z, 0.0) - jnp.log(1.0 + jnp.exp(-jnp.abs(z)))


ATTN_KEYS = 256
ATTN_GROUP = 4


def _attn_block(i, jb, lp, pad):
    start = jb * ATTN_KEYS
    r0 = pl.multiple_of(jnp.minimum(start, lp - ATTN_KEYS), LANES)
    rowpos = i * LANES + lax.broadcasted_iota(jnp.int32, (LANES, ATTN_KEYS), 0)
    keypos = r0 + lax.broadcasted_iota(jnp.int32, (LANES, ATTN_KEYS), 1)
    return r0, (keypos < rowpos) & (keypos >= jnp.maximum(start, pad))


def _tri_ones(strict_upper):
    r = lax.broadcasted_iota(jnp.int32, (ATTN_KEYS, ATTN_KEYS + LANES), 0)
    c = lax.broadcasted_iota(jnp.int32, (ATTN_KEYS, ATTN_KEYS + LANES), 1)
    tri = (r > c) if strict_upper else (r < c)
    return jnp.where((c >= ATTN_KEYS) | tri, 1.0, 0.0).astype(BF16)


def _split_sums(cr):
    rs = cr[:, ATTN_KEYS:]
    return cr[:, :ATTN_KEYS], jnp.concatenate([rs] * (ATTN_KEYS // LANES), axis=1)


def _head_masks():
    lane = lax.broadcasted_iota(jnp.int32, (1, LANES), 1)
    return [lane < SB_HEAD_DIM, lane >= SB_HEAD_DIM]


def _run_groups(n, first, sign, make):
    j, left, g = first, n, ATTN_GROUP
    while g >= 1:
        shift = g.bit_length() - 1
        count = lax.shift_right_logical(left, shift)
        fn = make(g)

        def loop(_, jcur, fn=fn, g=g):
            fn(jcur)
            return jcur + sign * g

        j = lax.fori_loop(0, count, loop, j)
        left = left - lax.shift_left(count, shift)
        g //= 2


def _attn_fwd(name, proj, n_batch, lp, pad, q_cb, k_cb, v_cb, n_pairs):
    rows = proj.shape[0]
    nq = lp // LANES
    scale = SB_HEAD_DIM ** -0.5

    def body(q_ref, k_ref, v_ref, o_ref, acc_s):
        i = pl.program_id(1)
        hm = _head_masks()
        comb = _tri_ones(True)
        n_blocks = lax.shift_right_logical(i + ATTN_KEYS // LANES, (ATTN_KEYS // LANES).bit_length() - 1)

        def pair(hp, carry):
            lanes = pl.ds(pl.multiple_of(hp * LANES, LANES), LANES)
            qs = q_ref[:, lanes] * scale
            qh = [jnp.where(m, qs, 0.0).astype(BF16) for m in hm]
            acc_s[...] = jnp.zeros_like(acc_s)
            o_ref[:, lanes] = jnp.zeros((LANES, LANES), F32)

            def make(group):
                def fn(jtop):
                    chains = []
                    for g in range(group):
                        r0, vis = _attn_block(i, jtop - g, lp, pad)
                        kj = k_ref[pl.ds(r0, ATTN_KEYS), lanes].astype(BF16)
                        vj = v_ref[pl.ds(r0, ATTN_KEYS), lanes]
                        for h in range(2):
                            z = lax.dot_general(qh[h], kj, NT, preferred_element_type=F32)
                            chains.append((h, vis, z, jnp.where(hm[h], vj, 0.0).astype(BF16)))
                    staged = []
                    for h, vis, z, vh in chains:
                        lsz = _log_sigmoid(z)
                        staged.append((h, vis, lsz, _running_sums(jnp.where(vis, lsz - z, 0.0), comb, split=True), vh))
                    out = o_ref[:, lanes]
                    for h, vis, lsz, cr, vh in staged:
                        later, rs = _split_sums(cr)
                        acc = acc_s[h]
                        wgt = jnp.where(vis, jnp.exp(lsz + later + acc), 0.0)
                        acc_s[h] = acc + rs
                        out = out + lax.dot_general(wgt.astype(BF16), vh, NN, preferred_element_type=F32)
                    o_ref[:, lanes] = out
                return fn

            _run_groups(n_blocks, n_blocks - 1, -1, make)
            return carry

        lax.fori_loop(0, n_pairs, pair, 0)

    wide = n_pairs * LANES
    assert q_cb % n_pairs == 0 and k_cb % n_pairs == 0 and v_cb % n_pairs == 0
    return pl.pallas_call(
        body, name=name, grid=(n_batch, nq),
        in_specs=[pl.BlockSpec((LANES, wide), lambda b, i: (b * nq + i, q_cb // n_pairs)),
                  pl.BlockSpec((lp, wide), lambda b, i: (b, k_cb // n_pairs)),
                  pl.BlockSpec((lp, wide), lambda b, i: (b, v_cb // n_pairs))],
        out_specs=pl.BlockSpec((LANES, wide), lambda b, i: (b * nq + i, 0)),
        out_shape=jax.ShapeDtypeStruct((rows, wide), F32),
        scratch_shapes=[pltpu.VMEM((2, LANES, ATTN_KEYS), F32)],
        compiler_params=_params(("parallel", "arbitrary")),
    )(proj, proj, proj)


def _attn_bwd(name, proj, g_out, n_batch, lp, pad, q_cb, k_cb, v_cb, go_cb, n_pairs):
    rows = proj.shape[0]
    nq = lp // LANES
    scale = SB_HEAD_DIM ** -0.5

    def body(q_ref, k_ref, v_ref, go_ref, gq_ref, gk_ref, gv_ref, ga_s, sz_s, acc_s):
        i = pl.program_id(1)

        @pl.when(i == 0)
        def _():
            gk_ref[...] = jnp.zeros_like(gk_ref)
            gv_ref[...] = jnp.zeros_like(gv_ref)

        hm = _head_masks()
        comb_up = _tri_ones(True)
        comb_lo = _tri_ones(False)
        n_blocks = lax.shift_right_logical(i + ATTN_KEYS // LANES, (ATTN_KEYS // LANES).bit_length() - 1)

        def pair(hp, carry):
            lanes = pl.ds(pl.multiple_of(hp * LANES, LANES), LANES)
            qs = q_ref[:, lanes] * scale
            go = go_ref[:, lanes]
            qh = [jnp.where(m, qs, 0.0).astype(BF16) for m in hm]
            goh = [jnp.where(m, go, 0.0).astype(BF16) for m in hm]
            acc_s[...] = jnp.zeros_like(acc_s)

            def make_down(group):
                def fn(jtop):
                    chains = []
                    for g in range(group):
                        j = jtop - g
                        r0, vis = _attn_block(i, j, lp, pad)
                        kj = k_ref[pl.ds(r0, ATTN_KEYS), lanes].astype(BF16)
                        vj = v_ref[pl.ds(r0, ATTN_KEYS), lanes].astype(BF16)
                        for h in range(2):
                            z = lax.dot_general(qh[h], kj, NT, preferred_element_type=F32)
                            gw = lax.dot_general(goh[h], vj, NT, preferred_element_type=F32)
                            chains.append((h, j, r0, vis, z, gw))
                    staged = []
                    for h, j, r0, vis, z, gw in chains:
                        lsz = _log_sigmoid(z)
                        staged.append((h, j, r0, vis, lsz, _running_sums(jnp.where(vis, lsz - z, 0.0), comb_up), gw))
                    for h, j, r0, vis, lsz, cr, gw in staged:
                        later, rs = _split_sums(cr)
                        acc = acc_s[h]
                        wgt = jnp.where(vis, jnp.exp(lsz + later + acc), 0.0)
                        acc_s[h] = acc + rs
                        ga_s[h, j] = gw * wgt
                        sz_s[h, j] = jnp.exp(lsz)
                        gv_ref[pl.ds(r0, ATTN_KEYS), lanes] += lax.dot_general(
                            wgt.astype(BF16), goh[h], TN, preferred_element_type=F32)
                return fn

            _run_groups(n_blocks, n_blocks - 1, -1, make_down)
            acc_s[...] = jnp.zeros_like(acc_s)

            def make_up(group):
                def fn(jbot):
                    pend = []
                    for g in range(group):
                        j = jbot + g
                        r0, vis = _attn_block(i, j, lp, pad)
                        kj = k_ref[pl.ds(r0, ATTN_KEYS), lanes]
                        for h in range(2):
                            ga = ga_s[h, j]
                            pend.append((h, j, r0, vis, ga, _running_sums(ga, comb_lo),
                                         jnp.where(hm[h], kj, 0.0).astype(BF16)))
                    gq = jnp.zeros((LANES, LANES), F32)
                    for h, j, r0, vis, ga, cr, kh in pend:
                        before, rs = _split_sums(cr)
                        pre = acc_s[h]
                        glk = before + pre
                        acc_s[h] = pre + rs
                        sz = sz_s[h, j]
                        gz = jnp.where(vis, ga * (1.0 - sz) - glk * sz, 0.0).astype(BF16)
                        gq = gq + lax.dot_general(gz, kh, NN, preferred_element_type=F32)
                        gk_ref[pl.ds(r0, ATTN_KEYS), lanes] += lax.dot_general(gz, qh[h], TN, preferred_element_type=F32)
                    gq_ref[:, lanes] += gq * scale
                return fn

            gq_ref[:, lanes] = jnp.zeros((LANES, LANES), F32)
            _run_groups(n_blocks, 0, 1, make_up)
            return carry

        lax.fori_loop(0, n_pairs, pair, 0)

    wide = n_pairs * LANES
    assert q_cb % n_pairs == 0 and k_cb % n_pairs == 0 and v_cb % n_pairs == 0 and go_cb % n_pairs == 0
    blk = lambda cb: pl.BlockSpec((LANES, wide), lambda b, i: (b * nq + i, cb // n_pairs))
    full = lambda cb: pl.BlockSpec((lp, wide), lambda b, i: (b, cb // n_pairs))
    shp = jax.ShapeDtypeStruct((rows, wide), F32)
    per_block = pltpu.VMEM((2, -(-lp // ATTN_KEYS), LANES, ATTN_KEYS), F32)
    return pl.pallas_call(
        body, name=name, grid=(n_batch, nq),
        in_specs=[blk(q_cb), full(k_cb), full(v_cb), blk(go_cb)],
        out_specs=[blk(0), full(0), full(0)], out_shape=[shp, shp, shp],
        scratch_shapes=[per_block, per_block, pltpu.VMEM((2, LANES, ATTN_KEYS), F32)],
        compiler_params=_params(("parallel", "arbitrary")),
    )(proj, proj, proj, g_out)


def _lb_fn(gamma):
    g0, g1 = gamma[0:1, :], gamma[1:2, :]
    mx = jnp.maximum(g0, g1)
    e0, e1 = jnp.exp(g0 - mx), jnp.exp(g1 - mx)
    p0, p1 = e0 / (e0 + e1), e1 / (e0 + e1)
    return (p0 + p1) - p0


def _lower_bound(name, gamma):
    def body(g_ref, o_ref):
        o_ref[...] = _lb_fn(g_ref[...])

    return pl.pallas_call(body, name=name, out_shape=jax.ShapeDtypeStruct((1, gamma.shape[1]), F32))(gamma)


def _lower_bound_bwd(name, gamma, g_lb_parts, g_ng_parts):
    def body(g_ref, glb_ref, gng_ref, o_ref, o2_ref):
        _, vjp = jax.vjp(_lb_fn, g_ref[...])
        o_ref[...] = vjp(jnp.sum(glb_ref[...], axis=0))[0]
        o2_ref[...] = jnp.sum(gng_ref[...], axis=0)

    return pl.pallas_call(
        body, name=name,
        out_shape=[jax.ShapeDtypeStruct(gamma.shape, F32), jax.ShapeDtypeStruct((1, gamma.shape[1]), F32)],
    )(gamma, g_lb_parts, g_ng_parts)


def _tri_times(tril, x, dims):
    hi = x.astype(BF16)
    lo = (x - hi.astype(F32)).astype(BF16)
    t = tril.astype(BF16)
    return (lax.dot_general(t, hi, dims, preferred_element_type=F32)
            + lax.dot_general(t, lo, dims, preferred_element_type=F32))


@jax.custom_vjp
def _cumsum_rows(x, tril):
    return _tri_times(tril, x, NN)


def _cumsum_rows_fwd(x, tril):
    return _tri_times(tril, x, NN), tril


def _cumsum_rows_bwd(tril, g):
    return _tri_times(tril, g, TN), jnp.zeros_like(tril)


_cumsum_rows.defvjp(_cumsum_rows_fwd, _cumsum_rows_bwd)


def _hg_decays(f_pre, lbs, masks, tril):
    f = [[lb + (1.0 - lb) * jax.nn.sigmoid(fc) for fc, lb in zip(row, lbs)] for row in f_pre]
    bcum = [[_cumsum_rows(jnp.log(x) * m, tril) for x in row] for row, m in zip(f, masks)]
    return [[1.0 - x for x in row] for row in f], bcum


def _hg_step(q, f_pre, i_in, g, lbs, ngs, sts, masks, tril):
    k, bcum = _hg_decays(f_pre, lbs, masks, tril)
    v = [[ic * m for ic in row] for row, m in zip(i_in, masks)]
    qd = [[qc * jnp.exp(b) for qc, b in zip(qr, br)] for qr, br in zip(q, bcum)]
    scores = [[jnp.where(tril > 0.5, _dot(a, kk * jnp.exp(-b), NT), 0.0) for a, kk, b in zip(ar, kr, br)]
              for ar, kr, br in zip(qd, k, bcum)]
    inner = [[_dot(s, x, NN) for s, x in zip(sr, vr)] for sr, vr in zip(scores, v)]
    add = [[_dot(x, kk * jnp.exp(b[HG_CHUNK - 1:, :] - b), TN) for x, kk, b in zip(vr, kr, br)]
           for vr, kr, br in zip(v, k, bcum)]
    outs = []
    for qr, br, nr, ar, gr in zip(qd, bcum, inner, add, g):
        o = [n + _dot(a, st, NT) for n, a, st in zip(nr, qr, sts)]
        sts = [jnp.exp(b[HG_CHUNK - 1:, :]) * st + a for b, a, st in zip(br, ar, sts)]
        o = [x * lax.rsqrt(jnp.mean(x * x, axis=-1, keepdims=True) + RMS_EPS) * ng for x, ng in zip(o, ngs)]
        outs.append([x * (gc * jax.nn.sigmoid(gc)) for x, gc in zip(o, gr)])
    return outs, sts


def _hg_consts(c, pad):
    r = lax.broadcasted_iota(jnp.int32, (HG_CHUNK, HG_CHUNK), 0)
    cc = lax.broadcasted_iota(jnp.int32, (HG_CHUNK, HG_CHUNK), 1)
    tril = jnp.where(r >= cc, 1.0, 0.0).astype(F32)
    pos = c * HG_CHUNK + lax.broadcasted_iota(jnp.int32, (HG_CHUNK, 1), 0)
    return tril, jnp.where(pos >= pad, 1.0, 0.0).astype(F32)


HG_HEADS_PER_STEP = 8
HG_CHUNKS_PER_STEP = 2


def _hg_layout(lp, n_heads):
    step_rows = HG_CHUNKS_PER_STEP * HG_CHUNK
    per = min(HG_HEADS_PER_STEP, n_heads)
    assert lp % step_rows == 0 and n_heads % per == 0
    heads = [(h, slice(h * HG_DK, (h + 1) * HG_DK)) for h in range(per)]
    return n_heads // per, lp // step_rows, step_rows, per * HG_DK, heads


def _hg_step_views(step, pad, heads):
    slices = [slice(u * HG_CHUNK, (u + 1) * HG_CHUNK) for u in range(HG_CHUNKS_PER_STEP)]
    consts = [_hg_consts(step * HG_CHUNKS_PER_STEP + u, pad) for u in range(HG_CHUNKS_PER_STEP)]
    load = lambda ref: [[ref[sl, cols] for _, cols in heads] for sl in slices]
    return slices, [m for _, m in consts], consts[0][0], load


def _hgrn_fwd(name, proj, lb, ng, n_batch, lp, pad, n_heads):
    rows = proj.shape[0]
    groups, steps, step_rows, wide, heads = _hg_layout(lp, n_heads)

    def body(q_ref, f_ref, i_ref, g_ref, lb_ref, ng_ref, o_ref, s_ref, st_s):
        t = pl.program_id(2)

        @pl.when(t == 0)
        def _():
            st_s[...] = jnp.zeros_like(st_s)

        slices, masks, tril, load = _hg_step_views(t, pad, heads)
        sts = [st_s[h] for h, _ in heads]
        for (_, cols), st in zip(heads, sts):
            s_ref[:, cols] = st
        outs, sts = _hg_step(load(q_ref), load(f_ref), load(i_ref), load(g_ref),
                             [lb_ref[:, cols] for _, cols in heads], [ng_ref[:, cols] for _, cols in heads],
                             sts, masks, tril)
        for sl, row in zip(slices, outs):
            for (_, cols), o in zip(heads, row):
                o_ref[sl, cols] = o
        for (h, _), st in zip(heads, sts):
            st_s[h] = st

    col = lambda off: pl.BlockSpec((step_rows, wide), lambda b, h, t: (b * steps + t, off * groups + h))
    vec = pl.BlockSpec((1, wide), lambda b, h, t: (0, h))
    return pl.pallas_call(
        body, name=name, grid=(n_batch, groups, steps), in_specs=[col(0), col(1), col(2), col(3), vec, vec],
        out_specs=[col(0), pl.BlockSpec((HG_DK, wide), lambda b, h, t: (b * steps + t, h))],
        out_shape=[jax.ShapeDtypeStruct((rows, n_heads * HG_DK), F32),
                   jax.ShapeDtypeStruct((n_batch * steps * HG_DK, n_heads * HG_DK), F32)],
        scratch_shapes=[pltpu.VMEM((len(heads), HG_DK, HG_DK), F32)],
        compiler_params=_params(("parallel", "parallel", "arbitrary")),
    )(proj, proj, proj, proj, lb, ng)


def _hgrn_bwd(name, proj, lb, ng, g_out, states, n_batch, lp, pad, n_heads):
    rows = proj.shape[0]
    width = n_heads * HG_DK
    groups, steps, step_rows, wide, heads = _hg_layout(lp, n_heads)
    assert groups == 1

    def body(q_ref, f_ref, i_ref, g_ref, lb_ref, ng_ref, go_ref, s_ref, gp_ref, glb_ref, gng_ref, gst_s):
        t = pl.program_id(2)

        @pl.when(t == 0)
        def _():
            gst_s[...] = jnp.zeros_like(gst_s)
            glb_ref[...] = jnp.zeros_like(glb_ref)
            gng_ref[...] = jnp.zeros_like(gng_ref)

        slices, masks, tril, load = _hg_step_views(steps - 1 - t, pad, heads)
        fn = functools.partial(_hg_step, masks=masks, tril=tril)
        _, vjp = jax.vjp(fn, load(q_ref), load(f_ref), load(i_ref), load(g_ref),
                         [lb_ref[:, cols] for _, cols in heads], [ng_ref[:, cols] for _, cols in heads],
                         [s_ref[:, cols] for _, cols in heads])
        gq, gf, gi, gg, glb, gng, gst = vjp((load(go_ref), [gst_s[h] for h, _ in heads]))
        for part, grads in enumerate((gq, gf, gi, gg)):
            for sl, row in zip(slices, grads):
                for (h, _), x in zip(heads, row):
                    lane0 = part * width + h * HG_DK
                    gp_ref[sl, lane0:lane0 + HG_DK] = x.astype(BF16)
        for (h, cols), a, b, c in zip(heads, gst, glb, gng):
            gst_s[h] = a
            glb_ref[:, cols] += b
            gng_ref[:, cols] += c

    col = lambda off: pl.BlockSpec((step_rows, wide), lambda b, h, t: (b * steps + steps - 1 - t, off * groups + h))
    vec = pl.BlockSpec((1, wide), lambda b, h, t: (0, h))
    part = pl.BlockSpec((None, 1, wide), lambda b, h, t: (b, 0, h))
    big = jax.ShapeDtypeStruct((rows, 4 * width), BF16)
    small = jax.ShapeDtypeStruct((n_batch, 1, width), F32)
    return pl.pallas_call(
        body, name=name, grid=(n_batch, groups, steps),
        in_specs=[col(0), col(1), col(2), col(3), vec, vec, col(0),
                  pl.BlockSpec((HG_DK, wide), lambda b, h, t: (b * steps + steps - 1 - t, h))],
        out_specs=[pl.BlockSpec((step_rows, 4 * width), lambda b, h, t: (b * steps + steps - 1 - t, 0)), part, part],
        out_shape=[big, small, small],
        scratch_shapes=[pltpu.VMEM((len(heads), HG_DK, HG_DK), F32)],
        compiler_params=_params(("parallel", "parallel", "arbitrary")),
    )(proj, proj, proj, proj, lb, ng, g_out, states)


def _exchange_copies(src, dst, send, recv, loc, scatter):
    x, y, c = lax.axis_index("x"), lax.axis_index("y"), lax.axis_index("c")
    me = 4 * x + 2 * y + c
    local, remote = [], []
    for w in range(len(src)):
        local.append(pltpu.make_async_copy(src[w].at[me] if scatter else src[w], dst[w].at[me], loc.at[w]))
    for k in range(1, N_DEV):
        px = 1 - x if k & 4 else x
        py = 1 - y if k & 2 else y
        pc = 1 - c if k & 1 else c
        peer = 4 * px + 2 * py + pc
        for w in range(len(src)):
            remote.append(pltpu.make_async_remote_copy(
                src_ref=src[w].at[peer] if scatter else src[w], dst_ref=dst[w].at[me],
                send_sem=send.at[w * (N_DEV - 1) + k - 1], recv_sem=recv.at[w * (N_DEV - 1) + k - 1],
                device_id=(px, py, pc), device_id_type=pl.DeviceIdType.MESH))
    return local, remote


_HBM_SPEC = pl.BlockSpec(memory_space=pltpu.HBM)
_SEM_SPEC = pl.BlockSpec(memory_space=pltpu.SEMAPHORE)
_ANY_SPEC = pl.BlockSpec(memory_space=pl.ANY)
_DATAFLOW = pltpu.SideEffectType.DATAFLOW_SIDE_EFFECTING


def _exchange_start(name, srcs, scatter, dep=None):
    nw = len(srcs)
    srcs = [pltpu.with_memory_space_constraint(s, pltpu.HBM) for s in srcs]
    lands = [pltpu.with_memory_space_constraint(lax.empty(s.shape if scatter else (N_DEV,) + s.shape, s.dtype), pltpu.HBM)
             for s in srcs]
    deps = [] if dep is None else [dep]

    def body(*refs):
        src, dst = refs[:nw], refs[nw:2 * nw]
        send, recv, loc = refs[2 * nw + len(deps):2 * nw + len(deps) + 3]
        token = refs[-1]
        local, remote = _exchange_copies(src, dst, send, recv, loc, scatter)
        for cp in local + remote:
            cp.start()
        token[...] = jnp.zeros_like(token)

    sems = [pltpu.SemaphoreType.DMA((nw * (N_DEV - 1),)), pltpu.SemaphoreType.DMA((nw * (N_DEV - 1),)),
            pltpu.SemaphoreType.DMA((nw,))]
    out = pl.pallas_call(
        body, name=name,
        out_shape=(*sems, *[pltpu.HBM(s.shape, s.dtype) for s in srcs], *[pltpu.HBM(s.shape, s.dtype) for s in lands],
                   jax.ShapeDtypeStruct((SUBLANES, LANES), F32)),
        in_specs=[_HBM_SPEC] * (2 * nw) + [_ANY_SPEC] * len(deps),
        out_specs=(_SEM_SPEC, _SEM_SPEC, _SEM_SPEC, *[_HBM_SPEC] * (2 * nw), pl.BlockSpec(memory_space=pltpu.VMEM)),
        input_output_aliases={i: 3 + i for i in range(2 * nw)},
        compiler_params=pltpu.CompilerParams(has_side_effects=_DATAFLOW),
    )(*srcs, *lands, *deps)
    return {"sems": out[:3], "srcs": out[3:3 + nw], "lands": out[3 + nw:3 + 2 * nw], "token": out[-1], "scatter": scatter}


def _exchange_wait(name, handle, after):
    nw = len(handle["srcs"])
    scatter = handle["scatter"]

    def body(*refs):
        src, dst = refs[:nw], refs[nw:2 * nw]
        send, recv, loc = refs[2 * nw:2 * nw + 3]
        local, remote = _exchange_copies(src, dst, send, recv, loc, scatter)
        for cp in local:
            cp.wait()
        for cp in remote:
            cp.wait_send()
            cp.wait_recv()

    out = pl.pallas_call(
        body, name=name,
        out_shape=(*[pltpu.HBM(s.shape, s.dtype) for s in handle["srcs"]],
                   *[pltpu.HBM(s.shape, s.dtype) for s in handle["lands"]]),
        in_specs=[_HBM_SPEC] * (2 * nw) + [_SEM_SPEC] * 3 + [_ANY_SPEC],
        out_specs=tuple([_HBM_SPEC] * (2 * nw)),
        input_output_aliases={i: i for i in range(2 * nw)},
        compiler_params=pltpu.CompilerParams(has_side_effects=_DATAFLOW),
    )(*handle["srcs"], *handle["lands"], *handle["sems"], after)
    return list(out[nw:])


def _adamw(w, g, m, v):
    m = ADAM_B1 * m + (1.0 - ADAM_B1) * g
    v = ADAM_B2 * v + (1.0 - ADAM_B2) * (g * g)
    m_hat = m / (1.0 - ADAM_B1 ** ADAM_STEP)
    v_hat = v / (1.0 - ADAM_B2 ** ADAM_STEP)
    delta = -ADAM_LR * (m_hat / (jnp.sqrt(v_hat) + ADAM_EPS) + ADAM_WD * w)
    return delta, m, v


def _adamw_summed(name, parts, w, m, v):
    layered = w.ndim == 3
    parts = list(parts) if layered else [parts]
    n_layers = len(parts)
    rows, cols = w.shape[-2:]
    n_parts = parts[0].shape[0]
    tr = _tile(rows, max(SUBLANES, (1 << 18) // cols))

    def body(*refs):
        p_refs = refs[:n_layers]
        w_ref, m_ref, v_ref, g_ref, d_ref, nm_ref, nv_ref = refs[n_layers:]
        layer = pl.program_id(0)

        def run(p_ref):
            g = p_ref[0].astype(F32)
            for s in range(1, n_parts):
                g = g + p_ref[s].astype(F32)
            d, nm, nv = _adamw(w_ref[...], g, m_ref[...], v_ref[...])
            g_ref[...] = g
            d_ref[...] = d
            nm_ref[...] = nm
            nv_ref[...] = nv

        for l in range(n_layers):
            pl.when(layer == l)(functools.partial(run, p_refs[l]))

    if layered:
        spec = pl.BlockSpec((None, tr, cols), lambda l, i: (l, i, 0))
    else:
        spec = pl.BlockSpec((tr, cols), lambda l, i: (i, 0))
    p_specs = [pl.BlockSpec((n_parts, tr, cols), lambda l, i, q=q: (0, jnp.where(l == q, i, 0), 0))
               for q in range(n_layers)]
    shp = jax.ShapeDtypeStruct(w.shape, F32)
    return pl.pallas_call(
        body, name=name, grid=(n_layers, rows // tr), in_specs=[*p_specs, spec, spec, spec],
        out_specs=[spec] * 4, out_shape=[shp] * 4, compiler_params=_params(("parallel", "parallel")),
    )(*parts, w, m, v)


def _pack_rows(arrays, cols):
    out = []
    for a in arrays:
        flat = a.reshape(-1)
        n = -(-flat.shape[0] // cols) * cols
        out.append(jnp.pad(flat, (0, n - flat.shape[0])).reshape(-1, cols))
    packed = jnp.concatenate(out, axis=0)
    return jnp.pad(packed, ((0, -packed.shape[0] % SUBLANES), (0, 0)))


def _unpack_rows(packed, shapes, cols):
    out, r = [], 0
    for s in shapes:
        n = math.prod(s)
        nr = -(-n // cols)
        out.append(packed[r:r + nr].reshape(-1)[:n].reshape(s))
        r += nr
    return out


def _block_diag(blocks):
    g, a, b = blocks.shape
    eye = jnp.eye(g, dtype=blocks.dtype)
    return (eye[:, None, :, None] * blocks[:, :, None, :]).reshape(g * a, g * b)


def _diag_blocks(dense, g):
    a, b = dense.shape[0] // g, dense.shape[1] // g
    return jnp.einsum("gagb->gab", dense.reshape(g, a, g, b))


def _local_step(x, target, meta, wts, small, late_weights, on_grads, on_small):
    n_batch, seq, d = x.shape
    n_meta = meta.shape[0]
    pad = -(seq + n_meta) % LANES
    lead = pad + n_meta
    lp = lead + seq
    rows = n_batch * lp
    s5w = wts["glu"].shape[0]
    n_ab = wts["in_ab"].shape[2]
    ab_cols = wts["in_ab"].shape[0] * n_ab
    sbw = (ab_cols - s5w) // 3
    dff = small["mlp_b_up"].shape[1]
    n_pairs = sbw // LANES
    n_hg = d // HG_DK
    s5_cb = s5w // LANES
    sb_cb = sbw // LANES
    tm = _tile(rows, ROW_TILE)
    groups, n_state, grp = small["s5_b_re"].shape[1:]
    ns = groups * n_state
    sw = min(SCAN_LANES, ns)

    h0 = jnp.concatenate(
        [jnp.zeros((n_batch, pad, d), F32), jnp.broadcast_to(meta[None], (n_batch, n_meta, d)), x], axis=1
    ).reshape(rows, d)

    lam_re, lam_im = small["s5_lam_re"][0], small["s5_lam_im"][0]
    log_dt = small["s5_log_dt"][0][:, None]
    b_re_t = small["s5_b_re"][0].transpose(0, 2, 1)
    b_im_t = small["s5_b_im"][0].transpose(0, 2, 1)
    c_re, c_im = small["s5_c_re"][0], small["s5_c_im"][0]
    lbr, lbi, bbr, bbi = _s5_params("s5_params", lam_re, lam_im, log_dt, b_re_t, b_im_t)
    b_blk = _interleave(_block_diag(bbr), _block_diag(bbi), sw).astype(BF16)
    c_blk = _interleave(_block_diag(c_re), _block_diag(-c_im), sw).T.astype(BF16)
    lam_row = _interleave(lbr.reshape(1, ns), lbi.reshape(1, ns), sw)
    d_row = small["s5_d"].reshape(1, s5w)

    def ln_store(outs, acc, res, bias, g, b):
        r = ALPHA * res + acc + bias
        outs[0][...] = r
        if len(outs) > 1:
            outs[1][...] = _ln(r, g, b)

    zero_bias = jnp.zeros((1, d), F32)

    def mix_ln(name, a, w, k_total, tk, res, bias, g, b, a_fn=None, emit_h=True):
        n_out = 2 if emit_h else 1
        return _mm_act(name, a, w, "nat", n_out_cols=d, k_total=k_total, tn=d, tk=tk, a_fn=a_fn,
                       extras=(res, bias, g, b), extra_specs=(_row_spec(tm, d), _vec_spec(d), _vec_spec(d), _vec_spec(d)),
                       store=ln_store, out_shape=[jax.ShapeDtypeStruct((rows, d), F32)] * n_out,
                       out_specs=[_row_spec(tm, d)] * n_out)

    def two(width):
        return [jax.ShapeDtypeStruct((rows, width), F32)] * 2, [_row_spec(tm, width)] * 2

    def shard_tile(total, shard, cap=1024):
        t = max(shard, cap - cap % shard)
        while total % t:
            t -= shard
        return t

    proj_ab = _mm_act("in_ab", h0, wts["in_ab"], "stk", n_out_cols=ab_cols, k_total=d, tn=shard_tile(ab_cols, n_ab), tk=d)[0]
    bu = _mm_act("s5_bu", proj_ab, b_blk, "nat", n_out_cols=2 * ns, k_total=s5w, tn=min(2 * ns, 2048), tk=s5w)[0]
    states = _s5_scan("s5_scan", bu, lam_row, n_batch, lp, sw)

    def gelu_store(outs, acc, u, dv):
        ypre = acc + dv * u
        outs[0][...] = ypre
        outs[1][...] = jax.nn.gelu(ypre)

    shp2, spec2 = two(s5w)
    ypre, y = _mm_act(
        "s5_y", states, c_blk, "nat", n_out_cols=s5w, k_total=2 * ns, tn=s5w, tk=min(2 * ns, 1024),
        extras=(proj_ab, d_row), extra_specs=(_row_spec(tm, s5w), _vec_spec(s5w)), store=gelu_store,
        out_shape=shp2, out_specs=spec2)

    def glu_store(outs, acc, yv, bias):
        gate = acc + bias
        outs[0][...] = gate
        outs[1][...] = _glu(yv, gate)

    gate, a_out = _mm_act(
        "s5_glu", y, wts["glu"], "nat", n_out_cols=s5w, k_total=s5w, tn=s5w, tk=s5w,
        extras=(y, small["s5_b_glu"]), extra_specs=(_row_spec(tm, s5w), _vec_spec(s5w)), store=glu_store,
        out_shape=shp2, out_specs=spec2)
    b_out = _attn_fwd("sb_attn", proj_ab, n_batch, lp, pad, s5_cb, s5_cb + sb_cb, s5_cb + 2 * sb_cb, n_pairs)

    def bias_store(outs, acc, bias):
        outs[0][...] = (acc + bias).astype(outs[0].dtype)

    def wide(width, dtype):
        return [jax.ShapeDtypeStruct((rows, dff), dtype)], [_row_spec(tm, width)]

    def mlp_fwd(layer, h_in, emit_h=True):
        tn = shard_tile(dff, n_up)
        shp, spec = wide(tn, BF16)
        up = _mm_act(f"up{layer}", h_in, wts["up"][layer], "stk", n_out_cols=dff, k_total=d, tn=tn, tk=d,
                     extras=(small["mlp_b_up"][layer:layer + 1],), extra_specs=(_vec_spec(tn),), store=bias_store,
                     out_shape=shp, out_specs=spec)[0]
        return (up, *mix_ln(f"down{layer}", up, wts["down"][layer], dff, min(dff, 1024), h_in,
                            small["mlp_b_down"][layer:layer + 1], small["ln_mlp_g"][layer:layer + 1],
                            small["ln_mlp_b"][layer:layer + 1], a_fn=_relu2, emit_h=emit_h))

    r1, h1 = mix_ln("out_ab", [a_out, b_out], wts["out_ab"], s5w + sbw, min(s5w, sbw), h0, zero_bias,
                    small["ln_mix_g"][0:1], small["ln_mix_b"][0:1])
    wts = {**wts, **late_weights(r1)}
    n_c = wts["in_c"].shape[2]
    n_up = wts["up"][0].shape[2]
    up0, r2, h2 = mlp_fwd(0, h1)

    lb = _lower_bound("hg_lb", small["hgrn_gamma"])
    proj_c = _mm_act("in_c", h2, wts["in_c"], "stk", n_out_cols=4 * d, k_total=d, tn=shard_tile(4 * d, n_c), tk=d)[0]
    c_out, hg_states = _hgrn_fwd("hgrn", proj_c, lb, wts["ng"], n_batch, lp, pad, n_hg)
    r3, h3 = mix_ln("out_c", c_out, wts["out_c"], d, d, h2, zero_bias, small["ln_mix_g"][1:2], small["ln_mix_b"][1:2])
    up1, r4 = mlp_fwd(1, h3, emit_h=False)

    gr = {}
    g_r4, gr["ln_mlp_g1"], gr["ln_mlp_b1"], loss_tile = _loss_grad(
        "loss", r4, small["ln_mlp_g"][1:2], small["ln_mlp_b"][1:2], target, n_batch, lp, lead)

    def res_store(outs, acc, g_res):
        outs[0][...] = acc + ALPHA * g_res

    def ln_bwd_store(outs, acc, g_res, r_in, g, b, first_step):
        gr_in, gg, gb = jax.vjp(_ln, r_in, g, b)[1](acc + ALPHA * g_res)
        outs[0][...] = gr_in

        @pl.when(first_step)
        def _():
            outs[1][...] = jnp.zeros_like(outs[1])
            outs[2][...] = jnp.zeros_like(outs[2])

        outs[1][...] += gg
        outs[2][...] += gb

    def through_ln(name, a, w, k_total, tk, g_res, r_in, g, b, dep=None):
        vec = pl.BlockSpec((1, d), lambda i, j, k: (0, 0))
        return _mm_act(name, a, w, "stkT", n_out_cols=d, k_total=k_total, tn=d, tk=tk,
                       extras=(g_res, r_in, g, b), extra_specs=(_row_spec(tm, d), _row_spec(tm, d), vec, vec),
                       store=ln_bwd_store, sequential=True, dep=dep,
                       out_shape=[jax.ShapeDtypeStruct((rows, d), F32)] + [jax.ShapeDtypeStruct((1, d), F32)] * 2,
                       out_specs=[_row_spec(tm, d), vec, vec])

    def mlp_bwd(layer, g_r, up, h_in, r_in, send=None):
        def gup_store(outs, acc, upv):
            outs[0][...] = (acc * (2.0 * jnp.maximum(upv.astype(F32), 0.0))).astype(outs[0].dtype)

        tf = min(dff, 1024)
        shp, spec = wide(tf, BF16)
        g_up = _mm_act(f"g_up{layer}", g_r, wts["down"][layer], "natT", n_out_cols=dff, k_total=d, tn=tf, tk=d,
                       extras=(up,), extra_specs=(_row_spec(tm, tf),), store=gup_store, out_shape=shp, out_specs=spec)[0]
        gr[f"down{layer}"], gr[f"mlp_b_down{layer}"] = _mm_wgrad(
            f"dw_down{layer}", up, g_r, kw=dff, n=d, tmw=tf, tn=d, a_fn=_relu2, out_dtype=BF16, colsum=True)
        gr[f"up{layer}"], gr[f"mlp_b_up{layer}"] = _mm_wgrad(
            f"dw_up{layer}", h_in, g_up, kw=d, n=dff, tmw=d, tn=min(dff, 2048), shard_cols=n_up, out_dtype=BF16, colsum=True)
        dep = send() if send is not None else None
        g_r_in, gr[f"ln_mix_g{layer}"], gr[f"ln_mix_b{layer}"] = through_ln(
            f"g_hmid{layer}", g_up, wts["up"][layer], dff, shard_tile(dff, n_up), g_r, r_in,
            small["ln_mix_g"][layer:layer + 1], small["ln_mix_b"][layer:layer + 1], dep=dep)
        return g_r_in

    g_r3 = mlp_bwd(1, g_r4, up1, h3, r3)
    g_cout = _mm_act("g_cout", g_r3, wts["out_c"], "natT", n_out_cols=d, k_total=d, tn=d, tk=d)[0]
    gr["out_c"] = _mm_wgrad("dw_out_c", c_out, g_r3, kw=d, n=d, tmw=d, tn=d, out_dtype=BF16)
    g_pc, g_lb_parts, g_ng_parts = _hgrn_bwd("hgrn_bwd", proj_c, lb, wts["ng"], g_cout, hg_states, n_batch, lp, pad, n_hg)
    gr["hgrn_gamma"], gr["ng"] = _lower_bound_bwd("hg_lb_bwd", small["hgrn_gamma"], g_lb_parts, g_ng_parts)
    gr["in_c"] = _mm_wgrad("dw_in_c", h2, g_pc, kw=d, n=4 * d, tmw=d, tn=min(4 * d, 2048), shard_cols=n_c, out_dtype=BF16)
    sent1 = on_grads(1, {"down1": gr["down1"], "up1": gr["up1"], "out_c": gr["out_c"], "in_c": gr["in_c"], "ng": gr["ng"]})
    g_r2, gr["ln_mlp_g0"], gr["ln_mlp_b0"] = through_ln(
        "g_h2", g_pc, wts["in_c"], 4 * d, shard_tile(4 * d, n_c), g_r3, r2, small["ln_mlp_g"][0:1], small["ln_mlp_b"][0:1],
        dep=sent1)

    g_r1 = mlp_bwd(0, g_r2, up0, h1, r1, send=lambda: on_grads(2, {"down0": gr["down0"], "up0": gr["up0"]}))
    g_cat = _mm_act("g_cat", g_r1, wts["out_ab"], "natT", n_out_cols=d, k_total=d, tn=d, tk=d)[0]
    gr["out_ab"] = _mm_wgrad("dw_out_ab", [a_out, b_out], g_r1, kw=s5w + sbw, n=d, tmw=min(s5w, sbw), tn=d, out_dtype=BF16)
    g_q, g_k, g_v = _attn_bwd("sb_attn_bwd", proj_ab, g_cat, n_batch, lp, pad, s5_cb, s5_cb + sb_cb, s5_cb + 2 * sb_cb,
                              s5_cb, n_pairs)

    g_y_direct, g_gate = _rowwise("s5_glu_bwd", lambda ga, yv, gt: jax.vjp(_glu, yv, gt)[1](ga),
                                  [(g_cat, 0, s5w), (y, 0, s5w), (gate, 0, s5w)], 2, s5w)

    def gelu_bwd_store(outs, acc, gyd, yp, u, dv):
        gyp = jax.vjp(jax.nn.gelu, yp)[1](acc + gyd)[0]
        outs[0][...] = gyp
        outs[1][...] = dv * gyp
        outs[2][...] = jnp.sum(gyp * u, axis=0, keepdims=True)

    rs = _row_spec(tm, s5w)
    g_ypre, g_u_direct, gd_parts = _mm_act(
        "s5_g_y", g_gate, wts["glu"], "natT", n_out_cols=s5w, k_total=s5w, tn=s5w, tk=s5w,
        extras=(g_y_direct, ypre, proj_ab, d_row), extra_specs=(rs, rs, rs, _vec_spec(s5w)), store=gelu_bwd_store,
        out_shape=[jax.ShapeDtypeStruct((rows, s5w), F32)] * 2 + [jax.ShapeDtypeStruct((rows // tm, 1, s5w), F32)],
        out_specs=[rs, rs, pl.BlockSpec((None, 1, s5w), lambda i, j, k: (i, 0, j))])
    gr["glu"], gr["s5_b_glu"] = _mm_wgrad("dw_glu", y, g_gate, kw=s5w, n=s5w, tmw=s5w, tn=s5w, out_dtype=BF16, colsum=True)
    g_sd = _mm_act("s5_g_states", g_ypre, c_blk, "natT", n_out_cols=2 * ns, k_total=s5w, tn=min(2 * ns, 2048), tk=s5w)[0]
    d_cblk = _mm_wgrad("dw_cblk", states, g_ypre, kw=2 * ns, n=s5w, tmw=min(2 * ns, 1024), tn=s5w)
    gs, gl_parts = _s5_scan_bwd("s5_scan_bwd", g_sd, states, lam_row, n_batch, lp, sw)

    def add_store(outs, acc, other):
        outs[0][...] = acc + other

    g_u = _mm_act("s5_g_u", gs, b_blk, "natT", n_out_cols=s5w, k_total=2 * ns, tn=s5w, tk=min(2 * ns, 1024),
                  extras=(g_u_direct,), extra_specs=(rs,), store=add_store)[0]
    d_bblk = _mm_wgrad("dw_bblk", proj_ab, gs, kw=s5w, n=2 * ns, tmw=s5w, tn=min(2 * ns, 2048))
    db_re, db_im = _deinterleave(d_bblk, sw)
    dc_re, dc_im = _deinterleave(d_cblk.T, sw)
    glr, gli = _deinterleave(gl_parts, sw)
    g_lam_re, g_lam_im, g_log_dt, g_b_re_t, g_b_im_t, g_d = _s5_params_bwd(
        "s5_params_bwd", lam_re, lam_im, log_dt, b_re_t, b_im_t,
        glr.reshape(n_batch, groups, n_state), gli.reshape(n_batch, groups, n_state),
        _diag_blocks(db_re, groups), _diag_blocks(db_im, groups), gd_parts)

    cat2 = lambda key: jnp.concatenate([gr[key + "0"], gr[key + "1"]], axis=0)
    small_sent = on_small({
        "s5_lam_re": g_lam_re[None], "s5_lam_im": g_lam_im[None], "s5_log_dt": g_log_dt.reshape(1, groups),
        "s5_b_re": g_b_re_t.transpose(0, 2, 1)[None], "s5_b_im": g_b_im_t.transpose(0, 2, 1)[None],
        "s5_c_re": _diag_blocks(dc_re, groups)[None], "s5_c_im": -_diag_blocks(dc_im, groups)[None],
        "s5_d": g_d.reshape(1, groups, grp), "s5_b_glu": gr["s5_b_glu"], "hgrn_gamma": gr["hgrn_gamma"],
        "ln_mix_g": cat2("ln_mix_g"), "ln_mix_b": cat2("ln_mix_b"), "mlp_b_up": cat2("mlp_b_up"),
        "mlp_b_down": cat2("mlp_b_down"), "ln_mlp_g": cat2("ln_mlp_g"), "ln_mlp_b": cat2("ln_mlp_b"),
    }, loss_tile)

    g_pab = [g_u, g_q, g_k, g_v]
    assert s5w == sbw
    gr["in_ab"] = _mm_wgrad("dw_in_ab", h0, g_pab, kw=d, n=ab_cols, tmw=d, tn=s5w, shard_cols=n_ab, out_dtype=BF16,
                            dep=small_sent)
    g_h0 = _mm_act("g_h0", g_pab, wts["in_ab"], "stkT", n_out_cols=d, k_total=ab_cols, tn=d, tk=shard_tile(s5w, n_ab),
                   extras=(g_r1,), extra_specs=(_row_spec(tm, d),), store=res_store)[0]
    grad_x = g_h0.reshape(n_batch, lp, d)[:, lead:, :]
    g_meta = _meta_grad("g_meta", g_h0, n_batch, lp, pad, n_meta)
    on_grads(3, {"meta": g_meta, "in_ab": gr["in_ab"], "glu": gr["glu"], "out_ab": gr["out_ab"]})
    return grad_x


SMALL_NAMES = ("s5_lam_re", "s5_lam_im", "s5_log_dt", "s5_b_re", "s5_b_im", "s5_c_re", "s5_c_im", "s5_d", "s5_b_glu",
               "hgrn_gamma", "ln_mix_g", "ln_mix_b", "mlp_b_up", "mlp_b_down", "ln_mlp_g", "ln_mlp_b")
WEIGHT_ORDER = ("meta", "w_in_ab", "s5_lam_re", "s5_lam_im", "s5_log_dt", "s5_b_re", "s5_b_im", "s5_c_re", "s5_c_im",
                "s5_d", "s5_w_glu", "s5_b_glu", "w_out_ab", "w_in_c", "hgrn_gamma", "hgrn_norm_g", "w_out_c", "ln_mix_g",
                "ln_mix_b", "mlp_w_up", "mlp_b_up", "mlp_w_down", "mlp_b_down", "ln_mlp_g", "ln_mlp_b")


def kernel(x, meta, w_in_ab, s5_lam_re, s5_lam_im, s5_log_dt, s5_b_re, s5_b_im, s5_c_re, s5_c_im, s5_d, s5_w_glu, s5_b_glu, w_out_ab, w_in_c, hgrn_gamma, hgrn_norm_g, w_out_c, ln_mix_g, ln_mix_b, mlp_w_up, mlp_b_up, mlp_w_down, mlp_b_down, ln_mlp_g, ln_mlp_b, loss_target, m_meta, m_w_in_ab, m_s5_lam_re, m_s5_lam_im, m_s5_log_dt, m_s5_b_re, m_s5_b_im, m_s5_c_re, m_s5_c_im, m_s5_d, m_s5_w_glu, m_s5_b_glu, m_w_out_ab, m_w_in_c, m_hgrn_gamma, m_hgrn_norm_g, m_w_out_c, m_ln_mix_g, m_ln_mix_b, m_mlp_w_up, m_mlp_b_up, m_mlp_w_down, m_mlp_b_down, m_ln_mlp_g, m_ln_mlp_b, v_meta, v_w_in_ab, v_s5_lam_re, v_s5_lam_im, v_s5_log_dt, v_s5_b_re, v_s5_b_im, v_s5_c_re, v_s5_c_im, v_s5_d, v_s5_w_glu, v_s5_b_glu, v_w_out_ab, v_w_in_c, v_hgrn_gamma, v_hgrn_norm_g, v_w_out_c, v_ln_mix_g, v_ln_mix_b, v_mlp_w_up, v_mlp_b_up, v_mlp_w_down, v_mlp_b_down, v_ln_mlp_g, v_ln_mlp_b):
    args = dict(locals())
    w = {n: args[n] for n in WEIGHT_ORDER}
    mom = {n: args["m_" + n] for n in WEIGHT_ORDER}
    var = {n: args["v_" + n] for n in WEIGHT_ORDER}
    d = x.shape[2]
    n_meta = meta.shape[0]

    cast = lambda a: a.astype(BF16)
    early = _exchange_start("gather_early_start", [w["meta"], cast(w["w_in_ab"][0]), cast(w["s5_w_glu"][0]),
                                                   cast(w["w_out_ab"][0])], False)
    late = _exchange_start("gather_late_start", [w["hgrn_norm_g"], cast(w["w_in_c"][0]), cast(w["w_out_c"][0]),
                                                 cast(w["mlp_w_up"][0]), cast(w["mlp_w_up"][1]),
                                                 cast(w["mlp_w_down"][0]), cast(w["mlp_w_down"][1])], False, dep=early["token"])
    a_meta, a_in_ab, a_glu, a_out_ab = _exchange_wait("gather_early_wait", early, late["token"])
    wts = {"in_ab": a_in_ab, "glu": a_glu.reshape(-1, a_glu.shape[2]), "out_ab": a_out_ab.reshape(-1, d)}
    meta_full = a_meta.transpose(1, 0, 2).reshape(n_meta, d)
    small = {n: w[n] for n in SMALL_NAMES}

    def late_weights(after):
        a_ng, a_in_c, a_out_c, a_up0, a_up1, a_dn0, a_dn1 = _exchange_wait("gather_late_wait", late, after)
        return {"in_c": a_in_c, "ng": a_ng.transpose(1, 0, 2).reshape(1, d), "out_c": a_out_c.reshape(-1, d),
                "up": [a_up0, a_up1], "down": [a_dn0.reshape(-1, d), a_dn1.reshape(-1, d)]}

    n_loc = d // N_DEV
    rows_of = lambda g: g.reshape(N_DEV, -1, g.shape[-1])
    cols_of = lambda g: g.reshape(g.shape[0], N_DEV, n_loc).transpose(1, 0, 2)
    sent = {}

    def on_grads(stage, g):
        if stage == 1:
            order = (("mlp_w_down", 1), ("mlp_w_up", 1), ("w_out_c", 0), ("w_in_c", 0), ("hgrn_norm_g", None))
            parts = [rows_of(g["down1"]), g["up1"], rows_of(g["out_c"]), g["in_c"], cols_of(g["ng"])]
        elif stage == 2:
            order = (("mlp_w_down", 0), ("mlp_w_up", 0))
            parts = [rows_of(g["down0"]), g["up0"]]
        else:
            order = (("w_out_ab", 0), ("s5_w_glu", 0), ("w_in_ab", 0), ("meta", None))
            parts = [rows_of(g["out_ab"]), rows_of(g["glu"]), g["in_ab"], cols_of(g["meta"])]
        sent[stage] = (order, _exchange_start(f"scatter_start{stage}", parts, True))
        return sent[stage][1]["token"]

    def on_small(sg, loss_tile):
        g_pack = _pack_rows([sg[n] for n in SMALL_NAMES] + [loss_tile], PACK_COLS)
        sent["small"] = _exchange_start("gather_small_start", [g_pack], False)
        return sent["small"]["token"]

    grad_x = _local_step(x, loss_target, meta_full, wts, small, late_weights, on_grads, on_small)
    small_sent = sent["small"]
    tile = (SUBLANES, LANES)
    shapes = [w[n].shape for n in SMALL_NAMES] + [tile]
    zeros = jnp.zeros(tile, F32)
    w_pack = _pack_rows([w[n] for n in SMALL_NAMES] + [zeros], PACK_COLS)
    m_pack = _pack_rows([mom[n] for n in SMALL_NAMES] + [zeros], PACK_COLS)
    v_pack = _pack_rows([var[n] for n in SMALL_NAMES] + [zeros], PACK_COLS)

    received, res = {}, {}

    def wait(stage, after):
        order, handle = sent[stage]
        for key, rc in zip(order, _exchange_wait(f"scatter_wait{stage}", handle, after)):
            received[key] = rc

    def update(nm):
        layered = w[nm].ndim == 3
        parts = [received[(nm, l)] for l in range(w[nm].shape[0])] if layered else received[(nm, None)]
        res[nm] = _adamw_summed(f"adamw_{nm}", parts, w[nm], mom[nm], var[nm])
        return res[nm][0]

    wait(1, sent[3][1]["token"])
    done = [update(nm) for nm in ("w_out_c", "w_in_c", "hgrn_norm_g")]
    wait(2, done[0])
    done = [update(nm) for nm in ("mlp_w_up", "mlp_w_down")]
    g_all = _exchange_wait("gather_small_wait", small_sent, done[0])[0]
    packed = _adamw_summed("adamw_small", g_all, w_pack, m_pack, v_pack)
    wait(3, packed[0])
    for nm in ("w_out_ab", "s5_w_glu", "w_in_ab", "meta"):
        update(nm)
    unpacked = [_unpack_rows(p, shapes, PACK_COLS) for p in packed]
    loss = unpacked[0][-1][0, 0]

    def pick(nm, which):
        return unpacked[which][SMALL_NAMES.index(nm)] if nm in SMALL_NAMES else res[nm][which]

    return (loss, grad_x, *[pick(n, 0) for n in WEIGHT_ORDER], *[pick(n, 1) for n in WEIGHT_ORDER],
            *[pick(n, 2) for n in WEIGHT_ORDER], *[pick(n, 3) for n in WEIGHT_ORDER])
```

```python
import functools
import math

import jax
import jax.numpy as jnp
from jax import lax
from jax.experimental import pallas as pl
from jax.experimental.pallas import tpu as pltpu

F32 = jnp.float32
BF16 = jnp.bfloat16

N_DEV = 8
DEPTH = 2
ALPHA = (2.0 * DEPTH) ** 0.25
LN_EPS = 1e-5
RMS_EPS = 1e-6
SB_HEAD_DIM = 64
HG_DK = 128
HG_CHUNK = 64
LANES = 128
SUBLANES = 8
PACKED_ROWS = 16
VMEM_LIMIT_BYTES = 56 * 1024 * 1024
ROW_TILE = 1088
SCAN_LANES = 256
SCAN_UNROLL = 2
PACK_COLS = 1024

ADAM_LR = 0.001
ADAM_B1 = 0.9
ADAM_B2 = 0.999
ADAM_EPS = 1e-08
ADAM_WD = 0.01
ADAM_STEP = 10

NN = (((1,), (0,)), ((), ()))
NT = (((1,), (1,)), ((), ()))
TN = (((0,), (0,)), ((), ()))


def _tile(n, pref, align=SUBLANES):
    t = min(n, pref)
    t -= t % align
    while t >= align:
        if n % t == 0:
            return t
        t -= align
    return n


def _unrolled_loop(n, body, init, unroll):
    assert n % unroll == 0

    def outer(t, carry):
        for u in range(unroll):
            carry = body(t * unroll + u, carry)
        return carry

    return lax.fori_loop(0, n // unroll, outer, init)


def _params(sem):
    return pltpu.CompilerParams(dimension_semantics=sem, vmem_limit_bytes=VMEM_LIMIT_BYTES)


def _dot_raw(a, b, dims):
    return lax.dot_general(a.astype(BF16), b.astype(BF16), dims, preferred_element_type=F32)


def _make_dot(dims, da_rule, db_rule):
    @jax.custom_vjp
    def f(a, b):
        return _dot_raw(a, b, dims)

    def fwd(a, b):
        return _dot_raw(a, b, dims), (a, b)

    def bwd(res, g):
        a, b = res
        return da_rule(g, a, b), db_rule(g, a, b)

    f.defvjp(fwd, bwd)
    return f


_DOTS = {
    NN: _make_dot(NN, lambda g, a, b: _dot_raw(g, b, NT), lambda g, a, b: _dot_raw(a, g, TN)),
    NT: _make_dot(NT, lambda g, a, b: _dot_raw(g, b, NN), lambda g, a, b: _dot_raw(g, a, TN)),
    TN: _make_dot(TN, lambda g, a, b: _dot_raw(b, g, NT), lambda g, a, b: _dot_raw(a, g, NN)),
}


def _dot(a, b, dims):
    return _DOTS[dims](a, b)


def _running_sums(a, tri_ones, split=False):
    hi = a.astype(BF16)
    out = lax.dot_general(hi, tri_ones, NN, preferred_element_type=F32)
    if split:
        lo = (a - hi.astype(F32)).astype(BF16)
        out = out + lax.dot_general(lo, tri_ones, NN, preferred_element_type=F32)
    return out


def _piece_specs(pieces, block_rows, block_cols, row_of, col_of, cb0):
    per = pieces[0].shape[1] // block_cols if len(pieces) > 1 else None
    specs = []
    for p in range(len(pieces)):
        if per is None:
            specs.append(pl.BlockSpec((block_rows, block_cols), lambda *g: (row_of(*g), cb0 + col_of(*g))))
        else:
            specs.append(pl.BlockSpec(
                (block_rows, block_cols),
                lambda *g, p=p: (row_of(*g), jnp.clip(col_of(*g) - p * per, 0, per - 1))))
    return specs, per


def _mm_call(name, grid, dims, a_pieces, a_specs, a_sel, b_pieces, b_specs, b_sel, extras, extra_specs,
             out_shape, out_specs, acc_shape, a_fn, store, colsum_width=0, sequential=False, deps=()):
    na, nb, ne, no, nd = len(a_pieces), len(b_pieces), len(extras), len(out_shape), len(deps)
    nk = grid[2]

    def body(*refs):
        a_refs, b_refs = refs[:na], refs[na:na + nb]
        extra = refs[na + nb:na + nb + ne]
        outs = refs[na + nb + ne + nd:na + nb + ne + nd + no]
        acc = refs[na + nb + ne + nd + no]
        ids = (pl.program_id(0), pl.program_id(1), pl.program_id(2))
        k = ids[2]

        @pl.when(k == 0)
        def _():
            acc[...] = jnp.zeros_like(acc)

        def run(a_ref, b_ref):
            a = a_ref[...]
            if a_fn is not None:
                a = a_fn(a)
            b = b_ref[...]
            if b.ndim == 3 and dims == NN:
                n = b.shape[2]
                for q in range(b.shape[0]):
                    acc[:, q * n:(q + 1) * n] += _dot_raw(a, b[q], dims)
            elif b.ndim == 3:
                n = b.shape[2]
                for q in range(b.shape[0]):
                    acc[...] += _dot_raw(a[:, q * n:(q + 1) * n], b[q], dims)
            else:
                acc[...] += _dot_raw(a, b, dims)
            if colsum_width:
                cs = refs[-1]
                first = ids[1] == 0

                @pl.when(first & (k == 0))
                def _():
                    cs[...] = jnp.zeros_like(cs)

                @pl.when(first)
                def _():
                    cs[...] += jnp.sum(b.astype(F32), axis=0, keepdims=True)

        if na == 1 and nb == 1:
            run(a_refs[0], b_refs[0])
        elif nb == 1:
            per, fn = a_sel
            which = fn(*ids) // per
            for p in range(na):
                pl.when(which == p)(functools.partial(run, a_refs[p], b_refs[0]))
        else:
            assert na == 1
            per, fn = b_sel
            which = fn(*ids) // per
            for p in range(nb):
                pl.when(which == p)(functools.partial(run, a_refs[0], b_refs[p]))

        @pl.when(k == nk - 1)
        def _():
            if sequential:
                store(outs, acc[...], *[e[...] for e in extra], first_step=(ids[0] == 0) & (ids[1] == 0))
            else:
                store(outs, acc[...], *[e[...] for e in extra])
            if colsum_width:
                @pl.when(ids[1] == 0)
                def _():
                    outs[-1][...] = refs[-1][...]

    scratch = [pltpu.VMEM(acc_shape, F32)]
    if colsum_width:
        scratch.append(pltpu.VMEM((1, colsum_width), F32))
    sem = ("parallel", "arbitrary", "arbitrary") if colsum_width else ("parallel", "parallel", "arbitrary")
    if sequential:
        sem = ("arbitrary",) * 3
    return pl.pallas_call(
        body, name=name, grid=grid,
        in_specs=[*a_specs, *b_specs, *extra_specs, *[pl.BlockSpec(memory_space=pl.ANY)] * nd], out_specs=out_specs,
        out_shape=out_shape, scratch_shapes=scratch, compiler_params=_params(sem),
    )(*a_pieces, *b_pieces, *extras, *deps)


def _store_plain(outs, acc):
    outs[0][...] = acc.astype(outs[0].dtype)


def _row_spec(tm, tn):
    return pl.BlockSpec((tm, tn), lambda i, j, k: (i, j))


def _vec_spec(tn):
    return pl.BlockSpec((1, tn), lambda i, j, k: (0, j))


def _mm_act(name, a, w, wkind, *, n_out_cols, k_total, tn, tk, a_cb0=0, a_fn=None, extras=(), extra_specs=(),
            store=_store_plain, out_shape=None, out_specs=None, sequential=False, dep=None):
    a_pieces = list(a) if isinstance(a, (list, tuple)) else [a]
    rows = a_pieces[0].shape[0]
    tm = _tile(rows, ROW_TILE)
    grid = (rows // tm, n_out_cols // tn, k_total // tk)
    a_specs, per = _piece_specs(a_pieces, tm, tk, lambda i, j, k: i, lambda i, j, k: k, a_cb0)
    if wkind == "nat":
        b_spec, dims = pl.BlockSpec((tk, tn), lambda i, j, k: (k, j)), NN
    elif wkind == "stk":
        n = w.shape[2]
        assert tn % n == 0
        b_spec, dims = pl.BlockSpec((tn // n, tk, n), lambda i, j, k: (j, k, 0)), NN
    elif wkind == "natT":
        b_spec, dims = pl.BlockSpec((tn, tk), lambda i, j, k: (j, k)), NT
    else:
        n = w.shape[2]
        assert wkind == "stkT" and tk % n == 0
        b_spec, dims = pl.BlockSpec((tk // n, tn, n), lambda i, j, k: (k, j, 0)), NT
    if out_shape is None:
        out_shape = [jax.ShapeDtypeStruct((rows, n_out_cols), F32)]
        out_specs = [_row_spec(tm, tn)]
    return _mm_call(name, grid, dims, a_pieces, a_specs, (per, lambda i, j, k: k), [w], [b_spec], None,
                    list(extras), list(extra_specs), out_shape, out_specs, (tm, tn), a_fn, store,
                    sequential=sequential, deps=() if dep is None else (dep,))


def _mm_wgrad(name, a, g, *, kw, n, tmw, tn, a_cb0=0, a_fn=None, shard_cols=0, out_dtype=F32, colsum=False, dep=None):
    a_pieces = list(a) if isinstance(a, (list, tuple)) else [a]
    g_pieces = list(g) if isinstance(g, (list, tuple)) else [g]
    rows = a_pieces[0].shape[0]
    tr = _tile(rows, ROW_TILE)
    grid = (n // tn, kw // tmw, rows // tr)
    a_specs, a_per = _piece_specs(a_pieces, tr, tmw, lambda j, i, k: k, lambda j, i, k: i, a_cb0)
    g_specs, g_per = _piece_specs(g_pieces, tr, tn, lambda j, i, k: k, lambda j, i, k: j, 0)
    if shard_cols:
        per = tn // shard_cols
        out_shape = [jax.ShapeDtypeStruct((n // shard_cols, kw, shard_cols), out_dtype)]
        out_specs = [pl.BlockSpec((per, tmw, shard_cols), lambda j, i, k: (j, i, 0))]

        def store(outs, acc):
            for q in range(per):
                outs[0][q] = acc[:, q * shard_cols:(q + 1) * shard_cols].astype(out_dtype)
    else:
        out_shape = [jax.ShapeDtypeStruct((kw, n), out_dtype)]
        out_specs = [pl.BlockSpec((tmw, tn), lambda j, i, k: (i, j))]

        def store(outs, acc):
            outs[0][...] = acc.astype(out_dtype)
    if colsum:
        out_shape.append(jax.ShapeDtypeStruct((1, n), F32))
        out_specs.append(pl.BlockSpec((1, tn), lambda j, i, k: (0, j)))
    res = _mm_call(name, grid, TN, a_pieces, a_specs, (a_per, lambda j, i, k: i), g_pieces, g_specs,
                   (g_per, lambda j, i, k: j), [], [], out_shape, out_specs, (tmw, tn), a_fn, store,
                   colsum_width=tn if colsum else 0, deps=() if dep is None else (dep,))
    return res if colsum else res[0]


def _ln(x, g, b):
    mu = jnp.mean(x, axis=-1, keepdims=True)
    xc = x - mu
    var = jnp.mean(xc * xc, axis=-1, keepdims=True)
    return xc * lax.rsqrt(var + LN_EPS) * g + b


def _relu2(x):
    r = jnp.maximum(x.astype(F32), 0.0)
    return r * r


def _glu(y, gate):
    return y * jax.nn.sigmoid(gate)


def _rowwise(name, fn, ins, n_out, width):
    rows = ins[0][0].shape[0]
    tm = _tile(rows, ROW_TILE)

    def body(*refs):
        res = fn(*[r[...] for r in refs[:len(ins)]])
        for o, v in zip(refs[len(ins):], res):
            o[...] = v

    return pl.pallas_call(
        body, name=name, grid=(rows // tm,),
        in_specs=[pl.BlockSpec((tm, wd), lambda i, cb=cb: (i, cb)) for _, cb, wd in ins],
        out_specs=[pl.BlockSpec((tm, width), lambda i: (i, 0))] * n_out,
        out_shape=[jax.ShapeDtypeStruct((rows, width), F32)] * n_out, compiler_params=_params(("parallel",)),
    )(*[a for a, _, _ in ins])


def _loss_grad(name, r, g, b, target, n_batch, lp, lead):
    rows, d = r.shape
    nq = lp // LANES
    lead_blocks = lead // LANES

    def body(r_ref, g_ref, b_ref, t_ref, gr_ref, gg_ref, gb_ref, loss_ref):
        i = pl.program_id(1)

        @pl.when((pl.program_id(0) == 0) & (i == 0))
        def _():
            loss_ref[...] = jnp.zeros_like(loss_ref)
            gg_ref[...] = jnp.zeros_like(gg_ref)
            gb_ref[...] = jnp.zeros_like(gb_ref)

        h, vjp = jax.vjp(_ln, r_ref[...], g_ref[...], b_ref[...])
        diff = jnp.where(i >= lead_blocks, h - t_ref[...], 0.0)
        gr, gg, gb = vjp(diff * (1.0 / d))
        gr_ref[...] = gr
        gg_ref[...] += gg
        gb_ref[...] += gb
        loss_ref[...] += 0.5 * jnp.sum(diff * diff) * (1.0 / d)

    vec = pl.BlockSpec((1, d), lambda b, i: (0, 0))
    row = pl.BlockSpec((LANES, d), lambda b, i: (b * nq + i, 0))
    return pl.pallas_call(
        body, name=name, grid=(n_batch, nq),
        in_specs=[row, vec, vec, pl.BlockSpec((None, LANES, d), lambda b, i: (b, jnp.maximum(i - lead_blocks, 0), 0))],
        out_specs=[row, vec, vec, pl.BlockSpec((SUBLANES, LANES), lambda b, i: (0, 0))],
        out_shape=[jax.ShapeDtypeStruct((rows, d), F32), jax.ShapeDtypeStruct((1, d), F32),
                   jax.ShapeDtypeStruct((1, d), F32), jax.ShapeDtypeStruct((SUBLANES, LANES), F32)],
        compiler_params=_params(("arbitrary", "arbitrary")),
    )(r, g, b, target)


def _meta_grad(name, g_h0, n_batch, lp, pad, n_meta):
    d = g_h0.shape[1]
    per = lp // n_meta
    at = pad // n_meta

    def body(g_ref, o_ref):
        @pl.when(pl.program_id(0) == 0)
        def _():
            o_ref[...] = jnp.zeros_like(o_ref)

        o_ref[...] += g_ref[...]

    return pl.pallas_call(
        body, name=name, grid=(n_batch,),
        in_specs=[pl.BlockSpec((n_meta, d), lambda b: (b * per + at, 0))],
        out_specs=pl.BlockSpec((n_meta, d), lambda b: (0, 0)),
        out_shape=jax.ShapeDtypeStruct((n_meta, d), F32),
        compiler_params=_params(("arbitrary",)),
    )(g_h0)


def _s5_param_fn(lr, li, ldt, br, bi):
    dt = jnp.exp(ldt)
    e = jnp.exp(lr * dt)
    w = li * dt
    lbr = e * jnp.cos(w)
    lbi = e * jnp.sin(w)
    nr = lbr - 1.0
    den = lr * lr + li * li
    cr = (nr * lr + lbi * li) / den
    ci = (lbi * lr - nr * li) / den
    bbr = cr[:, None, :] * br - ci[:, None, :] * bi
    bbi = cr[:, None, :] * bi + ci[:, None, :] * br
    return lbr, lbi, bbr, bbi


def _s5_params(name, lr, li, ldt, br, bi):
    def body(lr_ref, li_ref, ldt_ref, br_ref, bi_ref, o1, o2, o3, o4):
        res = _s5_param_fn(lr_ref[...], li_ref[...], ldt_ref[...], br_ref[...], bi_ref[...])
        for o, v in zip((o1, o2, o3, o4), res):
            o[...] = v

    shp = [jax.ShapeDtypeStruct(lr.shape, F32)] * 2 + [jax.ShapeDtypeStruct(br.shape, F32)] * 2
    return pl.pallas_call(body, name=name, out_shape=shp)(lr, li, ldt, br, bi)


def _s5_params_bwd(name, lr, li, ldt, br, bi, g_lbr, g_lbi, g_bbr, g_bbi, gd_parts):
    def body(lr_ref, li_ref, ldt_ref, br_ref, bi_ref, g1, g2, g3, g4, gd_ref, o1, o2, o3, o4, o5, o6):
        _, vjp = jax.vjp(_s5_param_fn, lr_ref[...], li_ref[...], ldt_ref[...], br_ref[...], bi_ref[...])
        res = vjp((jnp.sum(g1[...], axis=0), jnp.sum(g2[...], axis=0), g3[...], g4[...]))
        for o, v in zip((o1, o2, o3, o4, o5), res):
            o[...] = v
        o6[...] = jnp.sum(gd_ref[...], axis=0)

    shp = ([jax.ShapeDtypeStruct(lr.shape, F32)] * 2 + [jax.ShapeDtypeStruct(ldt.shape, F32)]
           + [jax.ShapeDtypeStruct(br.shape, F32)] * 2 + [jax.ShapeDtypeStruct(gd_parts.shape[1:], F32)])
    return pl.pallas_call(body, name=name, out_shape=shp)(lr, li, ldt, br, bi, g_lbr, g_lbi, g_bbr, g_bbi, gd_parts)


def _interleave(re, im, w):
    nj = re.shape[-1] // w
    return jnp.concatenate([x[..., j * w:(j + 1) * w] for j in range(nj) for x in (re, im)], axis=-1)


def _deinterleave(x, w):
    nj = x.shape[-1] // (2 * w)
    return (jnp.concatenate([x[..., 2 * j * w:(2 * j + 1) * w] for j in range(nj)], axis=-1),
            jnp.concatenate([x[..., (2 * j + 1) * w:(2 * j + 2) * w] for j in range(nj)], axis=-1))


def _cmul(ar, ai, br, bi):
    return ar * br - ai * bi, ar * bi + ai * br


def _powers(lr, li):
    p = [(lr, li)]
    p.append(_cmul(*p[0], *p[0]))
    p.append(_cmul(*p[1], *p[0]))
    p.append(_cmul(*p[1], *p[1]))
    p.append(_cmul(*p[3], *p[0]))
    p.append(_cmul(*p[3], *p[1]))
    p.append(_cmul(*p[3], *p[2]))
    p.append(_cmul(*p[3], *p[3]))
    return p


def _scan_steps(pw, shifts, keep):
    return [(sh, jnp.where(m, pw[s - 1][0], 0.0), jnp.where(m, pw[s - 1][1], 0.0))
            for s, sh, m in zip((1, 2, 4), shifts, keep)]


def _scan_tile(xr, xi, steps):
    for sh, br, bi in steps:
        rr = pltpu.roll(xr, sh, 0)
        ri = pltpu.roll(xi, sh, 0)
        xr, xi = xr + (br * rr - bi * ri), xi + (br * ri + bi * rr)
    return xr, xi


def _s5_scan(name, bu, lam, n_batch, lp, w):
    rows, two_ns = bu.shape
    nj = two_ns // (2 * w)

    def body(x_ref, lam_ref, s_ref):
        pw = _powers(lam_ref[:, :w], lam_ref[:, w:])
        tab_r = jnp.concatenate([p[0] for p in pw], axis=0)
        tab_i = jnp.concatenate([p[1] for p in pw], axis=0)
        row = lax.broadcasted_iota(jnp.int32, (SUBLANES, w), 0)
        steps = _scan_steps(pw, (1, 2, 4), [row >= s for s in (1, 2, 4)])

        def packed_tile(t, carry):
            cr, ci = carry
            r0 = pl.multiple_of(t * PACKED_ROWS, PACKED_ROWS)
            x = x_ref[pl.ds(r0, PACKED_ROWS), :].astype(F32)
            done = []
            for half in range(PACKED_ROWS // SUBLANES):
                xt = x[half * SUBLANES:(half + 1) * SUBLANES, :]
                xr, xi = _scan_tile(xt[:, :w], xt[:, w:], steps)
                sr = xr + (tab_r * cr - tab_i * ci)
                si = xi + (tab_r * ci + tab_i * cr)
                done.append(jnp.concatenate([sr, si], axis=1))
                cr, ci = sr[SUBLANES - 1:, :], si[SUBLANES - 1:, :]
            s_ref[pl.ds(r0, PACKED_ROWS), :] = jnp.concatenate(done, axis=0).astype(s_ref.dtype)
            return cr, ci

        zero = jnp.zeros((1, w), F32)
        _unrolled_loop(lp // PACKED_ROWS, packed_tile, (zero, zero), SCAN_UNROLL)

    spec = pl.BlockSpec((lp, 2 * w), lambda b, j: (b, j))
    return pl.pallas_call(
        body, name=name, grid=(n_batch, nj), in_specs=[spec, pl.BlockSpec((1, 2 * w), lambda b, j: (0, j))],
        out_specs=spec, out_shape=jax.ShapeDtypeStruct((rows, two_ns), BF16),
        compiler_params=_params(("parallel", "parallel")),
    )(bu, lam)


def _s5_scan_bwd(name, gd, states, lam, n_batch, lp, w):
    rows, two_ns = gd.shape
    nj = two_ns // (2 * w)

    def body(x_ref, s_ref, lam_ref, g_ref, gl_ref):
        pw = _powers(lam_ref[:, :w], -lam_ref[:, w:])
        tab_r = jnp.concatenate([p[0] for p in reversed(pw)], axis=0)
        tab_i = jnp.concatenate([p[1] for p in reversed(pw)], axis=0)
        row = lax.broadcasted_iota(jnp.int32, (SUBLANES, w), 0)
        steps = _scan_steps(pw, [SUBLANES - s for s in (1, 2, 4)], [row < SUBLANES - s for s in (1, 2, 4)])

        n_packed = lp // PACKED_ROWS
        halves = PACKED_ROWS // SUBLANES

        def packed_tile(u, carry):
            cr, ci, ar, ai = carry
            t = n_packed - 1 - u
            r0 = pl.multiple_of(t * PACKED_ROWS, PACKED_ROWS)
            x = x_ref[pl.ds(r0, PACKED_ROWS), :].astype(F32)
            cur = s_ref[pl.ds(r0, PACKED_ROWS), :].astype(F32)
            p0 = pl.multiple_of(jnp.maximum(t - 1, 0) * PACKED_ROWS, PACKED_ROWS)
            before = s_ref[pl.ds(p0, PACKED_ROWS), :].astype(F32)[PACKED_ROWS - 1:, :] * jnp.where(t > 0, 1.0, 0.0)
            done = [None] * halves
            for half in reversed(range(halves)):
                rows_h = slice(half * SUBLANES, (half + 1) * SUBLANES)
                xt, st = x[rows_h, :], cur[rows_h, :]
                xr, xi = _scan_tile(xt[:, :w], xt[:, w:], steps)
                gr = xr + (tab_r * cr - tab_i * ci)
                gi = xi + (tab_r * ci + tab_i * cr)
                done[half] = jnp.concatenate([gr, gi], axis=1)
                prev = before if half == 0 else cur[half * SUBLANES - 1:half * SUBLANES, :]
                spr = jnp.where(row >= 1, pltpu.roll(st[:, :w], 1, 0), prev[:, :w])
                spi = jnp.where(row >= 1, pltpu.roll(st[:, w:], 1, 0), prev[:, w:])
                cr, ci, ar, ai = gr[:1, :], gi[:1, :], ar + gr * spr + gi * spi, ai + gi * spr - gr * spi
            g_ref[pl.ds(r0, PACKED_ROWS), :] = jnp.concatenate(done, axis=0).astype(g_ref.dtype)
            return cr, ci, ar, ai

        z1 = jnp.zeros((1, w), F32)
        z8 = jnp.zeros((SUBLANES, w), F32)
        _, _, ar, ai = _unrolled_loop(n_packed, packed_tile, (z1, z1, z8, z8), SCAN_UNROLL)
        gl_ref[...] = jnp.concatenate([jnp.sum(ar, axis=0, keepdims=True), jnp.sum(ai, axis=0, keepdims=True)], axis=1)

    spec = pl.BlockSpec((lp, 2 * w), lambda b, j: (b, j))
    return pl.pallas_call(
        body, name=name, grid=(n_batch, nj),
        in_specs=[spec, spec, pl.BlockSpec((1, 2 * w), lambda b, j: (0, j))],
        out_specs=[spec, pl.BlockSpec((None, 1, 2 * w), lambda b, j: (b, 0, j))],
        out_shape=[jax.ShapeDtypeStruct((rows, two_ns), BF16), jax.ShapeDtypeStruct((n_batch, 1, two_ns), F32)],
        compiler_params=_params(("parallel", "parallel")),
    )(gd, states, lam)


def _log_sigmoid(z):
    return jnp.minimum(z, 0.0) - jnp.log(1.0 + jnp.exp(-jnp.abs(z)))


ATTN_KEYS = 256
ATTN_GROUP = 4


def _attn_block(i, jb, lp, pad):
    start = jb * ATTN_KEYS
    r0 = pl.multiple_of(jnp.minimum(start, lp - ATTN_KEYS), LANES)
    rowpos = i * LANES + lax.broadcasted_iota(jnp.int32, (LANES, ATTN_KEYS), 0)
    keypos = r0 + lax.broadcasted_iota(jnp.int32, (LANES, ATTN_KEYS), 1)
    return r0, (keypos < rowpos) & (keypos >= jnp.maximum(start, pad))


def _tri_ones(strict_upper):
    r = lax.broadcasted_iota(jnp.int32, (ATTN_KEYS, ATTN_KEYS + LANES), 0)
    c = lax.broadcasted_iota(jnp.int32, (ATTN_KEYS, ATTN_KEYS + LANES), 1)
    tri = (r > c) if strict_upper else (r < c)
    return jnp.where((c >= ATTN_KEYS) | tri, 1.0, 0.0).astype(BF16)


def _split_sums(cr):
    rs = cr[:, ATTN_KEYS:]
    return cr[:, :ATTN_KEYS], jnp.concatenate([rs] * (ATTN_KEYS // LANES), axis=1)


def _head_masks():
    lane = lax.broadcasted_iota(jnp.int32, (1, LANES), 1)
    return [lane < SB_HEAD_DIM, lane >= SB_HEAD_DIM]


def _run_groups(n, first, sign, make):
    j, left, g = first, n, ATTN_GROUP
    while g >= 1:
        shift = g.bit_length() - 1
        count = lax.shift_right_logical(left, shift)
        fn = make(g)

        def loop(_, jcur, fn=fn, g=g):
            fn(jcur)
            return jcur + sign * g

        j = lax.fori_loop(0, count, loop, j)
        left = left - lax.shift_left(count, shift)
        g //= 2


def _attn_fwd(name, proj, n_batch, lp, pad, q_cb, k_cb, v_cb, n_pairs):
    rows = proj.shape[0]
    nq = lp // LANES
    scale = SB_HEAD_DIM ** -0.5

    def body(q_ref, k_ref, v_ref, o_ref, acc_s):
        i = pl.program_id(1)
        hm = _head_masks()
        comb = _tri_ones(True)
        n_blocks = lax.shift_right_logical(i + ATTN_KEYS // LANES, (ATTN_KEYS // LANES).bit_length() - 1)

        def pair(hp, carry):
            lanes = pl.ds(pl.multiple_of(hp * LANES, LANES), LANES)
            qs = q_ref[:, lanes] * scale
            qh = [jnp.where(m, qs, 0.0).astype(BF16) for m in hm]
            acc_s[...] = jnp.zeros_like(acc_s)
            o_ref[:, lanes] = jnp.zeros((LANES, LANES), F32)

            def make(group):
                def fn(jtop):
                    chains = []
                    for g in range(group):
                        r0, vis = _attn_block(i, jtop - g, lp, pad)
                        kj = k_ref[pl.ds(r0, ATTN_KEYS), lanes].astype(BF16)
                        vj = v_ref[pl.ds(r0, ATTN_KEYS), lanes]
                        for h in range(2):
                            z = lax.dot_general(qh[h], kj, NT, preferred_element_type=F32)
                            chains.append((h, vis, z, jnp.where(hm[h], vj, 0.0).astype(BF16)))
                    staged = []
                    for h, vis, z, vh in chains:
                        lsz = _log_sigmoid(z)
                        staged.append((h, vis, lsz, _running_sums(jnp.where(vis, lsz - z, 0.0), comb, split=True), vh))
                    out = o_ref[:, lanes]
                    for h, vis, lsz, cr, vh in staged:
                        later, rs = _split_sums(cr)
                        acc = acc_s[h]
                        wgt = jnp.where(vis, jnp.exp(lsz + later + acc), 0.0)
                        acc_s[h] = acc + rs
                        out = out + lax.dot_general(wgt.astype(BF16), vh, NN, preferred_element_type=F32)
                    o_ref[:, lanes] = out
                return fn

            _run_groups(n_blocks, n_blocks - 1, -1, make)
            return carry

        lax.fori_loop(0, n_pairs, pair, 0)

    wide = n_pairs * LANES
    assert q_cb % n_pairs == 0 and k_cb % n_pairs == 0 and v_cb % n_pairs == 0
    return pl.pallas_call(
        body, name=name, grid=(n_batch, nq),
        in_specs=[pl.BlockSpec((LANES, wide), lambda b, i: (b * nq + i, q_cb // n_pairs)),
                  pl.BlockSpec((lp, wide), lambda b, i: (b, k_cb // n_pairs)),
                  pl.BlockSpec((lp, wide), lambda b, i: (b, v_cb // n_pairs))],
        out_specs=pl.BlockSpec((LANES, wide), lambda b, i: (b * nq + i, 0)),
        out_shape=jax.ShapeDtypeStruct((rows, wide), F32),
        scratch_shapes=[pltpu.VMEM((2, LANES, ATTN_KEYS), F32)],
        compiler_params=_params(("parallel", "arbitrary")),
    )(proj, proj, proj)


def _attn_bwd(name, proj, g_out, n_batch, lp, pad, q_cb, k_cb, v_cb, go_cb, n_pairs):
    rows = proj.shape[0]
    nq = lp // LANES
    scale = SB_HEAD_DIM ** -0.5

    def body(q_ref, k_ref, v_ref, go_ref, gq_ref, gk_ref, gv_ref, ga_s, sz_s, acc_s):
        i = pl.program_id(1)

        @pl.when(i == 0)
        def _():
            gk_ref[...] = jnp.zeros_like(gk_ref)
            gv_ref[...] = jnp.zeros_like(gv_ref)

        hm = _head_masks()
        comb_up = _tri_ones(True)
        comb_lo = _tri_ones(False)
        n_blocks = lax.shift_right_logical(i + ATTN_KEYS // LANES, (ATTN_KEYS // LANES).bit_length() - 1)

        def pair(hp, carry):
            lanes = pl.ds(pl.multiple_of(hp * LANES, LANES), LANES)
            qs = q_ref[:, lanes] * scale
            go = go_ref[:, lanes]
            qh = [jnp.where(m, qs, 0.0).astype(BF16) for m in hm]
            goh = [jnp.where(m, go, 0.0).astype(BF16) for m in hm]
            acc_s[...] = jnp.zeros_like(acc_s)

            def make_down(group):
                def fn(jtop):
                    chains = []
                    for g in range(group):
                        j = jtop - g
                        r0, vis = _attn_block(i, j, lp, pad)
                        kj = k_ref[pl.ds(r0, ATTN_KEYS), lanes].astype(BF16)
                        vj = v_ref[pl.ds(r0, ATTN_KEYS), lanes].astype(BF16)
                        for h in range(2):
                            z = lax.dot_general(qh[h], kj, NT, preferred_element_type=F32)
                            gw = lax.dot_general(goh[h], vj, NT, preferred_element_type=F32)
                            chains.append((h, j, r0, vis, z, gw))
                    staged = []
                    for h, j, r0, vis, z, gw in chains:
                        lsz = _log_sigmoid(z)
                        staged.append((h, j, r0, vis, lsz, _running_sums(jnp.where(vis, lsz - z, 0.0), comb_up), gw))
                    for h, j, r0, vis, lsz, cr, gw in staged:
                        later, rs = _split_sums(cr)
                        acc = acc_s[h]
                        wgt = jnp.where(vis, jnp.exp(lsz + later + acc), 0.0)
                        acc_s[h] = acc + rs
                        ga_s[h, j] = gw * wgt
                        sz_s[h, j] = jnp.exp(lsz)
                        gv_ref[pl.ds(r0, ATTN_KEYS), lanes] += lax.dot_general(
                            wgt.astype(BF16), goh[h], TN, preferred_element_type=F32)
                return fn

            _run_groups(n_blocks, n_blocks - 1, -1, make_down)
            acc_s[...] = jnp.zeros_like(acc_s)

            def make_up(group):
                def fn(jbot):
                    pend = []
                    for g in range(group):
                        j = jbot + g
                        r0, vis = _attn_block(i, j, lp, pad)
                        kj = k_ref[pl.ds(r0, ATTN_KEYS), lanes]
                        for h in range(2):
                            ga = ga_s[h, j]
                            pend.append((h, j, r0, vis, ga, _running_sums(ga, comb_lo),
                                         jnp.where(hm[h], kj, 0.0).astype(BF16)))
                    gq = jnp.zeros((LANES, LANES), F32)
                    for h, j, r0, vis, ga, cr, kh in pend:
                        before, rs = _split_sums(cr)
                        pre = acc_s[h]
                        glk = before + pre
                        acc_s[h] = pre + rs
                        sz = sz_s[h, j]
                        gz = jnp.where(vis, ga * (1.0 - sz) - glk * sz, 0.0).astype(BF16)
                        gq = gq + lax.dot_general(gz, kh, NN, preferred_element_type=F32)
                        gk_ref[pl.ds(r0, ATTN_KEYS), lanes] += lax.dot_general(gz, qh[h], TN, preferred_element_type=F32)
                    gq_ref[:, lanes] += gq * scale
                return fn

            gq_ref[:, lanes] = jnp.zeros((LANES, LANES), F32)
            _run_groups(n_blocks, 0, 1, make_up)
            return carry

        lax.fori_loop(0, n_pairs, pair, 0)

    wide = n_pairs * LANES
    assert q_cb % n_pairs == 0 and k_cb % n_pairs == 0 and v_cb % n_pairs == 0 and go_cb % n_pairs == 0
    blk = lambda cb: pl.BlockSpec((LANES, wide), lambda b, i: (b * nq + i, cb // n_pairs))
    full = lambda cb: pl.BlockSpec((lp, wide), lambda b, i: (b, cb // n_pairs))
    shp = jax.ShapeDtypeStruct((rows, wide), F32)
    per_block = pltpu.VMEM((2, -(-lp // ATTN_KEYS), LANES, ATTN_KEYS), F32)
    return pl.pallas_call(
        body, name=name, grid=(n_batch, nq),
        in_specs=[blk(q_cb), full(k_cb), full(v_cb), blk(go_cb)],
        out_specs=[blk(0), full(0), full(0)], out_shape=[shp, shp, shp],
        scratch_shapes=[per_block, per_block, pltpu.VMEM((2, LANES, ATTN_KEYS), F32)],
        compiler_params=_params(("parallel", "arbitrary")),
    )(proj, proj, proj, g_out)


def _lb_fn(gamma):
    g0, g1 = gamma[0:1, :], gamma[1:2, :]
    mx = jnp.maximum(g0, g1)
    e0, e1 = jnp.exp(g0 - mx), jnp.exp(g1 - mx)
    p0, p1 = e0 / (e0 + e1), e1 / (e0 + e1)
    return (p0 + p1) - p0


def _lower_bound(name, gamma):
    def body(g_ref, o_ref):
        o_ref[...] = _lb_fn(g_ref[...])

    return pl.pallas_call(body, name=name, out_shape=jax.ShapeDtypeStruct((1, gamma.shape[1]), F32))(gamma)


def _lower_bound_bwd(name, gamma, g_lb_parts, g_ng_parts):
    def body(g_ref, glb_ref, gng_ref, o_ref, o2_ref):
        _, vjp = jax.vjp(_lb_fn, g_ref[...])
        o_ref[...] = vjp(jnp.sum(glb_ref[...], axis=0))[0]
        o2_ref[...] = jnp.sum(gng_ref[...], axis=0)

    return pl.pallas_call(
        body, name=name,
        out_shape=[jax.ShapeDtypeStruct(gamma.shape, F32), jax.ShapeDtypeStruct((1, gamma.shape[1]), F32)],
    )(gamma, g_lb_parts, g_ng_parts)


def _tri_times(tril, x, dims):
    hi = x.astype(BF16)
    lo = (x - hi.astype(F32)).astype(BF16)
    t = tril.astype(BF16)
    return (lax.dot_general(t, hi, dims, preferred_element_type=F32)
            + lax.dot_general(t, lo, dims, preferred_element_type=F32))


@jax.custom_vjp
def _cumsum_rows(x, tril):
    return _tri_times(tril, x, NN)


def _cumsum_rows_fwd(x, tril):
    return _tri_times(tril, x, NN), tril


def _cumsum_rows_bwd(tril, g):
    return _tri_times(tril, g, TN), jnp.zeros_like(tril)


_cumsum_rows.defvjp(_cumsum_rows_fwd, _cumsum_rows_bwd)


def _hg_decays(f_pre, lbs, masks, tril):
    f = [[lb + (1.0 - lb) * jax.nn.sigmoid(fc) for fc, lb in zip(row, lbs)] for row in f_pre]
    bcum = [[_cumsum_rows(jnp.log(x) * m, tril) for x in row] for row, m in zip(f, masks)]
    return [[1.0 - x for x in row] for row in f], bcum


def _hg_step(q, f_pre, i_in, g, lbs, ngs, sts, masks, tril):
    k, bcum = _hg_decays(f_pre, lbs, masks, tril)
    v = [[ic * m for ic in row] for row, m in zip(i_in, masks)]
    qd = [[qc * jnp.exp(b) for qc, b in zip(qr, br)] for qr, br in zip(q, bcum)]
    scores = [[jnp.where(tril > 0.5, _dot(a, kk * jnp.exp(-b), NT), 0.0) for a, kk, b in zip(ar, kr, br)]
              for ar, kr, br in zip(qd, k, bcum)]
    inner = [[_dot(s, x, NN) for s, x in zip(sr, vr)] for sr, vr in zip(scores, v)]
    add = [[_dot(x, kk * jnp.exp(b[HG_CHUNK - 1:, :] - b), TN) for x, kk, b in zip(vr, kr, br)]
           for vr, kr, br in zip(v, k, bcum)]
    outs = []
    for qr, br, nr, ar, gr in zip(qd, bcum, inner, add, g):
        o = [n + _dot(a, st, NT) for n, a, st in zip(nr, qr, sts)]
        sts = [jnp.exp(b[HG_CHUNK - 1:, :]) * st + a for b, a, st in zip(br, ar, sts)]
        o = [x * lax.rsqrt(jnp.mean(x * x, axis=-1, keepdims=True) + RMS_EPS) * ng for x, ng in zip(o, ngs)]
        outs.append([x * (gc * jax.nn.sigmoid(gc)) for x, gc in zip(o, gr)])
    return outs, sts


def _hg_consts(c, pad):
    r = lax.broadcasted_iota(jnp.int32, (HG_CHUNK, HG_CHUNK), 0)
    cc = lax.broadcasted_iota(jnp.int32, (HG_CHUNK, HG_CHUNK), 1)
    tril = jnp.where(r >= cc, 1.0, 0.0).astype(F32)
    pos = c * HG_CHUNK + lax.broadcasted_iota(jnp.int32, (HG_CHUNK, 1), 0)
    return tril, jnp.where(pos >= pad, 1.0, 0.0).astype(F32)


HG_HEADS_PER_STEP = 8
HG_CHUNKS_PER_STEP = 2


def _hg_layout(lp, n_heads):
    step_rows = HG_CHUNKS_PER_STEP * HG_CHUNK
    per = min(HG_HEADS_PER_STEP, n_heads)
    assert lp % step_rows == 0 and n_heads % per == 0
    heads = [(h, slice(h * HG_DK, (h + 1) * HG_DK)) for h in range(per)]
    return n_heads // per, lp // step_rows, step_rows, per * HG_DK, heads


def _hg_step_views(step, pad, heads):
    slices = [slice(u * HG_CHUNK, (u + 1) * HG_CHUNK) for u in range(HG_CHUNKS_PER_STEP)]
    consts = [_hg_consts(step * HG_CHUNKS_PER_STEP + u, pad) for u in range(HG_CHUNKS_PER_STEP)]
    load = lambda ref: [[ref[sl, cols] for _, cols in heads] for sl in slices]
    return slices, [m for _, m in consts], consts[0][0], load


def _hgrn_fwd(name, proj, lb, ng, n_batch, lp, pad, n_heads):
    rows = proj.shape[0]
    groups, steps, step_rows, wide, heads = _hg_layout(lp, n_heads)

    def body(q_ref, f_ref, i_ref, g_ref, lb_ref, ng_ref, o_ref, s_ref, st_s):
        t = pl.program_id(2)

        @pl.when(t == 0)
        def _():
            st_s[...] = jnp.zeros_like(st_s)

        slices, masks, tril, load = _hg_step_views(t, pad, heads)
        sts = [st_s[h] for h, _ in heads]
        for (_, cols), st in zip(heads, sts):
            s_ref[:, cols] = st
        outs, sts = _hg_step(load(q_ref), load(f_ref), load(i_ref), load(g_ref),
                             [lb_ref[:, cols] for _, cols in heads], [ng_ref[:, cols] for _, cols in heads],
                             sts, masks, tril)
        for sl, row in zip(slices, outs):
            for (_, cols), o in zip(heads, row):
                o_ref[sl, cols] = o
        for (h, _), st in zip(heads, sts):
            st_s[h] = st

    col = lambda off: pl.BlockSpec((step_rows, wide), lambda b, h, t: (b * steps + t, off * groups + h))
    vec = pl.BlockSpec((1, wide), lambda b, h, t: (0, h))
    return pl.pallas_call(
        body, name=name, grid=(n_batch, groups, steps), in_specs=[col(0), col(1), col(2), col(3), vec, vec],
        out_specs=[col(0), pl.BlockSpec((HG_DK, wide), lambda b, h, t: (b * steps + t, h))],
        out_shape=[jax.ShapeDtypeStruct((rows, n_heads * HG_DK), F32),
                   jax.ShapeDtypeStruct((n_batch * steps * HG_DK, n_heads * HG_DK), F32)],
        scratch_shapes=[pltpu.VMEM((len(heads), HG_DK, HG_DK), F32)],
        compiler_params=_params(("parallel", "parallel", "arbitrary")),
    )(proj, proj, proj, proj, lb, ng)


def _hgrn_bwd(name, proj, lb, ng, g_out, states, n_batch, lp, pad, n_heads):
    rows = proj.shape[0]
    width = n_heads * HG_DK
    groups, steps, step_rows, wide, heads = _hg_layout(lp, n_heads)
    assert groups == 1

    def body(q_ref, f_ref, i_ref, g_ref, lb_ref, ng_ref, go_ref, s_ref, gp_ref, glb_ref, gng_ref, gst_s):
        t = pl.program_id(2)

        @pl.when(t == 0)
        def _():
            gst_s[...] = jnp.zeros_like(gst_s)
            glb_ref[...] = jnp.zeros_like(glb_ref)
            gng_ref[...] = jnp.zeros_like(gng_ref)

        slices, masks, tril, load = _hg_step_views(steps - 1 - t, pad, heads)
        fn = functools.partial(_hg_step, masks=masks, tril=tril)
        _, vjp = jax.vjp(fn, load(q_ref), load(f_ref), load(i_ref), load(g_ref),
                         [lb_ref[:, cols] for _, cols in heads], [ng_ref[:, cols] for _, cols in heads],
                         [s_ref[:, cols] for _, cols in heads])
        gq, gf, gi, gg, glb, gng, gst = vjp((load(go_ref), [gst_s[h] for h, _ in heads]))
        for part, grads in enumerate((gq, gf, gi, gg)):
            for sl, row in zip(slices, grads):
                for (h, _), x in zip(heads, row):
                    lane0 = part * width + h * HG_DK
                    gp_ref[sl, lane0:lane0 + HG_DK] = x.astype(BF16)
        for (h, cols), a, b, c in zip(heads, gst, glb, gng):
            gst_s[h] = a
            glb_ref[:, cols] += b
            gng_ref[:, cols] += c

    col = lambda off: pl.BlockSpec((step_rows, wide), lambda b, h, t: (b * steps + steps - 1 - t, off * groups + h))
    vec = pl.BlockSpec((1, wide), lambda b, h, t: (0, h))
    part = pl.BlockSpec((None, 1, wide), lambda b, h, t: (b, 0, h))
    big = jax.ShapeDtypeStruct((rows, 4 * width), BF16)
    small = jax.ShapeDtypeStruct((n_batch, 1, width), F32)
    return pl.pallas_call(
        body, name=name, grid=(n_batch, groups, steps),
        in_specs=[col(0), col(1), col(2), col(3), vec, vec, col(0),
                  pl.BlockSpec((HG_DK, wide), lambda b, h, t: (b * steps + steps - 1 - t, h))],
        out_specs=[pl.BlockSpec((step_rows, 4 * width), lambda b, h, t: (b * steps + steps - 1 - t, 0)), part, part],
        out_shape=[big, small, small],
        scratch_shapes=[pltpu.VMEM((len(heads), HG_DK, HG_DK), F32)],
        compiler_params=_params(("parallel", "parallel", "arbitrary")),
    )(proj, proj, proj, proj, lb, ng, g_out, states)


def _exchange_copies(src, dst, send, recv, loc, scatter):
    x, y, c = lax.axis_index("x"), lax.axis_index("y"), lax.axis_index("c")
    me = 4 * x + 2 * y + c
    local, remote = [], []
    for w in range(len(src)):
        local.append(pltpu.make_async_copy(src[w].at[me] if scatter else src[w], dst[w].at[me], loc.at[w]))
    for k in range(1, N_DEV):
        px = 1 - x if k & 4 else x
        py = 1 - y if k & 2 else y
        pc = 1 - c if k & 1 else c
        peer = 4 * px + 2 * py + pc
        for w in range(len(src)):
            remote.append(pltpu.make_async_remote_copy(
                src_ref=src[w].at[peer] if scatter else src[w], dst_ref=dst[w].at[me],
                send_sem=send.at[w * (N_DEV - 1) + k - 1], recv_sem=recv.at[w * (N_DEV - 1) + k - 1],
                device_id=(px, py, pc), device_id_type=pl.DeviceIdType.MESH))
    return local, remote


_HBM_SPEC = pl.BlockSpec(memory_space=pltpu.HBM)
_SEM_SPEC = pl.BlockSpec(memory_space=pltpu.SEMAPHORE)
_ANY_SPEC = pl.BlockSpec(memory_space=pl.ANY)
_DATAFLOW = pltpu.SideEffectType.DATAFLOW_SIDE_EFFECTING


def _exchange_start(name, srcs, scatter, dep=None):
    nw = len(srcs)
    srcs = [pltpu.with_memory_space_constraint(s, pltpu.HBM) for s in srcs]
    lands = [pltpu.with_memory_space_constraint(lax.empty(s.shape if scatter else (N_DEV,) + s.shape, s.dtype), pltpu.HBM)
             for s in srcs]
    deps = [] if dep is None else [dep]

    def body(*refs):
        src, dst = refs[:nw], refs[nw:2 * nw]
        send, recv, loc = refs[2 * nw + len(deps):2 * nw + len(deps) + 3]
        token = refs[-1]
        local, remote = _exchange_copies(src, dst, send, recv, loc, scatter)
        for cp in local + remote:
            cp.start()
        token[...] = jnp.zeros_like(token)

    sems = [pltpu.SemaphoreType.DMA((nw * (N_DEV - 1),)), pltpu.SemaphoreType.DMA((nw * (N_DEV - 1),)),
            pltpu.SemaphoreType.DMA((nw,))]
    out = pl.pallas_call(
        body, name=name,
        out_shape=(*sems, *[pltpu.HBM(s.shape, s.dtype) for s in srcs], *[pltpu.HBM(s.shape, s.dtype) for s in lands],
                   jax.ShapeDtypeStruct((SUBLANES, LANES), F32)),
        in_specs=[_HBM_SPEC] * (2 * nw) + [_ANY_SPEC] * len(deps),
        out_specs=(_SEM_SPEC, _SEM_SPEC, _SEM_SPEC, *[_HBM_SPEC] * (2 * nw), pl.BlockSpec(memory_space=pltpu.VMEM)),
        input_output_aliases={i: 3 + i for i in range(2 * nw)},
        compiler_params=pltpu.CompilerParams(has_side_effects=_DATAFLOW),
    )(*srcs, *lands, *deps)
    return {"sems": out[:3], "srcs": out[3:3 + nw], "lands": out[3 + nw:3 + 2 * nw], "token": out[-1], "scatter": scatter}


def _exchange_wait(name, handle, after):
    nw = len(handle["srcs"])
    scatter = handle["scatter"]

    def body(*refs):
        src, dst = refs[:nw], refs[nw:2 * nw]
        send, recv, loc = refs[2 * nw:2 * nw + 3]
        local, remote = _exchange_copies(src, dst, send, recv, loc, scatter)
        for cp in local:
            cp.wait()
        for cp in remote:
            cp.wait_send()
            cp.wait_recv()

    out = pl.pallas_call(
        body, name=name,
        out_shape=(*[pltpu.HBM(s.shape, s.dtype) for s in handle["srcs"]],
                   *[pltpu.HBM(s.shape, s.dtype) for s in handle["lands"]]),
        in_specs=[_HBM_SPEC] * (2 * nw) + [_SEM_SPEC] * 3 + [_ANY_SPEC],
        out_specs=tuple([_HBM_SPEC] * (2 * nw)),
        input_output_aliases={i: i for i in range(2 * nw)},
        compiler_params=pltpu.CompilerParams(has_side_effects=_DATAFLOW),
    )(*handle["srcs"], *handle["lands"], *handle["sems"], after)
    return list(out[nw:])


def _adamw(w, g, m, v):
    m = ADAM_B1 * m + (1.0 - ADAM_B1) * g
    v = ADAM_B2 * v + (1.0 - ADAM_B2) * (g * g)
    m_hat = m / (1.0 - ADAM_B1 ** ADAM_STEP)
    v_hat = v / (1.0 - ADAM_B2 ** ADAM_STEP)
    delta = -ADAM_LR * (m_hat / (jnp.sqrt(v_hat) + ADAM_EPS) + ADAM_WD * w)
    return delta, m, v


def _adamw_summed(name, parts, w, m, v):
    layered = w.ndim == 3
    parts = list(parts) if layered else [parts]
    n_layers = len(parts)
    rows, cols = w.shape[-2:]
    n_parts = parts[0].shape[0]
    tr = _tile(rows, max(SUBLANES, (1 << 18) // cols))

    def body(*refs):
        p_refs = refs[:n_layers]
        w_ref, m_ref, v_ref, g_ref, d_ref, nm_ref, nv_ref = refs[n_layers:]
        layer = pl.program_id(0)

        def run(p_ref):
            g = p_ref[0].astype(F32)
            for s in range(1, n_parts):
                g = g + p_ref[s].astype(F32)
            d, nm, nv = _adamw(w_ref[...], g, m_ref[...], v_ref[...])
            g_ref[...] = g
            d_ref[...] = d
            nm_ref[...] = nm
            nv_ref[...] = nv

        for l in range(n_layers):
            pl.when(layer == l)(functools.partial(run, p_refs[l]))

    if layered:
        spec = pl.BlockSpec((None, tr, cols), lambda l, i: (l, i, 0))
    else:
        spec = pl.BlockSpec((tr, cols), lambda l, i: (i, 0))
    p_specs = [pl.BlockSpec((n_parts, tr, cols), lambda l, i, q=q: (0, jnp.where(l == q, i, 0), 0))
               for q in range(n_layers)]
    shp = jax.ShapeDtypeStruct(w.shape, F32)
    return pl.pallas_call(
        body, name=name, grid=(n_layers, rows // tr), in_specs=[*p_specs, spec, spec, spec],
        out_specs=[spec] * 4, out_shape=[shp] * 4, compiler_params=_params(("parallel", "parallel")),
    )(*parts, w, m, v)


def _pack_rows(arrays, cols):
    out = []
    for a in arrays:
        flat = a.reshape(-1)
        n = -(-flat.shape[0] // cols) * cols
        out.append(jnp.pad(flat, (0, n - flat.shape[0])).reshape(-1, cols))
    packed = jnp.concatenate(out, axis=0)
    return jnp.pad(packed, ((0, -packed.shape[0] % SUBLANES), (0, 0)))


def _unpack_rows(packed, shapes, cols):
    out, r = [], 0
    for s in shapes:
        n = math.prod(s)
        nr = -(-n // cols)
        out.append(packed[r:r + nr].reshape(-1)[:n].reshape(s))
        r += nr
    return out


def _block_diag(blocks):
    g, a, b = blocks.shape
    eye = jnp.eye(g, dtype=blocks.dtype)
    return (eye[:, None, :, None] * blocks[:, :, None, :]).reshape(g * a, g * b)


def _diag_blocks(dense, g):
    a, b = dense.shape[0] // g, dense.shape[1] // g
    return jnp.einsum("gagb->gab", dense.reshape(g, a, g, b))


def _local_step(x, target, meta, wts, small, late_weights, on_grads, on_small):
    n_batch, seq, d = x.shape
    n_meta = meta.shape[0]
    pad = -(seq + n_meta) % LANES
    lead = pad + n_meta
    lp = lead + seq
    rows = n_batch * lp
    s5w = wts["glu"].shape[0]
    n_ab = wts["in_ab"].shape[2]
    ab_cols = wts["in_ab"].shape[0] * n_ab
    sbw = (ab_cols - s5w) // 3
    dff = small["mlp_b_up"].shape[1]
    n_pairs = sbw // LANES
    n_hg = d // HG_DK
    s5_cb = s5w // LANES
    sb_cb = sbw // LANES
    tm = _tile(rows, ROW_TILE)
    groups, n_state, grp = small["s5_b_re"].shape[1:]
    ns = groups * n_state
    sw = min(SCAN_LANES, ns)

    h0 = jnp.concatenate(
        [jnp.zeros((n_batch, pad, d), F32), jnp.broadcast_to(meta[None], (n_batch, n_meta, d)), x], axis=1
    ).reshape(rows, d)

    lam_re, lam_im = small["s5_lam_re"][0], small["s5_lam_im"][0]
    log_dt = small["s5_log_dt"][0][:, None]
    b_re_t = small["s5_b_re"][0].transpose(0, 2, 1)
    b_im_t = small["s5_b_im"][0].transpose(0, 2, 1)
    c_re, c_im = small["s5_c_re"][0], small["s5_c_im"][0]
    lbr, lbi, bbr, bbi = _s5_params("s5_params", lam_re, lam_im, log_dt, b_re_t, b_im_t)
    b_blk = _interleave(_block_diag(bbr), _block_diag(bbi), sw).astype(BF16)
    c_blk = _interleave(_block_diag(c_re), _block_diag(-c_im), sw).T.astype(BF16)
    lam_row = _interleave(lbr.reshape(1, ns), lbi.reshape(1, ns), sw)
    d_row = small["s5_d"].reshape(1, s5w)

    def ln_store(outs, acc, res, bias, g, b):
        r = ALPHA * res + acc + bias
        outs[0][...] = r
        if len(outs) > 1:
            h = _ln(r, g, b)
            outs[1][...] = h
            outs[2][...] = h.astype(BF16)

    zero_bias = jnp.zeros((1, d), F32)

    def mix_ln(name, a, w, k_total, tk, res, bias, g, b, a_fn=None, emit_h=True):
        dtypes = (F32, F32, BF16) if emit_h else (F32,)
        return _mm_act(name, a, w, "nat", n_out_cols=d, k_total=k_total, tn=d, tk=tk, a_fn=a_fn,
                       extras=(res, bias, g, b), extra_specs=(_row_spec(tm, d), _vec_spec(d), _vec_spec(d), _vec_spec(d)),
                       store=ln_store, out_shape=[jax.ShapeDtypeStruct((rows, d), t) for t in dtypes],
                       out_specs=[_row_spec(tm, d)] * len(dtypes))

    def two(width):
        return [jax.ShapeDtypeStruct((rows, width), F32)] * 2, [_row_spec(tm, width)] * 2

    def shard_tile(total, shard, cap=1024):
        t = max(shard, cap - cap % shard)
        while total % t:
            t -= shard
        return t

    h0b = h0.astype(BF16)
    proj_ab = _mm_act("in_ab", h0b, wts["in_ab"], "stk", n_out_cols=ab_cols, k_total=d, tn=shard_tile(ab_cols, n_ab), tk=d)[0]
    bu = _mm_act("s5_bu", proj_ab, b_blk, "nat", n_out_cols=2 * ns, k_total=s5w, tn=min(2 * ns, 2048), tk=s5w)[0]
    states = _s5_scan("s5_scan", bu, lam_row, n_batch, lp, sw)

    def gelu_store(outs, acc, u, dv):
        ypre = acc + dv * u
        outs[0][...] = ypre
        outs[1][...] = jax.nn.gelu(ypre)

    shp2, spec2 = two(s5w)
    ypre, y = _mm_act(
        "s5_y", states, c_blk, "nat", n_out_cols=s5w, k_total=2 * ns, tn=s5w, tk=min(2 * ns, 1024),
        extras=(proj_ab, d_row), extra_specs=(_row_spec(tm, s5w), _vec_spec(s5w)), store=gelu_store,
        out_shape=shp2, out_specs=spec2)

    def glu_store(outs, acc, yv, bias):
        gate = acc + bias
        outs[0][...] = gate
        outs[1][...] = _glu(yv, gate)

    gate, a_out = _mm_act(
        "s5_glu", y, wts["glu"], "nat", n_out_cols=s5w, k_total=s5w, tn=s5w, tk=s5w,
        extras=(y, small["s5_b_glu"]), extra_specs=(_row_spec(tm, s5w), _vec_spec(s5w)), store=glu_store,
        out_shape=shp2, out_specs=spec2)
    b_out = _attn_fwd("sb_attn", proj_ab, n_batch, lp, pad, s5_cb, s5_cb + sb_cb, s5_cb + 2 * sb_cb, n_pairs)

    def bias_store(outs, acc, bias):
        outs[0][...] = (acc + bias).astype(outs[0].dtype)

    def wide(width, dtype):
        return [jax.ShapeDtypeStruct((rows, dff), dtype)], [_row_spec(tm, width)]

    def mlp_fwd(layer, h_in, h_in_b, emit_h=True):
        tn = shard_tile(dff, n_up)
        shp, spec = wide(tn, BF16)
        up = _mm_act(f"up{layer}", h_in_b, wts["up"][layer], "stk", n_out_cols=dff, k_total=d, tn=tn, tk=d,
                     extras=(small["mlp_b_up"][layer:layer + 1],), extra_specs=(_vec_spec(tn),), store=bias_store,
                     out_shape=shp, out_specs=spec)[0]
        return (up, *mix_ln(f"down{layer}", up, wts["down"][layer], dff, min(dff, 1024), h_in,
                            small["mlp_b_down"][layer:layer + 1], small["ln_mlp_g"][layer:layer + 1],
                            small["ln_mlp_b"][layer:layer + 1], a_fn=_relu2, emit_h=emit_h))

    r1, h1, h1b = mix_ln("out_ab", [a_out, b_out], wts["out_ab"], s5w + sbw, min(s5w, sbw), h0, zero_bias,
                         small["ln_mix_g"][0:1], small["ln_mix_b"][0:1])
    wts = {**wts, **late_weights(r1)}
    n_c = wts["in_c"].shape[2]
    n_up = wts["up"][0].shape[2]
    up0, r2, h2, h2b = mlp_fwd(0, h1, h1b)

    lb = _lower_bound("hg_lb", small["hgrn_gamma"])
    proj_c = _mm_act("in_c", h2b, wts["in_c"], "stk", n_out_cols=4 * d, k_total=d, tn=shard_tile(4 * d, n_c), tk=d)[0]
    c_out, hg_states = _hgrn_fwd("hgrn", proj_c, lb, wts["ng"], n_batch, lp, pad, n_hg)
    r3, h3, h3b = mix_ln("out_c", c_out, wts["out_c"], d, d, h2, zero_bias, small["ln_mix_g"][1:2], small["ln_mix_b"][1:2])
    up1, r4 = mlp_fwd(1, h3, h3b, emit_h=False)

    gr = {}
    g_r4, gr["ln_mlp_g1"], gr["ln_mlp_b1"], loss_tile = _loss_grad(
        "loss", r4, small["ln_mlp_g"][1:2], small["ln_mlp_b"][1:2], target, n_batch, lp, lead)

    def res_store(outs, acc, g_res):
        outs[0][...] = acc + ALPHA * g_res

    def ln_bwd_store(outs, acc, g_res, r_in, g, b, first_step):
        gr_in, gg, gb = jax.vjp(_ln, r_in, g, b)[1](acc + ALPHA * g_res)
        outs[0][...] = gr_in

        @pl.when(first_step)
        def _():
            outs[1][...] = jnp.zeros_like(outs[1])
            outs[2][...] = jnp.zeros_like(outs[2])

        outs[1][...] += gg
        outs[2][...] += gb

    def through_ln(name, a, w, k_total, tk, g_res, r_in, g, b, dep=None):
        vec = pl.BlockSpec((1, d), lambda i, j, k: (0, 0))
        return _mm_act(name, a, w, "stkT", n_out_cols=d, k_total=k_total, tn=d, tk=tk,
                       extras=(g_res, r_in, g, b), extra_specs=(_row_spec(tm, d), _row_spec(tm, d), vec, vec),
                       store=ln_bwd_store, sequential=True, dep=dep,
                       out_shape=[jax.ShapeDtypeStruct((rows, d), F32)] + [jax.ShapeDtypeStruct((1, d), F32)] * 2,
                       out_specs=[_row_spec(tm, d), vec, vec])

    def mlp_bwd(layer, g_r, up, h_in, r_in, send=None):
        def gup_store(outs, acc, upv):
            outs[0][...] = (acc * (2.0 * jnp.maximum(upv.astype(F32), 0.0))).astype(outs[0].dtype)

        tf = min(dff, 1024)
        shp, spec = wide(tf, BF16)
        g_up = _mm_act(f"g_up{layer}", g_r, wts["down"][layer], "natT", n_out_cols=dff, k_total=d, tn=tf, tk=d,
                       extras=(up,), extra_specs=(_row_spec(tm, tf),), store=gup_store, out_shape=shp, out_specs=spec)[0]
        gr[f"down{layer}"], gr[f"mlp_b_down{layer}"] = _mm_wgrad(
            f"dw_down{layer}", up, g_r, kw=dff, n=d, tmw=tf, tn=d, a_fn=_relu2, out_dtype=BF16, colsum=True)
        gr[f"up{layer}"], gr[f"mlp_b_up{layer}"] = _mm_wgrad(
            f"dw_up{layer}", h_in, g_up, kw=d, n=dff, tmw=d, tn=min(dff, 2048), shard_cols=n_up, out_dtype=BF16, colsum=True)
        dep = send() if send is not None else None
        g_r_in, gr[f"ln_mix_g{layer}"], gr[f"ln_mix_b{layer}"] = through_ln(
            f"g_hmid{layer}", g_up, wts["up"][layer], dff, shard_tile(dff, n_up), g_r, r_in,
            small["ln_mix_g"][layer:layer + 1], small["ln_mix_b"][layer:layer + 1], dep=dep)
        return g_r_in

    g_r3 = mlp_bwd(1, g_r4, up1, h3b, r3)
    g_cout = _mm_act("g_cout", g_r3, wts["out_c"], "natT", n_out_cols=d, k_total=d, tn=d, tk=d)[0]
    gr["out_c"] = _mm_wgrad("dw_out_c", c_out, g_r3, kw=d, n=d, tmw=d, tn=d, out_dtype=BF16)
    g_pc, g_lb_parts, g_ng_parts = _hgrn_bwd("hgrn_bwd", proj_c, lb, wts["ng"], g_cout, hg_states, n_batch, lp, pad, n_hg)
    gr["hgrn_gamma"], gr["ng"] = _lower_bound_bwd("hg_lb_bwd", small["hgrn_gamma"], g_lb_parts, g_ng_parts)
    gr["in_c"] = _mm_wgrad("dw_in_c", h2b, g_pc, kw=d, n=4 * d, tmw=d, tn=min(4 * d, 2048), shard_cols=n_c, out_dtype=BF16)
    sent1 = on_grads(1, {"down1": gr["down1"], "up1": gr["up1"], "out_c": gr["out_c"], "in_c": gr["in_c"], "ng": gr["ng"]})
    g_r2, gr["ln_mlp_g0"], gr["ln_mlp_b0"] = through_ln(
        "g_h2", g_pc, wts["in_c"], 4 * d, shard_tile(4 * d, n_c), g_r3, r2, small["ln_mlp_g"][0:1], small["ln_mlp_b"][0:1],
        dep=sent1)

    g_r1 = mlp_bwd(0, g_r2, up0, h1b, r1, send=lambda: on_grads(2, {"down0": gr["down0"], "up0": gr["up0"]}))
    g_cat = _mm_act("g_cat", g_r1, wts["out_ab"], "natT", n_out_cols=d, k_total=d, tn=d, tk=d)[0]
    gr["out_ab"] = _mm_wgrad("dw_out_ab", [a_out, b_out], g_r1, kw=s5w + sbw, n=d, tmw=min(s5w, sbw), tn=d, out_dtype=BF16)
    g_q, g_k, g_v = _attn_bwd("sb_attn_bwd", proj_ab, g_cat, n_batch, lp, pad, s5_cb, s5_cb + sb_cb, s5_cb + 2 * sb_cb,
                              s5_cb, n_pairs)

    g_y_direct, g_gate = _rowwise("s5_glu_bwd", lambda ga, yv, gt: jax.vjp(_glu, yv, gt)[1](ga),
                                  [(g_cat, 0, s5w), (y, 0, s5w), (gate, 0, s5w)], 2, s5w)

    def gelu_bwd_store(outs, acc, gyd, yp, u, dv):
        gyp = jax.vjp(jax.nn.gelu, yp)[1](acc + gyd)[0]
        outs[0][...] = gyp
        outs[1][...] = dv * gyp
        outs[2][...] = jnp.sum(gyp * u, axis=0, keepdims=True)

    rs = _row_spec(tm, s5w)
    g_ypre, g_u_direct, gd_parts = _mm_act(
        "s5_g_y", g_gate, wts["glu"], "natT", n_out_cols=s5w, k_total=s5w, tn=s5w, tk=s5w,
        extras=(g_y_direct, ypre, proj_ab, d_row), extra_specs=(rs, rs, rs, _vec_spec(s5w)), store=gelu_bwd_store,
        out_shape=[jax.ShapeDtypeStruct((rows, s5w), F32)] * 2 + [jax.ShapeDtypeStruct((rows // tm, 1, s5w), F32)],
        out_specs=[rs, rs, pl.BlockSpec((None, 1, s5w), lambda i, j, k: (i, 0, j))])
    gr["glu"], gr["s5_b_glu"] = _mm_wgrad("dw_glu", y, g_gate, kw=s5w, n=s5w, tmw=s5w, tn=s5w, out_dtype=BF16, colsum=True)
    g_sd = _mm_act("s5_g_states", g_ypre, c_blk, "natT", n_out_cols=2 * ns, k_total=s5w, tn=min(2 * ns, 2048), tk=s5w)[0]
    d_cblk = _mm_wgrad("dw_cblk", states, g_ypre, kw=2 * ns, n=s5w, tmw=min(2 * ns, 1024), tn=s5w)
    gs, gl_parts = _s5_scan_bwd("s5_scan_bwd", g_sd, states, lam_row, n_batch, lp, sw)

    def add_store(outs, acc, other):
        outs[0][...] = acc + other

    g_u = _mm_act("s5_g_u", gs, b_blk, "natT", n_out_cols=s5w, k_total=2 * ns, tn=s5w, tk=min(2 * ns, 1024),
                  extras=(g_u_direct,), extra_specs=(rs,), store=add_store)[0]
    d_bblk = _mm_wgrad("dw_bblk", proj_ab, gs, kw=s5w, n=2 * ns, tmw=s5w, tn=min(2 * ns, 2048))
    db_re, db_im = _deinterleave(d_bblk, sw)
    dc_re, dc_im = _deinterleave(d_cblk.T, sw)
    glr, gli = _deinterleave(gl_parts, sw)
    g_lam_re, g_lam_im, g_log_dt, g_b_re_t, g_b_im_t, g_d = _s5_params_bwd(
        "s5_params_bwd", lam_re, lam_im, log_dt, b_re_t, b_im_t,
        glr.reshape(n_batch, groups, n_state), gli.reshape(n_batch, groups, n_state),
        _diag_blocks(db_re, groups), _diag_blocks(db_im, groups), gd_parts)

    cat2 = lambda key: jnp.concatenate([gr[key + "0"], gr[key + "1"]], axis=0)
    small_sent = on_small({
        "s5_lam_re": g_lam_re[None], "s5_lam_im": g_lam_im[None], "s5_log_dt": g_log_dt.reshape(1, groups),
        "s5_b_re": g_b_re_t.transpose(0, 2, 1)[None], "s5_b_im": g_b_im_t.transpose(0, 2, 1)[None],
        "s5_c_re": _diag_blocks(dc_re, groups)[None], "s5_c_im": -_diag_blocks(dc_im, groups)[None],
        "s5_d": g_d.reshape(1, groups, grp), "s5_b_glu": gr["s5_b_glu"], "hgrn_gamma": gr["hgrn_gamma"],
        "ln_mix_g": cat2("ln_mix_g"), "ln_mix_b": cat2("ln_mix_b"), "mlp_b_up": cat2("mlp_b_up"),
        "mlp_b_down": cat2("mlp_b_down"), "ln_mlp_g": cat2("ln_mlp_g"), "ln_mlp_b": cat2("ln_mlp_b"),
    }, loss_tile)

    g_pab = [g_u, g_q, g_k, g_v]
    assert s5w == sbw
    gr["in_ab"] = _mm_wgrad("dw_in_ab", h0b, g_pab, kw=d, n=ab_cols, tmw=d, tn=s5w, shard_cols=n_ab, out_dtype=BF16,
                            dep=small_sent)
    g_h0 = _mm_act("g_h0", g_pab, wts["in_ab"], "stkT", n_out_cols=d, k_total=ab_cols, tn=d, tk=shard_tile(s5w, n_ab),
                   extras=(g_r1,), extra_specs=(_row_spec(tm, d),), store=res_store)[0]
    grad_x = g_h0.reshape(n_batch, lp, d)[:, lead:, :]
    g_meta = _meta_grad("g_meta", g_h0, n_batch, lp, pad, n_meta)
    on_grads(3, {"meta": g_meta, "in_ab": gr["in_ab"], "glu": gr["glu"], "out_ab": gr["out_ab"]})
    return grad_x


SMALL_NAMES = ("s5_lam_re", "s5_lam_im", "s5_log_dt", "s5_b_re", "s5_b_im", "s5_c_re", "s5_c_im", "s5_d", "s5_b_glu",
               "hgrn_gamma", "ln_mix_g", "ln_mix_b", "mlp_b_up", "mlp_b_down", "ln_mlp_g", "ln_mlp_b")
WEIGHT_ORDER = ("meta", "w_in_ab", "s5_lam_re", "s5_lam_im", "s5_log_dt", "s5_b_re", "s5_b_im", "s5_c_re", "s5_c_im",
                "s5_d", "s5_w_glu", "s5_b_glu", "w_out_ab", "w_in_c", "hgrn_gamma", "hgrn_norm_g", "w_out_c", "ln_mix_g",
                "ln_mix_b", "mlp_w_up", "mlp_b_up", "mlp_w_down", "mlp_b_down", "ln_mlp_g", "ln_mlp_b")


def kernel(x, meta, w_in_ab, s5_lam_re, s5_lam_im, s5_log_dt, s5_b_re, s5_b_im, s5_c_re, s5_c_im, s5_d, s5_w_glu, s5_b_glu, w_out_ab, w_in_c, hgrn_gamma, hgrn_norm_g, w_out_c, ln_mix_g, ln_mix_b, mlp_w_up, mlp_b_up, mlp_w_down, mlp_b_down, ln_mlp_g, ln_mlp_b, loss_target, m_meta, m_w_in_ab, m_s5_lam_re, m_s5_lam_im, m_s5_log_dt, m_s5_b_re, m_s5_b_im, m_s5_c_re, m_s5_c_im, m_s5_d, m_s5_w_glu, m_s5_b_glu, m_w_out_ab, m_w_in_c, m_hgrn_gamma, m_hgrn_norm_g, m_w_out_c, m_ln_mix_g, m_ln_mix_b, m_mlp_w_up, m_mlp_b_up, m_mlp_w_down, m_mlp_b_down, m_ln_mlp_g, m_ln_mlp_b, v_meta, v_w_in_ab, v_s5_lam_re, v_s5_lam_im, v_s5_log_dt, v_s5_b_re, v_s5_b_im, v_s5_c_re, v_s5_c_im, v_s5_d, v_s5_w_glu, v_s5_b_glu, v_w_out_ab, v_w_in_c, v_hgrn_gamma, v_hgrn_norm_g, v_w_out_c, v_ln_mix_g, v_ln_mix_b, v_mlp_w_up, v_mlp_b_up, v_mlp_w_down, v_mlp_b_down, v_ln_mlp_g, v_ln_mlp_b):
    args = dict(locals())
    w = {n: args[n] for n in WEIGHT_ORDER}
    mom = {n: args["m_" + n] for n in WEIGHT_ORDER}
    var = {n: args["v_" + n] for n in WEIGHT_ORDER}
    d = x.shape[2]
    n_meta = meta.shape[0]

    cast = lambda a: a.astype(BF16)
    early = _exchange_start("gather_early_start", [w["meta"], cast(w["w_in_ab"][0]), cast(w["s5_w_glu"][0]),
                                                   cast(w["w_out_ab"][0])], False)
    late = _exchange_start("gather_late_start", [w["hgrn_norm_g"], cast(w["w_in_c"][0]), cast(w["w_out_c"][0]),
                                                 cast(w["mlp_w_up"][0]), cast(w["mlp_w_up"][1]),
                                                 cast(w["mlp_w_down"][0]), cast(w["mlp_w_down"][1])], False, dep=early["token"])
    a_meta, a_in_ab, a_glu, a_out_ab = _exchange_wait("gather_early_wait", early, late["token"])
    wts = {"in_ab": a_in_ab, "glu": a_glu.reshape(-1, a_glu.shape[2]), "out_ab": a_out_ab.reshape(-1, d)}
    meta_full = a_meta.transpose(1, 0, 2).reshape(n_meta, d)
    small = {n: w[n] for n in SMALL_NAMES}

    def late_weights(after):
        a_ng, a_in_c, a_out_c, a_up0, a_up1, a_dn0, a_dn1 = _exchange_wait("gather_late_wait", late, after)
        return {"in_c": a_in_c, "ng": a_ng.transpose(1, 0, 2).reshape(1, d), "out_c": a_out_c.reshape(-1, d),
                "up": [a_up0, a_up1], "down": [a_dn0.reshape(-1, d), a_dn1.reshape(-1, d)]}

    n_loc = d // N_DEV
    rows_of = lambda g: g.reshape(N_DEV, -1, g.shape[-1])
    cols_of = lambda g: g.reshape(g.shape[0], N_DEV, n_loc).transpose(1, 0, 2)
    sent = {}

    def on_grads(stage, g):
        if stage == 1:
            order = (("mlp_w_down", 1), ("mlp_w_up", 1), ("w_out_c", 0), ("w_in_c", 0), ("hgrn_norm_g", None))
            parts = [rows_of(g["down1"]), g["up1"], rows_of(g["out_c"]), g["in_c"], cols_of(g["ng"])]
        elif stage == 2:
            order = (("mlp_w_down", 0), ("mlp_w_up", 0))
            parts = [rows_of(g["down0"]), g["up0"]]
        else:
            order = (("w_out_ab", 0), ("s5_w_glu", 0), ("w_in_ab", 0), ("meta", None))
            parts = [rows_of(g["out_ab"]), rows_of(g["glu"]), g["in_ab"], cols_of(g["meta"])]
        sent[stage] = (order, _exchange_start(f"scatter_start{stage}", parts, True))
        return sent[stage][1]["token"]

    def on_small(sg, loss_tile):
        g_pack = _pack_rows([sg[n] for n in SMALL_NAMES] + [loss_tile], PACK_COLS)
        sent["small"] = _exchange_start("gather_small_start", [g_pack], False)
        return sent["small"]["token"]

    grad_x = _local_step(x, loss_target, meta_full, wts, small, late_weights, on_grads, on_small)
    small_sent = sent["small"]
    tile = (SUBLANES, LANES)
    shapes = [w[n].shape for n in SMALL_NAMES] + [tile]
    zeros = jnp.zeros(tile, F32)
    w_pack = _pack_rows([w[n] for n in SMALL_NAMES] + [zeros], PACK_COLS)
    m_pack = _pack_rows([mom[n] for n in SMALL_NAMES] + [zeros], PACK_COLS)
    v_pack = _pack_rows([var[n] for n in SMALL_NAMES] + [zeros], PACK_COLS)

    received, res = {}, {}

    def wait(stage, after):
        order, handle = sent[stage]
        for key, rc in zip(order, _exchange_wait(f"scatter_wait{stage}", handle, after)):
            received[key] = rc

    def update(nm):
        layered = w[nm].ndim == 3
        parts = [received[(nm, l)] for l in range(w[nm].shape[0])] if layered else received[(nm, None)]
        res[nm] = _adamw_summed(f"adamw_{nm}", parts, w[nm], mom[nm], var[nm])
        return res[nm][0]

    wait(1, sent[3][1]["token"])
    done = [update(nm) for nm in ("w_out_c", "w_in_c", "hgrn_norm_g")]
    wait(2, done[0])
    done = [update(nm) for nm in ("mlp_w_up", "mlp_w_down")]
    g_all = _exchange_wait("gather_small_wait", small_sent, done[0])[0]
    packed = _adamw_summed("adamw_small", g_all, w_pack, m_pack, v_pack)
    wait(3, packed[0])
    for nm in ("w_out_ab", "s5_w_glu", "w_in_ab", "meta"):
        update(nm)
    unpacked = [_unpack_rows(p, shapes, PACK_COLS) for p in packed]
    loss = unpacked[0][-1][0, 0]

    def pick(nm, which):
        return unpacked[which][SMALL_NAMES.index(nm)] if nm in SMALL_NAMES else res[nm][which]

    return (loss, grad_x, *[pick(n, 0) for n in WEIGHT_ORDER], *[pick(n, 1) for n in WEIGHT_ORDER],
            *[pick(n, 2) for n in WEIGHT_ORDER], *[pick(n, 3) for n in WEIGHT_ORDER])
```

```python
import functools
import math

import jax
import jax.numpy as jnp
from jax import lax
from jax.experimental import pallas as pl
from jax.experimental.pallas import tpu as pltpu

F32 = jnp.float32
BF16 = jnp.bfloat16

N_DEV = 8
DEPTH = 2
ALPHA = (2.0 * DEPTH) ** 0.25
LN_EPS = 1e-5
RMS_EPS = 1e-6
SB_HEAD_DIM = 64
HG_DK = 128
HG_CHUNK = 64
LANES = 128
SUBLANES = 8
PACKED_ROWS = 16
VMEM_LIMIT_BYTES = 56 * 1024 * 1024
ROW_TILE = 1088
SCAN_LANES = 256
SCAN_UNROLL = 2
PACK_COLS = 1024

ADAM_LR = 0.001
ADAM_B1 = 0.9
ADAM_B2 = 0.999
ADAM_EPS = 1e-08
ADAM_WD = 0.01
ADAM_STEP = 10

NN = (((1,), (0,)), ((), ()))
NT = (((1,), (1,)), ((), ()))
TN = (((0,), (0,)), ((), ()))


def _tile(n, pref, align=SUBLANES):
    t = min(n, pref)
    t -= t % align
    while t >= align:
        if n % t == 0:
            return t
        t -= align
    return n


def _unrolled_loop(n, body, init, unroll):
    assert n % unroll == 0

    def outer(t, carry):
        for u in range(unroll):
            carry = body(t * unroll + u, carry)
        return carry

    return lax.fori_loop(0, n // unroll, outer, init)


def _params(sem):
    return pltpu.CompilerParams(dimension_semantics=sem, vmem_limit_bytes=VMEM_LIMIT_BYTES)


def _dot_raw(a, b, dims):
    return lax.dot_general(a.astype(BF16), b.astype(BF16), dims, preferred_element_type=F32)


def _make_dot(dims, da_rule, db_rule):
    @jax.custom_vjp
    def f(a, b):
        return _dot_raw(a, b, dims)

    def fwd(a, b):
        return _dot_raw(a, b, dims), (a, b)

    def bwd(res, g):
        a, b = res
        return da_rule(g, a, b), db_rule(g, a, b)

    f.defvjp(fwd, bwd)
    return f


_DOTS = {
    NN: _make_dot(NN, lambda g, a, b: _dot_raw(g, b, NT), lambda g, a, b: _dot_raw(a, g, TN)),
    NT: _make_dot(NT, lambda g, a, b: _dot_raw(g, b, NN), lambda g, a, b: _dot_raw(g, a, TN)),
    TN: _make_dot(TN, lambda g, a, b: _dot_raw(b, g, NT), lambda g, a, b: _dot_raw(a, g, NN)),
}


def _dot(a, b, dims):
    return _DOTS[dims](a, b)


def _running_sums(a, tri_ones, split=False):
    hi = a.astype(BF16)
    out = lax.dot_general(hi, tri_ones, NN, preferred_element_type=F32)
    if split:
        lo = (a - hi.astype(F32)).astype(BF16)
        out = out + lax.dot_general(lo, tri_ones, NN, preferred_element_type=F32)
    return out


def _piece_specs(pieces, block_rows, block_cols, row_of, col_of, cb0):
    per = pieces[0].shape[1] // block_cols if len(pieces) > 1 else None
    specs = []
    for p in range(len(pieces)):
        if per is None:
            specs.append(pl.BlockSpec((block_rows, block_cols), lambda *g: (row_of(*g), cb0 + col_of(*g))))
        else:
            specs.append(pl.BlockSpec(
                (block_rows, block_cols),
                lambda *g, p=p: (row_of(*g), jnp.clip(col_of(*g) - p * per, 0, per - 1))))
    return specs, per


def _mm_call(name, grid, dims, a_pieces, a_specs, a_sel, b_pieces, b_specs, b_sel, extras, extra_specs,
             out_shape, out_specs, acc_shape, a_fn, store, colsum_width=0, sequential=False, deps=()):
    na, nb, ne, no, nd = len(a_pieces), len(b_pieces), len(extras), len(out_shape), len(deps)
    nk = grid[2]

    def body(*refs):
        a_refs, b_refs = refs[:na], refs[na:na + nb]
        extra = refs[na + nb:na + nb + ne]
        outs = refs[na + nb + ne + nd:na + nb + ne + nd + no]
        acc = refs[na + nb + ne + nd + no]
        ids = (pl.program_id(0), pl.program_id(1), pl.program_id(2))
        k = ids[2]

        @pl.when(k == 0)
        def _():
            acc[...] = jnp.zeros_like(acc)

        def run(a_ref, b_ref):
            a = a_ref[...]
            if a_fn is not None:
                a = a_fn(a)
            b = b_ref[...]
            if b.ndim == 3 and dims == NN:
                n = b.shape[2]
                for q in range(b.shape[0]):
                    acc[:, q * n:(q + 1) * n] += _dot_raw(a, b[q], dims)
            elif b.ndim == 3:
                n = b.shape[2]
                for q in range(b.shape[0]):
                    acc[...] += _dot_raw(a[:, q * n:(q + 1) * n], b[q], dims)
            else:
                acc[...] += _dot_raw(a, b, dims)
            if colsum_width:
                cs = refs[-1]
                first = ids[1] == 0

                @pl.when(first & (k == 0))
                def _():
                    cs[...] = jnp.zeros_like(cs)

                @pl.when(first)
                def _():
                    cs[...] += jnp.sum(b.astype(F32), axis=0, keepdims=True)

        if na == 1 and nb == 1:
            run(a_refs[0], b_refs[0])
        elif nb == 1:
            per, fn = a_sel
            which = fn(*ids) // per
            for p in range(na):
                pl.when(which == p)(functools.partial(run, a_refs[p], b_refs[0]))
        else:
            assert na == 1
            per, fn = b_sel
            which = fn(*ids) // per
            for p in range(nb):
                pl.when(which == p)(functools.partial(run, a_refs[0], b_refs[p]))

        @pl.when(k == nk - 1)
        def _():
            if sequential:
                store(outs, acc[...], *[e[...] for e in extra], first_step=(ids[0] == 0) & (ids[1] == 0))
            else:
                store(outs, acc[...], *[e[...] for e in extra])
            if colsum_width:
                @pl.when(ids[1] == 0)
                def _():
                    outs[-1][...] = refs[-1][...]

    scratch = [pltpu.VMEM(acc_shape, F32)]
    if colsum_width:
        scratch.append(pltpu.VMEM((1, colsum_width), F32))
    sem = ("parallel", "arbitrary", "arbitrary") if colsum_width else ("parallel", "parallel", "arbitrary")
    if sequential:
        sem = ("arbitrary",) * 3
    return pl.pallas_call(
        body, name=name, grid=grid,
        in_specs=[*a_specs, *b_specs, *extra_specs, *[pl.BlockSpec(memory_space=pl.ANY)] * nd], out_specs=out_specs,
        out_shape=out_shape, scratch_shapes=scratch, compiler_params=_params(sem),
    )(*a_pieces, *b_pieces, *extras, *deps)


def _store_plain(outs, acc):
    outs[0][...] = acc.astype(outs[0].dtype)


def _row_spec(tm, tn):
    return pl.BlockSpec((tm, tn), lambda i, j, k: (i, j))


def _vec_spec(tn):
    return pl.BlockSpec((1, tn), lambda i, j, k: (0, j))


def _mm_act(name, a, w, wkind, *, n_out_cols, k_total, tn, tk, a_cb0=0, a_fn=None, extras=(), extra_specs=(),
            store=_store_plain, out_shape=None, out_specs=None, sequential=False, dep=None):
    a_pieces = list(a) if isinstance(a, (list, tuple)) else [a]
    rows = a_pieces[0].shape[0]
    tm = _tile(rows, ROW_TILE)
    grid = (rows // tm, n_out_cols // tn, k_total // tk)
    a_specs, per = _piece_specs(a_pieces, tm, tk, lambda i, j, k: i, lambda i, j, k: k, a_cb0)
    if wkind == "nat":
        b_spec, dims = pl.BlockSpec((tk, tn), lambda i, j, k: (k, j)), NN
    elif wkind == "stk":
        n = w.shape[2]
        assert tn % n == 0
        b_spec, dims = pl.BlockSpec((tn // n, tk, n), lambda i, j, k: (j, k, 0)), NN
    elif wkind == "natT":
        b_spec, dims = pl.BlockSpec((tn, tk), lambda i, j, k: (j, k)), NT
    else:
        n = w.shape[2]
        assert wkind == "stkT" and tk % n == 0
        b_spec, dims = pl.BlockSpec((tk // n, tn, n), lambda i, j, k: (k, j, 0)), NT
    if out_shape is None:
        out_shape = [jax.ShapeDtypeStruct((rows, n_out_cols), F32)]
        out_specs = [_row_spec(tm, tn)]
    return _mm_call(name, grid, dims, a_pieces, a_specs, (per, lambda i, j, k: k), [w], [b_spec], None,
                    list(extras), list(extra_specs), out_shape, out_specs, (tm, tn), a_fn, store,
                    sequential=sequential, deps=() if dep is None else (dep,))


def _mm_wgrad(name, a, g, *, kw, n, tmw, tn, a_cb0=0, a_fn=None, shard_cols=0, out_dtype=F32, colsum=False, dep=None):
    a_pieces = list(a) if isinstance(a, (list, tuple)) else [a]
    g_pieces = list(g) if isinstance(g, (list, tuple)) else [g]
    rows = a_pieces[0].shape[0]
    tr = _tile(rows, ROW_TILE)
    grid = (n // tn, kw // tmw, rows // tr)
    a_specs, a_per = _piece_specs(a_pieces, tr, tmw, lambda j, i, k: k, lambda j, i, k: i, a_cb0)
    g_specs, g_per = _piece_specs(g_pieces, tr, tn, lambda j, i, k: k, lambda j, i, k: j, 0)
    if shard_cols:
        per = tn // shard_cols
        out_shape = [jax.ShapeDtypeStruct((n // shard_cols, kw, shard_cols), out_dtype)]
        out_specs = [pl.BlockSpec((per, tmw, shard_cols), lambda j, i, k: (j, i, 0))]

        def store(outs, acc):
            for q in range(per):
                outs[0][q] = acc[:, q * shard_cols:(q + 1) * shard_cols].astype(out_dtype)
    else:
        out_shape = [jax.ShapeDtypeStruct((kw, n), out_dtype)]
        out_specs = [pl.BlockSpec((tmw, tn), lambda j, i, k: (i, j))]

        def store(outs, acc):
            outs[0][...] = acc.astype(out_dtype)
    if colsum:
        out_shape.append(jax.ShapeDtypeStruct((1, n), F32))
        out_specs.append(pl.BlockSpec((1, tn), lambda j, i, k: (0, j)))
    res = _mm_call(name, grid, TN, a_pieces, a_specs, (a_per, lambda j, i, k: i), g_pieces, g_specs,
                   (g_per, lambda j, i, k: j), [], [], out_shape, out_specs, (tmw, tn), a_fn, store,
                   colsum_width=tn if colsum else 0, deps=() if dep is None else (dep,))
    return res if colsum else res[0]


def _ln(x, g, b):
    mu = jnp.mean(x, axis=-1, keepdims=True)
    xc = x - mu
    var = jnp.mean(xc * xc, axis=-1, keepdims=True)
    return xc * lax.rsqrt(var + LN_EPS) * g + b


def _relu2(x):
    r = jnp.maximum(x.astype(F32), 0.0)
    return r * r


def _glu(y, gate):
    return y * jax.nn.sigmoid(gate)


def _rowwise(name, fn, ins, n_out, width):
    rows = ins[0][0].shape[0]
    tm = _tile(rows, ROW_TILE)

    def body(*refs):
        res = fn(*[r[...] for r in refs[:len(ins)]])
        for o, v in zip(refs[len(ins):], res):
            o[...] = v

    return pl.pallas_call(
        body, name=name, grid=(rows // tm,),
        in_specs=[pl.BlockSpec((tm, wd), lambda i, cb=cb: (i, cb)) for _, cb, wd in ins],
        out_specs=[pl.BlockSpec((tm, width), lambda i: (i, 0))] * n_out,
        out_shape=[jax.ShapeDtypeStruct((rows, width), F32)] * n_out, compiler_params=_params(("parallel",)),
    )(*[a for a, _, _ in ins])


def _loss_grad(name, r, g, b, target, n_batch, lp, lead):
    rows, d = r.shape
    nq = lp // LANES
    lead_blocks = lead // LANES

    def body(r_ref, g_ref, b_ref, t_ref, gr_ref, gg_ref, gb_ref, loss_ref, grb_ref):
        i = pl.program_id(1)

        @pl.when((pl.program_id(0) == 0) & (i == 0))
        def _():
            loss_ref[...] = jnp.zeros_like(loss_ref)
            gg_ref[...] = jnp.zeros_like(gg_ref)
            gb_ref[...] = jnp.zeros_like(gb_ref)

        h, vjp = jax.vjp(_ln, r_ref[...], g_ref[...], b_ref[...])
        diff = jnp.where(i >= lead_blocks, h - t_ref[...], 0.0)
        gr, gg, gb = vjp(diff * (1.0 / d))
        gr_ref[...] = gr
        grb_ref[...] = gr.astype(BF16)
        gg_ref[...] += gg
        gb_ref[...] += gb
        loss_ref[...] += 0.5 * jnp.sum(diff * diff) * (1.0 / d)

    vec = pl.BlockSpec((1, d), lambda b, i: (0, 0))
    row = pl.BlockSpec((LANES, d), lambda b, i: (b * nq + i, 0))
    return pl.pallas_call(
        body, name=name, grid=(n_batch, nq),
        in_specs=[row, vec, vec, pl.BlockSpec((None, LANES, d), lambda b, i: (b, jnp.maximum(i - lead_blocks, 0), 0))],
        out_specs=[row, vec, vec, pl.BlockSpec((SUBLANES, LANES), lambda b, i: (0, 0)), row],
        out_shape=[jax.ShapeDtypeStruct((rows, d), F32), jax.ShapeDtypeStruct((1, d), F32),
                   jax.ShapeDtypeStruct((1, d), F32), jax.ShapeDtypeStruct((SUBLANES, LANES), F32),
                   jax.ShapeDtypeStruct((rows, d), BF16)],
        compiler_params=_params(("arbitrary", "arbitrary")),
    )(r, g, b, target)


def _meta_grad(name, g_h0, n_batch, lp, pad, n_meta):
    d = g_h0.shape[1]
    per = lp // n_meta
    at = pad // n_meta

    def body(g_ref, o_ref):
        @pl.when(pl.program_id(0) == 0)
        def _():
            o_ref[...] = jnp.zeros_like(o_ref)

        o_ref[...] += g_ref[...]

    return pl.pallas_call(
        body, name=name, grid=(n_batch,),
        in_specs=[pl.BlockSpec((n_meta, d), lambda b: (b * per + at, 0))],
        out_specs=pl.BlockSpec((n_meta, d), lambda b: (0, 0)),
        out_shape=jax.ShapeDtypeStruct((n_meta, d), F32),
        compiler_params=_params(("arbitrary",)),
    )(g_h0)


def _s5_param_fn(lr, li, ldt, br, bi):
    dt = jnp.exp(ldt)
    e = jnp.exp(lr * dt)
    w = li * dt
    lbr = e * jnp.cos(w)
    lbi = e * jnp.sin(w)
    nr = lbr - 1.0
    den = lr * lr + li * li
    cr = (nr * lr + lbi * li) / den
    ci = (lbi * lr - nr * li) / den
    bbr = cr[:, None, :] * br - ci[:, None, :] * bi
    bbi = cr[:, None, :] * bi + ci[:, None, :] * br
    return lbr, lbi, bbr, bbi


def _s5_params(name, lr, li, ldt, br, bi):
    def body(lr_ref, li_ref, ldt_ref, br_ref, bi_ref, o1, o2, o3, o4):
        res = _s5_param_fn(lr_ref[...], li_ref[...], ldt_ref[...], br_ref[...], bi_ref[...])
        for o, v in zip((o1, o2, o3, o4), res):
            o[...] = v

    shp = [jax.ShapeDtypeStruct(lr.shape, F32)] * 2 + [jax.ShapeDtypeStruct(br.shape, F32)] * 2
    return pl.pallas_call(body, name=name, out_shape=shp)(lr, li, ldt, br, bi)


def _s5_params_bwd(name, lr, li, ldt, br, bi, g_lbr, g_lbi, g_bbr, g_bbi, gd_parts):
    def body(lr_ref, li_ref, ldt_ref, br_ref, bi_ref, g1, g2, g3, g4, gd_ref, o1, o2, o3, o4, o5, o6):
        _, vjp = jax.vjp(_s5_param_fn, lr_ref[...], li_ref[...], ldt_ref[...], br_ref[...], bi_ref[...])
        res = vjp((jnp.sum(g1[...], axis=0), jnp.sum(g2[...], axis=0), g3[...], g4[...]))
        for o, v in zip((o1, o2, o3, o4, o5), res):
            o[...] = v
        o6[...] = jnp.sum(gd_ref[...], axis=0)

    shp = ([jax.ShapeDtypeStruct(lr.shape, F32)] * 2 + [jax.ShapeDtypeStruct(ldt.shape, F32)]
           + [jax.ShapeDtypeStruct(br.shape, F32)] * 2 + [jax.ShapeDtypeStruct(gd_parts.shape[1:], F32)])
    return pl.pallas_call(body, name=name, out_shape=shp)(lr, li, ldt, br, bi, g_lbr, g_lbi, g_bbr, g_bbi, gd_parts)


def _interleave(re, im, w):
    nj = re.shape[-1] // w
    return jnp.concatenate([x[..., j * w:(j + 1) * w] for j in range(nj) for x in (re, im)], axis=-1)


def _deinterleave(x, w):
    nj = x.shape[-1] // (2 * w)
    return (jnp.concatenate([x[..., 2 * j * w:(2 * j + 1) * w] for j in range(nj)], axis=-1),
            jnp.concatenate([x[..., (2 * j + 1) * w:(2 * j + 2) * w] for j in range(nj)], axis=-1))


def _cmul(ar, ai, br, bi):
    return ar * br - ai * bi, ar * bi + ai * br


def _powers(lr, li):
    p = [(lr, li)]
    p.append(_cmul(*p[0], *p[0]))
    p.append(_cmul(*p[1], *p[0]))
    p.append(_cmul(*p[1], *p[1]))
    p.append(_cmul(*p[3], *p[0]))
    p.append(_cmul(*p[3], *p[1]))
    p.append(_cmul(*p[3], *p[2]))
    p.append(_cmul(*p[3], *p[3]))
    return p


def _scan_steps(pw, shifts, keep):
    return [(sh, jnp.where(m, pw[s - 1][0], 0.0), jnp.where(m, pw[s - 1][1], 0.0))
            for s, sh, m in zip((1, 2, 4), shifts, keep)]


def _scan_tile(xr, xi, steps):
    for sh, br, bi in steps:
        rr = pltpu.roll(xr, sh, 0)
        ri = pltpu.roll(xi, sh, 0)
        xr, xi = xr + (br * rr - bi * ri), xi + (br * ri + bi * rr)
    return xr, xi


def _s5_scan(name, bu, lam, n_batch, lp, w):
    rows, two_ns = bu.shape
    nj = two_ns // (2 * w)

    def body(x_ref, lam_ref, s_ref):
        pw = _powers(lam_ref[:, :w], lam_ref[:, w:])
        tab_r = jnp.concatenate([p[0] for p in pw], axis=0)
        tab_i = jnp.concatenate([p[1] for p in pw], axis=0)
        row = lax.broadcasted_iota(jnp.int32, (SUBLANES, w), 0)
        steps = _scan_steps(pw, (1, 2, 4), [row >= s for s in (1, 2, 4)])

        def packed_tile(t, carry):
            cr, ci = carry
            r0 = pl.multiple_of(t * PACKED_ROWS, PACKED_ROWS)
            x = x_ref[pl.ds(r0, PACKED_ROWS), :].astype(F32)
            done = []
            for half in range(PACKED_ROWS // SUBLANES):
                xt = x[half * SUBLANES:(half + 1) * SUBLANES, :]
                xr, xi = _scan_tile(xt[:, :w], xt[:, w:], steps)
                sr = xr + (tab_r * cr - tab_i * ci)
                si = xi + (tab_r * ci + tab_i * cr)
                done.append(jnp.concatenate([sr, si], axis=1))
                cr, ci = sr[SUBLANES - 1:, :], si[SUBLANES - 1:, :]
            s_ref[pl.ds(r0, PACKED_ROWS), :] = jnp.concatenate(done, axis=0).astype(s_ref.dtype)
            return cr, ci

        zero = jnp.zeros((1, w), F32)
        _unrolled_loop(lp // PACKED_ROWS, packed_tile, (zero, zero), SCAN_UNROLL)

    spec = pl.BlockSpec((lp, 2 * w), lambda b, j: (b, j))
    return pl.pallas_call(
        body, name=name, grid=(n_batch, nj), in_specs=[spec, pl.BlockSpec((1, 2 * w), lambda b, j: (0, j))],
        out_specs=spec, out_shape=jax.ShapeDtypeStruct((rows, two_ns), BF16),
        compiler_params=_params(("parallel", "parallel")),
    )(bu, lam)


def _s5_scan_bwd(name, gd, states, lam, n_batch, lp, w):
    rows, two_ns = gd.shape
    nj = two_ns // (2 * w)

    def body(x_ref, s_ref, lam_ref, g_ref, gl_ref):
        pw = _powers(lam_ref[:, :w], -lam_ref[:, w:])
        tab_r = jnp.concatenate([p[0] for p in reversed(pw)], axis=0)
        tab_i = jnp.concatenate([p[1] for p in reversed(pw)], axis=0)
        row = lax.broadcasted_iota(jnp.int32, (SUBLANES, w), 0)
        steps = _scan_steps(pw, [SUBLANES - s for s in (1, 2, 4)], [row < SUBLANES - s for s in (1, 2, 4)])

        n_packed = lp // PACKED_ROWS
        halves = PACKED_ROWS // SUBLANES

        def packed_tile(u, carry):
            cr, ci, ar, ai = carry
            t = n_packed - 1 - u
            r0 = pl.multiple_of(t * PACKED_ROWS, PACKED_ROWS)
            x = x_ref[pl.ds(r0, PACKED_ROWS), :].astype(F32)
            cur = s_ref[pl.ds(r0, PACKED_ROWS), :].astype(F32)
            p0 = pl.multiple_of(jnp.maximum(t - 1, 0) * PACKED_ROWS, PACKED_ROWS)
            before = s_ref[pl.ds(p0, PACKED_ROWS), :].astype(F32)[PACKED_ROWS - 1:, :] * jnp.where(t > 0, 1.0, 0.0)
            done = [None] * halves
            for half in reversed(range(halves)):
                rows_h = slice(half * SUBLANES, (half + 1) * SUBLANES)
                xt, st = x[rows_h, :], cur[rows_h, :]
                xr, xi = _scan_tile(xt[:, :w], xt[:, w:], steps)
                gr = xr + (tab_r * cr - tab_i * ci)
                gi = xi + (tab_r * ci + tab_i * cr)
                done[half] = jnp.concatenate([gr, gi], axis=1)
                prev = before if half == 0 else cur[half * SUBLANES - 1:half * SUBLANES, :]
                spr = jnp.where(row >= 1, pltpu.roll(st[:, :w], 1, 0), prev[:, :w])
                spi = jnp.where(row >= 1, pltpu.roll(st[:, w:], 1, 0), prev[:, w:])
                cr, ci, ar, ai = gr[:1, :], gi[:1, :], ar + gr * spr + gi * spi, ai + gi * spr - gr * spi
            g_ref[pl.ds(r0, PACKED_ROWS), :] = jnp.concatenate(done, axis=0).astype(g_ref.dtype)
            return cr, ci, ar, ai

        z1 = jnp.zeros((1, w), F32)
        z8 = jnp.zeros((SUBLANES, w), F32)
        _, _, ar, ai = _unrolled_loop(n_packed, packed_tile, (z1, z1, z8, z8), SCAN_UNROLL)
        gl_ref[...] = jnp.concatenate([jnp.sum(ar, axis=0, keepdims=True), jnp.sum(ai, axis=0, keepdims=True)], axis=1)

    spec = pl.BlockSpec((lp, 2 * w), lambda b, j: (b, j))
    return pl.pallas_call(
        body, name=name, grid=(n_batch, nj),
        in_specs=[spec, spec, pl.BlockSpec((1, 2 * w), lambda b, j: (0, j))],
        out_specs=[spec, pl.BlockSpec((None, 1, 2 * w), lambda b, j: (b, 0, j))],
        out_shape=[jax.ShapeDtypeStruct((rows, two_ns), BF16), jax.ShapeDtypeStruct((n_batch, 1, two_ns), F32)],
        compiler_params=_params(("parallel", "parallel")),
    )(gd, states, lam)


def _log_sigmoid(z):
    return jnp.minimum(z, 0.0) - jnp.log(1.0 + jnp.exp(-jnp.abs(z)))


ATTN_KEYS = 256
ATTN_GROUP = 4


def _attn_block(i, jb, lp, pad):
    start = jb * ATTN_KEYS
    r0 = pl.multiple_of(jnp.minimum(start, lp - ATTN_KEYS), LANES)
    rowpos = i * LANES + lax.broadcasted_iota(jnp.int32, (LANES, ATTN_KEYS), 0)
    keypos = r0 + lax.broadcasted_iota(jnp.int32, (LANES, ATTN_KEYS), 1)
    return r0, (keypos < rowpos) & (keypos >= jnp.maximum(start, pad))


def _tri_ones(strict_upper):
    r = lax.broadcasted_iota(jnp.int32, (ATTN_KEYS, ATTN_KEYS + LANES), 0)
    c = lax.broadcasted_iota(jnp.int32, (ATTN_KEYS, ATTN_KEYS + LANES), 1)
    tri = (r > c) if strict_upper else (r < c)
    return jnp.where((c >= ATTN_KEYS) | tri, 1.0, 0.0).astype(BF16)


def _split_sums(cr):
    rs = cr[:, ATTN_KEYS:]
    return cr[:, :ATTN_KEYS], jnp.concatenate([rs] * (ATTN_KEYS // LANES), axis=1)


def _head_masks():
    lane = lax.broadcasted_iota(jnp.int32, (1, LANES), 1)
    return [lane < SB_HEAD_DIM, lane >= SB_HEAD_DIM]


def _run_groups(n, first, sign, make):
    j, left, g = first, n, ATTN_GROUP
    while g >= 1:
        shift = g.bit_length() - 1
        count = lax.shift_right_logical(left, shift)
        fn = make(g)

        def loop(_, jcur, fn=fn, g=g):
            fn(jcur)
            return jcur + sign * g

        j = lax.fori_loop(0, count, loop, j)
        left = left - lax.shift_left(count, shift)
        g //= 2


def _attn_fwd(name, proj, n_batch, lp, pad, q_cb, k_cb, v_cb, n_pairs):
    rows = proj.shape[0]
    nq = lp // LANES
    scale = SB_HEAD_DIM ** -0.5

    def body(q_ref, k_ref, v_ref, o_ref, acc_s):
        i = pl.program_id(1)
        hm = _head_masks()
        comb = _tri_ones(True)
        n_blocks = lax.shift_right_logical(i + ATTN_KEYS // LANES, (ATTN_KEYS // LANES).bit_length() - 1)

        def pair(hp, carry):
            lanes = pl.ds(pl.multiple_of(hp * LANES, LANES), LANES)
            qs = q_ref[:, lanes] * scale
            qh = [jnp.where(m, qs, 0.0).astype(BF16) for m in hm]
            acc_s[...] = jnp.zeros_like(acc_s)
            o_ref[:, lanes] = jnp.zeros((LANES, LANES), F32)

            def make(group):
                def fn(jtop):
                    chains = []
                    for g in range(group):
                        r0, vis = _attn_block(i, jtop - g, lp, pad)
                        kj = k_ref[pl.ds(r0, ATTN_KEYS), lanes].astype(BF16)
                        vj = v_ref[pl.ds(r0, ATTN_KEYS), lanes]
                        for h in range(2):
                            z = lax.dot_general(qh[h], kj, NT, preferred_element_type=F32)
                            chains.append((h, vis, z, jnp.where(hm[h], vj, 0.0).astype(BF16)))
                    staged = []
                    for h, vis, z, vh in chains:
                        lsz = _log_sigmoid(z)
                        staged.append((h, vis, lsz, _running_sums(jnp.where(vis, lsz - z, 0.0), comb, split=True), vh))
                    out = o_ref[:, lanes]
                    for h, vis, lsz, cr, vh in staged:
                        later, rs = _split_sums(cr)
                        acc = acc_s[h]
                        wgt = jnp.where(vis, jnp.exp(lsz + later + acc), 0.0)
                        acc_s[h] = acc + rs
                        out = out + lax.dot_general(wgt.astype(BF16), vh, NN, preferred_element_type=F32)
                    o_ref[:, lanes] = out
                return fn

            _run_groups(n_blocks, n_blocks - 1, -1, make)
            return carry

        lax.fori_loop(0, n_pairs, pair, 0)

    wide = n_pairs * LANES
    assert q_cb % n_pairs == 0 and k_cb % n_pairs == 0 and v_cb % n_pairs == 0
    return pl.pallas_call(
        body, name=name, grid=(n_batch, nq),
        in_specs=[pl.BlockSpec((LANES, wide), lambda b, i: (b * nq + i, q_cb // n_pairs)),
                  pl.BlockSpec((lp, wide), lambda b, i: (b, k_cb // n_pairs)),
                  pl.BlockSpec((lp, wide), lambda b, i: (b, v_cb // n_pairs))],
        out_specs=pl.BlockSpec((LANES, wide), lambda b, i: (b * nq + i, 0)),
        out_shape=jax.ShapeDtypeStruct((rows, wide), F32),
        scratch_shapes=[pltpu.VMEM((2, LANES, ATTN_KEYS), F32)],
        compiler_params=_params(("parallel", "arbitrary")),
    )(proj, proj, proj)


def _attn_bwd(name, proj, g_out, n_batch, lp, pad, q_cb, k_cb, v_cb, go_cb, n_pairs):
    rows = proj.shape[0]
    nq = lp // LANES
    scale = SB_HEAD_DIM ** -0.5

    def body(q_ref, k_ref, v_ref, go_ref, gq_ref, gk_ref, gv_ref, ga_s, sz_s, acc_s):
        i = pl.program_id(1)

        @pl.when(i == 0)
        def _():
            gk_ref[...] = jnp.zeros_like(gk_ref)
            gv_ref[...] = jnp.zeros_like(gv_ref)

        hm = _head_masks()
        comb_up = _tri_ones(True)
        comb_lo = _tri_ones(False)
        n_blocks = lax.shift_right_logical(i + ATTN_KEYS // LANES, (ATTN_KEYS // LANES).bit_length() - 1)

        def pair(hp, carry):
            lanes = pl.ds(pl.multiple_of(hp * LANES, LANES), LANES)
            qs = q_ref[:, lanes] * scale
            go = go_ref[:, lanes]
            qh = [jnp.where(m, qs, 0.0).astype(BF16) for m in hm]
            goh = [jnp.where(m, go, 0.0).astype(BF16) for m in hm]
            acc_s[...] = jnp.zeros_like(acc_s)

            def make_down(group):
                def fn(jtop):
                    chains = []
                    for g in range(group):
                        j = jtop - g
                        r0, vis = _attn_block(i, j, lp, pad)
                        kj = k_ref[pl.ds(r0, ATTN_KEYS), lanes].astype(BF16)
                        vj = v_ref[pl.ds(r0, ATTN_KEYS), lanes].astype(BF16)
                        for h in range(2):
                            z = lax.dot_general(qh[h], kj, NT, preferred_element_type=F32)
                            gw = lax.dot_general(goh[h], vj, NT, preferred_element_type=F32)
                            chains.append((h, j, r0, vis, z, gw))
                    staged = []
                    for h, j, r0, vis, z, gw in chains:
                        lsz = _log_sigmoid(z)
                        staged.append((h, j, r0, vis, lsz, _running_sums(jnp.where(vis, lsz - z, 0.0), comb_up), gw))
                    for h, j, r0, vis, lsz, cr, gw in staged:
                        later, rs = _split_sums(cr)
                        acc = acc_s[h]
                        wgt = jnp.where(vis, jnp.exp(lsz + later + acc), 0.0)
                        acc_s[h] = acc + rs
                        ga_s[h, j] = gw * wgt
                        sz_s[h, j] = jnp.exp(lsz)
                        gv_ref[pl.ds(r0, ATTN_KEYS), lanes] += lax.dot_general(
                            wgt.astype(BF16), goh[h], TN, preferred_element_type=F32)
                return fn

            _run_groups(n_blocks, n_blocks - 1, -1, make_down)
            acc_s[...] = jnp.zeros_like(acc_s)

            def make_up(group):
                def fn(jbot):
                    pend = []
                    for g in range(group):
                        j = jbot + g
                        r0, vis = _attn_block(i, j, lp, pad)
                        kj = k_ref[pl.ds(r0, ATTN_KEYS), lanes]
                        for h in range(2):
                            ga = ga_s[h, j]
                            pend.append((h, j, r0, vis, ga, _running_sums(ga, comb_lo),
                                         jnp.where(hm[h], kj, 0.0).astype(BF16)))
                    gq = jnp.zeros((LANES, LANES), F32)
                    for h, j, r0, vis, ga, cr, kh in pend:
                        before, rs = _split_sums(cr)
                        pre = acc_s[h]
                        glk = before + pre
                        acc_s[h] = pre + rs
                        sz = sz_s[h, j]
                        gz = jnp.where(vis, ga * (1.0 - sz) - glk * sz, 0.0).astype(BF16)
                        gq = gq + lax.dot_general(gz, kh, NN, preferred_element_type=F32)
                        gk_ref[pl.ds(r0, ATTN_KEYS), lanes] += lax.dot_general(gz, qh[h], TN, preferred_element_type=F32)
                    gq_ref[:, lanes] += gq * scale
                return fn

            gq_ref[:, lanes] = jnp.zeros((LANES, LANES), F32)
            _run_groups(n_blocks, 0, 1, make_up)
            return carry

        lax.fori_loop(0, n_pairs, pair, 0)

    wide = n_pairs * LANES
    assert q_cb % n_pairs == 0 and k_cb % n_pairs == 0 and v_cb % n_pairs == 0 and go_cb % n_pairs == 0
    blk = lambda cb: pl.BlockSpec((LANES, wide), lambda b, i: (b * nq + i, cb // n_pairs))
    full = lambda cb: pl.BlockSpec((lp, wide), lambda b, i: (b, cb // n_pairs))
    shp = jax.ShapeDtypeStruct((rows, wide), F32)
    per_block = pltpu.VMEM((2, -(-lp // ATTN_KEYS), LANES, ATTN_KEYS), F32)
    return pl.pallas_call(
        body, name=name, grid=(n_batch, nq),
        in_specs=[blk(q_cb), full(k_cb), full(v_cb), blk(go_cb)],
        out_specs=[blk(0), full(0), full(0)], out_shape=[shp, shp, shp],
        scratch_shapes=[per_block, per_block, pltpu.VMEM((2, LANES, ATTN_KEYS), F32)],
        compiler_params=_params(("parallel", "arbitrary")),
    )(proj, proj, proj, g_out)


def _lb_fn(gamma):
    g0, g1 = gamma[0:1, :], gamma[1:2, :]
    mx = jnp.maximum(g0, g1)
    e0, e1 = jnp.exp(g0 - mx), jnp.exp(g1 - mx)
    p0, p1 = e0 / (e0 + e1), e1 / (e0 + e1)
    return (p0 + p1) - p0


def _lower_bound(name, gamma):
    def body(g_ref, o_ref):
        o_ref[...] = _lb_fn(g_ref[...])

    return pl.pallas_call(body, name=name, out_shape=jax.ShapeDtypeStruct((1, gamma.shape[1]), F32))(gamma)


def _lower_bound_bwd(name, gamma, g_lb_parts, g_ng_parts):
    def body(g_ref, glb_ref, gng_ref, o_ref, o2_ref):
        _, vjp = jax.vjp(_lb_fn, g_ref[...])
        o_ref[...] = vjp(jnp.sum(glb_ref[...], axis=0))[0]
        o2_ref[...] = jnp.sum(gng_ref[...], axis=0)

    return pl.pallas_call(
        body, name=name,
        out_shape=[jax.ShapeDtypeStruct(gamma.shape, F32), jax.ShapeDtypeStruct((1, gamma.shape[1]), F32)],
    )(gamma, g_lb_parts, g_ng_parts)


def _tri_times(tril, x, dims):
    hi = x.astype(BF16)
    lo = (x - hi.astype(F32)).astype(BF16)
    t = tril.astype(BF16)
    return (lax.dot_general(t, hi, dims, preferred_element_type=F32)
            + lax.dot_general(t, lo, dims, preferred_element_type=F32))


@jax.custom_vjp
def _cumsum_rows(x, tril):
    return _tri_times(tril, x, NN)


def _cumsum_rows_fwd(x, tril):
    return _tri_times(tril, x, NN), tril


def _cumsum_rows_bwd(tril, g):
    return _tri_times(tril, g, TN), jnp.zeros_like(tril)


_cumsum_rows.defvjp(_cumsum_rows_fwd, _cumsum_rows_bwd)


def _hg_decays(f_pre, lbs, masks, tril):
    f = [[lb + (1.0 - lb) * jax.nn.sigmoid(fc) for fc, lb in zip(row, lbs)] for row in f_pre]
    bcum = [[_cumsum_rows(jnp.log(x) * m, tril) for x in row] for row, m in zip(f, masks)]
    return [[1.0 - x for x in row] for row in f], bcum


def _hg_step(q, f_pre, i_in, g, lbs, ngs, sts, masks, tril):
    k, bcum = _hg_decays(f_pre, lbs, masks, tril)
    v = [[ic * m for ic in row] for row, m in zip(i_in, masks)]
    qd = [[qc * jnp.exp(b) for qc, b in zip(qr, br)] for qr, br in zip(q, bcum)]
    scores = [[jnp.where(tril > 0.5, _dot(a, kk * jnp.exp(-b), NT), 0.0) for a, kk, b in zip(ar, kr, br)]
              for ar, kr, br in zip(qd, k, bcum)]
    inner = [[_dot(s, x, NN) for s, x in zip(sr, vr)] for sr, vr in zip(scores, v)]
    add = [[_dot(x, kk * jnp.exp(b[HG_CHUNK - 1:, :] - b), TN) for x, kk, b in zip(vr, kr, br)]
           for vr, kr, br in zip(v, k, bcum)]
    outs = []
    for qr, br, nr, ar, gr in zip(qd, bcum, inner, add, g):
        o = [n + _dot(a, st, NT) for n, a, st in zip(nr, qr, sts)]
        sts = [jnp.exp(b[HG_CHUNK - 1:, :]) * st + a for b, a, st in zip(br, ar, sts)]
        o = [x * lax.rsqrt(jnp.mean(x * x, axis=-1, keepdims=True) + RMS_EPS) * ng for x, ng in zip(o, ngs)]
        outs.append([x * (gc * jax.nn.sigmoid(gc)) for x, gc in zip(o, gr)])
    return outs, sts


def _hg_consts(c, pad):
    r = lax.broadcasted_iota(jnp.int32, (HG_CHUNK, HG_CHUNK), 0)
    cc = lax.broadcasted_iota(jnp.int32, (HG_CHUNK, HG_CHUNK), 1)
    tril = jnp.where(r >= cc, 1.0, 0.0).astype(F32)
    pos = c * HG_CHUNK + lax.broadcasted_iota(jnp.int32, (HG_CHUNK, 1), 0)
    return tril, jnp.where(pos >= pad, 1.0, 0.0).astype(F32)


HG_HEADS_PER_STEP = 8
HG_CHUNKS_PER_STEP = 2


def _hg_layout(lp, n_heads):
    step_rows = HG_CHUNKS_PER_STEP * HG_CHUNK
    per = min(HG_HEADS_PER_STEP, n_heads)
    assert lp % step_rows == 0 and n_heads % per == 0
    heads = [(h, slice(h * HG_DK, (h + 1) * HG_DK)) for h in range(per)]
    return n_heads // per, lp // step_rows, step_rows, per * HG_DK, heads


def _hg_step_views(step, pad, heads):
    slices = [slice(u * HG_CHUNK, (u + 1) * HG_CHUNK) for u in range(HG_CHUNKS_PER_STEP)]
    consts = [_hg_consts(step * HG_CHUNKS_PER_STEP + u, pad) for u in range(HG_CHUNKS_PER_STEP)]
    load = lambda ref: [[ref[sl, cols] for _, cols in heads] for sl in slices]
    return slices, [m for _, m in consts], consts[0][0], load


def _hgrn_fwd(name, proj, lb, ng, n_batch, lp, pad, n_heads):
    rows = proj.shape[0]
    groups, steps, step_rows, wide, heads = _hg_layout(lp, n_heads)

    def body(q_ref, f_ref, i_ref, g_ref, lb_ref, ng_ref, o_ref, s_ref, st_s):
        t = pl.program_id(2)

        @pl.when(t == 0)
        def _():
            st_s[...] = jnp.zeros_like(st_s)

        slices, masks, tril, load = _hg_step_views(t, pad, heads)
        sts = [st_s[h] for h, _ in heads]
        for (_, cols), st in zip(heads, sts):
            s_ref[:, cols] = st
        outs, sts = _hg_step(load(q_ref), load(f_ref), load(i_ref), load(g_ref),
                             [lb_ref[:, cols] for _, cols in heads], [ng_ref[:, cols] for _, cols in heads],
                             sts, masks, tril)
        for sl, row in zip(slices, outs):
            for (_, cols), o in zip(heads, row):
                o_ref[sl, cols] = o
        for (h, _), st in zip(heads, sts):
            st_s[h] = st

    col = lambda off: pl.BlockSpec((step_rows, wide), lambda b, h, t: (b * steps + t, off * groups + h))
    vec = pl.BlockSpec((1, wide), lambda b, h, t: (0, h))
    return pl.pallas_call(
        body, name=name, grid=(n_batch, groups, steps), in_specs=[col(0), col(1), col(2), col(3), vec, vec],
        out_specs=[col(0), pl.BlockSpec((HG_DK, wide), lambda b, h, t: (b * steps + t, h))],
        out_shape=[jax.ShapeDtypeStruct((rows, n_heads * HG_DK), F32),
                   jax.ShapeDtypeStruct((n_batch * steps * HG_DK, n_heads * HG_DK), F32)],
        scratch_shapes=[pltpu.VMEM((len(heads), HG_DK, HG_DK), F32)],
        compiler_params=_params(("parallel", "parallel", "arbitrary")),
    )(proj, proj, proj, proj, lb, ng)


def _hgrn_bwd(name, proj, lb, ng, g_out, states, n_batch, lp, pad, n_heads):
    rows = proj.shape[0]
    width = n_heads * HG_DK
    groups, steps, step_rows, wide, heads = _hg_layout(lp, n_heads)
    assert groups == 1

    def body(q_ref, f_ref, i_ref, g_ref, lb_ref, ng_ref, go_ref, s_ref, gp_ref, glb_ref, gng_ref, gst_s):
        t = pl.program_id(2)

        @pl.when(t == 0)
        def _():
            gst_s[...] = jnp.zeros_like(gst_s)
            glb_ref[...] = jnp.zeros_like(glb_ref)
            gng_ref[...] = jnp.zeros_like(gng_ref)

        slices, masks, tril, load = _hg_step_views(steps - 1 - t, pad, heads)
        fn = functools.partial(_hg_step, masks=masks, tril=tril)
        _, vjp = jax.vjp(fn, load(q_ref), load(f_ref), load(i_ref), load(g_ref),
                         [lb_ref[:, cols] for _, cols in heads], [ng_ref[:, cols] for _, cols in heads],
                         [s_ref[:, cols] for _, cols in heads])
        gq, gf, gi, gg, glb, gng, gst = vjp((load(go_ref), [gst_s[h] for h, _ in heads]))
        for part, grads in enumerate((gq, gf, gi, gg)):
            for sl, row in zip(slices, grads):
                for (h, _), x in zip(heads, row):
                    lane0 = part * width + h * HG_DK
                    gp_ref[sl, lane0:lane0 + HG_DK] = x.astype(BF16)
        for (h, cols), a, b, c in zip(heads, gst, glb, gng):
            gst_s[h] = a
            glb_ref[:, cols] += b
            gng_ref[:, cols] += c

    col = lambda off: pl.BlockSpec((step_rows, wide), lambda b, h, t: (b * steps + steps - 1 - t, off * groups + h))
    vec = pl.BlockSpec((1, wide), lambda b, h, t: (0, h))
    part = pl.BlockSpec((None, 1, wide), lambda b, h, t: (b, 0, h))
    big = jax.ShapeDtypeStruct((rows, 4 * width), BF16)
    small = jax.ShapeDtypeStruct((n_batch, 1, width), F32)
    return pl.pallas_call(
        body, name=name, grid=(n_batch, groups, steps),
        in_specs=[col(0), col(1), col(2), col(3), vec, vec, col(0),
                  pl.BlockSpec((HG_DK, wide), lambda b, h, t: (b * steps + steps - 1 - t, h))],
        out_specs=[pl.BlockSpec((step_rows, 4 * width), lambda b, h, t: (b * steps + steps - 1 - t, 0)), part, part],
        out_shape=[big, small, small],
        scratch_shapes=[pltpu.VMEM((len(heads), HG_DK, HG_DK), F32)],
        compiler_params=_params(("parallel", "parallel", "arbitrary")),
    )(proj, proj, proj, proj, lb, ng, g_out, states)


def _exchange_copies(src, dst, send, recv, loc, scatter):
    x, y, c = lax.axis_index("x"), lax.axis_index("y"), lax.axis_index("c")
    me = 4 * x + 2 * y + c
    local, remote = [], []
    for w in range(len(src)):
        local.append(pltpu.make_async_copy(src[w].at[me] if scatter else src[w], dst[w].at[me], loc.at[w]))
    for k in range(1, N_DEV):
        px = 1 - x if k & 4 else x
        py = 1 - y if k & 2 else y
        pc = 1 - c if k & 1 else c
        peer = 4 * px + 2 * py + pc
        for w in range(len(src)):
            remote.append(pltpu.make_async_remote_copy(
                src_ref=src[w].at[peer] if scatter else src[w], dst_ref=dst[w].at[me],
                send_sem=send.at[w * (N_DEV - 1) + k - 1], recv_sem=recv.at[w * (N_DEV - 1) + k - 1],
                device_id=(px, py, pc), device_id_type=pl.DeviceIdType.MESH))
    return local, remote


_HBM_SPEC = pl.BlockSpec(memory_space=pltpu.HBM)
_SEM_SPEC = pl.BlockSpec(memory_space=pltpu.SEMAPHORE)
_ANY_SPEC = pl.BlockSpec(memory_space=pl.ANY)
_DATAFLOW = pltpu.SideEffectType.DATAFLOW_SIDE_EFFECTING


def _exchange_start(name, srcs, scatter, dep=None):
    nw = len(srcs)
    srcs = [pltpu.with_memory_space_constraint(s, pltpu.HBM) for s in srcs]
    lands = [pltpu.with_memory_space_constraint(lax.empty(s.shape if scatter else (N_DEV,) + s.shape, s.dtype), pltpu.HBM)
             for s in srcs]
    deps = [] if dep is None else [dep]

    def body(*refs):
        src, dst = refs[:nw], refs[nw:2 * nw]
        send, recv, loc = refs[2 * nw + len(deps):2 * nw + len(deps) + 3]
        token = refs[-1]
        local, remote = _exchange_copies(src, dst, send, recv, loc, scatter)
        for cp in local + remote:
            cp.start()
        token[...] = jnp.zeros_like(token)

    sems = [pltpu.SemaphoreType.DMA((nw * (N_DEV - 1),)), pltpu.SemaphoreType.DMA((nw * (N_DEV - 1),)),
            pltpu.SemaphoreType.DMA((nw,))]
    out = pl.pallas_call(
        body, name=name,
        out_shape=(*sems, *[pltpu.HBM(s.shape, s.dtype) for s in srcs], *[pltpu.HBM(s.shape, s.dtype) for s in lands],
                   jax.ShapeDtypeStruct((SUBLANES, LANES), F32)),
        in_specs=[_HBM_SPEC] * (2 * nw) + [_ANY_SPEC] * len(deps),
        out_specs=(_SEM_SPEC, _SEM_SPEC, _SEM_SPEC, *[_HBM_SPEC] * (2 * nw), pl.BlockSpec(memory_space=pltpu.VMEM)),
        input_output_aliases={i: 3 + i for i in range(2 * nw)},
        compiler_params=pltpu.CompilerParams(has_side_effects=_DATAFLOW),
    )(*srcs, *lands, *deps)
    return {"sems": out[:3], "srcs": out[3:3 + nw], "lands": out[3 + nw:3 + 2 * nw], "token": out[-1], "scatter": scatter}


def _exchange_wait(name, handle, after):
    nw = len(handle["srcs"])
    scatter = handle["scatter"]

    def body(*refs):
        src, dst = refs[:nw], refs[nw:2 * nw]
        send, recv, loc = refs[2 * nw:2 * nw + 3]
        local, remote = _exchange_copies(src, dst, send, recv, loc, scatter)
        for cp in local:
            cp.wait()
        for cp in remote:
            cp.wait_send()
            cp.wait_recv()

    out = pl.pallas_call(
        body, name=name,
        out_shape=(*[pltpu.HBM(s.shape, s.dtype) for s in handle["srcs"]],
                   *[pltpu.HBM(s.shape, s.dtype) for s in handle["lands"]]),
        in_specs=[_HBM_SPEC] * (2 * nw) + [_SEM_SPEC] * 3 + [_ANY_SPEC],
        out_specs=tuple([_HBM_SPEC] * (2 * nw)),
        input_output_aliases={i: i for i in range(2 * nw)},
        compiler_params=pltpu.CompilerParams(has_side_effects=_DATAFLOW),
    )(*handle["srcs"], *handle["lands"], *handle["sems"], after)
    return list(out[nw:])


def _adamw(w, g, m, v):
    m = ADAM_B1 * m + (1.0 - ADAM_B1) * g
    v = ADAM_B2 * v + (1.0 - ADAM_B2) * (g * g)
    m_hat = m / (1.0 - ADAM_B1 ** ADAM_STEP)
    v_hat = v / (1.0 - ADAM_B2 ** ADAM_STEP)
    delta = -ADAM_LR * (m_hat / (jnp.sqrt(v_hat) + ADAM_EPS) + ADAM_WD * w)
    return delta, m, v


def _adamw_summed(name, parts, w, m, v):
    layered = w.ndim == 3
    parts = list(parts) if layered else [parts]
    n_layers = len(parts)
    rows, cols = w.shape[-2:]
    n_parts = parts[0].shape[0]
    tr = _tile(rows, max(SUBLANES, (1 << 18) // cols))

    def body(*refs):
        p_refs = refs[:n_layers]
        w_ref, m_ref, v_ref, g_ref, d_ref, nm_ref, nv_ref = refs[n_layers:]
        layer = pl.program_id(0)

        def run(p_ref):
            g = p_ref[0].astype(F32)
            for s in range(1, n_parts):
                g = g + p_ref[s].astype(F32)
            d, nm, nv = _adamw(w_ref[...], g, m_ref[...], v_ref[...])
            g_ref[...] = g
            d_ref[...] = d
            nm_ref[...] = nm
            nv_ref[...] = nv

        for l in range(n_layers):
            pl.when(layer == l)(functools.partial(run, p_refs[l]))

    if layered:
        spec = pl.BlockSpec((None, tr, cols), lambda l, i: (l, i, 0))
    else:
        spec = pl.BlockSpec((tr, cols), lambda l, i: (i, 0))
    p_specs = [pl.BlockSpec((n_parts, tr, cols), lambda l, i, q=q: (0, jnp.where(l == q, i, 0), 0))
               for q in range(n_layers)]
    shp = jax.ShapeDtypeStruct(w.shape, F32)
    return pl.pallas_call(
        body, name=name, grid=(n_layers, rows // tr), in_specs=[*p_specs, spec, spec, spec],
        out_specs=[spec] * 4, out_shape=[shp] * 4, compiler_params=_params(("parallel", "parallel")),
    )(*parts, w, m, v)


def _pack_rows(arrays, cols):
    out = []
    for a in arrays:
        flat = a.reshape(-1)
        n = -(-flat.shape[0] // cols) * cols
        out.append(jnp.pad(flat, (0, n - flat.shape[0])).reshape(-1, cols))
    packed = jnp.concatenate(out, axis=0)
    return jnp.pad(packed, ((0, -packed.shape[0] % SUBLANES), (0, 0)))


def _unpack_rows(packed, shapes, cols):
    out, r = [], 0
    for s in shapes:
        n = math.prod(s)
        nr = -(-n // cols)
        out.append(packed[r:r + nr].reshape(-1)[:n].reshape(s))
        r += nr
    return out


def _block_diag(blocks):
    g, a, b = blocks.shape
    eye = jnp.eye(g, dtype=blocks.dtype)
    return (eye[:, None, :, None] * blocks[:, :, None, :]).reshape(g * a, g * b)


def _diag_blocks(dense, g):
    a, b = dense.shape[0] // g, dense.shape[1] // g
    return jnp.einsum("gagb->gab", dense.reshape(g, a, g, b))


def _local_step(x, target, meta, wts, small, late_weights, on_grads, on_small):
    n_batch, seq, d = x.shape
    n_meta = meta.shape[0]
    pad = -(seq + n_meta) % LANES
    lead = pad + n_meta
    lp = lead + seq
    rows = n_batch * lp
    s5w = wts["glu"].shape[0]
    n_ab = wts["in_ab"].shape[2]
    ab_cols = wts["in_ab"].shape[0] * n_ab
    sbw = (ab_cols - s5w) // 3
    dff = small["mlp_b_up"].shape[1]
    n_pairs = sbw // LANES
    n_hg = d // HG_DK
    s5_cb = s5w // LANES
    sb_cb = sbw // LANES
    tm = _tile(rows, ROW_TILE)
    groups, n_state, grp = small["s5_b_re"].shape[1:]
    ns = groups * n_state
    sw = min(SCAN_LANES, ns)

    h0 = jnp.concatenate(
        [jnp.zeros((n_batch, pad, d), F32), jnp.broadcast_to(meta[None], (n_batch, n_meta, d)), x], axis=1
    ).reshape(rows, d)

    lam_re, lam_im = small["s5_lam_re"][0], small["s5_lam_im"][0]
    log_dt = small["s5_log_dt"][0][:, None]
    b_re_t = small["s5_b_re"][0].transpose(0, 2, 1)
    b_im_t = small["s5_b_im"][0].transpose(0, 2, 1)
    c_re, c_im = small["s5_c_re"][0], small["s5_c_im"][0]
    lbr, lbi, bbr, bbi = _s5_params("s5_params", lam_re, lam_im, log_dt, b_re_t, b_im_t)
    b_blk = _interleave(_block_diag(bbr), _block_diag(bbi), sw).astype(BF16)
    c_blk = _interleave(_block_diag(c_re), _block_diag(-c_im), sw).T.astype(BF16)
    lam_row = _interleave(lbr.reshape(1, ns), lbi.reshape(1, ns), sw)
    d_row = small["s5_d"].reshape(1, s5w)

    def ln_store(outs, acc, res, bias, g, b):
        r = ALPHA * res + acc + bias
        outs[0][...] = r
        if len(outs) > 1:
            h = _ln(r, g, b)
            outs[1][...] = h
            outs[2][...] = h.astype(BF16)

    zero_bias = jnp.zeros((1, d), F32)

    def mix_ln(name, a, w, k_total, tk, res, bias, g, b, a_fn=None, emit_h=True):
        dtypes = (F32, F32, BF16) if emit_h else (F32,)
        return _mm_act(name, a, w, "nat", n_out_cols=d, k_total=k_total, tn=d, tk=tk, a_fn=a_fn,
                       extras=(res, bias, g, b), extra_specs=(_row_spec(tm, d), _vec_spec(d), _vec_spec(d), _vec_spec(d)),
                       store=ln_store, out_shape=[jax.ShapeDtypeStruct((rows, d), t) for t in dtypes],
                       out_specs=[_row_spec(tm, d)] * len(dtypes))

    def two(width):
        return [jax.ShapeDtypeStruct((rows, width), F32)] * 2, [_row_spec(tm, width)] * 2

    def shard_tile(total, shard, cap=1024):
        t = max(shard, cap - cap % shard)
        while total % t:
            t -= shard
        return t

    h0b = h0.astype(BF16)
    proj_ab = _mm_act("in_ab", h0b, wts["in_ab"], "stk", n_out_cols=ab_cols, k_total=d, tn=shard_tile(ab_cols, n_ab), tk=d)[0]
    bu = _mm_act("s5_bu", proj_ab, b_blk, "nat", n_out_cols=2 * ns, k_total=s5w, tn=min(2 * ns, 2048), tk=s5w)[0]
    states = _s5_scan("s5_scan", bu, lam_row, n_batch, lp, sw)

    def gelu_store(outs, acc, u, dv):
        ypre = acc + dv * u
        outs[0][...] = ypre
        outs[1][...] = jax.nn.gelu(ypre)

    shp2, spec2 = two(s5w)
    ypre, y = _mm_act(
        "s5_y", states, c_blk, "nat", n_out_cols=s5w, k_total=2 * ns, tn=s5w, tk=min(2 * ns, 1024),
        extras=(proj_ab, d_row), extra_specs=(_row_spec(tm, s5w), _vec_spec(s5w)), store=gelu_store,
        out_shape=shp2, out_specs=spec2)

    def glu_store(outs, acc, yv, bias):
        gate = acc + bias
        outs[0][...] = gate
        outs[1][...] = _glu(yv, gate)

    gate, a_out = _mm_act(
        "s5_glu", y, wts["glu"], "nat", n_out_cols=s5w, k_total=s5w, tn=s5w, tk=s5w,
        extras=(y, small["s5_b_glu"]), extra_specs=(_row_spec(tm, s5w), _vec_spec(s5w)), store=glu_store,
        out_shape=shp2, out_specs=spec2)
    b_out = _attn_fwd("sb_attn", proj_ab, n_batch, lp, pad, s5_cb, s5_cb + sb_cb, s5_cb + 2 * sb_cb, n_pairs)

    def bias_store(outs, acc, bias):
        outs[0][...] = (acc + bias).astype(outs[0].dtype)

    def wide(width, dtype):
        return [jax.ShapeDtypeStruct((rows, dff), dtype)], [_row_spec(tm, width)]

    def mlp_fwd(layer, h_in, h_in_b, emit_h=True):
        tn = shard_tile(dff, n_up)
        shp, spec = wide(tn, BF16)
        up = _mm_act(f"up{layer}", h_in_b, wts["up"][layer], "stk", n_out_cols=dff, k_total=d, tn=tn, tk=d,
                     extras=(small["mlp_b_up"][layer:layer + 1],), extra_specs=(_vec_spec(tn),), store=bias_store,
                     out_shape=shp, out_specs=spec)[0]
        return (up, *mix_ln(f"down{layer}", up, wts["down"][layer], dff, min(dff, 1024), h_in,
                            small["mlp_b_down"][layer:layer + 1], small["ln_mlp_g"][layer:layer + 1],
                            small["ln_mlp_b"][layer:layer + 1], a_fn=_relu2, emit_h=emit_h))

    r1, h1, h1b = mix_ln("out_ab", [a_out, b_out], wts["out_ab"], s5w + sbw, min(s5w, sbw), h0, zero_bias,
                         small["ln_mix_g"][0:1], small["ln_mix_b"][0:1])
    wts = {**wts, **late_weights(r1)}
    n_c = wts["in_c"].shape[2]
    n_up = wts["up"][0].shape[2]
    up0, r2, h2, h2b = mlp_fwd(0, h1, h1b)

    lb = _lower_bound("hg_lb", small["hgrn_gamma"])
    proj_c = _mm_act("in_c", h2b, wts["in_c"], "stk", n_out_cols=4 * d, k_total=d, tn=shard_tile(4 * d, n_c), tk=d)[0]
    c_out, hg_states = _hgrn_fwd("hgrn", proj_c, lb, wts["ng"], n_batch, lp, pad, n_hg)
    r3, h3, h3b = mix_ln("out_c", c_out, wts["out_c"], d, d, h2, zero_bias, small["ln_mix_g"][1:2], small["ln_mix_b"][1:2])
    up1, r4 = mlp_fwd(1, h3, h3b, emit_h=False)

    gr = {}
    g_r4, gr["ln_mlp_g1"], gr["ln_mlp_b1"], loss_tile, g_r4b = _loss_grad(
        "loss", r4, small["ln_mlp_g"][1:2], small["ln_mlp_b"][1:2], target, n_batch, lp, lead)

    def res_store(outs, acc, g_res):
        outs[0][...] = acc + ALPHA * g_res

    def ln_bwd_store(outs, acc, g_res, r_in, g, b, first_step):
        gr_in, gg, gb = jax.vjp(_ln, r_in, g, b)[1](acc + ALPHA * g_res)
        outs[0][...] = gr_in
        outs[3][...] = gr_in.astype(BF16)

        @pl.when(first_step)
        def _():
            outs[1][...] = jnp.zeros_like(outs[1])
            outs[2][...] = jnp.zeros_like(outs[2])

        outs[1][...] += gg
        outs[2][...] += gb

    def through_ln(name, a, w, k_total, tk, g_res, r_in, g, b, dep=None):
        vec = pl.BlockSpec((1, d), lambda i, j, k: (0, 0))
        return _mm_act(name, a, w, "stkT", n_out_cols=d, k_total=k_total, tn=d, tk=tk,
                       extras=(g_res, r_in, g, b), extra_specs=(_row_spec(tm, d), _row_spec(tm, d), vec, vec),
                       store=ln_bwd_store, sequential=True, dep=dep,
                       out_shape=[jax.ShapeDtypeStruct((rows, d), F32)] + [jax.ShapeDtypeStruct((1, d), F32)] * 2
                       + [jax.ShapeDtypeStruct((rows, d), BF16)],
                       out_specs=[_row_spec(tm, d), vec, vec, _row_spec(tm, d)])

    def mlp_bwd(layer, g_r, g_rb, up, h_in, r_in, send=None):
        def gup_store(outs, acc, upv):
            outs[0][...] = (acc * (2.0 * jnp.maximum(upv.astype(F32), 0.0))).astype(outs[0].dtype)

        tf = min(dff, 1024)
        shp, spec = wide(tf, BF16)
        g_up = _mm_act(f"g_up{layer}", g_rb, wts["down"][layer], "natT", n_out_cols=dff, k_total=d, tn=tf, tk=d,
                       extras=(up,), extra_specs=(_row_spec(tm, tf),), store=gup_store, out_shape=shp, out_specs=spec)[0]
        gr[f"down{layer}"], gr[f"mlp_b_down{layer}"] = _mm_wgrad(
            f"dw_down{layer}", up, g_rb, kw=dff, n=d, tmw=tf, tn=d, a_fn=_relu2, out_dtype=BF16, colsum=True)
        gr[f"up{layer}"], gr[f"mlp_b_up{layer}"] = _mm_wgrad(
            f"dw_up{layer}", h_in, g_up, kw=d, n=dff, tmw=d, tn=min(dff, 2048), shard_cols=n_up, out_dtype=BF16, colsum=True)
        dep = send() if send is not None else None
        g_r_in, gr[f"ln_mix_g{layer}"], gr[f"ln_mix_b{layer}"], g_r_in_b = through_ln(
            f"g_hmid{layer}", g_up, wts["up"][layer], dff, shard_tile(dff, n_up), g_r, r_in,
            small["ln_mix_g"][layer:layer + 1], small["ln_mix_b"][layer:layer + 1], dep=dep)
        return g_r_in, g_r_in_b

    g_r3, g_r3b = mlp_bwd(1, g_r4, g_r4b, up1, h3b, r3)
    g_cout = _mm_act("g_cout", g_r3b, wts["out_c"], "natT", n_out_cols=d, k_total=d, tn=d, tk=d)[0]
    gr["out_c"] = _mm_wgrad("dw_out_c", c_out, g_r3b, kw=d, n=d, tmw=d, tn=d, out_dtype=BF16)
    g_pc, g_lb_parts, g_ng_parts = _hgrn_bwd("hgrn_bwd", proj_c, lb, wts["ng"], g_cout, hg_states, n_batch, lp, pad, n_hg)
    gr["hgrn_gamma"], gr["ng"] = _lower_bound_bwd("hg_lb_bwd", small["hgrn_gamma"], g_lb_parts, g_ng_parts)
    gr["in_c"] = _mm_wgrad("dw_in_c", h2b, g_pc, kw=d, n=4 * d, tmw=d, tn=min(4 * d, 2048), shard_cols=n_c, out_dtype=BF16)
    sent1 = on_grads(1, {"down1": gr["down1"], "up1": gr["up1"], "out_c": gr["out_c"], "in_c": gr["in_c"], "ng": gr["ng"]})
    g_r2, gr["ln_mlp_g0"], gr["ln_mlp_b0"], g_r2b = through_ln(
        "g_h2", g_pc, wts["in_c"], 4 * d, shard_tile(4 * d, n_c), g_r3, r2, small["ln_mlp_g"][0:1], small["ln_mlp_b"][0:1],
        dep=sent1)

    g_r1, g_r1b = mlp_bwd(0, g_r2, g_r2b, up0, h1b, r1, send=lambda: on_grads(2, {"down0": gr["down0"], "up0": gr["up0"]}))
    g_cat = _mm_act("g_cat", g_r1b, wts["out_ab"], "natT", n_out_cols=d, k_total=d, tn=d, tk=d)[0]
    gr["out_ab"] = _mm_wgrad("dw_out_ab", [a_out, b_out], g_r1b, kw=s5w + sbw, n=d, tmw=min(s5w, sbw), tn=d, out_dtype=BF16)
    g_q, g_k, g_v = _attn_bwd("sb_attn_bwd", proj_ab, g_cat, n_batch, lp, pad, s5_cb, s5_cb + sb_cb, s5_cb + 2 * sb_cb,
                              s5_cb, n_pairs)

    g_y_direct, g_gate = _rowwise("s5_glu_bwd", lambda ga, yv, gt: jax.vjp(_glu, yv, gt)[1](ga),
                                  [(g_cat, 0, s5w), (y, 0, s5w), (gate, 0, s5w)], 2, s5w)

    def gelu_bwd_store(outs, acc, gyd, yp, u, dv):
        gyp = jax.vjp(jax.nn.gelu, yp)[1](acc + gyd)[0]
        outs[0][...] = gyp
        outs[1][...] = dv * gyp
        outs[2][...] = jnp.sum(gyp * u, axis=0, keepdims=True)

    rs = _row_spec(tm, s5w)
    g_ypre, g_u_direct, gd_parts = _mm_act(
        "s5_g_y", g_gate, wts["glu"], "natT", n_out_cols=s5w, k_total=s5w, tn=s5w, tk=s5w,
        extras=(g_y_direct, ypre, proj_ab, d_row), extra_specs=(rs, rs, rs, _vec_spec(s5w)), store=gelu_bwd_store,
        out_shape=[jax.ShapeDtypeStruct((rows, s5w), F32)] * 2 + [jax.ShapeDtypeStruct((rows // tm, 1, s5w), F32)],
        out_specs=[rs, rs, pl.BlockSpec((None, 1, s5w), lambda i, j, k: (i, 0, j))])
    gr["glu"], gr["s5_b_glu"] = _mm_wgrad("dw_glu", y, g_gate, kw=s5w, n=s5w, tmw=s5w, tn=s5w, out_dtype=BF16, colsum=True)
    g_sd = _mm_act("s5_g_states", g_ypre, c_blk, "natT", n_out_cols=2 * ns, k_total=s5w, tn=min(2 * ns, 2048), tk=s5w)[0]
    d_cblk = _mm_wgrad("dw_cblk", states, g_ypre, kw=2 * ns, n=s5w, tmw=min(2 * ns, 1024), tn=s5w)
    gs, gl_parts = _s5_scan_bwd("s5_scan_bwd", g_sd, states, lam_row, n_batch, lp, sw)

    def add_store(outs, acc, other):
        outs[0][...] = acc + other

    g_u = _mm_act("s5_g_u", gs, b_blk, "natT", n_out_cols=s5w, k_total=2 * ns, tn=s5w, tk=min(2 * ns, 1024),
                  extras=(g_u_direct,), extra_specs=(rs,), store=add_store)[0]
    d_bblk = _mm_wgrad("dw_bblk", proj_ab, gs, kw=s5w, n=2 * ns, tmw=s5w, tn=min(2 * ns, 2048))
    db_re, db_im = _deinterleave(d_bblk, sw)
    dc_re, dc_im = _deinterleave(d_cblk.T, sw)
    glr, gli = _deinterleave(gl_parts, sw)
    g_lam_re, g_lam_im, g_log_dt, g_b_re_t, g_b_im_t, g_d = _s5_params_bwd(
        "s5_params_bwd", lam_re, lam_im, log_dt, b_re_t, b_im_t,
        glr.reshape(n_batch, groups, n_state), gli.reshape(n_batch, groups, n_state),
        _diag_blocks(db_re, groups), _diag_blocks(db_im, groups), gd_parts)

    cat2 = lambda key: jnp.concatenate([gr[key + "0"], gr[key + "1"]], axis=0)
    small_sent = on_small({
        "s5_lam_re": g_lam_re[None], "s5_lam_im": g_lam_im[None], "s5_log_dt": g_log_dt.reshape(1, groups),
        "s5_b_re": g_b_re_t.transpose(0, 2, 1)[None], "s5_b_im": g_b_im_t.transpose(0, 2, 1)[None],
        "s5_c_re": _diag_blocks(dc_re, groups)[None], "s5_c_im": -_diag_blocks(dc_im, groups)[None],
        "s5_d": g_d.reshape(1, groups, grp), "s5_b_glu": gr["s5_b_glu"], "hgrn_gamma": gr["hgrn_gamma"],
        "ln_mix_g": cat2("ln_mix_g"), "ln_mix_b": cat2("ln_mix_b"), "mlp_b_up": cat2("mlp_b_up"),
        "mlp_b_down": cat2("mlp_b_down"), "ln_mlp_g": cat2("ln_mlp_g"), "ln_mlp_b": cat2("ln_mlp_b"),
    }, loss_tile)

    g_pab = [g_u, g_q, g_k, g_v]
    assert s5w == sbw
    gr["in_ab"] = _mm_wgrad("dw_in_ab", h0b, g_pab, kw=d, n=ab_cols, tmw=d, tn=s5w, shard_cols=n_ab, out_dtype=BF16,
                            dep=small_sent)
    g_h0 = _mm_act("g_h0", g_pab, wts["in_ab"], "stkT", n_out_cols=d, k_total=ab_cols, tn=d, tk=shard_tile(s5w, n_ab),
                   extras=(g_r1,), extra_specs=(_row_spec(tm, d),), store=res_store)[0]
    grad_x = g_h0.reshape(n_batch, lp, d)[:, lead:, :]
    g_meta = _meta_grad("g_meta", g_h0, n_batch, lp, pad, n_meta)
    on_grads(3, {"meta": g_meta, "in_ab": gr["in_ab"], "glu": gr["glu"], "out_ab": gr["out_ab"]})
    return grad_x


SMALL_NAMES = ("s5_lam_re", "s5_lam_im", "s5_log_dt", "s5_b_re", "s5_b_im", "s5_c_re", "s5_c_im", "s5_d", "s5_b_glu",
               "hgrn_gamma", "ln_mix_g", "ln_mix_b", "mlp_b_up", "mlp_b_down", "ln_mlp_g", "ln_mlp_b")
WEIGHT_ORDER = ("meta", "w_in_ab", "s5_lam_re", "s5_lam_im", "s5_log_dt", "s5_b_re", "s5_b_im", "s5_c_re", "s5_c_im",
                "s5_d", "s5_w_glu", "s5_b_glu", "w_out_ab", "w_in_c", "hgrn_gamma", "hgrn_norm_g", "w_out_c", "ln_mix_g",
                "ln_mix_b", "mlp_w_up", "mlp_b_up", "mlp_w_down", "mlp_b_down", "ln_mlp_g", "ln_mlp_b")


def kernel(x, meta, w_in_ab, s5_lam_re, s5_lam_im, s5_log_dt, s5_b_re, s5_b_im, s5_c_re, s5_c_im, s5_d, s5_w_glu, s5_b_glu, w_out_ab, w_in_c, hgrn_gamma, hgrn_norm_g, w_out_c, ln_mix_g, ln_mix_b, mlp_w_up, mlp_b_up, mlp_w_down, mlp_b_down, ln_mlp_g, ln_mlp_b, loss_target, m_meta, m_w_in_ab, m_s5_lam_re, m_s5_lam_im, m_s5_log_dt, m_s5_b_re, m_s5_b_im, m_s5_c_re, m_s5_c_im, m_s5_d, m_s5_w_glu, m_s5_b_glu, m_w_out_ab, m_w_in_c, m_hgrn_gamma, m_hgrn_norm_g, m_w_out_c, m_ln_mix_g, m_ln_mix_b, m_mlp_w_up, m_mlp_b_up, m_mlp_w_down, m_mlp_b_down, m_ln_mlp_g, m_ln_mlp_b, v_meta, v_w_in_ab, v_s5_lam_re, v_s5_lam_im, v_s5_log_dt, v_s5_b_re, v_s5_b_im, v_s5_c_re, v_s5_c_im, v_s5_d, v_s5_w_glu, v_s5_b_glu, v_w_out_ab, v_w_in_c, v_hgrn_gamma, v_hgrn_norm_g, v_w_out_c, v_ln_mix_g, v_ln_mix_b, v_mlp_w_up, v_mlp_b_up, v_mlp_w_down, v_mlp_b_down, v_ln_mlp_g, v_ln_mlp_b):
    args = dict(locals())
    w = {n: args[n] for n in WEIGHT_ORDER}
    mom = {n: args["m_" + n] for n in WEIGHT_ORDER}
    var = {n: args["v_" + n] for n in WEIGHT_ORDER}
    d = x.shape[2]
    n_meta = meta.shape[0]

    cast = lambda a: a.astype(BF16)
    early = _exchange_start("gather_early_start", [w["meta"], cast(w["w_in_ab"][0]), cast(w["s5_w_glu"][0]),
                                                   cast(w["w_out_ab"][0])], False)
    late = _exchange_start("gather_late_start", [w["hgrn_norm_g"], cast(w["w_in_c"][0]), cast(w["w_out_c"][0]),
                                                 cast(w["mlp_w_up"][0]), cast(w["mlp_w_up"][1]),
                                                 cast(w["mlp_w_down"][0]), cast(w["mlp_w_down"][1])], False, dep=early["token"])
    a_meta, a_in_ab, a_glu, a_out_ab = _exchange_wait("gather_early_wait", early, late["token"])
    wts = {"in_ab": a_in_ab, "glu": a_glu.reshape(-1, a_glu.shape[2]), "out_ab": a_out_ab.reshape(-1, d)}
    meta_full = a_meta.transpose(1, 0, 2).reshape(n_meta, d)
    small = {n: w[n] for n in SMALL_NAMES}

    def late_weights(after):
        a_ng, a_in_c, a_out_c, a_up0, a_up1, a_dn0, a_dn1 = _exchange_wait("gather_late_wait", late, after)
        return {"in_c": a_in_c, "ng": a_ng.transpose(1, 0, 2).reshape(1, d), "out_c": a_out_c.reshape(-1, d),
                "up": [a_up0, a_up1], "down": [a_dn0.reshape(-1, d), a_dn1.reshape(-1, d)]}

    n_loc = d // N_DEV
    rows_of = lambda g: g.reshape(N_DEV, -1, g.shape[-1])
    cols_of = lambda g: g.reshape(g.shape[0], N_DEV, n_loc).transpose(1, 0, 2)
    sent = {}

    def on_grads(stage, g):
        if stage == 1:
            order = (("mlp_w_down", 1), ("mlp_w_up", 1), ("w_out_c", 0), ("w_in_c", 0), ("hgrn_norm_g", None))
            parts = [rows_of(g["down1"]), g["up1"], rows_of(g["out_c"]), g["in_c"], cols_of(g["ng"])]
        elif stage == 2:
            order = (("mlp_w_down", 0), ("mlp_w_up", 0))
            parts = [rows_of(g["down0"]), g["up0"]]
        else:
            order = (("w_out_ab", 0), ("s5_w_glu", 0), ("w_in_ab", 0), ("meta", None))
            parts = [rows_of(g["out_ab"]), rows_of(g["glu"]), g["in_ab"], cols_of(g["meta"])]
        sent[stage] = (order, _exchange_start(f"scatter_start{stage}", parts, True))
        return sent[stage][1]["token"]

    def on_small(sg, loss_tile):
        g_pack = _pack_rows([sg[n] for n in SMALL_NAMES] + [loss_tile], PACK_COLS)
        sent["small"] = _exchange_start("gather_small_start", [g_pack], False)
        return sent["small"]["token"]

    grad_x = _local_step(x, loss_target, meta_full, wts, small, late_weights, on_grads, on_small)
    small_sent = sent["small"]
    tile = (SUBLANES, LANES)
    shapes = [w[n].shape for n in SMALL_NAMES] + [tile]
    zeros = jnp.zeros(tile, F32)
    w_pack = _pack_rows([w[n] for n in SMALL_NAMES] + [zeros], PACK_COLS)
    m_pack = _pack_rows([mom[n] for n in SMALL_NAMES] + [zeros], PACK_COLS)
    v_pack = _pack_rows([var[n] for n in SMALL_NAMES] + [zeros], PACK_COLS)

    received, res = {}, {}

    def wait(stage, after):
        order, handle = sent[stage]
        for key, rc in zip(order, _exchange_wait(f"scatter_wait{stage}", handle, after)):
            received[key] = rc

    def update(nm):
        layered = w[nm].ndim == 3
        parts = [received[(nm, l)] for l in range(w[nm].shape[0])] if layered else received[(nm, None)]
        res[nm] = _adamw_summed(f"adamw_{nm}", parts, w[nm], mom[nm], var[nm])
        return res[nm][0]

    wait(1, sent[3][1]["token"])
    done = [update(nm) for nm in ("w_out_c", "w_in_c", "hgrn_norm_g")]
    wait(2, done[0])
    done = [update(nm) for nm in ("mlp_w_up", "mlp_w_down")]
    g_all = _exchange_wait("gather_small_wait", small_sent, done[0])[0]
    packed = _adamw_summed("adamw_small", g_all, w_pack, m_pack, v_pack)
    wait(3, packed[0])
    for nm in ("w_out_ab", "s5_w_glu", "w_in_ab", "meta"):
        update(nm)
    unpacked = [_unpack_rows(p, shapes, PACK_COLS) for p in packed]
    loss = unpacked[0][-1][0, 0]

    def pick(nm, which):
        return unpacked[which][SMALL_NAMES.index(nm)] if nm in SMALL_NAMES else res[nm][which]

    return (loss, grad_x, *[pick(n, 0) for n in WEIGHT_ORDER], *[pick(n, 1) for n in WEIGHT_ORDER],
            *[pick(n, 2) for n in WEIGHT_ORDER], *[pick(n, 3) for n in WEIGHT_ORDER])
```

```python
import functools
import math

import jax
import jax.numpy as jnp
from jax import lax
from jax.experimental import pallas as pl
from jax.experimental.pallas import tpu as pltpu

F32 = jnp.float32
BF16 = jnp.bfloat16

N_DEV = 8
DEPTH = 2
ALPHA = (2.0 * DEPTH) ** 0.25
LN_EPS = 1e-5
RMS_EPS = 1e-6
SB_HEAD_DIM = 64
HG_DK = 128
HG_CHUNK = 64
LANES = 128
SUBLANES = 8
PACKED_ROWS = 16
VMEM_LIMIT_BYTES = 56 * 1024 * 1024
ROW_TILE = 1088
SCAN_LANES = 256
SCAN_UNROLL = 2
PACK_COLS = 1024

ADAM_LR = 0.001
ADAM_B1 = 0.9
ADAM_B2 = 0.999
ADAM_EPS = 1e-08
ADAM_WD = 0.01
ADAM_STEP = 10

NN = (((1,), (0,)), ((), ()))
NT = (((1,), (1,)), ((), ()))
TN = (((0,), (0,)), ((), ()))


def _tile(n, pref, align=SUBLANES):
    t = min(n, pref)
    t -= t % align
    while t >= align:
        if n % t == 0:
            return t
        t -= align
    return n


def _unrolled_loop(n, body, init, unroll):
    assert n % unroll == 0

    def outer(t, carry):
        for u in range(unroll):
            carry = body(t * unroll + u, carry)
        return carry

    return lax.fori_loop(0, n // unroll, outer, init)


def _params(sem):
    return pltpu.CompilerParams(dimension_semantics=sem, vmem_limit_bytes=VMEM_LIMIT_BYTES)


def _dot_raw(a, b, dims):
    return lax.dot_general(a.astype(BF16), b.astype(BF16), dims, preferred_element_type=F32)


def _make_dot(dims, da_rule, db_rule):
    @jax.custom_vjp
    def f(a, b):
        return _dot_raw(a, b, dims)

    def fwd(a, b):
        return _dot_raw(a, b, dims), (a, b)

    def bwd(res, g):
        a, b = res
        return da_rule(g, a, b), db_rule(g, a, b)

    f.defvjp(fwd, bwd)
    return f


_DOTS = {
    NN: _make_dot(NN, lambda g, a, b: _dot_raw(g, b, NT), lambda g, a, b: _dot_raw(a, g, TN)),
    NT: _make_dot(NT, lambda g, a, b: _dot_raw(g, b, NN), lambda g, a, b: _dot_raw(g, a, TN)),
    TN: _make_dot(TN, lambda g, a, b: _dot_raw(b, g, NT), lambda g, a, b: _dot_raw(a, g, NN)),
}


def _dot(a, b, dims):
    return _DOTS[dims](a, b)


def _running_sums(a, tri_ones, split=False):
    hi = a.astype(BF16)
    out = lax.dot_general(hi, tri_ones, NN, preferred_element_type=F32)
    if split:
        lo = (a - hi.astype(F32)).astype(BF16)
        out = out + lax.dot_general(lo, tri_ones, NN, preferred_element_type=F32)
    return out


def _piece_specs(pieces, block_rows, block_cols, row_of, col_of, cb0):
    per = pieces[0].shape[1] // block_cols if len(pieces) > 1 else None
    specs = []
    for p in range(len(pieces)):
        if per is None:
            specs.append(pl.BlockSpec((block_rows, block_cols), lambda *g: (row_of(*g), cb0 + col_of(*g))))
        else:
            specs.append(pl.BlockSpec(
                (block_rows, block_cols),
                lambda *g, p=p: (row_of(*g), jnp.clip(col_of(*g) - p * per, 0, per - 1))))
    return specs, per


def _mm_call(name, grid, dims, a_pieces, a_specs, a_sel, b_pieces, b_specs, b_sel, extras, extra_specs,
             out_shape, out_specs, acc_shape, a_fn, store, colsum_width=0, sequential=False, deps=()):
    na, nb, ne, no, nd = len(a_pieces), len(b_pieces), len(extras), len(out_shape), len(deps)
    nk = grid[2]

    def body(*refs):
        a_refs, b_refs = refs[:na], refs[na:na + nb]
        extra = refs[na + nb:na + nb + ne]
        outs = refs[na + nb + ne + nd:na + nb + ne + nd + no]
        acc = refs[na + nb + ne + nd + no]
        ids = (pl.program_id(0), pl.program_id(1), pl.program_id(2))
        k = ids[2]

        @pl.when(k == 0)
        def _():
            acc[...] = jnp.zeros_like(acc)

        def run(a_ref, b_ref):
            a = a_ref[...]
            if a_fn is not None:
                a = a_fn(a)
            b = b_ref[...]
            if b.ndim == 3 and dims == NN:
                n = b.shape[2]
                for q in range(b.shape[0]):
                    acc[:, q * n:(q + 1) * n] += _dot_raw(a, b[q], dims)
            elif b.ndim == 3:
                n = b.shape[2]
                for q in range(b.shape[0]):
                    acc[...] += _dot_raw(a[:, q * n:(q + 1) * n], b[q], dims)
            else:
                acc[...] += _dot_raw(a, b, dims)
            if colsum_width:
                cs = refs[-1]
                first = ids[1] == 0

                @pl.when(first & (k == 0))
                def _():
                    cs[...] = jnp.zeros_like(cs)

                @pl.when(first)
                def _():
                    cs[...] += jnp.sum(b.astype(F32), axis=0, keepdims=True)

        if na == 1 and nb == 1:
            run(a_refs[0], b_refs[0])
        elif nb == 1:
            per, fn = a_sel
            which = fn(*ids) // per
            for p in range(na):
                pl.when(which == p)(functools.partial(run, a_refs[p], b_refs[0]))
        else:
            assert na == 1
            per, fn = b_sel
            which = fn(*ids) // per
            for p in range(nb):
                pl.when(which == p)(functools.partial(run, a_refs[0], b_refs[p]))

        @pl.when(k == nk - 1)
        def _():
            if sequential:
                store(outs, acc[...], *[e[...] for e in extra], first_step=(ids[0] == 0) & (ids[1] == 0))
            else:
                store(outs, acc[...], *[e[...] for e in extra])
            if colsum_width:
                @pl.when(ids[1] == 0)
                def _():
                    outs[-1][...] = refs[-1][...]

    scratch = [pltpu.VMEM(acc_shape, F32)]
    if colsum_width:
        scratch.append(pltpu.VMEM((1, colsum_width), F32))
    sem = ("parallel", "arbitrary", "arbitrary") if colsum_width else ("parallel", "parallel", "arbitrary")
    if sequential:
        sem = ("arbitrary",) * 3
    return pl.pallas_call(
        body, name=name, grid=grid,
        in_specs=[*a_specs, *b_specs, *extra_specs, *[pl.BlockSpec(memory_space=pl.ANY)] * nd], out_specs=out_specs,
        out_shape=out_shape, scratch_shapes=scratch, compiler_params=_params(sem),
    )(*a_pieces, *b_pieces, *extras, *deps)


def _store_plain(outs, acc):
    outs[0][...] = acc.astype(outs[0].dtype)


def _row_spec(tm, tn):
    return pl.BlockSpec((tm, tn), lambda i, j, k: (i, j))


def _vec_spec(tn):
    return pl.BlockSpec((1, tn), lambda i, j, k: (0, j))


def _mm_act(name, a, w, wkind, *, n_out_cols, k_total, tn, tk, a_cb0=0, a_fn=None, extras=(), extra_specs=(),
            store=_store_plain, out_shape=None, out_specs=None, sequential=False, dep=None):
    a_pieces = list(a) if isinstance(a, (list, tuple)) else [a]
    rows = a_pieces[0].shape[0]
    tm = _tile(rows, ROW_TILE)
    grid = (rows // tm, n_out_cols // tn, k_total // tk)
    a_specs, per = _piece_specs(a_pieces, tm, tk, lambda i, j, k: i, lambda i, j, k: k, a_cb0)
    if wkind == "nat":
        b_spec, dims = pl.BlockSpec((tk, tn), lambda i, j, k: (k, j)), NN
    elif wkind == "stk":
        n = w.shape[2]
        assert tn % n == 0
        b_spec, dims = pl.BlockSpec((tn // n, tk, n), lambda i, j, k: (j, k, 0)), NN
    elif wkind == "natT":
        b_spec, dims = pl.BlockSpec((tn, tk), lambda i, j, k: (j, k)), NT
    else:
        n = w.shape[2]
        assert wkind == "stkT" and tk % n == 0
        b_spec, dims = pl.BlockSpec((tk // n, tn, n), lambda i, j, k: (k, j, 0)), NT
    if out_shape is None:
        out_shape = [jax.ShapeDtypeStruct((rows, n_out_cols), F32)]
        out_specs = [_row_spec(tm, tn)]
    return _mm_call(name, grid, dims, a_pieces, a_specs, (per, lambda i, j, k: k), [w], [b_spec], None,
                    list(extras), list(extra_specs), out_shape, out_specs, (tm, tn), a_fn, store,
                    sequential=sequential, deps=() if dep is None else (dep,))


def _mm_wgrad(name, a, g, *, kw, n, tmw, tn, a_cb0=0, a_fn=None, shard_cols=0, out_dtype=F32, colsum=False, dep=None):
    a_pieces = list(a) if isinstance(a, (list, tuple)) else [a]
    g_pieces = list(g) if isinstance(g, (list, tuple)) else [g]
    rows = a_pieces[0].shape[0]
    tr = _tile(rows, ROW_TILE)
    grid = (n // tn, kw // tmw, rows // tr)
    a_specs, a_per = _piece_specs(a_pieces, tr, tmw, lambda j, i, k: k, lambda j, i, k: i, a_cb0)
    g_specs, g_per = _piece_specs(g_pieces, tr, tn, lambda j, i, k: k, lambda j, i, k: j, 0)
    if shard_cols:
        per = tn // shard_cols
        out_shape = [jax.ShapeDtypeStruct((n // shard_cols, kw, shard_cols), out_dtype)]
        out_specs = [pl.BlockSpec((per, tmw, shard_cols), lambda j, i, k: (j, i, 0))]

        def store(outs, acc):
            for q in range(per):
                outs[0][q] = acc[:, q * shard_cols:(q + 1) * shard_cols].astype(out_dtype)
    else:
        out_shape = [jax.ShapeDtypeStruct((kw, n), out_dtype)]
        out_specs = [pl.BlockSpec((tmw, tn), lambda j, i, k: (i, j))]

        def store(outs, acc):
            outs[0][...] = acc.astype(out_dtype)
    if colsum:
        out_shape.append(jax.ShapeDtypeStruct((1, n), F32))
        out_specs.append(pl.BlockSpec((1, tn), lambda j, i, k: (0, j)))
    res = _mm_call(name, grid, TN, a_pieces, a_specs, (a_per, lambda j, i, k: i), g_pieces, g_specs,
                   (g_per, lambda j, i, k: j), [], [], out_shape, out_specs, (tmw, tn), a_fn, store,
                   colsum_width=tn if colsum else 0, deps=() if dep is None else (dep,))
    return res if colsum else res[0]


def _ln(x, g, b):
    mu = jnp.mean(x, axis=-1, keepdims=True)
    xc = x - mu
    var = jnp.mean(xc * xc, axis=-1, keepdims=True)
    return xc * lax.rsqrt(var + LN_EPS) * g + b


def _relu2(x):
    r = jnp.maximum(x.astype(F32), 0.0)
    return r * r


def _glu(y, gate):
    return y * jax.nn.sigmoid(gate)


def _rowwise(name, fn, ins, n_out, width):
    rows = ins[0][0].shape[0]
    tm = _tile(rows, ROW_TILE)

    def body(*refs):
        res = fn(*[r[...] for r in refs[:len(ins)]])
        for o, v in zip(refs[len(ins):], res):
            o[...] = v

    return pl.pallas_call(
        body, name=name, grid=(rows // tm,),
        in_specs=[pl.BlockSpec((tm, wd), lambda i, cb=cb: (i, cb)) for _, cb, wd in ins],
        out_specs=[pl.BlockSpec((tm, width), lambda i: (i, 0))] * n_out,
        out_shape=[jax.ShapeDtypeStruct((rows, width), F32)] * n_out, compiler_params=_params(("parallel",)),
    )(*[a for a, _, _ in ins])


def _loss_grad(name, r, g, b, target, n_batch, lp, lead):
    rows, d = r.shape
    nq = lp // LANES
    lead_blocks = lead // LANES

    def body(r_ref, g_ref, b_ref, t_ref, gr_ref, gg_ref, gb_ref, loss_ref, grb_ref):
        i = pl.program_id(1)

        @pl.when((pl.program_id(0) == 0) & (i == 0))
        def _():
            loss_ref[...] = jnp.zeros_like(loss_ref)
            gg_ref[...] = jnp.zeros_like(gg_ref)
            gb_ref[...] = jnp.zeros_like(gb_ref)

        h, vjp = jax.vjp(_ln, r_ref[...], g_ref[...], b_ref[...])
        diff = jnp.where(i >= lead_blocks, h - t_ref[...], 0.0)
        gr, gg, gb = vjp(diff * (1.0 / d))
        gr_ref[...] = gr
        grb_ref[...] = gr.astype(BF16)
        gg_ref[...] += gg
        gb_ref[...] += gb
        loss_ref[...] += 0.5 * jnp.sum(diff * diff) * (1.0 / d)

    vec = pl.BlockSpec((1, d), lambda b, i: (0, 0))
    row = pl.BlockSpec((LANES, d), lambda b, i: (b * nq + i, 0))
    return pl.pallas_call(
        body, name=name, grid=(n_batch, nq),
        in_specs=[row, vec, vec, pl.BlockSpec((None, LANES, d), lambda b, i: (b, jnp.maximum(i - lead_blocks, 0), 0))],
        out_specs=[row, vec, vec, pl.BlockSpec((SUBLANES, LANES), lambda b, i: (0, 0)), row],
        out_shape=[jax.ShapeDtypeStruct((rows, d), F32), jax.ShapeDtypeStruct((1, d), F32),
                   jax.ShapeDtypeStruct((1, d), F32), jax.ShapeDtypeStruct((SUBLANES, LANES), F32),
                   jax.ShapeDtypeStruct((rows, d), BF16)],
        compiler_params=_params(("arbitrary", "arbitrary")),
    )(r, g, b, target)


def _meta_grad(name, g_h0, n_batch, lp, pad, n_meta):
    d = g_h0.shape[1]
    per = lp // n_meta
    at = pad // n_meta

    def body(g_ref, o_ref):
        @pl.when(pl.program_id(0) == 0)
        def _():
            o_ref[...] = jnp.zeros_like(o_ref)

        o_ref[...] += g_ref[...]

    return pl.pallas_call(
        body, name=name, grid=(n_batch,),
        in_specs=[pl.BlockSpec((n_meta, d), lambda b: (b * per + at, 0))],
        out_specs=pl.BlockSpec((n_meta, d), lambda b: (0, 0)),
        out_shape=jax.ShapeDtypeStruct((n_meta, d), F32),
        compiler_params=_params(("arbitrary",)),
    )(g_h0)


def _s5_param_fn(lr, li, ldt, br, bi):
    dt = jnp.exp(ldt)
    e = jnp.exp(lr * dt)
    w = li * dt
    lbr = e * jnp.cos(w)
    lbi = e * jnp.sin(w)
    nr = lbr - 1.0
    den = lr * lr + li * li
    cr = (nr * lr + lbi * li) / den
    ci = (lbi * lr - nr * li) / den
    bbr = cr[:, None, :] * br - ci[:, None, :] * bi
    bbi = cr[:, None, :] * bi + ci[:, None, :] * br
    return lbr, lbi, bbr, bbi


def _s5_params(name, lr, li, ldt, br, bi):
    def body(lr_ref, li_ref, ldt_ref, br_ref, bi_ref, o1, o2, o3, o4):
        res = _s5_param_fn(lr_ref[...], li_ref[...], ldt_ref[...], br_ref[...], bi_ref[...])
        for o, v in zip((o1, o2, o3, o4), res):
            o[...] = v

    shp = [jax.ShapeDtypeStruct(lr.shape, F32)] * 2 + [jax.ShapeDtypeStruct(br.shape, F32)] * 2
    return pl.pallas_call(body, name=name, out_shape=shp)(lr, li, ldt, br, bi)


def _s5_params_bwd(name, lr, li, ldt, br, bi, g_lbr, g_lbi, g_bbr, g_bbi, gd_parts):
    def body(lr_ref, li_ref, ldt_ref, br_ref, bi_ref, g1, g2, g3, g4, gd_ref, o1, o2, o3, o4, o5, o6):
        _, vjp = jax.vjp(_s5_param_fn, lr_ref[...], li_ref[...], ldt_ref[...], br_ref[...], bi_ref[...])
        res = vjp((jnp.sum(g1[...], axis=0), jnp.sum(g2[...], axis=0), g3[...], g4[...]))
        for o, v in zip((o1, o2, o3, o4, o5), res):
            o[...] = v
        o6[...] = jnp.sum(gd_ref[...], axis=0)

    shp = ([jax.ShapeDtypeStruct(lr.shape, F32)] * 2 + [jax.ShapeDtypeStruct(ldt.shape, F32)]
           + [jax.ShapeDtypeStruct(br.shape, F32)] * 2 + [jax.ShapeDtypeStruct(gd_parts.shape[1:], F32)])
    return pl.pallas_call(body, name=name, out_shape=shp)(lr, li, ldt, br, bi, g_lbr, g_lbi, g_bbr, g_bbi, gd_parts)


def _interleave(re, im, w):
    nj = re.shape[-1] // w
    return jnp.concatenate([x[..., j * w:(j + 1) * w] for j in range(nj) for x in (re, im)], axis=-1)


def _deinterleave(x, w):
    nj = x.shape[-1] // (2 * w)
    return (jnp.concatenate([x[..., 2 * j * w:(2 * j + 1) * w] for j in range(nj)], axis=-1),
            jnp.concatenate([x[..., (2 * j + 1) * w:(2 * j + 2) * w] for j in range(nj)], axis=-1))


def _cmul(ar, ai, br, bi):
    return ar * br - ai * bi, ar * bi + ai * br


def _powers(lr, li):
    p = [(lr, li)]
    p.append(_cmul(*p[0], *p[0]))
    p.append(_cmul(*p[1], *p[0]))
    p.append(_cmul(*p[1], *p[1]))
    p.append(_cmul(*p[3], *p[0]))
    p.append(_cmul(*p[3], *p[1]))
    p.append(_cmul(*p[3], *p[2]))
    p.append(_cmul(*p[3], *p[3]))
    return p


def _scan_steps(pw, shifts, keep):
    return [(sh, jnp.where(m, pw[s - 1][0], 0.0), jnp.where(m, pw[s - 1][1], 0.0))
            for s, sh, m in zip((1, 2, 4), shifts, keep)]


def _scan_tile(xr, xi, steps):
    for sh, br, bi in steps:
        rr = pltpu.roll(xr, sh, 0)
        ri = pltpu.roll(xi, sh, 0)
        xr, xi = xr + (br * rr - bi * ri), xi + (br * ri + bi * rr)
    return xr, xi


def _s5_scan(name, bu, lam, n_batch, lp, w):
    rows, two_ns = bu.shape
    nj = two_ns // (2 * w)

    def body(x_ref, lam_ref, s_ref):
        pw = _powers(lam_ref[:, :w], lam_ref[:, w:])
        tab_r = jnp.concatenate([p[0] for p in pw], axis=0)
        tab_i = jnp.concatenate([p[1] for p in pw], axis=0)
        row = lax.broadcasted_iota(jnp.int32, (SUBLANES, w), 0)
        steps = _scan_steps(pw, (1, 2, 4), [row >= s for s in (1, 2, 4)])

        def packed_tile(t, carry):
            cr, ci = carry
            r0 = pl.multiple_of(t * PACKED_ROWS, PACKED_ROWS)
            x = x_ref[pl.ds(r0, PACKED_ROWS), :].astype(F32)
            done = []
            for half in range(PACKED_ROWS // SUBLANES):
                xt = x[half * SUBLANES:(half + 1) * SUBLANES, :]
                xr, xi = _scan_tile(xt[:, :w], xt[:, w:], steps)
                sr = xr + (tab_r * cr - tab_i * ci)
                si = xi + (tab_r * ci + tab_i * cr)
                done.append(jnp.concatenate([sr, si], axis=1))
                cr, ci = sr[SUBLANES - 1:, :], si[SUBLANES - 1:, :]
            s_ref[pl.ds(r0, PACKED_ROWS), :] = jnp.concatenate(done, axis=0).astype(s_ref.dtype)
            return cr, ci

        zero = jnp.zeros((1, w), F32)
        _unrolled_loop(lp // PACKED_ROWS, packed_tile, (zero, zero), SCAN_UNROLL)

    spec = pl.BlockSpec((lp, 2 * w), lambda b, j: (b, j))
    return pl.pallas_call(
        body, name=name, grid=(n_batch, nj), in_specs=[spec, pl.BlockSpec((1, 2 * w), lambda b, j: (0, j))],
        out_specs=spec, out_shape=jax.ShapeDtypeStruct((rows, two_ns), BF16),
        compiler_params=_params(("parallel", "parallel")),
    )(bu, lam)


def _s5_scan_bwd(name, gd, states, lam, n_batch, lp, w):
    rows, two_ns = gd.shape
    nj = two_ns // (2 * w)

    def body(x_ref, s_ref, lam_ref, g_ref, gl_ref):
        pw = _powers(lam_ref[:, :w], -lam_ref[:, w:])
        tab_r = jnp.concatenate([p[0] for p in reversed(pw)], axis=0)
        tab_i = jnp.concatenate([p[1] for p in reversed(pw)], axis=0)
        row = lax.broadcasted_iota(jnp.int32, (SUBLANES, w), 0)
        steps = _scan_steps(pw, [SUBLANES - s for s in (1, 2, 4)], [row < SUBLANES - s for s in (1, 2, 4)])

        n_packed = lp // PACKED_ROWS
        halves = PACKED_ROWS // SUBLANES

        def packed_tile(u, carry):
            cr, ci, ar, ai = carry
            t = n_packed - 1 - u
            r0 = pl.multiple_of(t * PACKED_ROWS, PACKED_ROWS)
            x = x_ref[pl.ds(r0, PACKED_ROWS), :].astype(F32)
            cur = s_ref[pl.ds(r0, PACKED_ROWS), :].astype(F32)
            p0 = pl.multiple_of(jnp.maximum(t - 1, 0) * PACKED_ROWS, PACKED_ROWS)
            before = s_ref[pl.ds(p0, PACKED_ROWS), :].astype(F32)[PACKED_ROWS - 1:, :] * jnp.where(t > 0, 1.0, 0.0)
            done = [None] * halves
            for half in reversed(range(halves)):
                rows_h = slice(half * SUBLANES, (half + 1) * SUBLANES)
                xt, st = x[rows_h, :], cur[rows_h, :]
                xr, xi = _scan_tile(xt[:, :w], xt[:, w:], steps)
                gr = xr + (tab_r * cr - tab_i * ci)
                gi = xi + (tab_r * ci + tab_i * cr)
                done[half] = jnp.concatenate([gr, gi], axis=1)
                prev = before if half == 0 else cur[half * SUBLANES - 1:half * SUBLANES, :]
                spr = jnp.where(row >= 1, pltpu.roll(st[:, :w], 1, 0), prev[:, :w])
                spi = jnp.where(row >= 1, pltpu.roll(st[:, w:], 1, 0), prev[:, w:])
                cr, ci, ar, ai = gr[:1, :], gi[:1, :], ar + gr * spr + gi * spi, ai + gi * spr - gr * spi
            g_ref[pl.ds(r0, PACKED_ROWS), :] = jnp.concatenate(done, axis=0).astype(g_ref.dtype)
            return cr, ci, ar, ai

        z1 = jnp.zeros((1, w), F32)
        z8 = jnp.zeros((SUBLANES, w), F32)
        _, _, ar, ai = _unrolled_loop(n_packed, packed_tile, (z1, z1, z8, z8), SCAN_UNROLL)
        gl_ref[...] = jnp.concatenate([jnp.sum(ar, axis=0, keepdims=True), jnp.sum(ai, axis=0, keepdims=True)], axis=1)

    spec = pl.BlockSpec((lp, 2 * w), lambda b, j: (b, j))
    return pl.pallas_call(
        body, name=name, grid=(n_batch, nj),
        in_specs=[spec, spec, pl.BlockSpec((1, 2 * w), lambda b, j: (0, j))],
        out_specs=[spec, pl.BlockSpec((None, 1, 2 * w), lambda b, j: (b, 0, j))],
        out_shape=[jax.ShapeDtypeStruct((rows, two_ns), BF16), jax.ShapeDtypeStruct((n_batch, 1, two_ns), F32)],
        compiler_params=_params(("parallel", "parallel")),
    )(gd, states, lam)


def _log_sigmoid(z):
    return jnp.minimum(z, 0.0) - jnp.log(1.0 + jnp.exp(-jnp.abs(z)))


ATTN_KEYS = 256
ATTN_GROUP = 4


def _attn_block(i, jb, lp, pad):
    start = jb * ATTN_KEYS
    r0 = pl.multiple_of(jnp.minimum(start, lp - ATTN_KEYS), LANES)
    rowpos = i * LANES + lax.broadcasted_iota(jnp.int32, (LANES, ATTN_KEYS), 0)
    keypos = r0 + lax.broadcasted_iota(jnp.int32, (LANES, ATTN_KEYS), 1)
    return r0, (keypos < rowpos) & (keypos >= jnp.maximum(start, pad))


def _tri_ones(strict_upper):
    r = lax.broadcasted_iota(jnp.int32, (ATTN_KEYS, ATTN_KEYS + LANES), 0)
    c = lax.broadcasted_iota(jnp.int32, (ATTN_KEYS, ATTN_KEYS + LANES), 1)
    tri = (r > c) if strict_upper else (r < c)
    return jnp.where((c >= ATTN_KEYS) | tri, 1.0, 0.0).astype(BF16)


def _split_sums(cr):
    rs = cr[:, ATTN_KEYS:]
    return cr[:, :ATTN_KEYS], jnp.concatenate([rs] * (ATTN_KEYS // LANES), axis=1)


def _head_masks():
    lane = lax.broadcasted_iota(jnp.int32, (1, LANES), 1)
    return [lane < SB_HEAD_DIM, lane >= SB_HEAD_DIM]


def _run_groups(n, first, sign, make):
    j, left, g = first, n, ATTN_GROUP
    while g >= 1:
        shift = g.bit_length() - 1
        count = lax.shift_right_logical(left, shift)
        fn = make(g)

        def loop(_, jcur, fn=fn, g=g):
            fn(jcur)
            return jcur + sign * g

        j = lax.fori_loop(0, count, loop, j)
        left = left - lax.shift_left(count, shift)
        g //= 2


def _attn_fwd(name, proj, n_batch, lp, pad, q_cb, k_cb, v_cb, n_pairs):
    rows = proj.shape[0]
    nq = lp // LANES
    scale = SB_HEAD_DIM ** -0.5

    def body(q_ref, k_ref, v_ref, o_ref, acc_s):
        i = pl.program_id(1)
        hm = _head_masks()
        comb = _tri_ones(True)
        n_blocks = lax.shift_right_logical(i + ATTN_KEYS // LANES, (ATTN_KEYS // LANES).bit_length() - 1)

        def pair(hp, carry):
            lanes = pl.ds(pl.multiple_of(hp * LANES, LANES), LANES)
            qs = q_ref[:, lanes] * scale
            qh = [jnp.where(m, qs, 0.0).astype(BF16) for m in hm]
            acc_s[...] = jnp.zeros_like(acc_s)
            o_ref[:, lanes] = jnp.zeros((LANES, LANES), F32)

            def make(group):
                def fn(jtop):
                    chains = []
                    for g in range(group):
                        r0, vis = _attn_block(i, jtop - g, lp, pad)
                        kj = k_ref[pl.ds(r0, ATTN_KEYS), lanes].astype(BF16)
                        vj = v_ref[pl.ds(r0, ATTN_KEYS), lanes]
                        for h in range(2):
                            z = lax.dot_general(qh[h], kj, NT, preferred_element_type=F32)
                            chains.append((h, vis, z, jnp.where(hm[h], vj, 0.0).astype(BF16)))
                    staged = []
                    for h, vis, z, vh in chains:
                        lsz = _log_sigmoid(z)
                        staged.append((h, vis, lsz, _running_sums(jnp.where(vis, lsz - z, 0.0), comb, split=True), vh))
                    out = o_ref[:, lanes]
                    for h, vis, lsz, cr, vh in staged:
                        later, rs = _split_sums(cr)
                        acc = acc_s[h]
                        wgt = jnp.where(vis, jnp.exp(lsz + later + acc), 0.0)
                        acc_s[h] = acc + rs
                        out = out + lax.dot_general(wgt.astype(BF16), vh, NN, preferred_element_type=F32)
                    o_ref[:, lanes] = out
                return fn

            _run_groups(n_blocks, n_blocks - 1, -1, make)
            return carry

        lax.fori_loop(0, n_pairs, pair, 0)

    wide = n_pairs * LANES
    assert q_cb % n_pairs == 0 and k_cb % n_pairs == 0 and v_cb % n_pairs == 0
    return pl.pallas_call(
        body, name=name, grid=(n_batch, nq),
        in_specs=[pl.BlockSpec((LANES, wide), lambda b, i: (b * nq + i, q_cb // n_pairs)),
                  pl.BlockSpec((lp, wide), lambda b, i: (b, k_cb // n_pairs)),
                  pl.BlockSpec((lp, wide), lambda b, i: (b, v_cb // n_pairs))],
        out_specs=pl.BlockSpec((LANES, wide), lambda b, i: (b * nq + i, 0)),
        out_shape=jax.ShapeDtypeStruct((rows, wide), F32),
        scratch_shapes=[pltpu.VMEM((2, LANES, ATTN_KEYS), F32)],
        compiler_params=_params(("parallel", "arbitrary")),
    )(proj, proj, proj)


def _attn_bwd(name, proj, g_out, n_batch, lp, pad, q_cb, k_cb, v_cb, go_cb, n_pairs):
    rows = proj.shape[0]
    nq = lp // LANES
    scale = SB_HEAD_DIM ** -0.5

    def body(q_ref, k_ref, v_ref, go_ref, gq_ref, gk_ref, gv_ref, ga_s, sz_s, acc_s):
        i = pl.program_id(1)

        @pl.when(i == 0)
        def _():
            gk_ref[...] = jnp.zeros_like(gk_ref)
            gv_ref[...] = jnp.zeros_like(gv_ref)

        hm = _head_masks()
        comb_up = _tri_ones(True)
        comb_lo = _tri_ones(False)
        n_blocks = lax.shift_right_logical(i + ATTN_KEYS // LANES, (ATTN_KEYS // LANES).bit_length() - 1)

        def pair(hp, carry):
            lanes = pl.ds(pl.multiple_of(hp * LANES, LANES), LANES)
            qs = q_ref[:, lanes] * scale
            go = go_ref[:, lanes]
            qh = [jnp.where(m, qs, 0.0).astype(BF16) for m in hm]
            goh = [jnp.where(m, go, 0.0).astype(BF16) for m in hm]
            acc_s[...] = jnp.zeros_like(acc_s)

            def make_down(group):
                def fn(jtop):
                    chains = []
                    for g in range(group):
                        j = jtop - g
                        r0, vis = _attn_block(i, j, lp, pad)
                        kj = k_ref[pl.ds(r0, ATTN_KEYS), lanes].astype(BF16)
                        vj = v_ref[pl.ds(r0, ATTN_KEYS), lanes].astype(BF16)
                        for h in range(2):
                            z = lax.dot_general(qh[h], kj, NT, preferred_element_type=F32)
                            gw = lax.dot_general(goh[h], vj, NT, preferred_element_type=F32)
                            chains.append((h, j, r0, vis, z, gw))
                    staged = []
                    for h, j, r0, vis, z, gw in chains:
                        lsz = _log_sigmoid(z)
                        staged.append((h, j, r0, vis, lsz, _running_sums(jnp.where(vis, lsz - z, 0.0), comb_up), gw))
                    for h, j, r0, vis, lsz, cr, gw in staged:
                        later, rs = _split_sums(cr)
                        acc = acc_s[h]
                        wgt = jnp.where(vis, jnp.exp(lsz + later + acc), 0.0)
                        acc_s[h] = acc + rs
                        ga_s[h, j] = gw * wgt
                        sz_s[h, j] = jnp.exp(lsz)
                        gv_ref[pl.ds(r0, ATTN_KEYS), lanes] += lax.dot_general(
                            wgt.astype(BF16), goh[h], TN, preferred_element_type=F32)
                return fn

            _run_groups(n_blocks, n_blocks - 1, -1, make_down)
            acc_s[...] = jnp.zeros_like(acc_s)

            def make_up(group):
                def fn(jbot):
                    pend = []
                    for g in range(group):
                        j = jbot + g
                        r0, vis = _attn_block(i, j, lp, pad)
                        kj = k_ref[pl.ds(r0, ATTN_KEYS), lanes]
                        for h in range(2):
                            ga = ga_s[h, j]
                            pend.append((h, j, r0, vis, ga, _running_sums(ga, comb_lo),
                                         jnp.where(hm[h], kj, 0.0).astype(BF16)))
                    gq = jnp.zeros((LANES, LANES), F32)
                    for h, j, r0, vis, ga, cr, kh in pend:
                        before, rs = _split_sums(cr)
                        pre = acc_s[h]
                        glk = before + pre
                        acc_s[h] = pre + rs
                        sz = sz_s[h, j]
                        gz = jnp.where(vis, ga * (1.0 - sz) - glk * sz, 0.0).astype(BF16)
                        gq = gq + lax.dot_general(gz, kh, NN, preferred_element_type=F32)
                        gk_ref[pl.ds(r0, ATTN_KEYS), lanes] += lax.dot_general(gz, qh[h], TN, preferred_element_type=F32)
                    gq_ref[:, lanes] += gq * scale
                return fn

            gq_ref[:, lanes] = jnp.zeros((LANES, LANES), F32)
            _run_groups(n_blocks, 0, 1, make_up)
            return carry

        lax.fori_loop(0, n_pairs, pair, 0)

    wide = n_pairs * LANES
    assert q_cb % n_pairs == 0 and k_cb % n_pairs == 0 and v_cb % n_pairs == 0 and go_cb % n_pairs == 0
    blk = lambda cb: pl.BlockSpec((LANES, wide), lambda b, i: (b * nq + i, cb // n_pairs))
    full = lambda cb: pl.BlockSpec((lp, wide), lambda b, i: (b, cb // n_pairs))
    shp = jax.ShapeDtypeStruct((rows, wide), F32)
    per_block = pltpu.VMEM((2, -(-lp // ATTN_KEYS), LANES, ATTN_KEYS), F32)
    return pl.pallas_call(
        body, name=name, grid=(n_batch, nq),
        in_specs=[blk(q_cb), full(k_cb), full(v_cb), blk(go_cb)],
        out_specs=[blk(0), full(0), full(0)], out_shape=[shp, shp, shp],
        scratch_shapes=[per_block, per_block, pltpu.VMEM((2, LANES, ATTN_KEYS), F32)],
        compiler_params=_params(("parallel", "arbitrary")),
    )(proj, proj, proj, g_out)


def _lb_fn(gamma):
    g0, g1 = gamma[0:1, :], gamma[1:2, :]
    mx = jnp.maximum(g0, g1)
    e0, e1 = jnp.exp(g0 - mx), jnp.exp(g1 - mx)
    p0, p1 = e0 / (e0 + e1), e1 / (e0 + e1)
    return (p0 + p1) - p0


def _lower_bound(name, gamma):
    def body(g_ref, o_ref):
        o_ref[...] = _lb_fn(g_ref[...])

    return pl.pallas_call(body, name=name, out_shape=jax.ShapeDtypeStruct((1, gamma.shape[1]), F32))(gamma)


def _lower_bound_bwd(name, gamma, g_lb_parts, g_ng_parts):
    def body(g_ref, glb_ref, gng_ref, o_ref, o2_ref):
        _, vjp = jax.vjp(_lb_fn, g_ref[...])
        o_ref[...] = vjp(jnp.sum(glb_ref[...], axis=0))[0]
        o2_ref[...] = jnp.sum(gng_ref[...], axis=0)

    return pl.pallas_call(
        body, name=name,
        out_shape=[jax.ShapeDtypeStruct(gamma.shape, F32), jax.ShapeDtypeStruct((1, gamma.shape[1]), F32)],
    )(gamma, g_lb_parts, g_ng_parts)


def _tri_times(tril, x, dims):
    hi = x.astype(BF16)
    lo = (x - hi.astype(F32)).astype(BF16)
    t = tril.astype(BF16)
    return (lax.dot_general(t, hi, dims, preferred_element_type=F32)
            + lax.dot_general(t, lo, dims, preferred_element_type=F32))


@jax.custom_vjp
def _cumsum_rows(x, tril):
    return _tri_times(tril, x, NN)


def _cumsum_rows_fwd(x, tril):
    return _tri_times(tril, x, NN), tril


def _cumsum_rows_bwd(tril, g):
    return _tri_times(tril, g, TN), jnp.zeros_like(tril)


_cumsum_rows.defvjp(_cumsum_rows_fwd, _cumsum_rows_bwd)


def _hg_decays(f_pre, lbs, masks, tril):
    f = [[lb + (1.0 - lb) * jax.nn.sigmoid(fc) for fc, lb in zip(row, lbs)] for row in f_pre]
    bcum = [[_cumsum_rows(jnp.log(x) * m, tril) for x in row] for row, m in zip(f, masks)]
    return [[1.0 - x for x in row] for row in f], bcum


def _hg_step(q, f_pre, i_in, g, lbs, ngs, sts, masks, tril):
    k, bcum = _hg_decays(f_pre, lbs, masks, tril)
    v = [[ic * m for ic in row] for row, m in zip(i_in, masks)]
    qd = [[qc * jnp.exp(b) for qc, b in zip(qr, br)] for qr, br in zip(q, bcum)]
    scores = [[jnp.where(tril > 0.5, _dot(a, kk * jnp.exp(-b), NT), 0.0) for a, kk, b in zip(ar, kr, br)]
              for ar, kr, br in zip(qd, k, bcum)]
    inner = [[_dot(s, x, NN) for s, x in zip(sr, vr)] for sr, vr in zip(scores, v)]
    add = [[_dot(x, kk * jnp.exp(b[HG_CHUNK - 1:, :] - b), TN) for x, kk, b in zip(vr, kr, br)]
           for vr, kr, br in zip(v, k, bcum)]
    outs = []
    for qr, br, nr, ar, gr in zip(qd, bcum, inner, add, g):
        o = [n + _dot(a, st, NT) for n, a, st in zip(nr, qr, sts)]
        sts = [jnp.exp(b[HG_CHUNK - 1:, :]) * st + a for b, a, st in zip(br, ar, sts)]
        o = [x * lax.rsqrt(jnp.mean(x * x, axis=-1, keepdims=True) + RMS_EPS) * ng for x, ng in zip(o, ngs)]
        outs.append([x * (gc * jax.nn.sigmoid(gc)) for x, gc in zip(o, gr)])
    return outs, sts


def _hg_consts(c, pad):
    r = lax.broadcasted_iota(jnp.int32, (HG_CHUNK, HG_CHUNK), 0)
    cc = lax.broadcasted_iota(jnp.int32, (HG_CHUNK, HG_CHUNK), 1)
    tril = jnp.where(r >= cc, 1.0, 0.0).astype(F32)
    pos = c * HG_CHUNK + lax.broadcasted_iota(jnp.int32, (HG_CHUNK, 1), 0)
    return tril, jnp.where(pos >= pad, 1.0, 0.0).astype(F32)


HG_HEADS_PER_STEP = 8
HG_CHUNKS_PER_STEP = 2


def _hg_layout(lp, n_heads):
    step_rows = HG_CHUNKS_PER_STEP * HG_CHUNK
    per = min(HG_HEADS_PER_STEP, n_heads)
    assert lp % step_rows == 0 and n_heads % per == 0
    heads = [(h, slice(h * HG_DK, (h + 1) * HG_DK)) for h in range(per)]
    return n_heads // per, lp // step_rows, step_rows, per * HG_DK, heads


def _hg_step_views(step, pad, heads):
    slices = [slice(u * HG_CHUNK, (u + 1) * HG_CHUNK) for u in range(HG_CHUNKS_PER_STEP)]
    consts = [_hg_consts(step * HG_CHUNKS_PER_STEP + u, pad) for u in range(HG_CHUNKS_PER_STEP)]
    load = lambda ref: [[ref[sl, cols] for _, cols in heads] for sl in slices]
    return slices, [m for _, m in consts], consts[0][0], load


def _hgrn_fwd(name, proj, lb, ng, n_batch, lp, pad, n_heads):
    rows = proj.shape[0]
    groups, steps, step_rows, wide, heads = _hg_layout(lp, n_heads)

    def body(q_ref, f_ref, i_ref, g_ref, lb_ref, ng_ref, o_ref, s_ref, st_s):
        t = pl.program_id(2)

        @pl.when(t == 0)
        def _():
            st_s[...] = jnp.zeros_like(st_s)

        slices, masks, tril, load = _hg_step_views(t, pad, heads)
        sts = [st_s[h] for h, _ in heads]
        for (_, cols), st in zip(heads, sts):
            s_ref[:, cols] = st
        outs, sts = _hg_step(load(q_ref), load(f_ref), load(i_ref), load(g_ref),
                             [lb_ref[:, cols] for _, cols in heads], [ng_ref[:, cols] for _, cols in heads],
                             sts, masks, tril)
        for sl, row in zip(slices, outs):
            for (_, cols), o in zip(heads, row):
                o_ref[sl, cols] = o
        for (h, _), st in zip(heads, sts):
            st_s[h] = st

    col = lambda off: pl.BlockSpec((step_rows, wide), lambda b, h, t: (b * steps + t, off * groups + h))
    vec = pl.BlockSpec((1, wide), lambda b, h, t: (0, h))
    return pl.pallas_call(
        body, name=name, grid=(n_batch, groups, steps), in_specs=[col(0), col(1), col(2), col(3), vec, vec],
        out_specs=[col(0), pl.BlockSpec((HG_DK, wide), lambda b, h, t: (b * steps + t, h))],
        out_shape=[jax.ShapeDtypeStruct((rows, n_heads * HG_DK), F32),
                   jax.ShapeDtypeStruct((n_batch * steps * HG_DK, n_heads * HG_DK), F32)],
        scratch_shapes=[pltpu.VMEM((len(heads), HG_DK, HG_DK), F32)],
        compiler_params=_params(("parallel", "parallel", "arbitrary")),
    )(proj, proj, proj, proj, lb, ng)


def _hgrn_bwd(name, proj, lb, ng, g_out, states, n_batch, lp, pad, n_heads):
    rows = proj.shape[0]
    width = n_heads * HG_DK
    groups, steps, step_rows, wide, heads = _hg_layout(lp, n_heads)
    assert groups == 1

    def body(q_ref, f_ref, i_ref, g_ref, lb_ref, ng_ref, go_ref, s_ref, gp_ref, glb_ref, gng_ref, gst_s):
        t = pl.program_id(2)

        @pl.when(t == 0)
        def _():
            gst_s[...] = jnp.zeros_like(gst_s)
            glb_ref[...] = jnp.zeros_like(glb_ref)
            gng_ref[...] = jnp.zeros_like(gng_ref)

        slices, masks, tril, load = _hg_step_views(steps - 1 - t, pad, heads)
        fn = functools.partial(_hg_step, masks=masks, tril=tril)
        _, vjp = jax.vjp(fn, load(q_ref), load(f_ref), load(i_ref), load(g_ref),
                         [lb_ref[:, cols] for _, cols in heads], [ng_ref[:, cols] for _, cols in heads],
                         [s_ref[:, cols] for _, cols in heads])
        gq, gf, gi, gg, glb, gng, gst = vjp((load(go_ref), [gst_s[h] for h, _ in heads]))
        for part, grads in enumerate((gq, gf, gi, gg)):
            for sl, row in zip(slices, grads):
                for (h, _), x in zip(heads, row):
                    lane0 = part * width + h * HG_DK
                    gp_ref[sl, lane0:lane0 + HG_DK] = x.astype(BF16)
        for (h, cols), a, b, c in zip(heads, gst, glb, gng):
            gst_s[h] = a
            glb_ref[:, cols] += b
            gng_ref[:, cols] += c

    col = lambda off: pl.BlockSpec((step_rows, wide), lambda b, h, t: (b * steps + steps - 1 - t, off * groups + h))
    vec = pl.BlockSpec((1, wide), lambda b, h, t: (0, h))
    part = pl.BlockSpec((None, 1, wide), lambda b, h, t: (b, 0, h))
    big = jax.ShapeDtypeStruct((rows, 4 * width), BF16)
    small = jax.ShapeDtypeStruct((n_batch, 1, width), F32)
    return pl.pallas_call(
        body, name=name, grid=(n_batch, groups, steps),
        in_specs=[col(0), col(1), col(2), col(3), vec, vec, col(0),
                  pl.BlockSpec((HG_DK, wide), lambda b, h, t: (b * steps + steps - 1 - t, h))],
        out_specs=[pl.BlockSpec((step_rows, 4 * width), lambda b, h, t: (b * steps + steps - 1 - t, 0)), part, part],
        out_shape=[big, small, small],
        scratch_shapes=[pltpu.VMEM((len(heads), HG_DK, HG_DK), F32)],
        compiler_params=_params(("parallel", "parallel", "arbitrary")),
    )(proj, proj, proj, proj, lb, ng, g_out, states)


def _exchange_copies(src, dst, send, recv, loc, scatter):
    x, y, c = lax.axis_index("x"), lax.axis_index("y"), lax.axis_index("c")
    me = 4 * x + 2 * y + c
    local, remote = [], []
    for w in range(len(src)):
        local.append(pltpu.make_async_copy(src[w].at[me] if scatter else src[w], dst[w].at[me], loc.at[w]))
    for k in range(1, N_DEV):
        px = 1 - x if k & 4 else x
        py = 1 - y if k & 2 else y
        pc = 1 - c if k & 1 else c
        peer = 4 * px + 2 * py + pc
        for w in range(len(src)):
            remote.append(pltpu.make_async_remote_copy(
                src_ref=src[w].at[peer] if scatter else src[w], dst_ref=dst[w].at[me],
                send_sem=send.at[w * (N_DEV - 1) + k - 1], recv_sem=recv.at[w * (N_DEV - 1) + k - 1],
                device_id=(px, py, pc), device_id_type=pl.DeviceIdType.MESH))
    return local, remote


_HBM_SPEC = pl.BlockSpec(memory_space=pltpu.HBM)
_SEM_SPEC = pl.BlockSpec(memory_space=pltpu.SEMAPHORE)
_ANY_SPEC = pl.BlockSpec(memory_space=pl.ANY)
_DATAFLOW = pltpu.SideEffectType.DATAFLOW_SIDE_EFFECTING


def _exchange_start(name, srcs, scatter, dep=None):
    nw = len(srcs)
    srcs = [pltpu.with_memory_space_constraint(s, pltpu.HBM) for s in srcs]
    lands = [pltpu.with_memory_space_constraint(lax.empty(s.shape if scatter else (N_DEV,) + s.shape, s.dtype), pltpu.HBM)
             for s in srcs]
    deps = [] if dep is None else [dep]

    def body(*refs):
        src, dst = refs[:nw], refs[nw:2 * nw]
        send, recv, loc = refs[2 * nw + len(deps):2 * nw + len(deps) + 3]
        token = refs[-1]
        local, remote = _exchange_copies(src, dst, send, recv, loc, scatter)
        for cp in local + remote:
            cp.start()
        token[...] = jnp.zeros_like(token)

    sems = [pltpu.SemaphoreType.DMA((nw * (N_DEV - 1),)), pltpu.SemaphoreType.DMA((nw * (N_DEV - 1),)),
            pltpu.SemaphoreType.DMA((nw,))]
    out = pl.pallas_call(
        body, name=name,
        out_shape=(*sems, *[pltpu.HBM(s.shape, s.dtype) for s in srcs], *[pltpu.HBM(s.shape, s.dtype) for s in lands],
                   jax.ShapeDtypeStruct((SUBLANES, LANES), F32)),
        in_specs=[_HBM_SPEC] * (2 * nw) + [_ANY_SPEC] * len(deps),
        out_specs=(_SEM_SPEC, _SEM_SPEC, _SEM_SPEC, *[_HBM_SPEC] * (2 * nw), pl.BlockSpec(memory_space=pltpu.VMEM)),
        input_output_aliases={i: 3 + i for i in range(2 * nw)},
        compiler_params=pltpu.CompilerParams(has_side_effects=_DATAFLOW),
    )(*srcs, *lands, *deps)
    return {"sems": out[:3], "srcs": out[3:3 + nw], "lands": out[3 + nw:3 + 2 * nw], "token": out[-1], "scatter": scatter}


def _exchange_wait(name, handle, after):
    nw = len(handle["srcs"])
    scatter = handle["scatter"]

    def body(*refs):
        src, dst = refs[:nw], refs[nw:2 * nw]
        send, recv, loc = refs[2 * nw:2 * nw + 3]
        local, remote = _exchange_copies(src, dst, send, recv, loc, scatter)
        for cp in local:
            cp.wait()
        for cp in remote:
            cp.wait_send()
            cp.wait_recv()

    out = pl.pallas_call(
        body, name=name,
        out_shape=(*[pltpu.HBM(s.shape, s.dtype) for s in handle["srcs"]],
                   *[pltpu.HBM(s.shape, s.dtype) for s in handle["lands"]]),
        in_specs=[_HBM_SPEC] * (2 * nw) + [_SEM_SPEC] * 3 + [_ANY_SPEC],
        out_specs=tuple([_HBM_SPEC] * (2 * nw)),
        input_output_aliases={i: i for i in range(2 * nw)},
        compiler_params=pltpu.CompilerParams(has_side_effects=_DATAFLOW),
    )(*handle["srcs"], *handle["lands"], *handle["sems"], after)
    return list(out[nw:])


def _adamw(w, g, m, v):
    m = ADAM_B1 * m + (1.0 - ADAM_B1) * g
    v = ADAM_B2 * v + (1.0 - ADAM_B2) * (g * g)
    m_hat = m / (1.0 - ADAM_B1 ** ADAM_STEP)
    v_hat = v / (1.0 - ADAM_B2 ** ADAM_STEP)
    delta = -ADAM_LR * (m_hat / (jnp.sqrt(v_hat) + ADAM_EPS) + ADAM_WD * w)
    return delta, m, v


def _adamw_summed(name, parts, w, m, v):
    layered = w.ndim == 3
    parts = list(parts) if layered else [parts]
    n_layers = len(parts)
    rows, cols = w.shape[-2:]
    n_parts = parts[0].shape[0]
    tr = _tile(rows, max(SUBLANES, (1 << 18) // cols))

    def body(*refs):
        p_refs = refs[:n_layers]
        w_ref, m_ref, v_ref, g_ref, d_ref, nm_ref, nv_ref = refs[n_layers:]
        layer = pl.program_id(0)

        def run(p_ref):
            g = p_ref[0].astype(F32)
            for s in range(1, n_parts):
                g = g + p_ref[s].astype(F32)
            d, nm, nv = _adamw(w_ref[...], g, m_ref[...], v_ref[...])
            g_ref[...] = g
            d_ref[...] = d
            nm_ref[...] = nm
            nv_ref[...] = nv

        for l in range(n_layers):
            pl.when(layer == l)(functools.partial(run, p_refs[l]))

    if layered:
        spec = pl.BlockSpec((None, tr, cols), lambda l, i: (l, i, 0))
    else:
        spec = pl.BlockSpec((tr, cols), lambda l, i: (i, 0))
    p_specs = [pl.BlockSpec((n_parts, tr, cols), lambda l, i, q=q: (0, jnp.where(l == q, i, 0), 0))
               for q in range(n_layers)]
    shp = jax.ShapeDtypeStruct(w.shape, F32)
    return pl.pallas_call(
        body, name=name, grid=(n_layers, rows // tr), in_specs=[*p_specs, spec, spec, spec],
        out_specs=[spec] * 4, out_shape=[shp] * 4, compiler_params=_params(("parallel", "parallel")),
    )(*parts, w, m, v)


def _pack_rows(arrays, cols):
    out = []
    for a in arrays:
        flat = a.reshape(-1)
        n = -(-flat.shape[0] // cols) * cols
        out.append(jnp.pad(flat, (0, n - flat.shape[0])).reshape(-1, cols))
    packed = jnp.concatenate(out, axis=0)
    return jnp.pad(packed, ((0, -packed.shape[0] % SUBLANES), (0, 0)))


def _unpack_rows(packed, shapes, cols):
    out, r = [], 0
    for s in shapes:
        n = math.prod(s)
        nr = -(-n // cols)
        out.append(packed[r:r + nr].reshape(-1)[:n].reshape(s))
        r += nr
    return out


def _block_diag(blocks):
    g, a, b = blocks.shape
    eye = jnp.eye(g, dtype=blocks.dtype)
    return (eye[:, None, :, None] * blocks[:, :, None, :]).reshape(g * a, g * b)


def _diag_blocks(dense, g):
    a, b = dense.shape[0] // g, dense.shape[1] // g
    return jnp.einsum("gagb->gab", dense.reshape(g, a, g, b))


def _local_step(x, target, meta, wts, small, late_weights, on_grads, on_small):
    n_batch, seq, d = x.shape
    n_meta = meta.shape[0]
    pad = -(seq + n_meta) % LANES
    lead = pad + n_meta
    lp = lead + seq
    rows = n_batch * lp
    s5w = wts["glu"].shape[0]
    n_ab = wts["in_ab"].shape[2]
    ab_cols = wts["in_ab"].shape[0] * n_ab
    sbw = (ab_cols - s5w) // 3
    dff = small["mlp_b_up"].shape[1]
    n_pairs = sbw // LANES
    n_hg = d // HG_DK
    s5_cb = s5w // LANES
    sb_cb = sbw // LANES
    tm = _tile(rows, ROW_TILE)
    groups, n_state, grp = small["s5_b_re"].shape[1:]
    ns = groups * n_state
    sw = min(SCAN_LANES, ns)

    h0 = jnp.concatenate(
        [jnp.zeros((n_batch, pad, d), F32), jnp.broadcast_to(meta[None], (n_batch, n_meta, d)), x], axis=1
    ).reshape(rows, d)

    lam_re, lam_im = small["s5_lam_re"][0], small["s5_lam_im"][0]
    log_dt = small["s5_log_dt"][0][:, None]
    b_re_t = small["s5_b_re"][0].transpose(0, 2, 1)
    b_im_t = small["s5_b_im"][0].transpose(0, 2, 1)
    c_re, c_im = small["s5_c_re"][0], small["s5_c_im"][0]
    lbr, lbi, bbr, bbi = _s5_params("s5_params", lam_re, lam_im, log_dt, b_re_t, b_im_t)
    b_blk = _interleave(_block_diag(bbr), _block_diag(bbi), sw).astype(BF16)
    c_blk = _interleave(_block_diag(c_re), _block_diag(-c_im), sw).T.astype(BF16)
    lam_row = _interleave(lbr.reshape(1, ns), lbi.reshape(1, ns), sw)
    d_row = small["s5_d"].reshape(1, s5w)

    def ln_store(outs, acc, res, bias, g, b):
        r = ALPHA * res + acc + bias
        outs[0][...] = r
        if len(outs) > 1:
            h = _ln(r, g, b)
            outs[1][...] = h
            outs[2][...] = h.astype(BF16)

    zero_bias = jnp.zeros((1, d), F32)

    def mix_ln(name, a, w, k_total, tk, res, bias, g, b, a_fn=None, emit_h=True):
        dtypes = (F32, F32, BF16) if emit_h else (F32,)
        return _mm_act(name, a, w, "nat", n_out_cols=d, k_total=k_total, tn=d, tk=tk, a_fn=a_fn,
                       extras=(res, bias, g, b), extra_specs=(_row_spec(tm, d), _vec_spec(d), _vec_spec(d), _vec_spec(d)),
                       store=ln_store, out_shape=[jax.ShapeDtypeStruct((rows, d), t) for t in dtypes],
                       out_specs=[_row_spec(tm, d)] * len(dtypes))

    def two(width):
        return [jax.ShapeDtypeStruct((rows, width), F32)] * 2, [_row_spec(tm, width)] * 2

    def shard_tile(total, shard, cap=1024):
        t = max(shard, cap - cap % shard)
        while total % t:
            t -= shard
        return t

    h0b = h0.astype(BF16)
    assert s5w % n_ab == 0
    u_shards = s5w // n_ab
    proj_u = _mm_act("in_a", h0b, wts["in_ab"][:u_shards], "stk", n_out_cols=s5w, k_total=d, tn=shard_tile(s5w, n_ab), tk=d)[0]
    tn_qkv = shard_tile(ab_cols - s5w, n_ab)
    proj_qkv = _mm_act("in_b", h0b, wts["in_ab"][u_shards:], "stk", n_out_cols=ab_cols - s5w, k_total=d, tn=tn_qkv, tk=d,
                       out_shape=[jax.ShapeDtypeStruct((rows, ab_cols - s5w), BF16)], out_specs=[_row_spec(tm, tn_qkv)])[0]
    bu = _mm_act("s5_bu", proj_u, b_blk, "nat", n_out_cols=2 * ns, k_total=s5w, tn=min(2 * ns, 2048), tk=s5w)[0]
    states = _s5_scan("s5_scan", bu, lam_row, n_batch, lp, sw)

    def gelu_store(outs, acc, u, dv):
        ypre = acc + dv * u
        outs[0][...] = ypre
        outs[1][...] = jax.nn.gelu(ypre)

    shp2, spec2 = two(s5w)
    ypre, y = _mm_act(
        "s5_y", states, c_blk, "nat", n_out_cols=s5w, k_total=2 * ns, tn=s5w, tk=min(2 * ns, 1024),
        extras=(proj_u, d_row), extra_specs=(_row_spec(tm, s5w), _vec_spec(s5w)), store=gelu_store,
        out_shape=shp2, out_specs=spec2)

    def glu_store(outs, acc, yv, bias):
        gate = acc + bias
        outs[0][...] = gate
        outs[1][...] = _glu(yv, gate)

    gate, a_out = _mm_act(
        "s5_glu", y, wts["glu"], "nat", n_out_cols=s5w, k_total=s5w, tn=s5w, tk=s5w,
        extras=(y, small["s5_b_glu"]), extra_specs=(_row_spec(tm, s5w), _vec_spec(s5w)), store=glu_store,
        out_shape=shp2, out_specs=spec2)
    b_out = _attn_fwd("sb_attn", proj_qkv, n_batch, lp, pad, 0, sb_cb, 2 * sb_cb, n_pairs)

    def bias_store(outs, acc, bias):
        outs[0][...] = (acc + bias).astype(outs[0].dtype)

    def wide(width, dtype):
        return [jax.ShapeDtypeStruct((rows, dff), dtype)], [_row_spec(tm, width)]

    def mlp_fwd(layer, h_in, h_in_b, emit_h=True):
        tn = shard_tile(dff, n_up)
        shp, spec = wide(tn, BF16)
        up = _mm_act(f"up{layer}", h_in_b, wts["up"][layer], "stk", n_out_cols=dff, k_total=d, tn=tn, tk=d,
                     extras=(small["mlp_b_up"][layer:layer + 1],), extra_specs=(_vec_spec(tn),), store=bias_store,
                     out_shape=shp, out_specs=spec)[0]
        return (up, *mix_ln(f"down{layer}", up, wts["down"][layer], dff, min(dff, 1024), h_in,
                            small["mlp_b_down"][layer:layer + 1], small["ln_mlp_g"][layer:layer + 1],
                            small["ln_mlp_b"][layer:layer + 1], a_fn=_relu2, emit_h=emit_h))

    r1, h1, h1b = mix_ln("out_ab", [a_out, b_out], wts["out_ab"], s5w + sbw, min(s5w, sbw), h0, zero_bias,
                         small["ln_mix_g"][0:1], small["ln_mix_b"][0:1])
    wts = {**wts, **late_weights(r1)}
    n_c = wts["in_c"].shape[2]
    n_up = wts["up"][0].shape[2]
    up0, r2, h2, h2b = mlp_fwd(0, h1, h1b)

    lb = _lower_bound("hg_lb", small["hgrn_gamma"])
    proj_c = _mm_act("in_c", h2b, wts["in_c"], "stk", n_out_cols=4 * d, k_total=d, tn=shard_tile(4 * d, n_c), tk=d)[0]
    c_out, hg_states = _hgrn_fwd("hgrn", proj_c, lb, wts["ng"], n_batch, lp, pad, n_hg)
    r3, h3, h3b = mix_ln("out_c", c_out, wts["out_c"], d, d, h2, zero_bias, small["ln_mix_g"][1:2], small["ln_mix_b"][1:2])
    up1, r4 = mlp_fwd(1, h3, h3b, emit_h=False)

    gr = {}
    g_r4, gr["ln_mlp_g1"], gr["ln_mlp_b1"], loss_tile, g_r4b = _loss_grad(
        "loss", r4, small["ln_mlp_g"][1:2], small["ln_mlp_b"][1:2], target, n_batch, lp, lead)

    def res_store(outs, acc, g_res):
        outs[0][...] = acc + ALPHA * g_res

    def ln_bwd_store(outs, acc, g_res, r_in, g, b, first_step):
        gr_in, gg, gb = jax.vjp(_ln, r_in, g, b)[1](acc + ALPHA * g_res)
        outs[0][...] = gr_in
        outs[3][...] = gr_in.astype(BF16)

        @pl.when(first_step)
        def _():
            outs[1][...] = jnp.zeros_like(outs[1])
            outs[2][...] = jnp.zeros_like(outs[2])

        outs[1][...] += gg
        outs[2][...] += gb

    def through_ln(name, a, w, k_total, tk, g_res, r_in, g, b, dep=None):
        vec = pl.BlockSpec((1, d), lambda i, j, k: (0, 0))
        return _mm_act(name, a, w, "stkT", n_out_cols=d, k_total=k_total, tn=d, tk=tk,
                       extras=(g_res, r_in, g, b), extra_specs=(_row_spec(tm, d), _row_spec(tm, d), vec, vec),
                       store=ln_bwd_store, sequential=True, dep=dep,
                       out_shape=[jax.ShapeDtypeStruct((rows, d), F32)] + [jax.ShapeDtypeStruct((1, d), F32)] * 2
                       + [jax.ShapeDtypeStruct((rows, d), BF16)],
                       out_specs=[_row_spec(tm, d), vec, vec, _row_spec(tm, d)])

    def mlp_bwd(layer, g_r, g_rb, up, h_in, r_in, send=None):
        def gup_store(outs, acc, upv):
            outs[0][...] = (acc * (2.0 * jnp.maximum(upv.astype(F32), 0.0))).astype(outs[0].dtype)

        tf = min(dff, 1024)
        shp, spec = wide(tf, BF16)
        g_up = _mm_act(f"g_up{layer}", g_rb, wts["down"][layer], "natT", n_out_cols=dff, k_total=d, tn=tf, tk=d,
                       extras=(up,), extra_specs=(_row_spec(tm, tf),), store=gup_store, out_shape=shp, out_specs=spec)[0]
        gr[f"down{layer}"], gr[f"mlp_b_down{layer}"] = _mm_wgrad(
            f"dw_down{layer}", up, g_rb, kw=dff, n=d, tmw=tf, tn=d, a_fn=_relu2, out_dtype=BF16, colsum=True)
        gr[f"up{layer}"], gr[f"mlp_b_up{layer}"] = _mm_wgrad(
            f"dw_up{layer}", h_in, g_up, kw=d, n=dff, tmw=d, tn=min(dff, 2048), shard_cols=n_up, out_dtype=BF16, colsum=True)
        dep = send() if send is not None else None
        g_r_in, gr[f"ln_mix_g{layer}"], gr[f"ln_mix_b{layer}"], g_r_in_b = through_ln(
            f"g_hmid{layer}", g_up, wts["up"][layer], dff, shard_tile(dff, n_up), g_r, r_in,
            small["ln_mix_g"][layer:layer + 1], small["ln_mix_b"][layer:layer + 1], dep=dep)
        return g_r_in, g_r_in_b

    g_r3, g_r3b = mlp_bwd(1, g_r4, g_r4b, up1, h3b, r3)
    g_cout = _mm_act("g_cout", g_r3b, wts["out_c"], "natT", n_out_cols=d, k_total=d, tn=d, tk=d)[0]
    gr["out_c"] = _mm_wgrad("dw_out_c", c_out, g_r3b, kw=d, n=d, tmw=d, tn=d, out_dtype=BF16)
    g_pc, g_lb_parts, g_ng_parts = _hgrn_bwd("hgrn_bwd", proj_c, lb, wts["ng"], g_cout, hg_states, n_batch, lp, pad, n_hg)
    gr["hgrn_gamma"], gr["ng"] = _lower_bound_bwd("hg_lb_bwd", small["hgrn_gamma"], g_lb_parts, g_ng_parts)
    gr["in_c"] = _mm_wgrad("dw_in_c", h2b, g_pc, kw=d, n=4 * d, tmw=d, tn=min(4 * d, 2048), shard_cols=n_c, out_dtype=BF16)
    sent1 = on_grads(1, {"down1": gr["down1"], "up1": gr["up1"], "out_c": gr["out_c"], "in_c": gr["in_c"], "ng": gr["ng"]})
    g_r2, gr["ln_mlp_g0"], gr["ln_mlp_b0"], g_r2b = through_ln(
        "g_h2", g_pc, wts["in_c"], 4 * d, shard_tile(4 * d, n_c), g_r3, r2, small["ln_mlp_g"][0:1], small["ln_mlp_b"][0:1],
        dep=sent1)

    g_r1, g_r1b = mlp_bwd(0, g_r2, g_r2b, up0, h1b, r1, send=lambda: on_grads(2, {"down0": gr["down0"], "up0": gr["up0"]}))
    g_cat = _mm_act("g_cat", g_r1b, wts["out_ab"], "natT", n_out_cols=d, k_total=d, tn=d, tk=d)[0]
    gr["out_ab"] = _mm_wgrad("dw_out_ab", [a_out, b_out], g_r1b, kw=s5w + sbw, n=d, tmw=min(s5w, sbw), tn=d, out_dtype=BF16)
    g_q, g_k, g_v = _attn_bwd("sb_attn_bwd", proj_qkv, g_cat, n_batch, lp, pad, 0, sb_cb, 2 * sb_cb, s5_cb, n_pairs)

    g_y_direct, g_gate = _rowwise("s5_glu_bwd", lambda ga, yv, gt: jax.vjp(_glu, yv, gt)[1](ga),
                                  [(g_cat, 0, s5w), (y, 0, s5w), (gate, 0, s5w)], 2, s5w)

    def gelu_bwd_store(outs, acc, gyd, yp, u, dv):
        gyp = jax.vjp(jax.nn.gelu, yp)[1](acc + gyd)[0]
        outs[0][...] = gyp
        outs[1][...] = dv * gyp
        outs[2][...] = jnp.sum(gyp * u, axis=0, keepdims=True)

    rs = _row_spec(tm, s5w)
    g_ypre, g_u_direct, gd_parts = _mm_act(
        "s5_g_y", g_gate, wts["glu"], "natT", n_out_cols=s5w, k_total=s5w, tn=s5w, tk=s5w,
        extras=(g_y_direct, ypre, proj_u, d_row), extra_specs=(rs, rs, rs, _vec_spec(s5w)), store=gelu_bwd_store,
        out_shape=[jax.ShapeDtypeStruct((rows, s5w), F32)] * 2 + [jax.ShapeDtypeStruct((rows // tm, 1, s5w), F32)],
        out_specs=[rs, rs, pl.BlockSpec((None, 1, s5w), lambda i, j, k: (i, 0, j))])
    gr["glu"], gr["s5_b_glu"] = _mm_wgrad("dw_glu", y, g_gate, kw=s5w, n=s5w, tmw=s5w, tn=s5w, out_dtype=BF16, colsum=True)
    g_sd = _mm_act("s5_g_states", g_ypre, c_blk, "natT", n_out_cols=2 * ns, k_total=s5w, tn=min(2 * ns, 2048), tk=s5w)[0]
    d_cblk = _mm_wgrad("dw_cblk", states, g_ypre, kw=2 * ns, n=s5w, tmw=min(2 * ns, 1024), tn=s5w)
    gs, gl_parts = _s5_scan_bwd("s5_scan_bwd", g_sd, states, lam_row, n_batch, lp, sw)

    def add_store(outs, acc, other):
        outs[0][...] = acc + other

    g_u = _mm_act("s5_g_u", gs, b_blk, "natT", n_out_cols=s5w, k_total=2 * ns, tn=s5w, tk=min(2 * ns, 1024),
                  extras=(g_u_direct,), extra_specs=(rs,), store=add_store)[0]
    d_bblk = _mm_wgrad("dw_bblk", proj_u, gs, kw=s5w, n=2 * ns, tmw=s5w, tn=min(2 * ns, 2048))
    db_re, db_im = _deinterleave(d_bblk, sw)
    dc_re, dc_im = _deinterleave(d_cblk.T, sw)
    glr, gli = _deinterleave(gl_parts, sw)
    g_lam_re, g_lam_im, g_log_dt, g_b_re_t, g_b_im_t, g_d = _s5_params_bwd(
        "s5_params_bwd", lam_re, lam_im, log_dt, b_re_t, b_im_t,
        glr.reshape(n_batch, groups, n_state), gli.reshape(n_batch, groups, n_state),
        _diag_blocks(db_re, groups), _diag_blocks(db_im, groups), gd_parts)

    cat2 = lambda key: jnp.concatenate([gr[key + "0"], gr[key + "1"]], axis=0)
    small_sent = on_small({
        "s5_lam_re": g_lam_re[None], "s5_lam_im": g_lam_im[None], "s5_log_dt": g_log_dt.reshape(1, groups),
        "s5_b_re": g_b_re_t.transpose(0, 2, 1)[None], "s5_b_im": g_b_im_t.transpose(0, 2, 1)[None],
        "s5_c_re": _diag_blocks(dc_re, groups)[None], "s5_c_im": -_diag_blocks(dc_im, groups)[None],
        "s5_d": g_d.reshape(1, groups, grp), "s5_b_glu": gr["s5_b_glu"], "hgrn_gamma": gr["hgrn_gamma"],
        "ln_mix_g": cat2("ln_mix_g"), "ln_mix_b": cat2("ln_mix_b"), "mlp_b_up": cat2("mlp_b_up"),
        "mlp_b_down": cat2("mlp_b_down"), "ln_mlp_g": cat2("ln_mlp_g"), "ln_mlp_b": cat2("ln_mlp_b"),
    }, loss_tile)

    g_pab = [g_u, g_q, g_k, g_v]
    assert s5w == sbw
    gr["in_ab"] = _mm_wgrad("dw_in_ab", h0b, g_pab, kw=d, n=ab_cols, tmw=d, tn=s5w, shard_cols=n_ab, out_dtype=BF16,
                            dep=small_sent)
    g_h0 = _mm_act("g_h0", g_pab, wts["in_ab"], "stkT", n_out_cols=d, k_total=ab_cols, tn=d, tk=shard_tile(s5w, n_ab),
                   extras=(g_r1,), extra_specs=(_row_spec(tm, d),), store=res_store)[0]
    grad_x = g_h0.reshape(n_batch, lp, d)[:, lead:, :]
    g_meta = _meta_grad("g_meta", g_h0, n_batch, lp, pad, n_meta)
    on_grads(3, {"meta": g_meta, "in_ab": gr["in_ab"], "glu": gr["glu"], "out_ab": gr["out_ab"]})
    return grad_x


SMALL_NAMES = ("s5_lam_re", "s5_lam_im", "s5_log_dt", "s5_b_re", "s5_b_im", "s5_c_re", "s5_c_im", "s5_d", "s5_b_glu",
               "hgrn_gamma", "ln_mix_g", "ln_mix_b", "mlp_b_up", "mlp_b_down", "ln_mlp_g", "ln_mlp_b")
WEIGHT_ORDER = ("meta", "w_in_ab", "s5_lam_re", "s5_lam_im", "s5_log_dt", "s5_b_re", "s5_b_im", "s5_c_re", "s5_c_im",
                "s5_d", "s5_w_glu", "s5_b_glu", "w_out_ab", "w_in_c", "hgrn_gamma", "hgrn_norm_g", "w_out_c", "ln_mix_g",
                "ln_mix_b", "mlp_w_up", "mlp_b_up", "mlp_w_down", "mlp_b_down", "ln_mlp_g", "ln_mlp_b")


def kernel(x, meta, w_in_ab, s5_lam_re, s5_lam_im, s5_log_dt, s5_b_re, s5_b_im, s5_c_re, s5_c_im, s5_d, s5_w_glu, s5_b_glu, w_out_ab, w_in_c, hgrn_gamma, hgrn_norm_g, w_out_c, ln_mix_g, ln_mix_b, mlp_w_up, mlp_b_up, mlp_w_down, mlp_b_down, ln_mlp_g, ln_mlp_b, loss_target, m_meta, m_w_in_ab, m_s5_lam_re, m_s5_lam_im, m_s5_log_dt, m_s5_b_re, m_s5_b_im, m_s5_c_re, m_s5_c_im, m_s5_d, m_s5_w_glu, m_s5_b_glu, m_w_out_ab, m_w_in_c, m_hgrn_gamma, m_hgrn_norm_g, m_w_out_c, m_ln_mix_g, m_ln_mix_b, m_mlp_w_up, m_mlp_b_up, m_mlp_w_down, m_mlp_b_down, m_ln_mlp_g, m_ln_mlp_b, v_meta, v_w_in_ab, v_s5_lam_re, v_s5_lam_im, v_s5_log_dt, v_s5_b_re, v_s5_b_im, v_s5_c_re, v_s5_c_im, v_s5_d, v_s5_w_glu, v_s5_b_glu, v_w_out_ab, v_w_in_c, v_hgrn_gamma, v_hgrn_norm_g, v_w_out_c, v_ln_mix_g, v_ln_mix_b, v_mlp_w_up, v_mlp_b_up, v_mlp_w_down, v_mlp_b_down, v_ln_mlp_g, v_ln_mlp_b):
    args = dict(locals())
    w = {n: args[n] for n in WEIGHT_ORDER}
    mom = {n: args["m_" + n] for n in WEIGHT_ORDER}
    var = {n: args["v_" + n] for n in WEIGHT_ORDER}
    d = x.shape[2]
    n_meta = meta.shape[0]

    cast = lambda a: a.astype(BF16)
    early = _exchange_start("gather_early_start", [w["meta"], cast(w["w_in_ab"][0]), cast(w["s5_w_glu"][0]),
                                                   cast(w["w_out_ab"][0])], False)
    late = _exchange_start("gather_late_start", [w["hgrn_norm_g"], cast(w["w_in_c"][0]), cast(w["w_out_c"][0]),
                                                 cast(w["mlp_w_up"][0]), cast(w["mlp_w_up"][1]),
                                                 cast(w["mlp_w_down"][0]), cast(w["mlp_w_down"][1])], False, dep=early["token"])
    a_meta, a_in_ab, a_glu, a_out_ab = _exchange_wait("gather_early_wait", early, late["token"])
    wts = {"in_ab": a_in_ab, "glu": a_glu.reshape(-1, a_glu.shape[2]), "out_ab": a_out_ab.reshape(-1, d)}
    meta_full = a_meta.transpose(1, 0, 2).reshape(n_meta, d)
    small = {n: w[n] for n in SMALL_NAMES}

    def late_weights(after):
        a_ng, a_in_c, a_out_c, a_up0, a_up1, a_dn0, a_dn1 = _exchange_wait("gather_late_wait", late, after)
        return {"in_c": a_in_c, "ng": a_ng.transpose(1, 0, 2).reshape(1, d), "out_c": a_out_c.reshape(-1, d),
                "up": [a_up0, a_up1], "down": [a_dn0.reshape(-1, d), a_dn1.reshape(-1, d)]}

    n_loc = d // N_DEV
    rows_of = lambda g: g.reshape(N_DEV, -1, g.shape[-1])
    cols_of = lambda g: g.reshape(g.shape[0], N_DEV, n_loc).transpose(1, 0, 2)
    sent = {}

    def on_grads(stage, g):
        if stage == 1:
            order = (("mlp_w_down", 1), ("mlp_w_up", 1), ("w_out_c", 0), ("w_in_c", 0), ("hgrn_norm_g", None))
            parts = [rows_of(g["down1"]), g["up1"], rows_of(g["out_c"]), g["in_c"], cols_of(g["ng"])]
        elif stage == 2:
            order = (("mlp_w_down", 0), ("mlp_w_up", 0))
            parts = [rows_of(g["down0"]), g["up0"]]
        else:
            order = (("w_out_ab", 0), ("s5_w_glu", 0), ("w_in_ab", 0), ("meta", None))
            parts = [rows_of(g["out_ab"]), rows_of(g["glu"]), g["in_ab"], cols_of(g["meta"])]
        sent[stage] = (order, _exchange_start(f"scatter_start{stage}", parts, True))
        return sent[stage][1]["token"]

    def on_small(sg, loss_tile):
        g_pack = _pack_rows([sg[n] for n in SMALL_NAMES] + [loss_tile], PACK_COLS)
        sent["small"] = _exchange_start("gather_small_start", [g_pack], False)
        return sent["small"]["token"]

    grad_x = _local_step(x, loss_target, meta_full, wts, small, late_weights, on_grads, on_small)
    small_sent = sent["small"]
    tile = (SUBLANES, LANES)
    shapes = [w[n].shape for n in SMALL_NAMES] + [tile]
    zeros = jnp.zeros(tile, F32)
    w_pack = _pack_rows([w[n] for n in SMALL_NAMES] + [zeros], PACK_COLS)
    m_pack = _pack_rows([mom[n] for n in SMALL_NAMES] + [zeros], PACK_COLS)
    v_pack = _pack_rows([var[n] for n in SMALL_NAMES] + [zeros], PACK_COLS)

    received, res = {}, {}

    def wait(stage, after):
        order, handle = sent[stage]
        for key, rc in zip(order, _exchange_wait(f"scatter_wait{stage}", handle, after)):
            received[key] = rc

    def update(nm):
        layered = w[nm].ndim == 3
        parts = [received[(nm, l)] for l in range(w[nm].shape[0])] if layered else received[(nm, None)]
        res[nm] = _adamw_summed(f"adamw_{nm}", parts, w[nm], mom[nm], var[nm])
        return res[nm][0]

    wait(1, sent[3][1]["token"])
    done = [update(nm) for nm in ("w_out_c", "w_in_c", "hgrn_norm_g")]
    wait(2, done[0])
    done = [update(nm) for nm in ("mlp_w_up", "mlp_w_down")]
    g_all = _exchange_wait("gather_small_wait", small_sent, done[0])[0]
    packed = _adamw_summed("adamw_small", g_all, w_pack, m_pack, v_pack)
    wait(3, packed[0])
    for nm in ("w_out_ab", "s5_w_glu", "w_in_ab", "meta"):
        update(nm)
    unpacked = [_unpack_rows(p, shapes, PACK_COLS) for p in packed]
    loss = unpacked[0][-1][0, 0]

    def pick(nm, which):
        return unpacked[which][SMALL_NAMES.index(nm)] if nm in SMALL_NAMES else res[nm][which]

    return (loss, grad_x, *[pick(n, 0) for n in WEIGHT_ORDER], *[pick(n, 1) for n in WEIGHT_ORDER],
            *[pick(n, 2) for n in WEIGHT_ORDER], *[pick(n, 3) for n in WEIGHT_ORDER])
```

```python
import functools
import math

import jax
import jax.numpy as jnp
from jax import lax
from jax.experimental import pallas as pl
from jax.experimental.pallas import tpu as pltpu

F32 = jnp.float32
BF16 = jnp.bfloat16

N_DEV = 8
DEPTH = 2
ALPHA = (2.0 * DEPTH) ** 0.25
LN_EPS = 1e-5
RMS_EPS = 1e-6
SB_HEAD_DIM = 64
HG_DK = 128
HG_CHUNK = 64
LANES = 128
SUBLANES = 8
PACKED_ROWS = 16
VMEM_LIMIT_BYTES = 56 * 1024 * 1024
ROW_TILE = 1088
SCAN_LANES = 256
SCAN_UNROLL = 2
PACK_COLS = 1024

ADAM_LR = 0.001
ADAM_B1 = 0.9
ADAM_B2 = 0.999
ADAM_EPS = 1e-08
ADAM_WD = 0.01
ADAM_STEP = 10

NN = (((1,), (0,)), ((), ()))
NT = (((1,), (1,)), ((), ()))
TN = (((0,), (0,)), ((), ()))


def _tile(n, pref, align=SUBLANES):
    t = min(n, pref)
    t -= t % align
    while t >= align:
        if n % t == 0:
            return t
        t -= align
    return n


def _unrolled_loop(n, body, init, unroll):
    assert n % unroll == 0

    def outer(t, carry):
        for u in range(unroll):
            carry = body(t * unroll + u, carry)
        return carry

    return lax.fori_loop(0, n // unroll, outer, init)


def _params(sem):
    return pltpu.CompilerParams(dimension_semantics=sem, vmem_limit_bytes=VMEM_LIMIT_BYTES)


def _dot_raw(a, b, dims):
    return lax.dot_general(a.astype(BF16), b.astype(BF16), dims, preferred_element_type=F32)


def _make_dot(dims, da_rule, db_rule):
    @jax.custom_vjp
    def f(a, b):
        return _dot_raw(a, b, dims)

    def fwd(a, b):
        return _dot_raw(a, b, dims), (a, b)

    def bwd(res, g):
        a, b = res
        return da_rule(g, a, b), db_rule(g, a, b)

    f.defvjp(fwd, bwd)
    return f


_DOTS = {
    NN: _make_dot(NN, lambda g, a, b: _dot_raw(g, b, NT), lambda g, a, b: _dot_raw(a, g, TN)),
    NT: _make_dot(NT, lambda g, a, b: _dot_raw(g, b, NN), lambda g, a, b: _dot_raw(g, a, TN)),
    TN: _make_dot(TN, lambda g, a, b: _dot_raw(b, g, NT), lambda g, a, b: _dot_raw(a, g, NN)),
}


def _dot(a, b, dims):
    return _DOTS[dims](a, b)


def _running_sums(a, tri_ones, split=False):
    hi = a.astype(BF16)
    out = lax.dot_general(hi, tri_ones, NN, preferred_element_type=F32)
    if split:
        lo = (a - hi.astype(F32)).astype(BF16)
        out = out + lax.dot_general(lo, tri_ones, NN, preferred_element_type=F32)
    return out


def _piece_specs(pieces, block_rows, block_cols, row_of, col_of, cb0):
    per = pieces[0].shape[1] // block_cols if len(pieces) > 1 else None
    specs = []
    for p in range(len(pieces)):
        if per is None:
            specs.append(pl.BlockSpec((block_rows, block_cols), lambda *g: (row_of(*g), cb0 + col_of(*g))))
        else:
            specs.append(pl.BlockSpec(
                (block_rows, block_cols),
                lambda *g, p=p: (row_of(*g), jnp.clip(col_of(*g) - p * per, 0, per - 1))))
    return specs, per


def _mm_call(name, grid, dims, a_pieces, a_specs, a_sel, b_pieces, b_specs, b_sel, extras, extra_specs,
             out_shape, out_specs, acc_shape, a_fn, store, colsum_width=0, sequential=False, deps=()):
    na, nb, ne, no, nd = len(a_pieces), len(b_pieces), len(extras), len(out_shape), len(deps)
    nk = grid[2]

    def body(*refs):
        a_refs, b_refs = refs[:na], refs[na:na + nb]
        extra = refs[na + nb:na + nb + ne]
        outs = refs[na + nb + ne + nd:na + nb + ne + nd + no]
        acc = refs[na + nb + ne + nd + no]
        ids = (pl.program_id(0), pl.program_id(1), pl.program_id(2))
        k = ids[2]

        @pl.when(k == 0)
        def _():
            acc[...] = jnp.zeros_like(acc)

        def run(a_ref, b_ref):
            a = a_ref[...]
            if a_fn is not None:
                a = a_fn(a)
            b = b_ref[...]
            if b.ndim == 3 and dims == NN:
                n = b.shape[2]
                for q in range(b.shape[0]):
                    acc[:, q * n:(q + 1) * n] += _dot_raw(a, b[q], dims)
            elif b.ndim == 3:
                acc[...] += _dot_raw(a, jnp.concatenate([b[q] for q in range(b.shape[0])], axis=1), dims)
            else:
                acc[...] += _dot_raw(a, b, dims)
            if colsum_width:
                cs = refs[-1]
                first = ids[1] == 0

                @pl.when(first & (k == 0))
                def _():
                    cs[...] = jnp.zeros_like(cs)

                @pl.when(first)
                def _():
                    cs[...] += jnp.sum(b.astype(F32), axis=0, keepdims=True)

        if na == 1 and nb == 1:
            run(a_refs[0], b_refs[0])
        elif nb == 1:
            per, fn = a_sel
            which = fn(*ids) // per
            for p in range(na):
                pl.when(which == p)(functools.partial(run, a_refs[p], b_refs[0]))
        else:
            assert na == 1
            per, fn = b_sel
            which = fn(*ids) // per
            for p in range(nb):
                pl.when(which == p)(functools.partial(run, a_refs[0], b_refs[p]))

        @pl.when(k == nk - 1)
        def _():
            if sequential:
                store(outs, acc[...], *[e[...] for e in extra], first_step=(ids[0] == 0) & (ids[1] == 0))
            else:
                store(outs, acc[...], *[e[...] for e in extra])
            if colsum_width:
                @pl.when(ids[1] == 0)
                def _():
                    outs[-1][...] = refs[-1][...]

    scratch = [pltpu.VMEM(acc_shape, F32)]
    if colsum_width:
        scratch.append(pltpu.VMEM((1, colsum_width), F32))
    sem = ("parallel", "arbitrary", "arbitrary") if colsum_width else ("parallel", "parallel", "arbitrary")
    if sequential:
        sem = ("arbitrary",) * 3
    return pl.pallas_call(
        body, name=name, grid=grid,
        in_specs=[*a_specs, *b_specs, *extra_specs, *[pl.BlockSpec(memory_space=pl.ANY)] * nd], out_specs=out_specs,
        out_shape=out_shape, scratch_shapes=scratch, compiler_params=_params(sem),
    )(*a_pieces, *b_pieces, *extras, *deps)


def _store_plain(outs, acc):
    outs[0][...] = acc.astype(outs[0].dtype)


def _row_spec(tm, tn):
    return pl.BlockSpec((tm, tn), lambda i, j, k: (i, j))


def _vec_spec(tn):
    return pl.BlockSpec((1, tn), lambda i, j, k: (0, j))


def _mm_act(name, a, w, wkind, *, n_out_cols, k_total, tn, tk, a_cb0=0, a_fn=None, extras=(), extra_specs=(),
            store=_store_plain, out_shape=None, out_specs=None, sequential=False, dep=None):
    a_pieces = list(a) if isinstance(a, (list, tuple)) else [a]
    rows = a_pieces[0].shape[0]
    tm = _tile(rows, ROW_TILE)
    grid = (rows // tm, n_out_cols // tn, k_total // tk)
    a_specs, per = _piece_specs(a_pieces, tm, tk, lambda i, j, k: i, lambda i, j, k: k, a_cb0)
    if wkind == "nat":
        b_spec, dims = pl.BlockSpec((tk, tn), lambda i, j, k: (k, j)), NN
    elif wkind == "stk":
        n = w.shape[2]
        assert tn % n == 0
        b_spec, dims = pl.BlockSpec((tn // n, tk, n), lambda i, j, k: (j, k, 0)), NN
    elif wkind == "natT":
        b_spec, dims = pl.BlockSpec((tn, tk), lambda i, j, k: (j, k)), NT
    else:
        n = w.shape[2]
        assert wkind == "stkT" and tk % n == 0
        b_spec, dims = pl.BlockSpec((tk // n, tn, n), lambda i, j, k: (k, j, 0)), NT
    if out_shape is None:
        out_shape = [jax.ShapeDtypeStruct((rows, n_out_cols), F32)]
        out_specs = [_row_spec(tm, tn)]
    return _mm_call(name, grid, dims, a_pieces, a_specs, (per, lambda i, j, k: k), [w], [b_spec], None,
                    list(extras), list(extra_specs), out_shape, out_specs, (tm, tn), a_fn, store,
                    sequential=sequential, deps=() if dep is None else (dep,))


def _mm_wgrad(name, a, g, *, kw, n, tmw, tn, a_cb0=0, a_fn=None, shard_cols=0, out_dtype=F32, colsum=False, dep=None):
    a_pieces = list(a) if isinstance(a, (list, tuple)) else [a]
    g_pieces = list(g) if isinstance(g, (list, tuple)) else [g]
    rows = a_pieces[0].shape[0]
    tr = _tile(rows, ROW_TILE)
    grid = (n // tn, kw // tmw, rows // tr)
    a_specs, a_per = _piece_specs(a_pieces, tr, tmw, lambda j, i, k: k, lambda j, i, k: i, a_cb0)
    g_specs, g_per = _piece_specs(g_pieces, tr, tn, lambda j, i, k: k, lambda j, i, k: j, 0)
    if shard_cols:
        per = tn // shard_cols
        out_shape = [jax.ShapeDtypeStruct((n // shard_cols, kw, shard_cols), out_dtype)]
        out_specs = [pl.BlockSpec((per, tmw, shard_cols), lambda j, i, k: (j, i, 0))]

        def store(outs, acc):
            for q in range(per):
                outs[0][q] = acc[:, q * shard_cols:(q + 1) * shard_cols].astype(out_dtype)
    else:
        out_shape = [jax.ShapeDtypeStruct((kw, n), out_dtype)]
        out_specs = [pl.BlockSpec((tmw, tn), lambda j, i, k: (i, j))]

        def store(outs, acc):
            outs[0][...] = acc.astype(out_dtype)
    if colsum:
        out_shape.append(jax.ShapeDtypeStruct((1, n), F32))
        out_specs.append(pl.BlockSpec((1, tn), lambda j, i, k: (0, j)))
    res = _mm_call(name, grid, TN, a_pieces, a_specs, (a_per, lambda j, i, k: i), g_pieces, g_specs,
                   (g_per, lambda j, i, k: j), [], [], out_shape, out_specs, (tmw, tn), a_fn, store,
                   colsum_width=tn if colsum else 0, deps=() if dep is None else (dep,))
    return res if colsum else res[0]


def _ln(x, g, b):
    mu = jnp.mean(x, axis=-1, keepdims=True)
    xc = x - mu
    var = jnp.mean(xc * xc, axis=-1, keepdims=True)
    return xc * lax.rsqrt(var + LN_EPS) * g + b


def _relu2(x):
    r = jnp.maximum(x.astype(F32), 0.0)
    return r * r


def _glu(y, gate):
    return y * jax.nn.sigmoid(gate)


def _rowwise(name, fn, ins, n_out, width):
    rows = ins[0][0].shape[0]
    tm = _tile(rows, ROW_TILE)

    def body(*refs):
        res = fn(*[r[...] for r in refs[:len(ins)]])
        for o, v in zip(refs[len(ins):], res):
            o[...] = v

    return pl.pallas_call(
        body, name=name, grid=(rows // tm,),
        in_specs=[pl.BlockSpec((tm, wd), lambda i, cb=cb: (i, cb)) for _, cb, wd in ins],
        out_specs=[pl.BlockSpec((tm, width), lambda i: (i, 0))] * n_out,
        out_shape=[jax.ShapeDtypeStruct((rows, width), F32)] * n_out, compiler_params=_params(("parallel",)),
    )(*[a for a, _, _ in ins])


def _loss_grad(name, r, g, b, target, n_batch, lp, lead):
    rows, d = r.shape
    nq = lp // LANES
    lead_blocks = lead // LANES

    def body(r_ref, g_ref, b_ref, t_ref, gr_ref, gg_ref, gb_ref, loss_ref, grb_ref):
        i = pl.program_id(1)

        @pl.when((pl.program_id(0) == 0) & (i == 0))
        def _():
            loss_ref[...] = jnp.zeros_like(loss_ref)
            gg_ref[...] = jnp.zeros_like(gg_ref)
            gb_ref[...] = jnp.zeros_like(gb_ref)

        h, vjp = jax.vjp(_ln, r_ref[...], g_ref[...], b_ref[...])
        diff = jnp.where(i >= lead_blocks, h - t_ref[...], 0.0)
        gr, gg, gb = vjp(diff * (1.0 / d))
        gr_ref[...] = gr
        grb_ref[...] = gr.astype(BF16)
        gg_ref[...] += gg
        gb_ref[...] += gb
        loss_ref[...] += 0.5 * jnp.sum(diff * diff) * (1.0 / d)

    vec = pl.BlockSpec((1, d), lambda b, i: (0, 0))
    row = pl.BlockSpec((LANES, d), lambda b, i: (b * nq + i, 0))
    return pl.pallas_call(
        body, name=name, grid=(n_batch, nq),
        in_specs=[row, vec, vec, pl.BlockSpec((None, LANES, d), lambda b, i: (b, jnp.maximum(i - lead_blocks, 0), 0))],
        out_specs=[row, vec, vec, pl.BlockSpec((SUBLANES, LANES), lambda b, i: (0, 0)), row],
        out_shape=[jax.ShapeDtypeStruct((rows, d), F32), jax.ShapeDtypeStruct((1, d), F32),
                   jax.ShapeDtypeStruct((1, d), F32), jax.ShapeDtypeStruct((SUBLANES, LANES), F32),
                   jax.ShapeDtypeStruct((rows, d), BF16)],
        compiler_params=_params(("arbitrary", "arbitrary")),
    )(r, g, b, target)


def _meta_grad(name, g_h0, n_batch, lp, pad, n_meta):
    d = g_h0.shape[1]
    per = lp // n_meta
    at = pad // n_meta

    def body(g_ref, o_ref):
        @pl.when(pl.program_id(0) == 0)
        def _():
            o_ref[...] = jnp.zeros_like(o_ref)

        o_ref[...] += g_ref[...]

    return pl.pallas_call(
        body, name=name, grid=(n_batch,),
        in_specs=[pl.BlockSpec((n_meta, d), lambda b: (b * per + at, 0))],
        out_specs=pl.BlockSpec((n_meta, d), lambda b: (0, 0)),
        out_shape=jax.ShapeDtypeStruct((n_meta, d), F32),
        compiler_params=_params(("arbitrary",)),
    )(g_h0)


def _s5_param_fn(lr, li, ldt, br, bi):
    dt = jnp.exp(ldt)
    e = jnp.exp(lr * dt)
    w = li * dt
    lbr = e * jnp.cos(w)
    lbi = e * jnp.sin(w)
    nr = lbr - 1.0
    den = lr * lr + li * li
    cr = (nr * lr + lbi * li) / den
    ci = (lbi * lr - nr * li) / den
    bbr = cr[:, None, :] * br - ci[:, None, :] * bi
    bbi = cr[:, None, :] * bi + ci[:, None, :] * br
    return lbr, lbi, bbr, bbi


def _s5_params(name, lr, li, ldt, br, bi):
    def body(lr_ref, li_ref, ldt_ref, br_ref, bi_ref, o1, o2, o3, o4):
        res = _s5_param_fn(lr_ref[...], li_ref[...], ldt_ref[...], br_ref[...], bi_ref[...])
        for o, v in zip((o1, o2, o3, o4), res):
            o[...] = v

    shp = [jax.ShapeDtypeStruct(lr.shape, F32)] * 2 + [jax.ShapeDtypeStruct(br.shape, F32)] * 2
    return pl.pallas_call(body, name=name, out_shape=shp)(lr, li, ldt, br, bi)


def _s5_params_bwd(name, lr, li, ldt, br, bi, g_lbr, g_lbi, g_bbr, g_bbi, gd_parts):
    def body(lr_ref, li_ref, ldt_ref, br_ref, bi_ref, g1, g2, g3, g4, gd_ref, o1, o2, o3, o4, o5, o6):
        _, vjp = jax.vjp(_s5_param_fn, lr_ref[...], li_ref[...], ldt_ref[...], br_ref[...], bi_ref[...])
        res = vjp((jnp.sum(g1[...], axis=0), jnp.sum(g2[...], axis=0), g3[...], g4[...]))
        for o, v in zip((o1, o2, o3, o4, o5), res):
            o[...] = v
        o6[...] = jnp.sum(gd_ref[...], axis=0)

    shp = ([jax.ShapeDtypeStruct(lr.shape, F32)] * 2 + [jax.ShapeDtypeStruct(ldt.shape, F32)]
           + [jax.ShapeDtypeStruct(br.shape, F32)] * 2 + [jax.ShapeDtypeStruct(gd_parts.shape[1:], F32)])
    return pl.pallas_call(body, name=name, out_shape=shp)(lr, li, ldt, br, bi, g_lbr, g_lbi, g_bbr, g_bbi, gd_parts)


def _interleave(re, im, w):
    nj = re.shape[-1] // w
    return jnp.concatenate([x[..., j * w:(j + 1) * w] for j in range(nj) for x in (re, im)], axis=-1)


def _deinterleave(x, w):
    nj = x.shape[-1] // (2 * w)
    return (jnp.concatenate([x[..., 2 * j * w:(2 * j + 1) * w] for j in range(nj)], axis=-1),
            jnp.concatenate([x[..., (2 * j + 1) * w:(2 * j + 2) * w] for j in range(nj)], axis=-1))


def _cmul(ar, ai, br, bi):
    return ar * br - ai * bi, ar * bi + ai * br


def _powers(lr, li):
    p = [(lr, li)]
    p.append(_cmul(*p[0], *p[0]))
    p.append(_cmul(*p[1], *p[0]))
    p.append(_cmul(*p[1], *p[1]))
    p.append(_cmul(*p[3], *p[0]))
    p.append(_cmul(*p[3], *p[1]))
    p.append(_cmul(*p[3], *p[2]))
    p.append(_cmul(*p[3], *p[3]))
    return p


def _scan_steps(pw, shifts, keep):
    return [(sh, jnp.where(m, pw[s - 1][0], 0.0), jnp.where(m, pw[s - 1][1], 0.0))
            for s, sh, m in zip((1, 2, 4), shifts, keep)]


def _scan_tile(xr, xi, steps):
    for sh, br, bi in steps:
        rr = pltpu.roll(xr, sh, 0)
        ri = pltpu.roll(xi, sh, 0)
        xr, xi = xr + (br * rr - bi * ri), xi + (br * ri + bi * rr)
    return xr, xi


def _s5_scan(name, bu, lam, n_batch, lp, w):
    rows, two_ns = bu.shape
    nj = two_ns // (2 * w)

    def body(x_ref, lam_ref, s_ref):
        pw = _powers(lam_ref[:, :w], lam_ref[:, w:])
        tab_r = jnp.concatenate([p[0] for p in pw], axis=0)
        tab_i = jnp.concatenate([p[1] for p in pw], axis=0)
        row = lax.broadcasted_iota(jnp.int32, (SUBLANES, w), 0)
        steps = _scan_steps(pw, (1, 2, 4), [row >= s for s in (1, 2, 4)])

        def packed_tile(t, carry):
            cr, ci = carry
            r0 = pl.multiple_of(t * PACKED_ROWS, PACKED_ROWS)
            x = x_ref[pl.ds(r0, PACKED_ROWS), :].astype(F32)
            done = []
            for half in range(PACKED_ROWS // SUBLANES):
                xt = x[half * SUBLANES:(half + 1) * SUBLANES, :]
                xr, xi = _scan_tile(xt[:, :w], xt[:, w:], steps)
                sr = xr + (tab_r * cr - tab_i * ci)
                si = xi + (tab_r * ci + tab_i * cr)
                done.append(jnp.concatenate([sr, si], axis=1))
                cr, ci = sr[SUBLANES - 1:, :], si[SUBLANES - 1:, :]
            s_ref[pl.ds(r0, PACKED_ROWS), :] = jnp.concatenate(done, axis=0).astype(s_ref.dtype)
            return cr, ci

        zero = jnp.zeros((1, w), F32)
        _unrolled_loop(lp // PACKED_ROWS, packed_tile, (zero, zero), SCAN_UNROLL)

    spec = pl.BlockSpec((lp, 2 * w), lambda b, j: (b, j))
    return pl.pallas_call(
        body, name=name, grid=(n_batch, nj), in_specs=[spec, pl.BlockSpec((1, 2 * w), lambda b, j: (0, j))],
        out_specs=spec, out_shape=jax.ShapeDtypeStruct((rows, two_ns), BF16),
        compiler_params=_params(("parallel", "parallel")),
    )(bu, lam)


def _s5_scan_bwd(name, gd, states, lam, n_batch, lp, w):
    rows, two_ns = gd.shape
    nj = two_ns // (2 * w)

    def body(x_ref, s_ref, lam_ref, g_ref, gl_ref):
        pw = _powers(lam_ref[:, :w], -lam_ref[:, w:])
        tab_r = jnp.concatenate([p[0] for p in reversed(pw)], axis=0)
        tab_i = jnp.concatenate([p[1] for p in reversed(pw)], axis=0)
        row = lax.broadcasted_iota(jnp.int32, (SUBLANES, w), 0)
        steps = _scan_steps(pw, [SUBLANES - s for s in (1, 2, 4)], [row < SUBLANES - s for s in (1, 2, 4)])

        n_packed = lp // PACKED_ROWS
        halves = PACKED_ROWS // SUBLANES

        def packed_tile(u, carry):
            cr, ci, ar, ai = carry
            t = n_packed - 1 - u
            r0 = pl.multiple_of(t * PACKED_ROWS, PACKED_ROWS)
            x = x_ref[pl.ds(r0, PACKED_ROWS), :].astype(F32)
            cur = s_ref[pl.ds(r0, PACKED_ROWS), :].astype(F32)
            p0 = pl.multiple_of(jnp.maximum(t - 1, 0) * PACKED_ROWS, PACKED_ROWS)
            before = s_ref[pl.ds(p0, PACKED_ROWS), :].astype(F32)[PACKED_ROWS - 1:, :] * jnp.where(t > 0, 1.0, 0.0)
            done = [None] * halves
            for half in reversed(range(halves)):
                rows_h = slice(half * SUBLANES, (half + 1) * SUBLANES)
                xt, st = x[rows_h, :], cur[rows_h, :]
                xr, xi = _scan_tile(xt[:, :w], xt[:, w:], steps)
                gr = xr + (tab_r * cr - tab_i * ci)
                gi = xi + (tab_r * ci + tab_i * cr)
                done[half] = jnp.concatenate([gr, gi], axis=1)
                prev = before if half == 0 else cur[half * SUBLANES - 1:half * SUBLANES, :]
                spr = jnp.where(row >= 1, pltpu.roll(st[:, :w], 1, 0), prev[:, :w])
                spi = jnp.where(row >= 1, pltpu.roll(st[:, w:], 1, 0), prev[:, w:])
                cr, ci, ar, ai = gr[:1, :], gi[:1, :], ar + gr * spr + gi * spi, ai + gi * spr - gr * spi
            g_ref[pl.ds(r0, PACKED_ROWS), :] = jnp.concatenate(done, axis=0).astype(g_ref.dtype)
            return cr, ci, ar, ai

        z1 = jnp.zeros((1, w), F32)
        z8 = jnp.zeros((SUBLANES, w), F32)
        _, _, ar, ai = _unrolled_loop(n_packed, packed_tile, (z1, z1, z8, z8), SCAN_UNROLL)
        gl_ref[...] = jnp.concatenate([jnp.sum(ar, axis=0, keepdims=True), jnp.sum(ai, axis=0, keepdims=True)], axis=1)

    spec = pl.BlockSpec((lp, 2 * w), lambda b, j: (b, j))
    return pl.pallas_call(
        body, name=name, grid=(n_batch, nj),
        in_specs=[spec, spec, pl.BlockSpec((1, 2 * w), lambda b, j: (0, j))],
        out_specs=[spec, pl.BlockSpec((None, 1, 2 * w), lambda b, j: (b, 0, j))],
        out_shape=[jax.ShapeDtypeStruct((rows, two_ns), BF16), jax.ShapeDtypeStruct((n_batch, 1, two_ns), F32)],
        compiler_params=_params(("parallel", "parallel")),
    )(gd, states, lam)


def _log_sigmoid(z):
    return jnp.minimum(z, 0.0) - jnp.log(1.0 + jnp.exp(-jnp.abs(z)))


ATTN_KEYS = 256
ATTN_GROUP = 4


def _attn_block(i, jb, lp, pad):
    start = jb * ATTN_KEYS
    r0 = pl.multiple_of(jnp.minimum(start, lp - ATTN_KEYS), LANES)
    rowpos = i * LANES + lax.broadcasted_iota(jnp.int32, (LANES, ATTN_KEYS), 0)
    keypos = r0 + lax.broadcasted_iota(jnp.int32, (LANES, ATTN_KEYS), 1)
    return r0, (keypos < rowpos) & (keypos >= jnp.maximum(start, pad))


def _tri_ones(strict_upper):
    r = lax.broadcasted_iota(jnp.int32, (ATTN_KEYS, ATTN_KEYS + LANES), 0)
    c = lax.broadcasted_iota(jnp.int32, (ATTN_KEYS, ATTN_KEYS + LANES), 1)
    tri = (r > c) if strict_upper else (r < c)
    return jnp.where((c >= ATTN_KEYS) | tri, 1.0, 0.0).astype(BF16)


def _split_sums(cr):
    rs = cr[:, ATTN_KEYS:]
    return cr[:, :ATTN_KEYS], jnp.concatenate([rs] * (ATTN_KEYS // LANES), axis=1)


def _head_masks():
    lane = lax.broadcasted_iota(jnp.int32, (1, LANES), 1)
    return [lane < SB_HEAD_DIM, lane >= SB_HEAD_DIM]


def _run_groups(n, first, sign, make):
    j, left, g = first, n, ATTN_GROUP
    while g >= 1:
        shift = g.bit_length() - 1
        count = lax.shift_right_logical(left, shift)
        fn = make(g)

        def loop(_, jcur, fn=fn, g=g):
            fn(jcur)
            return jcur + sign * g

        j = lax.fori_loop(0, count, loop, j)
        left = left - lax.shift_left(count, shift)
        g //= 2


def _attn_fwd(name, proj, n_batch, lp, pad, q_cb, k_cb, v_cb, n_pairs):
    rows = proj.shape[0]
    nq = lp // LANES
    scale = SB_HEAD_DIM ** -0.5

    def body(q_ref, k_ref, v_ref, o_ref, acc_s):
        i = pl.program_id(1)
        hm = _head_masks()
        comb = _tri_ones(True)
        n_blocks = lax.shift_right_logical(i + ATTN_KEYS // LANES, (ATTN_KEYS // LANES).bit_length() - 1)

        def pair(hp, carry):
            lanes = pl.ds(pl.multiple_of(hp * LANES, LANES), LANES)
            qs = q_ref[:, lanes] * scale
            qh = [jnp.where(m, qs, 0.0).astype(BF16) for m in hm]
            acc_s[...] = jnp.zeros_like(acc_s)
            o_ref[:, lanes] = jnp.zeros((LANES, LANES), F32)

            def make(group):
                def fn(jtop):
                    chains = []
                    for g in range(group):
                        r0, vis = _attn_block(i, jtop - g, lp, pad)
                        kj = k_ref[pl.ds(r0, ATTN_KEYS), lanes].astype(BF16)
                        vj = v_ref[pl.ds(r0, ATTN_KEYS), lanes]
                        for h in range(2):
                            z = lax.dot_general(qh[h], kj, NT, preferred_element_type=F32)
                            chains.append((h, vis, z, jnp.where(hm[h], vj, 0.0).astype(BF16)))
                    staged = []
                    for h, vis, z, vh in chains:
                        lsz = _log_sigmoid(z)
                        staged.append((h, vis, lsz, _running_sums(jnp.where(vis, lsz - z, 0.0), comb, split=True), vh))
                    out = o_ref[:, lanes]
                    for h, vis, lsz, cr, vh in staged:
                        later, rs = _split_sums(cr)
                        acc = acc_s[h]
                        wgt = jnp.where(vis, jnp.exp(lsz + later + acc), 0.0)
                        acc_s[h] = acc + rs
                        out = out + lax.dot_general(wgt.astype(BF16), vh, NN, preferred_element_type=F32)
                    o_ref[:, lanes] = out
                return fn

            _run_groups(n_blocks, n_blocks - 1, -1, make)
            return carry

        lax.fori_loop(0, n_pairs, pair, 0)

    wide = n_pairs * LANES
    assert q_cb % n_pairs == 0 and k_cb % n_pairs == 0 and v_cb % n_pairs == 0
    return pl.pallas_call(
        body, name=name, grid=(n_batch, nq),
        in_specs=[pl.BlockSpec((LANES, wide), lambda b, i: (b * nq + i, q_cb // n_pairs)),
                  pl.BlockSpec((lp, wide), lambda b, i: (b, k_cb // n_pairs)),
                  pl.BlockSpec((lp, wide), lambda b, i: (b, v_cb // n_pairs))],
        out_specs=pl.BlockSpec((LANES, wide), lambda b, i: (b * nq + i, 0)),
        out_shape=jax.ShapeDtypeStruct((rows, wide), F32),
        scratch_shapes=[pltpu.VMEM((2, LANES, ATTN_KEYS), F32)],
        compiler_params=_params(("parallel", "arbitrary")),
    )(proj, proj, proj)


def _attn_bwd(name, proj, g_out, n_batch, lp, pad, q_cb, k_cb, v_cb, go_cb, n_pairs):
    rows = proj.shape[0]
    nq = lp // LANES
    scale = SB_HEAD_DIM ** -0.5

    def body(q_ref, k_ref, v_ref, go_ref, gq_ref, gk_ref, gv_ref, ga_s, sz_s, acc_s):
        i = pl.program_id(1)

        @pl.when(i == 0)
        def _():
            gk_ref[...] = jnp.zeros_like(gk_ref)
            gv_ref[...] = jnp.zeros_like(gv_ref)

        hm = _head_masks()
        comb_up = _tri_ones(True)
        comb_lo = _tri_ones(False)
        n_blocks = lax.shift_right_logical(i + ATTN_KEYS // LANES, (ATTN_KEYS // LANES).bit_length() - 1)

        def pair(hp, carry):
            lanes = pl.ds(pl.multiple_of(hp * LANES, LANES), LANES)
            qs = q_ref[:, lanes] * scale
            go = go_ref[:, lanes]
            qh = [jnp.where(m, qs, 0.0).astype(BF16) for m in hm]
            goh = [jnp.where(m, go, 0.0).astype(BF16) for m in hm]
            acc_s[...] = jnp.zeros_like(acc_s)

            def make_down(group):
                def fn(jtop):
                    chains = []
                    for g in range(group):
                        j = jtop - g
                        r0, vis = _attn_block(i, j, lp, pad)
                        kj = k_ref[pl.ds(r0, ATTN_KEYS), lanes].astype(BF16)
                        vj = v_ref[pl.ds(r0, ATTN_KEYS), lanes].astype(BF16)
                        for h in range(2):
                            z = lax.dot_general(qh[h], kj, NT, preferred_element_type=F32)
                            gw = lax.dot_general(goh[h], vj, NT, preferred_element_type=F32)
                            chains.append((h, j, r0, vis, z, gw))
                    staged = []
                    for h, j, r0, vis, z, gw in chains:
                        lsz = _log_sigmoid(z)
                        staged.append((h, j, r0, vis, lsz, _running_sums(jnp.where(vis, lsz - z, 0.0), comb_up), gw))
                    for h, j, r0, vis, lsz, cr, gw in staged:
                        later, rs = _split_sums(cr)
                        acc = acc_s[h]
                        wgt = jnp.where(vis, jnp.exp(lsz + later + acc), 0.0)
                        acc_s[h] = acc + rs
                        ga_s[h, j] = gw * wgt
                        sz_s[h, j] = jnp.exp(lsz)
                        gv_ref[pl.ds(r0, ATTN_KEYS), lanes] += lax.dot_general(
                            wgt.astype(BF16), goh[h], TN, preferred_element_type=F32)
                return fn

            _run_groups(n_blocks, n_blocks - 1, -1, make_down)
            acc_s[...] = jnp.zeros_like(acc_s)

            def make_up(group):
                def fn(jbot):
                    pend = []
                    for g in range(group):
                        j = jbot + g
                        r0, vis = _attn_block(i, j, lp, pad)
                        kj = k_ref[pl.ds(r0, ATTN_KEYS), lanes]
                        for h in range(2):
                            ga = ga_s[h, j]
                            pend.append((h, j, r0, vis, ga, _running_sums(ga, comb_lo),
                                         jnp.where(hm[h], kj, 0.0).astype(BF16)))
                    gq = jnp.zeros((LANES, LANES), F32)
                    for h, j, r0, vis, ga, cr, kh in pend:
                        before, rs = _split_sums(cr)
                        pre = acc_s[h]
                        glk = before + pre
                        acc_s[h] = pre + rs
                        sz = sz_s[h, j]
                        gz = jnp.where(vis, ga * (1.0 - sz) - glk * sz, 0.0).astype(BF16)
                        gq = gq + lax.dot_general(gz, kh, NN, preferred_element_type=F32)
                        gk_ref[pl.ds(r0, ATTN_KEYS), lanes] += lax.dot_general(gz, qh[h], TN, preferred_element_type=F32)
                    gq_ref[:, lanes] += gq * scale
                return fn

            gq_ref[:, lanes] = jnp.zeros((LANES, LANES), F32)
            _run_groups(n_blocks, 0, 1, make_up)
            return carry

        lax.fori_loop(0, n_pairs, pair, 0)

    wide = n_pairs * LANES
    assert q_cb % n_pairs == 0 and k_cb % n_pairs == 0 and v_cb % n_pairs == 0 and go_cb % n_pairs == 0
    blk = lambda cb: pl.BlockSpec((LANES, wide), lambda b, i: (b * nq + i, cb // n_pairs))
    full = lambda cb: pl.BlockSpec((lp, wide), lambda b, i: (b, cb // n_pairs))
    shp = jax.ShapeDtypeStruct((rows, wide), F32)
    per_block = pltpu.VMEM((2, -(-lp // ATTN_KEYS), LANES, ATTN_KEYS), F32)
    return pl.pallas_call(
        body, name=name, grid=(n_batch, nq),
        in_specs=[blk(q_cb), full(k_cb), full(v_cb), blk(go_cb)],
        out_specs=[blk(0), full(0), full(0)], out_shape=[shp, shp, shp],
        scratch_shapes=[per_block, per_block, pltpu.VMEM((2, LANES, ATTN_KEYS), F32)],
        compiler_params=_params(("parallel", "arbitrary")),
    )(proj, proj, proj, g_out)


def _lb_fn(gamma):
    g0, g1 = gamma[0:1, :], gamma[1:2, :]
    mx = jnp.maximum(g0, g1)
    e0, e1 = jnp.exp(g0 - mx), jnp.exp(g1 - mx)
    p0, p1 = e0 / (e0 + e1), e1 / (e0 + e1)
    return (p0 + p1) - p0


def _lower_bound(name, gamma):
    def body(g_ref, o_ref):
        o_ref[...] = _lb_fn(g_ref[...])

    return pl.pallas_call(body, name=name, out_shape=jax.ShapeDtypeStruct((1, gamma.shape[1]), F32))(gamma)


def _lower_bound_bwd(name, gamma, g_lb_parts, g_ng_parts):
    def body(g_ref, glb_ref, gng_ref, o_ref, o2_ref):
        _, vjp = jax.vjp(_lb_fn, g_ref[...])
        o_ref[...] = vjp(jnp.sum(glb_ref[...], axis=0))[0]
        o2_ref[...] = jnp.sum(gng_ref[...], axis=0)

    return pl.pallas_call(
        body, name=name,
        out_shape=[jax.ShapeDtypeStruct(gamma.shape, F32), jax.ShapeDtypeStruct((1, gamma.shape[1]), F32)],
    )(gamma, g_lb_parts, g_ng_parts)


def _tri_times(tril, x, dims):
    hi = x.astype(BF16)
    lo = (x - hi.astype(F32)).astype(BF16)
    t = tril.astype(BF16)
    return (lax.dot_general(t, hi, dims, preferred_element_type=F32)
            + lax.dot_general(t, lo, dims, preferred_element_type=F32))


@jax.custom_vjp
def _cumsum_rows(x, tril):
    return _tri_times(tril, x, NN)


def _cumsum_rows_fwd(x, tril):
    return _tri_times(tril, x, NN), tril


def _cumsum_rows_bwd(tril, g):
    return _tri_times(tril, g, TN), jnp.zeros_like(tril)


_cumsum_rows.defvjp(_cumsum_rows_fwd, _cumsum_rows_bwd)


def _hg_decays(f_pre, lbs, masks, tril):
    f = [[lb + (1.0 - lb) * jax.nn.sigmoid(fc) for fc, lb in zip(row, lbs)] for row in f_pre]
    bcum = [[_cumsum_rows(jnp.log(x) * m, tril) for x in row] for row, m in zip(f, masks)]
    return [[1.0 - x for x in row] for row in f], bcum


def _hg_step(q, f_pre, i_in, g, lbs, ngs, sts, masks, tril):
    k, bcum = _hg_decays(f_pre, lbs, masks, tril)
    v = [[ic * m for ic in row] for row, m in zip(i_in, masks)]
    qd = [[qc * jnp.exp(b) for qc, b in zip(qr, br)] for qr, br in zip(q, bcum)]
    scores = [[jnp.where(tril > 0.5, _dot(a, kk * jnp.exp(-b), NT), 0.0) for a, kk, b in zip(ar, kr, br)]
              for ar, kr, br in zip(qd, k, bcum)]
    inner = [[_dot(s, x, NN) for s, x in zip(sr, vr)] for sr, vr in zip(scores, v)]
    add = [[_dot(x, kk * jnp.exp(b[HG_CHUNK - 1:, :] - b), TN) for x, kk, b in zip(vr, kr, br)]
           for vr, kr, br in zip(v, k, bcum)]
    outs = []
    for qr, br, nr, ar, gr in zip(qd, bcum, inner, add, g):
        o = [n + _dot(a, st, NT) for n, a, st in zip(nr, qr, sts)]
        sts = [jnp.exp(b[HG_CHUNK - 1:, :]) * st + a for b, a, st in zip(br, ar, sts)]
        o = [x * lax.rsqrt(jnp.mean(x * x, axis=-1, keepdims=True) + RMS_EPS) * ng for x, ng in zip(o, ngs)]
        outs.append([x * (gc * jax.nn.sigmoid(gc)) for x, gc in zip(o, gr)])
    return outs, sts


def _hg_consts(c, pad):
    r = lax.broadcasted_iota(jnp.int32, (HG_CHUNK, HG_CHUNK), 0)
    cc = lax.broadcasted_iota(jnp.int32, (HG_CHUNK, HG_CHUNK), 1)
    tril = jnp.where(r >= cc, 1.0, 0.0).astype(F32)
    pos = c * HG_CHUNK + lax.broadcasted_iota(jnp.int32, (HG_CHUNK, 1), 0)
    return tril, jnp.where(pos >= pad, 1.0, 0.0).astype(F32)


HG_HEADS_PER_STEP = 8
HG_CHUNKS_PER_STEP = 2


def _hg_layout(lp, n_heads):
    step_rows = HG_CHUNKS_PER_STEP * HG_CHUNK
    per = min(HG_HEADS_PER_STEP, n_heads)
    assert lp % step_rows == 0 and n_heads % per == 0
    heads = [(h, slice(h * HG_DK, (h + 1) * HG_DK)) for h in range(per)]
    return n_heads // per, lp // step_rows, step_rows, per * HG_DK, heads


def _hg_step_views(step, pad, heads):
    slices = [slice(u * HG_CHUNK, (u + 1) * HG_CHUNK) for u in range(HG_CHUNKS_PER_STEP)]
    consts = [_hg_consts(step * HG_CHUNKS_PER_STEP + u, pad) for u in range(HG_CHUNKS_PER_STEP)]
    load = lambda ref: [[ref[sl, cols] for _, cols in heads] for sl in slices]
    return slices, [m for _, m in consts], consts[0][0], load


def _hgrn_fwd(name, proj, lb, ng, n_batch, lp, pad, n_heads):
    rows = proj.shape[0]
    groups, steps, step_rows, wide, heads = _hg_layout(lp, n_heads)

    def body(q_ref, f_ref, i_ref, g_ref, lb_ref, ng_ref, o_ref, s_ref, st_s):
        t = pl.program_id(2)

        @pl.when(t == 0)
        def _():
            st_s[...] = jnp.zeros_like(st_s)

        slices, masks, tril, load = _hg_step_views(t, pad, heads)
        sts = [st_s[h] for h, _ in heads]
        for (_, cols), st in zip(heads, sts):
            s_ref[:, cols] = st
        outs, sts = _hg_step(load(q_ref), load(f_ref), load(i_ref), load(g_ref),
                             [lb_ref[:, cols] for _, cols in heads], [ng_ref[:, cols] for _, cols in heads],
                             sts, masks, tril)
        for sl, row in zip(slices, outs):
            for (_, cols), o in zip(heads, row):
                o_ref[sl, cols] = o
        for (h, _), st in zip(heads, sts):
            st_s[h] = st

    col = lambda off: pl.BlockSpec((step_rows, wide), lambda b, h, t: (b * steps + t, off * groups + h))
    vec = pl.BlockSpec((1, wide), lambda b, h, t: (0, h))
    return pl.pallas_call(
        body, name=name, grid=(n_batch, groups, steps), in_specs=[col(0), col(1), col(2), col(3), vec, vec],
        out_specs=[col(0), pl.BlockSpec((HG_DK, wide), lambda b, h, t: (b * steps + t, h))],
        out_shape=[jax.ShapeDtypeStruct((rows, n_heads * HG_DK), F32),
                   jax.ShapeDtypeStruct((n_batch * steps * HG_DK, n_heads * HG_DK), F32)],
        scratch_shapes=[pltpu.VMEM((len(heads), HG_DK, HG_DK), F32)],
        compiler_params=_params(("parallel", "parallel", "arbitrary")),
    )(proj, proj, proj, proj, lb, ng)


def _hgrn_bwd(name, proj, lb, ng, g_out, states, n_batch, lp, pad, n_heads):
    rows = proj.shape[0]
    width = n_heads * HG_DK
    groups, steps, step_rows, wide, heads = _hg_layout(lp, n_heads)
    assert groups == 1

    def body(q_ref, f_ref, i_ref, g_ref, lb_ref, ng_ref, go_ref, s_ref, gp_ref, glb_ref, gng_ref, gst_s):
        t = pl.program_id(2)

        @pl.when(t == 0)
        def _():
            gst_s[...] = jnp.zeros_like(gst_s)
            glb_ref[...] = jnp.zeros_like(glb_ref)
            gng_ref[...] = jnp.zeros_like(gng_ref)

        slices, masks, tril, load = _hg_step_views(steps - 1 - t, pad, heads)
        fn = functools.partial(_hg_step, masks=masks, tril=tril)
        _, vjp = jax.vjp(fn, load(q_ref), load(f_ref), load(i_ref), load(g_ref),
                         [lb_ref[:, cols] for _, cols in heads], [ng_ref[:, cols] for _, cols in heads],
                         [s_ref[:, cols] for _, cols in heads])
        gq, gf, gi, gg, glb, gng, gst = vjp((load(go_ref), [gst_s[h] for h, _ in heads]))
        for part, grads in enumerate((gq, gf, gi, gg)):
            for sl, row in zip(slices, grads):
                for (h, _), x in zip(heads, row):
                    lane0 = part * width + h * HG_DK
                    gp_ref[sl, lane0:lane0 + HG_DK] = x.astype(BF16)
        for (h, cols), a, b, c in zip(heads, gst, glb, gng):
            gst_s[h] = a
            glb_ref[:, cols] += b
            gng_ref[:, cols] += c

    col = lambda off: pl.BlockSpec((step_rows, wide), lambda b, h, t: (b * steps + steps - 1 - t, off * groups + h))
    vec = pl.BlockSpec((1, wide), lambda b, h, t: (0, h))
    part = pl.BlockSpec((None, 1, wide), lambda b, h, t: (b, 0, h))
    big = jax.ShapeDtypeStruct((rows, 4 * width), BF16)
    small = jax.ShapeDtypeStruct((n_batch, 1, width), F32)
    return pl.pallas_call(
        body, name=name, grid=(n_batch, groups, steps),
        in_specs=[col(0), col(1), col(2), col(3), vec, vec, col(0),
                  pl.BlockSpec((HG_DK, wide), lambda b, h, t: (b * steps + steps - 1 - t, h))],
        out_specs=[pl.BlockSpec((step_rows, 4 * width), lambda b, h, t: (b * steps + steps - 1 - t, 0)), part, part],
        out_shape=[big, small, small],
        scratch_shapes=[pltpu.VMEM((len(heads), HG_DK, HG_DK), F32)],
        compiler_params=_params(("parallel", "parallel", "arbitrary")),
    )(proj, proj, proj, proj, lb, ng, g_out, states)


def _exchange_copies(src, dst, send, recv, loc, scatter):
    x, y, c = lax.axis_index("x"), lax.axis_index("y"), lax.axis_index("c")
    me = 4 * x + 2 * y + c
    local, remote = [], []
    for w in range(len(src)):
        local.append(pltpu.make_async_copy(src[w].at[me] if scatter else src[w], dst[w].at[me], loc.at[w]))
    for k in range(1, N_DEV):
        px = 1 - x if k & 4 else x
        py = 1 - y if k & 2 else y
        pc = 1 - c if k & 1 else c
        peer = 4 * px + 2 * py + pc
        for w in range(len(src)):
            remote.append(pltpu.make_async_remote_copy(
                src_ref=src[w].at[peer] if scatter else src[w], dst_ref=dst[w].at[me],
                send_sem=send.at[w * (N_DEV - 1) + k - 1], recv_sem=recv.at[w * (N_DEV - 1) + k - 1],
                device_id=(px, py, pc), device_id_type=pl.DeviceIdType.MESH))
    return local, remote


_HBM_SPEC = pl.BlockSpec(memory_space=pltpu.HBM)
_SEM_SPEC = pl.BlockSpec(memory_space=pltpu.SEMAPHORE)
_ANY_SPEC = pl.BlockSpec(memory_space=pl.ANY)
_DATAFLOW = pltpu.SideEffectType.DATAFLOW_SIDE_EFFECTING


def _exchange_start(name, srcs, scatter, dep=None):
    nw = len(srcs)
    srcs = [pltpu.with_memory_space_constraint(s, pltpu.HBM) for s in srcs]
    lands = [pltpu.with_memory_space_constraint(lax.empty(s.shape if scatter else (N_DEV,) + s.shape, s.dtype), pltpu.HBM)
             for s in srcs]
    deps = [] if dep is None else [dep]

    def body(*refs):
        src, dst = refs[:nw], refs[nw:2 * nw]
        send, recv, loc = refs[2 * nw + len(deps):2 * nw + len(deps) + 3]
        token = refs[-1]
        local, remote = _exchange_copies(src, dst, send, recv, loc, scatter)
        for cp in local + remote:
            cp.start()
        token[...] = jnp.zeros_like(token)

    sems = [pltpu.SemaphoreType.DMA((nw * (N_DEV - 1),)), pltpu.SemaphoreType.DMA((nw * (N_DEV - 1),)),
            pltpu.SemaphoreType.DMA((nw,))]
    out = pl.pallas_call(
        body, name=name,
        out_shape=(*sems, *[pltpu.HBM(s.shape, s.dtype) for s in srcs], *[pltpu.HBM(s.shape, s.dtype) for s in lands],
                   jax.ShapeDtypeStruct((SUBLANES, LANES), F32)),
        in_specs=[_HBM_SPEC] * (2 * nw) + [_ANY_SPEC] * len(deps),
        out_specs=(_SEM_SPEC, _SEM_SPEC, _SEM_SPEC, *[_HBM_SPEC] * (2 * nw), pl.BlockSpec(memory_space=pltpu.VMEM)),
        input_output_aliases={i: 3 + i for i in range(2 * nw)},
        compiler_params=pltpu.CompilerParams(has_side_effects=_DATAFLOW),
    )(*srcs, *lands, *deps)
    return {"sems": out[:3], "srcs": out[3:3 + nw], "lands": out[3 + nw:3 + 2 * nw], "token": out[-1], "scatter": scatter}


def _exchange_wait(name, handle, after):
    nw = len(handle["srcs"])
    scatter = handle["scatter"]

    def body(*refs):
        src, dst = refs[:nw], refs[nw:2 * nw]
        send, recv, loc = refs[2 * nw:2 * nw + 3]
        local, remote = _exchange_copies(src, dst, send, recv, loc, scatter)
        for cp in local:
            cp.wait()
        for cp in remote:
            cp.wait_send()
            cp.wait_recv()

    out = pl.pallas_call(
        body, name=name,
        out_shape=(*[pltpu.HBM(s.shape, s.dtype) for s in handle["srcs"]],
                   *[pltpu.HBM(s.shape, s.dtype) for s in handle["lands"]]),
        in_specs=[_HBM_SPEC] * (2 * nw) + [_SEM_SPEC] * 3 + [_ANY_SPEC],
        out_specs=tuple([_HBM_SPEC] * (2 * nw)),
        input_output_aliases={i: i for i in range(2 * nw)},
        compiler_params=pltpu.CompilerParams(has_side_effects=_DATAFLOW),
    )(*handle["srcs"], *handle["lands"], *handle["sems"], after)
    return list(out[nw:])


def _adamw(w, g, m, v):
    m = ADAM_B1 * m + (1.0 - ADAM_B1) * g
    v = ADAM_B2 * v + (1.0 - ADAM_B2) * (g * g)
    m_hat = m / (1.0 - ADAM_B1 ** ADAM_STEP)
    v_hat = v / (1.0 - ADAM_B2 ** ADAM_STEP)
    delta = -ADAM_LR * (m_hat / (jnp.sqrt(v_hat) + ADAM_EPS) + ADAM_WD * w)
    return delta, m, v


def _adamw_summed(name, parts, w, m, v):
    layered = w.ndim == 3
    parts = list(parts) if layered else [parts]
    n_layers = len(parts)
    rows, cols = w.shape[-2:]
    n_parts = parts[0].shape[0]
    tr = _tile(rows, max(SUBLANES, (1 << 18) // cols))

    def body(*refs):
        p_refs = refs[:n_layers]
        w_ref, m_ref, v_ref, g_ref, d_ref, nm_ref, nv_ref = refs[n_layers:]
        layer = pl.program_id(0)

        def run(p_ref):
            g = p_ref[0].astype(F32)
            for s in range(1, n_parts):
                g = g + p_ref[s].astype(F32)
            d, nm, nv = _adamw(w_ref[...], g, m_ref[...], v_ref[...])
            g_ref[...] = g
            d_ref[...] = d
            nm_ref[...] = nm
            nv_ref[...] = nv

        for l in range(n_layers):
            pl.when(layer == l)(functools.partial(run, p_refs[l]))

    if layered:
        spec = pl.BlockSpec((None, tr, cols), lambda l, i: (l, i, 0))
    else:
        spec = pl.BlockSpec((tr, cols), lambda l, i: (i, 0))
    p_specs = [pl.BlockSpec((n_parts, tr, cols), lambda l, i, q=q: (0, jnp.where(l == q, i, 0), 0))
               for q in range(n_layers)]
    shp = jax.ShapeDtypeStruct(w.shape, F32)
    return pl.pallas_call(
        body, name=name, grid=(n_layers, rows // tr), in_specs=[*p_specs, spec, spec, spec],
        out_specs=[spec] * 4, out_shape=[shp] * 4, compiler_params=_params(("parallel", "parallel")),
    )(*parts, w, m, v)


def _pack_rows(arrays, cols):
    out = []
    for a in arrays:
        flat = a.reshape(-1)
        n = -(-flat.shape[0] // cols) * cols
        out.append(jnp.pad(flat, (0, n - flat.shape[0])).reshape(-1, cols))
    packed = jnp.concatenate(out, axis=0)
    return jnp.pad(packed, ((0, -packed.shape[0] % SUBLANES), (0, 0)))


def _unpack_rows(packed, shapes, cols):
    out, r = [], 0
    for s in shapes:
        n = math.prod(s)
        nr = -(-n // cols)
        out.append(packed[r:r + nr].reshape(-1)[:n].reshape(s))
        r += nr
    return out


def _block_diag(blocks):
    g, a, b = blocks.shape
    eye = jnp.eye(g, dtype=blocks.dtype)
    return (eye[:, None, :, None] * blocks[:, :, None, :]).reshape(g * a, g * b)


def _diag_blocks(dense, g):
    a, b = dense.shape[0] // g, dense.shape[1] // g
    return jnp.einsum("gagb->gab", dense.reshape(g, a, g, b))


def _local_step(x, target, meta, wts, small, late_weights, on_grads, on_small):
    n_batch, seq, d = x.shape
    n_meta = meta.shape[0]
    pad = -(seq + n_meta) % LANES
    lead = pad + n_meta
    lp = lead + seq
    rows = n_batch * lp
    s5w = wts["glu"].shape[0]
    n_ab = wts["in_ab"].shape[2]
    ab_cols = wts["in_ab"].shape[0] * n_ab
    sbw = (ab_cols - s5w) // 3
    dff = small["mlp_b_up"].shape[1]
    n_pairs = sbw // LANES
    n_hg = d // HG_DK
    s5_cb = s5w // LANES
    sb_cb = sbw // LANES
    tm = _tile(rows, ROW_TILE)
    groups, n_state, grp = small["s5_b_re"].shape[1:]
    ns = groups * n_state
    sw = min(SCAN_LANES, ns)

    h0 = jnp.concatenate(
        [jnp.zeros((n_batch, pad, d), F32), jnp.broadcast_to(meta[None], (n_batch, n_meta, d)), x], axis=1
    ).reshape(rows, d)

    lam_re, lam_im = small["s5_lam_re"][0], small["s5_lam_im"][0]
    log_dt = small["s5_log_dt"][0][:, None]
    b_re_t = small["s5_b_re"][0].transpose(0, 2, 1)
    b_im_t = small["s5_b_im"][0].transpose(0, 2, 1)
    c_re, c_im = small["s5_c_re"][0], small["s5_c_im"][0]
    lbr, lbi, bbr, bbi = _s5_params("s5_params", lam_re, lam_im, log_dt, b_re_t, b_im_t)
    b_blk = _interleave(_block_diag(bbr), _block_diag(bbi), sw).astype(BF16)
    c_blk = _interleave(_block_diag(c_re), _block_diag(-c_im), sw).T.astype(BF16)
    lam_row = _interleave(lbr.reshape(1, ns), lbi.reshape(1, ns), sw)
    d_row = small["s5_d"].reshape(1, s5w)

    def ln_store(outs, acc, res, bias, g, b):
        r = ALPHA * res + acc + bias
        outs[0][...] = r
        if len(outs) > 1:
            h = _ln(r, g, b)
            outs[1][...] = h
            outs[2][...] = h.astype(BF16)

    zero_bias = jnp.zeros((1, d), F32)

    def mix_ln(name, a, w, k_total, tk, res, bias, g, b, a_fn=None, emit_h=True):
        dtypes = (F32, F32, BF16) if emit_h else (F32,)
        return _mm_act(name, a, w, "nat", n_out_cols=d, k_total=k_total, tn=d, tk=tk, a_fn=a_fn,
                       extras=(res, bias, g, b), extra_specs=(_row_spec(tm, d), _vec_spec(d), _vec_spec(d), _vec_spec(d)),
                       store=ln_store, out_shape=[jax.ShapeDtypeStruct((rows, d), t) for t in dtypes],
                       out_specs=[_row_spec(tm, d)] * len(dtypes))

    def two(width):
        return [jax.ShapeDtypeStruct((rows, width), F32)] * 2, [_row_spec(tm, width)] * 2

    def shard_tile(total, shard, cap=1024):
        t = max(shard, cap - cap % shard)
        while total % t:
            t -= shard
        return t

    h0b = h0.astype(BF16)
    proj_ab = _mm_act("in_ab", h0b, wts["in_ab"], "stk", n_out_cols=ab_cols, k_total=d, tn=shard_tile(ab_cols, n_ab), tk=d)[0]
    bu = _mm_act("s5_bu", proj_ab, b_blk, "nat", n_out_cols=2 * ns, k_total=s5w, tn=min(2 * ns, 2048), tk=s5w)[0]
    states = _s5_scan("s5_scan", bu, lam_row, n_batch, lp, sw)

    def gelu_store(outs, acc, u, dv):
        ypre = acc + dv * u
        outs[0][...] = ypre
        outs[1][...] = jax.nn.gelu(ypre)

    shp2, spec2 = two(s5w)
    ypre, y = _mm_act(
        "s5_y", states, c_blk, "nat", n_out_cols=s5w, k_total=2 * ns, tn=s5w, tk=min(2 * ns, 1024),
        extras=(proj_ab, d_row), extra_specs=(_row_spec(tm, s5w), _vec_spec(s5w)), store=gelu_store,
        out_shape=shp2, out_specs=spec2)

    def glu_store(outs, acc, yv, bias):
        gate = acc + bias
        outs[0][...] = gate
        outs[1][...] = _glu(yv, gate)

    gate, a_out = _mm_act(
        "s5_glu", y, wts["glu"], "nat", n_out_cols=s5w, k_total=s5w, tn=s5w, tk=s5w,
        extras=(y, small["s5_b_glu"]), extra_specs=(_row_spec(tm, s5w), _vec_spec(s5w)), store=glu_store,
        out_shape=shp2, out_specs=spec2)
    b_out = _attn_fwd("sb_attn", proj_ab, n_batch, lp, pad, s5_cb, s5_cb + sb_cb, s5_cb + 2 * sb_cb, n_pairs)

    def bias_store(outs, acc, bias):
        outs[0][...] = (acc + bias).astype(outs[0].dtype)

    def wide(width, dtype):
        return [jax.ShapeDtypeStruct((rows, dff), dtype)], [_row_spec(tm, width)]

    def mlp_fwd(layer, h_in, h_in_b, emit_h=True):
        tn = shard_tile(dff, n_up)
        shp, spec = wide(tn, BF16)
        up = _mm_act(f"up{layer}", h_in_b, wts["up"][layer], "stk", n_out_cols=dff, k_total=d, tn=tn, tk=d,
                     extras=(small["mlp_b_up"][layer:layer + 1],), extra_specs=(_vec_spec(tn),), store=bias_store,
                     out_shape=shp, out_specs=spec)[0]
        return (up, *mix_ln(f"down{layer}", up, wts["down"][layer], dff, min(dff, 1024), h_in,
                            small["mlp_b_down"][layer:layer + 1], small["ln_mlp_g"][layer:layer + 1],
                            small["ln_mlp_b"][layer:layer + 1], a_fn=_relu2, emit_h=emit_h))

    r1, h1, h1b = mix_ln("out_ab", [a_out, b_out], wts["out_ab"], s5w + sbw, min(s5w, sbw), h0, zero_bias,
                         small["ln_mix_g"][0:1], small["ln_mix_b"][0:1])
    wts = {**wts, **late_weights(r1)}
    n_c = wts["in_c"].shape[2]
    n_up = wts["up"][0].shape[2]
    up0, r2, h2, h2b = mlp_fwd(0, h1, h1b)

    lb = _lower_bound("hg_lb", small["hgrn_gamma"])
    proj_c = _mm_act("in_c", h2b, wts["in_c"], "stk", n_out_cols=4 * d, k_total=d, tn=shard_tile(4 * d, n_c), tk=d)[0]
    c_out, hg_states = _hgrn_fwd("hgrn", proj_c, lb, wts["ng"], n_batch, lp, pad, n_hg)
    r3, h3, h3b = mix_ln("out_c", c_out, wts["out_c"], d, d, h2, zero_bias, small["ln_mix_g"][1:2], small["ln_mix_b"][1:2])
    up1, r4 = mlp_fwd(1, h3, h3b, emit_h=False)

    gr = {}
    g_r4, gr["ln_mlp_g1"], gr["ln_mlp_b1"], loss_tile, g_r4b = _loss_grad(
        "loss", r4, small["ln_mlp_g"][1:2], small["ln_mlp_b"][1:2], target, n_batch, lp, lead)

    def res_store(outs, acc, g_res):
        outs[0][...] = acc + ALPHA * g_res

    def ln_bwd_store(outs, acc, g_res, r_in, g, b, first_step):
        gr_in, gg, gb = jax.vjp(_ln, r_in, g, b)[1](acc + ALPHA * g_res)
        outs[0][...] = gr_in
        outs[3][...] = gr_in.astype(BF16)

        @pl.when(first_step)
        def _():
            outs[1][...] = jnp.zeros_like(outs[1])
            outs[2][...] = jnp.zeros_like(outs[2])

        outs[1][...] += gg
        outs[2][...] += gb

    def through_ln(name, a, w, k_total, tk, g_res, r_in, g, b, dep=None):
        vec = pl.BlockSpec((1, d), lambda i, j, k: (0, 0))
        return _mm_act(name, a, w, "stkT", n_out_cols=d, k_total=k_total, tn=d, tk=tk,
                       extras=(g_res, r_in, g, b), extra_specs=(_row_spec(tm, d), _row_spec(tm, d), vec, vec),
                       store=ln_bwd_store, sequential=True, dep=dep,
                       out_shape=[jax.ShapeDtypeStruct((rows, d), F32)] + [jax.ShapeDtypeStruct((1, d), F32)] * 2
                       + [jax.ShapeDtypeStruct((rows, d), BF16)],
                       out_specs=[_row_spec(tm, d), vec, vec, _row_spec(tm, d)])

    def mlp_bwd(layer, g_r, g_rb, up, h_in, r_in, send=None):
        def gup_store(outs, acc, upv):
            outs[0][...] = (acc * (2.0 * jnp.maximum(upv.astype(F32), 0.0))).astype(outs[0].dtype)

        tf = min(dff, 1024)
        shp, spec = wide(tf, BF16)
        g_up = _mm_act(f"g_up{layer}", g_rb, wts["down"][layer], "natT", n_out_cols=dff, k_total=d, tn=tf, tk=d,
                       extras=(up,), extra_specs=(_row_spec(tm, tf),), store=gup_store, out_shape=shp, out_specs=spec)[0]
        gr[f"down{layer}"], gr[f"mlp_b_down{layer}"] = _mm_wgrad(
            f"dw_down{layer}", up, g_rb, kw=dff, n=d, tmw=tf, tn=d, a_fn=_relu2, out_dtype=BF16, colsum=True)
        gr[f"up{layer}"], gr[f"mlp_b_up{layer}"] = _mm_wgrad(
            f"dw_up{layer}", h_in, g_up, kw=d, n=dff, tmw=d, tn=min(dff, 2048), shard_cols=n_up, out_dtype=BF16, colsum=True)
        dep = send() if send is not None else None
        g_r_in, gr[f"ln_mix_g{layer}"], gr[f"ln_mix_b{layer}"], g_r_in_b = through_ln(
            f"g_hmid{layer}", g_up, wts["up"][layer], dff, shard_tile(dff, n_up), g_r, r_in,
            small["ln_mix_g"][layer:layer + 1], small["ln_mix_b"][layer:layer + 1], dep=dep)
        return g_r_in, g_r_in_b

    g_r3, g_r3b = mlp_bwd(1, g_r4, g_r4b, up1, h3b, r3)
    g_cout = _mm_act("g_cout", g_r3b, wts["out_c"], "natT", n_out_cols=d, k_total=d, tn=d, tk=d)[0]
    gr["out_c"] = _mm_wgrad("dw_out_c", c_out, g_r3b, kw=d, n=d, tmw=d, tn=d, out_dtype=BF16)
    g_pc, g_lb_parts, g_ng_parts = _hgrn_bwd("hgrn_bwd", proj_c, lb, wts["ng"], g_cout, hg_states, n_batch, lp, pad, n_hg)
    gr["hgrn_gamma"], gr["ng"] = _lower_bound_bwd("hg_lb_bwd", small["hgrn_gamma"], g_lb_parts, g_ng_parts)
    gr["in_c"] = _mm_wgrad("dw_in_c", h2b, g_pc, kw=d, n=4 * d, tmw=d, tn=min(4 * d, 2048), shard_cols=n_c, out_dtype=BF16)
    sent1 = on_grads(1, {"down1": gr["down1"], "up1": gr["up1"], "out_c": gr["out_c"], "in_c": gr["in_c"], "ng": gr["ng"]})
    g_r2, gr["ln_mlp_g0"], gr["ln_mlp_b0"], g_r2b = through_ln(
        "g_h2", g_pc, wts["in_c"], 4 * d, shard_tile(4 * d, n_c), g_r3, r2, small["ln_mlp_g"][0:1], small["ln_mlp_b"][0:1],
        dep=sent1)

    g_r1, g_r1b = mlp_bwd(0, g_r2, g_r2b, up0, h1b, r1, send=lambda: on_grads(2, {"down0": gr["down0"], "up0": gr["up0"]}))
    g_cat = _mm_act("g_cat", g_r1b, wts["out_ab"], "natT", n_out_cols=d, k_total=d, tn=d, tk=d)[0]
    gr["out_ab"] = _mm_wgrad("dw_out_ab", [a_out, b_out], g_r1b, kw=s5w + sbw, n=d, tmw=min(s5w, sbw), tn=d, out_dtype=BF16)
    g_q, g_k, g_v = _attn_bwd("sb_attn_bwd", proj_ab, g_cat, n_batch, lp, pad, s5_cb, s5_cb + sb_cb, s5_cb + 2 * sb_cb,
                              s5_cb, n_pairs)

    g_y_direct, g_gate = _rowwise("s5_glu_bwd", lambda ga, yv, gt: jax.vjp(_glu, yv, gt)[1](ga),
                                  [(g_cat, 0, s5w), (y, 0, s5w), (gate, 0, s5w)], 2, s5w)

    def gelu_bwd_store(outs, acc, gyd, yp, u, dv):
        gyp = jax.vjp(jax.nn.gelu, yp)[1](acc + gyd)[0]
        outs[0][...] = gyp
        outs[1][...] = dv * gyp
        outs[2][...] = jnp.sum(gyp * u, axis=0, keepdims=True)

    rs = _row_spec(tm, s5w)
    g_ypre, g_u_direct, gd_parts = _mm_act(
        "s5_g_y", g_gate, wts["glu"], "natT", n_out_cols=s5w, k_total=s5w, tn=s5w, tk=s5w,
        extras=(g_y_direct, ypre, proj_ab, d_row), extra_specs=(rs, rs, rs, _vec_spec(s5w)), store=gelu_bwd_store,
        out_shape=[jax.ShapeDtypeStruct((rows, s5w), F32)] * 2 + [jax.ShapeDtypeStruct((rows // tm, 1, s5w), F32)],
        out_specs=[rs, rs, pl.BlockSpec((None, 1, s5w), lambda i, j, k: (i, 0, j))])
    gr["glu"], gr["s5_b_glu"] = _mm_wgrad("dw_glu", y, g_gate, kw=s5w, n=s5w, tmw=s5w, tn=s5w, out_dtype=BF16, colsum=True)
    g_sd = _mm_act("s5_g_states", g_ypre, c_blk, "natT", n_out_cols=2 * ns, k_total=s5w, tn=min(2 * ns, 2048), tk=s5w)[0]
    d_cblk = _mm_wgrad("dw_cblk", states, g_ypre, kw=2 * ns, n=s5w, tmw=min(2 * ns, 1024), tn=s5w)
    gs, gl_parts = _s5_scan_bwd("s5_scan_bwd", g_sd, states, lam_row, n_batch, lp, sw)

    def add_store(outs, acc, other):
        outs[0][...] = acc + other

    g_u = _mm_act("s5_g_u", gs, b_blk, "natT", n_out_cols=s5w, k_total=2 * ns, tn=s5w, tk=min(2 * ns, 1024),
                  extras=(g_u_direct,), extra_specs=(rs,), store=add_store)[0]
    d_bblk = _mm_wgrad("dw_bblk", proj_ab, gs, kw=s5w, n=2 * ns, tmw=s5w, tn=min(2 * ns, 2048))
    db_re, db_im = _deinterleave(d_bblk, sw)
    dc_re, dc_im = _deinterleave(d_cblk.T, sw)
    glr, gli = _deinterleave(gl_parts, sw)
    g_lam_re, g_lam_im, g_log_dt, g_b_re_t, g_b_im_t, g_d = _s5_params_bwd(
        "s5_params_bwd", lam_re, lam_im, log_dt, b_re_t, b_im_t,
        glr.reshape(n_batch, groups, n_state), gli.reshape(n_batch, groups, n_state),
        _diag_blocks(db_re, groups), _diag_blocks(db_im, groups), gd_parts)

    cat2 = lambda key: jnp.concatenate([gr[key + "0"], gr[key + "1"]], axis=0)
    small_sent = on_small({
        "s5_lam_re": g_lam_re[None], "s5_lam_im": g_lam_im[None], "s5_log_dt": g_log_dt.reshape(1, groups),
        "s5_b_re": g_b_re_t.transpose(0, 2, 1)[None], "s5_b_im": g_b_im_t.transpose(0, 2, 1)[None],
        "s5_c_re": _diag_blocks(dc_re, groups)[None], "s5_c_im": -_diag_blocks(dc_im, groups)[None],
        "s5_d": g_d.reshape(1, groups, grp), "s5_b_glu": gr["s5_b_glu"], "hgrn_gamma": gr["hgrn_gamma"],
        "ln_mix_g": cat2("ln_mix_g"), "ln_mix_b": cat2("ln_mix_b"), "mlp_b_up": cat2("mlp_b_up"),
        "mlp_b_down": cat2("mlp_b_down"), "ln_mlp_g": cat2("ln_mlp_g"), "ln_mlp_b": cat2("ln_mlp_b"),
    }, loss_tile)

    g_pab = [g_u, g_q, g_k, g_v]
    assert s5w == sbw
    gr["in_ab"] = _mm_wgrad("dw_in_ab", h0b, g_pab, kw=d, n=ab_cols, tmw=d, tn=s5w, shard_cols=n_ab, out_dtype=BF16,
                            dep=small_sent)
    g_h0 = _mm_act("g_h0", g_pab, wts["in_ab"], "stkT", n_out_cols=d, k_total=ab_cols, tn=d, tk=shard_tile(s5w, n_ab),
                   extras=(g_r1,), extra_specs=(_row_spec(tm, d),), store=res_store)[0]
    grad_x = g_h0.reshape(n_batch, lp, d)[:, lead:, :]
    g_meta = _meta_grad("g_meta", g_h0, n_batch, lp, pad, n_meta)
    on_grads(3, {"meta": g_meta, "in_ab": gr["in_ab"], "glu": gr["glu"], "out_ab": gr["out_ab"]})
    return grad_x


SMALL_NAMES = ("s5_lam_re", "s5_lam_im", "s5_log_dt", "s5_b_re", "s5_b_im", "s5_c_re", "s5_c_im", "s5_d", "s5_b_glu",
               "hgrn_gamma", "ln_mix_g", "ln_mix_b", "mlp_b_up", "mlp_b_down", "ln_mlp_g", "ln_mlp_b")
WEIGHT_ORDER = ("meta", "w_in_ab", "s5_lam_re", "s5_lam_im", "s5_log_dt", "s5_b_re", "s5_b_im", "s5_c_re", "s5_c_im",
                "s5_d", "s5_w_glu", "s5_b_glu", "w_out_ab", "w_in_c", "hgrn_gamma", "hgrn_norm_g", "w_out_c", "ln_mix_g",
                "ln_mix_b", "mlp_w_up", "mlp_b_up", "mlp_w_down", "mlp_b_down", "ln_mlp_g", "ln_mlp_b")


def kernel(x, meta, w_in_ab, s5_lam_re, s5_lam_im, s5_log_dt, s5_b_re, s5_b_im, s5_c_re, s5_c_im, s5_d, s5_w_glu, s5_b_glu, w_out_ab, w_in_c, hgrn_gamma, hgrn_norm_g, w_out_c, ln_mix_g, ln_mix_b, mlp_w_up, mlp_b_up, mlp_w_down, mlp_b_down, ln_mlp_g, ln_mlp_b, loss_target, m_meta, m_w_in_ab, m_s5_lam_re, m_s5_lam_im, m_s5_log_dt, m_s5_b_re, m_s5_b_im, m_s5_c_re, m_s5_c_im, m_s5_d, m_s5_w_glu, m_s5_b_glu, m_w_out_ab, m_w_in_c, m_hgrn_gamma, m_hgrn_norm_g, m_w_out_c, m_ln_mix_g, m_ln_mix_b, m_mlp_w_up, m_mlp_b_up, m_mlp_w_down, m_mlp_b_down, m_ln_mlp_g, m_ln_mlp_b, v_meta, v_w_in_ab, v_s5_lam_re, v_s5_lam_im, v_s5_log_dt, v_s5_b_re, v_s5_b_im, v_s5_c_re, v_s5_c_im, v_s5_d, v_s5_w_glu, v_s5_b_glu, v_w_out_ab, v_w_in_c, v_hgrn_gamma, v_hgrn_norm_g, v_w_out_c, v_ln_mix_g, v_ln_mix_b, v_mlp_w_up, v_mlp_b_up, v_mlp_w_down, v_mlp_b_down, v_ln_mlp_g, v_ln_mlp_b):
    args = dict(locals())
    w = {n: args[n] for n in WEIGHT_ORDER}
    mom = {n: args["m_" + n] for n in WEIGHT_ORDER}
    var = {n: args["v_" + n] for n in WEIGHT_ORDER}
    d = x.shape[2]
    n_meta = meta.shape[0]

    cast = lambda a: a.astype(BF16)
    early = _exchange_start("gather_early_start", [w["meta"], cast(w["w_in_ab"][0]), cast(w["s5_w_glu"][0]),
                                                   cast(w["w_out_ab"][0])], False)
    late = _exchange_start("gather_late_start", [w["hgrn_norm_g"], cast(w["w_in_c"][0]), cast(w["w_out_c"][0]),
                                                 cast(w["mlp_w_up"][0]), cast(w["mlp_w_up"][1]),
                                                 cast(w["mlp_w_down"][0]), cast(w["mlp_w_down"][1])], False, dep=early["token"])
    a_meta, a_in_ab, a_glu, a_out_ab = _exchange_wait("gather_early_wait", early, late["token"])
    wts = {"in_ab": a_in_ab, "glu": a_glu.reshape(-1, a_glu.shape[2]), "out_ab": a_out_ab.reshape(-1, d)}
    meta_full = a_meta.transpose(1, 0, 2).reshape(n_meta, d)
    small = {n: w[n] for n in SMALL_NAMES}

    def late_weights(after):
        a_ng, a_in_c, a_out_c, a_up0, a_up1, a_dn0, a_dn1 = _exchange_wait("gather_late_wait", late, after)
        return {"in_c": a_in_c, "ng": a_ng.transpose(1, 0, 2).reshape(1, d), "out_c": a_out_c.reshape(-1, d),
                "up": [a_up0, a_up1], "down": [a_dn0.reshape(-1, d), a_dn1.reshape(-1, d)]}

    n_loc = d // N_DEV
    rows_of = lambda g: g.reshape(N_DEV, -1, g.shape[-1])
    cols_of = lambda g: g.reshape(g.shape[0], N_DEV, n_loc).transpose(1, 0, 2)
    sent = {}

    def on_grads(stage, g):
        if stage == 1:
            order = (("mlp_w_down", 1), ("mlp_w_up", 1), ("w_out_c", 0), ("w_in_c", 0), ("hgrn_norm_g", None))
            parts = [rows_of(g["down1"]), g["up1"], rows_of(g["out_c"]), g["in_c"], cols_of(g["ng"])]
        elif stage == 2:
            order = (("mlp_w_down", 0), ("mlp_w_up", 0))
            parts = [rows_of(g["down0"]), g["up0"]]
        else:
            order = (("w_out_ab", 0), ("s5_w_glu", 0), ("w_in_ab", 0), ("meta", None))
            parts = [rows_of(g["out_ab"]), rows_of(g["glu"]), g["in_ab"], cols_of(g["meta"])]
        sent[stage] = (order, _exchange_start(f"scatter_start{stage}", parts, True))
        return sent[stage][1]["token"]

    def on_small(sg, loss_tile):
        g_pack = _pack_rows([sg[n] for n in SMALL_NAMES] + [loss_tile], PACK_COLS)
        sent["small"] = _exchange_start("gather_small_start", [g_pack], False)
        return sent["small"]["token"]

    grad_x = _local_step(x, loss_target, meta_full, wts, small, late_weights, on_grads, on_small)
    small_sent = sent["small"]
    tile = (SUBLANES, LANES)
    shapes = [w[n].shape for n in SMALL_NAMES] + [tile]
    zeros = jnp.zeros(tile, F32)
    w_pack = _pack_rows([w[n] for n in SMALL_NAMES] + [zeros], PACK_COLS)
    m_pack = _pack_rows([mom[n] for n in SMALL_NAMES] + [zeros], PACK_COLS)
    v_pack = _pack_rows([var[n] for n in SMALL_NAMES] + [zeros], PACK_COLS)

    received, res = {}, {}

    def wait(stage, after):
        order, handle = sent[stage]
        for key, rc in zip(order, _exchange_wait(f"scatter_wait{stage}", handle, after)):
            received[key] = rc

    def update(nm):
        layered = w[nm].ndim == 3
        parts = [received[(nm, l)] for l in range(w[nm].shape[0])] if layered else received[(nm, None)]
        res[nm] = _adamw_summed(f"adamw_{nm}", parts, w[nm], mom[nm], var[nm])
        return res[nm][0]

    wait(1, sent[3][1]["token"])
    done = [update(nm) for nm in ("w_out_c", "w_in_c", "hgrn_norm_g")]
    wait(2, done[0])
    done = [update(nm) for nm in ("mlp_w_up", "mlp_w_down")]
    g_all = _exchange_wait("gather_small_wait", small_sent, done[0])[0]
    packed = _adamw_summed("adamw_small", g_all, w_pack, m_pack, v_pack)
    wait(3, packed[0])
    for nm in ("w_out_ab", "s5_w_glu", "w_in_ab", "meta"):
        update(nm)
    unpacked = [_unpack_rows(p, shapes, PACK_COLS) for p in packed]
    loss = unpacked[0][-1][0, 0]

    def pick(nm, which):
        return unpacked[which][SMALL_NAMES.index(nm)] if nm in SMALL_NAMES else res[nm][which]

    return (loss, grad_x, *[pick(n, 0) for n in WEIGHT_ORDER], *[pick(n, 1) for n in WEIGHT_ORDER],
            *[pick(n, 2) for n in WEIGHT_ORDER], *[pick(n, 3) for n in WEIGHT_ORDER])
```

```python
import functools
import math

import jax
import jax.numpy as jnp
from jax import lax
from jax.experimental import pallas as pl
from jax.experimental.pallas import tpu as pltpu

F32 = jnp.float32
BF16 = jnp.bfloat16

N_DEV = 8
DEPTH = 2
ALPHA = (2.0 * DEPTH) ** 0.25
LN_EPS = 1e-5
RMS_EPS = 1e-6
SB_HEAD_DIM = 64
HG_DK = 128
HG_CHUNK = 64
LANES = 128
SUBLANES = 8
PACKED_ROWS = 16
VMEM_LIMIT_BYTES = 56 * 1024 * 1024
ROW_TILE = 1088
SCAN_LANES = 256
SCAN_UNROLL = 2
PACK_COLS = 1024

ADAM_LR = 0.001
ADAM_B1 = 0.9
ADAM_B2 = 0.999
ADAM_EPS = 1e-08
ADAM_WD = 0.01
ADAM_STEP = 10

NN = (((1,), (0,)), ((), ()))
NT = (((1,), (1,)), ((), ()))
TN = (((0,), (0,)), ((), ()))


def _tile(n, pref, align=SUBLANES):
    t = min(n, pref)
    t -= t % align
    while t >= align:
        if n % t == 0:
            return t
        t -= align
    return n


def _unrolled_loop(n, body, init, unroll):
    assert n % unroll == 0

    def outer(t, carry):
        for u in range(unroll):
            carry = body(t * unroll + u, carry)
        return carry

    return lax.fori_loop(0, n // unroll, outer, init)


def _params(sem):
    return pltpu.CompilerParams(dimension_semantics=sem, vmem_limit_bytes=VMEM_LIMIT_BYTES)


def _dot_raw(a, b, dims):
    return lax.dot_general(a.astype(BF16), b.astype(BF16), dims, preferred_element_type=F32)


def _make_dot(dims, da_rule, db_rule):
    @jax.custom_vjp
    def f(a, b):
        return _dot_raw(a, b, dims)

    def fwd(a, b):
        return _dot_raw(a, b, dims), (a, b)

    def bwd(res, g):
        a, b = res
        return da_rule(g, a, b), db_rule(g, a, b)

    f.defvjp(fwd, bwd)
    return f


_DOTS = {
    NN: _make_dot(NN, lambda g, a, b: _dot_raw(g, b, NT), lambda g, a, b: _dot_raw(a, g, TN)),
    NT: _make_dot(NT, lambda g, a, b: _dot_raw(g, b, NN), lambda g, a, b: _dot_raw(g, a, TN)),
    TN: _make_dot(TN, lambda g, a, b: _dot_raw(b, g, NT), lambda g, a, b: _dot_raw(a, g, NN)),
}


def _dot(a, b, dims):
    return _DOTS[dims](a, b)


def _running_sums(a, tri_ones, split=False):
    hi = a.astype(BF16)
    out = lax.dot_general(hi, tri_ones, NN, preferred_element_type=F32)
    if split:
        lo = (a - hi.astype(F32)).astype(BF16)
        out = out + lax.dot_general(lo, tri_ones, NN, preferred_element_type=F32)
    return out


def _piece_specs(pieces, block_rows, block_cols, row_of, col_of, cb0):
    per = pieces[0].shape[1] // block_cols if len(pieces) > 1 else None
    specs = []
    for p in range(len(pieces)):
        if per is None:
            specs.append(pl.BlockSpec((block_rows, block_cols), lambda *g: (row_of(*g), cb0 + col_of(*g))))
        else:
            specs.append(pl.BlockSpec(
                (block_rows, block_cols),
                lambda *g, p=p: (row_of(*g), jnp.clip(col_of(*g) - p * per, 0, per - 1))))
    return specs, per


def _mm_call(name, grid, dims, a_pieces, a_specs, a_sel, b_pieces, b_specs, b_sel, extras, extra_specs,
             out_shape, out_specs, acc_shape, a_fn, store, colsum_width=0, sequential=False, deps=()):
    na, nb, ne, no, nd = len(a_pieces), len(b_pieces), len(extras), len(out_shape), len(deps)
    nk = grid[2]

    def body(*refs):
        a_refs, b_refs = refs[:na], refs[na:na + nb]
        extra = refs[na + nb:na + nb + ne]
        outs = refs[na + nb + ne + nd:na + nb + ne + nd + no]
        acc = refs[na + nb + ne + nd + no]
        ids = (pl.program_id(0), pl.program_id(1), pl.program_id(2))
        k = ids[2]

        @pl.when(k == 0)
        def _():
            acc[...] = jnp.zeros_like(acc)

        def run(a_ref, b_ref):
            a = a_ref[...]
            if a_fn is not None:
                a = a_fn(a)
            b = b_ref[...]
            if b.ndim == 3 and dims == NN:
                n = b.shape[2]
                for q in range(b.shape[0]):
                    acc[:, q * n:(q + 1) * n] += _dot_raw(a, b[q], dims)
            elif b.ndim == 3:
                acc[...] += _dot_raw(a, jnp.concatenate([b[q] for q in range(b.shape[0])], axis=1), dims)
            else:
                acc[...] += _dot_raw(a, b, dims)
            if colsum_width:
                cs = refs[-1]
                first = ids[1] == 0

                @pl.when(first & (k == 0))
                def _():
                    cs[...] = jnp.zeros_like(cs)

                @pl.when(first)
                def _():
                    cs[...] += jnp.sum(b.astype(F32), axis=0, keepdims=True)

        if na == 1 and nb == 1:
            run(a_refs[0], b_refs[0])
        elif nb == 1:
            per, fn = a_sel
            which = fn(*ids) // per
            for p in range(na):
                pl.when(which == p)(functools.partial(run, a_refs[p], b_refs[0]))
        else:
            assert na == 1
            per, fn = b_sel
            which = fn(*ids) // per
            for p in range(nb):
                pl.when(which == p)(functools.partial(run, a_refs[0], b_refs[p]))

        @pl.when(k == nk - 1)
        def _():
            if sequential:
                store(outs, acc[...], *[e[...] for e in extra], first_step=(ids[0] == 0) & (ids[1] == 0))
            else:
                store(outs, acc[...], *[e[...] for e in extra])
            if colsum_width:
                @pl.when(ids[1] == 0)
                def _():
                    outs[-1][...] = refs[-1][...]

    scratch = [pltpu.VMEM(acc_shape, F32)]
    if colsum_width:
        scratch.append(pltpu.VMEM((1, colsum_width), F32))
    sem = ("parallel", "arbitrary", "arbitrary") if colsum_width else ("parallel", "parallel", "arbitrary")
    if sequential:
        sem = ("arbitrary",) * 3
    return pl.pallas_call(
        body, name=name, grid=grid,
        in_specs=[*a_specs, *b_specs, *extra_specs, *[pl.BlockSpec(memory_space=pl.ANY)] * nd], out_specs=out_specs,
        out_shape=out_shape, scratch_shapes=scratch, compiler_params=_params(sem),
    )(*a_pieces, *b_pieces, *extras, *deps)


def _store_plain(outs, acc):
    outs[0][...] = acc.astype(outs[0].dtype)


def _row_spec(tm, tn):
    return pl.BlockSpec((tm, tn), lambda i, j, k: (i, j))


def _vec_spec(tn):
    return pl.BlockSpec((1, tn), lambda i, j, k: (0, j))


def _mm_act(name, a, w, wkind, *, n_out_cols, k_total, tn, tk, a_cb0=0, a_fn=None, extras=(), extra_specs=(),
            store=_store_plain, out_shape=None, out_specs=None, sequential=False, dep=None):
    a_pieces = list(a) if isinstance(a, (list, tuple)) else [a]
    rows = a_pieces[0].shape[0]
    tm = _tile(rows, ROW_TILE)
    grid = (rows // tm, n_out_cols // tn, k_total // tk)
    a_specs, per = _piece_specs(a_pieces, tm, tk, lambda i, j, k: i, lambda i, j, k: k, a_cb0)
    if wkind == "nat":
        b_spec, dims = pl.BlockSpec((tk, tn), lambda i, j, k: (k, j)), NN
    elif wkind == "stk":
        n = w.shape[2]
        assert tn % n == 0
        b_spec, dims = pl.BlockSpec((tn // n, tk, n), lambda i, j, k: (j, k, 0)), NN
    elif wkind == "natT":
        b_spec, dims = pl.BlockSpec((tn, tk), lambda i, j, k: (j, k)), NT
    else:
        n = w.shape[2]
        assert wkind == "stkT" and tk % n == 0
        b_spec, dims = pl.BlockSpec((tk // n, tn, n), lambda i, j, k: (k, j, 0)), NT
    if out_shape is None:
        out_shape = [jax.ShapeDtypeStruct((rows, n_out_cols), F32)]
        out_specs = [_row_spec(tm, tn)]
    return _mm_call(name, grid, dims, a_pieces, a_specs, (per, lambda i, j, k: k), [w], [b_spec], None,
                    list(extras), list(extra_specs), out_shape, out_specs, (tm, tn), a_fn, store,
                    sequential=sequential, deps=() if dep is None else (dep,))


def _mm_wgrad(name, a, g, *, kw, n, tmw, tn, a_cb0=0, a_fn=None, shard_cols=0, out_dtype=F32, colsum=False, dep=None):
    a_pieces = list(a) if isinstance(a, (list, tuple)) else [a]
    g_pieces = list(g) if isinstance(g, (list, tuple)) else [g]
    rows = a_pieces[0].shape[0]
    tr = _tile(rows, ROW_TILE)
    grid = (n // tn, kw // tmw, rows // tr)
    a_specs, a_per = _piece_specs(a_pieces, tr, tmw, lambda j, i, k: k, lambda j, i, k: i, a_cb0)
    g_specs, g_per = _piece_specs(g_pieces, tr, tn, lambda j, i, k: k, lambda j, i, k: j, 0)
    if shard_cols:
        per = tn // shard_cols
        out_shape = [jax.ShapeDtypeStruct((n // shard_cols, kw, shard_cols), out_dtype)]
        out_specs = [pl.BlockSpec((per, tmw, shard_cols), lambda j, i, k: (j, i, 0))]

        def store(outs, acc):
            for q in range(per):
                outs[0][q] = acc[:, q * shard_cols:(q + 1) * shard_cols].astype(out_dtype)
    else:
        out_shape = [jax.ShapeDtypeStruct((kw, n), out_dtype)]
        out_specs = [pl.BlockSpec((tmw, tn), lambda j, i, k: (i, j))]

        def store(outs, acc):
            outs[0][...] = acc.astype(out_dtype)
    if colsum:
        out_shape.append(jax.ShapeDtypeStruct((1, n), F32))
        out_specs.append(pl.BlockSpec((1, tn), lambda j, i, k: (0, j)))
    res = _mm_call(name, grid, TN, a_pieces, a_specs, (a_per, lambda j, i, k: i), g_pieces, g_specs,
                   (g_per, lambda j, i, k: j), [], [], out_shape, out_specs, (tmw, tn), a_fn, store,
                   colsum_width=tn if colsum else 0, deps=() if dep is None else (dep,))
    return res if colsum else res[0]


def _ln(x, g, b):
    mu = jnp.mean(x, axis=-1, keepdims=True)
    xc = x - mu
    var = jnp.mean(xc * xc, axis=-1, keepdims=True)
    return xc * lax.rsqrt(var + LN_EPS) * g + b


def _relu2(x):
    r = jnp.maximum(x.astype(F32), 0.0)
    return r * r


def _glu(y, gate):
    return y * jax.nn.sigmoid(gate)


def _rowwise(name, fn, ins, n_out, width):
    rows = ins[0][0].shape[0]
    tm = _tile(rows, ROW_TILE)

    def body(*refs):
        res = fn(*[r[...] for r in refs[:len(ins)]])
        for o, v in zip(refs[len(ins):], res):
            o[...] = v

    return pl.pallas_call(
        body, name=name, grid=(rows // tm,),
        in_specs=[pl.BlockSpec((tm, wd), lambda i, cb=cb: (i, cb)) for _, cb, wd in ins],
        out_specs=[pl.BlockSpec((tm, width), lambda i: (i, 0))] * n_out,
        out_shape=[jax.ShapeDtypeStruct((rows, width), F32)] * n_out, compiler_params=_params(("parallel",)),
    )(*[a for a, _, _ in ins])


def _loss_grad(name, r, g, b, target, n_batch, lp, lead):
    rows, d = r.shape
    nq = lp // LANES
    lead_blocks = lead // LANES

    def body(r_ref, g_ref, b_ref, t_ref, gr_ref, gg_ref, gb_ref, loss_ref, grb_ref):
        i = pl.program_id(1)

        @pl.when((pl.program_id(0) == 0) & (i == 0))
        def _():
            loss_ref[...] = jnp.zeros_like(loss_ref)
            gg_ref[...] = jnp.zeros_like(gg_ref)
            gb_ref[...] = jnp.zeros_like(gb_ref)

        h, vjp = jax.vjp(_ln, r_ref[...], g_ref[...], b_ref[...])
        diff = jnp.where(i >= lead_blocks, h - t_ref[...], 0.0)
        gr, gg, gb = vjp(diff * (1.0 / d))
        gr_ref[...] = gr
        grb_ref[...] = gr.astype(BF16)
        gg_ref[...] += gg
        gb_ref[...] += gb
        loss_ref[...] += 0.5 * jnp.sum(diff * diff) * (1.0 / d)

    vec = pl.BlockSpec((1, d), lambda b, i: (0, 0))
    row = pl.BlockSpec((LANES, d), lambda b, i: (b * nq + i, 0))
    return pl.pallas_call(
        body, name=name, grid=(n_batch, nq),
        in_specs=[row, vec, vec, pl.BlockSpec((None, LANES, d), lambda b, i: (b, jnp.maximum(i - lead_blocks, 0), 0))],
        out_specs=[row, vec, vec, pl.BlockSpec((SUBLANES, LANES), lambda b, i: (0, 0)), row],
        out_shape=[jax.ShapeDtypeStruct((rows, d), F32), jax.ShapeDtypeStruct((1, d), F32),
                   jax.ShapeDtypeStruct((1, d), F32), jax.ShapeDtypeStruct((SUBLANES, LANES), F32),
                   jax.ShapeDtypeStruct((rows, d), BF16)],
        compiler_params=_params(("arbitrary", "arbitrary")),
    )(r, g, b, target)


def _meta_grad(name, g_h0, n_batch, lp, pad, n_meta):
    d = g_h0.shape[1]
    per = lp // n_meta
    at = pad // n_meta

    def body(g_ref, o_ref):
        @pl.when(pl.program_id(0) == 0)
        def _():
            o_ref[...] = jnp.zeros_like(o_ref)

        o_ref[...] += g_ref[...]

    return pl.pallas_call(
        body, name=name, grid=(n_batch,),
        in_specs=[pl.BlockSpec((n_meta, d), lambda b: (b * per + at, 0))],
        out_specs=pl.BlockSpec((n_meta, d), lambda b: (0, 0)),
        out_shape=jax.ShapeDtypeStruct((n_meta, d), F32),
        compiler_params=_params(("arbitrary",)),
    )(g_h0)


def _s5_param_fn(lr, li, ldt, br, bi):
    dt = jnp.exp(ldt)
    e = jnp.exp(lr * dt)
    w = li * dt
    lbr = e * jnp.cos(w)
    lbi = e * jnp.sin(w)
    nr = lbr - 1.0
    den = lr * lr + li * li
    cr = (nr * lr + lbi * li) / den
    ci = (lbi * lr - nr * li) / den
    bbr = cr[:, None, :] * br - ci[:, None, :] * bi
    bbi = cr[:, None, :] * bi + ci[:, None, :] * br
    return lbr, lbi, bbr, bbi


def _s5_params(name, lr, li, ldt, br, bi):
    def body(lr_ref, li_ref, ldt_ref, br_ref, bi_ref, o1, o2, o3, o4):
        res = _s5_param_fn(lr_ref[...], li_ref[...], ldt_ref[...], br_ref[...], bi_ref[...])
        for o, v in zip((o1, o2, o3, o4), res):
            o[...] = v

    shp = [jax.ShapeDtypeStruct(lr.shape, F32)] * 2 + [jax.ShapeDtypeStruct(br.shape, F32)] * 2
    return pl.pallas_call(body, name=name, out_shape=shp)(lr, li, ldt, br, bi)


def _s5_params_bwd(name, lr, li, ldt, br, bi, g_lbr, g_lbi, g_bbr, g_bbi, gd_parts):
    def body(lr_ref, li_ref, ldt_ref, br_ref, bi_ref, g1, g2, g3, g4, gd_ref, o1, o2, o3, o4, o5, o6):
        _, vjp = jax.vjp(_s5_param_fn, lr_ref[...], li_ref[...], ldt_ref[...], br_ref[...], bi_ref[...])
        res = vjp((jnp.sum(g1[...], axis=0), jnp.sum(g2[...], axis=0), g3[...], g4[...]))
        for o, v in zip((o1, o2, o3, o4, o5), res):
            o[...] = v
        o6[...] = jnp.sum(gd_ref[...], axis=0)

    shp = ([jax.ShapeDtypeStruct(lr.shape, F32)] * 2 + [jax.ShapeDtypeStruct(ldt.shape, F32)]
           + [jax.ShapeDtypeStruct(br.shape, F32)] * 2 + [jax.ShapeDtypeStruct(gd_parts.shape[1:], F32)])
    return pl.pallas_call(body, name=name, out_shape=shp)(lr, li, ldt, br, bi, g_lbr, g_lbi, g_bbr, g_bbi, gd_parts)


def _interleave(re, im, w):
    nj = re.shape[-1] // w
    return jnp.concatenate([x[..., j * w:(j + 1) * w] for j in range(nj) for x in (re, im)], axis=-1)


def _deinterleave(x, w):
    nj = x.shape[-1] // (2 * w)
    return (jnp.concatenate([x[..., 2 * j * w:(2 * j + 1) * w] for j in range(nj)], axis=-1),
            jnp.concatenate([x[..., (2 * j + 1) * w:(2 * j + 2) * w] for j in range(nj)], axis=-1))


def _cmul(ar, ai, br, bi):
    return ar * br - ai * bi, ar * bi + ai * br


def _powers(lr, li):
    p = [(lr, li)]
    p.append(_cmul(*p[0], *p[0]))
    p.append(_cmul(*p[1], *p[0]))
    p.append(_cmul(*p[1], *p[1]))
    p.append(_cmul(*p[3], *p[0]))
    p.append(_cmul(*p[3], *p[1]))
    p.append(_cmul(*p[3], *p[2]))
    p.append(_cmul(*p[3], *p[3]))
    return p


def _scan_steps(pw, shifts, keep):
    return [(sh, jnp.where(m, pw[s - 1][0], 0.0), jnp.where(m, pw[s - 1][1], 0.0))
            for s, sh, m in zip((1, 2, 4), shifts, keep)]


def _scan_tile(xr, xi, steps):
    for sh, br, bi in steps:
        rr = pltpu.roll(xr, sh, 0)
        ri = pltpu.roll(xi, sh, 0)
        xr, xi = xr + (br * rr - bi * ri), xi + (br * ri + bi * rr)
    return xr, xi


def _s5_scan(name, bu, lam, n_batch, lp, w):
    rows, two_ns = bu.shape
    nj = two_ns // (2 * w)

    def body(x_ref, lam_ref, s_ref):
        pw = _powers(lam_ref[:, :w], lam_ref[:, w:])
        tab_r = jnp.concatenate([p[0] for p in pw], axis=0)
        tab_i = jnp.concatenate([p[1] for p in pw], axis=0)
        row = lax.broadcasted_iota(jnp.int32, (SUBLANES, w), 0)
        steps = _scan_steps(pw, (1, 2, 4), [row >= s for s in (1, 2, 4)])

        def packed_tile(t, carry):
            cr, ci = carry
            r0 = pl.multiple_of(t * PACKED_ROWS, PACKED_ROWS)
            x = x_ref[pl.ds(r0, PACKED_ROWS), :].astype(F32)
            done = []
            for half in range(PACKED_ROWS // SUBLANES):
                xt = x[half * SUBLANES:(half + 1) * SUBLANES, :]
                xr, xi = _scan_tile(xt[:, :w], xt[:, w:], steps)
                sr = xr + (tab_r * cr - tab_i * ci)
                si = xi + (tab_r * ci + tab_i * cr)
                done.append(jnp.concatenate([sr, si], axis=1))
                cr, ci = sr[SUBLANES - 1:, :], si[SUBLANES - 1:, :]
            s_ref[pl.ds(r0, PACKED_ROWS), :] = jnp.concatenate(done, axis=0).astype(s_ref.dtype)
            return cr, ci

        zero = jnp.zeros((1, w), F32)
        _unrolled_loop(lp // PACKED_ROWS, packed_tile, (zero, zero), SCAN_UNROLL)

    spec = pl.BlockSpec((lp, 2 * w), lambda b, j: (b, j))
    return pl.pallas_call(
        body, name=name, grid=(n_batch, nj), in_specs=[spec, pl.BlockSpec((1, 2 * w), lambda b, j: (0, j))],
        out_specs=spec, out_shape=jax.ShapeDtypeStruct((rows, two_ns), BF16),
        compiler_params=_params(("parallel", "parallel")),
    )(bu, lam)


def _s5_scan_bwd(name, gd, states, lam, n_batch, lp, w):
    rows, two_ns = gd.shape
    nj = two_ns // (2 * w)

    def body(x_ref, s_ref, lam_ref, g_ref, gl_ref):
        pw = _powers(lam_ref[:, :w], -lam_ref[:, w:])
        tab_r = jnp.concatenate([p[0] for p in reversed(pw)], axis=0)
        tab_i = jnp.concatenate([p[1] for p in reversed(pw)], axis=0)
        row = lax.broadcasted_iota(jnp.int32, (SUBLANES, w), 0)
        steps = _scan_steps(pw, [SUBLANES - s for s in (1, 2, 4)], [row < SUBLANES - s for s in (1, 2, 4)])

        n_packed = lp // PACKED_ROWS
        halves = PACKED_ROWS // SUBLANES

        def packed_tile(u, carry):
            cr, ci, ar, ai = carry
            t = n_packed - 1 - u
            r0 = pl.multiple_of(t * PACKED_ROWS, PACKED_ROWS)
            x = x_ref[pl.ds(r0, PACKED_ROWS), :].astype(F32)
            cur = s_ref[pl.ds(r0, PACKED_ROWS), :].astype(F32)
            p0 = pl.multiple_of(jnp.maximum(t - 1, 0) * PACKED_ROWS, PACKED_ROWS)
            before = s_ref[pl.ds(p0, PACKED_ROWS), :].astype(F32)[PACKED_ROWS - 1:, :] * jnp.where(t > 0, 1.0, 0.0)
            done = [None] * halves
            for half in reversed(range(halves)):
                rows_h = slice(half * SUBLANES, (half + 1) * SUBLANES)
                xt, st = x[rows_h, :], cur[rows_h, :]
                xr, xi = _scan_tile(xt[:, :w], xt[:, w:], steps)
                gr = xr + (tab_r * cr - tab_i * ci)
                gi = xi + (tab_r * ci + tab_i * cr)
                done[half] = jnp.concatenate([gr, gi], axis=1)
                prev = before if half == 0 else cur[half * SUBLANES - 1:half * SUBLANES, :]
                spr = jnp.where(row >= 1, pltpu.roll(st[:, :w], 1, 0), prev[:, :w])
                spi = jnp.where(row >= 1, pltpu.roll(st[:, w:], 1, 0), prev[:, w:])
                cr, ci, ar, ai = gr[:1, :], gi[:1, :], ar + gr * spr + gi * spi, ai + gi * spr - gr * spi
            g_ref[pl.ds(r0, PACKED_ROWS), :] = jnp.concatenate(done, axis=0).astype(g_ref.dtype)
            return cr, ci, ar, ai

        z1 = jnp.zeros((1, w), F32)
        z8 = jnp.zeros((SUBLANES, w), F32)
        _, _, ar, ai = _unrolled_loop(n_packed, packed_tile, (z1, z1, z8, z8), SCAN_UNROLL)
        gl_ref[...] = jnp.concatenate([jnp.sum(ar, axis=0, keepdims=True), jnp.sum(ai, axis=0, keepdims=True)], axis=1)

    spec = pl.BlockSpec((lp, 2 * w), lambda b, j: (b, j))
    return pl.pallas_call(
        body, name=name, grid=(n_batch, nj),
        in_specs=[spec, spec, pl.BlockSpec((1, 2 * w), lambda b, j: (0, j))],
        out_specs=[spec, pl.BlockSpec((None, 1, 2 * w), lambda b, j: (b, 0, j))],
        out_shape=[jax.ShapeDtypeStruct((rows, two_ns), BF16), jax.ShapeDtypeStruct((n_batch, 1, two_ns), F32)],
        compiler_params=_params(("parallel", "parallel")),
    )(gd, states, lam)


def _log_sigmoid(z):
    return jnp.minimum(z, 0.0) - jnp.log(1.0 + jnp.exp(-jnp.abs(z)))


ATTN_KEYS = 256
ATTN_GROUP = 4


def _attn_block(i, jb, lp, pad):
    start = jb * ATTN_KEYS
    r0 = pl.multiple_of(jnp.minimum(start, lp - ATTN_KEYS), LANES)
    rowpos = i * LANES + lax.broadcasted_iota(jnp.int32, (LANES, ATTN_KEYS), 0)
    keypos = r0 + lax.broadcasted_iota(jnp.int32, (LANES, ATTN_KEYS), 1)
    return r0, (keypos < rowpos) & (keypos >= jnp.maximum(start, pad))


def _tri_ones(strict_upper):
    r = lax.broadcasted_iota(jnp.int32, (ATTN_KEYS, ATTN_KEYS + LANES), 0)
    c = lax.broadcasted_iota(jnp.int32, (ATTN_KEYS, ATTN_KEYS + LANES), 1)
    tri = (r > c) if strict_upper else (r < c)
    return jnp.where((c >= ATTN_KEYS) | tri, 1.0, 0.0).astype(BF16)


def _split_sums(cr):
    rs = cr[:, ATTN_KEYS:]
    return cr[:, :ATTN_KEYS], jnp.concatenate([rs] * (ATTN_KEYS // LANES), axis=1)


def _head_masks():
    lane = lax.broadcasted_iota(jnp.int32, (1, LANES), 1)
    return [lane < SB_HEAD_DIM, lane >= SB_HEAD_DIM]


def _run_groups(n, first, sign, make):
    j, left, g = first, n, ATTN_GROUP
    while g >= 1:
        shift = g.bit_length() - 1
        count = lax.shift_right_logical(left, shift)
        fn = make(g)

        def loop(_, jcur, fn=fn, g=g):
            fn(jcur)
            return jcur + sign * g

        j = lax.fori_loop(0, count, loop, j)
        left = left - lax.shift_left(count, shift)
        g //= 2


def _attn_fwd(name, proj, n_batch, lp, pad, q_cb, k_cb, v_cb, n_pairs):
    rows = proj.shape[0]
    nq = lp // LANES
    scale = SB_HEAD_DIM ** -0.5

    def body(q_ref, k_ref, v_ref, o_ref, acc_s):
        i = pl.program_id(1)
        hm = _head_masks()
        comb = _tri_ones(True)
        n_blocks = lax.shift_right_logical(i + ATTN_KEYS // LANES, (ATTN_KEYS // LANES).bit_length() - 1)

        def pair(hp, carry):
            lanes = pl.ds(pl.multiple_of(hp * LANES, LANES), LANES)
            qs = q_ref[:, lanes] * scale
            qh = [jnp.where(m, qs, 0.0).astype(BF16) for m in hm]
            acc_s[...] = jnp.zeros_like(acc_s)
            o_ref[:, lanes] = jnp.zeros((LANES, LANES), F32)

            def make(group):
                def fn(jtop):
                    chains = []
                    for g in range(group):
                        r0, vis = _attn_block(i, jtop - g, lp, pad)
                        kj = k_ref[pl.ds(r0, ATTN_KEYS), lanes].astype(BF16)
                        vj = v_ref[pl.ds(r0, ATTN_KEYS), lanes]
                        for h in range(2):
                            z = lax.dot_general(qh[h], kj, NT, preferred_element_type=F32)
                            chains.append((h, vis, z, jnp.where(hm[h], vj, 0.0).astype(BF16)))
                    staged = []
                    for h, vis, z, vh in chains:
                        lsz = _log_sigmoid(z)
                        staged.append((h, vis, lsz, _running_sums(jnp.where(vis, lsz - z, 0.0), comb, split=True), vh))
                    out = o_ref[:, lanes]
                    for h, vis, lsz, cr, vh in staged:
                        later, rs = _split_sums(cr)
                        acc = acc_s[h]
                        wgt = jnp.where(vis, jnp.exp(lsz + later + acc), 0.0)
                        acc_s[h] = acc + rs
                        out = out + lax.dot_general(wgt.astype(BF16), vh, NN, preferred_element_type=F32)
                    o_ref[:, lanes] = out
                return fn

            _run_groups(n_blocks, n_blocks - 1, -1, make)
            return carry

        lax.fori_loop(0, n_pairs, pair, 0)

    wide = n_pairs * LANES
    assert q_cb % n_pairs == 0 and k_cb % n_pairs == 0 and v_cb % n_pairs == 0
    return pl.pallas_call(
        body, name=name, grid=(n_batch, nq),
        in_specs=[pl.BlockSpec((LANES, wide), lambda b, i: (b * nq + i, q_cb // n_pairs)),
                  pl.BlockSpec((lp, wide), lambda b, i: (b, k_cb // n_pairs)),
                  pl.BlockSpec((lp, wide), lambda b, i: (b, v_cb // n_pairs))],
        out_specs=pl.BlockSpec((LANES, wide), lambda b, i: (b * nq + i, 0)),
        out_shape=jax.ShapeDtypeStruct((rows, wide), F32),
        scratch_shapes=[pltpu.VMEM((2, LANES, ATTN_KEYS), F32)],
        compiler_params=_params(("parallel", "arbitrary")),
    )(proj, proj, proj)


def _attn_bwd(name, proj, g_out, n_batch, lp, pad, q_cb, k_cb, v_cb, go_cb, n_pairs):
    rows = proj.shape[0]
    nq = lp // LANES
    scale = SB_HEAD_DIM ** -0.5

    def body(q_ref, k_ref, v_ref, go_ref, gq_ref, gk_ref, gv_ref, ga_s, sz_s, acc_s):
        i = pl.program_id(1)

        @pl.when(i == 0)
        def _():
            gk_ref[...] = jnp.zeros_like(gk_ref)
            gv_ref[...] = jnp.zeros_like(gv_ref)

        hm = _head_masks()
        comb_up = _tri_ones(True)
        comb_lo = _tri_ones(False)
        n_blocks = lax.shift_right_logical(i + ATTN_KEYS // LANES, (ATTN_KEYS // LANES).bit_length() - 1)

        def pair(hp, carry):
            lanes = pl.ds(pl.multiple_of(hp * LANES, LANES), LANES)
            qs = q_ref[:, lanes] * scale
            go = go_ref[:, lanes]
            qh = [jnp.where(m, qs, 0.0).astype(BF16) for m in hm]
            goh = [jnp.where(m, go, 0.0).astype(BF16) for m in hm]
            acc_s[...] = jnp.zeros_like(acc_s)

            def make_down(group):
                def fn(jtop):
                    chains = []
                    for g in range(group):
                        j = jtop - g
                        r0, vis = _attn_block(i, j, lp, pad)
                        kj = k_ref[pl.ds(r0, ATTN_KEYS), lanes].astype(BF16)
                        vj = v_ref[pl.ds(r0, ATTN_KEYS), lanes].astype(BF16)
                        for h in range(2):
                            z = lax.dot_general(qh[h], kj, NT, preferred_element_type=F32)
                            gw = lax.dot_general(goh[h], vj, NT, preferred_element_type=F32)
                            chains.append((h, j, r0, vis, z, gw))
                    staged = []
                    for h, j, r0, vis, z, gw in chains:
                        lsz = _log_sigmoid(z)
                        staged.append((h, j, r0, vis, lsz, _running_sums(jnp.where(vis, lsz - z, 0.0), comb_up), gw))
                    for h, j, r0, vis, lsz, cr, gw in staged:
                        later, rs = _split_sums(cr)
                        acc = acc_s[h]
                        wgt = jnp.where(vis, jnp.exp(lsz + later + acc), 0.0)
                        acc_s[h] = acc + rs
                        ga_s[h, j] = gw * wgt
                        sz_s[h, j] = jnp.exp(lsz)
                        gv_ref[pl.ds(r0, ATTN_KEYS), lanes] += lax.dot_general(
                            wgt.astype(BF16), goh[h], TN, preferred_element_type=F32)
                return fn

            _run_groups(n_blocks, n_blocks - 1, -1, make_down)
            acc_s[...] = jnp.zeros_like(acc_s)

            def make_up(group):
                def fn(jbot):
                    pend = []
                    for g in range(group):
                        j = jbot + g
                        r0, vis = _attn_block(i, j, lp, pad)
                        kj = k_ref[pl.ds(r0, ATTN_KEYS), lanes]
                        for h in range(2):
                            ga = ga_s[h, j]
                            pend.append((h, j, r0, vis, ga, _running_sums(ga, comb_lo),
                                         jnp.where(hm[h], kj, 0.0).astype(BF16)))
                    gq = jnp.zeros((LANES, LANES), F32)
                    for h, j, r0, vis, ga, cr, kh in pend:
                        before, rs = _split_sums(cr)
                        pre = acc_s[h]
                        glk = before + pre
                        acc_s[h] = pre + rs
                        sz = sz_s[h, j]
                        gz = jnp.where(vis, ga * (1.0 - sz) - glk * sz, 0.0).astype(BF16)
                        gq = gq + lax.dot_general(gz, kh, NN, preferred_element_type=F32)
                        gk_ref[pl.ds(r0, ATTN_KEYS), lanes] += lax.dot_general(gz, qh[h], TN, preferred_element_type=F32)
                    gq_ref[:, lanes] += gq * scale
                return fn

            gq_ref[:, lanes] = jnp.zeros((LANES, LANES), F32)
            _run_groups(n_blocks, 0, 1, make_up)
            return carry

        lax.fori_loop(0, n_pairs, pair, 0)

    wide = n_pairs * LANES
    assert q_cb % n_pairs == 0 and k_cb % n_pairs == 0 and v_cb % n_pairs == 0 and go_cb % n_pairs == 0
    blk = lambda cb: pl.BlockSpec((LANES, wide), lambda b, i: (b * nq + i, cb // n_pairs))
    full = lambda cb: pl.BlockSpec((lp, wide), lambda b, i: (b, cb // n_pairs))
    shp = jax.ShapeDtypeStruct((rows, wide), F32)
    per_block = pltpu.VMEM((2, -(-lp // ATTN_KEYS), LANES, ATTN_KEYS), F32)
    return pl.pallas_call(
        body, name=name, grid=(n_batch, nq),
        in_specs=[blk(q_cb), full(k_cb), full(v_cb), blk(go_cb)],
        out_specs=[blk(0), full(0), full(0)], out_shape=[shp, shp, shp],
        scratch_shapes=[per_block, per_block, pltpu.VMEM((2, LANES, ATTN_KEYS), F32)],
        compiler_params=_params(("parallel", "arbitrary")),
    )(proj, proj, proj, g_out)


def _lb_fn(gamma):
    g0, g1 = gamma[0:1, :], gamma[1:2, :]
    mx = jnp.maximum(g0, g1)
    e0, e1 = jnp.exp(g0 - mx), jnp.exp(g1 - mx)
    p0, p1 = e0 / (e0 + e1), e1 / (e0 + e1)
    return (p0 + p1) - p0


def _lower_bound(name, gamma):
    def body(g_ref, o_ref):
        o_ref[...] = _lb_fn(g_ref[...])

    return pl.pallas_call(body, name=name, out_shape=jax.ShapeDtypeStruct((1, gamma.shape[1]), F32))(gamma)


def _lower_bound_bwd(name, gamma, g_lb_parts, g_ng_parts):
    def body(g_ref, glb_ref, gng_ref, o_ref, o2_ref):
        _, vjp = jax.vjp(_lb_fn, g_ref[...])
        o_ref[...] = vjp(jnp.sum(glb_ref[...], axis=0))[0]
        o2_ref[...] = jnp.sum(gng_ref[...], axis=0)

    return pl.pallas_call(
        body, name=name,
        out_shape=[jax.ShapeDtypeStruct(gamma.shape, F32), jax.ShapeDtypeStruct((1, gamma.shape[1]), F32)],
    )(gamma, g_lb_parts, g_ng_parts)


def _tri_times(tril, x, dims):
    hi = x.astype(BF16)
    lo = (x - hi.astype(F32)).astype(BF16)
    t = tril.astype(BF16)
    return (lax.dot_general(t, hi, dims, preferred_element_type=F32)
            + lax.dot_general(t, lo, dims, preferred_element_type=F32))


@jax.custom_vjp
def _cumsum_rows(x, tril):
    return _tri_times(tril, x, NN)


def _cumsum_rows_fwd(x, tril):
    return _tri_times(tril, x, NN), tril


def _cumsum_rows_bwd(tril, g):
    return _tri_times(tril, g, TN), jnp.zeros_like(tril)


_cumsum_rows.defvjp(_cumsum_rows_fwd, _cumsum_rows_bwd)


def _hg_decays(f_pre, lbs, masks, tril):
    f = [[lb + (1.0 - lb) * jax.nn.sigmoid(fc) for fc, lb in zip(row, lbs)] for row in f_pre]
    bcum = [[_cumsum_rows(jnp.log(x) * m, tril) for x in row] for row, m in zip(f, masks)]
    return [[1.0 - x for x in row] for row in f], bcum


def _hg_step(q, f_pre, i_in, g, lbs, ngs, sts, masks, tril):
    k, bcum = _hg_decays(f_pre, lbs, masks, tril)
    v = [[ic * m for ic in row] for row, m in zip(i_in, masks)]
    qd = [[qc * jnp.exp(b) for qc, b in zip(qr, br)] for qr, br in zip(q, bcum)]
    scores = [[jnp.where(tril > 0.5, _dot(a, kk * jnp.exp(-b), NT), 0.0) for a, kk, b in zip(ar, kr, br)]
              for ar, kr, br in zip(qd, k, bcum)]
    inner = [[_dot(s, x, NN) for s, x in zip(sr, vr)] for sr, vr in zip(scores, v)]
    add = [[_dot(x, kk * jnp.exp(b[HG_CHUNK - 1:, :] - b), TN) for x, kk, b in zip(vr, kr, br)]
           for vr, kr, br in zip(v, k, bcum)]
    outs = []
    for qr, br, nr, ar, gr in zip(qd, bcum, inner, add, g):
        o = [n + _dot(a, st, NT) for n, a, st in zip(nr, qr, sts)]
        sts = [jnp.exp(b[HG_CHUNK - 1:, :]) * st + a for b, a, st in zip(br, ar, sts)]
        o = [x * lax.rsqrt(jnp.mean(x * x, axis=-1, keepdims=True) + RMS_EPS) * ng for x, ng in zip(o, ngs)]
        outs.append([x * (gc * jax.nn.sigmoid(gc)) for x, gc in zip(o, gr)])
    return outs, sts


def _hg_consts(c, pad):
    r = lax.broadcasted_iota(jnp.int32, (HG_CHUNK, HG_CHUNK), 0)
    cc = lax.broadcasted_iota(jnp.int32, (HG_CHUNK, HG_CHUNK), 1)
    tril = jnp.where(r >= cc, 1.0, 0.0).astype(F32)
    pos = c * HG_CHUNK + lax.broadcasted_iota(jnp.int32, (HG_CHUNK, 1), 0)
    return tril, jnp.where(pos >= pad, 1.0, 0.0).astype(F32)


HG_HEADS_PER_STEP = 8
HG_CHUNKS_PER_STEP = 2


def _hg_layout(lp, n_heads):
    step_rows = HG_CHUNKS_PER_STEP * HG_CHUNK
    per = min(HG_HEADS_PER_STEP, n_heads)
    assert lp % step_rows == 0 and n_heads % per == 0
    heads = [(h, slice(h * HG_DK, (h + 1) * HG_DK)) for h in range(per)]
    return n_heads // per, lp // step_rows, step_rows, per * HG_DK, heads


def _hg_step_views(step, pad, heads):
    slices = [slice(u * HG_CHUNK, (u + 1) * HG_CHUNK) for u in range(HG_CHUNKS_PER_STEP)]
    consts = [_hg_consts(step * HG_CHUNKS_PER_STEP + u, pad) for u in range(HG_CHUNKS_PER_STEP)]
    load = lambda ref: [[ref[sl, cols] for _, cols in heads] for sl in slices]
    return slices, [m for _, m in consts], consts[0][0], load


def _hgrn_fwd(name, proj, lb, ng, n_batch, lp, pad, n_heads):
    rows = proj.shape[0]
    groups, steps, step_rows, wide, heads = _hg_layout(lp, n_heads)

    def body(q_ref, f_ref, i_ref, g_ref, lb_ref, ng_ref, o_ref, s_ref, st_s):
        t = pl.program_id(2)

        @pl.when(t == 0)
        def _():
            st_s[...] = jnp.zeros_like(st_s)

        slices, masks, tril, load = _hg_step_views(t, pad, heads)
        sts = [st_s[h] for h, _ in heads]
        for (_, cols), st in zip(heads, sts):
            s_ref[:, cols] = st
        outs, sts = _hg_step(load(q_ref), load(f_ref), load(i_ref), load(g_ref),
                             [lb_ref[:, cols] for _, cols in heads], [ng_ref[:, cols] for _, cols in heads],
                             sts, masks, tril)
        for sl, row in zip(slices, outs):
            for (_, cols), o in zip(heads, row):
                o_ref[sl, cols] = o
        for (h, _), st in zip(heads, sts):
            st_s[h] = st

    col = lambda off: pl.BlockSpec((step_rows, wide), lambda b, h, t: (b * steps + t, off * groups + h))
    vec = pl.BlockSpec((1, wide), lambda b, h, t: (0, h))
    return pl.pallas_call(
        body, name=name, grid=(n_batch, groups, steps), in_specs=[col(0), col(1), col(2), col(3), vec, vec],
        out_specs=[col(0), pl.BlockSpec((HG_DK, wide), lambda b, h, t: (b * steps + t, h))],
        out_shape=[jax.ShapeDtypeStruct((rows, n_heads * HG_DK), F32),
                   jax.ShapeDtypeStruct((n_batch * steps * HG_DK, n_heads * HG_DK), F32)],
        scratch_shapes=[pltpu.VMEM((len(heads), HG_DK, HG_DK), F32)],
        compiler_params=_params(("parallel", "parallel", "arbitrary")),
    )(proj, proj, proj, proj, lb, ng)


def _hgrn_bwd(name, proj, lb, ng, g_out, states, n_batch, lp, pad, n_heads):
    rows = proj.shape[0]
    width = n_heads * HG_DK
    groups, steps, step_rows, wide, heads = _hg_layout(lp, n_heads)
    assert groups == 1

    def body(q_ref, f_ref, i_ref, g_ref, lb_ref, ng_ref, go_ref, s_ref, gp_ref, glb_ref, gng_ref, gst_s):
        t = pl.program_id(2)

        @pl.when(t == 0)
        def _():
            gst_s[...] = jnp.zeros_like(gst_s)
            glb_ref[...] = jnp.zeros_like(glb_ref)
            gng_ref[...] = jnp.zeros_like(gng_ref)

        slices, masks, tril, load = _hg_step_views(steps - 1 - t, pad, heads)
        fn = functools.partial(_hg_step, masks=masks, tril=tril)
        _, vjp = jax.vjp(fn, load(q_ref), load(f_ref), load(i_ref), load(g_ref),
                         [lb_ref[:, cols] for _, cols in heads], [ng_ref[:, cols] for _, cols in heads],
                         [s_ref[:, cols] for _, cols in heads])
        gq, gf, gi, gg, glb, gng, gst = vjp((load(go_ref), [gst_s[h] for h, _ in heads]))
        for part, grads in enumerate((gq, gf, gi, gg)):
            for sl, row in zip(slices, grads):
                for (h, _), x in zip(heads, row):
                    lane0 = part * width + h * HG_DK
                    gp_ref[sl, lane0:lane0 + HG_DK] = x.astype(BF16)
        for (h, cols), a, b, c in zip(heads, gst, glb, gng):
            gst_s[h] = a
            glb_ref[:, cols] += b
            gng_ref[:, cols] += c

    col = lambda off: pl.BlockSpec((step_rows, wide), lambda b, h, t: (b * steps + steps - 1 - t, off * groups + h))
    vec = pl.BlockSpec((1, wide), lambda b, h, t: (0, h))
    part = pl.BlockSpec((None, 1, wide), lambda b, h, t: (b, 0, h))
    big = jax.ShapeDtypeStruct((rows, 4 * width), BF16)
    small = jax.ShapeDtypeStruct((n_batch, 1, width), F32)
    return pl.pallas_call(
        body, name=name, grid=(n_batch, groups, steps),
        in_specs=[col(0), col(1), col(2), col(3), vec, vec, col(0),
                  pl.BlockSpec((HG_DK, wide), lambda b, h, t: (b * steps + steps - 1 - t, h))],
        out_specs=[pl.BlockSpec((step_rows, 4 * width), lambda b, h, t: (b * steps + steps - 1 - t, 0)), part, part],
        out_shape=[big, small, small],
        scratch_shapes=[pltpu.VMEM((len(heads), HG_DK, HG_DK), F32)],
        compiler_params=_params(("parallel", "parallel", "arbitrary")),
    )(proj, proj, proj, proj, lb, ng, g_out, states)


def _exchange_copies(src, dst, send, recv, loc, scatter):
    x, y, c = lax.axis_index("x"), lax.axis_index("y"), lax.axis_index("c")
    me = 4 * x + 2 * y + c
    local, remote = [], []
    for w in range(len(src)):
        local.append(pltpu.make_async_copy(src[w].at[me] if scatter else src[w], dst[w].at[me], loc.at[w]))
    for k in range(1, N_DEV):
        px = 1 - x if k & 4 else x
        py = 1 - y if k & 2 else y
        pc = 1 - c if k & 1 else c
        peer = 4 * px + 2 * py + pc
        for w in range(len(src)):
            remote.append(pltpu.make_async_remote_copy(
                src_ref=src[w].at[peer] if scatter else src[w], dst_ref=dst[w].at[me],
                send_sem=send.at[w * (N_DEV - 1) + k - 1], recv_sem=recv.at[w * (N_DEV - 1) + k - 1],
                device_id=(px, py, pc), device_id_type=pl.DeviceIdType.MESH))
    return local, remote


_HBM_SPEC = pl.BlockSpec(memory_space=pltpu.HBM)
_SEM_SPEC = pl.BlockSpec(memory_space=pltpu.SEMAPHORE)
_ANY_SPEC = pl.BlockSpec(memory_space=pl.ANY)
_DATAFLOW = pltpu.SideEffectType.DATAFLOW_SIDE_EFFECTING


def _exchange_start(name, srcs, scatter, dep=None):
    nw = len(srcs)
    srcs = [pltpu.with_memory_space_constraint(s, pltpu.HBM) for s in srcs]
    lands = [pltpu.with_memory_space_constraint(lax.empty(s.shape if scatter else (N_DEV,) + s.shape, s.dtype), pltpu.HBM)
             for s in srcs]
    deps = [] if dep is None else [dep]

    def body(*refs):
        src, dst = refs[:nw], refs[nw:2 * nw]
        send, recv, loc = refs[2 * nw + len(deps):2 * nw + len(deps) + 3]
        token = refs[-1]
        local, remote = _exchange_copies(src, dst, send, recv, loc, scatter)
        for cp in local + remote:
            cp.start()
        token[...] = jnp.zeros_like(token)

    sems = [pltpu.SemaphoreType.DMA((nw * (N_DEV - 1),)), pltpu.SemaphoreType.DMA((nw * (N_DEV - 1),)),
            pltpu.SemaphoreType.DMA((nw,))]
    out = pl.pallas_call(
        body, name=name,
        out_shape=(*sems, *[pltpu.HBM(s.shape, s.dtype) for s in srcs], *[pltpu.HBM(s.shape, s.dtype) for s in lands],
                   jax.ShapeDtypeStruct((SUBLANES, LANES), F32)),
        in_specs=[_HBM_SPEC] * (2 * nw) + [_ANY_SPEC] * len(deps),
        out_specs=(_SEM_SPEC, _SEM_SPEC, _SEM_SPEC, *[_HBM_SPEC] * (2 * nw), pl.BlockSpec(memory_space=pltpu.VMEM)),
        input_output_aliases={i: 3 + i for i in range(2 * nw)},
        compiler_params=pltpu.CompilerParams(has_side_effects=_DATAFLOW),
    )(*srcs, *lands, *deps)
    return {"sems": out[:3], "srcs": out[3:3 + nw], "lands": out[3 + nw:3 + 2 * nw], "token": out[-1], "scatter": scatter}


def _exchange_wait(name, handle, after):
    nw = len(handle["srcs"])
    scatter = handle["scatter"]

    def body(*refs):
        src, dst = refs[:nw], refs[nw:2 * nw]
        send, recv, loc = refs[2 * nw:2 * nw + 3]
        local, remote = _exchange_copies(src, dst, send, recv, loc, scatter)
        for cp in local:
            cp.wait()
        for cp in remote:
            cp.wait_send()
            cp.wait_recv()

    out = pl.pallas_call(
        body, name=name,
        out_shape=(*[pltpu.HBM(s.shape, s.dtype) for s in handle["srcs"]],
                   *[pltpu.HBM(s.shape, s.dtype) for s in handle["lands"]]),
        in_specs=[_HBM_SPEC] * (2 * nw) + [_SEM_SPEC] * 3 + [_ANY_SPEC],
        out_specs=tuple([_HBM_SPEC] * (2 * nw)),
        input_output_aliases={i: i for i in range(2 * nw)},
        compiler_params=pltpu.CompilerParams(has_side_effects=_DATAFLOW),
    )(*handle["srcs"], *handle["lands"], *handle["sems"], after)
    return list(out[nw:])


def _adamw(w, g, m, v):
    m = ADAM_B1 * m + (1.0 - ADAM_B1) * g
    v = ADAM_B2 * v + (1.0 - ADAM_B2) * (g * g)
    m_hat = m / (1.0 - ADAM_B1 ** ADAM_STEP)
    v_hat = v / (1.0 - ADAM_B2 ** ADAM_STEP)
    delta = -ADAM_LR * (m_hat / (jnp.sqrt(v_hat) + ADAM_EPS) + ADAM_WD * w)
    return delta, m, v


def _adamw_summed(name, parts, w, m, v):
    layered = w.ndim == 3
    parts = list(parts) if layered else [parts]
    n_layers = len(parts)
    rows, cols = w.shape[-2:]
    n_parts = parts[0].shape[0]
    tr = _tile(rows, max(SUBLANES, (1 << 18) // cols))

    def body(*refs):
        p_refs = refs[:n_layers]
        w_ref, m_ref, v_ref, g_ref, d_ref, nm_ref, nv_ref = refs[n_layers:]
        layer = pl.program_id(0)

        def run(p_ref):
            g = p_ref[0].astype(F32)
            for s in range(1, n_parts):
                g = g + p_ref[s].astype(F32)
            d, nm, nv = _adamw(w_ref[...], g, m_ref[...], v_ref[...])
            g_ref[...] = g
            d_ref[...] = d
            nm_ref[...] = nm
            nv_ref[...] = nv

        for l in range(n_layers):
            pl.when(layer == l)(functools.partial(run, p_refs[l]))

    if layered:
        spec = pl.BlockSpec((None, tr, cols), lambda l, i: (l, i, 0))
    else:
        spec = pl.BlockSpec((tr, cols), lambda l, i: (i, 0))
    p_specs = [pl.BlockSpec((n_parts, tr, cols), lambda l, i, q=q: (0, jnp.where(l == q, i, 0), 0))
               for q in range(n_layers)]
    shp = jax.ShapeDtypeStruct(w.shape, F32)
    return pl.pallas_call(
        body, name=name, grid=(n_layers, rows // tr), in_specs=[*p_specs, spec, spec, spec],
        out_specs=[spec] * 4, out_shape=[shp] * 4, compiler_params=_params(("parallel", "parallel")),
    )(*parts, w, m, v)


def _pack_rows(arrays, cols):
    out = []
    for a in arrays:
        flat = a.reshape(-1)
        n = -(-flat.shape[0] // cols) * cols
        out.append(jnp.pad(flat, (0, n - flat.shape[0])).reshape(-1, cols))
    packed = jnp.concatenate(out, axis=0)
    return jnp.pad(packed, ((0, -packed.shape[0] % SUBLANES), (0, 0)))


def _unpack_rows(packed, shapes, cols):
    out, r = [], 0
    for s in shapes:
        n = math.prod(s)
        nr = -(-n // cols)
        out.append(packed[r:r + nr].reshape(-1)[:n].reshape(s))
        r += nr
    return out


def _block_diag(blocks):
    g, a, b = blocks.shape
    eye = jnp.eye(g, dtype=blocks.dtype)
    return (eye[:, None, :, None] * blocks[:, :, None, :]).reshape(g * a, g * b)


def _diag_blocks(dense, g):
    a, b = dense.shape[0] // g, dense.shape[1] // g
    return jnp.einsum("gagb->gab", dense.reshape(g, a, g, b))


def _local_step(x, target, meta, wts, small, late_weights, on_grads, on_small):
    n_batch, seq, d = x.shape
    n_meta = meta.shape[0]
    pad = -(seq + n_meta) % LANES
    lead = pad + n_meta
    lp = lead + seq
    rows = n_batch * lp
    s5w = wts["glu"].shape[0]
    n_ab = wts["in_ab"].shape[2]
    ab_cols = wts["in_ab"].shape[0] * n_ab
    sbw = (ab_cols - s5w) // 3
    dff = small["mlp_b_up"].shape[1]
    n_pairs = sbw // LANES
    n_hg = d // HG_DK
    s5_cb = s5w // LANES
    sb_cb = sbw // LANES
    tm = _tile(rows, ROW_TILE)
    groups, n_state, grp = small["s5_b_re"].shape[1:]
    ns = groups * n_state
    sw = min(SCAN_LANES, ns)

    h0 = jnp.concatenate(
        [jnp.zeros((n_batch, pad, d), F32), jnp.broadcast_to(meta[None], (n_batch, n_meta, d)), x], axis=1
    ).reshape(rows, d)

    lam_re, lam_im = small["s5_lam_re"][0], small["s5_lam_im"][0]
    log_dt = small["s5_log_dt"][0][:, None]
    b_re_t = small["s5_b_re"][0].transpose(0, 2, 1)
    b_im_t = small["s5_b_im"][0].transpose(0, 2, 1)
    c_re, c_im = small["s5_c_re"][0], small["s5_c_im"][0]
    lbr, lbi, bbr, bbi = _s5_params("s5_params", lam_re, lam_im, log_dt, b_re_t, b_im_t)
    b_blk = _interleave(_block_diag(bbr), _block_diag(bbi), sw).astype(BF16)
    c_blk = _interleave(_block_diag(c_re), _block_diag(-c_im), sw).T.astype(BF16)
    lam_row = _interleave(lbr.reshape(1, ns), lbi.reshape(1, ns), sw)
    d_row = small["s5_d"].reshape(1, s5w)

    def ln_store(outs, acc, res, bias, g, b):
        r = ALPHA * res + acc + bias
        outs[0][...] = r
        if len(outs) > 1:
            h = _ln(r, g, b)
            outs[1][...] = h
            outs[2][...] = h.astype(BF16)

    zero_bias = jnp.zeros((1, d), F32)

    def mix_ln(name, a, w, k_total, tk, res, bias, g, b, a_fn=None, emit_h=True):
        dtypes = (F32, F32, BF16) if emit_h else (F32,)
        return _mm_act(name, a, w, "nat", n_out_cols=d, k_total=k_total, tn=d, tk=tk, a_fn=a_fn,
                       extras=(res, bias, g, b), extra_specs=(_row_spec(tm, d), _vec_spec(d), _vec_spec(d), _vec_spec(d)),
                       store=ln_store, out_shape=[jax.ShapeDtypeStruct((rows, d), t) for t in dtypes],
                       out_specs=[_row_spec(tm, d)] * len(dtypes))

    def two(width):
        return [jax.ShapeDtypeStruct((rows, width), F32)] * 2, [_row_spec(tm, width)] * 2

    def shard_tile(total, shard, cap=1024):
        t = max(shard, cap - cap % shard)
        while total % t:
            t -= shard
        return t

    h0b = h0.astype(BF16)
    proj_ab = _mm_act("in_ab", h0b, wts["in_ab"], "stk", n_out_cols=ab_cols, k_total=d, tn=shard_tile(ab_cols, n_ab), tk=d)[0]
    bu = _mm_act("s5_bu", proj_ab, b_blk, "nat", n_out_cols=2 * ns, k_total=s5w, tn=min(2 * ns, 2048), tk=s5w)[0]
    states = _s5_scan("s5_scan", bu, lam_row, n_batch, lp, sw)

    def gelu_store(outs, acc, u, dv):
        ypre = acc + dv * u
        outs[0][...] = ypre
        outs[1][...] = jax.nn.gelu(ypre)

    shp2, spec2 = two(s5w)
    ypre, y = _mm_act(
        "s5_y", states, c_blk, "nat", n_out_cols=s5w, k_total=2 * ns, tn=s5w, tk=min(2 * ns, 1024),
        extras=(proj_ab, d_row), extra_specs=(_row_spec(tm, s5w), _vec_spec(s5w)), store=gelu_store,
        out_shape=shp2, out_specs=spec2)

    def glu_store(outs, acc, yv, bias):
        gate = acc + bias
        outs[0][...] = gate
        outs[1][...] = _glu(yv, gate)

    gate, a_out = _mm_act(
        "s5_glu", y, wts["glu"], "nat", n_out_cols=s5w, k_total=s5w, tn=s5w, tk=s5w,
        extras=(y, small["s5_b_glu"]), extra_specs=(_row_spec(tm, s5w), _vec_spec(s5w)), store=glu_store,
        out_shape=shp2, out_specs=spec2)
    b_out = _attn_fwd("sb_attn", proj_ab, n_batch, lp, pad, s5_cb, s5_cb + sb_cb, s5_cb + 2 * sb_cb, n_pairs)

    def bias_store(outs, acc, bias):
        outs[0][...] = (acc + bias).astype(outs[0].dtype)

    def wide(width, dtype):
        return [jax.ShapeDtypeStruct((rows, dff), dtype)], [_row_spec(tm, width)]

    def mlp_fwd(layer, h_in, h_in_b, emit_h=True):
        tn = shard_tile(dff, n_up)
        shp, spec = wide(tn, BF16)
        up = _mm_act(f"up{layer}", h_in_b, wts["up"][layer], "stk", n_out_cols=dff, k_total=d, tn=tn, tk=d,
                     extras=(small["mlp_b_up"][layer:layer + 1],), extra_specs=(_vec_spec(tn),), store=bias_store,
                     out_shape=shp, out_specs=spec)[0]
        return (up, *mix_ln(f"down{layer}", up, wts["down"][layer], dff, min(dff, 1024), h_in,
                            small["mlp_b_down"][layer:layer + 1], small["ln_mlp_g"][layer:layer + 1],
                            small["ln_mlp_b"][layer:layer + 1], a_fn=_relu2, emit_h=emit_h))

    r1, h1, h1b = mix_ln("out_ab", [a_out, b_out], wts["out_ab"], s5w + sbw, min(s5w, sbw), h0, zero_bias,
                         small["ln_mix_g"][0:1], small["ln_mix_b"][0:1])
    wts = {**wts, **late_weights(r1)}
    n_c = wts["in_c"].shape[2]
    n_up = wts["up"][0].shape[2]
    up0, r2, h2, h2b = mlp_fwd(0, h1, h1b)

    lb = _lower_bound("hg_lb", small["hgrn_gamma"])
    proj_c = _mm_act("in_c", h2b, wts["in_c"], "stk", n_out_cols=4 * d, k_total=d, tn=shard_tile(4 * d, n_c), tk=d)[0]
    c_out, hg_states = _hgrn_fwd("hgrn", proj_c, lb, wts["ng"], n_batch, lp, pad, n_hg)
    r3, h3, h3b = mix_ln("out_c", c_out, wts["out_c"], d, d, h2, zero_bias, small["ln_mix_g"][1:2], small["ln_mix_b"][1:2])
    up1, r4 = mlp_fwd(1, h3, h3b, emit_h=False)

    gr = {}
    g_r4, gr["ln_mlp_g1"], gr["ln_mlp_b1"], loss_tile, g_r4b = _loss_grad(
        "loss", r4, small["ln_mlp_g"][1:2], small["ln_mlp_b"][1:2], target, n_batch, lp, lead)

    def res_store(outs, acc, g_res):
        outs[0][...] = acc + ALPHA * g_res

    def ln_bwd_store(outs, acc, g_res, r_in, g, b, first_step):
        gr_in, gg, gb = jax.vjp(_ln, r_in, g, b)[1](acc + ALPHA * g_res)
        outs[0][...] = gr_in
        outs[3][...] = gr_in.astype(BF16)

        @pl.when(first_step)
        def _():
            outs[1][...] = jnp.zeros_like(outs[1])
            outs[2][...] = jnp.zeros_like(outs[2])

        outs[1][...] += gg
        outs[2][...] += gb

    def through_ln(name, a, w, k_total, tk, g_res, r_in, g, b, dep=None):
        vec = pl.BlockSpec((1, d), lambda i, j, k: (0, 0))
        return _mm_act(name, a, w, "stkT", n_out_cols=d, k_total=k_total, tn=d, tk=tk,
                       extras=(g_res, r_in, g, b), extra_specs=(_row_spec(tm, d), _row_spec(tm, d), vec, vec),
                       store=ln_bwd_store, sequential=True, dep=dep,
                       out_shape=[jax.ShapeDtypeStruct((rows, d), F32)] + [jax.ShapeDtypeStruct((1, d), F32)] * 2
                       + [jax.ShapeDtypeStruct((rows, d), BF16)],
                       out_specs=[_row_spec(tm, d), vec, vec, _row_spec(tm, d)])

    def mlp_bwd(layer, g_r, g_rb, up, h_in, r_in, send=None):
        def gup_store(outs, acc, upv):
            outs[0][...] = (acc * (2.0 * jnp.maximum(upv.astype(F32), 0.0))).astype(outs[0].dtype)

        tf = min(dff, 1024)
        shp, spec = wide(tf, BF16)
        g_up = _mm_act(f"g_up{layer}", g_rb, wts["down"][layer].T, "nat", n_out_cols=dff, k_total=d, tn=tf, tk=d,
                       extras=(up,), extra_specs=(_row_spec(tm, tf),), store=gup_store, out_shape=shp, out_specs=spec)[0]
        gr[f"down{layer}"], gr[f"mlp_b_down{layer}"] = _mm_wgrad(
            f"dw_down{layer}", up, g_rb, kw=dff, n=d, tmw=tf, tn=d, a_fn=_relu2, out_dtype=BF16, colsum=True)
        gr[f"up{layer}"], gr[f"mlp_b_up{layer}"] = _mm_wgrad(
            f"dw_up{layer}", h_in, g_up, kw=d, n=dff, tmw=d, tn=min(dff, 2048), shard_cols=n_up, out_dtype=BF16, colsum=True)
        dep = send() if send is not None else None
        g_r_in, gr[f"ln_mix_g{layer}"], gr[f"ln_mix_b{layer}"], g_r_in_b = through_ln(
            f"g_hmid{layer}", g_up, wts["up"][layer], dff, shard_tile(dff, n_up), g_r, r_in,
            small["ln_mix_g"][layer:layer + 1], small["ln_mix_b"][layer:layer + 1], dep=dep)
        return g_r_in, g_r_in_b

    g_r3, g_r3b = mlp_bwd(1, g_r4, g_r4b, up1, h3b, r3)
    g_cout = _mm_act("g_cout", g_r3b, wts["out_c"].T, "nat", n_out_cols=d, k_total=d, tn=d, tk=d)[0]
    gr["out_c"] = _mm_wgrad("dw_out_c", c_out, g_r3b, kw=d, n=d, tmw=d, tn=d, out_dtype=BF16)
    g_pc, g_lb_parts, g_ng_parts = _hgrn_bwd("hgrn_bwd", proj_c, lb, wts["ng"], g_cout, hg_states, n_batch, lp, pad, n_hg)
    gr["hgrn_gamma"], gr["ng"] = _lower_bound_bwd("hg_lb_bwd", small["hgrn_gamma"], g_lb_parts, g_ng_parts)
    gr["in_c"] = _mm_wgrad("dw_in_c", h2b, g_pc, kw=d, n=4 * d, tmw=d, tn=min(4 * d, 2048), shard_cols=n_c, out_dtype=BF16)
    sent1 = on_grads(1, {"down1": gr["down1"], "up1": gr["up1"], "out_c": gr["out_c"], "in_c": gr["in_c"], "ng": gr["ng"]})
    g_r2, gr["ln_mlp_g0"], gr["ln_mlp_b0"], g_r2b = through_ln(
        "g_h2", g_pc, wts["in_c"], 4 * d, shard_tile(4 * d, n_c), g_r3, r2, small["ln_mlp_g"][0:1], small["ln_mlp_b"][0:1],
        dep=sent1)

    g_r1, g_r1b = mlp_bwd(0, g_r2, g_r2b, up0, h1b, r1, send=lambda: on_grads(2, {"down0": gr["down0"], "up0": gr["up0"]}))
    g_cat = _mm_act("g_cat", g_r1b, wts["out_ab"].T, "nat", n_out_cols=d, k_total=d, tn=d, tk=d)[0]
    gr["out_ab"] = _mm_wgrad("dw_out_ab", [a_out, b_out], g_r1b, kw=s5w + sbw, n=d, tmw=min(s5w, sbw), tn=d, out_dtype=BF16)
    g_q, g_k, g_v = _attn_bwd("sb_attn_bwd", proj_ab, g_cat, n_batch, lp, pad, s5_cb, s5_cb + sb_cb, s5_cb + 2 * sb_cb,
                              s5_cb, n_pairs)

    g_y_direct, g_gate = _rowwise("s5_glu_bwd", lambda ga, yv, gt: jax.vjp(_glu, yv, gt)[1](ga),
                                  [(g_cat, 0, s5w), (y, 0, s5w), (gate, 0, s5w)], 2, s5w)

    def gelu_bwd_store(outs, acc, gyd, yp, u, dv):
        gyp = jax.vjp(jax.nn.gelu, yp)[1](acc + gyd)[0]
        outs[0][...] = gyp
        outs[1][...] = dv * gyp
        outs[2][...] = jnp.sum(gyp * u, axis=0, keepdims=True)

    rs = _row_spec(tm, s5w)
    g_ypre, g_u_direct, gd_parts = _mm_act(
        "s5_g_y", g_gate, wts["glu"], "natT", n_out_cols=s5w, k_total=s5w, tn=s5w, tk=s5w,
        extras=(g_y_direct, ypre, proj_ab, d_row), extra_specs=(rs, rs, rs, _vec_spec(s5w)), store=gelu_bwd_store,
        out_shape=[jax.ShapeDtypeStruct((rows, s5w), F32)] * 2 + [jax.ShapeDtypeStruct((rows // tm, 1, s5w), F32)],
        out_specs=[rs, rs, pl.BlockSpec((None, 1, s5w), lambda i, j, k: (i, 0, j))])
    gr["glu"], gr["s5_b_glu"] = _mm_wgrad("dw_glu", y, g_gate, kw=s5w, n=s5w, tmw=s5w, tn=s5w, out_dtype=BF16, colsum=True)
    g_sd = _mm_act("s5_g_states", g_ypre, c_blk, "natT", n_out_cols=2 * ns, k_total=s5w, tn=min(2 * ns, 2048), tk=s5w)[0]
    d_cblk = _mm_wgrad("dw_cblk", states, g_ypre, kw=2 * ns, n=s5w, tmw=min(2 * ns, 1024), tn=s5w)
    gs, gl_parts = _s5_scan_bwd("s5_scan_bwd", g_sd, states, lam_row, n_batch, lp, sw)

    def add_store(outs, acc, other):
        outs[0][...] = acc + other

    g_u = _mm_act("s5_g_u", gs, b_blk, "natT", n_out_cols=s5w, k_total=2 * ns, tn=s5w, tk=min(2 * ns, 1024),
                  extras=(g_u_direct,), extra_specs=(rs,), store=add_store)[0]
    d_bblk = _mm_wgrad("dw_bblk", proj_ab, gs, kw=s5w, n=2 * ns, tmw=s5w, tn=min(2 * ns, 2048))
    db_re, db_im = _deinterleave(d_bblk, sw)
    dc_re, dc_im = _deinterleave(d_cblk.T, sw)
    glr, gli = _deinterleave(gl_parts, sw)
    g_lam_re, g_lam_im, g_log_dt, g_b_re_t, g_b_im_t, g_d = _s5_params_bwd(
        "s5_params_bwd", lam_re, lam_im, log_dt, b_re_t, b_im_t,
        glr.reshape(n_batch, groups, n_state), gli.reshape(n_batch, groups, n_state),
        _diag_blocks(db_re, groups), _diag_blocks(db_im, groups), gd_parts)

    cat2 = lambda key: jnp.concatenate([gr[key + "0"], gr[key + "1"]], axis=0)
    small_sent = on_small({
        "s5_lam_re": g_lam_re[None], "s5_lam_im": g_lam_im[None], "s5_log_dt": g_log_dt.reshape(1, groups),
        "s5_b_re": g_b_re_t.transpose(0, 2, 1)[None], "s5_b_im": g_b_im_t.transpose(0, 2, 1)[None],
        "s5_c_re": _diag_blocks(dc_re, groups)[None], "s5_c_im": -_diag_blocks(dc_im, groups)[None],
        "s5_d": g_d.reshape(1, groups, grp), "s5_b_glu": gr["s5_b_glu"], "hgrn_gamma": gr["hgrn_gamma"],
        "ln_mix_g": cat2("ln_mix_g"), "ln_mix_b": cat2("ln_mix_b"), "mlp_b_up": cat2("mlp_b_up"),
        "mlp_b_down": cat2("mlp_b_down"), "ln_mlp_g": cat2("ln_mlp_g"), "ln_mlp_b": cat2("ln_mlp_b"),
    }, loss_tile)

    g_pab = [g_u, g_q, g_k, g_v]
    assert s5w == sbw
    gr["in_ab"] = _mm_wgrad("dw_in_ab", h0b, g_pab, kw=d, n=ab_cols, tmw=d, tn=s5w, shard_cols=n_ab, out_dtype=BF16,
                            dep=small_sent)
    g_h0 = _mm_act("g_h0", g_pab, wts["in_ab"], "stkT", n_out_cols=d, k_total=ab_cols, tn=d, tk=shard_tile(s5w, n_ab),
                   extras=(g_r1,), extra_specs=(_row_spec(tm, d),), store=res_store)[0]
    grad_x = g_h0.reshape(n_batch, lp, d)[:, lead:, :]
    g_meta = _meta_grad("g_meta", g_h0, n_batch, lp, pad, n_meta)
    on_grads(3, {"meta": g_meta, "in_ab": gr["in_ab"], "glu": gr["glu"], "out_ab": gr["out_ab"]})
    return grad_x


SMALL_NAMES = ("s5_lam_re", "s5_lam_im", "s5_log_dt", "s5_b_re", "s5_b_im", "s5_c_re", "s5_c_im", "s5_d", "s5_b_glu",
               "hgrn_gamma", "ln_mix_g", "ln_mix_b", "mlp_b_up", "mlp_b_down", "ln_mlp_g", "ln_mlp_b")
WEIGHT_ORDER = ("meta", "w_in_ab", "s5_lam_re", "s5_lam_im", "s5_log_dt", "s5_b_re", "s5_b_im", "s5_c_re", "s5_c_im",
                "s5_d", "s5_w_glu", "s5_b_glu", "w_out_ab", "w_in_c", "hgrn_gamma", "hgrn_norm_g", "w_out_c", "ln_mix_g",
                "ln_mix_b", "mlp_w_up", "mlp_b_up", "mlp_w_down", "mlp_b_down", "ln_mlp_g", "ln_mlp_b")


def kernel(x, meta, w_in_ab, s5_lam_re, s5_lam_im, s5_log_dt, s5_b_re, s5_b_im, s5_c_re, s5_c_im, s5_d, s5_w_glu, s5_b_glu, w_out_ab, w_in_c, hgrn_gamma, hgrn_norm_g, w_out_c, ln_mix_g, ln_mix_b, mlp_w_up, mlp_b_up, mlp_w_down, mlp_b_down, ln_mlp_g, ln_mlp_b, loss_target, m_meta, m_w_in_ab, m_s5_lam_re, m_s5_lam_im, m_s5_log_dt, m_s5_b_re, m_s5_b_im, m_s5_c_re, m_s5_c_im, m_s5_d, m_s5_w_glu, m_s5_b_glu, m_w_out_ab, m_w_in_c, m_hgrn_gamma, m_hgrn_norm_g, m_w_out_c, m_ln_mix_g, m_ln_mix_b, m_mlp_w_up, m_mlp_b_up, m_mlp_w_down, m_mlp_b_down, m_ln_mlp_g, m_ln_mlp_b, v_meta, v_w_in_ab, v_s5_lam_re, v_s5_lam_im, v_s5_log_dt, v_s5_b_re, v_s5_b_im, v_s5_c_re, v_s5_c_im, v_s5_d, v_s5_w_glu, v_s5_b_glu, v_w_out_ab, v_w_in_c, v_hgrn_gamma, v_hgrn_norm_g, v_w_out_c, v_ln_mix_g, v_ln_mix_b, v_mlp_w_up, v_mlp_b_up, v_mlp_w_down, v_mlp_b_down, v_ln_mlp_g, v_ln_mlp_b):
    args = dict(locals())
    w = {n: args[n] for n in WEIGHT_ORDER}
    mom = {n: args["m_" + n] for n in WEIGHT_ORDER}
    var = {n: args["v_" + n] for n in WEIGHT_ORDER}
    d = x.shape[2]
    n_meta = meta.shape[0]

    cast = lambda a: a.astype(BF16)
    early = _exchange_start("gather_early_start", [w["meta"], cast(w["w_in_ab"][0]), cast(w["s5_w_glu"][0]),
                                                   cast(w["w_out_ab"][0])], False)
    late = _exchange_start("gather_late_start", [w["hgrn_norm_g"], cast(w["w_in_c"][0]), cast(w["w_out_c"][0]),
                                                 cast(w["mlp_w_up"][0]), cast(w["mlp_w_up"][1]),
                                                 cast(w["mlp_w_down"][0]), cast(w["mlp_w_down"][1])], False, dep=early["token"])
    a_meta, a_in_ab, a_glu, a_out_ab = _exchange_wait("gather_early_wait", early, late["token"])
    wts = {"in_ab": a_in_ab, "glu": a_glu.reshape(-1, a_glu.shape[2]), "out_ab": a_out_ab.reshape(-1, d)}
    meta_full = a_meta.transpose(1, 0, 2).reshape(n_meta, d)
    small = {n: w[n] for n in SMALL_NAMES}

    def late_weights(after):
        a_ng, a_in_c, a_out_c, a_up0, a_up1, a_dn0, a_dn1 = _exchange_wait("gather_late_wait", late, after)
        return {"in_c": a_in_c, "ng": a_ng.transpose(1, 0, 2).reshape(1, d), "out_c": a_out_c.reshape(-1, d),
                "up": [a_up0, a_up1], "down": [a_dn0.reshape(-1, d), a_dn1.reshape(-1, d)]}

    n_loc = d // N_DEV
    rows_of = lambda g: g.reshape(N_DEV, -1, g.shape[-1])
    cols_of = lambda g: g.reshape(g.shape[0], N_DEV, n_loc).transpose(1, 0, 2)
    sent = {}

    def on_grads(stage, g):
        if stage == 1:
            order = (("mlp_w_down", 1), ("mlp_w_up", 1), ("w_out_c", 0), ("w_in_c", 0), ("hgrn_norm_g", None))
            parts = [rows_of(g["down1"]), g["up1"], rows_of(g["out_c"]), g["in_c"], cols_of(g["ng"])]
        elif stage == 2:
            order = (("mlp_w_down", 0), ("mlp_w_up", 0))
            parts = [rows_of(g["down0"]), g["up0"]]
        else:
            order = (("w_out_ab", 0), ("s5_w_glu", 0), ("w_in_ab", 0), ("meta", None))
            parts = [rows_of(g["out_ab"]), rows_of(g["glu"]), g["in_ab"], cols_of(g["meta"])]
        sent[stage] = (order, _exchange_start(f"scatter_start{stage}", parts, True))
        return sent[stage][1]["token"]

    def on_small(sg, loss_tile):
        g_pack = _pack_rows([sg[n] for n in SMALL_NAMES] + [loss_tile], PACK_COLS)
        sent["small"] = _exchange_start("gather_small_start", [g_pack], False)
        return sent["small"]["token"]

    grad_x = _local_step(x, loss_target, meta_full, wts, small, late_weights, on_grads, on_small)
    small_sent = sent["small"]
    tile = (SUBLANES, LANES)
    shapes = [w[n].shape for n in SMALL_NAMES] + [tile]
    zeros = jnp.zeros(tile, F32)
    w_pack = _pack_rows([w[n] for n in SMALL_NAMES] + [zeros], PACK_COLS)
    m_pack = _pack_rows([mom[n] for n in SMALL_NAMES] + [zeros], PACK_COLS)
    v_pack = _pack_rows([var[n] for n in SMALL_NAMES] + [zeros], PACK_COLS)

    received, res = {}, {}

    def wait(stage, after):
        order, handle = sent[stage]
        for key, rc in zip(order, _exchange_wait(f"scatter_wait{stage}", handle, after)):
            received[key] = rc

    def update(nm):
        layered = w[nm].ndim == 3
        parts = [received[(nm, l)] for l in range(w[nm].shape[0])] if layered else received[(nm, None)]
        res[nm] = _adamw_summed(f"adamw_{nm}", parts, w[nm], mom[nm], var[nm])
        return res[nm][0]

    wait(1, sent[3][1]["token"])
    done = [update(nm) for nm in ("w_out_c", "w_in_c", "hgrn_norm_g")]
    wait(2, done[0])
    done = [update(nm) for nm in ("mlp_w_up", "mlp_w_down")]
    g_all = _exchange_wait("gather_small_wait", small_sent, done[0])[0]
    packed = _adamw_summed("adamw_small", g_all, w_pack, m_pack, v_pack)
    wait(3, packed[0])
    for nm in ("w_out_ab", "s5_w_glu", "w_in_ab", "meta"):
        update(nm)
    unpacked = [_unpack_rows(p, shapes, PACK_COLS) for p in packed]
    loss = unpacked[0][-1][0, 0]

    def pick(nm, which):
        return unpacked[which][SMALL_NAMES.index(nm)] if nm in SMALL_NAMES else res[nm][which]

    return (loss, grad_x, *[pick(n, 0) for n in WEIGHT_ORDER], *[pick(n, 1) for n in WEIGHT_ORDER],
            *[pick(n, 2) for n in WEIGHT_ORDER], *[pick(n, 3) for n in WEIGHT_ORDER])
```
